```python
import math
import numpy as np
import jax
import jax.numpy as jnp
from jax import lax

D_MODEL = 1024
BATCH = 16
SEQ = 2048
DEPTH = 2

HEAD_DIM = 64
N_HEADS_TOTAL = D_MODEL // HEAD_DIM
N_HEADS_B = N_HEADS_TOTAL // 4
N_HEADS_C = (N_HEADS_TOTAL - N_HEADS_B) // 2
N_HEADS_A = N_HEADS_TOTAL - N_HEADS_B - N_HEADS_C
A_WIDTH = N_HEADS_A * HEAD_DIM
B_WIDTH = N_HEADS_B * HEAD_DIM
C_WIDTH = N_HEADS_C * HEAD_DIM
MIX_WIDTH = A_WIDTH + B_WIDTH + C_WIDTH
COL_SIZES = (3 * A_WIDTH, A_WIDTH, N_HEADS_A, N_HEADS_A,
             B_WIDTH, B_WIDTH, B_WIDTH, B_WIDTH,
             C_WIDTH, C_WIDTH, C_WIDTH, C_WIDTH)
IN_COLS = sum(COL_SIZES)
CONV_WIDTH = 4
GDN_CHUNK = 64
BLOCK = 128
ROPE_DIM = HEAD_DIM // 4
ROPE_THETA = 500000.0
DILATED_PAIRS = ((128, 1), (512, 4), (2048, 16))
RMS_EPS = 1e-6

kernel_name = "hybrid_gdn_stickbreak_dilated"


def rmsnorm(x, w):
    x32 = x.astype(jnp.float32)
    y = x32 * lax.rsqrt(jnp.mean(x32 * x32, axis=-1, keepdims=True) + RMS_EPS)
    return (y * w.astype(jnp.float32)).astype(x.dtype)


def l2norm(x):
    return x * lax.rsqrt(jnp.sum(x * x, axis=-1, keepdims=True) + RMS_EPS)


def causal_depthwise_conv(x, w):
    kw, ch = w.shape
    return lax.conv_general_dilated(
        x, w.astype(x.dtype)[:, None, :], window_strides=(1,),
        padding=((kw - 1, 0),), dimension_numbers=('NWC', 'WIO', 'NWC'),
        feature_group_count=ch)


def partial_rope(x, positions):
    half = ROPE_DIM // 2
    inv_freq = ROPE_THETA ** (-jnp.arange(half, dtype=jnp.float32) / half)
    ang = positions.astype(jnp.float32)[:, None] * inv_freq[None, :]
    cos = jnp.cos(ang)[None, :, None, :]
    sin = jnp.sin(ang)[None, :, None, :]
    x32 = x.astype(jnp.float32)
    x1 = x32[..., :half]
    x2 = x32[..., half:ROPE_DIM]
    out = jnp.concatenate([x1 * cos - x2 * sin, x2 * cos + x1 * sin, x32[..., ROPE_DIM:]], axis=-1)
    return out.astype(x.dtype)


def gated_delta_rule(q, k, v, g, beta):
    b, t, h, dk = q.shape
    dv = v.shape[-1]
    c = GDN_CHUNK
    n = t // c
    f32 = jnp.float32
    q = l2norm(q.astype(f32)) * (dk ** -0.5)
    k = l2norm(k.astype(f32))
    v = v.astype(f32)

    def chunk4(z):
        return z.reshape(b, n, c, h, z.shape[-1]).transpose(0, 1, 3, 2, 4)

    def chunk3(z):
        return z.astype(f32).reshape(b, n, c, h).transpose(0, 1, 3, 2)

    q, k, v = chunk4(q), chunk4(k), chunk4(v)
    g, beta = chunk3(g), chunk3(beta)
    gc = jnp.cumsum(g, axis=-1)
    idx = jnp.arange(c)
    incl = idx[:, None] >= idx[None, :]
    strict = idx[:, None] > idx[None, :]
    decay = jnp.exp(jnp.where(incl, gc[..., :, None] - gc[..., None, :], -jnp.inf))
    kb = k * beta[..., None]
    a = jnp.where(strict, jnp.einsum('bnhid,bnhjd->bnhij', kb, k) * decay, 0.0)
    eye = jnp.eye(c, dtype=f32)
    tmat = lax.linalg.triangular_solve(eye + a, jnp.broadcast_to(eye, a.shape),
                                       left_side=True, lower=True, unit_diagonal=True)
    u = tmat @ (v * beta[..., None])
    w = tmat @ (kb * jnp.exp(gc)[..., None])
    qk = jnp.einsum('bnhid,bnhjd->bnhij', q, k) * decay
    qg = q * jnp.exp(gc)[..., None]
    kg = k * jnp.exp(gc[..., -1:] - gc)[..., None]
    g_last = jnp.exp(gc[..., -1])

    def step(state, inp):
        qg_i, kg_i, u_i, w_i, qk_i, gl_i = inp
        v_new = u_i - w_i @ state
        o_i = qg_i @ state + qk_i @ v_new
        state = state * gl_i[..., None, None] + jnp.einsum('bhcd,bhce->bhde', kg_i, v_new)
        return state, o_i

    xs = tuple(jnp.moveaxis(z, 1, 0) for z in (qg, kg, u, w, qk, g_last))
    s0 = jnp.zeros((b, h, dk, dv), f32)
    _, o = lax.scan(step, s0, xs)
    return o.transpose(1, 0, 3, 2, 4).reshape(b, t, h, dv)


def stick_breaking_attention(q, k, v):
    b, t, h, dh = q.shape
    nb = t // BLOCK
    f32 = jnp.float32
    qb = (q.astype(f32) * (dh ** -0.5)).reshape(b, nb, BLOCK, h, dh).transpose(1, 0, 3, 2, 4)
    kt = k.astype(f32).transpose(0, 2, 1, 3)
    vt = v.astype(f32).transpose(0, 2, 1, 3)
    key_pos = jnp.arange(t)

    def one_block(args):
        q_blk, blk = args
        z = jnp.einsum('bhqd,bhkd->bhqk', q_blk, kt)
        q_pos = blk * BLOCK + jnp.arange(BLOCK)
        earlier = key_pos[None, :] < q_pos[:, None]
        log_beta = jnp.where(earlier, jax.nn.log_sigmoid(z), -jnp.inf)
        log_keep = jnp.where(earlier, jax.nn.log_sigmoid(-z), 0.0)
        log_keep_between = lax.cumsum(log_keep, axis=3, reverse=True) - log_keep
        wts = jnp.exp(log_beta + log_keep_between)
        return jnp.einsum('bhqk,bhkd->bhqd', wts, vt)

    o = lax.map(one_block, (qb, jnp.arange(nb)))
    return o.transpose(1, 0, 3, 2, 4).reshape(b, t, h, dh)


def dilated_window_attention(q, k, v, window, dilation):
    b, t, h, dh = q.shape
    steps = window // dilation
    length = t // dilation
    nb = -(-length // BLOCK)
    lp = nb * BLOCK
    f32 = jnp.float32

    def to_blocks(z):
        z = z.astype(f32).reshape(b, length, dilation, h, dh).transpose(0, 2, 3, 1, 4)
        z = jnp.pad(z, ((0, 0), (0, 0), (0, 0), (0, lp - length), (0, 0)))
        return z.reshape(b, dilation, h, nb, BLOCK, dh)

    def with_prev(z):
        prev = jnp.pad(z, ((0, 0), (0, 0), (0, 0), (1, 0), (0, 0), (0, 0)))[:, :, :, :-1]
        return jnp.concatenate([prev, z], axis=-2)

    qb = to_blocks(q) * (dh ** -0.5)
    kk = with_prev(to_blocks(k))
    vv = with_prev(to_blocks(v))
    s = jnp.einsum('brhnqe,brhnke->brhnqk', qb, kk)
    qi = jnp.arange(BLOCK)[:, None]
    kj = jnp.arange(2 * BLOCK)[None, :]
    rel = qi - kj + BLOCK
    key_idx = jnp.arange(nb)[:, None, None] * BLOCK + kj[None] - BLOCK
    mask = (rel >= 0) & (rel <= steps) & (key_idx >= 0)
    s = jnp.where(mask, s, -jnp.inf)
    m = jnp.max(s, axis=-1, keepdims=True)
    p = jnp.exp(s - m)
    denom = jnp.sum(p, axis=-1, keepdims=True)
    o = jnp.einsum('brhnqk,brhnke->brhnqe', p, vv) / denom
    lse = (m + jnp.log(denom))[..., 0]
    o = o.reshape(b, dilation, h, lp, dh)[:, :, :, :length].transpose(0, 3, 1, 2, 4).reshape(b, t, h, dh)
    lse = lse.reshape(b, dilation, h, lp)[..., :length].transpose(0, 3, 1, 2).reshape(b, t, h)
    return o, lse


def hybrid_layer(x, norm_w, w_in, conv_w, a_log, dt_bias, gdn_norm_w, q_norm_w, k_norm_w, w_out, positions):
    b, t, _ = x.shape
    f32 = jnp.float32
    hdn = rmsnorm(x, norm_w)
    proj = hdn @ w_in
    split_points = np.cumsum(COL_SIZES)[:-1].tolist()
    (qkv_a, z_a, beta_a, alpha_a, q_b, k_b, v_b, z_b,
     q_c, k_c, v_c, z_c) = jnp.split(proj, split_points, axis=-1)

    def heads(z, n_heads):
        return z.reshape(b, t, n_heads, HEAD_DIM)

    qkv_a = jax.nn.silu(causal_depthwise_conv(qkv_a, conv_w))
    q_a, k_a, v_a = jnp.split(qkv_a, 3, axis=-1)
    beta = jax.nn.sigmoid(beta_a.astype(f32))
    g = -jnp.exp(a_log.astype(f32)) * jax.nn.softplus(alpha_a.astype(f32) + dt_bias.astype(f32))
    o_a = gated_delta_rule(heads(q_a, N_HEADS_A), heads(k_a, N_HEADS_A), heads(v_a, N_HEADS_A), g, beta)
    o_a = rmsnorm(o_a, gdn_norm_w).reshape(b, t, A_WIDTH).astype(x.dtype) * jax.nn.silu(z_a)

    o_b = stick_breaking_attention(heads(q_b, N_HEADS_B), heads(k_b, N_HEADS_B), heads(v_b, N_HEADS_B))
    o_b = o_b.reshape(b, t, B_WIDTH).astype(x.dtype) * jax.nn.silu(z_b)

    qc = partial_rope(rmsnorm(heads(q_c, N_HEADS_C), q_norm_w), positions)
    kc = partial_rope(rmsnorm(heads(k_c, N_HEADS_C), k_norm_w), positions)
    vc = heads(v_c, N_HEADS_C)
    outs = []
    lses = []
    for window, dilation in DILATED_PAIRS:
        o_g, lse_g = dilated_window_attention(qc, kc, vc, window, dilation)
        outs.append(o_g)
        lses.append(lse_g)
    mix_w = jax.nn.softmax(jnp.stack(lses, axis=0), axis=0)
    o_c = jnp.einsum('gbth,gbthd->bthd', mix_w, jnp.stack(outs, axis=0))
    o_c = o_c.reshape(b, t, C_WIDTH).astype(x.dtype) * jax.nn.silu(z_c)

    mixed = jnp.concatenate([o_a, o_b, o_c], axis=-1)
    return x + mixed @ w_out


def _fwd_setup_inputs(seed: int = 0) -> dict:
    key = jax.random.key(seed)
    ks = jax.random.split(key, 10)
    f32 = jnp.float32
    x = jax.random.normal(ks[0], (BATCH, SEQ, D_MODEL), f32)
    norm_w = 1.0 + 0.02 * jax.random.normal(ks[1], (DEPTH, D_MODEL), f32)
    w_in = jax.random.normal(ks[2], (DEPTH, D_MODEL, IN_COLS), f32) * (D_MODEL ** -0.5)
    conv_w = jax.random.normal(ks[3], (DEPTH, CONV_WIDTH, 3 * A_WIDTH), f32) * (CONV_WIDTH ** -0.5)
    a_log = jnp.log(jax.random.uniform(ks[4], (DEPTH, N_HEADS_A), f32, minval=1.0, maxval=16.0))
    dt = jnp.exp(jax.random.uniform(ks[5], (DEPTH, N_HEADS_A), f32,
                                    minval=math.log(1e-3), maxval=math.log(1e-1)))
    dt_bias = dt + jnp.log(-jnp.expm1(-dt))
    gdn_norm_w = 1.0 + 0.02 * jax.random.normal(ks[6], (DEPTH, HEAD_DIM), f32)
    q_norm_w = 1.0 + 0.02 * jax.random.normal(ks[7], (DEPTH, HEAD_DIM), f32)
    k_norm_w = 1.0 + 0.02 * jax.random.normal(ks[8], (DEPTH, HEAD_DIM), f32)
    w_out = jax.random.normal(ks[9], (DEPTH, MIX_WIDTH, D_MODEL), f32) * (MIX_WIDTH ** -0.5)
    return {"x": x, "norm_w": norm_w, "w_in": w_in, "conv_w": conv_w, "a_log": a_log,
            "dt_bias": dt_bias, "gdn_norm_w": gdn_norm_w, "q_norm_w": q_norm_w,
            "k_norm_w": k_norm_w, "w_out": w_out}


def _fwd_reference(x, norm_w, w_in, conv_w, a_log, dt_bias, gdn_norm_w, q_norm_w, k_norm_w, w_out):
    positions = jnp.arange(x.shape[1], dtype=jnp.int32)
    for layer in range(DEPTH):
        x = hybrid_layer(x, norm_w[layer], w_in[layer], conv_w[layer], a_log[layer], dt_bias[layer],
                         gdn_norm_w[layer], q_norm_w[layer], k_norm_w[layer], w_out[layer], positions)
    return x


import jax as _jax
import jax.numpy as _jnp

TWIN_FORMAT = 'train_step'
FWD_PARAMS = ['x', 'norm_w', 'w_in', 'conv_w', 'a_log', 'dt_bias', 'gdn_norm_w', 'q_norm_w', 'k_norm_w', 'w_out']
TWIN_WEIGHTS = ['norm_w', 'w_in', 'conv_w', 'a_log', 'dt_bias', 'gdn_norm_w', 'q_norm_w', 'k_norm_w', 'w_out']
TWIN_DIFF_INPUT = 'x'
TWIN_INPUTS = ['x', 'norm_w', 'w_in', 'conv_w', 'a_log', 'dt_bias', 'gdn_norm_w', 'q_norm_w', 'k_norm_w', 'w_out', 'loss_target', 'm_norm_w', 'm_w_in', 'm_conv_w', 'm_a_log', 'm_dt_bias', 'm_gdn_norm_w', 'm_q_norm_w', 'm_k_norm_w', 'm_w_out', 'v_norm_w', 'v_w_in', 'v_conv_w', 'v_a_log', 'v_dt_bias', 'v_gdn_norm_w', 'v_q_norm_w', 'v_k_norm_w', 'v_w_out']
TWIN_OUTPUTS = ['loss', 'grad_x', 'grad_norm_w', 'grad_w_in', 'grad_conv_w', 'grad_a_log', 'grad_dt_bias', 'grad_gdn_norm_w', 'grad_q_norm_w', 'grad_k_norm_w', 'grad_w_out', 'delta_norm_w', 'delta_w_in', 'delta_conv_w', 'delta_a_log', 'delta_dt_bias', 'delta_gdn_norm_w', 'delta_q_norm_w', 'delta_k_norm_w', 'delta_w_out', 'new_m_norm_w', 'new_m_w_in', 'new_m_conv_w', 'new_m_a_log', 'new_m_dt_bias', 'new_m_gdn_norm_w', 'new_m_q_norm_w', 'new_m_k_norm_w', 'new_m_w_out', 'new_v_norm_w', 'new_v_w_in', 'new_v_conv_w', 'new_v_a_log', 'new_v_dt_bias', 'new_v_gdn_norm_w', 'new_v_q_norm_w', 'new_v_k_norm_w', 'new_v_w_out']
TWIN_LEAF_KINDS = {'loss': 'loss', 'grad_x': 'grad_x', 'grad_norm_w': 'grad_w', 'grad_w_in': 'grad_w', 'grad_conv_w': 'grad_w', 'grad_a_log': 'grad_w', 'grad_dt_bias': 'grad_w', 'grad_gdn_norm_w': 'grad_w', 'grad_q_norm_w': 'grad_w', 'grad_k_norm_w': 'grad_w', 'grad_w_out': 'grad_w', 'delta_norm_w': 'delta_w', 'delta_w_in': 'delta_w', 'delta_conv_w': 'delta_w', 'delta_a_log': 'delta_w', 'delta_dt_bias': 'delta_w', 'delta_gdn_norm_w': 'delta_w', 'delta_q_norm_w': 'delta_w', 'delta_k_norm_w': 'delta_w', 'delta_w_out': 'delta_w', 'new_m_norm_w': 'new_m', 'new_m_w_in': 'new_m', 'new_m_conv_w': 'new_m', 'new_m_a_log': 'new_m', 'new_m_dt_bias': 'new_m', 'new_m_gdn_norm_w': 'new_m', 'new_m_q_norm_w': 'new_m', 'new_m_k_norm_w': 'new_m', 'new_m_w_out': 'new_m', 'new_v_norm_w': 'new_v', 'new_v_w_in': 'new_v', 'new_v_conv_w': 'new_v', 'new_v_a_log': 'new_v', 'new_v_dt_bias': 'new_v', 'new_v_gdn_norm_w': 'new_v', 'new_v_q_norm_w': 'new_v', 'new_v_k_norm_w': 'new_v', 'new_v_w_out': 'new_v'}


def _forward(args):
    return _fwd_reference(*[args[k] for k in FWD_PARAMS])


def _output_shape():
    out = _jax.eval_shape(lambda: _forward(_fwd_setup_inputs(0)))
    return out.shape, out.dtype

N_MICROBATCH = 1
ADAM_LR = 0.001
ADAM_B1 = 0.9
ADAM_B2 = 0.999
ADAM_EPS = 1e-08
ADAM_WD = 0.01
ADAM_STEP = 10
PER_EXAMPLE_BATCH_AXIS = {'x': 0, 'loss_target': 0}
SHARED_INPUTS = []
_WEIGHT_DTYPES = {'norm_w': _jnp.float32, 'w_in': _jnp.float32, 'conv_w': _jnp.float32, 'a_log': _jnp.float32, 'dt_bias': _jnp.float32, 'gdn_norm_w': _jnp.float32, 'q_norm_w': _jnp.float32, 'k_norm_w': _jnp.float32, 'w_out': _jnp.float32}
MOMENT_SCALE = {'norm_w': 7.615362e+00, 'w_in': 2.443481e-01, 'conv_w': 4.258303e-01, 'a_log': 1.220045e+01, 'dt_bias': 1.150567e+01, 'gdn_norm_w': 7.099831e+01, 'q_norm_w': 3.870057e-01, 'k_norm_w': 3.865951e-01, 'w_out': 3.551448e-01}


def _to_microbatches(a, axis):
    t = _jnp.moveaxis(a, axis, 0)
    t = t.reshape((N_MICROBATCH, t.shape[0] // N_MICROBATCH) + t.shape[1:])
    return _jnp.moveaxis(t, 1, axis + 1)


def setup_inputs(seed: int = 0) -> dict:
    inp = _fwd_setup_inputs(seed)
    key = _jax.random.fold_in(_jax.random.key(seed), 7919)
    shape, _ = _output_shape()
    out = dict(inp)
    out["loss_target"] = _jax.random.normal(_jax.random.fold_in(key, 0), shape, _jnp.float32)
    for i, name in enumerate(TWIN_WEIGHTS):
        w = inp[name].astype(_jnp.float32)
        if MOMENT_SCALE is None:
            s = _jnp.sqrt(_jnp.mean(_jnp.square(w)) + 1e-30)
        else:
            s = MOMENT_SCALE[name]
        km, kv = _jax.random.split(_jax.random.fold_in(key, i + 1))
        out[name] = w
        out["m_" + name] = s * _jax.random.normal(km, w.shape, _jnp.float32)
        out["v_" + name] = (s * s) * _jax.random.uniform(kv, w.shape, _jnp.float32, 0.5, 1.5)
    if N_MICROBATCH > 1:
        for name, axis in PER_EXAMPLE_BATCH_AXIS.items():
            out[name] = _to_microbatches(out[name], axis)
    return {'x': out['x'], 'norm_w': out['norm_w'], 'w_in': out['w_in'], 'conv_w': out['conv_w'], 'a_log': out['a_log'], 'dt_bias': out['dt_bias'], 'gdn_norm_w': out['gdn_norm_w'], 'q_norm_w': out['q_norm_w'], 'k_norm_w': out['k_norm_w'], 'w_out': out['w_out'], 'loss_target': out['loss_target'], 'm_norm_w': out['m_norm_w'], 'm_w_in': out['m_w_in'], 'm_conv_w': out['m_conv_w'], 'm_a_log': out['m_a_log'], 'm_dt_bias': out['m_dt_bias'], 'm_gdn_norm_w': out['m_gdn_norm_w'], 'm_q_norm_w': out['m_q_norm_w'], 'm_k_norm_w': out['m_k_norm_w'], 'm_w_out': out['m_w_out'], 'v_norm_w': out['v_norm_w'], 'v_w_in': out['v_w_in'], 'v_conv_w': out['v_conv_w'], 'v_a_log': out['v_a_log'], 'v_dt_bias': out['v_dt_bias'], 'v_gdn_norm_w': out['v_gdn_norm_w'], 'v_q_norm_w': out['v_q_norm_w'], 'v_k_norm_w': out['v_k_norm_w'], 'v_w_out': out['v_w_out']}


def _loss(weights, diff, rest, loss_target):
    with _jax.named_scope("forward"):
        args = {**rest, TWIN_DIFF_INPUT: diff, **{k: w.astype(_WEIGHT_DTYPES[k]) for k, w in weights.items()}}
        y = _forward(args)
    with _jax.named_scope("loss_head"):
        err = _jnp.square(y.astype(_jnp.float32) - loss_target)
        return 0.5 * _jnp.sum(_jnp.mean(err, axis=-1)) if err.ndim else 0.5 * err


def _adamw(w, g, m, v):
    m = ADAM_B1 * m + (1.0 - ADAM_B1) * g
    v = ADAM_B2 * v + (1.0 - ADAM_B2) * _jnp.square(g)
    m_hat = m / (1.0 - ADAM_B1 ** ADAM_STEP)
    v_hat = v / (1.0 - ADAM_B2 ** ADAM_STEP)
    delta = -ADAM_LR * (m_hat / (_jnp.sqrt(v_hat) + ADAM_EPS) + ADAM_WD * w)
    return delta, m, v


def reference(x, norm_w, w_in, conv_w, a_log, dt_bias, gdn_norm_w, q_norm_w, k_norm_w, w_out, loss_target, m_norm_w, m_w_in, m_conv_w, m_a_log, m_dt_bias, m_gdn_norm_w, m_q_norm_w, m_k_norm_w, m_w_out, v_norm_w, v_w_in, v_conv_w, v_a_log, v_dt_bias, v_gdn_norm_w, v_q_norm_w, v_k_norm_w, v_w_out):
    given = dict(x=x, norm_w=norm_w, w_in=w_in, conv_w=conv_w, a_log=a_log, dt_bias=dt_bias, gdn_norm_w=gdn_norm_w, q_norm_w=q_norm_w, k_norm_w=k_norm_w, w_out=w_out, loss_target=loss_target, m_norm_w=m_norm_w, m_w_in=m_w_in, m_conv_w=m_conv_w, m_a_log=m_a_log, m_dt_bias=m_dt_bias, m_gdn_norm_w=m_gdn_norm_w, m_q_norm_w=m_q_norm_w, m_k_norm_w=m_k_norm_w, m_w_out=m_w_out, v_norm_w=v_norm_w, v_w_in=v_w_in, v_conv_w=v_conv_w, v_a_log=v_a_log, v_dt_bias=v_dt_bias, v_gdn_norm_w=v_gdn_norm_w, v_q_norm_w=v_q_norm_w, v_k_norm_w=v_k_norm_w, v_w_out=v_w_out)
    weights = {n: given[n] for n in TWIN_WEIGHTS}
    shared = {n: given[n] for n in SHARED_INPUTS}
    per_example = {n: given[n] for n in ['x']}
    grad_fn = _jax.value_and_grad(_loss, argnums=(0, 1))

    def one_microbatch(ex, loss_target):
        ex = dict(ex)
        diff = ex.pop(TWIN_DIFF_INPUT)
        return grad_fn(weights, diff, {**shared, **ex}, loss_target)

    if N_MICROBATCH == 1:
        loss, (grad_w, grad_x) = one_microbatch(per_example, given["loss_target"])
    else:
        def body(carry, xs):
            loss_sum, grad_sum = carry
            l_k, (gw_k, gx_k) = one_microbatch(xs[0], xs[1])
            with _jax.named_scope("update"):
                return (loss_sum + l_k, _jax.tree.map(_jnp.add, grad_sum, gw_k)), gx_k

        init = (_jnp.zeros((), _jnp.float32), _jax.tree.map(_jnp.zeros_like, weights))
        (loss, grad_w), grad_x = _jax.lax.scan(body, init, (per_example, given["loss_target"]))
    with _jax.named_scope("update"):
        delta_w, new_m, new_v = {}, {}, {}
        for n in TWIN_WEIGHTS:
            delta_w[n], new_m[n], new_v[n] = _adamw(weights[n], grad_w[n], given["m_" + n], given["v_" + n])
    return (loss, grad_x, *[grad_w[n] for n in TWIN_WEIGHTS], *[delta_w[n] for n in TWIN_WEIGHTS],
            *[new_m[n] for n in TWIN_WEIGHTS], *[new_v[n] for n in TWIN_WEIGHTS])
```

```python
import functools
import math

import jax
import jax.numpy as jnp
from jax import lax
from jax.experimental import pallas as pl
from jax.experimental.pallas import tpu as pltpu

F32 = jnp.float32
BF16 = jnp.bfloat16

D_MODEL = 1024
SEQ = 2048
DEPTH = 2
HEAD_DIM = 64
N_HEADS_A, N_HEADS_B, N_HEADS_C = 6, 4, 6
A_WIDTH, B_WIDTH, C_WIDTH = N_HEADS_A * HEAD_DIM, N_HEADS_B * HEAD_DIM, N_HEADS_C * HEAD_DIM
CONV_WIDTH = 4
GDN_CHUNK = 64
BLOCK = 128
ROPE_DIM = 16
ROPE_THETA = 500000.0
DILATED_PAIRS = ((128, 1), (512, 4), (2048, 16))
RMS_EPS = 1e-6
NEG = -1e30

NT = (((1,), (1,)), ((), ()))
NN = (((1,), (0,)), ((), ()))
TN = (((0,), (0,)), ((), ()))

VMEM_LIMIT = 48 * 1024 * 1024


def _mm(a, b, dims=NN):
    return lax.dot_general(a.astype(BF16), b.astype(BF16), dims, preferred_element_type=F32)


def _mm32(a, b, dims=NN):
    return lax.dot_general(a, b, dims, precision=lax.Precision.HIGHEST, preferred_element_type=F32)


def _cparams(*sem):
    return pltpu.CompilerParams(dimension_semantics=sem, vmem_limit_bytes=VMEM_LIMIT)


ORIG_COLS = 3 * A_WIDTH + A_WIDTH + 2 * N_HEADS_A + 4 * B_WIDTH + 4 * C_WIDTH
BA_ORIG = 4 * A_WIDTH
BA_PAD = BLOCK - 2 * N_HEADS_A
P_COLS = ORIG_COLS + BA_PAD
COL_BA = 4 * A_WIDTH
COL_B = COL_BA + BLOCK
COL_C = COL_B + 4 * B_WIDTH
TN_COLS = 384
TM_ROWS = 512


def _rms(x, w):
    return x * lax.rsqrt(jnp.mean(x * x, axis=-1, keepdims=True) + RMS_EPS) * w


def inproj_fwd(x, nw, w):
    n, d = x.shape
    p = w.shape[1]

    def body(x_ref, nw_ref, w_ref, proj_ref, hdn_ref):
        @pl.when(pl.program_id(1) == 0)
        def _():
            hdn_ref[...] = _rms(x_ref[...], nw_ref[...]).astype(BF16)

        proj_ref[...] = jnp.dot(hdn_ref[...], w_ref[...], preferred_element_type=F32)

    return pl.pallas_call(
        body, name="inproj_fwd", grid=(n // TM_ROWS, p // TN_COLS),
        in_specs=[pl.BlockSpec((TM_ROWS, d), lambda i, j: (i, 0)), pl.BlockSpec((1, d), lambda i, j: (0, 0)),
                  pl.BlockSpec((d, TN_COLS), lambda i, j: (0, j))],
        out_specs=[pl.BlockSpec((TM_ROWS, TN_COLS), lambda i, j: (i, j)), pl.BlockSpec((TM_ROWS, d), lambda i, j: (i, 0))],
        out_shape=[jax.ShapeDtypeStruct((n, p), F32), jax.ShapeDtypeStruct((n, d), BF16)],
        compiler_params=_cparams("parallel", "arbitrary"),
    )(x, nw, w)


def mat_tn(a, b, tn):
    n, ka = a.shape
    p = b.shape[1]

    def body(a_ref, b_ref, o_ref):
        @pl.when(pl.program_id(1) == 0)
        def _():
            o_ref[...] = jnp.zeros_like(o_ref)

        o_ref[...] += lax.dot_general(a_ref[...], b_ref[...].astype(BF16), TN, preferred_element_type=F32)

    return pl.pallas_call(
        body, name="mat_tn", grid=(p // tn, n // TM_ROWS),
        in_specs=[pl.BlockSpec((TM_ROWS, ka), lambda j, k: (k, 0)), pl.BlockSpec((TM_ROWS, tn), lambda j, k: (k, j))],
        out_specs=pl.BlockSpec((ka, tn), lambda j, k: (0, j)),
        out_shape=jax.ShapeDtypeStruct((ka, p), F32),
        compiler_params=_cparams("parallel", "arbitrary"),
    )(a, b)


def inproj_bwd(dproj, w, x, nw, dy):
    n, d = x.shape
    p = w.shape[1]
    tm = 256

    def body(dp_ref, w_ref, x_ref, nw_ref, dy_ref, dx_ref, dnw_ref):
        dh = lax.dot_general(dp_ref[...].astype(BF16), w_ref[...], NT, preferred_element_type=F32)
        _, vjp = jax.vjp(_rms, x_ref[...], nw_ref[...])
        dx, dnw = vjp(dh)
        dx_ref[...] = dx + dy_ref[...]
        dnw_ref[0] = dnw

    return pl.pallas_call(
        body, name="inproj_bwd", grid=(n // tm,),
        in_specs=[pl.BlockSpec((tm, p), lambda i: (i, 0)), pl.BlockSpec((d, p), lambda i: (0, 0)),
                  pl.BlockSpec((tm, d), lambda i: (i, 0)), pl.BlockSpec((1, d), lambda i: (0, 0)),
                  pl.BlockSpec((tm, d), lambda i: (i, 0))],
        out_specs=[pl.BlockSpec((tm, d), lambda i: (i, 0)), pl.BlockSpec((1, 1, d), lambda i: (i, 0, 0))],
        out_shape=[jax.ShapeDtypeStruct((n, d), F32), jax.ShapeDtypeStruct((n // tm, 1, d), F32)],
        compiler_params=_cparams("parallel"),
    )(dproj, w, x, nw, dy)


CONV_PAD = 8
CONV_ROWS = 256


def _conv_pre(pad_s, cw, c):
    xs = [pad_s[pl.ds(c * CONV_ROWS + CONV_PAD - (CONV_WIDTH - 1) + k, CONV_ROWS), :] for k in range(CONV_WIDTH)]
    pre = xs[0] * cw[0:1, :]
    for k in range(1, CONV_WIDTH):
        pre = pre + xs[k] * cw[k:k + 1, :]
    return pre, xs


def conv_fwd(proj, cw, nseq):
    n = proj.shape[0]
    t = n // nseq
    ch = cw.shape[1]

    def body(x_ref, cw_ref, y_ref, pad_s):
        pad_s[pl.ds(0, CONV_PAD), :] = jnp.zeros((CONV_PAD, TN_COLS), F32)
        pad_s[pl.ds(CONV_PAD, t), :] = x_ref[...]
        cwv = cw_ref[...]
        for c in range(t // CONV_ROWS):
            pre, _ = _conv_pre(pad_s, cwv, c)
            y_ref[pl.ds(c * CONV_ROWS, CONV_ROWS), :] = pre * _sigmoid(pre)

    return pl.pallas_call(
        body, name="conv_fwd", grid=(nseq, ch // TN_COLS),
        in_specs=[pl.BlockSpec((t, TN_COLS), lambda b, j: (b, j)), pl.BlockSpec((CONV_WIDTH, TN_COLS), lambda b, j: (0, j))],
        out_specs=pl.BlockSpec((t, TN_COLS), lambda b, j: (b, j)),
        out_shape=jax.ShapeDtypeStruct((n, ch), F32),
        scratch_shapes=[pltpu.VMEM((t + CONV_PAD, TN_COLS), F32)],
        compiler_params=_cparams("parallel", "parallel"),
    )(proj, cw)


def conv_bwd(proj, cw, dy, nseq):
    n = proj.shape[0]
    t = n // nseq
    ch = cw.shape[1]

    def body(x_ref, cw_ref, dy_ref, dx_ref, dcw_ref, pad_s, dpad_s):
        pad_s[pl.ds(0, CONV_PAD), :] = jnp.zeros((CONV_PAD, TN_COLS), F32)
        pad_s[pl.ds(CONV_PAD, t), :] = x_ref[...]
        dpad_s[pl.ds(t, CONV_PAD), :] = jnp.zeros((CONV_PAD, TN_COLS), F32)
        cwv = cw_ref[...]
        acc = [jnp.zeros((1, TN_COLS), F32)] * CONV_WIDTH
        for c in range(t // CONV_ROWS):
            pre, xs = _conv_pre(pad_s, cwv, c)
            sg = _sigmoid(pre)
            dpre = dy_ref[pl.ds(c * CONV_ROWS, CONV_ROWS), :] * (sg * (1.0 + pre * (1.0 - sg)))
            dpad_s[pl.ds(c * CONV_ROWS, CONV_ROWS), :] = dpre
            acc = [acc[k] + jnp.sum(dpre * xs[k], axis=0, keepdims=True) for k in range(CONV_WIDTH)]
        for k in range(CONV_WIDTH):
            dcw_ref[0, pl.ds(k, 1), :] = acc[k]
        for c in range(t // CONV_ROWS):
            dx = dpad_s[pl.ds(c * CONV_ROWS + CONV_WIDTH - 1, CONV_ROWS), :] * cwv[0:1, :]
            for k in range(1, CONV_WIDTH):
                dx = dx + dpad_s[pl.ds(c * CONV_ROWS + CONV_WIDTH - 1 - k, CONV_ROWS), :] * cwv[k:k + 1, :]
            dx_ref[pl.ds(c * CONV_ROWS, CONV_ROWS), :] = dx

    blk = pl.BlockSpec((t, TN_COLS), lambda b, j: (b, j))
    return pl.pallas_call(
        body, name="conv_bwd", grid=(nseq, ch // TN_COLS),
        in_specs=[blk, pl.BlockSpec((CONV_WIDTH, TN_COLS), lambda b, j: (0, j)), blk],
        out_specs=[blk, pl.BlockSpec((1, CONV_WIDTH, TN_COLS), lambda b, j: (b, 0, j))],
        out_shape=[jax.ShapeDtypeStruct((n, ch), F32), jax.ShapeDtypeStruct((nseq, CONV_WIDTH, ch), F32)],
        scratch_shapes=[pltpu.VMEM((t + CONV_PAD, TN_COLS), F32)] * 2,
        compiler_params=_cparams("parallel", "parallel"),
    )(proj, cw, dy)


def outproj_fwd(x, o, z, w):
    n, d = x.shape

    def body(x_ref, o_ref, z_ref, w_ref, y_ref, m_ref):
        zv = z_ref[...]
        m_ref[...] = (o_ref[...] * (zv * _sigmoid(zv))).astype(BF16)
        y_ref[...] = x_ref[...] + jnp.dot(m_ref[...], w_ref[...], preferred_element_type=F32)

    blk = pl.BlockSpec((TM_ROWS, d), lambda i: (i, 0))
    return pl.pallas_call(
        body, name="outproj_fwd", grid=(n // TM_ROWS,),
        in_specs=[blk, blk, blk, pl.BlockSpec((d, d), lambda i: (0, 0))],
        out_specs=[blk, blk],
        out_shape=[jax.ShapeDtypeStruct((n, d), F32), jax.ShapeDtypeStruct((n, d), BF16)],
        compiler_params=_cparams("parallel"),
    )(x, o, z, w)


def outproj_bwd(dy, o, z, w):
    n, d = dy.shape

    def body(dy_ref, o_ref, z_ref, w_ref, do_ref, dz_ref):
        dm = lax.dot_general(dy_ref[...].astype(BF16), w_ref[...], NT, preferred_element_type=F32)
        zv = z_ref[...]
        sg = _sigmoid(zv)
        do_ref[...] = dm * (zv * sg)
        dz_ref[...] = dm * o_ref[...] * (sg * (1.0 + zv * (1.0 - sg)))

    blk = pl.BlockSpec((TM_ROWS, d), lambda i: (i, 0))
    return pl.pallas_call(
        body, name="outproj_bwd", grid=(n // TM_ROWS,),
        in_specs=[blk, blk, blk, pl.BlockSpec((d, d), lambda i: (0, 0))],
        out_specs=[blk, blk],
        out_shape=[jax.ShapeDtypeStruct((n, d), F32)] * 2,
        compiler_params=_cparams("parallel"),
    )(dy, o, z, w)


def loss_fwd_bwd(y, target):
    n, d = y.shape

    def body(y_ref, t_ref, dy_ref, part_ref):
        e = y_ref[...] - t_ref[...]
        dy_ref[...] = e * (1.0 / d)
        part_ref[...] = jnp.zeros_like(part_ref) + 0.5 * jnp.sum(e * e) * (1.0 / d)

    blk = pl.BlockSpec((TM_ROWS, d), lambda i: (i, 0))
    return pl.pallas_call(
        body, name="loss", grid=(n // TM_ROWS,),
        in_specs=[blk, blk],
        out_specs=[blk, pl.BlockSpec((1, 8, BLOCK), lambda i: (i, 0, 0))],
        out_shape=[jax.ShapeDtypeStruct((n, d), F32), jax.ShapeDtypeStruct((n // TM_ROWS, 8, BLOCK), F32)],
        compiler_params=_cparams("parallel"),
    )(y, target)


ADAM_LR, ADAM_B1, ADAM_B2, ADAM_EPS, ADAM_WD, ADAM_STEP = 0.001, 0.9, 0.999, 1e-08, 0.01, 10


def adamw(w, g, m, v):
    r, c = w.shape
    tr = r if r <= 256 else 256

    def body(w_ref, g_ref, m_ref, v_ref, d_ref, nm_ref, nv_ref):
        gv = g_ref[...]
        nm = ADAM_B1 * m_ref[...] + (1.0 - ADAM_B1) * gv
        nv = ADAM_B2 * v_ref[...] + (1.0 - ADAM_B2) * (gv * gv)
        m_hat = nm / (1.0 - ADAM_B1 ** ADAM_STEP)
        v_hat = nv / (1.0 - ADAM_B2 ** ADAM_STEP)
        d_ref[...] = -ADAM_LR * (m_hat / (jnp.sqrt(v_hat) + ADAM_EPS) + ADAM_WD * w_ref[...])
        nm_ref[...] = nm
        nv_ref[...] = nv

    blk = pl.BlockSpec((tr, c), lambda i: (i, 0))
    return pl.pallas_call(
        body, name="adamw", grid=(r // tr,),
        in_specs=[blk] * 4, out_specs=[blk] * 3,
        out_shape=[jax.ShapeDtypeStruct((r, c), F32)] * 3,
        compiler_params=_cparams("parallel"),
    )(w, g, m, v)


def _sb_tile(q, k, v, carry, qpos, kpos, tri):
    z = _mm(q * (HEAD_DIM ** -0.5), k, NT)
    earlier = kpos < qpos
    sp = jnp.log(1.0 + jnp.exp(-jnp.abs(z)))
    ls_pos = jnp.minimum(z, 0.0) - sp
    ls_neg = jnp.minimum(-z, 0.0) - sp
    log_keep = jnp.where(earlier, ls_neg, 0.0)
    within = _mm32(log_keep, tri)
    wts = jnp.where(earlier, jnp.exp(jnp.where(earlier, ls_pos + within + carry, 0.0)), 0.0)
    return _mm(wts, v), jnp.sum(log_keep, axis=1, keepdims=True)


def _sb_consts():
    qi = lax.broadcasted_iota(jnp.int32, (BLOCK, 1), 0)
    kj = lax.broadcasted_iota(jnp.int32, (1, BLOCK), 1)
    r = lax.broadcasted_iota(jnp.int32, (BLOCK, BLOCK), 0)
    c = lax.broadcasted_iota(jnp.int32, (BLOCK, BLOCK), 1)
    tri = jnp.where(r > c, 1.0, 0.0).astype(F32)
    return qi, kj, tri


def sb_fwd(q, k, v):
    bh, t, dh = q.shape
    nq = t // BLOCK

    def body(q_ref, k_ref, v_ref, o_ref, carry_ref):
        i = pl.program_id(1)
        qi, kj, tri = _sb_consts()
        qv = q_ref[0]
        qpos = i * BLOCK + qi

        def step(it, st):
            o_acc, c = st
            j = i - it
            rows = pl.ds(pl.multiple_of(j * BLOCK, BLOCK), BLOCK)
            carry_ref[0, 0, j] = c
            o, tot = _sb_tile(qv, k_ref[0, rows, :], v_ref[0, rows, :], c, qpos, j * BLOCK + kj, tri)
            return o_acc + o, c + tot

        o_acc, _ = lax.fori_loop(0, i + 1, step, (jnp.zeros((BLOCK, dh), F32), jnp.zeros((BLOCK, 1), F32)))
        o_ref[0] = o_acc

    return pl.pallas_call(
        body, name="sb_fwd", grid=(bh, nq),
        in_specs=[pl.BlockSpec((1, BLOCK, dh), lambda b, i: (b, i, 0)),
                  pl.BlockSpec((1, t, dh), lambda b, i: (b, 0, 0)),
                  pl.BlockSpec((1, t, dh), lambda b, i: (b, 0, 0))],
        out_specs=[pl.BlockSpec((1, BLOCK, dh), lambda b, i: (b, i, 0)),
                   pl.BlockSpec((1, 1, nq, BLOCK, 1), lambda b, i: (b, i, 0, 0, 0))],
        out_shape=[jax.ShapeDtypeStruct((bh, t, dh), F32),
                   jax.ShapeDtypeStruct((bh, nq, nq, BLOCK, 1), F32)],
        compiler_params=_cparams("parallel", "arbitrary"),
    )(q, k, v)


def sb_bwd(q, k, v, carries, do):
    bh, t, dh = q.shape
    nq = t // BLOCK

    def body(q_ref, k_ref, v_ref, carry_ref, do_ref, dq_ref, dk_ref, dv_ref):
        i = pl.program_id(1)
        qi, kj, tri = _sb_consts()
        qv = q_ref[0]
        dov = do_ref[0]
        qpos = i * BLOCK + qi

        @pl.when(i == 0)
        def _():
            dk_ref[...] = jnp.zeros_like(dk_ref)
            dv_ref[...] = jnp.zeros_like(dv_ref)

        def step(j, st):
            dq_acc, dc = st
            rows = pl.ds(pl.multiple_of(j * BLOCK, BLOCK), BLOCK)
            kpos = j * BLOCK + kj
            f = lambda q_, k_, v_, c_: _sb_tile(q_, k_, v_, c_, qpos, kpos, tri)
            _, vjp = jax.vjp(f, qv, k_ref[0, rows, :], v_ref[0, rows, :], carry_ref[0, 0, j])
            dq, dk, dv, dcj = vjp((dov, dc))
            dk_ref[0, rows, :] += dk
            dv_ref[0, rows, :] += dv
            return dq_acc + dq, dc + dcj

        dq_acc, _ = lax.fori_loop(0, i + 1, step, (jnp.zeros((BLOCK, dh), F32), jnp.zeros((BLOCK, 1), F32)))
        dq_ref[0] = dq_acc

    full = pl.BlockSpec((1, t, dh), lambda b, i: (b, 0, 0))
    blk = pl.BlockSpec((1, BLOCK, dh), lambda b, i: (b, i, 0))
    return pl.pallas_call(
        body, name="sb_bwd", grid=(bh, nq),
        in_specs=[blk, full, full,
                  pl.BlockSpec((1, 1, nq, BLOCK, 1), lambda b, i: (b, i, 0, 0, 0)), blk],
        out_specs=[blk, full, full],
        out_shape=[jax.ShapeDtypeStruct((bh, t, dh), F32)] * 3,
        compiler_params=_cparams("parallel", "arbitrary"),
    )(q, k, v, carries, do)


def _sigmoid(x):
    return 0.5 * (jnp.tanh(0.5 * x) + 1.0)


def _softplus(x):
    return jnp.maximum(x, 0.0) + jnp.log(1.0 + jnp.exp(-jnp.abs(x)))


def _gdn_chunk(q, k, v, al_c, al_r, br_c, alog, dtb, nw, s):
    c = GDN_CHUNK
    ri = lax.broadcasted_iota(jnp.int32, (c, c), 0)
    ci = lax.broadcasted_iota(jnp.int32, (c, c), 1)
    incl, strict = ri >= ci, ri > ci
    eye = jnp.where(ri == ci, 1.0, 0.0).astype(F32)
    rate = -jnp.exp(alog)
    g_c = rate * _softplus(al_c + dtb)
    g_r = rate * _softplus(al_r + dtb)
    beta = _sigmoid(br_c)
    gc_c = jnp.sum(jnp.where(incl, g_r, 0.0), axis=1, keepdims=True)
    gc_r = jnp.sum(jnp.where(ri <= ci, g_c, 0.0), axis=0, keepdims=True)
    gl = jnp.sum(g_r, axis=1, keepdims=True)
    decay = jnp.where(incl, jnp.exp(jnp.where(incl, gc_c - gc_r, 0.0)), 0.0)
    qn = q * lax.rsqrt(jnp.sum(q * q, axis=-1, keepdims=True) + RMS_EPS) * (HEAD_DIM ** -0.5)
    kn = k * lax.rsqrt(jnp.sum(k * k, axis=-1, keepdims=True) + RMS_EPS)
    kb = kn * beta
    a = jnp.where(strict, _mm(kb, kn, NT) * decay, 0.0)
    tmat = eye - a
    p = a
    for _ in range(5):
        p = _mm32(p, p)
        tmat = tmat + _mm32(tmat, p)
    u = _mm(tmat, v * beta)
    w = _mm(tmat, kb * jnp.exp(gc_c))
    qk = _mm(qn, kn, NT) * decay
    v_new = u - _mm(w, s)
    o = _mm(qn * jnp.exp(gc_c), s) + _mm(qk, v_new)
    s_new = s * jnp.exp(gl) + _mm(kn * jnp.exp(gl - gc_c), v_new, TN)
    o = o * lax.rsqrt(jnp.mean(o * o, axis=-1, keepdims=True) + RMS_EPS) * nw
    return o, s_new


def _gdn_specs(t):
    nc = t // GDN_CHUNK
    seq = pl.BlockSpec((1, t, HEAD_DIM), lambda b: (b, 0, 0))
    col = pl.BlockSpec((1, t, 1), lambda b: (b, 0, 0))
    row = pl.BlockSpec((1, nc, GDN_CHUNK), lambda b: (b, 0, 0))
    one = pl.BlockSpec((1, 1, 1), lambda b: (b, 0, 0))
    vec = pl.BlockSpec((1, HEAD_DIM), lambda b: (0, 0))
    st = pl.BlockSpec((1, nc, HEAD_DIM, HEAD_DIM), lambda b: (b, 0, 0, 0))
    return seq, col, row, one, vec, st


def gdn_fwd(q, k, v, al_c, al_r, br_c, alog, dtb, nw):
    bh, t, dh = q.shape
    nc = t // GDN_CHUNK

    def body(q_ref, k_ref, v_ref, alc_ref, alr_ref, brc_ref, alog_ref, dtb_ref, nw_ref, o_ref, st_ref):
        def step(c, s):
            rows = pl.ds(pl.multiple_of(c * GDN_CHUNK, GDN_CHUNK), GDN_CHUNK)
            st_ref[0, c] = s
            o, s_new = _gdn_chunk(q_ref[0, rows, :], k_ref[0, rows, :], v_ref[0, rows, :], alc_ref[0, rows, :],
                                  alr_ref[0, pl.ds(c, 1), :], brc_ref[0, rows, :], alog_ref[0], dtb_ref[0], nw_ref[...], s)
            o_ref[0, rows, :] = o
            return s_new

        lax.fori_loop(0, nc, step, jnp.zeros((dh, dh), F32))

    seq, col, row, one, vec, st = _gdn_specs(t)
    return pl.pallas_call(
        body, name="gdn_fwd", grid=(bh,),
        in_specs=[seq, seq, seq, col, row, col, one, one, vec],
        out_specs=[seq, st],
        out_shape=[jax.ShapeDtypeStruct((bh, t, dh), F32), jax.ShapeDtypeStruct((bh, nc, dh, dh), F32)],
        compiler_params=_cparams("parallel"),
    )(q, k, v, al_c, al_r, br_c, alog, dtb, nw)


def gdn_bwd(q, k, v, al_c, al_r, br_c, alog, dtb, nw, states, do):
    bh, t, dh = q.shape
    nc = t // GDN_CHUNK

    def body(q_ref, k_ref, v_ref, alc_ref, alr_ref, brc_ref, alog_ref, dtb_ref, nw_ref, st_ref, do_ref,
             dq_ref, dk_ref, dv_ref, dalc_ref, dalr_ref, dbrc_ref, dalog_ref, ddtb_ref, dnw_ref):
        def step(it, carry):
            ds, dalog, ddtb, dnw = carry
            c = nc - 1 - it
            rows = pl.ds(pl.multiple_of(c * GDN_CHUNK, GDN_CHUNK), GDN_CHUNK)
            _, vjp = jax.vjp(_gdn_chunk, q_ref[0, rows, :], k_ref[0, rows, :], v_ref[0, rows, :], alc_ref[0, rows, :],
                             alr_ref[0, pl.ds(c, 1), :], brc_ref[0, rows, :], alog_ref[0], dtb_ref[0], nw_ref[...],
                             st_ref[0, c])
            dq, dk, dv, dalc, dalr, dbrc, da, dd, dn, ds = vjp((do_ref[0, rows, :], ds))
            dq_ref[0, rows, :] = dq
            dk_ref[0, rows, :] = dk
            dv_ref[0, rows, :] = dv
            dalc_ref[0, rows, :] = dalc
            dalr_ref[0, pl.ds(c, 1), :] = dalr
            dbrc_ref[0, rows, :] = dbrc
            return ds, dalog + da, ddtb + dd, dnw + dn

        z11 = jnp.zeros((1, 1), F32)
        _, dalog, ddtb, dnw = lax.fori_loop(0, nc, step, (jnp.zeros((dh, dh), F32), z11, z11, jnp.zeros((1, dh), F32)))
        dalog_ref[0] = dalog
        ddtb_ref[0] = ddtb
        dnw_ref[0] = dnw

    seq, col, row, one, vec, st = _gdn_specs(t)
    wout = pl.BlockSpec((1, 1, dh), lambda b: (b, 0, 0))
    sd = jax.ShapeDtypeStruct
    return pl.pallas_call(
        body, name="gdn_bwd", grid=(bh,),
        in_specs=[seq, seq, seq, col, row, col, one, one, vec, st, seq],
        out_specs=[seq, seq, seq, col, row, col, one, one, wout],
        out_shape=[sd((bh, t, dh), F32)] * 3 + [sd((bh, t, 1), F32), sd((bh, nc, GDN_CHUNK), F32), sd((bh, t, 1), F32),
                                                 sd((bh, 1, 1), F32), sd((bh, 1, 1), F32), sd((bh, 1, dh), F32)],
        compiler_params=_cparams("parallel"),
    )(q, k, v, al_c, al_r, br_c, alog, dtb, nw, states, do)


DIL_NB = tuple((SEQ // d) // BLOCK for _, d in DILATED_PAIRS)
DIL_D = tuple(d for _, d in DILATED_PAIRS)
DIL_STEPS = tuple(w // d for w, d in DILATED_PAIRS)
ROWS = 256


def _rope_tables(t):
    half = ROPE_DIM // 2
    inv_freq = ROPE_THETA ** (-jnp.arange(half, dtype=F32) / half)
    ang = jnp.arange(t, dtype=F32)[:, None] * inv_freq[None, :]
    ones = jnp.ones((t, HEAD_DIM - ROPE_DIM), F32)
    cs = jnp.concatenate([jnp.cos(ang), jnp.cos(ang), ones], axis=1)
    sn = jnp.concatenate([jnp.sin(ang), jnp.sin(ang), 0.0 * ones], axis=1)
    i = jnp.arange(HEAD_DIM)[:, None]
    j = jnp.arange(HEAD_DIM)[None, :]
    pm = (jnp.where((j < half) & (i == j + half), -1.0, 0.0)
          + jnp.where((j >= half) & (j < ROPE_DIM) & (i == j - half), 1.0, 0.0))
    return cs, sn, pm.astype(F32)


def _dil_prep(x, w, cs, sn, pm):
    y = x * lax.rsqrt(jnp.mean(x * x, axis=-1, keepdims=True) + RMS_EPS) * w
    return y * cs + _mm32(y, pm) * sn


def _dil_tile(qn, kk, vv, mask):
    s = jnp.where(mask, _mm(qn * (HEAD_DIM ** -0.5), kk, NT), NEG)
    m = lax.stop_gradient(jnp.max(s, axis=-1, keepdims=True))
    p = jnp.exp(s - m)
    denom = jnp.sum(p, axis=-1, keepdims=True)
    return _mm(p, vv) / denom, m + jnp.log(denom)


def _dil_mix(o1, o2, o3, l1, l2, l3):
    m = lax.stop_gradient(jnp.maximum(jnp.maximum(l1, l2), l3))
    e1, e2, e3 = jnp.exp(l1 - m), jnp.exp(l2 - m), jnp.exp(l3 - m)
    return (e1 * o1 + e2 * o2 + e3 * o3) / (e1 + e2 + e3)


def _dil_mask(it, g):
    nb = DIL_NB[g]
    n = it % nb
    r = it // nb
    kstart = jnp.maximum(it - 1, 0) * BLOCK
    iq = n * BLOCK + lax.broadcasted_iota(jnp.int32, (BLOCK, 1), 0)
    ik = kstart - r * (nb * BLOCK) + lax.broadcasted_iota(jnp.int32, (1, 2 * BLOCK), 1)
    mask = (ik >= 0) & (iq >= ik) & (iq - ik <= DIL_STEPS[g])
    return mask, pl.ds(pl.multiple_of(it * BLOCK, BLOCK), BLOCK), pl.ds(pl.multiple_of(kstart, BLOCK), 2 * BLOCK)


def _dil_gather(src, dst, d):
    t = src.shape[0]
    ln = t // d
    for r in range(d):
        dst[pl.ds(r * ln, ln), :] = src[pl.ds(r, ln, stride=d), :]


def _dil_scatter(src, dst, d):
    t = src.shape[0]
    ln = t // d
    for r in range(d):
        dst[pl.ds(r, ln, stride=d), :] = src[pl.ds(r * ln, ln), :]


def _dil_forward_parts(q_ref, k_ref, v_ref, qw, kw, cs_ref, sn_ref, pm, qn_s, kn_s, dl_s, od_s, ld_s, on_s, ln_s):
    t = qn_s.shape[0]

    def prep(c, _):
        rows = pl.ds(pl.multiple_of(c * ROWS, ROWS), ROWS)
        qn_s[rows, :] = _dil_prep(q_ref[0, rows, :], qw, cs_ref[rows, :], sn_ref[rows, :], pm)
        kn_s[rows, :] = _dil_prep(k_ref[0, rows, :], kw, cs_ref[rows, :], sn_ref[rows, :], pm)
        return 0

    lax.fori_loop(0, t // ROWS, prep, 0)
    for g in (1, 2):
        _dil_gather(qn_s, dl_s.at[g - 1, 0], DIL_D[g])
        _dil_gather(kn_s, dl_s.at[g - 1, 1], DIL_D[g])
        _dil_gather(v_ref.at[0], dl_s.at[g - 1, 2], DIL_D[g])
    for g in range(3):
        qs = qn_s if g == 0 else dl_s.at[g - 1, 0]
        ks = kn_s if g == 0 else dl_s.at[g - 1, 1]
        vs = v_ref.at[0] if g == 0 else dl_s.at[g - 1, 2]

        def tile(it, _, g=g, qs=qs, ks=ks, vs=vs):
            mask, qrows, krows = _dil_mask(it, g)
            o, lse = _dil_tile(qs[qrows, :], ks[krows, :], vs[krows, :], mask)
            od_s[g, qrows, :] = o
            ld_s[g, qrows, :] = lse
            return 0

        lax.fori_loop(0, t // BLOCK, tile, 0)
    for g in (1, 2):
        _dil_scatter(od_s.at[g], on_s.at[g - 1], DIL_D[g])
        _dil_scatter(ld_s.at[g], ln_s.at[g - 1], DIL_D[g])


def _dil_scratch(t):
    return [pltpu.VMEM((t, HEAD_DIM), F32), pltpu.VMEM((t, HEAD_DIM), F32),
            pltpu.VMEM((2, 3, t, HEAD_DIM), F32),
            pltpu.VMEM((3, t, HEAD_DIM), F32), pltpu.VMEM((3, t, 1), F32),
            pltpu.VMEM((2, t, HEAD_DIM), F32), pltpu.VMEM((2, t, 1), F32)]


def dil_fwd(q, k, v, qw, kw, cs, sn, pm):
    bh, t, dh = q.shape

    def body(q_ref, k_ref, v_ref, qw_ref, kw_ref, cs_ref, sn_ref, pm_ref, o_ref, qn_s, kn_s, dl_s, od_s, ld_s, on_s, ln_s):
        _dil_forward_parts(q_ref, k_ref, v_ref, qw_ref[...], kw_ref[...], cs_ref, sn_ref, pm_ref[...],
                           qn_s, kn_s, dl_s, od_s, ld_s, on_s, ln_s)

        def mix(c, _):
            rows = pl.ds(pl.multiple_of(c * ROWS, ROWS), ROWS)
            o_ref[0, rows, :] = _dil_mix(od_s[0, rows, :], on_s[0, rows, :], on_s[1, rows, :],
                                         ld_s[0, rows, :], ln_s[0, rows, :], ln_s[1, rows, :])
            return 0

        lax.fori_loop(0, t // ROWS, mix, 0)

    seq = pl.BlockSpec((1, t, dh), lambda b: (b, 0, 0))
    vec = pl.BlockSpec((1, dh), lambda b: (0, 0))
    tab = pl.BlockSpec((t, dh), lambda b: (0, 0))
    return pl.pallas_call(
        body, name="dil_fwd", grid=(bh,),
        in_specs=[seq, seq, seq, vec, vec, tab, tab, pl.BlockSpec((dh, dh), lambda b: (0, 0))],
        out_specs=seq,
        out_shape=jax.ShapeDtypeStruct((bh, t, dh), F32),
        scratch_shapes=_dil_scratch(t),
        compiler_params=_cparams("parallel"),
    )(q, k, v, qw, kw, cs, sn, pm)


def dil_bwd(q, k, v, qw, kw, cs, sn, pm, do):
    bh, t, dh = q.shape

    def body(q_ref, k_ref, v_ref, qw_ref, kw_ref, cs_ref, sn_ref, pm_ref, do_ref,
             dq_ref, dk_ref, dv_ref, dqw_ref, dkw_ref,
             qn_s, kn_s, dl_s, od_s, ld_s, on_s, ln_s, tq_s, tk_s, tv_s):
        qw, kw, pm = qw_ref[...], kw_ref[...], pm_ref[...]
        _dil_forward_parts(q_ref, k_ref, v_ref, qw, kw, cs_ref, sn_ref, pm, qn_s, kn_s, dl_s, od_s, ld_s, on_s, ln_s)

        def mix(c, _):
            rows = pl.ds(pl.multiple_of(c * ROWS, ROWS), ROWS)
            _, vjp = jax.vjp(_dil_mix, od_s[0, rows, :], on_s[0, rows, :], on_s[1, rows, :],
                             ld_s[0, rows, :], ln_s[0, rows, :], ln_s[1, rows, :])
            d1, d2, d3, e1, e2, e3 = vjp(do_ref[0, rows, :])
            od_s[0, rows, :] = d1
            on_s[0, rows, :] = d2
            on_s[1, rows, :] = d3
            ld_s[0, rows, :] = e1
            ln_s[0, rows, :] = e2
            ln_s[1, rows, :] = e3
            return 0

        lax.fori_loop(0, t // ROWS, mix, 0)
        for g in (1, 2):
            _dil_gather(on_s.at[g - 1], od_s.at[g], DIL_D[g])
            _dil_gather(ln_s.at[g - 1], ld_s.at[g], DIL_D[g])
        on_s[...] = jnp.zeros_like(on_s)
        dv_ref[...] = jnp.zeros_like(dv_ref)
        for g in range(3):
            qs = qn_s if g == 0 else dl_s.at[g - 1, 0]
            ks = kn_s if g == 0 else dl_s.at[g - 1, 1]
            vs = v_ref.at[0] if g == 0 else dl_s.at[g - 1, 2]
            gq = on_s.at[0] if g == 0 else tq_s
            gk = on_s.at[1] if g == 0 else tk_s
            gv = dv_ref.at[0] if g == 0 else tv_s
            if g > 0:
                tk_s[...] = jnp.zeros_like(tk_s)
                tv_s[...] = jnp.zeros_like(tv_s)

            def tile(it, _, g=g, qs=qs, ks=ks, vs=vs, gq=gq, gk=gk, gv=gv):
                mask, qrows, krows = _dil_mask(it, g)
                _, vjp = jax.vjp(functools.partial(_dil_tile, mask=mask), qs[qrows, :], ks[krows, :], vs[krows, :])
                dq, dkk, dvv = vjp((od_s[g, qrows, :], ld_s[g, qrows, :]))
                gq[qrows, :] = dq
                gk[krows, :] += dkk
                gv[krows, :] += dvv
                return 0

            lax.fori_loop(0, t // BLOCK, tile, 0)
            if g > 0:
                d = DIL_D[g]
                ln = t // d
                for r in range(d):
                    nat, dil = pl.ds(r, ln, stride=d), pl.ds(r * ln, ln)
                    on_s[0, nat, :] += tq_s[dil, :]
                    on_s[1, nat, :] += tk_s[dil, :]
                    dv_ref[0, nat, :] += tv_s[dil, :]

        def prep(c, acc):
            rows = pl.ds(pl.multiple_of(c * ROWS, ROWS), ROWS)
            f = lambda x, w: _dil_prep(x, w, cs_ref[rows, :], sn_ref[rows, :], pm)
            _, vq = jax.vjp(f, q_ref[0, rows, :], qw)
            _, vk = jax.vjp(f, k_ref[0, rows, :], kw)
            dq, dqw = vq(on_s[0, rows, :])
            dk, dkw = vk(on_s[1, rows, :])
            dq_ref[0, rows, :] = dq
            dk_ref[0, rows, :] = dk
            return acc[0] + dqw, acc[1] + dkw

        dqw, dkw = lax.fori_loop(0, t // ROWS, prep, (jnp.zeros((1, dh), F32), jnp.zeros((1, dh), F32)))
        dqw_ref[0] = dqw
        dkw_ref[0] = dkw

    seq = pl.BlockSpec((1, t, dh), lambda b: (b, 0, 0))
    vec = pl.BlockSpec((1, dh), lambda b: (0, 0))
    tab = pl.BlockSpec((t, dh), lambda b: (0, 0))
    wout = pl.BlockSpec((1, 1, dh), lambda b: (b, 0, 0))
    return pl.pallas_call(
        body, name="dil_bwd", grid=(bh,),
        in_specs=[seq, seq, seq, vec, vec, tab, tab, pl.BlockSpec((dh, dh), lambda b: (0, 0)), seq],
        out_specs=[seq, seq, seq, wout, wout],
        out_shape=[jax.ShapeDtypeStruct((bh, t, dh), F32)] * 3 + [jax.ShapeDtypeStruct((bh, 1, dh), F32)] * 2,
        scratch_shapes=_dil_scratch(t) + [pltpu.VMEM((t, dh), F32)] * 3,
        compiler_params=_cparams("parallel"),
    )(q, k, v, qw, kw, cs, sn, pm, do)


N_CHIPS = 4
MESH_IDS = pl.DeviceIdType.MESH
ANY = pl.BlockSpec(memory_space=pl.ANY)


def plane_exchange(src, all_to_all):
    blk_shape = src.shape[1:] if all_to_all else src.shape

    def body(src_ref, out_ref, send_sems, recv_sems, local_sem):
        x, y, c = lax.axis_index("x"), lax.axis_index("y"), lax.axis_index("c")
        me = 2 * x + y
        mine = pltpu.make_async_copy(src_ref.at[me] if all_to_all else src_ref, out_ref.at[me], local_sem)
        mine.start()
        sends = []
        for k in (1, 2, 3):
            px = 1 - x if k & 2 else x
            py = 1 - y if k & 1 else y
            peer = 2 * px + py
            cp = pltpu.make_async_remote_copy(
                src_ref=src_ref.at[peer] if all_to_all else src_ref, dst_ref=out_ref.at[me],
                send_sem=send_sems.at[k - 1], recv_sem=recv_sems.at[k - 1],
                device_id=(px, py, c), device_id_type=MESH_IDS)
            cp.start()
            sends.append((cp, peer, (px, py, c)))
        for k, (cp, peer, dev) in enumerate(sends):
            pltpu.make_async_remote_copy(
                src_ref=out_ref.at[me], dst_ref=out_ref.at[peer],
                send_sem=send_sems.at[k], recv_sem=recv_sems.at[k],
                device_id=dev, device_id_type=MESH_IDS).wait_recv()
        for cp, _, _ in sends:
            cp.wait_send()
        mine.wait()

    return pl.pallas_call(
        body, name="plane_all_to_all" if all_to_all else "plane_all_gather",
        in_specs=[ANY], out_specs=ANY,
        out_shape=jax.ShapeDtypeStruct((N_CHIPS,) + blk_shape, src.dtype),
        scratch_shapes=[pltpu.SemaphoreType.DMA((3,)), pltpu.SemaphoreType.DMA((3,)), pltpu.SemaphoreType.DMA],
    )(src)


def sibling_swap(src):
    def body(src_ref, out_ref, send_sem, recv_sem):
        x, y, c = lax.axis_index("x"), lax.axis_index("y"), lax.axis_index("c")
        cp = pltpu.make_async_remote_copy(src_ref=src_ref, dst_ref=out_ref, send_sem=send_sem, recv_sem=recv_sem,
                                          device_id=(x, y, 1 - c), device_id_type=MESH_IDS)
        cp.start()
        cp.wait()

    return pl.pallas_call(
        body, name="sibling_swap", in_specs=[ANY], out_specs=ANY,
        out_shape=jax.ShapeDtypeStruct(src.shape, src.dtype),
        scratch_shapes=[pltpu.SemaphoreType.DMA, pltpu.SemaphoreType.DMA],
    )(src)


def sum4(a):
    _, r, c = a.shape
    tr = 368

    def body(a_ref, o_ref):
        o_ref[...] = (a_ref[0] + a_ref[1]) + (a_ref[2] + a_ref[3])

    return pl.pallas_call(
        body, name="sum4", grid=(r // tr,),
        in_specs=[pl.BlockSpec((N_CHIPS, tr, c), lambda i: (0, i, 0))],
        out_specs=pl.BlockSpec((tr, c), lambda i: (i, 0)),
        out_shape=jax.ShapeDtypeStruct((r, c), F32),
        compiler_params=_cparams("parallel"),
    )(a)


def add2(a, b):
    r, c = a.shape
    tr = 368

    def body(a_ref, b_ref, o_ref):
        o_ref[...] = a_ref[...] + b_ref[...]

    blk = pl.BlockSpec((tr, c), lambda i: (i, 0))
    return pl.pallas_call(
        body, name="add2", grid=(r // tr,), in_specs=[blk, blk], out_specs=blk,
        out_shape=jax.ShapeDtypeStruct((r, c), F32), compiler_params=_cparams("parallel"),
    )(a, b)


PACK_COLS = 1024
PACK_ROWS = 2576


def _pack(parts):
    flat = jnp.concatenate([p.reshape(-1) for p in parts])
    flat = jnp.pad(flat, (0, PACK_ROWS * PACK_COLS - flat.shape[0]))
    return flat.reshape(PACK_ROWS, PACK_COLS)


def _unpack(buf, shapes):
    flat = buf.reshape(-1)
    out, at = [], 0
    for s in shapes:
        size = math.prod(s)
        out.append(flat[at:at + size].reshape(s))
        at += size
    return out


def _to_heads(a, nseq, nh):
    t = a.shape[0] // nseq
    return a.reshape(nseq, t, nh, HEAD_DIM).transpose(0, 2, 1, 3).reshape(nseq * nh, t, HEAD_DIM)


def _from_heads(a, nseq, nh):
    t = a.shape[1]
    return a.reshape(nseq, nh, t, HEAD_DIM).transpose(0, 2, 1, 3).reshape(nseq * t, nh * HEAD_DIM)


def _layer_fwd(x, p, nseq, tabs):
    n = x.shape[0]
    t = n // nseq
    proj, hdn = inproj_fwd(x, p["norm_w"][None], p["w_in"])
    ya = conv_fwd(proj, p["conv_w"], nseq)
    qa, ka, va = (_to_heads(ya[:, i * A_WIDTH:(i + 1) * A_WIDTH], nseq, N_HEADS_A) for i in range(3))
    ba = proj[:, COL_BA:COL_BA + 2 * N_HEADS_A].reshape(nseq, t, 2 * N_HEADS_A).transpose(0, 2, 1)
    beta_raw, alpha = ba[:, :N_HEADS_A], ba[:, N_HEADS_A:]
    bha = nseq * N_HEADS_A
    al_c, al_r = alpha.reshape(bha, t, 1), alpha.reshape(bha, t // GDN_CHUNK, GDN_CHUNK)
    br_c = beta_raw.reshape(bha, t, 1)
    alog = jnp.tile(p["a_log"], nseq).reshape(bha, 1, 1)
    dtb = jnp.tile(p["dt_bias"], nseq).reshape(bha, 1, 1)
    a_in = (qa, ka, va, al_c, al_r, br_c, alog, dtb, p["gdn_norm_w"][None])
    oa, states = gdn_fwd(*a_in)
    b_in = tuple(_to_heads(proj[:, COL_B + i * B_WIDTH:COL_B + (i + 1) * B_WIDTH], nseq, N_HEADS_B) for i in range(3))
    ob, carries = sb_fwd(*b_in)
    c_in = tuple(_to_heads(proj[:, COL_C + i * C_WIDTH:COL_C + (i + 1) * C_WIDTH], nseq, N_HEADS_C) for i in range(3))
    c_in = c_in + (p["q_norm_w"][None], p["k_norm_w"][None]) + tabs
    oc = dil_fwd(*c_in)
    o = jnp.concatenate([_from_heads(oa, nseq, N_HEADS_A), _from_heads(ob, nseq, N_HEADS_B),
                         _from_heads(oc, nseq, N_HEADS_C)], axis=1)
    z = jnp.concatenate([proj[:, 3 * A_WIDTH:4 * A_WIDTH], proj[:, COL_B + 3 * B_WIDTH:COL_C],
                         proj[:, COL_C + 3 * C_WIDTH:]], axis=1)
    y, mixed = outproj_fwd(x, o, z, p["w_out"])
    return y, dict(x=x, hdn=hdn, proj=proj, a_in=a_in, states=states, b_in=b_in, carries=carries, c_in=c_in,
                   o=o, z=z, mixed=mixed)


def _layer_bwd(dy, p, res, nseq):
    n = dy.shape[0]
    t = n // nseq
    g = {}
    g["w_out"] = mat_tn(res["mixed"], dy, 512)
    do, dz = outproj_bwd(dy, res["o"], res["z"], p["w_out"])
    doa = _to_heads(do[:, :A_WIDTH], nseq, N_HEADS_A)
    dob = _to_heads(do[:, A_WIDTH:A_WIDTH + B_WIDTH], nseq, N_HEADS_B)
    doc = _to_heads(do[:, A_WIDTH + B_WIDTH:], nseq, N_HEADS_C)
    dqc, dkc, dvc, dqw, dkw = dil_bwd(*res["c_in"], doc)
    g["q_norm_w"], g["k_norm_w"] = dqw.sum((0, 1)), dkw.sum((0, 1))
    dqb, dkb, dvb = sb_bwd(*res["b_in"], res["carries"], dob)
    dqa, dka, dva, dalc, dalr, dbrc, dalog, ddtb, dnw = gdn_bwd(*res["a_in"], res["states"], doa)
    g["a_log"] = dalog.reshape(nseq, N_HEADS_A).sum(0)
    g["dt_bias"] = ddtb.reshape(nseq, N_HEADS_A).sum(0)
    g["gdn_norm_w"] = dnw.sum((0, 1))
    dya = jnp.concatenate([_from_heads(d, nseq, N_HEADS_A) for d in (dqa, dka, dva)], axis=1)
    dqkv, dcw = conv_bwd(res["proj"], p["conv_w"], dya, nseq)
    g["conv_w"] = dcw.sum(0)
    dalpha = (dalc + dalr.reshape(dalc.shape)).reshape(nseq, N_HEADS_A, t)
    dba = jnp.concatenate([dbrc.reshape(nseq, N_HEADS_A, t), dalpha], axis=1).transpose(0, 2, 1).reshape(n, 2 * N_HEADS_A)
    dba = jnp.pad(dba, ((0, 0), (0, BA_PAD)))
    dproj = jnp.concatenate(
        [dqkv, dz[:, :A_WIDTH], dba]
        + [_from_heads(d, nseq, N_HEADS_B) for d in (dqb, dkb, dvb)] + [dz[:, A_WIDTH:A_WIDTH + B_WIDTH]]
        + [_from_heads(d, nseq, N_HEADS_C) for d in (dqc, dkc, dvc)] + [dz[:, A_WIDTH + B_WIDTH:]], axis=1)
    g["w_in"] = mat_tn(res["hdn"], dproj, TN_COLS)
    dx, dnw_tiles = inproj_bwd(dproj, p["w_in"], res["x"], p["norm_w"][None], dy)
    g["norm_w"] = dnw_tiles.sum((0, 1))
    return dx, g


SMALL = ("norm_w", "a_log", "dt_bias", "gdn_norm_w", "q_norm_w", "k_norm_w")


def _local_step(x, target, full):
    nseq, t, d = x.shape
    tabs = _rope_tables(t)
    h = x.reshape(nseq * t, d)
    saved = []
    for l in range(DEPTH):
        p = {k: v[l] for k, v in full.items()}
        h, res = _layer_fwd(h, p, nseq, tabs)
        saved.append((p, res))
    dy, parts = loss_fwd_bwd(h, target.reshape(nseq * t, d))
    loss = parts[:, 0, 0].sum()
    grads = [None] * DEPTH
    for l in reversed(range(DEPTH)):
        p, res = saved[l]
        dy, grads[l] = _layer_bwd(dy, p, res, nseq)
    return loss, dy.reshape(nseq, t, d), {k: jnp.stack([g[k] for g in grads]) for k in grads[0]}


def _pad_cols(w):
    cut = BA_ORIG + 2 * N_HEADS_A
    zeros = jnp.zeros(w.shape[:-1] + (BA_PAD,), w.dtype)
    return jnp.concatenate([w[..., :cut], zeros, w[..., cut:]], axis=-1)


def _unpad_cols(w):
    cut = BA_ORIG + 2 * N_HEADS_A
    return jnp.concatenate([w[..., :cut], w[..., cut + BA_PAD:]], axis=-1)


def kernel(x, norm_w, w_in, conv_w, a_log, dt_bias, gdn_norm_w, q_norm_w, k_norm_w, w_out, loss_target, m_norm_w, m_w_in, m_conv_w, m_a_log, m_dt_bias, m_gdn_norm_w, m_q_norm_w, m_k_norm_w, m_w_out, v_norm_w, v_w_in, v_conv_w, v_a_log, v_dt_bias, v_gdn_norm_w, v_q_norm_w, v_k_norm_w, v_w_out):
    weights = dict(norm_w=norm_w, w_in=w_in, conv_w=conv_w, a_log=a_log, dt_bias=dt_bias, gdn_norm_w=gdn_norm_w,
                   q_norm_w=q_norm_w, k_norm_w=k_norm_w, w_out=w_out)
    moms = dict(norm_w=m_norm_w, w_in=m_w_in, conv_w=m_conv_w, a_log=m_a_log, dt_bias=m_dt_bias,
                gdn_norm_w=m_gdn_norm_w, q_norm_w=m_q_norm_w, k_norm_w=m_k_norm_w, w_out=m_w_out)
    vars_ = dict(norm_w=v_norm_w, w_in=v_w_in, conv_w=v_conv_w, a_log=v_a_log, dt_bias=v_dt_bias,
                 gdn_norm_w=v_gdn_norm_w, q_norm_w=v_q_norm_w, k_norm_w=v_k_norm_w, w_out=v_w_out)
    names = list(weights)
    sharded = ("w_in", "w_out", "conv_w")
    shard_shapes = [weights[k].shape for k in sharded]

    got = plane_exchange(_pack([weights[k] for k in sharded]), all_to_all=False)
    per_chip = [_unpack(got[i], shard_shapes) for i in range(N_CHIPS)]
    full = {k: weights[k] for k in SMALL}
    full["w_in"] = _pad_cols(jnp.concatenate([pc[0] for pc in per_chip], axis=2)).astype(BF16)
    full["w_out"] = jnp.concatenate([pc[1] for pc in per_chip], axis=1).astype(BF16)
    full["conv_w"] = jnp.concatenate([pc[2] for pc in per_chip], axis=2)

    loss, grad_x, g = _local_step(x, loss_target, full)

    gw_in = _unpad_cols(g["w_in"])
    cols, rows = w_in.shape[2], w_out.shape[1]
    small = [g[k] for k in SMALL]
    send = jnp.stack([_pack([gw_in[:, :, i * cols:(i + 1) * cols], g["w_out"][:, i * rows:(i + 1) * rows],
                             g["conv_w"][:, :, i * conv_w.shape[2]:(i + 1) * conv_w.shape[2]]] + small)
                      for i in range(N_CHIPS)])
    plane_sum = sum4(plane_exchange(send, all_to_all=True))
    total = add2(plane_sum, sibling_swap(plane_sum))
    reduced = _unpack(total, shard_shapes + [weights[k].shape for k in SMALL])
    grads = dict(zip(sharded + SMALL, reduced))
    loss = lax.psum(loss, ("x", "y", "c"))

    def two_d(a):
        return a.reshape(-1, a.shape[-1])

    delta, new_m, new_v = {}, {}, {}
    for k in names:
        d_, m_, v_ = adamw(two_d(weights[k]), two_d(grads[k]), two_d(moms[k]), two_d(vars_[k]))
        delta[k], new_m[k], new_v[k] = (a.reshape(weights[k].shape) for a in (d_, m_, v_))
    return (loss, grad_x, *[grads[k] for k in names], *[delta[k] for k in names],
            *[new_m[k] for k in names], *[new_v[k] for k in names])
```

```python
import functools
import math

import jax
import jax.numpy as jnp
from jax import lax
from jax.experimental import pallas as pl
from jax.experimental.pallas import tpu as pltpu

F32 = jnp.float32
BF16 = jnp.bfloat16

D_MODEL = 1024
SEQ = 2048
DEPTH = 2
HEAD_DIM = 64
N_HEADS_A, N_HEADS_B, N_HEADS_C = 6, 4, 6
A_WIDTH, B_WIDTH, C_WIDTH = N_HEADS_A * HEAD_DIM, N_HEADS_B * HEAD_DIM, N_HEADS_C * HEAD_DIM
CONV_WIDTH = 4
GDN_CHUNK = 64
BLOCK = 128
ROPE_DIM = 16
ROPE_THETA = 500000.0
DILATED_PAIRS = ((128, 1), (512, 4), (2048, 16))
RMS_EPS = 1e-6
NEG = -1e30

NT = (((1,), (1,)), ((), ()))
NN = (((1,), (0,)), ((), ()))
TN = (((0,), (0,)), ((), ()))

VMEM_LIMIT = 48 * 1024 * 1024


def _mm(a, b, dims=NN):
    return lax.dot_general(a.astype(BF16), b.astype(BF16), dims, preferred_element_type=F32)


def _mm32(a, b, dims=NN):
    return lax.dot_general(a, b, dims, precision=lax.Precision.HIGH, preferred_element_type=F32)


def _cparams(*sem):
    return pltpu.CompilerParams(dimension_semantics=sem, vmem_limit_bytes=VMEM_LIMIT)


ORIG_COLS = 3 * A_WIDTH + A_WIDTH + 2 * N_HEADS_A + 4 * B_WIDTH + 4 * C_WIDTH
BA_ORIG = 4 * A_WIDTH
BA_PAD = BLOCK - 2 * N_HEADS_A
P_COLS = ORIG_COLS + BA_PAD
COL_BA = 4 * A_WIDTH
COL_B = COL_BA + BLOCK
COL_C = COL_B + 4 * B_WIDTH
TN_COLS = 384
TM_ROWS = 512


def _rms(x, w):
    return x * lax.rsqrt(jnp.mean(x * x, axis=-1, keepdims=True) + RMS_EPS) * w


def inproj_fwd(x, nw, w):
    n, d = x.shape
    p = w.shape[1]

    def body(x_ref, nw_ref, w_ref, proj_ref, hdn_ref):
        @pl.when(pl.program_id(1) == 0)
        def _():
            hdn_ref[...] = _rms(x_ref[...], nw_ref[...]).astype(BF16)

        proj_ref[...] = jnp.dot(hdn_ref[...], w_ref[...], preferred_element_type=F32)

    return pl.pallas_call(
        body, name="inproj_fwd", grid=(n // TM_ROWS, p // TN_COLS),
        in_specs=[pl.BlockSpec((TM_ROWS, d), lambda i, j: (i, 0)), pl.BlockSpec((1, d), lambda i, j: (0, 0)),
                  pl.BlockSpec((d, TN_COLS), lambda i, j: (0, j))],
        out_specs=[pl.BlockSpec((TM_ROWS, TN_COLS), lambda i, j: (i, j)), pl.BlockSpec((TM_ROWS, d), lambda i, j: (i, 0))],
        out_shape=[jax.ShapeDtypeStruct((n, p), F32), jax.ShapeDtypeStruct((n, d), BF16)],
        compiler_params=_cparams("parallel", "arbitrary"),
    )(x, nw, w)


def mat_tn(a, b, tn):
    n, ka = a.shape
    p = b.shape[1]

    def body(a_ref, b_ref, o_ref):
        @pl.when(pl.program_id(1) == 0)
        def _():
            o_ref[...] = jnp.zeros_like(o_ref)

        o_ref[...] += lax.dot_general(a_ref[...], b_ref[...].astype(BF16), TN, preferred_element_type=F32)

    return pl.pallas_call(
        body, name="mat_tn", grid=(p // tn, n // TM_ROWS),
        in_specs=[pl.BlockSpec((TM_ROWS, ka), lambda j, k: (k, 0)), pl.BlockSpec((TM_ROWS, tn), lambda j, k: (k, j))],
        out_specs=pl.BlockSpec((ka, tn), lambda j, k: (0, j)),
        out_shape=jax.ShapeDtypeStruct((ka, p), F32),
        compiler_params=_cparams("parallel", "arbitrary"),
    )(a, b)


def inproj_bwd(dproj, w, x, nw, dy):
    n, d = x.shape
    p = w.shape[1]
    tm = 256

    def body(dp_ref, w_ref, x_ref, nw_ref, dy_ref, dx_ref, dnw_ref):
        dh = lax.dot_general(dp_ref[...].astype(BF16), w_ref[...], NT, preferred_element_type=F32)
        _, vjp = jax.vjp(_rms, x_ref[...], nw_ref[...])
        dx, dnw = vjp(dh)
        dx_ref[...] = dx + dy_ref[...]
        dnw_ref[0] = dnw

    return pl.pallas_call(
        body, name="inproj_bwd", grid=(n // tm,),
        in_specs=[pl.BlockSpec((tm, p), lambda i: (i, 0)), pl.BlockSpec((d, p), lambda i: (0, 0)),
                  pl.BlockSpec((tm, d), lambda i: (i, 0)), pl.BlockSpec((1, d), lambda i: (0, 0)),
                  pl.BlockSpec((tm, d), lambda i: (i, 0))],
        out_specs=[pl.BlockSpec((tm, d), lambda i: (i, 0)), pl.BlockSpec((1, 1, d), lambda i: (i, 0, 0))],
        out_shape=[jax.ShapeDtypeStruct((n, d), F32), jax.ShapeDtypeStruct((n // tm, 1, d), F32)],
        compiler_params=_cparams("parallel"),
    )(dproj, w, x, nw, dy)


CONV_PAD = 8
CONV_ROWS = 256


def _conv_pre(pad_s, cw, c):
    xs = [pad_s[pl.ds(c * CONV_ROWS + CONV_PAD - (CONV_WIDTH - 1) + k, CONV_ROWS), :] for k in range(CONV_WIDTH)]
    pre = xs[0] * cw[0:1, :]
    for k in range(1, CONV_WIDTH):
        pre = pre + xs[k] * cw[k:k + 1, :]
    return pre, xs


def conv_fwd(proj, cw, nseq):
    n = proj.shape[0]
    t = n // nseq
    ch = cw.shape[1]

    def body(x_ref, cw_ref, y_ref, pad_s):
        pad_s[pl.ds(0, CONV_PAD), :] = jnp.zeros((CONV_PAD, TN_COLS), F32)
        pad_s[pl.ds(CONV_PAD, t), :] = x_ref[...]
        cwv = cw_ref[...]
        for c in range(t // CONV_ROWS):
            pre, _ = _conv_pre(pad_s, cwv, c)
            y_ref[pl.ds(c * CONV_ROWS, CONV_ROWS), :] = pre * _sigmoid(pre)

    return pl.pallas_call(
        body, name="conv_fwd", grid=(nseq, ch // TN_COLS),
        in_specs=[pl.BlockSpec((t, TN_COLS), lambda b, j: (b, j)), pl.BlockSpec((CONV_WIDTH, TN_COLS), lambda b, j: (0, j))],
        out_specs=pl.BlockSpec((t, TN_COLS), lambda b, j: (b, j)),
        out_shape=jax.ShapeDtypeStruct((n, ch), F32),
        scratch_shapes=[pltpu.VMEM((t + CONV_PAD, TN_COLS), F32)],
        compiler_params=_cparams("parallel", "parallel"),
    )(proj, cw)


def conv_bwd(proj, cw, dy, nseq):
    n = proj.shape[0]
    t = n // nseq
    ch = cw.shape[1]

    def body(x_ref, cw_ref, dy_ref, dx_ref, dcw_ref, pad_s, dpad_s):
        pad_s[pl.ds(0, CONV_PAD), :] = jnp.zeros((CONV_PAD, TN_COLS), F32)
        pad_s[pl.ds(CONV_PAD, t), :] = x_ref[...]
        dpad_s[pl.ds(t, CONV_PAD), :] = jnp.zeros((CONV_PAD, TN_COLS), F32)
        cwv = cw_ref[...]
        acc = [jnp.zeros((1, TN_COLS), F32)] * CONV_WIDTH
        for c in range(t // CONV_ROWS):
            pre, xs = _conv_pre(pad_s, cwv, c)
            sg = _sigmoid(pre)
            dpre = dy_ref[pl.ds(c * CONV_ROWS, CONV_ROWS), :] * (sg * (1.0 + pre * (1.0 - sg)))
            dpad_s[pl.ds(c * CONV_ROWS, CONV_ROWS), :] = dpre
            acc = [acc[k] + jnp.sum(dpre * xs[k], axis=0, keepdims=True) for k in range(CONV_WIDTH)]
        for k in range(CONV_WIDTH):
            dcw_ref[0, pl.ds(k, 1), :] = acc[k]
        for c in range(t // CONV_ROWS):
            dx = dpad_s[pl.ds(c * CONV_ROWS + CONV_WIDTH - 1, CONV_ROWS), :] * cwv[0:1, :]
            for k in range(1, CONV_WIDTH):
                dx = dx + dpad_s[pl.ds(c * CONV_ROWS + CONV_WIDTH - 1 - k, CONV_ROWS), :] * cwv[k:k + 1, :]
            dx_ref[pl.ds(c * CONV_ROWS, CONV_ROWS), :] = dx

    blk = pl.BlockSpec((t, TN_COLS), lambda b, j: (b, j))
    return pl.pallas_call(
        body, name="conv_bwd", grid=(nseq, ch // TN_COLS),
        in_specs=[blk, pl.BlockSpec((CONV_WIDTH, TN_COLS), lambda b, j: (0, j)), blk],
        out_specs=[blk, pl.BlockSpec((1, CONV_WIDTH, TN_COLS), lambda b, j: (b, 0, j))],
        out_shape=[jax.ShapeDtypeStruct((n, ch), F32), jax.ShapeDtypeStruct((nseq, CONV_WIDTH, ch), F32)],
        scratch_shapes=[pltpu.VMEM((t + CONV_PAD, TN_COLS), F32)] * 2,
        compiler_params=_cparams("parallel", "parallel"),
    )(proj, cw, dy)


def outproj_fwd(x, o, z, w):
    n, d = x.shape

    def body(x_ref, o_ref, z_ref, w_ref, y_ref, m_ref):
        zv = z_ref[...]
        m_ref[...] = (o_ref[...] * (zv * _sigmoid(zv))).astype(BF16)
        y_ref[...] = x_ref[...] + jnp.dot(m_ref[...], w_ref[...], preferred_element_type=F32)

    blk = pl.BlockSpec((TM_ROWS, d), lambda i: (i, 0))
    return pl.pallas_call(
        body, name="outproj_fwd", grid=(n // TM_ROWS,),
        in_specs=[blk, blk, blk, pl.BlockSpec((d, d), lambda i: (0, 0))],
        out_specs=[blk, blk],
        out_shape=[jax.ShapeDtypeStruct((n, d), F32), jax.ShapeDtypeStruct((n, d), BF16)],
        compiler_params=_cparams("parallel"),
    )(x, o, z, w)


def outproj_bwd(dy, o, z, w):
    n, d = dy.shape

    def body(dy_ref, o_ref, z_ref, w_ref, do_ref, dz_ref):
        dm = lax.dot_general(dy_ref[...].astype(BF16), w_ref[...], NT, preferred_element_type=F32)
        zv = z_ref[...]
        sg = _sigmoid(zv)
        do_ref[...] = dm * (zv * sg)
        dz_ref[...] = dm * o_ref[...] * (sg * (1.0 + zv * (1.0 - sg)))

    blk = pl.BlockSpec((TM_ROWS, d), lambda i: (i, 0))
    return pl.pallas_call(
        body, name="outproj_bwd", grid=(n // TM_ROWS,),
        in_specs=[blk, blk, blk, pl.BlockSpec((d, d), lambda i: (0, 0))],
        out_specs=[blk, blk],
        out_shape=[jax.ShapeDtypeStruct((n, d), F32)] * 2,
        compiler_params=_cparams("parallel"),
    )(dy, o, z, w)


def loss_fwd_bwd(y, target):
    n, d = y.shape

    def body(y_ref, t_ref, dy_ref, part_ref):
        e = y_ref[...] - t_ref[...]
        dy_ref[...] = e * (1.0 / d)
        part_ref[...] = jnp.zeros_like(part_ref) + 0.5 * jnp.sum(e * e) * (1.0 / d)

    blk = pl.BlockSpec((TM_ROWS, d), lambda i: (i, 0))
    return pl.pallas_call(
        body, name="loss", grid=(n // TM_ROWS,),
        in_specs=[blk, blk],
        out_specs=[blk, pl.BlockSpec((1, 8, BLOCK), lambda i: (i, 0, 0))],
        out_shape=[jax.ShapeDtypeStruct((n, d), F32), jax.ShapeDtypeStruct((n // TM_ROWS, 8, BLOCK), F32)],
        compiler_params=_cparams("parallel"),
    )(y, target)


ADAM_LR, ADAM_B1, ADAM_B2, ADAM_EPS, ADAM_WD, ADAM_STEP = 0.001, 0.9, 0.999, 1e-08, 0.01, 10


def adamw(w, g, m, v):
    r, c = w.shape
    tr = r if r <= 256 else 256

    def body(w_ref, g_ref, m_ref, v_ref, d_ref, nm_ref, nv_ref):
        gv = g_ref[...]
        nm = ADAM_B1 * m_ref[...] + (1.0 - ADAM_B1) * gv
        nv = ADAM_B2 * v_ref[...] + (1.0 - ADAM_B2) * (gv * gv)
        m_hat = nm / (1.0 - ADAM_B1 ** ADAM_STEP)
        v_hat = nv / (1.0 - ADAM_B2 ** ADAM_STEP)
        d_ref[...] = -ADAM_LR * (m_hat / (jnp.sqrt(v_hat) + ADAM_EPS) + ADAM_WD * w_ref[...])
        nm_ref[...] = nm
        nv_ref[...] = nv

    blk = pl.BlockSpec((tr, c), lambda i: (i, 0))
    return pl.pallas_call(
        body, name="adamw", grid=(r // tr,),
        in_specs=[blk] * 4, out_specs=[blk] * 3,
        out_shape=[jax.ShapeDtypeStruct((r, c), F32)] * 3,
        compiler_params=_cparams("parallel"),
    )(w, g, m, v)


def _sb_tile(q, k, v, carry, qpos, kpos, tri):
    z = _mm(q * (HEAD_DIM ** -0.5), k, NT)
    earlier = kpos < qpos
    sp = jnp.log(1.0 + jnp.exp(-jnp.abs(z)))
    ls_pos = jnp.minimum(z, 0.0) - sp
    ls_neg = jnp.minimum(-z, 0.0) - sp
    log_keep = jnp.where(earlier, ls_neg, 0.0)
    within = _mm32(log_keep, tri)
    wts = jnp.where(earlier, jnp.exp(jnp.where(earlier, ls_pos + within + carry, 0.0)), 0.0)
    return _mm(wts, v), jnp.sum(log_keep, axis=1, keepdims=True)


def _sb_consts():
    qi = lax.broadcasted_iota(jnp.int32, (BLOCK, 1), 0)
    kj = lax.broadcasted_iota(jnp.int32, (1, BLOCK), 1)
    r = lax.broadcasted_iota(jnp.int32, (BLOCK, BLOCK), 0)
    c = lax.broadcasted_iota(jnp.int32, (BLOCK, BLOCK), 1)
    tri = jnp.where(r > c, 1.0, 0.0).astype(F32)
    return qi, kj, tri


SB_G = 2


_sb_tiles = jax.vmap(_sb_tile, in_axes=(0, 0, 0, 0, None, None, None))


def sb_fwd(q, k, v):
    bh, t, dh = q.shape
    nq = t // BLOCK

    def body(q_ref, k_ref, v_ref, o_ref, carry_ref):
        i = pl.program_id(1)
        qi, kj, tri = _sb_consts()
        qv = q_ref[...]
        qpos = i * BLOCK + qi

        def step(it, st):
            o_acc, c = st
            j = i - it
            rows = pl.ds(pl.multiple_of(j * BLOCK, BLOCK), BLOCK)
            carry_ref[:, 0, j] = c
            o, tot = _sb_tiles(qv, k_ref[:, rows, :], v_ref[:, rows, :], c, qpos, j * BLOCK + kj, tri)
            return o_acc + o, c + tot

        o_acc, _ = lax.fori_loop(0, i + 1, step, (jnp.zeros((SB_G, BLOCK, dh), F32), jnp.zeros((SB_G, BLOCK, 1), F32)))
        o_ref[...] = o_acc

    return pl.pallas_call(
        body, name="sb_fwd", grid=(bh // SB_G, nq),
        in_specs=[pl.BlockSpec((SB_G, BLOCK, dh), lambda b, i: (b, i, 0)),
                  pl.BlockSpec((SB_G, t, dh), lambda b, i: (b, 0, 0)),
                  pl.BlockSpec((SB_G, t, dh), lambda b, i: (b, 0, 0))],
        out_specs=[pl.BlockSpec((SB_G, BLOCK, dh), lambda b, i: (b, i, 0)),
                   pl.BlockSpec((SB_G, 1, nq, BLOCK, 1), lambda b, i: (b, i, 0, 0, 0))],
        out_shape=[jax.ShapeDtypeStruct((bh, t, dh), F32),
                   jax.ShapeDtypeStruct((bh, nq, nq, BLOCK, 1), F32)],
        compiler_params=_cparams("parallel", "arbitrary"),
    )(q, k, v)


def sb_bwd(q, k, v, carries, do):
    bh, t, dh = q.shape
    nq = t // BLOCK

    def body(q_ref, k_ref, v_ref, carry_ref, do_ref, dq_ref, dk_ref, dv_ref):
        i = pl.program_id(1)
        qi, kj, tri = _sb_consts()
        qv = q_ref[...]
        dov = do_ref[...]
        qpos = i * BLOCK + qi

        @pl.when(i == 0)
        def _():
            dk_ref[...] = jnp.zeros_like(dk_ref)
            dv_ref[...] = jnp.zeros_like(dv_ref)

        def step(j, st):
            dq_acc, dc = st
            rows = pl.ds(pl.multiple_of(j * BLOCK, BLOCK), BLOCK)
            kpos = j * BLOCK + kj
            f = lambda q_, k_, v_, c_: _sb_tiles(q_, k_, v_, c_, qpos, kpos, tri)
            _, vjp = jax.vjp(f, qv, k_ref[:, rows, :], v_ref[:, rows, :], carry_ref[:, 0, j])
            dq, dk, dv, dcj = vjp((dov, dc))
            dk_ref[:, rows, :] += dk
            dv_ref[:, rows, :] += dv
            return dq_acc + dq, dc + dcj

        dq_acc, _ = lax.fori_loop(0, i + 1, step, (jnp.zeros((SB_G, BLOCK, dh), F32), jnp.zeros((SB_G, BLOCK, 1), F32)))
        dq_ref[...] = dq_acc

    full = pl.BlockSpec((SB_G, t, dh), lambda b, i: (b, 0, 0))
    blk = pl.BlockSpec((SB_G, BLOCK, dh), lambda b, i: (b, i, 0))
    return pl.pallas_call(
        body, name="sb_bwd", grid=(bh // SB_G, nq),
        in_specs=[blk, full, full,
                  pl.BlockSpec((SB_G, 1, nq, BLOCK, 1), lambda b, i: (b, i, 0, 0, 0)), blk],
        out_specs=[blk, full, full],
        out_shape=[jax.ShapeDtypeStruct((bh, t, dh), F32)] * 3,
        compiler_params=_cparams("parallel", "arbitrary"),
    )(q, k, v, carries, do)


def _sigmoid(x):
    return 0.5 * (jnp.tanh(0.5 * x) + 1.0)


def _softplus(x):
    return jnp.maximum(x, 0.0) + jnp.log(1.0 + jnp.exp(-jnp.abs(x)))


def _gdn_chunk(q, k, v, al_c, al_r, br_c, alog, dtb, nw, s):
    c = GDN_CHUNK
    ri = lax.broadcasted_iota(jnp.int32, (c, c), 0)
    ci = lax.broadcasted_iota(jnp.int32, (c, c), 1)
    incl, strict = ri >= ci, ri > ci
    eye = jnp.where(ri == ci, 1.0, 0.0).astype(F32)
    rate = -jnp.exp(alog)
    g_c = rate * _softplus(al_c + dtb)
    g_r = rate * _softplus(al_r + dtb)
    beta = _sigmoid(br_c)
    gc_c = jnp.sum(jnp.where(incl, g_r, 0.0), axis=1, keepdims=True)
    gc_r = jnp.sum(jnp.where(ri <= ci, g_c, 0.0), axis=0, keepdims=True)
    gl = jnp.sum(g_r, axis=1, keepdims=True)
    decay = jnp.where(incl, jnp.exp(jnp.where(incl, gc_c - gc_r, 0.0)), 0.0)
    qn = q * lax.rsqrt(jnp.sum(q * q, axis=-1, keepdims=True) + RMS_EPS) * (HEAD_DIM ** -0.5)
    kn = k * lax.rsqrt(jnp.sum(k * k, axis=-1, keepdims=True) + RMS_EPS)
    kb = kn * beta
    a = jnp.where(strict, _mm(kb, kn, NT) * decay, 0.0)
    tmat = eye - a
    p = a
    for _ in range(5):
        p = _mm32(p, p)
        tmat = tmat + _mm32(tmat, p)
    u = _mm(tmat, v * beta)
    w = _mm(tmat, kb * jnp.exp(gc_c))
    qk = _mm(qn, kn, NT) * decay
    v_new = u - _mm(w, s)
    o = _mm(qn * jnp.exp(gc_c), s) + _mm(qk, v_new)
    s_new = s * jnp.exp(gl) + _mm(kn * jnp.exp(gl - gc_c), v_new, TN)
    o = o * lax.rsqrt(jnp.mean(o * o, axis=-1, keepdims=True) + RMS_EPS) * nw
    return o, s_new


GDN_G = 6
GDN_TB = 256


def _gdn_specs(rev, nt):
    tpos = (lambda i: nt - 1 - i) if rev else (lambda i: i)
    ncb = GDN_TB // GDN_CHUNK
    seq = pl.BlockSpec((GDN_G, GDN_TB, HEAD_DIM), lambda b, i: (b, tpos(i), 0))
    col = pl.BlockSpec((GDN_G, GDN_TB, 1), lambda b, i: (b, tpos(i), 0))
    row = pl.BlockSpec((GDN_G, 1, ncb, GDN_CHUNK), lambda b, i: (b, tpos(i), 0, 0))
    one = pl.BlockSpec((GDN_G, 1, 1), lambda b, i: (b, 0, 0))
    vec = pl.BlockSpec((1, HEAD_DIM), lambda b, i: (0, 0))
    st = pl.BlockSpec((GDN_G, ncb, HEAD_DIM, HEAD_DIM), lambda b, i: (b, tpos(i), 0, 0))
    return seq, col, row, one, vec, st


_gdn_chunks = jax.vmap(_gdn_chunk, in_axes=(0, 0, 0, 0, 0, 0, 0, 0, None, 0))


def gdn_fwd(q, k, v, al_c, al_r, br_c, alog, dtb, nw):
    bh, t, dh = q.shape
    nc, nt, ncb = t // GDN_CHUNK, t // GDN_TB, GDN_TB // GDN_CHUNK

    def body(q_ref, k_ref, v_ref, alc_ref, alr_ref, brc_ref, alog_ref, dtb_ref, nw_ref, o_ref, st_ref, s_s):
        @pl.when(pl.program_id(1) == 0)
        def _():
            s_s[...] = jnp.zeros_like(s_s)

        def step(c, s):
            rows = pl.ds(pl.multiple_of(c * GDN_CHUNK, GDN_CHUNK), GDN_CHUNK)
            st_ref[:, c] = s
            o, s_new = _gdn_chunks(q_ref[:, rows, :], k_ref[:, rows, :], v_ref[:, rows, :], alc_ref[:, rows, :],
                                   alr_ref[:, 0, pl.ds(c, 1), :], brc_ref[:, rows, :], alog_ref[...], dtb_ref[...],
                                   nw_ref[...], s)
            o_ref[:, rows, :] = o
            return s_new

        s_s[...] = lax.fori_loop(0, ncb, step, s_s[...])

    seq, col, row, one, vec, st = _gdn_specs(False, nt)
    return pl.pallas_call(
        body, name="gdn_fwd", grid=(bh // GDN_G, nt),
        in_specs=[seq, seq, seq, col, row, col, one, one, vec],
        out_specs=[seq, st],
        out_shape=[jax.ShapeDtypeStruct((bh, t, dh), F32), jax.ShapeDtypeStruct((bh, nc, dh, dh), F32)],
        scratch_shapes=[pltpu.VMEM((GDN_G, dh, dh), F32)],
        compiler_params=_cparams("parallel", "arbitrary"),
    )(q, k, v, al_c, al_r, br_c, alog, dtb, nw)


def gdn_bwd(q, k, v, al_c, al_r, br_c, alog, dtb, nw, states, do):
    bh, t, dh = q.shape
    nc, nt, ncb = t // GDN_CHUNK, t // GDN_TB, GDN_TB // GDN_CHUNK

    def body(q_ref, k_ref, v_ref, alc_ref, alr_ref, brc_ref, alog_ref, dtb_ref, nw_ref, st_ref, do_ref,
             dq_ref, dk_ref, dv_ref, dalc_ref, dalr_ref, dbrc_ref, dalog_ref, ddtb_ref, dnw_ref, ds_s):
        @pl.when(pl.program_id(1) == 0)
        def _():
            ds_s[...] = jnp.zeros_like(ds_s)
            dalog_ref[...] = jnp.zeros_like(dalog_ref)
            ddtb_ref[...] = jnp.zeros_like(ddtb_ref)
            dnw_ref[...] = jnp.zeros_like(dnw_ref)

        def step(it, carry):
            ds, dalog, ddtb, dnw = carry
            c = ncb - 1 - it
            rows = pl.ds(pl.multiple_of(c * GDN_CHUNK, GDN_CHUNK), GDN_CHUNK)
            _, vjp = jax.vjp(_gdn_chunks, q_ref[:, rows, :], k_ref[:, rows, :], v_ref[:, rows, :], alc_ref[:, rows, :],
                             alr_ref[:, 0, pl.ds(c, 1), :], brc_ref[:, rows, :], alog_ref[...], dtb_ref[...], nw_ref[...],
                             st_ref[:, c])
            dq, dk, dv, dalc, dalr, dbrc, da, dd, dn, ds = vjp((do_ref[:, rows, :], ds))
            dq_ref[:, rows, :] = dq
            dk_ref[:, rows, :] = dk
            dv_ref[:, rows, :] = dv
            dalc_ref[:, rows, :] = dalc
            dalr_ref[:, 0, pl.ds(c, 1), :] = dalr
            dbrc_ref[:, rows, :] = dbrc
            return ds, dalog + da, ddtb + dd, dnw + dn

        z11 = jnp.zeros((GDN_G, 1, 1), F32)
        ds, dalog, ddtb, dnw = lax.fori_loop(0, ncb, step, (ds_s[...], z11, z11, jnp.zeros((1, dh), F32)))
        ds_s[...] = ds
        dalog_ref[...] += dalog
        ddtb_ref[...] += ddtb
        dnw_ref[0] += dnw

    seq, col, row, one, vec, st = _gdn_specs(True, nt)
    wout = pl.BlockSpec((1, 1, dh), lambda b, i: (b, 0, 0))
    sd = jax.ShapeDtypeStruct
    return pl.pallas_call(
        body, name="gdn_bwd", grid=(bh // GDN_G, nt),
        in_specs=[seq, seq, seq, col, row, col, one, one, vec, st, seq],
        out_specs=[seq, seq, seq, col, row, col, one, one, wout],
        out_shape=[sd((bh, t, dh), F32)] * 3 + [sd((bh, t, 1), F32), sd((bh, nt, ncb, GDN_CHUNK), F32), sd((bh, t, 1), F32),
                                                 sd((bh, 1, 1), F32), sd((bh, 1, 1), F32), sd((bh // GDN_G, 1, dh), F32)],
        scratch_shapes=[pltpu.VMEM((GDN_G, dh, dh), F32)],
        compiler_params=_cparams("parallel", "arbitrary"),
    )(q, k, v, al_c, al_r, br_c, alog, dtb, nw, states, do)


DIL_NB = tuple((SEQ // d) // BLOCK for _, d in DILATED_PAIRS)
DIL_D = tuple(d for _, d in DILATED_PAIRS)
DIL_STEPS = tuple(w // d for w, d in DILATED_PAIRS)
ROWS = 256


def _rope_tables(t):
    half = ROPE_DIM // 2
    inv_freq = ROPE_THETA ** (-jnp.arange(half, dtype=F32) / half)
    ang = jnp.arange(t, dtype=F32)[:, None] * inv_freq[None, :]
    ones = jnp.ones((t, HEAD_DIM - ROPE_DIM), F32)
    cs = jnp.concatenate([jnp.cos(ang), jnp.cos(ang), ones], axis=1)
    sn = jnp.concatenate([jnp.sin(ang), jnp.sin(ang), 0.0 * ones], axis=1)
    i = jnp.arange(HEAD_DIM)[:, None]
    j = jnp.arange(HEAD_DIM)[None, :]
    pm = (jnp.where((j < half) & (i == j + half), -1.0, 0.0)
          + jnp.where((j >= half) & (j < ROPE_DIM) & (i == j - half), 1.0, 0.0))
    return cs, sn, pm.astype(F32)


def _dil_prep(x, w, cs, sn, pm):
    y = x * lax.rsqrt(jnp.mean(x * x, axis=-1, keepdims=True) + RMS_EPS) * w
    return y * cs + _mm32(y, pm) * sn


def _dil_tile(qn, kk, vv, mask):
    s = jnp.where(mask, _mm(qn * (HEAD_DIM ** -0.5), kk, NT), NEG)
    m = lax.stop_gradient(jnp.max(s, axis=-1, keepdims=True))
    p = jnp.exp(s - m)
    denom = jnp.sum(p, axis=-1, keepdims=True)
    return _mm(p, vv) / denom, m + jnp.log(denom)


def _dil_mix(o1, o2, o3, l1, l2, l3):
    m = lax.stop_gradient(jnp.maximum(jnp.maximum(l1, l2), l3))
    e1, e2, e3 = jnp.exp(l1 - m), jnp.exp(l2 - m), jnp.exp(l3 - m)
    return (e1 * o1 + e2 * o2 + e3 * o3) / (e1 + e2 + e3)


def _dil_mask(it, g):
    nb = DIL_NB[g]
    n = it % nb
    r = it // nb
    kstart = jnp.maximum(it - 1, 0) * BLOCK
    iq = n * BLOCK + lax.broadcasted_iota(jnp.int32, (BLOCK, 1), 0)
    ik = kstart - r * (nb * BLOCK) + lax.broadcasted_iota(jnp.int32, (1, 2 * BLOCK), 1)
    mask = (ik >= 0) & (iq >= ik) & (iq - ik <= DIL_STEPS[g])
    return mask, pl.ds(pl.multiple_of(it * BLOCK, BLOCK), BLOCK), pl.ds(pl.multiple_of(kstart, BLOCK), 2 * BLOCK)


def _dil_gather(src, dst, d):
    t = src.shape[0]
    ln = t // d
    for r in range(d):
        dst[pl.ds(r * ln, ln), :] = src[pl.ds(r, ln, stride=d), :]


def _dil_scatter(src, dst, d):
    t = src.shape[0]
    ln = t // d
    for r in range(d):
        dst[pl.ds(r, ln, stride=d), :] = src[pl.ds(r * ln, ln), :]


def _dil_forward_parts(q_ref, k_ref, v_ref, qw, kw, cs_ref, sn_ref, pm, qn_s, kn_s, dl_s, od_s, ld_s, on_s, ln_s):
    t = qn_s.shape[0]

    def prep(c, _):
        rows = pl.ds(pl.multiple_of(c * ROWS, ROWS), ROWS)
        qn_s[rows, :] = _dil_prep(q_ref[0, rows, :], qw, cs_ref[rows, :], sn_ref[rows, :], pm)
        kn_s[rows, :] = _dil_prep(k_ref[0, rows, :], kw, cs_ref[rows, :], sn_ref[rows, :], pm)
        return 0

    lax.fori_loop(0, t // ROWS, prep, 0)
    for g in (1, 2):
        _dil_gather(qn_s, dl_s.at[g - 1, 0], DIL_D[g])
        _dil_gather(kn_s, dl_s.at[g - 1, 1], DIL_D[g])
        _dil_gather(v_ref.at[0], dl_s.at[g - 1, 2], DIL_D[g])
    for g in range(3):
        qs = qn_s if g == 0 else dl_s.at[g - 1, 0]
        ks = kn_s if g == 0 else dl_s.at[g - 1, 1]
        vs = v_ref.at[0] if g == 0 else dl_s.at[g - 1, 2]

        def tile(it, _, g=g, qs=qs, ks=ks, vs=vs):
            mask, qrows, krows = _dil_mask(it, g)
            o, lse = _dil_tile(qs[qrows, :], ks[krows, :], vs[krows, :], mask)
            od_s[g, qrows, :] = o
            ld_s[g, qrows, :] = lse
            return 0

        lax.fori_loop(0, t // BLOCK, tile, 0)
    for g in (1, 2):
        _dil_scatter(od_s.at[g], on_s.at[g - 1], DIL_D[g])
        _dil_scatter(ld_s.at[g], ln_s.at[g - 1], DIL_D[g])


def _dil_scratch(t):
    return [pltpu.VMEM((t, HEAD_DIM), F32), pltpu.VMEM((t, HEAD_DIM), F32),
            pltpu.VMEM((2, 3, t, HEAD_DIM), F32),
            pltpu.VMEM((3, t, HEAD_DIM), F32), pltpu.VMEM((3, t, 1), F32),
            pltpu.VMEM((2, t, HEAD_DIM), F32), pltpu.VMEM((2, t, 1), F32)]


def dil_fwd(q, k, v, qw, kw, cs, sn, pm):
    bh, t, dh = q.shape

    def body(q_ref, k_ref, v_ref, qw_ref, kw_ref, cs_ref, sn_ref, pm_ref, o_ref, qn_s, kn_s, dl_s, od_s, ld_s, on_s, ln_s):
        _dil_forward_parts(q_ref, k_ref, v_ref, qw_ref[...], kw_ref[...], cs_ref, sn_ref, pm_ref[...],
                           qn_s, kn_s, dl_s, od_s, ld_s, on_s, ln_s)

        def mix(c, _):
            rows = pl.ds(pl.multiple_of(c * ROWS, ROWS), ROWS)
            o_ref[0, rows, :] = _dil_mix(od_s[0, rows, :], on_s[0, rows, :], on_s[1, rows, :],
                                         ld_s[0, rows, :], ln_s[0, rows, :], ln_s[1, rows, :])
            return 0

        lax.fori_loop(0, t // ROWS, mix, 0)

    seq = pl.BlockSpec((1, t, dh), lambda b: (b, 0, 0))
    vec = pl.BlockSpec((1, dh), lambda b: (0, 0))
    tab = pl.BlockSpec((t, dh), lambda b: (0, 0))
    return pl.pallas_call(
        body, name="dil_fwd", grid=(bh,),
        in_specs=[seq, seq, seq, vec, vec, tab, tab, pl.BlockSpec((dh, dh), lambda b: (0, 0))],
        out_specs=seq,
        out_shape=jax.ShapeDtypeStruct((bh, t, dh), F32),
        scratch_shapes=_dil_scratch(t),
        compiler_params=_cparams("parallel"),
    )(q, k, v, qw, kw, cs, sn, pm)


def dil_bwd(q, k, v, qw, kw, cs, sn, pm, do):
    bh, t, dh = q.shape

    def body(q_ref, k_ref, v_ref, qw_ref, kw_ref, cs_ref, sn_ref, pm_ref, do_ref,
             dq_ref, dk_ref, dv_ref, dqw_ref, dkw_ref,
             qn_s, kn_s, dl_s, od_s, ld_s, on_s, ln_s, tq_s, tk_s, tv_s):
        qw, kw, pm = qw_ref[...], kw_ref[...], pm_ref[...]
        _dil_forward_parts(q_ref, k_ref, v_ref, qw, kw, cs_ref, sn_ref, pm, qn_s, kn_s, dl_s, od_s, ld_s, on_s, ln_s)

        def mix(c, _):
            rows = pl.ds(pl.multiple_of(c * ROWS, ROWS), ROWS)
            _, vjp = jax.vjp(_dil_mix, od_s[0, rows, :], on_s[0, rows, :], on_s[1, rows, :],
                             ld_s[0, rows, :], ln_s[0, rows, :], ln_s[1, rows, :])
            d1, d2, d3, e1, e2, e3 = vjp(do_ref[0, rows, :])
            od_s[0, rows, :] = d1
            on_s[0, rows, :] = d2
            on_s[1, rows, :] = d3
            ld_s[0, rows, :] = e1
            ln_s[0, rows, :] = e2
            ln_s[1, rows, :] = e3
            return 0

        lax.fori_loop(0, t // ROWS, mix, 0)
        for g in (1, 2):
            _dil_gather(on_s.at[g - 1], od_s.at[g], DIL_D[g])
            _dil_gather(ln_s.at[g - 1], ld_s.at[g], DIL_D[g])
        on_s[...] = jnp.zeros_like(on_s)
        dv_ref[...] = jnp.zeros_like(dv_ref)
        for g in range(3):
            qs = qn_s if g == 0 else dl_s.at[g - 1, 0]
            ks = kn_s if g == 0 else dl_s.at[g - 1, 1]
            vs = v_ref.at[0] if g == 0 else dl_s.at[g - 1, 2]
            gq = on_s.at[0] if g == 0 else tq_s
            gk = on_s.at[1] if g == 0 else tk_s
            gv = dv_ref.at[0] if g == 0 else tv_s
            if g > 0:
                tk_s[...] = jnp.zeros_like(tk_s)
                tv_s[...] = jnp.zeros_like(tv_s)

            def tile(it, _, g=g, qs=qs, ks=ks, vs=vs, gq=gq, gk=gk, gv=gv):
                mask, qrows, krows = _dil_mask(it, g)
                _, vjp = jax.vjp(functools.partial(_dil_tile, mask=mask), qs[qrows, :], ks[krows, :], vs[krows, :])
                dq, dkk, dvv = vjp((od_s[g, qrows, :], ld_s[g, qrows, :]))
                gq[qrows, :] = dq
                gk[krows, :] += dkk
                gv[krows, :] += dvv
                return 0

            lax.fori_loop(0, t // BLOCK, tile, 0)
            if g > 0:
                d = DIL_D[g]
                ln = t // d
                for r in range(d):
                    nat, dil = pl.ds(r, ln, stride=d), pl.ds(r * ln, ln)
                    on_s[0, nat, :] += tq_s[dil, :]
                    on_s[1, nat, :] += tk_s[dil, :]
                    dv_ref[0, nat, :] += tv_s[dil, :]

        def prep(c, acc):
            rows = pl.ds(pl.multiple_of(c * ROWS, ROWS), ROWS)
            f = lambda x, w: _dil_prep(x, w, cs_ref[rows, :], sn_ref[rows, :], pm)
            _, vq = jax.vjp(f, q_ref[0, rows, :], qw)
            _, vk = jax.vjp(f, k_ref[0, rows, :], kw)
            dq, dqw = vq(on_s[0, rows, :])
            dk, dkw = vk(on_s[1, rows, :])
            dq_ref[0, rows, :] = dq
            dk_ref[0, rows, :] = dk
            return acc[0] + dqw, acc[1] + dkw

        dqw, dkw = lax.fori_loop(0, t // ROWS, prep, (jnp.zeros((1, dh), F32), jnp.zeros((1, dh), F32)))
        dqw_ref[0] = dqw
        dkw_ref[0] = dkw

    seq = pl.BlockSpec((1, t, dh), lambda b: (b, 0, 0))
    vec = pl.BlockSpec((1, dh), lambda b: (0, 0))
    tab = pl.BlockSpec((t, dh), lambda b: (0, 0))
    wout = pl.BlockSpec((1, 1, dh), lambda b: (b, 0, 0))
    return pl.pallas_call(
        body, name="dil_bwd", grid=(bh,),
        in_specs=[seq, seq, seq, vec, vec, tab, tab, pl.BlockSpec((dh, dh), lambda b: (0, 0)), seq],
        out_specs=[seq, seq, seq, wout, wout],
        out_shape=[jax.ShapeDtypeStruct((bh, t, dh), F32)] * 3 + [jax.ShapeDtypeStruct((bh, 1, dh), F32)] * 2,
        scratch_shapes=_dil_scratch(t) + [pltpu.VMEM((t, dh), F32)] * 3,
        compiler_params=_cparams("parallel"),
    )(q, k, v, qw, kw, cs, sn, pm, do)


N_CHIPS = 4
MESH_IDS = pl.DeviceIdType.MESH
ANY = pl.BlockSpec(memory_space=pl.ANY)


def plane_exchange(src, all_to_all):
    blk_shape = src.shape[1:] if all_to_all else src.shape

    def body(src_ref, out_ref, send_sems, recv_sems, local_sem):
        x, y, c = lax.axis_index("x"), lax.axis_index("y"), lax.axis_index("c")
        me = 2 * x + y
        mine = pltpu.make_async_copy(src_ref.at[me] if all_to_all else src_ref, out_ref.at[me], local_sem)
        mine.start()
        sends = []
        for k in (1, 2, 3):
            px = 1 - x if k & 2 else x
            py = 1 - y if k & 1 else y
            peer = 2 * px + py
            cp = pltpu.make_async_remote_copy(
                src_ref=src_ref.at[peer] if all_to_all else src_ref, dst_ref=out_ref.at[me],
                send_sem=send_sems.at[k - 1], recv_sem=recv_sems.at[k - 1],
                device_id=(px, py, c), device_id_type=MESH_IDS)
            cp.start()
            sends.append((cp, peer, (px, py, c)))
        for k, (cp, peer, dev) in enumerate(sends):
            pltpu.make_async_remote_copy(
                src_ref=out_ref.at[me], dst_ref=out_ref.at[peer],
                send_sem=send_sems.at[k], recv_sem=recv_sems.at[k],
                device_id=dev, device_id_type=MESH_IDS).wait_recv()
        for cp, _, _ in sends:
            cp.wait_send()
        mine.wait()

    return pl.pallas_call(
        body, name="plane_all_to_all" if all_to_all else "plane_all_gather",
        in_specs=[ANY], out_specs=ANY,
        out_shape=jax.ShapeDtypeStruct((N_CHIPS,) + blk_shape, src.dtype),
        scratch_shapes=[pltpu.SemaphoreType.DMA((3,)), pltpu.SemaphoreType.DMA((3,)), pltpu.SemaphoreType.DMA],
    )(src)


def sibling_swap(src):
    def body(src_ref, out_ref, send_sem, recv_sem):
        x, y, c = lax.axis_index("x"), lax.axis_index("y"), lax.axis_index("c")
        cp = pltpu.make_async_remote_copy(src_ref=src_ref, dst_ref=out_ref, send_sem=send_sem, recv_sem=recv_sem,
                                          device_id=(x, y, 1 - c), device_id_type=MESH_IDS)
        cp.start()
        cp.wait()

    return pl.pallas_call(
        body, name="sibling_swap", in_specs=[ANY], out_specs=ANY,
        out_shape=jax.ShapeDtypeStruct(src.shape, src.dtype),
        scratch_shapes=[pltpu.SemaphoreType.DMA, pltpu.SemaphoreType.DMA],
    )(src)


def sum4(a):
    _, r, c = a.shape
    tr = 368

    def body(a_ref, o_ref):
        o_ref[...] = (a_ref[0] + a_ref[1]) + (a_ref[2] + a_ref[3])

    return pl.pallas_call(
        body, name="sum4", grid=(r // tr,),
        in_specs=[pl.BlockSpec((N_CHIPS, tr, c), lambda i: (0, i, 0))],
        out_specs=pl.BlockSpec((tr, c), lambda i: (i, 0)),
        out_shape=jax.ShapeDtypeStruct((r, c), F32),
        compiler_params=_cparams("parallel"),
    )(a)


def add2(a, b):
    r, c = a.shape
    tr = 368

    def body(a_ref, b_ref, o_ref):
        o_ref[...] = a_ref[...] + b_ref[...]

    blk = pl.BlockSpec((tr, c), lambda i: (i, 0))
    return pl.pallas_call(
        body, name="add2", grid=(r // tr,), in_specs=[blk, blk], out_specs=blk,
        out_shape=jax.ShapeDtypeStruct((r, c), F32), compiler_params=_cparams("parallel"),
    )(a, b)


PACK_COLS = 1024
PACK_ROWS = 2576


def _pack(parts):
    flat = jnp.concatenate([p.reshape(-1) for p in parts])
    flat = jnp.pad(flat, (0, PACK_ROWS * PACK_COLS - flat.shape[0]))
    return flat.reshape(PACK_ROWS, PACK_COLS)


def _unpack(buf, shapes):
    flat = buf.reshape(-1)
    out, at = [], 0
    for s in shapes:
        size = math.prod(s)
        out.append(flat[at:at + size].reshape(s))
        at += size
    return out


def _to_heads(a, nseq, nh):
    t = a.shape[0] // nseq
    return a.reshape(nseq, t, nh, HEAD_DIM).transpose(0, 2, 1, 3).reshape(nseq * nh, t, HEAD_DIM)


def _from_heads(a, nseq, nh):
    t = a.shape[1]
    return a.reshape(nseq, nh, t, HEAD_DIM).transpose(0, 2, 1, 3).reshape(nseq * t, nh * HEAD_DIM)


def _layer_fwd(x, p, nseq, tabs):
    n = x.shape[0]
    t = n // nseq
    proj, hdn = inproj_fwd(x, p["norm_w"][None], p["w_in"])
    ya = conv_fwd(proj, p["conv_w"], nseq)
    qa, ka, va = (_to_heads(ya[:, i * A_WIDTH:(i + 1) * A_WIDTH], nseq, N_HEADS_A) for i in range(3))
    ba = proj[:, COL_BA:COL_BA + 2 * N_HEADS_A].reshape(nseq, t, 2 * N_HEADS_A).transpose(0, 2, 1)
    beta_raw, alpha = ba[:, :N_HEADS_A], ba[:, N_HEADS_A:]
    bha = nseq * N_HEADS_A
    al_c, al_r = alpha.reshape(bha, t, 1), alpha.reshape(bha, t // GDN_TB, GDN_TB // GDN_CHUNK, GDN_CHUNK)
    br_c = beta_raw.reshape(bha, t, 1)
    alog = jnp.tile(p["a_log"], nseq).reshape(bha, 1, 1)
    dtb = jnp.tile(p["dt_bias"], nseq).reshape(bha, 1, 1)
    a_in = (qa, ka, va, al_c, al_r, br_c, alog, dtb, p["gdn_norm_w"][None])
    oa, states = gdn_fwd(*a_in)
    b_in = tuple(_to_heads(proj[:, COL_B + i * B_WIDTH:COL_B + (i + 1) * B_WIDTH], nseq, N_HEADS_B) for i in range(3))
    ob, carries = sb_fwd(*b_in)
    c_in = tuple(_to_heads(proj[:, COL_C + i * C_WIDTH:COL_C + (i + 1) * C_WIDTH], nseq, N_HEADS_C) for i in range(3))
    c_in = c_in + (p["q_norm_w"][None], p["k_norm_w"][None]) + tabs
    oc = dil_fwd(*c_in)
    o = jnp.concatenate([_from_heads(oa, nseq, N_HEADS_A), _from_heads(ob, nseq, N_HEADS_B),
                         _from_heads(oc, nseq, N_HEADS_C)], axis=1)
    z = jnp.concatenate([proj[:, 3 * A_WIDTH:4 * A_WIDTH], proj[:, COL_B + 3 * B_WIDTH:COL_C],
                         proj[:, COL_C + 3 * C_WIDTH:]], axis=1)
    y, mixed = outproj_fwd(x, o, z, p["w_out"])
    return y, dict(x=x, hdn=hdn, proj=proj, a_in=a_in, states=states, b_in=b_in, carries=carries, c_in=c_in,
                   o=o, z=z, mixed=mixed)


def _layer_bwd(dy, p, res, nseq):
    n = dy.shape[0]
    t = n // nseq
    g = {}
    g["w_out"] = mat_tn(res["mixed"], dy, 512)
    do, dz = outproj_bwd(dy, res["o"], res["z"], p["w_out"])
    doa = _to_heads(do[:, :A_WIDTH], nseq, N_HEADS_A)
    dob = _to_heads(do[:, A_WIDTH:A_WIDTH + B_WIDTH], nseq, N_HEADS_B)
    doc = _to_heads(do[:, A_WIDTH + B_WIDTH:], nseq, N_HEADS_C)
    dqc, dkc, dvc, dqw, dkw = dil_bwd(*res["c_in"], doc)
    g["q_norm_w"], g["k_norm_w"] = dqw.sum((0, 1)), dkw.sum((0, 1))
    dqb, dkb, dvb = sb_bwd(*res["b_in"], res["carries"], dob)
    dqa, dka, dva, dalc, dalr, dbrc, dalog, ddtb, dnw = gdn_bwd(*res["a_in"], res["states"], doa)
    g["a_log"] = dalog.reshape(nseq, N_HEADS_A).sum(0)
    g["dt_bias"] = ddtb.reshape(nseq, N_HEADS_A).sum(0)
    g["gdn_norm_w"] = dnw.sum((0, 1))
    dya = jnp.concatenate([_from_heads(d, nseq, N_HEADS_A) for d in (dqa, dka, dva)], axis=1)
    dqkv, dcw = conv_bwd(res["proj"], p["conv_w"], dya, nseq)
    g["conv_w"] = dcw.sum(0)
    dalpha = (dalc + dalr.reshape(dalc.shape)).reshape(nseq, N_HEADS_A, t)
    dba = jnp.concatenate([dbrc.reshape(nseq, N_HEADS_A, t), dalpha], axis=1).transpose(0, 2, 1).reshape(n, 2 * N_HEADS_A)
    dba = jnp.pad(dba, ((0, 0), (0, BA_PAD)))
    dproj = jnp.concatenate(
        [dqkv, dz[:, :A_WIDTH], dba]
        + [_from_heads(d, nseq, N_HEADS_B) for d in (dqb, dkb, dvb)] + [dz[:, A_WIDTH:A_WIDTH + B_WIDTH]]
        + [_from_heads(d, nseq, N_HEADS_C) for d in (dqc, dkc, dvc)] + [dz[:, A_WIDTH + B_WIDTH:]], axis=1)
    g["w_in"] = mat_tn(res["hdn"], dproj, TN_COLS)
    dx, dnw_tiles = inproj_bwd(dproj, p["w_in"], res["x"], p["norm_w"][None], dy)
    g["norm_w"] = dnw_tiles.sum((0, 1))
    return dx, g


SMALL = ("norm_w", "a_log", "dt_bias", "gdn_norm_w", "q_norm_w", "k_norm_w")


def _local_step(x, target, full):
    nseq, t, d = x.shape
    tabs = _rope_tables(t)
    h = x.reshape(nseq * t, d)
    saved = []
    for l in range(DEPTH):
        p = {k: v[l] for k, v in full.items()}
        h, res = _layer_fwd(h, p, nseq, tabs)
        saved.append((p, res))
    dy, parts = loss_fwd_bwd(h, target.reshape(nseq * t, d))
    loss = parts[:, 0, 0].sum()
    grads = [None] * DEPTH
    for l in reversed(range(DEPTH)):
        p, res = saved[l]
        dy, grads[l] = _layer_bwd(dy, p, res, nseq)
    return loss, dy.reshape(nseq, t, d), {k: jnp.stack([g[k] for g in grads]) for k in grads[0]}


def _pad_cols(w):
    cut = BA_ORIG + 2 * N_HEADS_A
    zeros = jnp.zeros(w.shape[:-1] + (BA_PAD,), w.dtype)
    return jnp.concatenate([w[..., :cut], zeros, w[..., cut:]], axis=-1)


def _unpad_cols(w):
    cut = BA_ORIG + 2 * N_HEADS_A
    return jnp.concatenate([w[..., :cut], w[..., cut + BA_PAD:]], axis=-1)


def kernel(x, norm_w, w_in, conv_w, a_log, dt_bias, gdn_norm_w, q_norm_w, k_norm_w, w_out, loss_target, m_norm_w, m_w_in, m_conv_w, m_a_log, m_dt_bias, m_gdn_norm_w, m_q_norm_w, m_k_norm_w, m_w_out, v_norm_w, v_w_in, v_conv_w, v_a_log, v_dt_bias, v_gdn_norm_w, v_q_norm_w, v_k_norm_w, v_w_out):
    weights = dict(norm_w=norm_w, w_in=w_in, conv_w=conv_w, a_log=a_log, dt_bias=dt_bias, gdn_norm_w=gdn_norm_w,
                   q_norm_w=q_norm_w, k_norm_w=k_norm_w, w_out=w_out)
    moms = dict(norm_w=m_norm_w, w_in=m_w_in, conv_w=m_conv_w, a_log=m_a_log, dt_bias=m_dt_bias,
                gdn_norm_w=m_gdn_norm_w, q_norm_w=m_q_norm_w, k_norm_w=m_k_norm_w, w_out=m_w_out)
    vars_ = dict(norm_w=v_norm_w, w_in=v_w_in, conv_w=v_conv_w, a_log=v_a_log, dt_bias=v_dt_bias,
                 gdn_norm_w=v_gdn_norm_w, q_norm_w=v_q_norm_w, k_norm_w=v_k_norm_w, w_out=v_w_out)
    names = list(weights)
    sharded = ("w_in", "w_out", "conv_w")
    shard_shapes = [weights[k].shape for k in sharded]

    got = plane_exchange(_pack([weights[k] for k in sharded]), all_to_all=False)
    per_chip = [_unpack(got[i], shard_shapes) for i in range(N_CHIPS)]
    full = {k: weights[k] for k in SMALL}
    full["w_in"] = _pad_cols(jnp.concatenate([pc[0] for pc in per_chip], axis=2)).astype(BF16)
    full["w_out"] = jnp.concatenate([pc[1] for pc in per_chip], axis=1).astype(BF16)
    full["conv_w"] = jnp.concatenate([pc[2] for pc in per_chip], axis=2)

    loss, grad_x, g = _local_step(x, loss_target, full)

    gw_in = _unpad_cols(g["w_in"])
    cols, rows = w_in.shape[2], w_out.shape[1]
    small = [g[k] for k in SMALL]
    send = jnp.stack([_pack([gw_in[:, :, i * cols:(i + 1) * cols], g["w_out"][:, i * rows:(i + 1) * rows],
                             g["conv_w"][:, :, i * conv_w.shape[2]:(i + 1) * conv_w.shape[2]]] + small)
                      for i in range(N_CHIPS)])
    plane_sum = sum4(plane_exchange(send, all_to_all=True))
    total = add2(plane_sum, sibling_swap(plane_sum))
    reduced = _unpack(total, shard_shapes + [weights[k].shape for k in SMALL])
    grads = dict(zip(sharded + SMALL, reduced))
    loss = lax.psum(loss, ("x", "y", "c"))

    def two_d(a):
        return a.reshape(-1, a.shape[-1])

    delta, new_m, new_v = {}, {}, {}
    for k in names:
        d_, m_, v_ = adamw(two_d(weights[k]), two_d(grads[k]), two_d(moms[k]), two_d(vars_[k]))
        delta[k], new_m[k], new_v[k] = (a.reshape(weights[k].shape) for a in (d_, m_, v_))
    return (loss, grad_x, *[grads[k] for k in names], *[delta[k] for k in names],
            *[new_m[k] for k in names], *[new_v[k] for k in names])
```

```python
import functools
import math

import jax
import jax.numpy as jnp
from jax import lax
from jax.experimental import pallas as pl
from jax.experimental.pallas import tpu as pltpu

F32 = jnp.float32
BF16 = jnp.bfloat16

D_MODEL = 1024
SEQ = 2048
DEPTH = 2
HEAD_DIM = 64
N_HEADS_A, N_HEADS_B, N_HEADS_C = 6, 4, 6
A_WIDTH, B_WIDTH, C_WIDTH = N_HEADS_A * HEAD_DIM, N_HEADS_B * HEAD_DIM, N_HEADS_C * HEAD_DIM
CONV_WIDTH = 4
GDN_CHUNK = 64
BLOCK = 128
ROPE_DIM = 16
ROPE_THETA = 500000.0
DILATED_PAIRS = ((128, 1), (512, 4), (2048, 16))
RMS_EPS = 1e-6
NEG = -1e30

NT = (((1,), (1,)), ((), ()))
NN = (((1,), (0,)), ((), ()))
TN = (((0,), (0,)), ((), ()))

VMEM_LIMIT = 48 * 1024 * 1024


def _mm(a, b, dims=NN):
    return lax.dot_general(a.astype(BF16), b.astype(BF16), dims, preferred_element_type=F32)


def _mm32(a, b, dims=NN):
    return lax.dot_general(a, b, dims, precision=lax.Precision.HIGH, preferred_element_type=F32)


def _cparams(*sem):
    return pltpu.CompilerParams(dimension_semantics=sem, vmem_limit_bytes=VMEM_LIMIT)


ORIG_COLS = 3 * A_WIDTH + A_WIDTH + 2 * N_HEADS_A + 4 * B_WIDTH + 4 * C_WIDTH
BA_ORIG = 4 * A_WIDTH
BA_PAD = BLOCK - 2 * N_HEADS_A
P_COLS = ORIG_COLS + BA_PAD
COL_BA = 4 * A_WIDTH
COL_B = COL_BA + BLOCK
COL_C = COL_B + 4 * B_WIDTH
TN_COLS = 384
TM_ROWS = 512


def _rms(x, w):
    return x * lax.rsqrt(jnp.mean(x * x, axis=-1, keepdims=True) + RMS_EPS) * w


def inproj_fwd(x, nw, w):
    n, d = x.shape
    p = w.shape[1]

    def body(x_ref, nw_ref, w_ref, proj_ref, hdn_ref):
        @pl.when(pl.program_id(1) == 0)
        def _():
            hdn_ref[...] = _rms(x_ref[...], nw_ref[...]).astype(BF16)

        proj_ref[...] = jnp.dot(hdn_ref[...], w_ref[...], preferred_element_type=F32)

    return pl.pallas_call(
        body, name="inproj_fwd", grid=(n // TM_ROWS, p // TN_COLS),
        in_specs=[pl.BlockSpec((TM_ROWS, d), lambda i, j: (i, 0)), pl.BlockSpec((1, d), lambda i, j: (0, 0)),
                  pl.BlockSpec((d, TN_COLS), lambda i, j: (0, j))],
        out_specs=[pl.BlockSpec((TM_ROWS, TN_COLS), lambda i, j: (i, j)), pl.BlockSpec((TM_ROWS, d), lambda i, j: (i, 0))],
        out_shape=[jax.ShapeDtypeStruct((n, p), F32), jax.ShapeDtypeStruct((n, d), BF16)],
        compiler_params=_cparams("parallel", "arbitrary"),
    )(x, nw, w)


def mat_tn(a, b, tn):
    n, ka = a.shape
    p = b.shape[1]

    def body(a_ref, b_ref, o_ref):
        @pl.when(pl.program_id(1) == 0)
        def _():
            o_ref[...] = jnp.zeros_like(o_ref)

        o_ref[...] += lax.dot_general(a_ref[...], b_ref[...].astype(BF16), TN, preferred_element_type=F32)

    return pl.pallas_call(
        body, name="mat_tn", grid=(p // tn, n // TM_ROWS),
        in_specs=[pl.BlockSpec((TM_ROWS, ka), lambda j, k: (k, 0)), pl.BlockSpec((TM_ROWS, tn), lambda j, k: (k, j))],
        out_specs=pl.BlockSpec((ka, tn), lambda j, k: (0, j)),
        out_shape=jax.ShapeDtypeStruct((ka, p), F32),
        compiler_params=_cparams("parallel", "arbitrary"),
    )(a, b)


def inproj_bwd(dproj, w, x, nw, dy):
    n, d = x.shape
    p = w.shape[1]
    tm = 256

    def body(dp_ref, w_ref, x_ref, nw_ref, dy_ref, dx_ref, dnw_ref):
        dh = lax.dot_general(dp_ref[...].astype(BF16), w_ref[...], NT, preferred_element_type=F32)
        _, vjp = jax.vjp(_rms, x_ref[...], nw_ref[...])
        dx, dnw = vjp(dh)
        dx_ref[...] = dx + dy_ref[...]
        dnw_ref[0] = dnw

    return pl.pallas_call(
        body, name="inproj_bwd", grid=(n // tm,),
        in_specs=[pl.BlockSpec((tm, p), lambda i: (i, 0)), pl.BlockSpec((d, p), lambda i: (0, 0)),
                  pl.BlockSpec((tm, d), lambda i: (i, 0)), pl.BlockSpec((1, d), lambda i: (0, 0)),
                  pl.BlockSpec((tm, d), lambda i: (i, 0))],
        out_specs=[pl.BlockSpec((tm, d), lambda i: (i, 0)), pl.BlockSpec((1, 1, d), lambda i: (i, 0, 0))],
        out_shape=[jax.ShapeDtypeStruct((n, d), F32), jax.ShapeDtypeStruct((n // tm, 1, d), F32)],
        compiler_params=_cparams("parallel"),
    )(dproj, w, x, nw, dy)


CONV_PAD = 8
CONV_ROWS = 256


def _conv_pre(pad_s, cw, c):
    xs = [pad_s[pl.ds(c * CONV_ROWS + CONV_PAD - (CONV_WIDTH - 1) + k, CONV_ROWS), :] for k in range(CONV_WIDTH)]
    pre = xs[0] * cw[0:1, :]
    for k in range(1, CONV_WIDTH):
        pre = pre + xs[k] * cw[k:k + 1, :]
    return pre, xs


def conv_fwd(proj, cw, nseq):
    n = proj.shape[0]
    t = n // nseq
    ch = cw.shape[1]

    def body(x_ref, cw_ref, y_ref, pad_s):
        pad_s[pl.ds(0, CONV_PAD), :] = jnp.zeros((CONV_PAD, TN_COLS), F32)
        pad_s[pl.ds(CONV_PAD, t), :] = x_ref[...]
        cwv = cw_ref[...]
        for c in range(t // CONV_ROWS):
            pre, _ = _conv_pre(pad_s, cwv, c)
            y_ref[pl.ds(c * CONV_ROWS, CONV_ROWS), :] = pre * _sigmoid(pre)

    return pl.pallas_call(
        body, name="conv_fwd", grid=(nseq, ch // TN_COLS),
        in_specs=[pl.BlockSpec((t, TN_COLS), lambda b, j: (b, j)), pl.BlockSpec((CONV_WIDTH, TN_COLS), lambda b, j: (0, j))],
        out_specs=pl.BlockSpec((t, TN_COLS), lambda b, j: (b, j)),
        out_shape=jax.ShapeDtypeStruct((n, ch), F32),
        scratch_shapes=[pltpu.VMEM((t + CONV_PAD, TN_COLS), F32)],
        compiler_params=_cparams("parallel", "parallel"),
    )(proj, cw)


def conv_bwd(proj, cw, dy, nseq):
    n = proj.shape[0]
    t = n // nseq
    ch = cw.shape[1]

    def body(x_ref, cw_ref, dy_ref, dx_ref, dcw_ref, pad_s, dpad_s):
        pad_s[pl.ds(0, CONV_PAD), :] = jnp.zeros((CONV_PAD, TN_COLS), F32)
        pad_s[pl.ds(CONV_PAD, t), :] = x_ref[...]
        dpad_s[pl.ds(t, CONV_PAD), :] = jnp.zeros((CONV_PAD, TN_COLS), F32)
        cwv = cw_ref[...]
        acc = [jnp.zeros((1, TN_COLS), F32)] * CONV_WIDTH
        for c in range(t // CONV_ROWS):
            pre, xs = _conv_pre(pad_s, cwv, c)
            sg = _sigmoid(pre)
            dpre = dy_ref[pl.ds(c * CONV_ROWS, CONV_ROWS), :] * (sg * (1.0 + pre * (1.0 - sg)))
            dpad_s[pl.ds(c * CONV_ROWS, CONV_ROWS), :] = dpre
            acc = [acc[k] + jnp.sum(dpre * xs[k], axis=0, keepdims=True) for k in range(CONV_WIDTH)]
        for k in range(CONV_WIDTH):
            dcw_ref[0, pl.ds(k, 1), :] = acc[k]
        for c in range(t // CONV_ROWS):
            dx = dpad_s[pl.ds(c * CONV_ROWS + CONV_WIDTH - 1, CONV_ROWS), :] * cwv[0:1, :]
            for k in range(1, CONV_WIDTH):
                dx = dx + dpad_s[pl.ds(c * CONV_ROWS + CONV_WIDTH - 1 - k, CONV_ROWS), :] * cwv[k:k + 1, :]
            dx_ref[pl.ds(c * CONV_ROWS, CONV_ROWS), :] = dx

    blk = pl.BlockSpec((t, TN_COLS), lambda b, j: (b, j))
    return pl.pallas_call(
        body, name="conv_bwd", grid=(nseq, ch // TN_COLS),
        in_specs=[blk, pl.BlockSpec((CONV_WIDTH, TN_COLS), lambda b, j: (0, j)), blk],
        out_specs=[blk, pl.BlockSpec((1, CONV_WIDTH, TN_COLS), lambda b, j: (b, 0, j))],
        out_shape=[jax.ShapeDtypeStruct((n, ch), F32), jax.ShapeDtypeStruct((nseq, CONV_WIDTH, ch), F32)],
        scratch_shapes=[pltpu.VMEM((t + CONV_PAD, TN_COLS), F32)] * 2,
        compiler_params=_cparams("parallel", "parallel"),
    )(proj, cw, dy)


def outproj_fwd(x, o, z, w):
    n, d = x.shape

    def body(x_ref, o_ref, z_ref, w_ref, y_ref, m_ref):
        zv = z_ref[...]
        m_ref[...] = (o_ref[...] * (zv * _sigmoid(zv))).astype(BF16)
        y_ref[...] = x_ref[...] + jnp.dot(m_ref[...], w_ref[...], preferred_element_type=F32)

    blk = pl.BlockSpec((TM_ROWS, d), lambda i: (i, 0))
    return pl.pallas_call(
        body, name="outproj_fwd", grid=(n // TM_ROWS,),
        in_specs=[blk, blk, blk, pl.BlockSpec((d, d), lambda i: (0, 0))],
        out_specs=[blk, blk],
        out_shape=[jax.ShapeDtypeStruct((n, d), F32), jax.ShapeDtypeStruct((n, d), BF16)],
        compiler_params=_cparams("parallel"),
    )(x, o, z, w)


def outproj_bwd(dy, o, z, w):
    n, d = dy.shape

    def body(dy_ref, o_ref, z_ref, w_ref, do_ref, dz_ref):
        dm = lax.dot_general(dy_ref[...].astype(BF16), w_ref[...], NT, preferred_element_type=F32)
        zv = z_ref[...]
        sg = _sigmoid(zv)
        do_ref[...] = dm * (zv * sg)
        dz_ref[...] = dm * o_ref[...] * (sg * (1.0 + zv * (1.0 - sg)))

    blk = pl.BlockSpec((TM_ROWS, d), lambda i: (i, 0))
    return pl.pallas_call(
        body, name="outproj_bwd", grid=(n // TM_ROWS,),
        in_specs=[blk, blk, blk, pl.BlockSpec((d, d), lambda i: (0, 0))],
        out_specs=[blk, blk],
        out_shape=[jax.ShapeDtypeStruct((n, d), F32)] * 2,
        compiler_params=_cparams("parallel"),
    )(dy, o, z, w)


def loss_fwd_bwd(y, target):
    n, d = y.shape

    def body(y_ref, t_ref, dy_ref, part_ref):
        e = y_ref[...] - t_ref[...]
        dy_ref[...] = e * (1.0 / d)
        part_ref[...] = jnp.zeros_like(part_ref) + 0.5 * jnp.sum(e * e) * (1.0 / d)

    blk = pl.BlockSpec((TM_ROWS, d), lambda i: (i, 0))
    return pl.pallas_call(
        body, name="loss", grid=(n // TM_ROWS,),
        in_specs=[blk, blk],
        out_specs=[blk, pl.BlockSpec((1, 8, BLOCK), lambda i: (i, 0, 0))],
        out_shape=[jax.ShapeDtypeStruct((n, d), F32), jax.ShapeDtypeStruct((n // TM_ROWS, 8, BLOCK), F32)],
        compiler_params=_cparams("parallel"),
    )(y, target)


ADAM_LR, ADAM_B1, ADAM_B2, ADAM_EPS, ADAM_WD, ADAM_STEP = 0.001, 0.9, 0.999, 1e-08, 0.01, 10


def adamw(w, g, m, v):
    r, c = w.shape
    tr = r if r <= 256 else 256

    def body(w_ref, g_ref, m_ref, v_ref, d_ref, nm_ref, nv_ref):
        gv = g_ref[...]
        nm = ADAM_B1 * m_ref[...] + (1.0 - ADAM_B1) * gv
        nv = ADAM_B2 * v_ref[...] + (1.0 - ADAM_B2) * (gv * gv)
        m_hat = nm / (1.0 - ADAM_B1 ** ADAM_STEP)
        v_hat = nv / (1.0 - ADAM_B2 ** ADAM_STEP)
        d_ref[...] = -ADAM_LR * (m_hat / (jnp.sqrt(v_hat) + ADAM_EPS) + ADAM_WD * w_ref[...])
        nm_ref[...] = nm
        nv_ref[...] = nv

    blk = pl.BlockSpec((tr, c), lambda i: (i, 0))
    return pl.pallas_call(
        body, name="adamw", grid=(r // tr,),
        in_specs=[blk] * 4, out_specs=[blk] * 3,
        out_shape=[jax.ShapeDtypeStruct((r, c), F32)] * 3,
        compiler_params=_cparams("parallel"),
    )(w, g, m, v)


def _sb_tile(q, k, v, carry, qpos, kpos, tri):
    z = _mm(q * (HEAD_DIM ** -0.5), k, NT)
    earlier = kpos < qpos
    sp = jnp.log(1.0 + jnp.exp(-jnp.abs(z)))
    ls_pos = jnp.minimum(z, 0.0) - sp
    ls_neg = jnp.minimum(-z, 0.0) - sp
    log_keep = jnp.where(earlier, ls_neg, 0.0)
    within = _mm32(log_keep, tri)
    wts = jnp.where(earlier, jnp.exp(jnp.where(earlier, ls_pos + within + carry, 0.0)), 0.0)
    return _mm(wts, v), jnp.sum(log_keep, axis=1, keepdims=True)


def _sb_consts():
    qi = lax.broadcasted_iota(jnp.int32, (BLOCK, 1), 0)
    kj = lax.broadcasted_iota(jnp.int32, (1, BLOCK), 1)
    r = lax.broadcasted_iota(jnp.int32, (BLOCK, BLOCK), 0)
    c = lax.broadcasted_iota(jnp.int32, (BLOCK, BLOCK), 1)
    tri = jnp.where(r > c, 1.0, 0.0).astype(F32)
    return qi, kj, tri


SB_G = 2
SB_GF = 4


_sb_tiles = jax.vmap(_sb_tile, in_axes=(0, 0, 0, 0, None, None, None))


def sb_fwd(q, k, v):
    bh, t, dh = q.shape
    nq = t // BLOCK

    def body(q_ref, k_ref, v_ref, o_ref, carry_ref):
        i = pl.program_id(1)
        qi, kj, tri = _sb_consts()
        qv = q_ref[...]
        qpos = i * BLOCK + qi

        def step(it, st):
            o_acc, c = st
            j = i - it
            rows = pl.ds(pl.multiple_of(j * BLOCK, BLOCK), BLOCK)
            carry_ref[:, 0, j] = c
            o, tot = _sb_tiles(qv, k_ref[:, rows, :], v_ref[:, rows, :], c, qpos, j * BLOCK + kj, tri)
            return o_acc + o, c + tot

        o_acc, _ = lax.fori_loop(0, i + 1, step, (jnp.zeros((SB_GF, BLOCK, dh), F32), jnp.zeros((SB_GF, BLOCK, 1), F32)))
        o_ref[...] = o_acc

    return pl.pallas_call(
        body, name="sb_fwd", grid=(bh // SB_GF, nq),
        in_specs=[pl.BlockSpec((SB_GF, BLOCK, dh), lambda b, i: (b, i, 0)),
                  pl.BlockSpec((SB_GF, t, dh), lambda b, i: (b, 0, 0)),
                  pl.BlockSpec((SB_GF, t, dh), lambda b, i: (b, 0, 0))],
        out_specs=[pl.BlockSpec((SB_GF, BLOCK, dh), lambda b, i: (b, i, 0)),
                   pl.BlockSpec((SB_GF, 1, nq, BLOCK, 1), lambda b, i: (b, i, 0, 0, 0))],
        out_shape=[jax.ShapeDtypeStruct((bh, t, dh), F32),
                   jax.ShapeDtypeStruct((bh, nq, nq, BLOCK, 1), F32)],
        compiler_params=_cparams("parallel", "arbitrary"),
    )(q, k, v)


def sb_bwd(q, k, v, carries, do):
    bh, t, dh = q.shape
    nq = t // BLOCK

    def body(q_ref, k_ref, v_ref, carry_ref, do_ref, dq_ref, dk_ref, dv_ref):
        i = pl.program_id(1)
        qi, kj, tri = _sb_consts()
        qv = q_ref[...]
        dov = do_ref[...]
        qpos = i * BLOCK + qi

        @pl.when(i == 0)
        def _():
            dk_ref[...] = jnp.zeros_like(dk_ref)
            dv_ref[...] = jnp.zeros_like(dv_ref)

        def step(j, st):
            dq_acc, dc = st
            rows = pl.ds(pl.multiple_of(j * BLOCK, BLOCK), BLOCK)
            kpos = j * BLOCK + kj
            f = lambda q_, k_, v_, c_: _sb_tiles(q_, k_, v_, c_, qpos, kpos, tri)
            _, vjp = jax.vjp(f, qv, k_ref[:, rows, :], v_ref[:, rows, :], carry_ref[:, 0, j])
            dq, dk, dv, dcj = vjp((dov, dc))
            dk_ref[:, rows, :] += dk
            dv_ref[:, rows, :] += dv
            return dq_acc + dq, dc + dcj

        dq_acc, _ = lax.fori_loop(0, i + 1, step, (jnp.zeros((SB_G, BLOCK, dh), F32), jnp.zeros((SB_G, BLOCK, 1), F32)))
        dq_ref[...] = dq_acc

    full = pl.BlockSpec((SB_G, t, dh), lambda b, i: (b, 0, 0))
    blk = pl.BlockSpec((SB_G, BLOCK, dh), lambda b, i: (b, i, 0))
    return pl.pallas_call(
        body, name="sb_bwd", grid=(bh // SB_G, nq),
        in_specs=[blk, full, full,
                  pl.BlockSpec((SB_G, 1, nq, BLOCK, 1), lambda b, i: (b, i, 0, 0, 0)), blk],
        out_specs=[blk, full, full],
        out_shape=[jax.ShapeDtypeStruct((bh, t, dh), F32)] * 3,
        compiler_params=_cparams("parallel", "arbitrary"),
    )(q, k, v, carries, do)


def _sigmoid(x):
    return 0.5 * (jnp.tanh(0.5 * x) + 1.0)


def _softplus(x):
    return jnp.maximum(x, 0.0) + jnp.log(1.0 + jnp.exp(-jnp.abs(x)))


def _gdn_chunk(q, k, v, al_c, al_r, br_c, alog, dtb, nw, s):
    c = GDN_CHUNK
    ri = lax.broadcasted_iota(jnp.int32, (c, c), 0)
    ci = lax.broadcasted_iota(jnp.int32, (c, c), 1)
    incl, strict = ri >= ci, ri > ci
    eye = jnp.where(ri == ci, 1.0, 0.0).astype(F32)
    rate = -jnp.exp(alog)
    g_c = rate * _softplus(al_c + dtb)
    g_r = rate * _softplus(al_r + dtb)
    beta = _sigmoid(br_c)
    gc_c = jnp.sum(jnp.where(incl, g_r, 0.0), axis=1, keepdims=True)
    gc_r = jnp.sum(jnp.where(ri <= ci, g_c, 0.0), axis=0, keepdims=True)
    gl = jnp.sum(g_r, axis=1, keepdims=True)
    decay = jnp.where(incl, jnp.exp(jnp.where(incl, gc_c - gc_r, 0.0)), 0.0)
    qn = q * lax.rsqrt(jnp.sum(q * q, axis=-1, keepdims=True) + RMS_EPS) * (HEAD_DIM ** -0.5)
    kn = k * lax.rsqrt(jnp.sum(k * k, axis=-1, keepdims=True) + RMS_EPS)
    kb = kn * beta
    a = jnp.where(strict, _mm(kb, kn, NT) * decay, 0.0)
    tmat = eye - a
    p = a
    for _ in range(5):
        p = _mm32(p, p)
        tmat = tmat + _mm32(tmat, p)
    u = _mm(tmat, v * beta)
    w = _mm(tmat, kb * jnp.exp(gc_c))
    qk = _mm(qn, kn, NT) * decay
    v_new = u - _mm(w, s)
    o = _mm(qn * jnp.exp(gc_c), s) + _mm(qk, v_new)
    s_new = s * jnp.exp(gl) + _mm(kn * jnp.exp(gl - gc_c), v_new, TN)
    o = o * lax.rsqrt(jnp.mean(o * o, axis=-1, keepdims=True) + RMS_EPS) * nw
    return o, s_new


GDN_G = 6
GDN_TB = 256


def _gdn_specs(rev, nt):
    tpos = (lambda i: nt - 1 - i) if rev else (lambda i: i)
    ncb = GDN_TB // GDN_CHUNK
    seq = pl.BlockSpec((GDN_G, GDN_TB, HEAD_DIM), lambda b, i: (b, tpos(i), 0))
    col = pl.BlockSpec((GDN_G, GDN_TB, 1), lambda b, i: (b, tpos(i), 0))
    row = pl.BlockSpec((GDN_G, 1, ncb, GDN_CHUNK), lambda b, i: (b, tpos(i), 0, 0))
    one = pl.BlockSpec((GDN_G, 1, 1), lambda b, i: (b, 0, 0))
    vec = pl.BlockSpec((1, HEAD_DIM), lambda b, i: (0, 0))
    st = pl.BlockSpec((GDN_G, ncb, HEAD_DIM, HEAD_DIM), lambda b, i: (b, tpos(i), 0, 0))
    return seq, col, row, one, vec, st


_gdn_chunks = jax.vmap(_gdn_chunk, in_axes=(0, 0, 0, 0, 0, 0, 0, 0, None, 0))


def gdn_fwd(q, k, v, al_c, al_r, br_c, alog, dtb, nw):
    bh, t, dh = q.shape
    nc, nt, ncb = t // GDN_CHUNK, t // GDN_TB, GDN_TB // GDN_CHUNK

    def body(q_ref, k_ref, v_ref, alc_ref, alr_ref, brc_ref, alog_ref, dtb_ref, nw_ref, o_ref, st_ref, s_s):
        @pl.when(pl.program_id(1) == 0)
        def _():
            s_s[...] = jnp.zeros_like(s_s)

        def step(c, s):
            rows = pl.ds(pl.multiple_of(c * GDN_CHUNK, GDN_CHUNK), GDN_CHUNK)
            st_ref[:, c] = s
            o, s_new = _gdn_chunks(q_ref[:, rows, :], k_ref[:, rows, :], v_ref[:, rows, :], alc_ref[:, rows, :],
                                   alr_ref[:, 0, pl.ds(c, 1), :], brc_ref[:, rows, :], alog_ref[...], dtb_ref[...],
                                   nw_ref[...], s)
            o_ref[:, rows, :] = o
            return s_new

        s_s[...] = lax.fori_loop(0, ncb, step, s_s[...])

    seq, col, row, one, vec, st = _gdn_specs(False, nt)
    return pl.pallas_call(
        body, name="gdn_fwd", grid=(bh // GDN_G, nt),
        in_specs=[seq, seq, seq, col, row, col, one, one, vec],
        out_specs=[seq, st],
        out_shape=[jax.ShapeDtypeStruct((bh, t, dh), F32), jax.ShapeDtypeStruct((bh, nc, dh, dh), F32)],
        scratch_shapes=[pltpu.VMEM((GDN_G, dh, dh), F32)],
        compiler_params=_cparams("parallel", "arbitrary"),
    )(q, k, v, al_c, al_r, br_c, alog, dtb, nw)


def gdn_bwd(q, k, v, al_c, al_r, br_c, alog, dtb, nw, states, do):
    bh, t, dh = q.shape
    nc, nt, ncb = t // GDN_CHUNK, t // GDN_TB, GDN_TB // GDN_CHUNK

    def body(q_ref, k_ref, v_ref, alc_ref, alr_ref, brc_ref, alog_ref, dtb_ref, nw_ref, st_ref, do_ref,
             dq_ref, dk_ref, dv_ref, dalc_ref, dalr_ref, dbrc_ref, dalog_ref, ddtb_ref, dnw_ref, ds_s):
        @pl.when(pl.program_id(1) == 0)
        def _():
            ds_s[...] = jnp.zeros_like(ds_s)
            dalog_ref[...] = jnp.zeros_like(dalog_ref)
            ddtb_ref[...] = jnp.zeros_like(ddtb_ref)
            dnw_ref[...] = jnp.zeros_like(dnw_ref)

        def step(it, carry):
            ds, dalog, ddtb, dnw = carry
            c = ncb - 1 - it
            rows = pl.ds(pl.multiple_of(c * GDN_CHUNK, GDN_CHUNK), GDN_CHUNK)
            _, vjp = jax.vjp(_gdn_chunks, q_ref[:, rows, :], k_ref[:, rows, :], v_ref[:, rows, :], alc_ref[:, rows, :],
                             alr_ref[:, 0, pl.ds(c, 1), :], brc_ref[:, rows, :], alog_ref[...], dtb_ref[...], nw_ref[...],
                             st_ref[:, c])
            dq, dk, dv, dalc, dalr, dbrc, da, dd, dn, ds = vjp((do_ref[:, rows, :], ds))
            dq_ref[:, rows, :] = dq
            dk_ref[:, rows, :] = dk
            dv_ref[:, rows, :] = dv
            dalc_ref[:, rows, :] = dalc
            dalr_ref[:, 0, pl.ds(c, 1), :] = dalr
            dbrc_ref[:, rows, :] = dbrc
            return ds, dalog + da, ddtb + dd, dnw + dn

        z11 = jnp.zeros((GDN_G, 1, 1), F32)
        ds, dalog, ddtb, dnw = lax.fori_loop(0, ncb, step, (ds_s[...], z11, z11, jnp.zeros((1, dh), F32)))
        ds_s[...] = ds
        dalog_ref[...] += dalog
        ddtb_ref[...] += ddtb
        dnw_ref[0] += dnw

    seq, col, row, one, vec, st = _gdn_specs(True, nt)
    wout = pl.BlockSpec((1, 1, dh), lambda b, i: (b, 0, 0))
    sd = jax.ShapeDtypeStruct
    return pl.pallas_call(
        body, name="gdn_bwd", grid=(bh // GDN_G, nt),
        in_specs=[seq, seq, seq, col, row, col, one, one, vec, st, seq],
        out_specs=[seq, seq, seq, col, row, col, one, one, wout],
        out_shape=[sd((bh, t, dh), F32)] * 3 + [sd((bh, t, 1), F32), sd((bh, nt, ncb, GDN_CHUNK), F32), sd((bh, t, 1), F32),
                                                 sd((bh, 1, 1), F32), sd((bh, 1, 1), F32), sd((bh // GDN_G, 1, dh), F32)],
        scratch_shapes=[pltpu.VMEM((GDN_G, dh, dh), F32)],
        compiler_params=_cparams("parallel", "arbitrary"),
    )(q, k, v, al_c, al_r, br_c, alog, dtb, nw, states, do)


DIL_NB = tuple((SEQ // d) // BLOCK for _, d in DILATED_PAIRS)
DIL_D = tuple(d for _, d in DILATED_PAIRS)
DIL_STEPS = tuple(w // d for w, d in DILATED_PAIRS)
ROWS = 256
DIL_B = 4


def _rope_tables(t):
    half = ROPE_DIM // 2
    inv_freq = ROPE_THETA ** (-jnp.arange(half, dtype=F32) / half)
    ang = jnp.arange(t, dtype=F32)[:, None] * inv_freq[None, :]
    ones = jnp.ones((t, HEAD_DIM - ROPE_DIM), F32)
    cs = jnp.concatenate([jnp.cos(ang), jnp.cos(ang), ones], axis=1)
    sn = jnp.concatenate([jnp.sin(ang), jnp.sin(ang), 0.0 * ones], axis=1)
    i = jnp.arange(HEAD_DIM)[:, None]
    j = jnp.arange(HEAD_DIM)[None, :]
    pm = (jnp.where((j < half) & (i == j + half), -1.0, 0.0)
          + jnp.where((j >= half) & (j < ROPE_DIM) & (i == j - half), 1.0, 0.0))
    return cs, sn, pm.astype(F32)


def _dil_prep(x, w, cs, sn, pm):
    y = x * lax.rsqrt(jnp.mean(x * x, axis=-1, keepdims=True) + RMS_EPS) * w
    return y * cs + _mm32(y, pm) * sn


def _dil_tile(qn, kk, vv, mask):
    s = jnp.where(mask, _mm(qn * (HEAD_DIM ** -0.5), kk, NT), NEG)
    m = lax.stop_gradient(jnp.max(s, axis=-1, keepdims=True))
    p = jnp.exp(s - m)
    denom = jnp.sum(p, axis=-1, keepdims=True)
    return _mm(p, vv) / denom, m + jnp.log(denom)


_dil_tiles = jax.vmap(_dil_tile)


def _dil_mix(o1, o2, o3, l1, l2, l3):
    m = lax.stop_gradient(jnp.maximum(jnp.maximum(l1, l2), l3))
    e1, e2, e3 = jnp.exp(l1 - m), jnp.exp(l2 - m), jnp.exp(l3 - m)
    return (e1 * o1 + e2 * o2 + e3 * o3) / (e1 + e2 + e3)


def _dil_mask(it, g):
    nb = DIL_NB[g]
    n = it % nb
    r = it // nb
    kstart = jnp.maximum(it - 1, 0) * BLOCK
    iq = n * BLOCK + lax.broadcasted_iota(jnp.int32, (BLOCK, 1), 0)
    ik = kstart - r * (nb * BLOCK) + lax.broadcasted_iota(jnp.int32, (1, 2 * BLOCK), 1)
    mask = (ik >= 0) & (iq >= ik) & (iq - ik <= DIL_STEPS[g])
    return mask, pl.ds(pl.multiple_of(it * BLOCK, BLOCK), BLOCK), pl.ds(pl.multiple_of(kstart, BLOCK), 2 * BLOCK)


def _dil_gather(src, dst, d):
    t = src.shape[0]
    ln = t // d
    for r in range(d):
        dst[pl.ds(r * ln, ln), :] = src[pl.ds(r, ln, stride=d), :]


def _dil_scatter(src, dst, d):
    t = src.shape[0]
    ln = t // d
    for r in range(d):
        dst[pl.ds(r, ln, stride=d), :] = src[pl.ds(r * ln, ln), :]


def _dil_forward_parts(q_ref, k_ref, v_ref, qw, kw, cs_ref, sn_ref, pm, qn_s, kn_s, dl_s, od_s, ld_s, on_s, ln_s):
    t = qn_s.shape[0]

    def prep(c, _):
        rows = pl.ds(pl.multiple_of(c * ROWS, ROWS), ROWS)
        qn_s[rows, :] = _dil_prep(q_ref[0, rows, :], qw, cs_ref[rows, :], sn_ref[rows, :], pm)
        kn_s[rows, :] = _dil_prep(k_ref[0, rows, :], kw, cs_ref[rows, :], sn_ref[rows, :], pm)
        return 0

    lax.fori_loop(0, t // ROWS, prep, 0)
    for g in (1, 2):
        _dil_gather(qn_s, dl_s.at[g - 1, 0], DIL_D[g])
        _dil_gather(kn_s, dl_s.at[g - 1, 1], DIL_D[g])
        _dil_gather(v_ref.at[0], dl_s.at[g - 1, 2], DIL_D[g])
    for g in range(3):
        qs = qn_s if g == 0 else dl_s.at[g - 1, 0]
        ks = kn_s if g == 0 else dl_s.at[g - 1, 1]
        vs = v_ref.at[0] if g == 0 else dl_s.at[g - 1, 2]

        def tiles(i, _, g=g, qs=qs, ks=ks, vs=vs):
            where = [_dil_mask(i * DIL_B + b, g) for b in range(DIL_B)]
            o, lse = _dil_tiles(jnp.stack([qs[qr, :] for _, qr, _ in where]), jnp.stack([ks[kr, :] for _, _, kr in where]),
                                jnp.stack([vs[kr, :] for _, _, kr in where]), jnp.stack([m for m, _, _ in where]))
            for b, (_, qr, _) in enumerate(where):
                od_s[g, qr, :] = o[b]
                ld_s[g, qr, :] = lse[b]
            return 0

        lax.fori_loop(0, t // BLOCK // DIL_B, tiles, 0)
    for g in (1, 2):
        _dil_scatter(od_s.at[g], on_s.at[g - 1], DIL_D[g])
        _dil_scatter(ld_s.at[g], ln_s.at[g - 1], DIL_D[g])


def _dil_scratch(t):
    return [pltpu.VMEM((t, HEAD_DIM), F32), pltpu.VMEM((t, HEAD_DIM), F32),
            pltpu.VMEM((2, 3, t, HEAD_DIM), F32),
            pltpu.VMEM((3, t, HEAD_DIM), F32), pltpu.VMEM((3, t, 1), F32),
            pltpu.VMEM((2, t, HEAD_DIM), F32), pltpu.VMEM((2, t, 1), F32)]


def dil_fwd(q, k, v, qw, kw, cs, sn, pm):
    bh, t, dh = q.shape

    def body(q_ref, k_ref, v_ref, qw_ref, kw_ref, cs_ref, sn_ref, pm_ref, o_ref, qn_s, kn_s, dl_s, od_s, ld_s, on_s, ln_s):
        _dil_forward_parts(q_ref, k_ref, v_ref, qw_ref[...], kw_ref[...], cs_ref, sn_ref, pm_ref[...],
                           qn_s, kn_s, dl_s, od_s, ld_s, on_s, ln_s)

        def mix(c, _):
            rows = pl.ds(pl.multiple_of(c * ROWS, ROWS), ROWS)
            o_ref[0, rows, :] = _dil_mix(od_s[0, rows, :], on_s[0, rows, :], on_s[1, rows, :],
                                         ld_s[0, rows, :], ln_s[0, rows, :], ln_s[1, rows, :])
            return 0

        lax.fori_loop(0, t // ROWS, mix, 0)

    seq = pl.BlockSpec((1, t, dh), lambda b: (b, 0, 0))
    vec = pl.BlockSpec((1, dh), lambda b: (0, 0))
    tab = pl.BlockSpec((t, dh), lambda b: (0, 0))
    return pl.pallas_call(
        body, name="dil_fwd", grid=(bh,),
        in_specs=[seq, seq, seq, vec, vec, tab, tab, pl.BlockSpec((dh, dh), lambda b: (0, 0))],
        out_specs=seq,
        out_shape=jax.ShapeDtypeStruct((bh, t, dh), F32),
        scratch_shapes=_dil_scratch(t),
        compiler_params=_cparams("parallel"),
    )(q, k, v, qw, kw, cs, sn, pm)


def dil_bwd(q, k, v, qw, kw, cs, sn, pm, do):
    bh, t, dh = q.shape

    def body(q_ref, k_ref, v_ref, qw_ref, kw_ref, cs_ref, sn_ref, pm_ref, do_ref,
             dq_ref, dk_ref, dv_ref, dqw_ref, dkw_ref,
             qn_s, kn_s, dl_s, od_s, ld_s, on_s, ln_s, tq_s, tk_s, tv_s):
        qw, kw, pm = qw_ref[...], kw_ref[...], pm_ref[...]
        _dil_forward_parts(q_ref, k_ref, v_ref, qw, kw, cs_ref, sn_ref, pm, qn_s, kn_s, dl_s, od_s, ld_s, on_s, ln_s)

        def mix(c, _):
            rows = pl.ds(pl.multiple_of(c * ROWS, ROWS), ROWS)
            _, vjp = jax.vjp(_dil_mix, od_s[0, rows, :], on_s[0, rows, :], on_s[1, rows, :],
                             ld_s[0, rows, :], ln_s[0, rows, :], ln_s[1, rows, :])
            d1, d2, d3, e1, e2, e3 = vjp(do_ref[0, rows, :])
            od_s[0, rows, :] = d1
            on_s[0, rows, :] = d2
            on_s[1, rows, :] = d3
            ld_s[0, rows, :] = e1
            ln_s[0, rows, :] = e2
            ln_s[1, rows, :] = e3
            return 0

        lax.fori_loop(0, t // ROWS, mix, 0)
        for g in (1, 2):
            _dil_gather(on_s.at[g - 1], od_s.at[g], DIL_D[g])
            _dil_gather(ln_s.at[g - 1], ld_s.at[g], DIL_D[g])
        on_s[...] = jnp.zeros_like(on_s)
        dv_ref[...] = jnp.zeros_like(dv_ref)
        for g in range(3):
            qs = qn_s if g == 0 else dl_s.at[g - 1, 0]
            ks = kn_s if g == 0 else dl_s.at[g - 1, 1]
            vs = v_ref.at[0] if g == 0 else dl_s.at[g - 1, 2]
            gq = on_s.at[0] if g == 0 else tq_s
            gk = on_s.at[1] if g == 0 else tk_s
            gv = dv_ref.at[0] if g == 0 else tv_s
            if g > 0:
                tk_s[...] = jnp.zeros_like(tk_s)
                tv_s[...] = jnp.zeros_like(tv_s)

            def tiles(i, _, g=g, qs=qs, ks=ks, vs=vs, gq=gq, gk=gk, gv=gv):
                where = [_dil_mask(i * DIL_B + b, g) for b in range(DIL_B)]
                masks = jnp.stack([m for m, _, _ in where])
                _, vjp = jax.vjp(lambda q_, k_, v_: _dil_tiles(q_, k_, v_, masks),
                                 jnp.stack([qs[qr, :] for _, qr, _ in where]), jnp.stack([ks[kr, :] for _, _, kr in where]),
                                 jnp.stack([vs[kr, :] for _, _, kr in where]))
                dq, dkk, dvv = vjp((jnp.stack([od_s[g, qr, :] for _, qr, _ in where]),
                                    jnp.stack([ld_s[g, qr, :] for _, qr, _ in where])))
                for b, (_, qr, kr) in enumerate(where):
                    gq[qr, :] = dq[b]
                    gk[kr, :] += dkk[b]
                    gv[kr, :] += dvv[b]
                return 0

            lax.fori_loop(0, t // BLOCK // DIL_B, tiles, 0)
            if g > 0:
                d = DIL_D[g]
                ln = t // d
                for r in range(d):
                    nat, dil = pl.ds(r, ln, stride=d), pl.ds(r * ln, ln)
                    on_s[0, nat, :] += tq_s[dil, :]
                    on_s[1, nat, :] += tk_s[dil, :]
                    dv_ref[0, nat, :] += tv_s[dil, :]

        def prep(c, acc):
            rows = pl.ds(pl.multiple_of(c * ROWS, ROWS), ROWS)
            f = lambda x, w: _dil_prep(x, w, cs_ref[rows, :], sn_ref[rows, :], pm)
            _, vq = jax.vjp(f, q_ref[0, rows, :], qw)
            _, vk = jax.vjp(f, k_ref[0, rows, :], kw)
            dq, dqw = vq(on_s[0, rows, :])
            dk, dkw = vk(on_s[1, rows, :])
            dq_ref[0, rows, :] = dq
            dk_ref[0, rows, :] = dk
            return acc[0] + dqw, acc[1] + dkw

        dqw, dkw = lax.fori_loop(0, t // ROWS, prep, (jnp.zeros((1, dh), F32), jnp.zeros((1, dh), F32)))
        dqw_ref[0] = dqw
        dkw_ref[0] = dkw

    seq = pl.BlockSpec((1, t, dh), lambda b: (b, 0, 0))
    vec = pl.BlockSpec((1, dh), lambda b: (0, 0))
    tab = pl.BlockSpec((t, dh), lambda b: (0, 0))
    wout = pl.BlockSpec((1, 1, dh), lambda b: (b, 0, 0))
    return pl.pallas_call(
        body, name="dil_bwd", grid=(bh,),
        in_specs=[seq, seq, seq, vec, vec, tab, tab, pl.BlockSpec((dh, dh), lambda b: (0, 0)), seq],
        out_specs=[seq, seq, seq, wout, wout],
        out_shape=[jax.ShapeDtypeStruct((bh, t, dh), F32)] * 3 + [jax.ShapeDtypeStruct((bh, 1, dh), F32)] * 2,
        scratch_shapes=_dil_scratch(t) + [pltpu.VMEM((t, dh), F32)] * 3,
        compiler_params=_cparams("parallel"),
    )(q, k, v, qw, kw, cs, sn, pm, do)


N_CHIPS = 4
SUM_ROWS = 184
MESH_IDS = pl.DeviceIdType.MESH
ANY = pl.BlockSpec(memory_space=pl.ANY)


def plane_exchange(src, all_to_all):
    blk_shape = src.shape[1:] if all_to_all else src.shape

    def body(src_ref, out_ref, send_sems, recv_sems, local_sem):
        x, y, c = lax.axis_index("x"), lax.axis_index("y"), lax.axis_index("c")
        me = 2 * x + y
        mine = pltpu.make_async_copy(src_ref.at[me] if all_to_all else src_ref, out_ref.at[me], local_sem)
        mine.start()
        sends = []
        for k in (1, 2, 3):
            px = 1 - x if k & 2 else x
            py = 1 - y if k & 1 else y
            peer = 2 * px + py
            cp = pltpu.make_async_remote_copy(
                src_ref=src_ref.at[peer] if all_to_all else src_ref, dst_ref=out_ref.at[me],
                send_sem=send_sems.at[k - 1], recv_sem=recv_sems.at[k - 1],
                device_id=(px, py, c), device_id_type=MESH_IDS)
            cp.start()
            sends.append((cp, peer, (px, py, c)))
        for k, (cp, peer, dev) in enumerate(sends):
            pltpu.make_async_remote_copy(
                src_ref=out_ref.at[me], dst_ref=out_ref.at[peer],
                send_sem=send_sems.at[k], recv_sem=recv_sems.at[k],
                device_id=dev, device_id_type=MESH_IDS).wait_recv()
        for cp, _, _ in sends:
            cp.wait_send()
        mine.wait()

    return pl.pallas_call(
        body, name="plane_all_to_all" if all_to_all else "plane_all_gather",
        in_specs=[ANY], out_specs=ANY,
        out_shape=jax.ShapeDtypeStruct((N_CHIPS,) + blk_shape, src.dtype),
        scratch_shapes=[pltpu.SemaphoreType.DMA((3,)), pltpu.SemaphoreType.DMA((3,)), pltpu.SemaphoreType.DMA],
    )(src)


def sibling_swap(src):
    def body(src_ref, out_ref, send_sem, recv_sem):
        x, y, c = lax.axis_index("x"), lax.axis_index("y"), lax.axis_index("c")
        cp = pltpu.make_async_remote_copy(src_ref=src_ref, dst_ref=out_ref, send_sem=send_sem, recv_sem=recv_sem,
                                          device_id=(x, y, 1 - c), device_id_type=MESH_IDS)
        cp.start()
        cp.wait()

    return pl.pallas_call(
        body, name="sibling_swap", in_specs=[ANY], out_specs=ANY,
        out_shape=jax.ShapeDtypeStruct(src.shape, src.dtype),
        scratch_shapes=[pltpu.SemaphoreType.DMA, pltpu.SemaphoreType.DMA],
    )(src)


def sum4(a):
    _, r, c = a.shape
    tr = SUM_ROWS

    def body(a_ref, o_ref):
        o_ref[...] = (a_ref[0] + a_ref[1]) + (a_ref[2] + a_ref[3])

    return pl.pallas_call(
        body, name="sum4", grid=(r // tr,),
        in_specs=[pl.BlockSpec((N_CHIPS, tr, c), lambda i: (0, i, 0))],
        out_specs=pl.BlockSpec((tr, c), lambda i: (i, 0)),
        out_shape=jax.ShapeDtypeStruct((r, c), F32),
        compiler_params=_cparams("parallel"),
    )(a)


def add2(a, b):
    r, c = a.shape
    tr = SUM_ROWS

    def body(a_ref, b_ref, o_ref):
        o_ref[...] = a_ref[...] + b_ref[...]

    blk = pl.BlockSpec((tr, c), lambda i: (i, 0))
    return pl.pallas_call(
        body, name="add2", grid=(r // tr,), in_specs=[blk, blk], out_specs=blk,
        out_shape=jax.ShapeDtypeStruct((r, c), F32), compiler_params=_cparams("parallel"),
    )(a, b)


PACK_COLS = 1024
PACK_ROWS = 2576


def _pack(parts):
    flat = jnp.concatenate([p.reshape(-1) for p in parts])
    flat = jnp.pad(flat, (0, PACK_ROWS * PACK_COLS - flat.shape[0]))
    return flat.reshape(PACK_ROWS, PACK_COLS)


def _unpack(buf, shapes):
    flat = buf.reshape(-1)
    out, at = [], 0
    for s in shapes:
        size = math.prod(s)
        out.append(flat[at:at + size].reshape(s))
        at += size
    return out


def _to_heads(a, nseq, nh):
    t = a.shape[0] // nseq
    return a.reshape(nseq, t, nh, HEAD_DIM).transpose(0, 2, 1, 3).reshape(nseq * nh, t, HEAD_DIM)


def _from_heads(a, nseq, nh):
    t = a.shape[1]
    return a.reshape(nseq, nh, t, HEAD_DIM).transpose(0, 2, 1, 3).reshape(nseq * t, nh * HEAD_DIM)


def _layer_fwd(x, p, nseq, tabs):
    n = x.shape[0]
    t = n // nseq
    proj, hdn = inproj_fwd(x, p["norm_w"][None], p["w_in"])
    ya = conv_fwd(proj, p["conv_w"], nseq)
    qa, ka, va = (_to_heads(ya[:, i * A_WIDTH:(i + 1) * A_WIDTH], nseq, N_HEADS_A) for i in range(3))
    ba = proj[:, COL_BA:COL_BA + 2 * N_HEADS_A].reshape(nseq, t, 2 * N_HEADS_A).transpose(0, 2, 1)
    beta_raw, alpha = ba[:, :N_HEADS_A], ba[:, N_HEADS_A:]
    bha = nseq * N_HEADS_A
    al_c, al_r = alpha.reshape(bha, t, 1), alpha.reshape(bha, t // GDN_TB, GDN_TB // GDN_CHUNK, GDN_CHUNK)
    br_c = beta_raw.reshape(bha, t, 1)
    alog = jnp.tile(p["a_log"], nseq).reshape(bha, 1, 1)
    dtb = jnp.tile(p["dt_bias"], nseq).reshape(bha, 1, 1)
    a_in = (qa, ka, va, al_c, al_r, br_c, alog, dtb, p["gdn_norm_w"][None])
    oa, states = gdn_fwd(*a_in)
    b_in = tuple(_to_heads(proj[:, COL_B + i * B_WIDTH:COL_B + (i + 1) * B_WIDTH], nseq, N_HEADS_B) for i in range(3))
    ob, carries = sb_fwd(*b_in)
    c_in = tuple(_to_heads(proj[:, COL_C + i * C_WIDTH:COL_C + (i + 1) * C_WIDTH], nseq, N_HEADS_C) for i in range(3))
    c_in = c_in + (p["q_norm_w"][None], p["k_norm_w"][None]) + tabs
    oc = dil_fwd(*c_in)
    o = jnp.concatenate([_from_heads(oa, nseq, N_HEADS_A), _from_heads(ob, nseq, N_HEADS_B),
                         _from_heads(oc, nseq, N_HEADS_C)], axis=1)
    z = jnp.concatenate([proj[:, 3 * A_WIDTH:4 * A_WIDTH], proj[:, COL_B + 3 * B_WIDTH:COL_C],
                         proj[:, COL_C + 3 * C_WIDTH:]], axis=1)
    y, mixed = outproj_fwd(x, o, z, p["w_out"])
    return y, dict(x=x, hdn=hdn, proj=proj, a_in=a_in, states=states, b_in=b_in, carries=carries, c_in=c_in,
                   o=o, z=z, mixed=mixed)


def _layer_bwd(dy, p, res, nseq):
    n = dy.shape[0]
    t = n // nseq
    g = {}
    g["w_out"] = mat_tn(res["mixed"], dy, 512)
    do, dz = outproj_bwd(dy, res["o"], res["z"], p["w_out"])
    doa = _to_heads(do[:, :A_WIDTH], nseq, N_HEADS_A)
    dob = _to_heads(do[:, A_WIDTH:A_WIDTH + B_WIDTH], nseq, N_HEADS_B)
    doc = _to_heads(do[:, A_WIDTH + B_WIDTH:], nseq, N_HEADS_C)
    dqc, dkc, dvc, dqw, dkw = dil_bwd(*res["c_in"], doc)
    g["q_norm_w"], g["k_norm_w"] = dqw.sum((0, 1)), dkw.sum((0, 1))
    dqb, dkb, dvb = sb_bwd(*res["b_in"], res["carries"], dob)
    dqa, dka, dva, dalc, dalr, dbrc, dalog, ddtb, dnw = gdn_bwd(*res["a_in"], res["states"], doa)
    g["a_log"] = dalog.reshape(nseq, N_HEADS_A).sum(0)
    g["dt_bias"] = ddtb.reshape(nseq, N_HEADS_A).sum(0)
    g["gdn_norm_w"] = dnw.sum((0, 1))
    dya = jnp.concatenate([_from_heads(d, nseq, N_HEADS_A) for d in (dqa, dka, dva)], axis=1)
    dqkv, dcw = conv_bwd(res["proj"], p["conv_w"], dya, nseq)
    g["conv_w"] = dcw.sum(0)
    dalpha = (dalc + dalr.reshape(dalc.shape)).reshape(nseq, N_HEADS_A, t)
    dba = jnp.concatenate([dbrc.reshape(nseq, N_HEADS_A, t), dalpha], axis=1).transpose(0, 2, 1).reshape(n, 2 * N_HEADS_A)
    dba = jnp.pad(dba, ((0, 0), (0, BA_PAD)))
    dproj = jnp.concatenate(
        [dqkv, dz[:, :A_WIDTH], dba]
        + [_from_heads(d, nseq, N_HEADS_B) for d in (dqb, dkb, dvb)] + [dz[:, A_WIDTH:A_WIDTH + B_WIDTH]]
        + [_from_heads(d, nseq, N_HEADS_C) for d in (dqc, dkc, dvc)] + [dz[:, A_WIDTH + B_WIDTH:]], axis=1)
    g["w_in"] = mat_tn(res["hdn"], dproj, TN_COLS)
    dx, dnw_tiles = inproj_bwd(dproj, p["w_in"], res["x"], p["norm_w"][None], dy)
    g["norm_w"] = dnw_tiles.sum((0, 1))
    return dx, g


SMALL = ("norm_w", "a_log", "dt_bias", "gdn_norm_w", "q_norm_w", "k_norm_w")


def _local_step(x, target, full):
    nseq, t, d = x.shape
    tabs = _rope_tables(t)
    h = x.reshape(nseq * t, d)
    saved = []
    for l in range(DEPTH):
        p = {k: v[l] for k, v in full.items()}
        h, res = _layer_fwd(h, p, nseq, tabs)
        saved.append((p, res))
    dy, parts = loss_fwd_bwd(h, target.reshape(nseq * t, d))
    loss = parts[:, 0, 0].sum()
    grads = [None] * DEPTH
    for l in reversed(range(DEPTH)):
        p, res = saved[l]
        dy, grads[l] = _layer_bwd(dy, p, res, nseq)
    return loss, dy.reshape(nseq, t, d), {k: jnp.stack([g[k] for g in grads]) for k in grads[0]}


def _pad_cols(w):
    cut = BA_ORIG + 2 * N_HEADS_A
    zeros = jnp.zeros(w.shape[:-1] + (BA_PAD,), w.dtype)
    return jnp.concatenate([w[..., :cut], zeros, w[..., cut:]], axis=-1)


def _unpad_cols(w):
    cut = BA_ORIG + 2 * N_HEADS_A
    return jnp.concatenate([w[..., :cut], w[..., cut + BA_PAD:]], axis=-1)


def kernel(x, norm_w, w_in, conv_w, a_log, dt_bias, gdn_norm_w, q_norm_w, k_norm_w, w_out, loss_target, m_norm_w, m_w_in, m_conv_w, m_a_log, m_dt_bias, m_gdn_norm_w, m_q_norm_w, m_k_norm_w, m_w_out, v_norm_w, v_w_in, v_conv_w, v_a_log, v_dt_bias, v_gdn_norm_w, v_q_norm_w, v_k_norm_w, v_w_out):
    weights = dict(norm_w=norm_w, w_in=w_in, conv_w=conv_w, a_log=a_log, dt_bias=dt_bias, gdn_norm_w=gdn_norm_w,
                   q_norm_w=q_norm_w, k_norm_w=k_norm_w, w_out=w_out)
    moms = dict(norm_w=m_norm_w, w_in=m_w_in, conv_w=m_conv_w, a_log=m_a_log, dt_bias=m_dt_bias,
                gdn_norm_w=m_gdn_norm_w, q_norm_w=m_q_norm_w, k_norm_w=m_k_norm_w, w_out=m_w_out)
    vars_ = dict(norm_w=v_norm_w, w_in=v_w_in, conv_w=v_conv_w, a_log=v_a_log, dt_bias=v_dt_bias,
                 gdn_norm_w=v_gdn_norm_w, q_norm_w=v_q_norm_w, k_norm_w=v_k_norm_w, w_out=v_w_out)
    names = list(weights)
    sharded = ("w_in", "w_out", "conv_w")
    shard_shapes = [weights[k].shape for k in sharded]

    conv_bits = lax.bitcast_convert_type(conv_w, BF16)
    got = plane_exchange(_pack([w_in.astype(BF16), w_out.astype(BF16), conv_bits]), all_to_all=False)
    per_chip = [_unpack(got[i], shard_shapes[:2] + [conv_bits.shape]) for i in range(N_CHIPS)]
    full = {k: weights[k] for k in SMALL}
    full["w_in"] = _pad_cols(jnp.concatenate([pc[0] for pc in per_chip], axis=2))
    full["w_out"] = jnp.concatenate([pc[1] for pc in per_chip], axis=1)
    full["conv_w"] = jnp.concatenate([lax.bitcast_convert_type(pc[2], F32) for pc in per_chip], axis=2)

    loss, grad_x, g = _local_step(x, loss_target, full)

    gw_in = _unpad_cols(g["w_in"])
    cols, rows = w_in.shape[2], w_out.shape[1]
    small = [g[k] for k in SMALL]
    send = jnp.stack([_pack([gw_in[:, :, i * cols:(i + 1) * cols], g["w_out"][:, i * rows:(i + 1) * rows],
                             g["conv_w"][:, :, i * conv_w.shape[2]:(i + 1) * conv_w.shape[2]]] + small)
                      for i in range(N_CHIPS)])
    c = lax.axis_index("c")
    half = PACK_ROWS // 2
    keep = lax.dynamic_slice_in_dim(send, c * half, half, axis=1)
    give = lax.dynamic_slice_in_dim(send, (1 - c) * half, half, axis=1)
    chip_sum = add2(keep.reshape(N_CHIPS * half, PACK_COLS), sibling_swap(give).reshape(N_CHIPS * half, PACK_COLS))
    mine = sum4(plane_exchange(chip_sum.reshape(N_CHIPS, half, PACK_COLS), all_to_all=True))
    other = sibling_swap(mine)
    total = jnp.concatenate([jnp.where(c == 0, mine, other), jnp.where(c == 0, other, mine)])
    reduced = _unpack(total, shard_shapes + [weights[k].shape for k in SMALL])
    grads = dict(zip(sharded + SMALL, reduced))
    loss = lax.psum(loss, ("x", "y", "c"))

    def two_d(a):
        return a.reshape(-1, a.shape[-1])

    delta, new_m, new_v = {}, {}, {}
    for k in names:
        d_, m_, v_ = adamw(two_d(weights[k]), two_d(grads[k]), two_d(moms[k]), two_d(vars_[k]))
        delta[k], new_m[k], new_v[k] = (a.reshape(weights[k].shape) for a in (d_, m_, v_))
    return (loss, grad_x, *[grads[k] for k in names], *[delta[k] for k in names],
            *[new_m[k] for k in names], *[new_v[k] for k in names])
```

```python
import functools
import math

import jax
import jax.numpy as jnp
from jax import lax
from jax.experimental import pallas as pl
from jax.experimental.pallas import tpu as pltpu

F32 = jnp.float32
BF16 = jnp.bfloat16

D_MODEL = 1024
SEQ = 2048
DEPTH = 2
HEAD_DIM = 64
N_HEADS_A, N_HEADS_B, N_HEADS_C = 6, 4, 6
A_WIDTH, B_WIDTH, C_WIDTH = N_HEADS_A * HEAD_DIM, N_HEADS_B * HEAD_DIM, N_HEADS_C * HEAD_DIM
CONV_WIDTH = 4
GDN_CHUNK = 64
BLOCK = 128
ROPE_DIM = 16
ROPE_THETA = 500000.0
DILATED_PAIRS = ((128, 1), (512, 4), (2048, 16))
RMS_EPS = 1e-6
NEG = -1e30

NT = (((1,), (1,)), ((), ()))
NN = (((1,), (0,)), ((), ()))
TN = (((0,), (0,)), ((), ()))

VMEM_LIMIT = 48 * 1024 * 1024

ORIG_A = 4 * A_WIDTH
ORIG_BA = 2 * N_HEADS_A
ORIG_B = 4 * B_WIDTH
COL_AZ = 3 * A_WIDTH
COL_C = 4 * A_WIDTH
COL_B = COL_C + 4 * C_WIDTH
COL_BA = COL_B + 4 * B_WIDTH
P_COLS = COL_BA + BLOCK
TN_COLS = 384
TM_ROWS = 512
ROWS = 256


def _mm(a, b, dims=NN):
    return lax.dot_general(a.astype(BF16), b.astype(BF16), dims, preferred_element_type=F32)


def _mm32(a, b, dims=NN):
    return lax.dot_general(a, b, dims, precision=lax.Precision.HIGH, preferred_element_type=F32)


def _cparams(*sem):
    return pltpu.CompilerParams(dimension_semantics=sem, vmem_limit_bytes=VMEM_LIMIT)


def _sigmoid(x):
    return 0.5 * (jnp.tanh(0.5 * x) + 1.0)


def _softplus(x):
    return jnp.maximum(x, 0.0) + jnp.log(1.0 + jnp.exp(-jnp.abs(x)))


def _rms(x, w):
    return x * lax.rsqrt(jnp.mean(x * x, axis=-1, keepdims=True) + RMS_EPS) * w


def _heads(a, n):
    return jnp.stack([a[:, h * HEAD_DIM:(h + 1) * HEAD_DIM] for h in range(n)])


def _unheads(a):
    return jnp.concatenate([a[h] for h in range(a.shape[0])], axis=1)


def _row_chunks(t):
    return [pl.ds(c * ROWS, ROWS) for c in range(t // ROWS)]


def inproj_fwd(x, nw, w):
    n, d = x.shape
    p = w.shape[1]

    def body(x_ref, nw_ref, w_ref, proj_ref, hdn_ref):
        @pl.when(pl.program_id(1) == 0)
        def _():
            hdn_ref[...] = _rms(x_ref[...], nw_ref[...]).astype(BF16)

        proj_ref[...] = jnp.dot(hdn_ref[...], w_ref[...], preferred_element_type=F32)

    return pl.pallas_call(
        body, name="inproj_fwd", grid=(n // TM_ROWS, p // TN_COLS),
        in_specs=[pl.BlockSpec((TM_ROWS, d), lambda i, j: (i, 0)), pl.BlockSpec((1, d), lambda i, j: (0, 0)),
                  pl.BlockSpec((d, TN_COLS), lambda i, j: (0, j))],
        out_specs=[pl.BlockSpec((TM_ROWS, TN_COLS), lambda i, j: (i, j)), pl.BlockSpec((TM_ROWS, d), lambda i, j: (i, 0))],
        out_shape=[jax.ShapeDtypeStruct((n, p), F32), jax.ShapeDtypeStruct((n, d), BF16)],
        compiler_params=_cparams("parallel", "arbitrary"),
    )(x, nw, w)


def mat_tn(a, slabs):
    n, ka = a.shape
    ns = len(slabs)

    def body(*refs):
        a_ref, s_refs, o_refs = refs[0], refs[1:1 + ns], refs[1 + ns:]

        @pl.when(pl.program_id(0) == 0)
        def _():
            for o_ref in o_refs:
                o_ref[...] = jnp.zeros_like(o_ref)

        av = a_ref[...]
        for s_ref, o_ref in zip(s_refs, o_refs):
            o_ref[...] += lax.dot_general(av, s_ref[...].astype(BF16), TN, preferred_element_type=F32)

    return pl.pallas_call(
        body, name="mat_tn", grid=(n // TM_ROWS,),
        in_specs=[pl.BlockSpec((TM_ROWS, ka), lambda k: (k, 0))]
                 + [pl.BlockSpec((TM_ROWS, s.shape[1]), lambda k: (k, 0)) for s in slabs],
        out_specs=[pl.BlockSpec((ka, s.shape[1]), lambda k: (0, 0)) for s in slabs],
        out_shape=[jax.ShapeDtypeStruct((ka, s.shape[1]), F32) for s in slabs],
        compiler_params=_cparams("arbitrary"),
    )(a, *slabs)


def inproj_bwd(slabs, w, x, nw, dy):
    n, d = x.shape
    p = w.shape[1]
    tm = 256
    ns = len(slabs)

    def body(*refs):
        s_refs = refs[:ns]
        w_ref, x_ref, nw_ref, dy_ref, dx_ref, dnw_ref = refs[ns:]
        dh = jnp.zeros((tm, d), F32)
        at = 0
        for s_ref in s_refs:
            wd = s_ref.shape[1]
            dh = dh + lax.dot_general(s_ref[...].astype(BF16), w_ref[:, at:at + wd], NT, preferred_element_type=F32)
            at += wd
        _, vjp = jax.vjp(_rms, x_ref[...], nw_ref[...])
        dx, dnw = vjp(dh)
        dx_ref[...] = dx + dy_ref[...]
        dnw_ref[0] = dnw

    return pl.pallas_call(
        body, name="inproj_bwd", grid=(n // tm,),
        in_specs=[pl.BlockSpec((tm, s.shape[1]), lambda i: (i, 0)) for s in slabs]
                 + [pl.BlockSpec((d, p), lambda i: (0, 0)), pl.BlockSpec((tm, d), lambda i: (i, 0)),
                    pl.BlockSpec((1, d), lambda i: (0, 0)), pl.BlockSpec((tm, d), lambda i: (i, 0))],
        out_specs=[pl.BlockSpec((tm, d), lambda i: (i, 0)), pl.BlockSpec((1, 1, d), lambda i: (i, 0, 0))],
        out_shape=[jax.ShapeDtypeStruct((n, d), F32), jax.ShapeDtypeStruct((n // tm, 1, d), F32)],
        compiler_params=_cparams("parallel"),
    )(*slabs, w, x, nw, dy)


CONV_PAD = 8
CONV_ROWS = 256


def _conv_pre(pad_s, cw, c):
    xs = [pad_s[pl.ds(c * CONV_ROWS + CONV_PAD - (CONV_WIDTH - 1) + k, CONV_ROWS), :] for k in range(CONV_WIDTH)]
    pre = xs[0] * cw[0:1, :]
    for k in range(1, CONV_WIDTH):
        pre = pre + xs[k] * cw[k:k + 1, :]
    return pre, xs


def conv_fwd(proj, cw, nseq):
    n = proj.shape[0]
    t = n // nseq
    ch = cw.shape[1]

    def body(x_ref, cw_ref, y_ref, pad_s):
        pad_s[pl.ds(0, CONV_PAD), :] = jnp.zeros((CONV_PAD, TN_COLS), F32)
        pad_s[pl.ds(CONV_PAD, t), :] = x_ref[...]
        cwv = cw_ref[...]
        for c in range(t // CONV_ROWS):
            pre, _ = _conv_pre(pad_s, cwv, c)
            y_ref[pl.ds(c * CONV_ROWS, CONV_ROWS), :] = pre * _sigmoid(pre)

    return pl.pallas_call(
        body, name="conv_fwd", grid=(nseq, ch // TN_COLS),
        in_specs=[pl.BlockSpec((t, TN_COLS), lambda b, j: (b, j)), pl.BlockSpec((CONV_WIDTH, TN_COLS), lambda b, j: (0, j))],
        out_specs=pl.BlockSpec((t, TN_COLS), lambda b, j: (b, j)),
        out_shape=jax.ShapeDtypeStruct((n, ch), F32),
        scratch_shapes=[pltpu.VMEM((t + CONV_PAD, TN_COLS), F32)],
        compiler_params=_cparams("parallel", "parallel"),
    )(proj, cw)


def conv_bwd(proj, cw, dy, nseq):
    n = proj.shape[0]
    t = n // nseq
    ch = cw.shape[1]

    def body(x_ref, cw_ref, dy_ref, dx_ref, dcw_ref, pad_s, dpad_s):
        pad_s[pl.ds(0, CONV_PAD), :] = jnp.zeros((CONV_PAD, TN_COLS), F32)
        pad_s[pl.ds(CONV_PAD, t), :] = x_ref[...]
        dpad_s[pl.ds(t, CONV_PAD), :] = jnp.zeros((CONV_PAD, TN_COLS), F32)
        cwv = cw_ref[...]
        acc = [jnp.zeros((1, TN_COLS), F32)] * CONV_WIDTH
        for c in range(t // CONV_ROWS):
            pre, xs = _conv_pre(pad_s, cwv, c)
            sg = _sigmoid(pre)
            dpre = dy_ref[pl.ds(c * CONV_ROWS, CONV_ROWS), :] * (sg * (1.0 + pre * (1.0 - sg)))
            dpad_s[pl.ds(c * CONV_ROWS, CONV_ROWS), :] = dpre
            acc = [acc[k] + jnp.sum(dpre * xs[k], axis=0, keepdims=True) for k in range(CONV_WIDTH)]
        for k in range(CONV_WIDTH):
            dcw_ref[0, pl.ds(k, 1), :] = acc[k]
        for c in range(t // CONV_ROWS):
            dx = dpad_s[pl.ds(c * CONV_ROWS + CONV_WIDTH - 1, CONV_ROWS), :] * cwv[0:1, :]
            for k in range(1, CONV_WIDTH):
                dx = dx + dpad_s[pl.ds(c * CONV_ROWS + CONV_WIDTH - 1 - k, CONV_ROWS), :] * cwv[k:k + 1, :]
            dx_ref[pl.ds(c * CONV_ROWS, CONV_ROWS), :] = dx

    blk = pl.BlockSpec((t, TN_COLS), lambda b, j: (b, j))
    return pl.pallas_call(
        body, name="conv_bwd", grid=(nseq, ch // TN_COLS),
        in_specs=[blk, pl.BlockSpec((CONV_WIDTH, TN_COLS), lambda b, j: (0, j)), blk],
        out_specs=[blk, pl.BlockSpec((1, CONV_WIDTH, TN_COLS), lambda b, j: (b, 0, j))],
        out_shape=[jax.ShapeDtypeStruct((n, ch), F32), jax.ShapeDtypeStruct((nseq, CONV_WIDTH, ch), F32)],
        scratch_shapes=[pltpu.VMEM((t + CONV_PAD, TN_COLS), F32)] * 2,
        compiler_params=_cparams("parallel", "parallel"),
    )(proj, cw, dy)


def _gate_specs(d):
    wide = pl.BlockSpec((TM_ROWS, d), lambda i: (i, 0))
    oa = pl.BlockSpec((TM_ROWS, A_WIDTH), lambda i: (i, 0))
    ob = pl.BlockSpec((TM_ROWS, B_WIDTH), lambda i: (i, 0))
    oc = pl.BlockSpec((TM_ROWS, C_WIDTH), lambda i: (i, 0))
    za = pl.BlockSpec((TM_ROWS, A_WIDTH), lambda i: (i, COL_AZ // A_WIDTH))
    zb = pl.BlockSpec((TM_ROWS, B_WIDTH), lambda i: (i, (COL_B + 3 * B_WIDTH) // B_WIDTH))
    zc = pl.BlockSpec((TM_ROWS, C_WIDTH), lambda i: (i, (COL_C + 3 * C_WIDTH) // C_WIDTH))
    return wide, oa, ob, oc, za, zb, zc


BRANCH_COLS = ((0, A_WIDTH), (A_WIDTH, A_WIDTH + B_WIDTH), (A_WIDTH + B_WIDTH, D_MODEL))


def outproj_fwd(x, oa, ob, oc, proj, w):
    n, d = x.shape

    def body(x_ref, oa_ref, ob_ref, oc_ref, za_ref, zb_ref, zc_ref, w_ref, y_ref, m_ref):
        for (lo, hi), o_ref, z_ref in zip(BRANCH_COLS, (oa_ref, ob_ref, oc_ref), (za_ref, zb_ref, zc_ref)):
            zv = z_ref[...]
            m_ref[:, lo:hi] = (o_ref[...] * (zv * _sigmoid(zv))).astype(BF16)
        y_ref[...] = x_ref[...] + jnp.dot(m_ref[...], w_ref[...], preferred_element_type=F32)

    wide, sa, sb, sc, za, zb, zc = _gate_specs(d)
    return pl.pallas_call(
        body, name="outproj_fwd", grid=(n // TM_ROWS,),
        in_specs=[wide, sa, sb, sc, za, zb, zc, pl.BlockSpec((d, d), lambda i: (0, 0))],
        out_specs=[wide, wide],
        out_shape=[jax.ShapeDtypeStruct((n, d), F32), jax.ShapeDtypeStruct((n, d), BF16)],
        compiler_params=_cparams("parallel"),
    )(x, oa, ob, oc, proj, proj, proj, w)


def outproj_bwd(dy, oa, ob, oc, proj, w):
    n, d = dy.shape

    def body(dy_ref, oa_ref, ob_ref, oc_ref, za_ref, zb_ref, zc_ref, w_ref, doa_ref, dob_ref, doc_ref, dza_ref, dzb_ref, dzc_ref):
        dm = lax.dot_general(dy_ref[...].astype(BF16), w_ref[...], NT, preferred_element_type=F32)
        for (lo, hi), o_ref, z_ref, do_ref, dz_ref in zip(BRANCH_COLS, (oa_ref, ob_ref, oc_ref), (za_ref, zb_ref, zc_ref),
                                                          (doa_ref, dob_ref, doc_ref), (dza_ref, dzb_ref, dzc_ref)):
            zv = z_ref[...]
            sg = _sigmoid(zv)
            dmv = dm[:, lo:hi]
            do_ref[...] = dmv * (zv * sg)
            dz_ref[...] = dmv * o_ref[...] * (sg * (1.0 + zv * (1.0 - sg)))

    wide, sa, sb, sc, za, zb, zc = _gate_specs(d)
    sd = jax.ShapeDtypeStruct
    outs = [sd((n, A_WIDTH), F32), sd((n, B_WIDTH), F32), sd((n, C_WIDTH), F32)]
    return pl.pallas_call(
        body, name="outproj_bwd", grid=(n // TM_ROWS,),
        in_specs=[wide, sa, sb, sc, za, zb, zc, pl.BlockSpec((d, d), lambda i: (0, 0))],
        out_specs=[sa, sb, sc, sa, sb, sc],
        out_shape=outs + outs,
        compiler_params=_cparams("parallel"),
    )(dy, oa, ob, oc, proj, proj, proj, w)


def loss_fwd_bwd(y, target):
    n, d = y.shape

    def body(y_ref, t_ref, dy_ref, part_ref):
        e = y_ref[...] - t_ref[...]
        dy_ref[...] = e * (1.0 / d)
        part_ref[...] = jnp.zeros_like(part_ref) + 0.5 * jnp.sum(e * e) * (1.0 / d)

    blk = pl.BlockSpec((TM_ROWS, d), lambda i: (i, 0))
    return pl.pallas_call(
        body, name="loss", grid=(n // TM_ROWS,),
        in_specs=[blk, blk],
        out_specs=[blk, pl.BlockSpec((1, 8, BLOCK), lambda i: (i, 0, 0))],
        out_shape=[jax.ShapeDtypeStruct((n, d), F32), jax.ShapeDtypeStruct((n // TM_ROWS, 8, BLOCK), F32)],
        compiler_params=_cparams("parallel"),
    )(y, target)


ADAM_LR, ADAM_B1, ADAM_B2, ADAM_EPS, ADAM_WD, ADAM_STEP = 0.001, 0.9, 0.999, 1e-08, 0.01, 10


def adamw(w, g, m, v):
    r, c = w.shape
    tr = r if r <= 256 else 256

    def body(w_ref, g_ref, m_ref, v_ref, d_ref, nm_ref, nv_ref):
        gv = g_ref[...]
        nm = ADAM_B1 * m_ref[...] + (1.0 - ADAM_B1) * gv
        nv = ADAM_B2 * v_ref[...] + (1.0 - ADAM_B2) * (gv * gv)
        m_hat = nm / (1.0 - ADAM_B1 ** ADAM_STEP)
        v_hat = nv / (1.0 - ADAM_B2 ** ADAM_STEP)
        d_ref[...] = -ADAM_LR * (m_hat / (jnp.sqrt(v_hat) + ADAM_EPS) + ADAM_WD * w_ref[...])
        nm_ref[...] = nm
        nv_ref[...] = nv

    blk = pl.BlockSpec((tr, c), lambda i: (i, 0))
    return pl.pallas_call(
        body, name="adamw", grid=(r // tr,),
        in_specs=[blk] * 4, out_specs=[blk] * 3,
        out_shape=[jax.ShapeDtypeStruct((r, c), F32)] * 3,
        compiler_params=_cparams("parallel"),
    )(w, g, m, v)


SB_GF = N_HEADS_B
SB_G = 2


def _sb_tile(q, k, v, carry, qpos, kpos, tri):
    z = _mm(q * (HEAD_DIM ** -0.5), k, NT)
    earlier = kpos < qpos
    sp = jnp.log(1.0 + jnp.exp(-jnp.abs(z)))
    ls_pos = jnp.minimum(z, 0.0) - sp
    ls_neg = jnp.minimum(-z, 0.0) - sp
    log_keep = jnp.where(earlier, ls_neg, 0.0)
    within = _mm32(log_keep, tri)
    wts = jnp.where(earlier, jnp.exp(jnp.where(earlier, ls_pos + within + carry, 0.0)), 0.0)
    return _mm(wts, v), jnp.sum(log_keep, axis=1, keepdims=True)


_sb_tiles = jax.vmap(_sb_tile, in_axes=(0, 0, 0, 0, None, None, None))


def _sb_consts():
    qi = lax.broadcasted_iota(jnp.int32, (BLOCK, 1), 0)
    kj = lax.broadcasted_iota(jnp.int32, (1, BLOCK), 1)
    r = lax.broadcasted_iota(jnp.int32, (BLOCK, BLOCK), 0)
    c = lax.broadcasted_iota(jnp.int32, (BLOCK, BLOCK), 1)
    tri = jnp.where(r > c, 1.0, 0.0).astype(F32)
    return qi, kj, tri


def _split_heads_into(src_ref, dst_s, g):
    for rows in _row_chunks(src_ref.shape[0]):
        blk = src_ref[rows, :]
        for h in range(g):
            dst_s[h, rows, :] = blk[:, h * HEAD_DIM:(h + 1) * HEAD_DIM]


def sb_fwd(proj, nseq):
    n = proj.shape[0]
    t = n // nseq
    nq = t // BLOCK
    g = SB_GF
    wd = g * HEAD_DIM

    def body(q_ref, k_ref, v_ref, o_ref, carry_ref, ks_s, vs_s):
        i = pl.program_id(1)

        @pl.when(i == 0)
        def _():
            _split_heads_into(k_ref, ks_s, g)
            _split_heads_into(v_ref, vs_s, g)

        qi, kj, tri = _sb_consts()
        qv = _heads(q_ref[...], g)
        qpos = i * BLOCK + qi

        def step(it, st):
            o_acc, c = st
            j = i - it
            rows = pl.ds(pl.multiple_of(j * BLOCK, BLOCK), BLOCK)
            carry_ref[0, 0, j] = c
            o, tot = _sb_tiles(qv, ks_s[:, rows, :], vs_s[:, rows, :], c, qpos, j * BLOCK + kj, tri)
            return o_acc + o, c + tot

        o_acc, _ = lax.fori_loop(0, i + 1, step, (jnp.zeros((g, BLOCK, HEAD_DIM), F32), jnp.zeros((g, BLOCK, 1), F32)))
        o_ref[...] = _unheads(o_acc)

    cb = COL_B // wd
    return pl.pallas_call(
        body, name="sb_fwd", grid=(nseq, nq),
        in_specs=[pl.BlockSpec((BLOCK, wd), lambda b, i: (b * nq + i, cb)),
                  pl.BlockSpec((t, wd), lambda b, i: (b, cb + 1)),
                  pl.BlockSpec((t, wd), lambda b, i: (b, cb + 2))],
        out_specs=[pl.BlockSpec((BLOCK, wd), lambda b, i: (b * nq + i, 0)),
                   pl.BlockSpec((1, 1, nq, g, BLOCK, 1), lambda b, i: (b, i, 0, 0, 0, 0))],
        out_shape=[jax.ShapeDtypeStruct((n, B_WIDTH), F32),
                   jax.ShapeDtypeStruct((nseq, nq, nq, N_HEADS_B, BLOCK, 1), F32)],
        scratch_shapes=[pltpu.VMEM((g, t, HEAD_DIM), F32)] * 2,
        compiler_params=_cparams("parallel", "arbitrary"),
    )(proj, proj, proj)


def sb_bwd(proj, carries, do, nseq):
    n = proj.shape[0]
    t = n // nseq
    nq = t // BLOCK
    g = SB_G
    wd = g * HEAD_DIM
    ng = N_HEADS_B // g

    def body(q_ref, k_ref, v_ref, carry_ref, do_ref, dq_ref, dk_ref, dv_ref, ks_s, vs_s, dks_s, dvs_s):
        i = pl.program_id(2)

        @pl.when(i == 0)
        def _():
            _split_heads_into(k_ref, ks_s, g)
            _split_heads_into(v_ref, vs_s, g)
            dks_s[...] = jnp.zeros_like(dks_s)
            dvs_s[...] = jnp.zeros_like(dvs_s)

        qi, kj, tri = _sb_consts()
        qv = _heads(q_ref[...], g)
        dov = _heads(do_ref[...], g)
        qpos = i * BLOCK + qi

        def step(j, st):
            dq_acc, dc = st
            rows = pl.ds(pl.multiple_of(j * BLOCK, BLOCK), BLOCK)
            kpos = j * BLOCK + kj
            f = lambda q_, k_, v_, c_: _sb_tiles(q_, k_, v_, c_, qpos, kpos, tri)
            _, vjp = jax.vjp(f, qv, ks_s[:, rows, :], vs_s[:, rows, :], carry_ref[0, 0, j])
            dq, dk, dv, dcj = vjp((dov, dc))
            dks_s[:, rows, :] += dk
            dvs_s[:, rows, :] += dv
            return dq_acc + dq, dc + dcj

        dq_acc, _ = lax.fori_loop(0, i + 1, step, (jnp.zeros((g, BLOCK, HEAD_DIM), F32), jnp.zeros((g, BLOCK, 1), F32)))
        dq_ref[...] = _unheads(dq_acc)

        @pl.when(i == nq - 1)
        def _():
            for rows in _row_chunks(t):
                dk_ref[rows, :] = _unheads(dks_s[:, rows, :])
                dv_ref[rows, :] = _unheads(dvs_s[:, rows, :])

    cb = COL_B // wd
    per = B_WIDTH // wd
    full = pl.BlockSpec((t, wd), lambda b, p, i: (b, p))
    blk = pl.BlockSpec((BLOCK, wd), lambda b, p, i: (b * nq + i, p))
    return pl.pallas_call(
        body, name="sb_bwd", grid=(nseq, ng, nq),
        in_specs=[pl.BlockSpec((BLOCK, wd), lambda b, p, i: (b * nq + i, cb + p)),
                  pl.BlockSpec((t, wd), lambda b, p, i: (b, cb + per + p)),
                  pl.BlockSpec((t, wd), lambda b, p, i: (b, cb + 2 * per + p)),
                  pl.BlockSpec((1, 1, nq, g, BLOCK, 1), lambda b, p, i: (b, i, 0, p, 0, 0)), blk],
        out_specs=[blk, full, full],
        out_shape=[jax.ShapeDtypeStruct((n, B_WIDTH), F32)] * 3,
        scratch_shapes=[pltpu.VMEM((g, t, HEAD_DIM), F32)] * 4,
        compiler_params=_cparams("parallel", "parallel", "arbitrary"),
    )(proj, proj, proj, carries, do)


def _gdn_chunk(q, k, v, al_c, al_r, br_c, alog, dtb, nw, s):
    c = GDN_CHUNK
    ri = lax.broadcasted_iota(jnp.int32, (c, c), 0)
    ci = lax.broadcasted_iota(jnp.int32, (c, c), 1)
    incl, strict = ri >= ci, ri > ci
    eye = jnp.where(ri == ci, 1.0, 0.0).astype(F32)
    rate = -jnp.exp(alog)
    g_c = rate * _softplus(al_c + dtb)
    g_r = rate * _softplus(al_r + dtb)
    beta = _sigmoid(br_c)
    gc_c = jnp.sum(jnp.where(incl, g_r, 0.0), axis=1, keepdims=True)
    gc_r = jnp.sum(jnp.where(ri <= ci, g_c, 0.0), axis=0, keepdims=True)
    gl = jnp.sum(g_r, axis=1, keepdims=True)
    decay = jnp.where(incl, jnp.exp(jnp.where(incl, gc_c - gc_r, 0.0)), 0.0)
    qn = q * lax.rsqrt(jnp.sum(q * q, axis=-1, keepdims=True) + RMS_EPS) * (HEAD_DIM ** -0.5)
    kn = k * lax.rsqrt(jnp.sum(k * k, axis=-1, keepdims=True) + RMS_EPS)
    kb = kn * beta
    a = jnp.where(strict, _mm(kb, kn, NT) * decay, 0.0)
    tmat = eye - a
    p = a
    for _ in range(5):
        p = _mm32(p, p)
        tmat = tmat + _mm32(tmat, p)
    u = _mm(tmat, v * beta)
    w = _mm(tmat, kb * jnp.exp(gc_c))
    qk = _mm(qn, kn, NT) * decay
    v_new = u - _mm(w, s)
    o = _mm(qn * jnp.exp(gc_c), s) + _mm(qk, v_new)
    s_new = s * jnp.exp(gl) + _mm(kn * jnp.exp(gl - gc_c), v_new, TN)
    o = o * lax.rsqrt(jnp.mean(o * o, axis=-1, keepdims=True) + RMS_EPS) * nw
    return o, s_new


_gdn_chunks = jax.vmap(_gdn_chunk, in_axes=(0, 0, 0, 0, 0, 0, 0, 0, None, 0))

GDN_TB = 256


def _gdn_block(q3, k3, v3, ba, alog, dtb, nw, s):
    nh = N_HEADS_A
    bat = ba.T
    br_c = jnp.stack([ba[:, h:h + 1] for h in range(nh)])
    al_c = jnp.stack([ba[:, nh + h:nh + h + 1] for h in range(nh)])
    al_r = jnp.stack([bat[nh + h:nh + h + 1, :] for h in range(nh)])
    o, s_new = _gdn_chunks(_heads(q3, nh), _heads(k3, nh), _heads(v3, nh), al_c, al_r, br_c, alog, dtb, nw, s)
    return _unheads(o), s_new


def _gdn_specs(nt, rev):
    tpos = (lambda i: nt - 1 - i) if rev else (lambda i: i)
    ncb = GDN_TB // GDN_CHUNK
    qkv = [pl.BlockSpec((GDN_TB, A_WIDTH), lambda b, i, j=j: (b * nt + tpos(i), j)) for j in range(3)]
    ba = pl.BlockSpec((GDN_TB, BLOCK), lambda b, i: (b * nt + tpos(i), COL_BA // BLOCK))
    one = pl.BlockSpec((N_HEADS_A, 1, 1), lambda b, i: (0, 0, 0))
    vec = pl.BlockSpec((1, HEAD_DIM), lambda b, i: (0, 0))
    st = pl.BlockSpec((1, ncb, N_HEADS_A, HEAD_DIM, HEAD_DIM), lambda b, i: (b, tpos(i), 0, 0, 0))
    oa = pl.BlockSpec((GDN_TB, A_WIDTH), lambda b, i: (b * nt + tpos(i), 0))
    return qkv, ba, one, vec, st, oa, tpos


def gdn_fwd(ya, proj, alog, dtb, nw, nseq):
    n = ya.shape[0]
    t = n // nseq
    nc, nt, ncb = t // GDN_CHUNK, t // GDN_TB, GDN_TB // GDN_CHUNK

    def body(q_ref, k_ref, v_ref, ba_ref, alog_ref, dtb_ref, nw_ref, o_ref, st_ref, s_s):
        @pl.when(pl.program_id(1) == 0)
        def _():
            s_s[...] = jnp.zeros_like(s_s)

        def step(c, s):
            rows = pl.ds(pl.multiple_of(c * GDN_CHUNK, GDN_CHUNK), GDN_CHUNK)
            st_ref[0, c] = s
            o, s_new = _gdn_block(q_ref[rows, :], k_ref[rows, :], v_ref[rows, :], ba_ref[rows, :],
                                  alog_ref[...], dtb_ref[...], nw_ref[...], s)
            o_ref[rows, :] = o
            return s_new

        s_s[...] = lax.fori_loop(0, ncb, step, s_s[...])

    qkv, ba, one, vec, st, oa, _ = _gdn_specs(nt, False)
    return pl.pallas_call(
        body, name="gdn_fwd", grid=(nseq, nt),
        in_specs=qkv + [ba, one, one, vec],
        out_specs=[oa, st],
        out_shape=[jax.ShapeDtypeStruct((n, A_WIDTH), F32),
                   jax.ShapeDtypeStruct((nseq, nc, N_HEADS_A, HEAD_DIM, HEAD_DIM), F32)],
        scratch_shapes=[pltpu.VMEM((N_HEADS_A, HEAD_DIM, HEAD_DIM), F32)],
        compiler_params=_cparams("parallel", "arbitrary"),
    )(ya, ya, ya, proj, alog, dtb, nw)


def gdn_bwd(ya, proj, alog, dtb, nw, states, do, nseq):
    n = ya.shape[0]
    t = n // nseq
    nt, ncb = t // GDN_TB, GDN_TB // GDN_CHUNK
    nh = N_HEADS_A

    def body(q_ref, k_ref, v_ref, ba_ref, alog_ref, dtb_ref, nw_ref, st_ref, do_ref,
             dya_ref, dba_ref, dalog_ref, ddtb_ref, dnw_ref, ds_s):
        @pl.when(pl.program_id(1) == 0)
        def _():
            ds_s[...] = jnp.zeros_like(ds_s)
            dalog_ref[...] = jnp.zeros_like(dalog_ref)
            ddtb_ref[...] = jnp.zeros_like(ddtb_ref)
            dnw_ref[...] = jnp.zeros_like(dnw_ref)

        def step(it, carry):
            ds, dalog, ddtb, dnw = carry
            c = ncb - 1 - it
            rows = pl.ds(pl.multiple_of(c * GDN_CHUNK, GDN_CHUNK), GDN_CHUNK)
            _, vjp = jax.vjp(_gdn_block, q_ref[rows, :], k_ref[rows, :], v_ref[rows, :], ba_ref[rows, :],
                             alog_ref[...], dtb_ref[...], nw_ref[...], st_ref[0, c])
            dq, dk, dv, dba, da, dd, dn, ds = vjp((do_ref[rows, :], ds))
            dya_ref[rows, 0:A_WIDTH] = dq
            dya_ref[rows, A_WIDTH:2 * A_WIDTH] = dk
            dya_ref[rows, 2 * A_WIDTH:3 * A_WIDTH] = dv
            dba_ref[rows, :] = dba
            return ds, dalog + da, ddtb + dd, dnw + dn

        z11 = jnp.zeros((nh, 1, 1), F32)
        ds, dalog, ddtb, dnw = lax.fori_loop(0, ncb, step, (ds_s[...], z11, z11, jnp.zeros((1, HEAD_DIM), F32)))
        ds_s[...] = ds
        dalog_ref[0] += dalog
        ddtb_ref[0] += ddtb
        dnw_ref[0] += dnw

    qkv, ba, one, vec, st, oa, tpos = _gdn_specs(nt, True)
    per_seq = pl.BlockSpec((1, nh, 1, 1), lambda b, i: (b, 0, 0, 0))
    sd = jax.ShapeDtypeStruct
    return pl.pallas_call(
        body, name="gdn_bwd", grid=(nseq, nt),
        in_specs=qkv + [ba, one, one, vec, st, oa],
        out_specs=[pl.BlockSpec((GDN_TB, 3 * A_WIDTH), lambda b, i: (b * nt + tpos(i), 0)),
                   pl.BlockSpec((GDN_TB, BLOCK), lambda b, i: (b * nt + tpos(i), 0)),
                   per_seq, per_seq, pl.BlockSpec((1, 1, HEAD_DIM), lambda b, i: (b, 0, 0))],
        out_shape=[sd((n, 3 * A_WIDTH), F32), sd((n, BLOCK), F32), sd((nseq, nh, 1, 1), F32), sd((nseq, nh, 1, 1), F32),
                   sd((nseq, 1, HEAD_DIM), F32)],
        scratch_shapes=[pltpu.VMEM((nh, HEAD_DIM, HEAD_DIM), F32)],
        compiler_params=_cparams("parallel", "arbitrary"),
    )(ya, ya, ya, proj, alog, dtb, nw, states, do)


DIL_NB = tuple((SEQ // d) // BLOCK for _, d in DILATED_PAIRS)
DIL_D = tuple(d for _, d in DILATED_PAIRS)
DIL_STEPS = tuple(w // d for w, d in DILATED_PAIRS)
DIL_B = 4


def _rope_tables(t):
    half = ROPE_DIM // 2
    inv_freq = ROPE_THETA ** (-jnp.arange(half, dtype=F32) / half)
    ang = jnp.arange(t, dtype=F32)[:, None] * inv_freq[None, :]
    ones = jnp.ones((t, HEAD_DIM - ROPE_DIM), F32)
    cs = jnp.concatenate([jnp.cos(ang), jnp.cos(ang), ones], axis=1)
    sn = jnp.concatenate([jnp.sin(ang), jnp.sin(ang), 0.0 * ones], axis=1)
    i = jnp.arange(HEAD_DIM)[:, None]
    j = jnp.arange(HEAD_DIM)[None, :]
    pm = (jnp.where((j < half) & (i == j + half), -1.0, 0.0)
          + jnp.where((j >= half) & (j < ROPE_DIM) & (i == j - half), 1.0, 0.0))
    return cs, sn, pm.astype(F32)


def _dil_prep(x, w, cs, sn, pm):
    y = x * lax.rsqrt(jnp.mean(x * x, axis=-1, keepdims=True) + RMS_EPS) * w
    return y * cs + _mm32(y, pm) * sn


def _dil_tile(qn, kk, vv, mask):
    s = jnp.where(mask, _mm(qn * (HEAD_DIM ** -0.5), kk, NT), NEG)
    m = lax.stop_gradient(jnp.max(s, axis=-1, keepdims=True))
    p = jnp.exp(s - m)
    denom = jnp.sum(p, axis=-1, keepdims=True)
    return _mm(p, vv) / denom, m + jnp.log(denom)


_dil_tiles = jax.vmap(_dil_tile)


def _dil_mix(o1, o2, o3, l1, l2, l3):
    m = lax.stop_gradient(jnp.maximum(jnp.maximum(l1, l2), l3))
    e1, e2, e3 = jnp.exp(l1 - m), jnp.exp(l2 - m), jnp.exp(l3 - m)
    return (e1 * o1 + e2 * o2 + e3 * o3) / (e1 + e2 + e3)


def _dil_mask(it, g):
    nb = DIL_NB[g]
    n = it % nb
    r = it // nb
    kstart = jnp.maximum(it - 1, 0) * BLOCK
    iq = n * BLOCK + lax.broadcasted_iota(jnp.int32, (BLOCK, 1), 0)
    ik = kstart - r * (nb * BLOCK) + lax.broadcasted_iota(jnp.int32, (1, 2 * BLOCK), 1)
    mask = (ik >= 0) & (iq >= ik) & (iq - ik <= DIL_STEPS[g])
    return mask, pl.ds(pl.multiple_of(it * BLOCK, BLOCK), BLOCK), pl.ds(pl.multiple_of(kstart, BLOCK), 2 * BLOCK)


def _dil_gather(src, dst, d):
    t = src.shape[0]
    ln = t // d
    for r in range(d):
        dst[pl.ds(r * ln, ln), :] = src[pl.ds(r, ln, stride=d), :]


def _dil_scatter(src, dst, d):
    t = src.shape[0]
    ln = t // d
    for r in range(d):
        dst[pl.ds(r, ln, stride=d), :] = src[pl.ds(r * ln, ln), :]


def _half(x, h):
    return jnp.where(h == 0, x[:, :HEAD_DIM], x[:, HEAD_DIM:])


def _store_half(ref, rows, val, h):
    @pl.when(h == 0)
    def _():
        ref[rows, 0:HEAD_DIM] = val

    @pl.when(h == 1)
    def _():
        ref[rows, HEAD_DIM:2 * HEAD_DIM] = val


def _dil_forward_parts(h, q_ref, k_ref, v_ref, qw, kw, cs_ref, sn_ref, pm, qn_s, kn_s, v_s, dl_s, od_s, ld_s, on_s, ln_s):
    t = qn_s.shape[0]

    def prep(c, _):
        rows = pl.ds(pl.multiple_of(c * ROWS, ROWS), ROWS)
        qn_s[rows, :] = _dil_prep(_half(q_ref[rows, :], h), qw, cs_ref[rows, :], sn_ref[rows, :], pm)
        kn_s[rows, :] = _dil_prep(_half(k_ref[rows, :], h), kw, cs_ref[rows, :], sn_ref[rows, :], pm)
        v_s[rows, :] = _half(v_ref[rows, :], h)
        return 0

    lax.fori_loop(0, t // ROWS, prep, 0)
    for g in (1, 2):
        _dil_gather(qn_s, dl_s.at[g - 1, 0], DIL_D[g])
        _dil_gather(kn_s, dl_s.at[g - 1, 1], DIL_D[g])
        _dil_gather(v_s, dl_s.at[g - 1, 2], DIL_D[g])
    for g in range(3):
        qs = qn_s if g == 0 else dl_s.at[g - 1, 0]
        ks = kn_s if g == 0 else dl_s.at[g - 1, 1]
        vs = v_s if g == 0 else dl_s.at[g - 1, 2]

        def tiles(i, _, g=g, qs=qs, ks=ks, vs=vs):
            where = [_dil_mask(i * DIL_B + b, g) for b in range(DIL_B)]
            o, lse = _dil_tiles(jnp.stack([qs[qr, :] for _, qr, _ in where]), jnp.stack([ks[kr, :] for _, _, kr in where]),
                                jnp.stack([vs[kr, :] for _, _, kr in where]), jnp.stack([m for m, _, _ in where]))
            for b, (_, qr, _) in enumerate(where):
                od_s[g, qr, :] = o[b]
                ld_s[g, qr, :] = lse[b]
            return 0

        lax.fori_loop(0, t // BLOCK // DIL_B, tiles, 0)
    for g in (1, 2):
        _dil_scatter(od_s.at[g], on_s.at[g - 1], DIL_D[g])
        _dil_scatter(ld_s.at[g], ln_s.at[g - 1], DIL_D[g])


def _dil_scratch(t):
    return [pltpu.VMEM((t, HEAD_DIM), F32), pltpu.VMEM((t, HEAD_DIM), F32), pltpu.VMEM((t, HEAD_DIM), F32),
            pltpu.VMEM((2, 3, t, HEAD_DIM), F32),
            pltpu.VMEM((3, t, HEAD_DIM), F32), pltpu.VMEM((3, t, 1), F32),
            pltpu.VMEM((2, t, HEAD_DIM), F32), pltpu.VMEM((2, t, 1), F32)]


def _dil_specs(t):
    cb = COL_C // BLOCK
    per = C_WIDTH // BLOCK
    qkv = [pl.BlockSpec((t, BLOCK), lambda b, p, h, j=j: (b, cb + j * per + p)) for j in range(3)]
    vec = pl.BlockSpec((1, HEAD_DIM), lambda b, p, h: (0, 0))
    tab = pl.BlockSpec((t, HEAD_DIM), lambda b, p, h: (0, 0))
    mat = pl.BlockSpec((HEAD_DIM, HEAD_DIM), lambda b, p, h: (0, 0))
    pair = pl.BlockSpec((t, BLOCK), lambda b, p, h: (b, p))
    return qkv, vec, tab, mat, pair


def dil_fwd(proj, qw, kw, cs, sn, pm, nseq):
    n = proj.shape[0]
    t = n // nseq

    def body(q_ref, k_ref, v_ref, qw_ref, kw_ref, cs_ref, sn_ref, pm_ref, o_ref, qn_s, kn_s, v_s, dl_s, od_s, ld_s, on_s, ln_s):
        h = pl.program_id(2)
        _dil_forward_parts(h, q_ref, k_ref, v_ref, qw_ref[...], kw_ref[...], cs_ref, sn_ref, pm_ref[...],
                           qn_s, kn_s, v_s, dl_s, od_s, ld_s, on_s, ln_s)

        def mix(c, _):
            rows = pl.ds(pl.multiple_of(c * ROWS, ROWS), ROWS)
            _store_half(o_ref, rows, _dil_mix(od_s[0, rows, :], on_s[0, rows, :], on_s[1, rows, :],
                                              ld_s[0, rows, :], ln_s[0, rows, :], ln_s[1, rows, :]), h)
            return 0

        lax.fori_loop(0, t // ROWS, mix, 0)

    qkv, vec, tab, mat, pair = _dil_specs(t)
    return pl.pallas_call(
        body, name="dil_fwd", grid=(nseq, C_WIDTH // BLOCK, 2),
        in_specs=qkv + [vec, vec, tab, tab, mat],
        out_specs=pair,
        out_shape=jax.ShapeDtypeStruct((n, C_WIDTH), F32),
        scratch_shapes=_dil_scratch(t),
        compiler_params=_cparams("parallel", "parallel", "arbitrary"),
    )(proj, proj, proj, qw, kw, cs, sn, pm)


def dil_bwd(proj, qw, kw, cs, sn, pm, do, nseq):
    n = proj.shape[0]
    t = n // nseq
    dh = HEAD_DIM

    def body(q_ref, k_ref, v_ref, qw_ref, kw_ref, cs_ref, sn_ref, pm_ref, do_ref,
             dq_ref, dk_ref, dv_ref, dqw_ref, dkw_ref,
             qn_s, kn_s, v_s, dl_s, od_s, ld_s, on_s, ln_s, tq_s, tk_s, tv_s, dv_s):
        h = pl.program_id(2)
        qw, kw, pm = qw_ref[...], kw_ref[...], pm_ref[...]
        _dil_forward_parts(h, q_ref, k_ref, v_ref, qw, kw, cs_ref, sn_ref, pm, qn_s, kn_s, v_s, dl_s, od_s, ld_s, on_s, ln_s)

        def mix(c, _):
            rows = pl.ds(pl.multiple_of(c * ROWS, ROWS), ROWS)
            _, vjp = jax.vjp(_dil_mix, od_s[0, rows, :], on_s[0, rows, :], on_s[1, rows, :],
                             ld_s[0, rows, :], ln_s[0, rows, :], ln_s[1, rows, :])
            d1, d2, d3, e1, e2, e3 = vjp(_half(do_ref[rows, :], h))
            od_s[0, rows, :] = d1
            on_s[0, rows, :] = d2
            on_s[1, rows, :] = d3
            ld_s[0, rows, :] = e1
            ln_s[0, rows, :] = e2
            ln_s[1, rows, :] = e3
            return 0

        lax.fori_loop(0, t // ROWS, mix, 0)
        for g in (1, 2):
            _dil_gather(on_s.at[g - 1], od_s.at[g], DIL_D[g])
            _dil_gather(ln_s.at[g - 1], ld_s.at[g], DIL_D[g])
        on_s[...] = jnp.zeros_like(on_s)
        dv_s[...] = jnp.zeros_like(dv_s)
        for g in range(3):
            qs = qn_s if g == 0 else dl_s.at[g - 1, 0]
            ks = kn_s if g == 0 else dl_s.at[g - 1, 1]
            vs = v_s if g == 0 else dl_s.at[g - 1, 2]
            gq = on_s.at[0] if g == 0 else tq_s
            gk = on_s.at[1] if g == 0 else tk_s
            gv = dv_s if g == 0 else tv_s
            if g > 0:
                tk_s[...] = jnp.zeros_like(tk_s)
                tv_s[...] = jnp.zeros_like(tv_s)

            def tiles(i, _, g=g, qs=qs, ks=ks, vs=vs, gq=gq, gk=gk, gv=gv):
                where = [_dil_mask(i * DIL_B + b, g) for b in range(DIL_B)]
                masks = jnp.stack([m for m, _, _ in where])
                _, vjp = jax.vjp(lambda q_, k_, v_: _dil_tiles(q_, k_, v_, masks),
                                 jnp.stack([qs[qr, :] for _, qr, _ in where]), jnp.stack([ks[kr, :] for _, _, kr in where]),
                                 jnp.stack([vs[kr, :] for _, _, kr in where]))
                dq, dkk, dvv = vjp((jnp.stack([od_s[g, qr, :] for _, qr, _ in where]),
                                    jnp.stack([ld_s[g, qr, :] for _, qr, _ in where])))
                for b, (_, qr, kr) in enumerate(where):
                    gq[qr, :] = dq[b]
                    gk[kr, :] += dkk[b]
                    gv[kr, :] += dvv[b]
                return 0

            lax.fori_loop(0, t // BLOCK // DIL_B, tiles, 0)
            if g > 0:
                d = DIL_D[g]
                ln = t // d
                for r in range(d):
                    nat, dil = pl.ds(r, ln, stride=d), pl.ds(r * ln, ln)
                    on_s[0, nat, :] += tq_s[dil, :]
                    on_s[1, nat, :] += tk_s[dil, :]
                    dv_s[nat, :] += tv_s[dil, :]

        def prep(c, acc):
            rows = pl.ds(pl.multiple_of(c * ROWS, ROWS), ROWS)
            f = lambda x, w: _dil_prep(x, w, cs_ref[rows, :], sn_ref[rows, :], pm)
            _, vq = jax.vjp(f, _half(q_ref[rows, :], h), qw)
            _, vk = jax.vjp(f, _half(k_ref[rows, :], h), kw)
            dq, dqw = vq(on_s[0, rows, :])
            dk, dkw = vk(on_s[1, rows, :])
            _store_half(dq_ref, rows, dq, h)
            _store_half(dk_ref, rows, dk, h)
            _store_half(dv_ref, rows, dv_s[rows, :], h)
            return acc[0] + dqw, acc[1] + dkw

        dqw, dkw = lax.fori_loop(0, t // ROWS, prep, (jnp.zeros((1, dh), F32), jnp.zeros((1, dh), F32)))
        dqw_ref[0] = dqw
        dkw_ref[0] = dkw

    qkv, vec, tab, mat, pair = _dil_specs(t)
    per = C_WIDTH // BLOCK
    wout = pl.BlockSpec((1, 1, dh), lambda b, p, h: ((b * per + p) * 2 + h, 0, 0))
    return pl.pallas_call(
        body, name="dil_bwd", grid=(nseq, per, 2),
        in_specs=qkv + [vec, vec, tab, tab, mat, pair],
        out_specs=[pair, pair, pair, wout, wout],
        out_shape=[jax.ShapeDtypeStruct((n, C_WIDTH), F32)] * 3 + [jax.ShapeDtypeStruct((nseq * N_HEADS_C, 1, dh), F32)] * 2,
        scratch_shapes=_dil_scratch(t) + [pltpu.VMEM((t, dh), F32)] * 4,
        compiler_params=_cparams("parallel", "parallel", "arbitrary"),
    )(proj, proj, proj, qw, kw, cs, sn, pm, do)


N_CHIPS = 4
SUM_ROWS = 184
MESH_IDS = pl.DeviceIdType.MESH
ANY = pl.BlockSpec(memory_space=pl.ANY)


def plane_exchange(src, all_to_all):
    blk_shape = src.shape[1:] if all_to_all else src.shape

    def body(src_ref, out_ref, send_sems, recv_sems, local_sem):
        x, y, c = lax.axis_index("x"), lax.axis_index("y"), lax.axis_index("c")
        me = 2 * x + y
        mine = pltpu.make_async_copy(src_ref.at[me] if all_to_all else src_ref, out_ref.at[me], local_sem)
        mine.start()
        sends = []
        for k in (1, 2, 3):
            px = 1 - x if k & 2 else x
            py = 1 - y if k & 1 else y
            peer = 2 * px + py
            cp = pltpu.make_async_remote_copy(
                src_ref=src_ref.at[peer] if all_to_all else src_ref, dst_ref=out_ref.at[me],
                send_sem=send_sems.at[k - 1], recv_sem=recv_sems.at[k - 1],
                device_id=(px, py, c), device_id_type=MESH_IDS)
            cp.start()
            sends.append((cp, peer, (px, py, c)))
        for k, (cp, peer, dev) in enumerate(sends):
            pltpu.make_async_remote_copy(
                src_ref=out_ref.at[me], dst_ref=out_ref.at[peer],
                send_sem=send_sems.at[k], recv_sem=recv_sems.at[k],
                device_id=dev, device_id_type=MESH_IDS).wait_recv()
        for cp, _, _ in sends:
            cp.wait_send()
        mine.wait()

    return pl.pallas_call(
        body, name="plane_all_to_all" if all_to_all else "plane_all_gather",
        in_specs=[ANY], out_specs=ANY,
        out_shape=jax.ShapeDtypeStruct((N_CHIPS,) + blk_shape, src.dtype),
        scratch_shapes=[pltpu.SemaphoreType.DMA((3,)), pltpu.SemaphoreType.DMA((3,)), pltpu.SemaphoreType.DMA],
    )(src)


def sibling_swap(src):
    def body(src_ref, out_ref, send_sem, recv_sem):
        x, y, c = lax.axis_index("x"), lax.axis_index("y"), lax.axis_index("c")
        cp = pltpu.make_async_remote_copy(src_ref=src_ref, dst_ref=out_ref, send_sem=send_sem, recv_sem=recv_sem,
                                          device_id=(x, y, 1 - c), device_id_type=MESH_IDS)
        cp.start()
        cp.wait()

    return pl.pallas_call(
        body, name="sibling_swap", in_specs=[ANY], out_specs=ANY,
        out_shape=jax.ShapeDtypeStruct(src.shape, src.dtype),
        scratch_shapes=[pltpu.SemaphoreType.DMA, pltpu.SemaphoreType.DMA],
    )(src)


def sum4(a):
    _, r, c = a.shape
    tr = SUM_ROWS

    def body(a_ref, o_ref):
        o_ref[...] = (a_ref[0] + a_ref[1]) + (a_ref[2] + a_ref[3])

    return pl.pallas_call(
        body, name="sum4", grid=(r // tr,),
        in_specs=[pl.BlockSpec((N_CHIPS, tr, c), lambda i: (0, i, 0))],
        out_specs=pl.BlockSpec((tr, c), lambda i: (i, 0)),
        out_shape=jax.ShapeDtypeStruct((r, c), F32),
        compiler_params=_cparams("parallel"),
    )(a)


def add2(a, b):
    r, c = a.shape
    tr = SUM_ROWS

    def body(a_ref, b_ref, o_ref):
        o_ref[...] = a_ref[...] + b_ref[...]

    blk = pl.BlockSpec((tr, c), lambda i: (i, 0))
    return pl.pallas_call(
        body, name="add2", grid=(r // tr,), in_specs=[blk, blk], out_specs=blk,
        out_shape=jax.ShapeDtypeStruct((r, c), F32), compiler_params=_cparams("parallel"),
    )(a, b)


PACK_COLS = 1024
PACK_ROWS = 2576


def _pack(parts):
    flat = jnp.concatenate([p.reshape(-1) for p in parts])
    flat = jnp.pad(flat, (0, PACK_ROWS * PACK_COLS - flat.shape[0]))
    return flat.reshape(PACK_ROWS, PACK_COLS)


def _unpack(buf, shapes):
    flat = buf.reshape(-1)
    out, at = [], 0
    for s in shapes:
        size = math.prod(s)
        out.append(flat[at:at + size].reshape(s))
        at += size
    return out


def _layer_fwd(x, p, nseq, tabs):
    proj, hdn = inproj_fwd(x, p["norm_w"][None], p["w_in"])
    ya = conv_fwd(proj, p["conv_w"], nseq)
    oa, states = gdn_fwd(ya, proj, p["a_log"].reshape(N_HEADS_A, 1, 1), p["dt_bias"].reshape(N_HEADS_A, 1, 1),
                         p["gdn_norm_w"][None], nseq)
    ob, carries = sb_fwd(proj, nseq)
    oc = dil_fwd(proj, p["q_norm_w"][None], p["k_norm_w"][None], *tabs, nseq)
    y, mixed = outproj_fwd(x, oa, ob, oc, proj, p["w_out"])
    return y, dict(x=x, hdn=hdn, proj=proj, ya=ya, states=states, carries=carries, oa=oa, ob=ob, oc=oc, mixed=mixed)


def _layer_bwd(dy, p, res, nseq, tabs):
    proj = res["proj"]
    g = {}
    g["w_out"] = mat_tn(res["mixed"], [dy])[0]
    doa, dob, doc, dza, dzb, dzc = outproj_bwd(dy, res["oa"], res["ob"], res["oc"], proj, p["w_out"])
    dqc, dkc, dvc, dqw, dkw = dil_bwd(proj, p["q_norm_w"][None], p["k_norm_w"][None], *tabs, doc, nseq)
    g["q_norm_w"], g["k_norm_w"] = dqw.sum((0, 1)), dkw.sum((0, 1))
    dqb, dkb, dvb = sb_bwd(proj, res["carries"], dob, nseq)
    dya, dba, dalog, ddtb, dnw = gdn_bwd(res["ya"], proj, p["a_log"].reshape(N_HEADS_A, 1, 1),
                                         p["dt_bias"].reshape(N_HEADS_A, 1, 1), p["gdn_norm_w"][None], res["states"], doa, nseq)
    g["a_log"], g["dt_bias"], g["gdn_norm_w"] = dalog.sum(0).reshape(-1), ddtb.sum(0).reshape(-1), dnw.sum((0, 1))
    dqkv, dcw = conv_bwd(proj, p["conv_w"], dya, nseq)
    g["conv_w"] = dcw.sum(0)
    slabs = [dqkv, dza, dqc, dkc, dvc, dzc, dqb, dkb, dvb, dzb, dba]
    hdn = res["hdn"]
    g["w_in"] = jnp.concatenate(mat_tn(hdn, slabs[:2]) + mat_tn(hdn, slabs[2:6]) + mat_tn(hdn, slabs[6:]), axis=1)
    dx, dnw_tiles = inproj_bwd(slabs, p["w_in"], res["x"], p["norm_w"][None], dy)
    g["norm_w"] = dnw_tiles.sum((0, 1))
    return dx, g


SMALL = ("norm_w", "a_log", "dt_bias", "gdn_norm_w", "q_norm_w", "k_norm_w")


def _local_step(x, target, full):
    nseq, t, d = x.shape
    tabs = _rope_tables(t)
    h = x.reshape(nseq * t, d)
    saved = []
    for l in range(DEPTH):
        p = {k: v[l] for k, v in full.items()}
        h, res = _layer_fwd(h, p, nseq, tabs)
        saved.append((p, res))
    dy, parts = loss_fwd_bwd(h, target.reshape(nseq * t, d))
    loss = parts[:, 0, 0].sum()
    grads = [None] * DEPTH
    for l in reversed(range(DEPTH)):
        p, res = saved[l]
        dy, grads[l] = _layer_bwd(dy, p, res, nseq, tabs)
    return loss, dy.reshape(nseq, t, d), {k: jnp.stack([g[k] for g in grads]) for k in grads[0]}


def _pad_cols(w):
    b0 = ORIG_A + ORIG_BA
    c0 = b0 + ORIG_B
    zeros = jnp.zeros(w.shape[:-1] + (BLOCK - ORIG_BA,), w.dtype)
    return jnp.concatenate([w[..., :ORIG_A], w[..., c0:], w[..., b0:c0], w[..., ORIG_A:b0], zeros], axis=-1)


def _unpad_cols(w):
    return jnp.concatenate([w[..., :COL_C], w[..., COL_BA:COL_BA + ORIG_BA], w[..., COL_B:COL_BA], w[..., COL_C:COL_B]],
                           axis=-1)


def kernel(x, norm_w, w_in, conv_w, a_log, dt_bias, gdn_norm_w, q_norm_w, k_norm_w, w_out, loss_target, m_norm_w, m_w_in, m_conv_w, m_a_log, m_dt_bias, m_gdn_norm_w, m_q_norm_w, m_k_norm_w, m_w_out, v_norm_w, v_w_in, v_conv_w, v_a_log, v_dt_bias, v_gdn_norm_w, v_q_norm_w, v_k_norm_w, v_w_out):
    weights = dict(norm_w=norm_w, w_in=w_in, conv_w=conv_w, a_log=a_log, dt_bias=dt_bias, gdn_norm_w=gdn_norm_w,
                   q_norm_w=q_norm_w, k_norm_w=k_norm_w, w_out=w_out)
    moms = dict(norm_w=m_norm_w, w_in=m_w_in, conv_w=m_conv_w, a_log=m_a_log, dt_bias=m_dt_bias,
                gdn_norm_w=m_gdn_norm_w, q_norm_w=m_q_norm_w, k_norm_w=m_k_norm_w, w_out=m_w_out)
    vars_ = dict(norm_w=v_norm_w, w_in=v_w_in, conv_w=v_conv_w, a_log=v_a_log, dt_bias=v_dt_bias,
                 gdn_norm_w=v_gdn_norm_w, q_norm_w=v_q_norm_w, k_norm_w=v_k_norm_w, w_out=v_w_out)
    names = list(weights)
    sharded = ("w_in", "w_out", "conv_w")
    shard_shapes = [weights[k].shape for k in sharded]

    conv_bits = lax.bitcast_convert_type(conv_w, BF16)
    got = plane_exchange(_pack([w_in.astype(BF16), w_out.astype(BF16), conv_bits]), all_to_all=False)
    per_chip = [_unpack(got[i], shard_shapes[:2] + [conv_bits.shape]) for i in range(N_CHIPS)]
    full = {k: weights[k] for k in SMALL}
    full["w_in"] = _pad_cols(jnp.concatenate([pc[0] for pc in per_chip], axis=2))
    full["w_out"] = jnp.concatenate([pc[1] for pc in per_chip], axis=1)
    full["conv_w"] = jnp.concatenate([lax.bitcast_convert_type(pc[2], F32) for pc in per_chip], axis=2)

    loss, grad_x, g = _local_step(x, loss_target, full)

    gw_in = _unpad_cols(g["w_in"])
    cols, rows = w_in.shape[2], w_out.shape[1]
    small = [g[k] for k in SMALL]
    send = jnp.stack([_pack([gw_in[:, :, i * cols:(i + 1) * cols], g["w_out"][:, i * rows:(i + 1) * rows],
                             g["conv_w"][:, :, i * conv_w.shape[2]:(i + 1) * conv_w.shape[2]]] + small)
                      for i in range(N_CHIPS)])
    c = lax.axis_index("c")
    half = PACK_ROWS // 2
    keep = lax.dynamic_slice_in_dim(send, c * half, half, axis=1)
    give = lax.dynamic_slice_in_dim(send, (1 - c) * half, half, axis=1)
    chip_sum = add2(keep.reshape(N_CHIPS * half, PACK_COLS), sibling_swap(give).reshape(N_CHIPS * half, PACK_COLS))
    mine = sum4(plane_exchange(chip_sum.reshape(N_CHIPS, half, PACK_COLS), all_to_all=True))
    other = sibling_swap(mine)
    total = jnp.concatenate([jnp.where(c == 0, mine, other), jnp.where(c == 0, other, mine)])
    reduced = _unpack(total, shard_shapes + [weights[k].shape for k in SMALL])
    grads = dict(zip(sharded + SMALL, reduced))
    loss = lax.psum(loss, ("x", "y", "c"))

    def two_d(a):
        return a.reshape(-1, a.shape[-1])

    delta, new_m, new_v = {}, {}, {}
    for k in names:
        d_, m_, v_ = adamw(two_d(weights[k]), two_d(grads[k]), two_d(moms[k]), two_d(vars_[k]))
        delta[k], new_m[k], new_v[k] = (a.reshape(weights[k].shape) for a in (d_, m_, v_))
    return (loss, grad_x, *[grads[k] for k in names], *[delta[k] for k in names],
            *[new_m[k] for k in names], *[new_v[k] for k in names])
```

```python
import functools
import math

import jax
import jax.numpy as jnp
from jax import lax
from jax.experimental import pallas as pl
from jax.experimental.pallas import tpu as pltpu

F32 = jnp.float32
BF16 = jnp.bfloat16

D_MODEL = 1024
SEQ = 2048
DEPTH = 2
HEAD_DIM = 64
N_HEADS_A, N_HEADS_B, N_HEADS_C = 6, 4, 6
A_WIDTH, B_WIDTH, C_WIDTH = N_HEADS_A * HEAD_DIM, N_HEADS_B * HEAD_DIM, N_HEADS_C * HEAD_DIM
CONV_WIDTH = 4
GDN_CHUNK = 64
BLOCK = 128
ROPE_DIM = 16
ROPE_THETA = 500000.0
DILATED_PAIRS = ((128, 1), (512, 4), (2048, 16))
RMS_EPS = 1e-6
NEG = -1e30

NT = (((1,), (1,)), ((), ()))
NN = (((1,), (0,)), ((), ()))
TN = (((0,), (0,)), ((), ()))

VMEM_LIMIT = 48 * 1024 * 1024

ORIG_A = 4 * A_WIDTH
ORIG_BA = 2 * N_HEADS_A
ORIG_B = 4 * B_WIDTH
COL_AZ = 3 * A_WIDTH
COL_C = 4 * A_WIDTH
COL_B = COL_C + 4 * C_WIDTH
COL_BA = COL_B + 4 * B_WIDTH
P_COLS = COL_BA + BLOCK
TN_COLS = 384
TM_ROWS = 512
ROWS = 256


def _mm(a, b, dims=NN):
    return lax.dot_general(a.astype(BF16), b.astype(BF16), dims, preferred_element_type=F32)


def _mm32(a, b, dims=NN):
    return lax.dot_general(a, b, dims, precision=lax.Precision.HIGH, preferred_element_type=F32)


def _cparams(*sem):
    return pltpu.CompilerParams(dimension_semantics=sem, vmem_limit_bytes=VMEM_LIMIT)


def _sigmoid(x):
    return 0.5 * (jnp.tanh(0.5 * x) + 1.0)


def _softplus(x):
    return jnp.maximum(x, 0.0) + jnp.log(1.0 + jnp.exp(-jnp.abs(x)))


def _rms(x, w):
    return x * lax.rsqrt(jnp.mean(x * x, axis=-1, keepdims=True) + RMS_EPS) * w


def _heads(a, n):
    return jnp.stack([a[:, h * HEAD_DIM:(h + 1) * HEAD_DIM] for h in range(n)])


def _unheads(a):
    return jnp.concatenate([a[h] for h in range(a.shape[0])], axis=1)


def _row_chunks(t):
    return [pl.ds(c * ROWS, ROWS) for c in range(t // ROWS)]


def inproj_fwd(x, nw, w):
    n, d = x.shape
    p = w.shape[1]

    def body(x_ref, nw_ref, w_ref, proj_ref, hdn_ref):
        @pl.when(pl.program_id(1) == 0)
        def _():
            hdn_ref[...] = _rms(x_ref[...], nw_ref[...]).astype(BF16)

        proj_ref[...] = jnp.dot(hdn_ref[...], w_ref[...], preferred_element_type=F32)

    return pl.pallas_call(
        body, name="inproj_fwd", grid=(n // TM_ROWS, p // TN_COLS),
        in_specs=[pl.BlockSpec((TM_ROWS, d), lambda i, j: (i, 0)), pl.BlockSpec((1, d), lambda i, j: (0, 0)),
                  pl.BlockSpec((d, TN_COLS), lambda i, j: (0, j))],
        out_specs=[pl.BlockSpec((TM_ROWS, TN_COLS), lambda i, j: (i, j)), pl.BlockSpec((TM_ROWS, d), lambda i, j: (i, 0))],
        out_shape=[jax.ShapeDtypeStruct((n, p), F32), jax.ShapeDtypeStruct((n, d), BF16)],
        compiler_params=_cparams("parallel", "arbitrary"),
    )(x, nw, w)


def mat_tn(a, slabs):
    n, ka = a.shape
    ns = len(slabs)

    def body(*refs):
        a_ref, s_refs, o_refs = refs[0], refs[1:1 + ns], refs[1 + ns:]

        @pl.when(pl.program_id(0) == 0)
        def _():
            for o_ref in o_refs:
                o_ref[...] = jnp.zeros_like(o_ref)

        av = a_ref[...]
        for s_ref, o_ref in zip(s_refs, o_refs):
            o_ref[...] += lax.dot_general(av, s_ref[...].astype(BF16), TN, preferred_element_type=F32)

    return pl.pallas_call(
        body, name="mat_tn", grid=(n // TM_ROWS,),
        in_specs=[pl.BlockSpec((TM_ROWS, ka), lambda k: (k, 0))]
                 + [pl.BlockSpec((TM_ROWS, s.shape[1]), lambda k: (k, 0)) for s in slabs],
        out_specs=[pl.BlockSpec((ka, s.shape[1]), lambda k: (0, 0)) for s in slabs],
        out_shape=[jax.ShapeDtypeStruct((ka, s.shape[1]), F32) for s in slabs],
        compiler_params=_cparams("arbitrary"),
    )(a, *slabs)


def inproj_bwd(slabs, w, x, nw, dy):
    n, d = x.shape
    p = w.shape[1]
    tm = 256
    ns = len(slabs)

    def body(*refs):
        s_refs = refs[:ns]
        w_ref, x_ref, nw_ref, dy_ref, dx_ref, dnw_ref = refs[ns:]
        dh = jnp.zeros((tm, d), F32)
        at = 0
        for s_ref in s_refs:
            wd = s_ref.shape[1]
            dh = dh + lax.dot_general(s_ref[...].astype(BF16), w_ref[:, at:at + wd], NT, preferred_element_type=F32)
            at += wd
        _, vjp = jax.vjp(_rms, x_ref[...], nw_ref[...])
        dx, dnw = vjp(dh)
        dx_ref[...] = dx + dy_ref[...]
        dnw_ref[0] = dnw

    return pl.pallas_call(
        body, name="inproj_bwd", grid=(n // tm,),
        in_specs=[pl.BlockSpec((tm, s.shape[1]), lambda i: (i, 0)) for s in slabs]
                 + [pl.BlockSpec((d, p), lambda i: (0, 0)), pl.BlockSpec((tm, d), lambda i: (i, 0)),
                    pl.BlockSpec((1, d), lambda i: (0, 0)), pl.BlockSpec((tm, d), lambda i: (i, 0))],
        out_specs=[pl.BlockSpec((tm, d), lambda i: (i, 0)), pl.BlockSpec((1, 1, d), lambda i: (i, 0, 0))],
        out_shape=[jax.ShapeDtypeStruct((n, d), F32), jax.ShapeDtypeStruct((n // tm, 1, d), F32)],
        compiler_params=_cparams("parallel"),
    )(*slabs, w, x, nw, dy)


CONV_PAD = 8
CONV_ROWS = 256


def _conv_pre(pad_s, cw, c):
    xs = [pad_s[pl.ds(c * CONV_ROWS + CONV_PAD - (CONV_WIDTH - 1) + k, CONV_ROWS), :] for k in range(CONV_WIDTH)]
    pre = xs[0] * cw[0:1, :]
    for k in range(1, CONV_WIDTH):
        pre = pre + xs[k] * cw[k:k + 1, :]
    return pre, xs


def conv_fwd(proj, cw, nseq):
    n = proj.shape[0]
    t = n // nseq
    ch = cw.shape[1]

    def body(x_ref, cw_ref, y_ref, pad_s):
        pad_s[pl.ds(0, CONV_PAD), :] = jnp.zeros((CONV_PAD, TN_COLS), F32)
        pad_s[pl.ds(CONV_PAD, t), :] = x_ref[...]
        cwv = cw_ref[...]
        for c in range(t // CONV_ROWS):
            pre, _ = _conv_pre(pad_s, cwv, c)
            y_ref[pl.ds(c * CONV_ROWS, CONV_ROWS), :] = pre * _sigmoid(pre)

    return pl.pallas_call(
        body, name="conv_fwd", grid=(nseq, ch // TN_COLS),
        in_specs=[pl.BlockSpec((t, TN_COLS), lambda b, j: (b, j)), pl.BlockSpec((CONV_WIDTH, TN_COLS), lambda b, j: (0, j))],
        out_specs=pl.BlockSpec((t, TN_COLS), lambda b, j: (b, j)),
        out_shape=jax.ShapeDtypeStruct((n, ch), F32),
        scratch_shapes=[pltpu.VMEM((t + CONV_PAD, TN_COLS), F32)],
        compiler_params=_cparams("parallel", "parallel"),
    )(proj, cw)


def conv_bwd(proj, cw, dy, nseq):
    n = proj.shape[0]
    t = n // nseq
    ch = cw.shape[1]

    def body(x_ref, cw_ref, dy_ref, dx_ref, dcw_ref, pad_s, dpad_s):
        pad_s[pl.ds(0, CONV_PAD), :] = jnp.zeros((CONV_PAD, TN_COLS), F32)
        pad_s[pl.ds(CONV_PAD, t), :] = x_ref[...]
        dpad_s[pl.ds(t, CONV_PAD), :] = jnp.zeros((CONV_PAD, TN_COLS), F32)
        cwv = cw_ref[...]
        acc = [jnp.zeros((1, TN_COLS), F32)] * CONV_WIDTH
        for c in range(t // CONV_ROWS):
            pre, xs = _conv_pre(pad_s, cwv, c)
            sg = _sigmoid(pre)
            dpre = dy_ref[pl.ds(c * CONV_ROWS, CONV_ROWS), :] * (sg * (1.0 + pre * (1.0 - sg)))
            dpad_s[pl.ds(c * CONV_ROWS, CONV_ROWS), :] = dpre
            acc = [acc[k] + jnp.sum(dpre * xs[k], axis=0, keepdims=True) for k in range(CONV_WIDTH)]
        for k in range(CONV_WIDTH):
            dcw_ref[0, pl.ds(k, 1), :] = acc[k]
        for c in range(t // CONV_ROWS):
            dx = dpad_s[pl.ds(c * CONV_ROWS + CONV_WIDTH - 1, CONV_ROWS), :] * cwv[0:1, :]
            for k in range(1, CONV_WIDTH):
                dx = dx + dpad_s[pl.ds(c * CONV_ROWS + CONV_WIDTH - 1 - k, CONV_ROWS), :] * cwv[k:k + 1, :]
            dx_ref[pl.ds(c * CONV_ROWS, CONV_ROWS), :] = dx

    blk = pl.BlockSpec((t, TN_COLS), lambda b, j: (b, j))
    return pl.pallas_call(
        body, name="conv_bwd", grid=(nseq, ch // TN_COLS),
        in_specs=[blk, pl.BlockSpec((CONV_WIDTH, TN_COLS), lambda b, j: (0, j)), blk],
        out_specs=[blk, pl.BlockSpec((1, CONV_WIDTH, TN_COLS), lambda b, j: (b, 0, j))],
        out_shape=[jax.ShapeDtypeStruct((n, ch), F32), jax.ShapeDtypeStruct((nseq, CONV_WIDTH, ch), F32)],
        scratch_shapes=[pltpu.VMEM((t + CONV_PAD, TN_COLS), F32)] * 2,
        compiler_params=_cparams("parallel", "parallel"),
    )(proj, cw, dy)


def _gate_specs(d):
    wide = pl.BlockSpec((TM_ROWS, d), lambda i: (i, 0))
    oa = pl.BlockSpec((TM_ROWS, A_WIDTH), lambda i: (i, 0))
    ob = pl.BlockSpec((TM_ROWS, B_WIDTH), lambda i: (i, 0))
    oc = pl.BlockSpec((TM_ROWS, C_WIDTH), lambda i: (i, 0))
    za = pl.BlockSpec((TM_ROWS, A_WIDTH), lambda i: (i, COL_AZ // A_WIDTH))
    zb = pl.BlockSpec((TM_ROWS, B_WIDTH), lambda i: (i, (COL_B + 3 * B_WIDTH) // B_WIDTH))
    zc = pl.BlockSpec((TM_ROWS, C_WIDTH), lambda i: (i, (COL_C + 3 * C_WIDTH) // C_WIDTH))
    return wide, oa, ob, oc, za, zb, zc


BRANCH_COLS = ((0, A_WIDTH), (A_WIDTH, A_WIDTH + B_WIDTH), (A_WIDTH + B_WIDTH, D_MODEL))


def outproj_fwd(x, oa, ob, oc, proj, w):
    n, d = x.shape

    def body(x_ref, oa_ref, ob_ref, oc_ref, za_ref, zb_ref, zc_ref, w_ref, y_ref, m_ref):
        for (lo, hi), o_ref, z_ref in zip(BRANCH_COLS, (oa_ref, ob_ref, oc_ref), (za_ref, zb_ref, zc_ref)):
            zv = z_ref[...]
            m_ref[:, lo:hi] = (o_ref[...] * (zv * _sigmoid(zv))).astype(BF16)
        y_ref[...] = x_ref[...] + jnp.dot(m_ref[...], w_ref[...], preferred_element_type=F32)

    wide, sa, sb, sc, za, zb, zc = _gate_specs(d)
    return pl.pallas_call(
        body, name="outproj_fwd", grid=(n // TM_ROWS,),
        in_specs=[wide, sa, sb, sc, za, zb, zc, pl.BlockSpec((d, d), lambda i: (0, 0))],
        out_specs=[wide, wide],
        out_shape=[jax.ShapeDtypeStruct((n, d), F32), jax.ShapeDtypeStruct((n, d), BF16)],
        compiler_params=_cparams("parallel"),
    )(x, oa, ob, oc, proj, proj, proj, w)


def outproj_bwd(dy, oa, ob, oc, proj, w):
    n, d = dy.shape

    def body(dy_ref, oa_ref, ob_ref, oc_ref, za_ref, zb_ref, zc_ref, w_ref, doa_ref, dob_ref, doc_ref, dza_ref, dzb_ref, dzc_ref):
        dm = lax.dot_general(dy_ref[...].astype(BF16), w_ref[...], NT, preferred_element_type=F32)
        for (lo, hi), o_ref, z_ref, do_ref, dz_ref in zip(BRANCH_COLS, (oa_ref, ob_ref, oc_ref), (za_ref, zb_ref, zc_ref),
                                                          (doa_ref, dob_ref, doc_ref), (dza_ref, dzb_ref, dzc_ref)):
            zv = z_ref[...]
            sg = _sigmoid(zv)
            dmv = dm[:, lo:hi]
            do_ref[...] = dmv * (zv * sg)
            dz_ref[...] = dmv * o_ref[...] * (sg * (1.0 + zv * (1.0 - sg)))

    wide, sa, sb, sc, za, zb, zc = _gate_specs(d)
    sd = jax.ShapeDtypeStruct
    outs = [sd((n, A_WIDTH), F32), sd((n, B_WIDTH), F32), sd((n, C_WIDTH), F32)]
    return pl.pallas_call(
        body, name="outproj_bwd", grid=(n // TM_ROWS,),
        in_specs=[wide, sa, sb, sc, za, zb, zc, pl.BlockSpec((d, d), lambda i: (0, 0))],
        out_specs=[sa, sb, sc, sa, sb, sc],
        out_shape=outs + outs,
        compiler_params=_cparams("parallel"),
    )(dy, oa, ob, oc, proj, proj, proj, w)


def loss_fwd_bwd(y, target):
    n, d = y.shape

    def body(y_ref, t_ref, dy_ref, part_ref):
        e = y_ref[...] - t_ref[...]
        dy_ref[...] = e * (1.0 / d)
        part_ref[...] = jnp.zeros_like(part_ref) + 0.5 * jnp.sum(e * e) * (1.0 / d)

    blk = pl.BlockSpec((TM_ROWS, d), lambda i: (i, 0))
    return pl.pallas_call(
        body, name="loss", grid=(n // TM_ROWS,),
        in_specs=[blk, blk],
        out_specs=[blk, pl.BlockSpec((1, 8, BLOCK), lambda i: (i, 0, 0))],
        out_shape=[jax.ShapeDtypeStruct((n, d), F32), jax.ShapeDtypeStruct((n // TM_ROWS, 8, BLOCK), F32)],
        compiler_params=_cparams("parallel"),
    )(y, target)


ADAM_LR, ADAM_B1, ADAM_B2, ADAM_EPS, ADAM_WD, ADAM_STEP = 0.001, 0.9, 0.999, 1e-08, 0.01, 10


def adamw(w, g, m, v):
    r, c = w.shape
    tr = r if r <= 256 else 256

    def body(w_ref, g_ref, m_ref, v_ref, d_ref, nm_ref, nv_ref):
        gv = g_ref[...]
        nm = ADAM_B1 * m_ref[...] + (1.0 - ADAM_B1) * gv
        nv = ADAM_B2 * v_ref[...] + (1.0 - ADAM_B2) * (gv * gv)
        m_hat = nm / (1.0 - ADAM_B1 ** ADAM_STEP)
        v_hat = nv / (1.0 - ADAM_B2 ** ADAM_STEP)
        d_ref[...] = -ADAM_LR * (m_hat / (jnp.sqrt(v_hat) + ADAM_EPS) + ADAM_WD * w_ref[...])
        nm_ref[...] = nm
        nv_ref[...] = nv

    blk = pl.BlockSpec((tr, c), lambda i: (i, 0))
    return pl.pallas_call(
        body, name="adamw", grid=(r // tr,),
        in_specs=[blk] * 4, out_specs=[blk] * 3,
        out_shape=[jax.ShapeDtypeStruct((r, c), F32)] * 3,
        compiler_params=_cparams("parallel"),
    )(w, g, m, v)


SB_G = N_HEADS_B


def _sb_tile(q, k, v, carry, qpos, kpos, tri):
    z = _mm(q * (HEAD_DIM ** -0.5), k, NT)
    earlier = kpos < qpos
    sp = jnp.log(1.0 + jnp.exp(-jnp.abs(z)))
    ls_pos = jnp.minimum(z, 0.0) - sp
    ls_neg = jnp.minimum(-z, 0.0) - sp
    log_keep = jnp.where(earlier, ls_neg, 0.0)
    within = _mm32(log_keep, tri)
    wts = jnp.where(earlier, jnp.exp(jnp.where(earlier, ls_pos + within + carry, 0.0)), 0.0)
    return _mm(wts, v), jnp.sum(log_keep, axis=1, keepdims=True)


_sb_tiles = jax.vmap(_sb_tile, in_axes=(0, 0, 0, 0, None, None, None))


def _sb_consts():
    qi = lax.broadcasted_iota(jnp.int32, (BLOCK, 1), 0)
    kj = lax.broadcasted_iota(jnp.int32, (1, BLOCK), 1)
    r = lax.broadcasted_iota(jnp.int32, (BLOCK, BLOCK), 0)
    c = lax.broadcasted_iota(jnp.int32, (BLOCK, BLOCK), 1)
    tri = jnp.where(r > c, 1.0, 0.0).astype(F32)
    return qi, kj, tri


def _sb_specs(t, nq):
    cb = COL_B // B_WIDTH
    q = pl.BlockSpec((BLOCK, B_WIDTH), lambda b, i: (b * nq + i, cb))
    k = pl.BlockSpec((t, B_WIDTH), lambda b, i: (b, cb + 1))
    v = pl.BlockSpec((t, B_WIDTH), lambda b, i: (b, cb + 2))
    blk = pl.BlockSpec((BLOCK, B_WIDTH), lambda b, i: (b * nq + i, 0))
    full = pl.BlockSpec((t, B_WIDTH), lambda b, i: (b, 0))
    carry = pl.BlockSpec((1, 1, nq, BLOCK, SB_G), lambda b, i: (b, i, 0, 0, 0))
    return q, k, v, blk, full, carry


def sb_fwd(proj, nseq):
    n = proj.shape[0]
    t = n // nseq
    nq = t // BLOCK
    g = SB_G

    def body(q_ref, k_ref, v_ref, o_ref, carry_ref):
        i = pl.program_id(1)
        qi, kj, tri = _sb_consts()
        qv = _heads(q_ref[...], g)
        qpos = i * BLOCK + qi

        def step(it, st):
            o_acc, c = st
            j = i - it
            rows = pl.ds(pl.multiple_of(j * BLOCK, BLOCK), BLOCK)
            carry_ref[0, 0, j] = jnp.concatenate([c[h] for h in range(g)], axis=1)
            o, tot = _sb_tiles(qv, _heads(k_ref[rows, :], g), _heads(v_ref[rows, :], g), c, qpos, j * BLOCK + kj, tri)
            return o_acc + o, c + tot

        o_acc, _ = lax.fori_loop(0, i + 1, step, (jnp.zeros((g, BLOCK, HEAD_DIM), F32), jnp.zeros((g, BLOCK, 1), F32)))
        o_ref[...] = _unheads(o_acc)

    q, k, v, blk, _, carry = _sb_specs(t, nq)
    return pl.pallas_call(
        body, name="sb_fwd", grid=(nseq, nq),
        in_specs=[q, k, v],
        out_specs=[blk, carry],
        out_shape=[jax.ShapeDtypeStruct((n, B_WIDTH), F32),
                   jax.ShapeDtypeStruct((nseq, nq, nq, BLOCK, g), F32)],
        compiler_params=_cparams("parallel", "arbitrary"),
    )(proj, proj, proj)


def sb_bwd(proj, carries, do, nseq):
    n = proj.shape[0]
    t = n // nseq
    nq = t // BLOCK
    g = SB_G

    def body(q_ref, k_ref, v_ref, carry_ref, do_ref, dq_ref, dk_ref, dv_ref):
        i = pl.program_id(1)

        @pl.when(i == 0)
        def _():
            dk_ref[...] = jnp.zeros_like(dk_ref)
            dv_ref[...] = jnp.zeros_like(dv_ref)

        qi, kj, tri = _sb_consts()
        qv = _heads(q_ref[...], g)
        dov = _heads(do_ref[...], g)
        qpos = i * BLOCK + qi

        def step(j, st):
            dq_acc, dc = st
            rows = pl.ds(pl.multiple_of(j * BLOCK, BLOCK), BLOCK)
            kpos = j * BLOCK + kj
            cj = carry_ref[0, 0, j]
            f = lambda q_, k_, v_, c_: _sb_tiles(q_, k_, v_, c_, qpos, kpos, tri)
            _, vjp = jax.vjp(f, qv, _heads(k_ref[rows, :], g), _heads(v_ref[rows, :], g),
                             jnp.stack([cj[:, h:h + 1] for h in range(g)]))
            dq, dk, dv, dcj = vjp((dov, dc))
            dk_ref[rows, :] += _unheads(dk)
            dv_ref[rows, :] += _unheads(dv)
            return dq_acc + dq, dc + dcj

        dq_acc, _ = lax.fori_loop(0, i + 1, step, (jnp.zeros((g, BLOCK, HEAD_DIM), F32), jnp.zeros((g, BLOCK, 1), F32)))
        dq_ref[...] = _unheads(dq_acc)

    q, k, v, blk, full, carry = _sb_specs(t, nq)
    return pl.pallas_call(
        body, name="sb_bwd", grid=(nseq, nq),
        in_specs=[q, k, v, carry, blk],
        out_specs=[blk, full, full],
        out_shape=[jax.ShapeDtypeStruct((n, B_WIDTH), F32)] * 3,
        compiler_params=_cparams("parallel", "arbitrary"),
    )(proj, proj, proj, carries, do)


def _gdn_chunk(q, k, v, al_c, al_r, br_c, alog, dtb, nw, s):
    c = GDN_CHUNK
    ri = lax.broadcasted_iota(jnp.int32, (c, c), 0)
    ci = lax.broadcasted_iota(jnp.int32, (c, c), 1)
    incl, strict = ri >= ci, ri > ci
    eye = jnp.where(ri == ci, 1.0, 0.0).astype(F32)
    rate = -jnp.exp(alog)
    g_c = rate * _softplus(al_c + dtb)
    g_r = rate * _softplus(al_r + dtb)
    beta = _sigmoid(br_c)
    gc_c = jnp.sum(jnp.where(incl, g_r, 0.0), axis=1, keepdims=True)
    gc_r = jnp.sum(jnp.where(ri <= ci, g_c, 0.0), axis=0, keepdims=True)
    gl = jnp.sum(g_r, axis=1, keepdims=True)
    decay = jnp.where(incl, jnp.exp(jnp.where(incl, gc_c - gc_r, 0.0)), 0.0)
    qn = q * lax.rsqrt(jnp.sum(q * q, axis=-1, keepdims=True) + RMS_EPS) * (HEAD_DIM ** -0.5)
    kn = k * lax.rsqrt(jnp.sum(k * k, axis=-1, keepdims=True) + RMS_EPS)
    kb = kn * beta
    a = jnp.where(strict, _mm(kb, kn, NT) * decay, 0.0)
    tmat = eye - a
    p = a
    for _ in range(5):
        p = _mm32(p, p)
        tmat = tmat + _mm32(tmat, p)
    u = _mm(tmat, v * beta)
    w = _mm(tmat, kb * jnp.exp(gc_c))
    qk = _mm(qn, kn, NT) * decay
    v_new = u - _mm(w, s)
    o = _mm(qn * jnp.exp(gc_c), s) + _mm(qk, v_new)
    s_new = s * jnp.exp(gl) + _mm(kn * jnp.exp(gl - gc_c), v_new, TN)
    o = o * lax.rsqrt(jnp.mean(o * o, axis=-1, keepdims=True) + RMS_EPS) * nw
    return o, s_new


_gdn_chunks = jax.vmap(_gdn_chunk, in_axes=(0, 0, 0, 0, 0, 0, 0, 0, None, 0))

GDN_TB = 256


def _gdn_block(q3, k3, v3, ba, alog, dtb, nw, s):
    nh = N_HEADS_A
    bat = ba.T
    br_c = jnp.stack([ba[:, h:h + 1] for h in range(nh)])
    al_c = jnp.stack([ba[:, nh + h:nh + h + 1] for h in range(nh)])
    al_r = jnp.stack([bat[nh + h:nh + h + 1, :] for h in range(nh)])
    o, s_new = _gdn_chunks(_heads(q3, nh), _heads(k3, nh), _heads(v3, nh), al_c, al_r, br_c, alog, dtb, nw, s)
    return _unheads(o), s_new


def _gdn_specs(nt, rev):
    tpos = (lambda i: nt - 1 - i) if rev else (lambda i: i)
    ncb = GDN_TB // GDN_CHUNK
    qkv = [pl.BlockSpec((GDN_TB, A_WIDTH), lambda b, i, j=j: (b * nt + tpos(i), j)) for j in range(3)]
    ba = pl.BlockSpec((GDN_TB, BLOCK), lambda b, i: (b * nt + tpos(i), COL_BA // BLOCK))
    one = pl.BlockSpec((N_HEADS_A, 1, 1), lambda b, i: (0, 0, 0))
    vec = pl.BlockSpec((1, HEAD_DIM), lambda b, i: (0, 0))
    st = pl.BlockSpec((1, ncb, N_HEADS_A, HEAD_DIM, HEAD_DIM), lambda b, i: (b, tpos(i), 0, 0, 0))
    oa = pl.BlockSpec((GDN_TB, A_WIDTH), lambda b, i: (b * nt + tpos(i), 0))
    return qkv, ba, one, vec, st, oa, tpos


def gdn_fwd(ya, proj, alog, dtb, nw, nseq):
    n = ya.shape[0]
    t = n // nseq
    nc, nt, ncb = t // GDN_CHUNK, t // GDN_TB, GDN_TB // GDN_CHUNK

    def body(q_ref, k_ref, v_ref, ba_ref, alog_ref, dtb_ref, nw_ref, o_ref, st_ref, s_s):
        @pl.when(pl.program_id(1) == 0)
        def _():
            s_s[...] = jnp.zeros_like(s_s)

        def step(c, s):
            rows = pl.ds(pl.multiple_of(c * GDN_CHUNK, GDN_CHUNK), GDN_CHUNK)
            st_ref[0, c] = s
            o, s_new = _gdn_block(q_ref[rows, :], k_ref[rows, :], v_ref[rows, :], ba_ref[rows, :],
                                  alog_ref[...], dtb_ref[...], nw_ref[...], s)
            o_ref[rows, :] = o
            return s_new

        s_s[...] = lax.fori_loop(0, ncb, step, s_s[...])

    qkv, ba, one, vec, st, oa, _ = _gdn_specs(nt, False)
    return pl.pallas_call(
        body, name="gdn_fwd", grid=(nseq, nt),
        in_specs=qkv + [ba, one, one, vec],
        out_specs=[oa, st],
        out_shape=[jax.ShapeDtypeStruct((n, A_WIDTH), F32),
                   jax.ShapeDtypeStruct((nseq, nc, N_HEADS_A, HEAD_DIM, HEAD_DIM), F32)],
        scratch_shapes=[pltpu.VMEM((N_HEADS_A, HEAD_DIM, HEAD_DIM), F32)],
        compiler_params=_cparams("parallel", "arbitrary"),
    )(ya, ya, ya, proj, alog, dtb, nw)


def gdn_bwd(ya, proj, alog, dtb, nw, states, do, nseq):
    n = ya.shape[0]
    t = n // nseq
    nt, ncb = t // GDN_TB, GDN_TB // GDN_CHUNK
    nh = N_HEADS_A

    def body(q_ref, k_ref, v_ref, ba_ref, alog_ref, dtb_ref, nw_ref, st_ref, do_ref,
             dya_ref, dba_ref, dalog_ref, ddtb_ref, dnw_ref, ds_s):
        @pl.when(pl.program_id(1) == 0)
        def _():
            ds_s[...] = jnp.zeros_like(ds_s)
            dalog_ref[...] = jnp.zeros_like(dalog_ref)
            ddtb_ref[...] = jnp.zeros_like(ddtb_ref)
            dnw_ref[...] = jnp.zeros_like(dnw_ref)

        def step(it, carry):
            ds, dalog, ddtb, dnw = carry
            c = ncb - 1 - it
            rows = pl.ds(pl.multiple_of(c * GDN_CHUNK, GDN_CHUNK), GDN_CHUNK)
            _, vjp = jax.vjp(_gdn_block, q_ref[rows, :], k_ref[rows, :], v_ref[rows, :], ba_ref[rows, :],
                             alog_ref[...], dtb_ref[...], nw_ref[...], st_ref[0, c])
            dq, dk, dv, dba, da, dd, dn, ds = vjp((do_ref[rows, :], ds))
            dya_ref[rows, 0:A_WIDTH] = dq
            dya_ref[rows, A_WIDTH:2 * A_WIDTH] = dk
            dya_ref[rows, 2 * A_WIDTH:3 * A_WIDTH] = dv
            dba_ref[rows, :] = dba
            return ds, dalog + da, ddtb + dd, dnw + dn

        z11 = jnp.zeros((nh, 1, 1), F32)
        ds, dalog, ddtb, dnw = lax.fori_loop(0, ncb, step, (ds_s[...], z11, z11, jnp.zeros((1, HEAD_DIM), F32)))
        ds_s[...] = ds
        dalog_ref[0] += dalog
        ddtb_ref[0] += ddtb
        dnw_ref[0] += dnw

    qkv, ba, one, vec, st, oa, tpos = _gdn_specs(nt, True)
    per_seq = pl.BlockSpec((1, nh, 1, 1), lambda b, i: (b, 0, 0, 0))
    sd = jax.ShapeDtypeStruct
    return pl.pallas_call(
        body, name="gdn_bwd", grid=(nseq, nt),
        in_specs=qkv + [ba, one, one, vec, st, oa],
        out_specs=[pl.BlockSpec((GDN_TB, 3 * A_WIDTH), lambda b, i: (b * nt + tpos(i), 0)),
                   pl.BlockSpec((GDN_TB, BLOCK), lambda b, i: (b * nt + tpos(i), 0)),
                   per_seq, per_seq, pl.BlockSpec((1, 1, HEAD_DIM), lambda b, i: (b, 0, 0))],
        out_shape=[sd((n, 3 * A_WIDTH), F32), sd((n, BLOCK), F32), sd((nseq, nh, 1, 1), F32), sd((nseq, nh, 1, 1), F32),
                   sd((nseq, 1, HEAD_DIM), F32)],
        scratch_shapes=[pltpu.VMEM((nh, HEAD_DIM, HEAD_DIM), F32)],
        compiler_params=_cparams("parallel", "arbitrary"),
    )(ya, ya, ya, proj, alog, dtb, nw, states, do)


DIL_NB = tuple((SEQ // d) // BLOCK for _, d in DILATED_PAIRS)
DIL_D = tuple(d for _, d in DILATED_PAIRS)
DIL_STEPS = tuple(w // d for w, d in DILATED_PAIRS)
DIL_B = 4


def _rope_tables(t):
    half = ROPE_DIM // 2
    inv_freq = ROPE_THETA ** (-jnp.arange(half, dtype=F32) / half)
    ang = jnp.arange(t, dtype=F32)[:, None] * inv_freq[None, :]
    ones = jnp.ones((t, HEAD_DIM - ROPE_DIM), F32)
    cs = jnp.concatenate([jnp.cos(ang), jnp.cos(ang), ones], axis=1)
    sn = jnp.concatenate([jnp.sin(ang), jnp.sin(ang), 0.0 * ones], axis=1)
    i = jnp.arange(HEAD_DIM)[:, None]
    j = jnp.arange(HEAD_DIM)[None, :]
    pm = (jnp.where((j < half) & (i == j + half), -1.0, 0.0)
          + jnp.where((j >= half) & (j < ROPE_DIM) & (i == j - half), 1.0, 0.0))
    return cs, sn, pm.astype(F32)


def _dil_prep(x, w, cs, sn, pm):
    y = x * lax.rsqrt(jnp.mean(x * x, axis=-1, keepdims=True) + RMS_EPS) * w
    return y * cs + _mm32(y, pm) * sn


def _dil_tile(qn, kk, vv, mask):
    s = jnp.where(mask, _mm(qn * (HEAD_DIM ** -0.5), kk, NT), NEG)
    m = lax.stop_gradient(jnp.max(s, axis=-1, keepdims=True))
    p = jnp.exp(s - m)
    denom = jnp.sum(p, axis=-1, keepdims=True)
    return _mm(p, vv) / denom, m + jnp.log(denom)


_dil_tiles = jax.vmap(_dil_tile)


def _dil_mix(o1, o2, o3, l1, l2, l3):
    m = lax.stop_gradient(jnp.maximum(jnp.maximum(l1, l2), l3))
    e1, e2, e3 = jnp.exp(l1 - m), jnp.exp(l2 - m), jnp.exp(l3 - m)
    return (e1 * o1 + e2 * o2 + e3 * o3) / (e1 + e2 + e3)


def _dil_mask(it, g):
    nb = DIL_NB[g]
    n = it % nb
    r = it // nb
    kstart = jnp.maximum(it - 1, 0) * BLOCK
    iq = n * BLOCK + lax.broadcasted_iota(jnp.int32, (BLOCK, 1), 0)
    ik = kstart - r * (nb * BLOCK) + lax.broadcasted_iota(jnp.int32, (1, 2 * BLOCK), 1)
    mask = (ik >= 0) & (iq >= ik) & (iq - ik <= DIL_STEPS[g])
    return mask, pl.ds(pl.multiple_of(it * BLOCK, BLOCK), BLOCK), pl.ds(pl.multiple_of(kstart, BLOCK), 2 * BLOCK)


def _dil_gather(src, dst, d):
    t = src.shape[0]
    ln = t // d
    for r in range(d):
        dst[pl.ds(r * ln, ln), :] = src[pl.ds(r, ln, stride=d), :]


def _dil_scatter(src, dst, d):
    t = src.shape[0]
    ln = t // d
    for r in range(d):
        dst[pl.ds(r, ln, stride=d), :] = src[pl.ds(r * ln, ln), :]


def _half(x, h):
    return jnp.where(h == 0, x[:, :HEAD_DIM], x[:, HEAD_DIM:])


def _store_half(ref, rows, val, h):
    @pl.when(h == 0)
    def _():
        ref[rows, 0:HEAD_DIM] = val

    @pl.when(h == 1)
    def _():
        ref[rows, HEAD_DIM:2 * HEAD_DIM] = val


def _dil_forward_parts(h, q_ref, k_ref, v_ref, qw, kw, cs_ref, sn_ref, pm, qn_s, kn_s, v_s, dl_s, od_s, ld_s, on_s, ln_s):
    t = qn_s.shape[0]

    def prep(c, _):
        rows = pl.ds(pl.multiple_of(c * ROWS, ROWS), ROWS)
        qn_s[rows, :] = _dil_prep(_half(q_ref[rows, :], h), qw, cs_ref[rows, :], sn_ref[rows, :], pm)
        kn_s[rows, :] = _dil_prep(_half(k_ref[rows, :], h), kw, cs_ref[rows, :], sn_ref[rows, :], pm)
        v_s[rows, :] = _half(v_ref[rows, :], h)
        return 0

    lax.fori_loop(0, t // ROWS, prep, 0)
    for g in (1, 2):
        _dil_gather(qn_s, dl_s.at[g - 1, 0], DIL_D[g])
        _dil_gather(kn_s, dl_s.at[g - 1, 1], DIL_D[g])
        _dil_gather(v_s, dl_s.at[g - 1, 2], DIL_D[g])
    for g in range(3):
        qs = qn_s if g == 0 else dl_s.at[g - 1, 0]
        ks = kn_s if g == 0 else dl_s.at[g - 1, 1]
        vs = v_s if g == 0 else dl_s.at[g - 1, 2]

        def tiles(i, _, g=g, qs=qs, ks=ks, vs=vs):
            where = [_dil_mask(i * DIL_B + b, g) for b in range(DIL_B)]
            o, lse = _dil_tiles(jnp.stack([qs[qr, :] for _, qr, _ in where]), jnp.stack([ks[kr, :] for _, _, kr in where]),
                                jnp.stack([vs[kr, :] for _, _, kr in where]), jnp.stack([m for m, _, _ in where]))
            for b, (_, qr, _) in enumerate(where):
                od_s[g, qr, :] = o[b]
                ld_s[g, qr, :] = lse[b]
            return 0

        lax.fori_loop(0, t // BLOCK // DIL_B, tiles, 0)
    for g in (1, 2):
        _dil_scatter(od_s.at[g], on_s.at[g - 1], DIL_D[g])
        _dil_scatter(ld_s.at[g], ln_s.at[g - 1], DIL_D[g])


def _dil_scratch(t):
    return [pltpu.VMEM((t, HEAD_DIM), F32), pltpu.VMEM((t, HEAD_DIM), F32), pltpu.VMEM((t, HEAD_DIM), F32),
            pltpu.VMEM((2, 3, t, HEAD_DIM), F32),
            pltpu.VMEM((3, t, HEAD_DIM), F32), pltpu.VMEM((3, t, 1), F32),
            pltpu.VMEM((2, t, HEAD_DIM), F32), pltpu.VMEM((2, t, 1), F32)]


def _dil_specs(t):
    cb = COL_C // BLOCK
    per = C_WIDTH // BLOCK
    qkv = [pl.BlockSpec((t, BLOCK), lambda b, p, h, j=j: (b, cb + j * per + p)) for j in range(3)]
    vec = pl.BlockSpec((1, HEAD_DIM), lambda b, p, h: (0, 0))
    tab = pl.BlockSpec((t, HEAD_DIM), lambda b, p, h: (0, 0))
    mat = pl.BlockSpec((HEAD_DIM, HEAD_DIM), lambda b, p, h: (0, 0))
    pair = pl.BlockSpec((t, BLOCK), lambda b, p, h: (b, p))
    return qkv, vec, tab, mat, pair


def dil_fwd(proj, qw, kw, cs, sn, pm, nseq):
    n = proj.shape[0]
    t = n // nseq

    def body(q_ref, k_ref, v_ref, qw_ref, kw_ref, cs_ref, sn_ref, pm_ref, o_ref, qn_s, kn_s, v_s, dl_s, od_s, ld_s, on_s, ln_s):
        h = pl.program_id(2)
        _dil_forward_parts(h, q_ref, k_ref, v_ref, qw_ref[...], kw_ref[...], cs_ref, sn_ref, pm_ref[...],
                           qn_s, kn_s, v_s, dl_s, od_s, ld_s, on_s, ln_s)

        def mix(c, _):
            rows = pl.ds(pl.multiple_of(c * ROWS, ROWS), ROWS)
            _store_half(o_ref, rows, _dil_mix(od_s[0, rows, :], on_s[0, rows, :], on_s[1, rows, :],
                                              ld_s[0, rows, :], ln_s[0, rows, :], ln_s[1, rows, :]), h)
            return 0

        lax.fori_loop(0, t // ROWS, mix, 0)

    qkv, vec, tab, mat, pair = _dil_specs(t)
    return pl.pallas_call(
        body, name="dil_fwd", grid=(nseq, C_WIDTH // BLOCK, 2),
        in_specs=qkv + [vec, vec, tab, tab, mat],
        out_specs=pair,
        out_shape=jax.ShapeDtypeStruct((n, C_WIDTH), F32),
        scratch_shapes=_dil_scratch(t),
        compiler_params=_cparams("parallel", "parallel", "arbitrary"),
    )(proj, proj, proj, qw, kw, cs, sn, pm)


def dil_bwd(proj, qw, kw, cs, sn, pm, do, nseq):
    n = proj.shape[0]
    t = n // nseq
    dh = HEAD_DIM

    def body(q_ref, k_ref, v_ref, qw_ref, kw_ref, cs_ref, sn_ref, pm_ref, do_ref,
             dq_ref, dk_ref, dv_ref, dqw_ref, dkw_ref,
             qn_s, kn_s, v_s, dl_s, od_s, ld_s, on_s, ln_s, tq_s, tk_s, tv_s, dv_s):
        h = pl.program_id(2)
        qw, kw, pm = qw_ref[...], kw_ref[...], pm_ref[...]
        _dil_forward_parts(h, q_ref, k_ref, v_ref, qw, kw, cs_ref, sn_ref, pm, qn_s, kn_s, v_s, dl_s, od_s, ld_s, on_s, ln_s)

        def mix(c, _):
            rows = pl.ds(pl.multiple_of(c * ROWS, ROWS), ROWS)
            _, vjp = jax.vjp(_dil_mix, od_s[0, rows, :], on_s[0, rows, :], on_s[1, rows, :],
                             ld_s[0, rows, :], ln_s[0, rows, :], ln_s[1, rows, :])
            d1, d2, d3, e1, e2, e3 = vjp(_half(do_ref[rows, :], h))
            od_s[0, rows, :] = d1
            on_s[0, rows, :] = d2
            on_s[1, rows, :] = d3
            ld_s[0, rows, :] = e1
            ln_s[0, rows, :] = e2
            ln_s[1, rows, :] = e3
            return 0

        lax.fori_loop(0, t // ROWS, mix, 0)
        for g in (1, 2):
            _dil_gather(on_s.at[g - 1], od_s.at[g], DIL_D[g])
            _dil_gather(ln_s.at[g - 1], ld_s.at[g], DIL_D[g])
        on_s[...] = jnp.zeros_like(on_s)
        dv_s[...] = jnp.zeros_like(dv_s)
        for g in range(3):
            qs = qn_s if g == 0 else dl_s.at[g - 1, 0]
            ks = kn_s if g == 0 else dl_s.at[g - 1, 1]
            vs = v_s if g == 0 else dl_s.at[g - 1, 2]
            gq = on_s.at[0] if g == 0 else tq_s
            gk = on_s.at[1] if g == 0 else tk_s
            gv = dv_s if g == 0 else tv_s
            if g > 0:
                tk_s[...] = jnp.zeros_like(tk_s)
                tv_s[...] = jnp.zeros_like(tv_s)

            def tiles(i, _, g=g, qs=qs, ks=ks, vs=vs, gq=gq, gk=gk, gv=gv):
                where = [_dil_mask(i * DIL_B + b, g) for b in range(DIL_B)]
                masks = jnp.stack([m for m, _, _ in where])
                _, vjp = jax.vjp(lambda q_, k_, v_: _dil_tiles(q_, k_, v_, masks),
                                 jnp.stack([qs[qr, :] for _, qr, _ in where]), jnp.stack([ks[kr, :] for _, _, kr in where]),
                                 jnp.stack([vs[kr, :] for _, _, kr in where]))
                dq, dkk, dvv = vjp((jnp.stack([od_s[g, qr, :] for _, qr, _ in where]),
                                    jnp.stack([ld_s[g, qr, :] for _, qr, _ in where])))
                for b, (_, qr, kr) in enumerate(where):
                    gq[qr, :] = dq[b]
                    gk[kr, :] += dkk[b]
                    gv[kr, :] += dvv[b]
                return 0

            lax.fori_loop(0, t // BLOCK // DIL_B, tiles, 0)
            if g > 0:
                d = DIL_D[g]
                ln = t // d
                for r in range(d):
                    nat, dil = pl.ds(r, ln, stride=d), pl.ds(r * ln, ln)
                    on_s[0, nat, :] += tq_s[dil, :]
                    on_s[1, nat, :] += tk_s[dil, :]
                    dv_s[nat, :] += tv_s[dil, :]

        def prep(c, acc):
            rows = pl.ds(pl.multiple_of(c * ROWS, ROWS), ROWS)
            f = lambda x, w: _dil_prep(x, w, cs_ref[rows, :], sn_ref[rows, :], pm)
            _, vq = jax.vjp(f, _half(q_ref[rows, :], h), qw)
            _, vk = jax.vjp(f, _half(k_ref[rows, :], h), kw)
            dq, dqw = vq(on_s[0, rows, :])
            dk, dkw = vk(on_s[1, rows, :])
            _store_half(dq_ref, rows, dq, h)
            _store_half(dk_ref, rows, dk, h)
            _store_half(dv_ref, rows, dv_s[rows, :], h)
            return acc[0] + dqw, acc[1] + dkw

        dqw, dkw = lax.fori_loop(0, t // ROWS, prep, (jnp.zeros((1, dh), F32), jnp.zeros((1, dh), F32)))
        dqw_ref[0] = dqw
        dkw_ref[0] = dkw

    qkv, vec, tab, mat, pair = _dil_specs(t)
    per = C_WIDTH // BLOCK
    wout = pl.BlockSpec((1, 1, dh), lambda b, p, h: ((b * per + p) * 2 + h, 0, 0))
    return pl.pallas_call(
        body, name="dil_bwd", grid=(nseq, per, 2),
        in_specs=qkv + [vec, vec, tab, tab, mat, pair],
        out_specs=[pair, pair, pair, wout, wout],
        out_shape=[jax.ShapeDtypeStruct((n, C_WIDTH), F32)] * 3 + [jax.ShapeDtypeStruct((nseq * N_HEADS_C, 1, dh), F32)] * 2,
        scratch_shapes=_dil_scratch(t) + [pltpu.VMEM((t, dh), F32)] * 4,
        compiler_params=_cparams("parallel", "parallel", "arbitrary"),
    )(proj, proj, proj, qw, kw, cs, sn, pm, do)


N_CHIPS = 4
SUM_ROWS = 432
MESH_IDS = pl.DeviceIdType.MESH
ANY = pl.BlockSpec(memory_space=pl.ANY)


def plane_exchange(src, all_to_all):
    blk_shape = src.shape[1:] if all_to_all else src.shape

    def body(src_ref, out_ref, send_sems, recv_sems, local_sem):
        x, y, c = lax.axis_index("x"), lax.axis_index("y"), lax.axis_index("c")
        me = 2 * x + y
        mine = pltpu.make_async_copy(src_ref.at[me] if all_to_all else src_ref, out_ref.at[me], local_sem)
        mine.start()
        sends = []
        for k in (1, 2, 3):
            px = 1 - x if k & 2 else x
            py = 1 - y if k & 1 else y
            peer = 2 * px + py
            cp = pltpu.make_async_remote_copy(
                src_ref=src_ref.at[peer] if all_to_all else src_ref, dst_ref=out_ref.at[me],
                send_sem=send_sems.at[k - 1], recv_sem=recv_sems.at[k - 1],
                device_id=(px, py, c), device_id_type=MESH_IDS)
            cp.start()
            sends.append((cp, peer, (px, py, c)))
        for k, (cp, peer, dev) in enumerate(sends):
            pltpu.make_async_remote_copy(
                src_ref=out_ref.at[me], dst_ref=out_ref.at[peer],
                send_sem=send_sems.at[k], recv_sem=recv_sems.at[k],
                device_id=dev, device_id_type=MESH_IDS).wait_recv()
        for cp, _, _ in sends:
            cp.wait_send()
        mine.wait()

    return pl.pallas_call(
        body, name="plane_all_to_all" if all_to_all else "plane_all_gather",
        in_specs=[ANY], out_specs=ANY,
        out_shape=jax.ShapeDtypeStruct((N_CHIPS,) + blk_shape, src.dtype),
        scratch_shapes=[pltpu.SemaphoreType.DMA((3,)), pltpu.SemaphoreType.DMA((3,)), pltpu.SemaphoreType.DMA],
    )(src)


def sibling_swap(src):
    def body(src_ref, out_ref, send_sem, recv_sem):
        x, y, c = lax.axis_index("x"), lax.axis_index("y"), lax.axis_index("c")
        cp = pltpu.make_async_remote_copy(src_ref=src_ref, dst_ref=out_ref, send_sem=send_sem, recv_sem=recv_sem,
                                          device_id=(x, y, 1 - c), device_id_type=MESH_IDS)
        cp.start()
        cp.wait()

    return pl.pallas_call(
        body, name="sibling_swap", in_specs=[ANY], out_specs=ANY,
        out_shape=jax.ShapeDtypeStruct(src.shape, src.dtype),
        scratch_shapes=[pltpu.SemaphoreType.DMA, pltpu.SemaphoreType.DMA],
    )(src)


def sum4(a):
    _, r, c = a.shape
    tr = SUM_ROWS

    def body(a_ref, o_ref):
        o_ref[...] = (a_ref[0] + a_ref[1]) + (a_ref[2] + a_ref[3])

    return pl.pallas_call(
        body, name="sum4", grid=(r // tr,),
        in_specs=[pl.BlockSpec((N_CHIPS, tr, c), lambda i: (0, i, 0))],
        out_specs=pl.BlockSpec((tr, c), lambda i: (i, 0)),
        out_shape=jax.ShapeDtypeStruct((r, c), F32),
        compiler_params=_cparams("parallel"),
    )(a)


def add2(a, b):
    r, c = a.shape
    tr = SUM_ROWS

    def body(a_ref, b_ref, o_ref):
        o_ref[...] = a_ref[...] + b_ref[...]

    blk = pl.BlockSpec((tr, c), lambda i: (i, 0))
    return pl.pallas_call(
        body, name="add2", grid=(r // tr,), in_specs=[blk, blk], out_specs=blk,
        out_shape=jax.ShapeDtypeStruct((r, c), F32), compiler_params=_cparams("parallel"),
    )(a, b)


PACK_COLS = 1152
PACK_ROWS = 2592
ROW_TILE = 16


def _pack(parts):
    blocks = []
    for p in parts:
        p2 = p.reshape(-1, p.shape[-1])
        blocks.append(jnp.pad(p2, ((0, -p2.shape[0] % ROW_TILE), (0, PACK_COLS - p2.shape[1]))))
    rows = sum(b.shape[0] for b in blocks)
    blocks.append(jnp.zeros((PACK_ROWS - rows, PACK_COLS), blocks[0].dtype))
    return jnp.concatenate(blocks)


def _unpack(buf, shapes):
    out, at = [], 0
    for s in shapes:
        rows = math.prod(s[:-1])
        out.append(buf[at:at + rows, :s[-1]].reshape(s))
        at += rows + (-rows % ROW_TILE)
    return out


def _pack_small(g):
    blk = jnp.zeros((ROW_TILE, PACK_COLS), F32)
    for i, k in enumerate(SMALL):
        blk = blk.at[2 * i:2 * i + 2, :g[k].shape[1]].set(g[k])
    return blk


def _unpack_small(blk, shapes):
    return [blk[2 * i:2 * i + 2, :s[1]] for i, s in enumerate(shapes)]


def _layer_fwd(x, p, nseq, tabs):
    proj, hdn = inproj_fwd(x, p["norm_w"][None], p["w_in"])
    ya = conv_fwd(proj, p["conv_w"], nseq)
    oa, states = gdn_fwd(ya, proj, p["a_log"].reshape(N_HEADS_A, 1, 1), p["dt_bias"].reshape(N_HEADS_A, 1, 1),
                         p["gdn_norm_w"][None], nseq)
    ob, carries = sb_fwd(proj, nseq)
    oc = dil_fwd(proj, p["q_norm_w"][None], p["k_norm_w"][None], *tabs, nseq)
    y, mixed = outproj_fwd(x, oa, ob, oc, proj, p["w_out"])
    return y, dict(x=x, hdn=hdn, proj=proj, ya=ya, states=states, carries=carries, oa=oa, ob=ob, oc=oc, mixed=mixed)


def _layer_bwd(dy, p, res, nseq, tabs):
    proj = res["proj"]
    g = {}
    g["w_out"] = mat_tn(res["mixed"], [dy])[0]
    doa, dob, doc, dza, dzb, dzc = outproj_bwd(dy, res["oa"], res["ob"], res["oc"], proj, p["w_out"])
    dqc, dkc, dvc, dqw, dkw = dil_bwd(proj, p["q_norm_w"][None], p["k_norm_w"][None], *tabs, doc, nseq)
    g["q_norm_w"], g["k_norm_w"] = dqw.sum((0, 1)), dkw.sum((0, 1))
    dqb, dkb, dvb = sb_bwd(proj, res["carries"], dob, nseq)
    dya, dba, dalog, ddtb, dnw = gdn_bwd(res["ya"], proj, p["a_log"].reshape(N_HEADS_A, 1, 1),
                                         p["dt_bias"].reshape(N_HEADS_A, 1, 1), p["gdn_norm_w"][None], res["states"], doa, nseq)
    g["a_log"], g["dt_bias"], g["gdn_norm_w"] = dalog.sum(0).reshape(-1), ddtb.sum(0).reshape(-1), dnw.sum((0, 1))
    dqkv, dcw = conv_bwd(proj, p["conv_w"], dya, nseq)
    g["conv_w"] = dcw.sum(0)
    slabs = [dqkv, dza, dqc, dkc, dvc, dzc, dqb, dkb, dvb, dzb, dba]
    hdn = res["hdn"]
    g["w_in"] = jnp.concatenate(mat_tn(hdn, slabs[:2]) + mat_tn(hdn, slabs[2:6]) + mat_tn(hdn, slabs[6:]), axis=1)
    dx, dnw_tiles = inproj_bwd(slabs, p["w_in"], res["x"], p["norm_w"][None], dy)
    g["norm_w"] = dnw_tiles.sum((0, 1))
    return dx, g


SMALL = ("norm_w", "a_log", "dt_bias", "gdn_norm_w", "q_norm_w", "k_norm_w")


def _local_step(x, target, full):
    nseq, t, d = x.shape
    tabs = _rope_tables(t)
    h = x.reshape(nseq * t, d)
    saved = []
    for l in range(DEPTH):
        p = {k: v[l] for k, v in full.items()}
        h, res = _layer_fwd(h, p, nseq, tabs)
        saved.append((p, res))
    dy, parts = loss_fwd_bwd(h, target.reshape(nseq * t, d))
    loss = parts[:, 0, 0].sum()
    grads = [None] * DEPTH
    for l in reversed(range(DEPTH)):
        p, res = saved[l]
        dy, grads[l] = _layer_bwd(dy, p, res, nseq, tabs)
    return loss, dy.reshape(nseq, t, d), {k: jnp.stack([g[k] for g in grads]) for k in grads[0]}


def _pad_cols(w):
    b0 = ORIG_A + ORIG_BA
    c0 = b0 + ORIG_B
    zeros = jnp.zeros(w.shape[:-1] + (BLOCK - ORIG_BA,), w.dtype)
    return jnp.concatenate([w[..., :ORIG_A], w[..., c0:], w[..., b0:c0], w[..., ORIG_A:b0], zeros], axis=-1)


def _unpad_cols(w):
    return jnp.concatenate([w[..., :COL_C], w[..., COL_BA:COL_BA + ORIG_BA], w[..., COL_B:COL_BA], w[..., COL_C:COL_B]],
                           axis=-1)


def kernel(x, norm_w, w_in, conv_w, a_log, dt_bias, gdn_norm_w, q_norm_w, k_norm_w, w_out, loss_target, m_norm_w, m_w_in, m_conv_w, m_a_log, m_dt_bias, m_gdn_norm_w, m_q_norm_w, m_k_norm_w, m_w_out, v_norm_w, v_w_in, v_conv_w, v_a_log, v_dt_bias, v_gdn_norm_w, v_q_norm_w, v_k_norm_w, v_w_out):
    weights = dict(norm_w=norm_w, w_in=w_in, conv_w=conv_w, a_log=a_log, dt_bias=dt_bias, gdn_norm_w=gdn_norm_w,
                   q_norm_w=q_norm_w, k_norm_w=k_norm_w, w_out=w_out)
    moms = dict(norm_w=m_norm_w, w_in=m_w_in, conv_w=m_conv_w, a_log=m_a_log, dt_bias=m_dt_bias,
                gdn_norm_w=m_gdn_norm_w, q_norm_w=m_q_norm_w, k_norm_w=m_k_norm_w, w_out=m_w_out)
    vars_ = dict(norm_w=v_norm_w, w_in=v_w_in, conv_w=v_conv_w, a_log=v_a_log, dt_bias=v_dt_bias,
                 gdn_norm_w=v_gdn_norm_w, q_norm_w=v_q_norm_w, k_norm_w=v_k_norm_w, w_out=v_w_out)
    names = list(weights)
    sharded = ("w_in", "w_out", "conv_w")
    shard_shapes = [weights[k].shape for k in sharded]

    conv_bits = lax.bitcast_convert_type(conv_w, BF16).reshape(conv_w.shape[:2] + (2 * conv_w.shape[2],))
    got = plane_exchange(_pack([w_in.astype(BF16), w_out.astype(BF16), conv_bits]), all_to_all=False)
    per_chip = [_unpack(got[i], shard_shapes[:2] + [conv_bits.shape]) for i in range(N_CHIPS)]
    full = {k: weights[k] for k in SMALL}
    full["w_in"] = _pad_cols(jnp.concatenate([pc[0] for pc in per_chip], axis=2))
    full["w_out"] = jnp.concatenate([pc[1] for pc in per_chip], axis=1)
    full["conv_w"] = jnp.concatenate(
        [lax.bitcast_convert_type(pc[2].reshape(conv_w.shape + (2,)), F32) for pc in per_chip], axis=2)

    loss, grad_x, g = _local_step(x, loss_target, full)

    gw_in = _unpad_cols(g["w_in"])
    cols, rows = w_in.shape[2], w_out.shape[1]
    small = _pack_small(g)
    send = jnp.stack([_pack([gw_in[:, :, i * cols:(i + 1) * cols], g["w_out"][:, i * rows:(i + 1) * rows],
                             g["conv_w"][:, :, i * conv_w.shape[2]:(i + 1) * conv_w.shape[2]], small])
                      for i in range(N_CHIPS)])
    c = lax.axis_index("c")
    half = PACK_ROWS // 2
    keep = lax.dynamic_slice_in_dim(send, c * half, half, axis=1)
    give = lax.dynamic_slice_in_dim(send, (1 - c) * half, half, axis=1)
    chip_sum = add2(keep.reshape(N_CHIPS * half, PACK_COLS), sibling_swap(give).reshape(N_CHIPS * half, PACK_COLS))
    mine = sum4(plane_exchange(chip_sum.reshape(N_CHIPS, half, PACK_COLS), all_to_all=True))
    other = sibling_swap(mine)
    total = jnp.concatenate([jnp.where(c == 0, mine, other), jnp.where(c == 0, other, mine)])
    reduced = _unpack(total, shard_shapes + [(ROW_TILE, PACK_COLS)])
    grads = dict(zip(sharded, reduced[:3]))
    grads.update(zip(SMALL, _unpack_small(reduced[3], [weights[k].shape for k in SMALL])))
    loss = lax.psum(loss, ("x", "y", "c"))

    def two_d(a):
        return a.reshape(-1, a.shape[-1])

    delta, new_m, new_v = {}, {}, {}
    for k in names:
        d_, m_, v_ = adamw(two_d(weights[k]), two_d(grads[k]), two_d(moms[k]), two_d(vars_[k]))
        delta[k], new_m[k], new_v[k] = (a.reshape(weights[k].shape) for a in (d_, m_, v_))
    return (loss, grad_x, *[grads[k] for k in names], *[delta[k] for k in names],
            *[new_m[k] for k in names], *[new_v[k] for k in names])
```

```python
import functools
import math

import jax
import jax.numpy as jnp
from jax import lax
from jax.experimental import pallas as pl
from jax.experimental.pallas import tpu as pltpu

F32 = jnp.float32
BF16 = jnp.bfloat16

D_MODEL = 1024
SEQ = 2048
DEPTH = 2
HEAD_DIM = 64
N_HEADS_A, N_HEADS_B, N_HEADS_C = 6, 4, 6
A_WIDTH, B_WIDTH, C_WIDTH = N_HEADS_A * HEAD_DIM, N_HEADS_B * HEAD_DIM, N_HEADS_C * HEAD_DIM
CONV_WIDTH = 4
GDN_CHUNK = 64
BLOCK = 128
ROPE_DIM = 16
ROPE_THETA = 500000.0
DILATED_PAIRS = ((128, 1), (512, 4), (2048, 16))
RMS_EPS = 1e-6
NEG = -1e30

NT = (((1,), (1,)), ((), ()))
NN = (((1,), (0,)), ((), ()))
TN = (((0,), (0,)), ((), ()))

VMEM_LIMIT = 48 * 1024 * 1024

ORIG_A = 4 * A_WIDTH
ORIG_BA = 2 * N_HEADS_A
ORIG_B = 4 * B_WIDTH
COL_AZ = 3 * A_WIDTH
COL_C = 4 * A_WIDTH
COL_B = COL_C + 4 * C_WIDTH
COL_BA = COL_B + 4 * B_WIDTH
P_COLS = COL_BA + BLOCK
TN_COLS = 384
INPROJ_COLS = P_COLS // 3
TM_ROWS = 512
ROWS = 256


def _mm(a, b, dims=NN):
    return lax.dot_general(a.astype(BF16), b.astype(BF16), dims, preferred_element_type=F32)


def _mm32(a, b, dims=NN):
    return lax.dot_general(a, b, dims, precision=lax.Precision.HIGH, preferred_element_type=F32)


def _cparams(*sem):
    return pltpu.CompilerParams(dimension_semantics=sem, vmem_limit_bytes=VMEM_LIMIT)


def _sigmoid(x):
    return 0.5 * (jnp.tanh(0.5 * x) + 1.0)


def _softplus(x):
    return jnp.maximum(x, 0.0) + jnp.log(1.0 + jnp.exp(-jnp.abs(x)))


def _rms(x, w):
    return x * lax.rsqrt(jnp.mean(x * x, axis=-1, keepdims=True) + RMS_EPS) * w


def _heads(a, n):
    return jnp.stack([a[:, h * HEAD_DIM:(h + 1) * HEAD_DIM] for h in range(n)])


def _unheads(a):
    return jnp.concatenate([a[h] for h in range(a.shape[0])], axis=1)


def _row_chunks(t):
    return [pl.ds(c * ROWS, ROWS) for c in range(t // ROWS)]


def inproj_fwd(x, nw, w):
    n, d = x.shape
    p = w.shape[1]

    def body(x_ref, nw_ref, w_ref, proj_ref, hdn_ref):
        @pl.when(pl.program_id(1) == 0)
        def _():
            hdn_ref[...] = _rms(x_ref[...], nw_ref[...]).astype(BF16)

        proj_ref[...] = jnp.dot(hdn_ref[...], w_ref[...], preferred_element_type=F32)

    return pl.pallas_call(
        body, name="inproj_fwd", grid=(n // TM_ROWS, p // INPROJ_COLS),
        in_specs=[pl.BlockSpec((TM_ROWS, d), lambda i, j: (i, 0)), pl.BlockSpec((1, d), lambda i, j: (0, 0)),
                  pl.BlockSpec((d, INPROJ_COLS), lambda i, j: (0, j))],
        out_specs=[pl.BlockSpec((TM_ROWS, INPROJ_COLS), lambda i, j: (i, j)), pl.BlockSpec((TM_ROWS, d), lambda i, j: (i, 0))],
        out_shape=[jax.ShapeDtypeStruct((n, p), F32), jax.ShapeDtypeStruct((n, d), BF16)],
        compiler_params=_cparams("parallel", "arbitrary"),
    )(x, nw, w)


def mat_tn(a, slabs):
    n, ka = a.shape
    ns = len(slabs)

    def body(*refs):
        a_ref, s_refs, o_refs = refs[0], refs[1:1 + ns], refs[1 + ns:]

        @pl.when(pl.program_id(0) == 0)
        def _():
            for o_ref in o_refs:
                o_ref[...] = jnp.zeros_like(o_ref)

        av = a_ref[...]
        for s_ref, o_ref in zip(s_refs, o_refs):
            o_ref[...] += lax.dot_general(av, s_ref[...].astype(BF16), TN, preferred_element_type=F32)

    return pl.pallas_call(
        body, name="mat_tn", grid=(n // TM_ROWS,),
        in_specs=[pl.BlockSpec((TM_ROWS, ka), lambda k: (k, 0))]
                 + [pl.BlockSpec((TM_ROWS, s.shape[1]), lambda k: (k, 0)) for s in slabs],
        out_specs=[pl.BlockSpec((ka, s.shape[1]), lambda k: (0, 0)) for s in slabs],
        out_shape=[jax.ShapeDtypeStruct((ka, s.shape[1]), F32) for s in slabs],
        compiler_params=_cparams("arbitrary"),
    )(a, *slabs)


def inproj_bwd(slabs, w, x, nw, dy):
    n, d = x.shape
    p = w.shape[1]
    tm = 256
    ns = len(slabs)

    def body(*refs):
        s_refs = refs[:ns]
        w_ref, x_ref, nw_ref, dy_ref, dx_ref, dnw_ref = refs[ns:]
        dh = jnp.zeros((tm, d), F32)
        at = 0
        for s_ref in s_refs:
            wd = s_ref.shape[1]
            dh = dh + lax.dot_general(s_ref[...].astype(BF16), w_ref[:, at:at + wd], NT, preferred_element_type=F32)
            at += wd
        _, vjp = jax.vjp(_rms, x_ref[...], nw_ref[...])
        dx, dnw = vjp(dh)
        dx_ref[...] = dx + dy_ref[...]
        dnw_ref[0] = dnw

    return pl.pallas_call(
        body, name="inproj_bwd", grid=(n // tm,),
        in_specs=[pl.BlockSpec((tm, s.shape[1]), lambda i: (i, 0)) for s in slabs]
                 + [pl.BlockSpec((d, p), lambda i: (0, 0)), pl.BlockSpec((tm, d), lambda i: (i, 0)),
                    pl.BlockSpec((1, d), lambda i: (0, 0)), pl.BlockSpec((tm, d), lambda i: (i, 0))],
        out_specs=[pl.BlockSpec((tm, d), lambda i: (i, 0)), pl.BlockSpec((1, 1, d), lambda i: (i, 0, 0))],
        out_shape=[jax.ShapeDtypeStruct((n, d), F32), jax.ShapeDtypeStruct((n // tm, 1, d), F32)],
        compiler_params=_cparams("parallel"),
    )(*slabs, w, x, nw, dy)


CONV_PAD = 8
CONV_ROWS = 256


def _conv_pre(pad_s, cw, c):
    xs = [pad_s[pl.ds(c * CONV_ROWS + CONV_PAD - (CONV_WIDTH - 1) + k, CONV_ROWS), :] for k in range(CONV_WIDTH)]
    pre = xs[0] * cw[0:1, :]
    for k in range(1, CONV_WIDTH):
        pre = pre + xs[k] * cw[k:k + 1, :]
    return pre, xs


def conv_fwd(proj, cw, nseq):
    n = proj.shape[0]
    t = n // nseq
    ch = cw.shape[1]

    def body(x_ref, cw_ref, y_ref, pad_s):
        pad_s[pl.ds(0, CONV_PAD), :] = jnp.zeros((CONV_PAD, TN_COLS), F32)
        pad_s[pl.ds(CONV_PAD, t), :] = x_ref[...]
        cwv = cw_ref[...]
        for c in range(t // CONV_ROWS):
            pre, _ = _conv_pre(pad_s, cwv, c)
            y_ref[pl.ds(c * CONV_ROWS, CONV_ROWS), :] = pre * _sigmoid(pre)

    return pl.pallas_call(
        body, name="conv_fwd", grid=(nseq, ch // TN_COLS),
        in_specs=[pl.BlockSpec((t, TN_COLS), lambda b, j: (b, j)), pl.BlockSpec((CONV_WIDTH, TN_COLS), lambda b, j: (0, j))],
        out_specs=pl.BlockSpec((t, TN_COLS), lambda b, j: (b, j)),
        out_shape=jax.ShapeDtypeStruct((n, ch), F32),
        scratch_shapes=[pltpu.VMEM((t + CONV_PAD, TN_COLS), F32)],
        compiler_params=_cparams("parallel", "parallel"),
    )(proj, cw)


def conv_bwd(proj, cw, dy, nseq):
    n = proj.shape[0]
    t = n // nseq
    ch = cw.shape[1]

    def body(x_ref, cw_ref, dy_ref, dx_ref, dcw_ref, pad_s, dpad_s):
        pad_s[pl.ds(0, CONV_PAD), :] = jnp.zeros((CONV_PAD, TN_COLS), F32)
        pad_s[pl.ds(CONV_PAD, t), :] = x_ref[...]
        dpad_s[pl.ds(t, CONV_PAD), :] = jnp.zeros((CONV_PAD, TN_COLS), F32)
        cwv = cw_ref[...]
        acc = [jnp.zeros((1, TN_COLS), F32)] * CONV_WIDTH
        for c in range(t // CONV_ROWS):
            pre, xs = _conv_pre(pad_s, cwv, c)
            sg = _sigmoid(pre)
            dpre = dy_ref[pl.ds(c * CONV_ROWS, CONV_ROWS), :] * (sg * (1.0 + pre * (1.0 - sg)))
            dpad_s[pl.ds(c * CONV_ROWS, CONV_ROWS), :] = dpre
            acc = [acc[k] + jnp.sum(dpre * xs[k], axis=0, keepdims=True) for k in range(CONV_WIDTH)]
        for k in range(CONV_WIDTH):
            dcw_ref[0, pl.ds(k, 1), :] = acc[k]
        for c in range(t // CONV_ROWS):
            dx = dpad_s[pl.ds(c * CONV_ROWS + CONV_WIDTH - 1, CONV_ROWS), :] * cwv[0:1, :]
            for k in range(1, CONV_WIDTH):
                dx = dx + dpad_s[pl.ds(c * CONV_ROWS + CONV_WIDTH - 1 - k, CONV_ROWS), :] * cwv[k:k + 1, :]
            dx_ref[pl.ds(c * CONV_ROWS, CONV_ROWS), :] = dx

    blk = pl.BlockSpec((t, TN_COLS), lambda b, j: (b, j))
    return pl.pallas_call(
        body, name="conv_bwd", grid=(nseq, ch // TN_COLS),
        in_specs=[blk, pl.BlockSpec((CONV_WIDTH, TN_COLS), lambda b, j: (0, j)), blk],
        out_specs=[blk, pl.BlockSpec((1, CONV_WIDTH, TN_COLS), lambda b, j: (b, 0, j))],
        out_shape=[jax.ShapeDtypeStruct((n, ch), F32), jax.ShapeDtypeStruct((nseq, CONV_WIDTH, ch), F32)],
        scratch_shapes=[pltpu.VMEM((t + CONV_PAD, TN_COLS), F32)] * 2,
        compiler_params=_cparams("parallel", "parallel"),
    )(proj, cw, dy)


def _gate_specs(d):
    wide = pl.BlockSpec((TM_ROWS, d), lambda i: (i, 0))
    oa = pl.BlockSpec((TM_ROWS, A_WIDTH), lambda i: (i, 0))
    ob = pl.BlockSpec((TM_ROWS, B_WIDTH), lambda i: (i, 0))
    oc = pl.BlockSpec((TM_ROWS, C_WIDTH), lambda i: (i, 0))
    za = pl.BlockSpec((TM_ROWS, A_WIDTH), lambda i: (i, COL_AZ // A_WIDTH))
    zb = pl.BlockSpec((TM_ROWS, B_WIDTH), lambda i: (i, (COL_B + 3 * B_WIDTH) // B_WIDTH))
    zc = pl.BlockSpec((TM_ROWS, C_WIDTH), lambda i: (i, (COL_C + 3 * C_WIDTH) // C_WIDTH))
    return wide, oa, ob, oc, za, zb, zc


BRANCH_COLS = ((0, A_WIDTH), (A_WIDTH, A_WIDTH + B_WIDTH), (A_WIDTH + B_WIDTH, D_MODEL))


def outproj_fwd(x, oa, ob, oc, proj, w):
    n, d = x.shape

    def body(x_ref, oa_ref, ob_ref, oc_ref, za_ref, zb_ref, zc_ref, w_ref, y_ref, m_ref):
        for (lo, hi), o_ref, z_ref in zip(BRANCH_COLS, (oa_ref, ob_ref, oc_ref), (za_ref, zb_ref, zc_ref)):
            zv = z_ref[...]
            m_ref[:, lo:hi] = (o_ref[...] * (zv * _sigmoid(zv))).astype(BF16)
        y_ref[...] = x_ref[...] + jnp.dot(m_ref[...], w_ref[...], preferred_element_type=F32)

    wide, sa, sb, sc, za, zb, zc = _gate_specs(d)
    return pl.pallas_call(
        body, name="outproj_fwd", grid=(n // TM_ROWS,),
        in_specs=[wide, sa, sb, sc, za, zb, zc, pl.BlockSpec((d, d), lambda i: (0, 0))],
        out_specs=[wide, wide],
        out_shape=[jax.ShapeDtypeStruct((n, d), F32), jax.ShapeDtypeStruct((n, d), BF16)],
        compiler_params=_cparams("parallel"),
    )(x, oa, ob, oc, proj, proj, proj, w)


def outproj_bwd(dy, oa, ob, oc, proj, w):
    n, d = dy.shape

    def body(dy_ref, oa_ref, ob_ref, oc_ref, za_ref, zb_ref, zc_ref, w_ref, doa_ref, dob_ref, doc_ref, dza_ref, dzb_ref, dzc_ref):
        dm = lax.dot_general(dy_ref[...].astype(BF16), w_ref[...], NT, preferred_element_type=F32)
        for (lo, hi), o_ref, z_ref, do_ref, dz_ref in zip(BRANCH_COLS, (oa_ref, ob_ref, oc_ref), (za_ref, zb_ref, zc_ref),
                                                          (doa_ref, dob_ref, doc_ref), (dza_ref, dzb_ref, dzc_ref)):
            zv = z_ref[...]
            sg = _sigmoid(zv)
            dmv = dm[:, lo:hi]
            do_ref[...] = dmv * (zv * sg)
            dz_ref[...] = dmv * o_ref[...] * (sg * (1.0 + zv * (1.0 - sg)))

    wide, sa, sb, sc, za, zb, zc = _gate_specs(d)
    sd = jax.ShapeDtypeStruct
    outs = [sd((n, A_WIDTH), F32), sd((n, B_WIDTH), F32), sd((n, C_WIDTH), F32)]
    return pl.pallas_call(
        body, name="outproj_bwd", grid=(n // TM_ROWS,),
        in_specs=[wide, sa, sb, sc, za, zb, zc, pl.BlockSpec((d, d), lambda i: (0, 0))],
        out_specs=[sa, sb, sc, sa, sb, sc],
        out_shape=outs + outs,
        compiler_params=_cparams("parallel"),
    )(dy, oa, ob, oc, proj, proj, proj, w)


def loss_fwd_bwd(y, target):
    n, d = y.shape

    def body(y_ref, t_ref, dy_ref, part_ref):
        e = y_ref[...] - t_ref[...]
        dy_ref[...] = e * (1.0 / d)
        part_ref[...] = jnp.zeros_like(part_ref) + 0.5 * jnp.sum(e * e) * (1.0 / d)

    blk = pl.BlockSpec((TM_ROWS, d), lambda i: (i, 0))
    return pl.pallas_call(
        body, name="loss", grid=(n // TM_ROWS,),
        in_specs=[blk, blk],
        out_specs=[blk, pl.BlockSpec((1, 8, BLOCK), lambda i: (i, 0, 0))],
        out_shape=[jax.ShapeDtypeStruct((n, d), F32), jax.ShapeDtypeStruct((n // TM_ROWS, 8, BLOCK), F32)],
        compiler_params=_cparams("parallel"),
    )(y, target)


ADAM_LR, ADAM_B1, ADAM_B2, ADAM_EPS, ADAM_WD, ADAM_STEP = 0.001, 0.9, 0.999, 1e-08, 0.01, 10


def adamw(w, g, m, v):
    r, c = w.shape
    tr = r if r <= 256 else 256

    def body(w_ref, g_ref, m_ref, v_ref, d_ref, nm_ref, nv_ref):
        gv = g_ref[...]
        nm = ADAM_B1 * m_ref[...] + (1.0 - ADAM_B1) * gv
        nv = ADAM_B2 * v_ref[...] + (1.0 - ADAM_B2) * (gv * gv)
        m_hat = nm / (1.0 - ADAM_B1 ** ADAM_STEP)
        v_hat = nv / (1.0 - ADAM_B2 ** ADAM_STEP)
        d_ref[...] = -ADAM_LR * (m_hat / (jnp.sqrt(v_hat) + ADAM_EPS) + ADAM_WD * w_ref[...])
        nm_ref[...] = nm
        nv_ref[...] = nv

    blk = pl.BlockSpec((tr, c), lambda i: (i, 0))
    return pl.pallas_call(
        body, name="adamw", grid=(r // tr,),
        in_specs=[blk] * 4, out_specs=[blk] * 3,
        out_shape=[jax.ShapeDtypeStruct((r, c), F32)] * 3,
        compiler_params=_cparams("parallel"),
    )(w, g, m, v)


SB_G = N_HEADS_B


def _sb_tile(q, k, v, carry, qpos, kpos, tri):
    z = _mm(q * (HEAD_DIM ** -0.5), k, NT)
    earlier = kpos < qpos
    sp = jnp.log(1.0 + jnp.exp(-jnp.abs(z)))
    ls_pos = jnp.minimum(z, 0.0) - sp
    ls_neg = jnp.minimum(-z, 0.0) - sp
    log_keep = jnp.where(earlier, ls_neg, 0.0)
    hi = log_keep.astype(BF16)
    lo = lax.stop_gradient(log_keep - hi.astype(F32)).astype(BF16)
    within = lax.dot_general(jnp.concatenate([hi, lo], axis=1), tri, NN, preferred_element_type=F32)
    wts = jnp.where(earlier, jnp.exp(jnp.where(earlier, ls_pos + within + carry, 0.0)), 0.0)
    return _mm(wts, v), jnp.sum(log_keep, axis=1, keepdims=True)


_sb_tiles = jax.vmap(_sb_tile, in_axes=(0, 0, 0, 0, None, None, None))


def _sb_consts():
    qi = lax.broadcasted_iota(jnp.int32, (BLOCK, 1), 0)
    kj = lax.broadcasted_iota(jnp.int32, (1, BLOCK), 1)
    r = lax.broadcasted_iota(jnp.int32, (2 * BLOCK, BLOCK), 0) % BLOCK
    c = lax.broadcasted_iota(jnp.int32, (2 * BLOCK, BLOCK), 1)
    tri = jnp.where(r > c, 1.0, 0.0).astype(BF16)
    return qi, kj, tri


def _sb_specs(t, nq):
    cb = COL_B // B_WIDTH
    q = pl.BlockSpec((BLOCK, B_WIDTH), lambda b, i: (b * nq + i, cb))
    k = pl.BlockSpec((t, B_WIDTH), lambda b, i: (b, cb + 1))
    v = pl.BlockSpec((t, B_WIDTH), lambda b, i: (b, cb + 2))
    blk = pl.BlockSpec((BLOCK, B_WIDTH), lambda b, i: (b * nq + i, 0))
    full = pl.BlockSpec((t, B_WIDTH), lambda b, i: (b, 0))
    carry = pl.BlockSpec((1, 1, nq, BLOCK, SB_G), lambda b, i: (b, i, 0, 0, 0))
    return q, k, v, blk, full, carry


def sb_fwd(proj, nseq):
    n = proj.shape[0]
    t = n // nseq
    nq = t // BLOCK
    g = SB_G

    def body(q_ref, k_ref, v_ref, o_ref, carry_ref):
        i = pl.program_id(1)
        qi, kj, tri = _sb_consts()
        qv = _heads(q_ref[...], g)
        qpos = i * BLOCK + qi

        def step(it, st):
            o_acc, c = st
            j = i - it
            rows = pl.ds(pl.multiple_of(j * BLOCK, BLOCK), BLOCK)
            carry_ref[0, 0, j] = jnp.concatenate([c[h] for h in range(g)], axis=1)
            o, tot = _sb_tiles(qv, _heads(k_ref[rows, :], g), _heads(v_ref[rows, :], g), c, qpos, j * BLOCK + kj, tri)
            return o_acc + o, c + tot

        o_acc, _ = lax.fori_loop(0, i + 1, step, (jnp.zeros((g, BLOCK, HEAD_DIM), F32), jnp.zeros((g, BLOCK, 1), F32)))
        o_ref[...] = _unheads(o_acc)

    q, k, v, blk, _, carry = _sb_specs(t, nq)
    return pl.pallas_call(
        body, name="sb_fwd", grid=(nseq, nq),
        in_specs=[q, k, v],
        out_specs=[blk, carry],
        out_shape=[jax.ShapeDtypeStruct((n, B_WIDTH), F32),
                   jax.ShapeDtypeStruct((nseq, nq, nq, BLOCK, g), F32)],
        compiler_params=_cparams("parallel", "arbitrary"),
    )(proj, proj, proj)


def sb_bwd(proj, carries, do, nseq):
    n = proj.shape[0]
    t = n // nseq
    nq = t // BLOCK
    g = SB_G

    def body(q_ref, k_ref, v_ref, carry_ref, do_ref, dq_ref, dk_ref, dv_ref):
        i = pl.program_id(1)

        @pl.when(i == 0)
        def _():
            dk_ref[...] = jnp.zeros_like(dk_ref)
            dv_ref[...] = jnp.zeros_like(dv_ref)

        qi, kj, tri = _sb_consts()
        qv = _heads(q_ref[...], g)
        dov = _heads(do_ref[...], g)
        qpos = i * BLOCK + qi

        def step(j, st):
            dq_acc, dc = st
            rows = pl.ds(pl.multiple_of(j * BLOCK, BLOCK), BLOCK)
            kpos = j * BLOCK + kj
            cj = carry_ref[0, 0, j]
            f = lambda q_, k_, v_, c_: _sb_tiles(q_, k_, v_, c_, qpos, kpos, tri)
            _, vjp = jax.vjp(f, qv, _heads(k_ref[rows, :], g), _heads(v_ref[rows, :], g),
                             jnp.stack([cj[:, h:h + 1] for h in range(g)]))
            dq, dk, dv, dcj = vjp((dov, dc))
            dk_ref[rows, :] += _unheads(dk)
            dv_ref[rows, :] += _unheads(dv)
            return dq_acc + dq, dc + dcj

        dq_acc, _ = lax.fori_loop(0, i + 1, step, (jnp.zeros((g, BLOCK, HEAD_DIM), F32), jnp.zeros((g, BLOCK, 1), F32)))
        dq_ref[...] = _unheads(dq_acc)

    q, k, v, blk, full, carry = _sb_specs(t, nq)
    return pl.pallas_call(
        body, name="sb_bwd", grid=(nseq, nq),
        in_specs=[q, k, v, carry, blk],
        out_specs=[blk, full, full],
        out_shape=[jax.ShapeDtypeStruct((n, B_WIDTH), F32)] * 3,
        compiler_params=_cparams("parallel", "arbitrary"),
    )(proj, proj, proj, carries, do)


def _gdn_chunk(q, k, v, al_c, al_r, br_c, alog, dtb, nw, s):
    c = GDN_CHUNK
    ri = lax.broadcasted_iota(jnp.int32, (c, c), 0)
    ci = lax.broadcasted_iota(jnp.int32, (c, c), 1)
    incl, strict = ri >= ci, ri > ci
    eye = jnp.where(ri == ci, 1.0, 0.0).astype(F32)
    rate = -jnp.exp(alog)
    g_c = rate * _softplus(al_c + dtb)
    g_r = rate * _softplus(al_r + dtb)
    beta = _sigmoid(br_c)
    gc_c = jnp.sum(jnp.where(incl, g_r, 0.0), axis=1, keepdims=True)
    gc_r = jnp.sum(jnp.where(ri <= ci, g_c, 0.0), axis=0, keepdims=True)
    gl = jnp.sum(g_r, axis=1, keepdims=True)
    decay = jnp.where(incl, jnp.exp(jnp.where(incl, gc_c - gc_r, 0.0)), 0.0)
    qn = q * lax.rsqrt(jnp.sum(q * q, axis=-1, keepdims=True) + RMS_EPS) * (HEAD_DIM ** -0.5)
    kn = k * lax.rsqrt(jnp.sum(k * k, axis=-1, keepdims=True) + RMS_EPS)
    kb = kn * beta
    a = jnp.where(strict, _mm(kb, kn, NT) * decay, 0.0)
    tmat = eye - a
    p = a
    for _ in range(5):
        p = _mm32(p, p)
        tmat = tmat + _mm32(tmat, p)
    u = _mm(tmat, v * beta)
    w = _mm(tmat, kb * jnp.exp(gc_c))
    qk = _mm(qn, kn, NT) * decay
    v_new = u - _mm(w, s)
    o = _mm(qn * jnp.exp(gc_c), s) + _mm(qk, v_new)
    s_new = s * jnp.exp(gl) + _mm(kn * jnp.exp(gl - gc_c), v_new, TN)
    o = o * lax.rsqrt(jnp.mean(o * o, axis=-1, keepdims=True) + RMS_EPS) * nw
    return o, s_new


_gdn_chunks = jax.vmap(_gdn_chunk, in_axes=(0, 0, 0, 0, 0, 0, 0, 0, None, 0))

GDN_TB = 256


def _gdn_block(q3, k3, v3, ba, alog, dtb, nw, s):
    nh = N_HEADS_A
    bat = ba.T
    br_c = jnp.stack([ba[:, h:h + 1] for h in range(nh)])
    al_c = jnp.stack([ba[:, nh + h:nh + h + 1] for h in range(nh)])
    al_r = jnp.stack([bat[nh + h:nh + h + 1, :] for h in range(nh)])
    o, s_new = _gdn_chunks(_heads(q3, nh), _heads(k3, nh), _heads(v3, nh), al_c, al_r, br_c, alog, dtb, nw, s)
    return _unheads(o), s_new


def _gdn_specs(nt, rev):
    tpos = (lambda i: nt - 1 - i) if rev else (lambda i: i)
    ncb = GDN_TB // GDN_CHUNK
    qkv = [pl.BlockSpec((GDN_TB, A_WIDTH), lambda b, i, j=j: (b * nt + tpos(i), j)) for j in range(3)]
    ba = pl.BlockSpec((GDN_TB, BLOCK), lambda b, i: (b * nt + tpos(i), COL_BA // BLOCK))
    one = pl.BlockSpec((N_HEADS_A, 1, 1), lambda b, i: (0, 0, 0))
    vec = pl.BlockSpec((1, HEAD_DIM), lambda b, i: (0, 0))
    st = pl.BlockSpec((1, ncb, N_HEADS_A, HEAD_DIM, HEAD_DIM), lambda b, i: (b, tpos(i), 0, 0, 0))
    oa = pl.BlockSpec((GDN_TB, A_WIDTH), lambda b, i: (b * nt + tpos(i), 0))
    return qkv, ba, one, vec, st, oa, tpos


def gdn_fwd(ya, proj, alog, dtb, nw, nseq):
    n = ya.shape[0]
    t = n // nseq
    nc, nt, ncb = t // GDN_CHUNK, t // GDN_TB, GDN_TB // GDN_CHUNK

    def body(q_ref, k_ref, v_ref, ba_ref, alog_ref, dtb_ref, nw_ref, o_ref, st_ref, s_s):
        @pl.when(pl.program_id(1) == 0)
        def _():
            s_s[...] = jnp.zeros_like(s_s)

        def step(c, s):
            rows = pl.ds(pl.multiple_of(c * GDN_CHUNK, GDN_CHUNK), GDN_CHUNK)
            st_ref[0, c] = s
            o, s_new = _gdn_block(q_ref[rows, :], k_ref[rows, :], v_ref[rows, :], ba_ref[rows, :],
                                  alog_ref[...], dtb_ref[...], nw_ref[...], s)
            o_ref[rows, :] = o
            return s_new

        s_s[...] = lax.fori_loop(0, ncb, step, s_s[...])

    qkv, ba, one, vec, st, oa, _ = _gdn_specs(nt, False)
    return pl.pallas_call(
        body, name="gdn_fwd", grid=(nseq, nt),
        in_specs=qkv + [ba, one, one, vec],
        out_specs=[oa, st],
        out_shape=[jax.ShapeDtypeStruct((n, A_WIDTH), F32),
                   jax.ShapeDtypeStruct((nseq, nc, N_HEADS_A, HEAD_DIM, HEAD_DIM), F32)],
        scratch_shapes=[pltpu.VMEM((N_HEADS_A, HEAD_DIM, HEAD_DIM), F32)],
        compiler_params=_cparams("parallel", "arbitrary"),
    )(ya, ya, ya, proj, alog, dtb, nw)


def gdn_bwd(ya, proj, alog, dtb, nw, states, do, nseq):
    n = ya.shape[0]
    t = n // nseq
    nt, ncb = t // GDN_TB, GDN_TB // GDN_CHUNK
    nh = N_HEADS_A

    def body(q_ref, k_ref, v_ref, ba_ref, alog_ref, dtb_ref, nw_ref, st_ref, do_ref,
             dya_ref, dba_ref, dalog_ref, ddtb_ref, dnw_ref, ds_s):
        @pl.when(pl.program_id(1) == 0)
        def _():
            ds_s[...] = jnp.zeros_like(ds_s)
            dalog_ref[...] = jnp.zeros_like(dalog_ref)
            ddtb_ref[...] = jnp.zeros_like(ddtb_ref)
            dnw_ref[...] = jnp.zeros_like(dnw_ref)

        def step(it, carry):
            ds, dalog, ddtb, dnw = carry
            c = ncb - 1 - it
            rows = pl.ds(pl.multiple_of(c * GDN_CHUNK, GDN_CHUNK), GDN_CHUNK)
            _, vjp = jax.vjp(_gdn_block, q_ref[rows, :], k_ref[rows, :], v_ref[rows, :], ba_ref[rows, :],
                             alog_ref[...], dtb_ref[...], nw_ref[...], st_ref[0, c])
            dq, dk, dv, dba, da, dd, dn, ds = vjp((do_ref[rows, :], ds))
            dya_ref[rows, 0:A_WIDTH] = dq
            dya_ref[rows, A_WIDTH:2 * A_WIDTH] = dk
            dya_ref[rows, 2 * A_WIDTH:3 * A_WIDTH] = dv
            dba_ref[rows, :] = dba
            return ds, dalog + da, ddtb + dd, dnw + dn

        z11 = jnp.zeros((nh, 1, 1), F32)
        ds, dalog, ddtb, dnw = lax.fori_loop(0, ncb, step, (ds_s[...], z11, z11, jnp.zeros((1, HEAD_DIM), F32)))
        ds_s[...] = ds
        dalog_ref[0] += dalog
        ddtb_ref[0] += ddtb
        dnw_ref[0] += dnw

    qkv, ba, one, vec, st, oa, tpos = _gdn_specs(nt, True)
    per_seq = pl.BlockSpec((1, nh, 1, 1), lambda b, i: (b, 0, 0, 0))
    sd = jax.ShapeDtypeStruct
    return pl.pallas_call(
        body, name="gdn_bwd", grid=(nseq, nt),
        in_specs=qkv + [ba, one, one, vec, st, oa],
        out_specs=[pl.BlockSpec((GDN_TB, 3 * A_WIDTH), lambda b, i: (b * nt + tpos(i), 0)),
                   pl.BlockSpec((GDN_TB, BLOCK), lambda b, i: (b * nt + tpos(i), 0)),
                   per_seq, per_seq, pl.BlockSpec((1, 1, HEAD_DIM), lambda b, i: (b, 0, 0))],
        out_shape=[sd((n, 3 * A_WIDTH), F32), sd((n, BLOCK), F32), sd((nseq, nh, 1, 1), F32), sd((nseq, nh, 1, 1), F32),
                   sd((nseq, 1, HEAD_DIM), F32)],
        scratch_shapes=[pltpu.VMEM((nh, HEAD_DIM, HEAD_DIM), F32)],
        compiler_params=_cparams("parallel", "arbitrary"),
    )(ya, ya, ya, proj, alog, dtb, nw, states, do)


DIL_NB = tuple((SEQ // d) // BLOCK for _, d in DILATED_PAIRS)
DIL_D = tuple(d for _, d in DILATED_PAIRS)
DIL_STEPS = tuple(w // d for w, d in DILATED_PAIRS)
DIL_B = 4


def _rope_tables(t):
    half = ROPE_DIM // 2
    inv_freq = ROPE_THETA ** (-jnp.arange(half, dtype=F32) / half)
    ang = jnp.arange(t, dtype=F32)[:, None] * inv_freq[None, :]
    ones = jnp.ones((t, HEAD_DIM - ROPE_DIM), F32)
    cs = jnp.concatenate([jnp.cos(ang), jnp.cos(ang), ones], axis=1)
    sn = jnp.concatenate([jnp.sin(ang), jnp.sin(ang), 0.0 * ones], axis=1)
    i = jnp.arange(HEAD_DIM)[:, None]
    j = jnp.arange(HEAD_DIM)[None, :]
    pm = (jnp.where((j < half) & (i == j + half), -1.0, 0.0)
          + jnp.where((j >= half) & (j < ROPE_DIM) & (i == j - half), 1.0, 0.0))
    return cs, sn, jnp.concatenate([pm, pm]).astype(BF16)


def _dil_prep(x, w, cs, sn, pm):
    y = x * lax.rsqrt(jnp.mean(x * x, axis=-1, keepdims=True) + RMS_EPS) * w
    hi = y.astype(BF16)
    lo = lax.stop_gradient(y - hi.astype(F32)).astype(BF16)
    return y * cs + lax.dot_general(jnp.concatenate([hi, lo], axis=1), pm, NN, preferred_element_type=F32) * sn


def _dil_tile(qn, kk, vv, mask):
    s = jnp.where(mask, _mm(qn * (HEAD_DIM ** -0.5), kk, NT), NEG)
    m = lax.stop_gradient(jnp.max(s, axis=-1, keepdims=True))
    p = jnp.exp(s - m)
    denom = jnp.sum(p, axis=-1, keepdims=True)
    return _mm(p, vv) / denom, m + jnp.log(denom)


_dil_tiles = jax.vmap(_dil_tile)


def _dil_mix(o1, o2, o3, l1, l2, l3):
    m = lax.stop_gradient(jnp.maximum(jnp.maximum(l1, l2), l3))
    e1, e2, e3 = jnp.exp(l1 - m), jnp.exp(l2 - m), jnp.exp(l3 - m)
    return (e1 * o1 + e2 * o2 + e3 * o3) / (e1 + e2 + e3)


def _dil_mask(it, g):
    nb = DIL_NB[g]
    n = it % nb
    r = it // nb
    kstart = jnp.maximum(it - 1, 0) * BLOCK
    iq = n * BLOCK + lax.broadcasted_iota(jnp.int32, (BLOCK, 1), 0)
    ik = kstart - r * (nb * BLOCK) + lax.broadcasted_iota(jnp.int32, (1, 2 * BLOCK), 1)
    mask = (ik >= 0) & (iq >= ik) & (iq - ik <= DIL_STEPS[g])
    return mask, pl.ds(pl.multiple_of(it * BLOCK, BLOCK), BLOCK), pl.ds(pl.multiple_of(kstart, BLOCK), 2 * BLOCK)


def _dil_gather(src, dst, d):
    t = src.shape[0]
    ln = t // d
    for r in range(d):
        dst[pl.ds(r * ln, ln), :] = src[pl.ds(r, ln, stride=d), :]


def _dil_scatter(src, dst, d):
    t = src.shape[0]
    ln = t // d
    for r in range(d):
        dst[pl.ds(r, ln, stride=d), :] = src[pl.ds(r * ln, ln), :]


def _half(x, h):
    return jnp.where(h == 0, x[:, :HEAD_DIM], x[:, HEAD_DIM:])


def _store_half(ref, rows, val, h):
    @pl.when(h == 0)
    def _():
        ref[rows, 0:HEAD_DIM] = val

    @pl.when(h == 1)
    def _():
        ref[rows, HEAD_DIM:2 * HEAD_DIM] = val


def _dil_forward_parts(h, q_ref, k_ref, v_ref, qw, kw, cs_ref, sn_ref, pm, qn_s, kn_s, v_s, dl_s, od_s, ld_s, on_s, ln_s):
    t = qn_s.shape[0]

    def prep(c, _):
        rows = pl.ds(pl.multiple_of(c * ROWS, ROWS), ROWS)
        qn_s[rows, :] = _dil_prep(_half(q_ref[rows, :], h), qw, cs_ref[rows, :], sn_ref[rows, :], pm)
        kn_s[rows, :] = _dil_prep(_half(k_ref[rows, :], h), kw, cs_ref[rows, :], sn_ref[rows, :], pm)
        v_s[rows, :] = _half(v_ref[rows, :], h)
        return 0

    lax.fori_loop(0, t // ROWS, prep, 0)
    for g in (1, 2):
        _dil_gather(qn_s, dl_s.at[g - 1, 0], DIL_D[g])
        _dil_gather(kn_s, dl_s.at[g - 1, 1], DIL_D[g])
        _dil_gather(v_s, dl_s.at[g - 1, 2], DIL_D[g])
    for g in range(3):
        qs = qn_s if g == 0 else dl_s.at[g - 1, 0]
        ks = kn_s if g == 0 else dl_s.at[g - 1, 1]
        vs = v_s if g == 0 else dl_s.at[g - 1, 2]

        def tiles(i, _, g=g, qs=qs, ks=ks, vs=vs):
            where = [_dil_mask(i * DIL_B + b, g) for b in range(DIL_B)]
            o, lse = _dil_tiles(jnp.stack([qs[qr, :] for _, qr, _ in where]), jnp.stack([ks[kr, :] for _, _, kr in where]),
                                jnp.stack([vs[kr, :] for _, _, kr in where]), jnp.stack([m for m, _, _ in where]))
            for b, (_, qr, _) in enumerate(where):
                od_s[g, qr, :] = o[b]
                ld_s[g, qr, :] = lse[b]
            return 0

        lax.fori_loop(0, t // BLOCK // DIL_B, tiles, 0)
    for g in (1, 2):
        _dil_scatter(od_s.at[g], on_s.at[g - 1], DIL_D[g])
        _dil_scatter(ld_s.at[g], ln_s.at[g - 1], DIL_D[g])


def _dil_scratch(t):
    return [pltpu.VMEM((t, HEAD_DIM), F32), pltpu.VMEM((t, HEAD_DIM), F32), pltpu.VMEM((t, HEAD_DIM), F32),
            pltpu.VMEM((2, 3, t, HEAD_DIM), F32),
            pltpu.VMEM((3, t, HEAD_DIM), F32), pltpu.VMEM((3, t, 1), F32),
            pltpu.VMEM((2, t, HEAD_DIM), F32), pltpu.VMEM((2, t, 1), F32)]


def _dil_specs(t):
    cb = COL_C // BLOCK
    per = C_WIDTH // BLOCK
    qkv = [pl.BlockSpec((t, BLOCK), lambda b, p, h, j=j: (b, cb + j * per + p)) for j in range(3)]
    vec = pl.BlockSpec((1, HEAD_DIM), lambda b, p, h: (0, 0))
    tab = pl.BlockSpec((t, HEAD_DIM), lambda b, p, h: (0, 0))
    mat = pl.BlockSpec((2 * HEAD_DIM, HEAD_DIM), lambda b, p, h: (0, 0))
    pair = pl.BlockSpec((t, BLOCK), lambda b, p, h: (b, p))
    return qkv, vec, tab, mat, pair


def dil_fwd(proj, qw, kw, cs, sn, pm, nseq):
    n = proj.shape[0]
    t = n // nseq

    def body(q_ref, k_ref, v_ref, qw_ref, kw_ref, cs_ref, sn_ref, pm_ref, o_ref, qn_s, kn_s, v_s, dl_s, od_s, ld_s, on_s, ln_s):
        h = pl.program_id(2)
        _dil_forward_parts(h, q_ref, k_ref, v_ref, qw_ref[...], kw_ref[...], cs_ref, sn_ref, pm_ref[...],
                           qn_s, kn_s, v_s, dl_s, od_s, ld_s, on_s, ln_s)

        def mix(c, _):
            rows = pl.ds(pl.multiple_of(c * ROWS, ROWS), ROWS)
            _store_half(o_ref, rows, _dil_mix(od_s[0, rows, :], on_s[0, rows, :], on_s[1, rows, :],
                                              ld_s[0, rows, :], ln_s[0, rows, :], ln_s[1, rows, :]), h)
            return 0

        lax.fori_loop(0, t // ROWS, mix, 0)

    qkv, vec, tab, mat, pair = _dil_specs(t)
    return pl.pallas_call(
        body, name="dil_fwd", grid=(nseq, C_WIDTH // BLOCK, 2),
        in_specs=qkv + [vec, vec, tab, tab, mat],
        out_specs=pair,
        out_shape=jax.ShapeDtypeStruct((n, C_WIDTH), F32),
        scratch_shapes=_dil_scratch(t),
        compiler_params=_cparams("parallel", "parallel", "arbitrary"),
    )(proj, proj, proj, qw, kw, cs, sn, pm)


def dil_bwd(proj, qw, kw, cs, sn, pm, do, nseq):
    n = proj.shape[0]
    t = n // nseq
    dh = HEAD_DIM

    def body(q_ref, k_ref, v_ref, qw_ref, kw_ref, cs_ref, sn_ref, pm_ref, do_ref,
             dq_ref, dk_ref, dv_ref, dqw_ref, dkw_ref,
             qn_s, kn_s, v_s, dl_s, od_s, ld_s, on_s, ln_s, tq_s, tk_s, tv_s, dv_s):
        h = pl.program_id(2)
        qw, kw, pm = qw_ref[...], kw_ref[...], pm_ref[...]
        _dil_forward_parts(h, q_ref, k_ref, v_ref, qw, kw, cs_ref, sn_ref, pm, qn_s, kn_s, v_s, dl_s, od_s, ld_s, on_s, ln_s)

        def mix(c, _):
            rows = pl.ds(pl.multiple_of(c * ROWS, ROWS), ROWS)
            _, vjp = jax.vjp(_dil_mix, od_s[0, rows, :], on_s[0, rows, :], on_s[1, rows, :],
                             ld_s[0, rows, :], ln_s[0, rows, :], ln_s[1, rows, :])
            d1, d2, d3, e1, e2, e3 = vjp(_half(do_ref[rows, :], h))
            od_s[0, rows, :] = d1
            on_s[0, rows, :] = d2
            on_s[1, rows, :] = d3
            ld_s[0, rows, :] = e1
            ln_s[0, rows, :] = e2
            ln_s[1, rows, :] = e3
            return 0

        lax.fori_loop(0, t // ROWS, mix, 0)
        for g in (1, 2):
            _dil_gather(on_s.at[g - 1], od_s.at[g], DIL_D[g])
            _dil_gather(ln_s.at[g - 1], ld_s.at[g], DIL_D[g])
        on_s[...] = jnp.zeros_like(on_s)
        dv_s[...] = jnp.zeros_like(dv_s)
        for g in range(3):
            qs = qn_s if g == 0 else dl_s.at[g - 1, 0]
            ks = kn_s if g == 0 else dl_s.at[g - 1, 1]
            vs = v_s if g == 0 else dl_s.at[g - 1, 2]
            gq = on_s.at[0] if g == 0 else tq_s
            gk = on_s.at[1] if g == 0 else tk_s
            gv = dv_s if g == 0 else tv_s
            if g > 0:
                tk_s[...] = jnp.zeros_like(tk_s)
                tv_s[...] = jnp.zeros_like(tv_s)

            def tiles(i, _, g=g, qs=qs, ks=ks, vs=vs, gq=gq, gk=gk, gv=gv):
                where = [_dil_mask(i * DIL_B + b, g) for b in range(DIL_B)]
                masks = jnp.stack([m for m, _, _ in where])
                _, vjp = jax.vjp(lambda q_, k_, v_: _dil_tiles(q_, k_, v_, masks),
                                 jnp.stack([qs[qr, :] for _, qr, _ in where]), jnp.stack([ks[kr, :] for _, _, kr in where]),
                                 jnp.stack([vs[kr, :] for _, _, kr in where]))
                dq, dkk, dvv = vjp((jnp.stack([od_s[g, qr, :] for _, qr, _ in where]),
                                    jnp.stack([ld_s[g, qr, :] for _, qr, _ in where])))
                for b, (_, qr, kr) in enumerate(where):
                    gq[qr, :] = dq[b]
                    gk[kr, :] += dkk[b]
                    gv[kr, :] += dvv[b]
                return 0

            lax.fori_loop(0, t // BLOCK // DIL_B, tiles, 0)
            if g > 0:
                d = DIL_D[g]
                ln = t // d
                for r in range(d):
                    nat, dil = pl.ds(r, ln, stride=d), pl.ds(r * ln, ln)
                    on_s[0, nat, :] += tq_s[dil, :]
                    on_s[1, nat, :] += tk_s[dil, :]
                    dv_s[nat, :] += tv_s[dil, :]

        def prep(c, acc):
            rows = pl.ds(pl.multiple_of(c * ROWS, ROWS), ROWS)
            f = lambda x, w: _dil_prep(x, w, cs_ref[rows, :], sn_ref[rows, :], pm)
            _, vq = jax.vjp(f, _half(q_ref[rows, :], h), qw)
            _, vk = jax.vjp(f, _half(k_ref[rows, :], h), kw)
            dq, dqw = vq(on_s[0, rows, :])
            dk, dkw = vk(on_s[1, rows, :])
            _store_half(dq_ref, rows, dq, h)
            _store_half(dk_ref, rows, dk, h)
            _store_half(dv_ref, rows, dv_s[rows, :], h)
            return acc[0] + dqw, acc[1] + dkw

        dqw, dkw = lax.fori_loop(0, t // ROWS, prep, (jnp.zeros((1, dh), F32), jnp.zeros((1, dh), F32)))
        dqw_ref[0] = dqw
        dkw_ref[0] = dkw

    qkv, vec, tab, mat, pair = _dil_specs(t)
    per = C_WIDTH // BLOCK
    wout = pl.BlockSpec((1, 1, dh), lambda b, p, h: ((b * per + p) * 2 + h, 0, 0))
    return pl.pallas_call(
        body, name="dil_bwd", grid=(nseq, per, 2),
        in_specs=qkv + [vec, vec, tab, tab, mat, pair],
        out_specs=[pair, pair, pair, wout, wout],
        out_shape=[jax.ShapeDtypeStruct((n, C_WIDTH), F32)] * 3 + [jax.ShapeDtypeStruct((nseq * N_HEADS_C, 1, dh), F32)] * 2,
        scratch_shapes=_dil_scratch(t) + [pltpu.VMEM((t, dh), F32)] * 4,
        compiler_params=_cparams("parallel", "parallel", "arbitrary"),
    )(proj, proj, proj, qw, kw, cs, sn, pm, do)


N_CHIPS = 4
SUM_ROWS = 432
MESH_IDS = pl.DeviceIdType.MESH
ANY = pl.BlockSpec(memory_space=pl.ANY)


def plane_exchange(src, all_to_all):
    blk_shape = src.shape[1:] if all_to_all else src.shape

    def body(src_ref, out_ref, send_sems, recv_sems, local_sem):
        x, y, c = lax.axis_index("x"), lax.axis_index("y"), lax.axis_index("c")
        me = 2 * x + y
        mine = pltpu.make_async_copy(src_ref.at[me] if all_to_all else src_ref, out_ref.at[me], local_sem)
        mine.start()
        sends = []
        for k in (1, 2, 3):
            px = 1 - x if k & 2 else x
            py = 1 - y if k & 1 else y
            peer = 2 * px + py
            cp = pltpu.make_async_remote_copy(
                src_ref=src_ref.at[peer] if all_to_all else src_ref, dst_ref=out_ref.at[me],
                send_sem=send_sems.at[k - 1], recv_sem=recv_sems.at[k - 1],
                device_id=(px, py, c), device_id_type=MESH_IDS)
            cp.start()
            sends.append((cp, peer, (px, py, c)))
        for k, (cp, peer, dev) in enumerate(sends):
            pltpu.make_async_remote_copy(
                src_ref=out_ref.at[me], dst_ref=out_ref.at[peer],
                send_sem=send_sems.at[k], recv_sem=recv_sems.at[k],
                device_id=dev, device_id_type=MESH_IDS).wait_recv()
        for cp, _, _ in sends:
            cp.wait_send()
        mine.wait()

    return pl.pallas_call(
        body, name="plane_all_to_all" if all_to_all else "plane_all_gather",
        in_specs=[ANY], out_specs=ANY,
        out_shape=jax.ShapeDtypeStruct((N_CHIPS,) + blk_shape, src.dtype),
        scratch_shapes=[pltpu.SemaphoreType.DMA((3,)), pltpu.SemaphoreType.DMA((3,)), pltpu.SemaphoreType.DMA],
    )(src)


def sibling_swap(src):
    def body(src_ref, out_ref, send_sem, recv_sem):
        x, y, c = lax.axis_index("x"), lax.axis_index("y"), lax.axis_index("c")
        cp = pltpu.make_async_remote_copy(src_ref=src_ref, dst_ref=out_ref, send_sem=send_sem, recv_sem=recv_sem,
                                          device_id=(x, y, 1 - c), device_id_type=MESH_IDS)
        cp.start()
        cp.wait()

    return pl.pallas_call(
        body, name="sibling_swap", in_specs=[ANY], out_specs=ANY,
        out_shape=jax.ShapeDtypeStruct(src.shape, src.dtype),
        scratch_shapes=[pltpu.SemaphoreType.DMA, pltpu.SemaphoreType.DMA],
    )(src)


def sum4(a):
    _, r, c = a.shape
    tr = SUM_ROWS

    def body(a_ref, o_ref):
        o_ref[...] = (a_ref[0] + a_ref[1]) + (a_ref[2] + a_ref[3])

    return pl.pallas_call(
        body, name="sum4", grid=(r // tr,),
        in_specs=[pl.BlockSpec((N_CHIPS, tr, c), lambda i: (0, i, 0))],
        out_specs=pl.BlockSpec((tr, c), lambda i: (i, 0)),
        out_shape=jax.ShapeDtypeStruct((r, c), F32),
        compiler_params=_cparams("parallel"),
    )(a)


def add2(a, b):
    r, c = a.shape
    tr = SUM_ROWS

    def body(a_ref, b_ref, o_ref):
        o_ref[...] = a_ref[...] + b_ref[...]

    blk = pl.BlockSpec((tr, c), lambda i: (i, 0))
    return pl.pallas_call(
        body, name="add2", grid=(r // tr,), in_specs=[blk, blk], out_specs=blk,
        out_shape=jax.ShapeDtypeStruct((r, c), F32), compiler_params=_cparams("parallel"),
    )(a, b)


PACK_COLS = 1152
PACK_ROWS = 2592
ROW_TILE = 16


def _pack(parts):
    blocks = []
    for p in parts:
        p2 = p.reshape(-1, p.shape[-1])
        blocks.append(jnp.pad(p2, ((0, -p2.shape[0] % ROW_TILE), (0, PACK_COLS - p2.shape[1]))))
    rows = sum(b.shape[0] for b in blocks)
    blocks.append(jnp.zeros((PACK_ROWS - rows, PACK_COLS), blocks[0].dtype))
    return jnp.concatenate(blocks)


def _unpack(buf, shapes):
    out, at = [], 0
    for s in shapes:
        rows = math.prod(s[:-1])
        out.append(buf[at:at + rows, :s[-1]].reshape(s))
        at += rows + (-rows % ROW_TILE)
    return out


def _pack_small(g):
    blk = jnp.zeros((ROW_TILE, PACK_COLS), F32)
    for i, k in enumerate(SMALL):
        blk = blk.at[2 * i:2 * i + 2, :g[k].shape[1]].set(g[k])
    return blk


def _unpack_small(blk, shapes):
    return [blk[2 * i:2 * i + 2, :s[1]] for i, s in enumerate(shapes)]


def _layer_fwd(x, p, nseq, tabs):
    proj, hdn = inproj_fwd(x, p["norm_w"][None], p["w_in"])
    ya = conv_fwd(proj, p["conv_w"], nseq)
    oa, states = gdn_fwd(ya, proj, p["a_log"].reshape(N_HEADS_A, 1, 1), p["dt_bias"].reshape(N_HEADS_A, 1, 1),
                         p["gdn_norm_w"][None], nseq)
    ob, carries = sb_fwd(proj, nseq)
    oc = dil_fwd(proj, p["q_norm_w"][None], p["k_norm_w"][None], *tabs, nseq)
    y, mixed = outproj_fwd(x, oa, ob, oc, proj, p["w_out"])
    return y, dict(x=x, hdn=hdn, proj=proj, ya=ya, states=states, carries=carries, oa=oa, ob=ob, oc=oc, mixed=mixed)


def _layer_bwd(dy, p, res, nseq, tabs):
    proj = res["proj"]
    g = {}
    g["w_out"] = mat_tn(res["mixed"], [dy])[0]
    doa, dob, doc, dza, dzb, dzc = outproj_bwd(dy, res["oa"], res["ob"], res["oc"], proj, p["w_out"])
    dqc, dkc, dvc, dqw, dkw = dil_bwd(proj, p["q_norm_w"][None], p["k_norm_w"][None], *tabs, doc, nseq)
    g["q_norm_w"], g["k_norm_w"] = dqw.sum((0, 1)), dkw.sum((0, 1))
    dqb, dkb, dvb = sb_bwd(proj, res["carries"], dob, nseq)
    dya, dba, dalog, ddtb, dnw = gdn_bwd(res["ya"], proj, p["a_log"].reshape(N_HEADS_A, 1, 1),
                                         p["dt_bias"].reshape(N_HEADS_A, 1, 1), p["gdn_norm_w"][None], res["states"], doa, nseq)
    g["a_log"], g["dt_bias"], g["gdn_norm_w"] = dalog.sum(0).reshape(-1), ddtb.sum(0).reshape(-1), dnw.sum((0, 1))
    dqkv, dcw = conv_bwd(proj, p["conv_w"], dya, nseq)
    g["conv_w"] = dcw.sum(0)
    slabs = [dqkv, dza, dqc, dkc, dvc, dzc, dqb, dkb, dvb, dzb, dba]
    hdn = res["hdn"]
    g["w_in"] = jnp.concatenate(mat_tn(hdn, slabs[:2]) + mat_tn(hdn, slabs[2:6]) + mat_tn(hdn, slabs[6:]), axis=1)
    dx, dnw_tiles = inproj_bwd(slabs, p["w_in"], res["x"], p["norm_w"][None], dy)
    g["norm_w"] = dnw_tiles.sum((0, 1))
    return dx, g


SMALL = ("norm_w", "a_log", "dt_bias", "gdn_norm_w", "q_norm_w", "k_norm_w")


def _local_step(x, target, full):
    nseq, t, d = x.shape
    tabs = _rope_tables(t)
    h = x.reshape(nseq * t, d)
    saved = []
    for l in range(DEPTH):
        p = {k: v[l] for k, v in full.items()}
        h, res = _layer_fwd(h, p, nseq, tabs)
        saved.append((p, res))
    dy, parts = loss_fwd_bwd(h, target.reshape(nseq * t, d))
    loss = parts[:, 0, 0].sum()
    grads = [None] * DEPTH
    for l in reversed(range(DEPTH)):
        p, res = saved[l]
        dy, grads[l] = _layer_bwd(dy, p, res, nseq, tabs)
    return loss, dy.reshape(nseq, t, d), {k: jnp.stack([g[k] for g in grads]) for k in grads[0]}


def _pad_cols(w):
    b0 = ORIG_A + ORIG_BA
    c0 = b0 + ORIG_B
    zeros = jnp.zeros(w.shape[:-1] + (BLOCK - ORIG_BA,), w.dtype)
    return jnp.concatenate([w[..., :ORIG_A], w[..., c0:], w[..., b0:c0], w[..., ORIG_A:b0], zeros], axis=-1)


def _unpad_cols(w):
    return jnp.concatenate([w[..., :COL_C], w[..., COL_BA:COL_BA + ORIG_BA], w[..., COL_B:COL_BA], w[..., COL_C:COL_B]],
                           axis=-1)


def kernel(x, norm_w, w_in, conv_w, a_log, dt_bias, gdn_norm_w, q_norm_w, k_norm_w, w_out, loss_target, m_norm_w, m_w_in, m_conv_w, m_a_log, m_dt_bias, m_gdn_norm_w, m_q_norm_w, m_k_norm_w, m_w_out, v_norm_w, v_w_in, v_conv_w, v_a_log, v_dt_bias, v_gdn_norm_w, v_q_norm_w, v_k_norm_w, v_w_out):
    weights = dict(norm_w=norm_w, w_in=w_in, conv_w=conv_w, a_log=a_log, dt_bias=dt_bias, gdn_norm_w=gdn_norm_w,
                   q_norm_w=q_norm_w, k_norm_w=k_norm_w, w_out=w_out)
    moms = dict(norm_w=m_norm_w, w_in=m_w_in, conv_w=m_conv_w, a_log=m_a_log, dt_bias=m_dt_bias,
                gdn_norm_w=m_gdn_norm_w, q_norm_w=m_q_norm_w, k_norm_w=m_k_norm_w, w_out=m_w_out)
    vars_ = dict(norm_w=v_norm_w, w_in=v_w_in, conv_w=v_conv_w, a_log=v_a_log, dt_bias=v_dt_bias,
                 gdn_norm_w=v_gdn_norm_w, q_norm_w=v_q_norm_w, k_norm_w=v_k_norm_w, w_out=v_w_out)
    names = list(weights)
    sharded = ("w_in", "w_out", "conv_w")
    shard_shapes = [weights[k].shape for k in sharded]

    conv_bits = lax.bitcast_convert_type(conv_w, BF16).reshape(conv_w.shape[:2] + (2 * conv_w.shape[2],))
    got = plane_exchange(_pack([w_in.astype(BF16), w_out.astype(BF16), conv_bits]), all_to_all=False)
    per_chip = [_unpack(got[i], shard_shapes[:2] + [conv_bits.shape]) for i in range(N_CHIPS)]
    full = {k: weights[k] for k in SMALL}
    full["w_in"] = _pad_cols(jnp.concatenate([pc[0] for pc in per_chip], axis=2))
    full["w_out"] = jnp.concatenate([pc[1] for pc in per_chip], axis=1)
    full["conv_w"] = jnp.concatenate(
        [lax.bitcast_convert_type(pc[2].reshape(conv_w.shape + (2,)), F32) for pc in per_chip], axis=2)

    loss, grad_x, g = _local_step(x, loss_target, full)

    gw_in = _unpad_cols(g["w_in"])
    cols, rows = w_in.shape[2], w_out.shape[1]
    small = _pack_small(g)
    send = jnp.stack([_pack([gw_in[:, :, i * cols:(i + 1) * cols], g["w_out"][:, i * rows:(i + 1) * rows],
                             g["conv_w"][:, :, i * conv_w.shape[2]:(i + 1) * conv_w.shape[2]], small])
                      for i in range(N_CHIPS)])
    c = lax.axis_index("c")
    half = PACK_ROWS // 2
    keep = lax.dynamic_slice_in_dim(send, c * half, half, axis=1)
    give = lax.dynamic_slice_in_dim(send, (1 - c) * half, half, axis=1)
    chip_sum = add2(keep.reshape(N_CHIPS * half, PACK_COLS), sibling_swap(give).reshape(N_CHIPS * half, PACK_COLS))
    mine = sum4(plane_exchange(chip_sum.reshape(N_CHIPS, half, PACK_COLS), all_to_all=True))
    other = sibling_swap(mine)
    total = jnp.concatenate([jnp.where(c == 0, mine, other), jnp.where(c == 0, other, mine)])
    reduced = _unpack(total, shard_shapes + [(ROW_TILE, PACK_COLS)])
    grads = dict(zip(sharded, reduced[:3]))
    grads.update(zip(SMALL, _unpack_small(reduced[3], [weights[k].shape for k in SMALL])))
    loss = lax.psum(loss, ("x", "y", "c"))

    def two_d(a):
        return a.reshape(-1, a.shape[-1])

    delta, new_m, new_v = {}, {}, {}
    for k in names:
        d_, m_, v_ = adamw(two_d(weights[k]), two_d(grads[k]), two_d(moms[k]), two_d(vars_[k]))
        delta[k], new_m[k], new_v[k] = (a.reshape(weights[k].shape) for a in (d_, m_, v_))
    return (loss, grad_x, *[grads[k] for k in names], *[delta[k] for k in names],
            *[new_m[k] for k in names], *[new_v[k] for k in names])
```

```python
import functools
import math

import jax
import jax.numpy as jnp
from jax import lax
from jax.experimental import pallas as pl
from jax.experimental.pallas import tpu as pltpu

F32 = jnp.float32
BF16 = jnp.bfloat16

D_MODEL = 1024
SEQ = 2048
DEPTH = 2
HEAD_DIM = 64
N_HEADS_A, N_HEADS_B, N_HEADS_C = 6, 4, 6
A_WIDTH, B_WIDTH, C_WIDTH = N_HEADS_A * HEAD_DIM, N_HEADS_B * HEAD_DIM, N_HEADS_C * HEAD_DIM
CONV_WIDTH = 4
GDN_CHUNK = 64
BLOCK = 128
ROPE_DIM = 16
ROPE_THETA = 500000.0
DILATED_PAIRS = ((128, 1), (512, 4), (2048, 16))
RMS_EPS = 1e-6
NEG = -1e30

NT = (((1,), (1,)), ((), ()))
NN = (((1,), (0,)), ((), ()))
TN = (((0,), (0,)), ((), ()))

VMEM_LIMIT = 48 * 1024 * 1024

ORIG_A = 4 * A_WIDTH
ORIG_BA = 2 * N_HEADS_A
ORIG_B = 4 * B_WIDTH
COL_AZ = 3 * A_WIDTH
COL_C = 4 * A_WIDTH
COL_B = COL_C + 4 * C_WIDTH
COL_BA = COL_B + 4 * B_WIDTH
P_COLS = COL_BA + BLOCK
TN_COLS = 384
INPROJ_COLS = P_COLS // 3
TM_ROWS = 512
ROWS = 256


def _mm(a, b, dims=NN):
    return lax.dot_general(a.astype(BF16), b.astype(BF16), dims, preferred_element_type=F32)


def _mm32(a, b, dims=NN):
    return lax.dot_general(a, b, dims, precision=lax.Precision.HIGH, preferred_element_type=F32)


def _cparams(*sem):
    return pltpu.CompilerParams(dimension_semantics=sem, vmem_limit_bytes=VMEM_LIMIT)


def _sigmoid(x):
    return 0.5 * (jnp.tanh(0.5 * x) + 1.0)


def _softplus(x):
    return jnp.maximum(x, 0.0) + jnp.log(1.0 + jnp.exp(-jnp.abs(x)))


def _rms(x, w):
    return x * lax.rsqrt(jnp.mean(x * x, axis=-1, keepdims=True) + RMS_EPS) * w


def _heads(a, n):
    return jnp.stack([a[:, h * HEAD_DIM:(h + 1) * HEAD_DIM] for h in range(n)])


def _unheads(a):
    return jnp.concatenate([a[h] for h in range(a.shape[0])], axis=1)


def _row_chunks(t):
    return [pl.ds(c * ROWS, ROWS) for c in range(t // ROWS)]


def inproj_fwd(x, nw, w):
    n, d = x.shape
    p = w.shape[1]

    def body(x_ref, nw_ref, w_ref, proj_ref, hdn_ref):
        @pl.when(pl.program_id(1) == 0)
        def _():
            hdn_ref[...] = _rms(x_ref[...], nw_ref[...]).astype(BF16)

        proj_ref[...] = jnp.dot(hdn_ref[...], w_ref[...], preferred_element_type=F32)

    return pl.pallas_call(
        body, name="inproj_fwd", grid=(n // TM_ROWS, p // INPROJ_COLS),
        in_specs=[pl.BlockSpec((TM_ROWS, d), lambda i, j: (i, 0)), pl.BlockSpec((1, d), lambda i, j: (0, 0)),
                  pl.BlockSpec((d, INPROJ_COLS), lambda i, j: (0, j))],
        out_specs=[pl.BlockSpec((TM_ROWS, INPROJ_COLS), lambda i, j: (i, j)), pl.BlockSpec((TM_ROWS, d), lambda i, j: (i, 0))],
        out_shape=[jax.ShapeDtypeStruct((n, p), F32), jax.ShapeDtypeStruct((n, d), BF16)],
        compiler_params=_cparams("parallel", "arbitrary"),
    )(x, nw, w)


def mat_tn(a, slabs):
    n, ka = a.shape
    ns = len(slabs)

    def body(*refs):
        a_ref, s_refs, o_refs = refs[0], refs[1:1 + ns], refs[1 + ns:]

        @pl.when(pl.program_id(0) == 0)
        def _():
            for o_ref in o_refs:
                o_ref[...] = jnp.zeros_like(o_ref)

        av = a_ref[...]
        for s_ref, o_ref in zip(s_refs, o_refs):
            o_ref[...] += lax.dot_general(av, s_ref[...].astype(BF16), TN, preferred_element_type=F32)

    return pl.pallas_call(
        body, name="mat_tn", grid=(n // TM_ROWS,),
        in_specs=[pl.BlockSpec((TM_ROWS, ka), lambda k: (k, 0))]
                 + [pl.BlockSpec((TM_ROWS, s.shape[1]), lambda k: (k, 0)) for s in slabs],
        out_specs=[pl.BlockSpec((ka, s.shape[1]), lambda k: (0, 0)) for s in slabs],
        out_shape=[jax.ShapeDtypeStruct((ka, s.shape[1]), F32) for s in slabs],
        compiler_params=_cparams("arbitrary"),
    )(a, *slabs)


def inproj_bwd(slabs, w, x, nw, dy):
    n, d = x.shape
    p = w.shape[1]
    tm = 256
    ns = len(slabs)

    def body(*refs):
        s_refs = refs[:ns]
        w_ref, x_ref, nw_ref, dy_ref, dx_ref, dnw_ref = refs[ns:]
        dh = jnp.zeros((tm, d), F32)
        at = 0
        for s_ref in s_refs:
            wd = s_ref.shape[1]
            dh = dh + lax.dot_general(s_ref[...].astype(BF16), w_ref[:, at:at + wd], NT, preferred_element_type=F32)
            at += wd
        _, vjp = jax.vjp(_rms, x_ref[...], nw_ref[...])
        dx, dnw = vjp(dh)
        dx_ref[...] = dx + dy_ref[...]
        dnw_ref[0] = dnw

    return pl.pallas_call(
        body, name="inproj_bwd", grid=(n // tm,),
        in_specs=[pl.BlockSpec((tm, s.shape[1]), lambda i: (i, 0)) for s in slabs]
                 + [pl.BlockSpec((d, p), lambda i: (0, 0)), pl.BlockSpec((tm, d), lambda i: (i, 0)),
                    pl.BlockSpec((1, d), lambda i: (0, 0)), pl.BlockSpec((tm, d), lambda i: (i, 0))],
        out_specs=[pl.BlockSpec((tm, d), lambda i: (i, 0)), pl.BlockSpec((1, 1, d), lambda i: (i, 0, 0))],
        out_shape=[jax.ShapeDtypeStruct((n, d), F32), jax.ShapeDtypeStruct((n // tm, 1, d), F32)],
        compiler_params=_cparams("parallel"),
    )(*slabs, w, x, nw, dy)


CONV_PAD = 8
CONV_ROWS = 256


def _conv_pre(pad_s, cw, c):
    xs = [pad_s[pl.ds(c * CONV_ROWS + CONV_PAD - (CONV_WIDTH - 1) + k, CONV_ROWS), :] for k in range(CONV_WIDTH)]
    pre = xs[0] * cw[0:1, :]
    for k in range(1, CONV_WIDTH):
        pre = pre + xs[k] * cw[k:k + 1, :]
    return pre, xs


def conv_fwd(proj, cw, nseq):
    n = proj.shape[0]
    t = n // nseq
    ch = cw.shape[1]

    def body(x_ref, cw_ref, y_ref, pad_s):
        pad_s[pl.ds(0, CONV_PAD), :] = jnp.zeros((CONV_PAD, TN_COLS), F32)
        pad_s[pl.ds(CONV_PAD, t), :] = x_ref[...]
        cwv = cw_ref[...]
        for c in range(t // CONV_ROWS):
            pre, _ = _conv_pre(pad_s, cwv, c)
            y_ref[pl.ds(c * CONV_ROWS, CONV_ROWS), :] = pre * _sigmoid(pre)

    return pl.pallas_call(
        body, name="conv_fwd", grid=(nseq, ch // TN_COLS),
        in_specs=[pl.BlockSpec((t, TN_COLS), lambda b, j: (b, j)), pl.BlockSpec((CONV_WIDTH, TN_COLS), lambda b, j: (0, j))],
        out_specs=pl.BlockSpec((t, TN_COLS), lambda b, j: (b, j)),
        out_shape=jax.ShapeDtypeStruct((n, ch), F32),
        scratch_shapes=[pltpu.VMEM((t + CONV_PAD, TN_COLS), F32)],
        compiler_params=_cparams("parallel", "parallel"),
    )(proj, cw)


def conv_bwd(proj, cw, dy, nseq):
    n = proj.shape[0]
    t = n // nseq
    ch = cw.shape[1]

    def body(x_ref, cw_ref, dy_ref, dx_ref, dcw_ref, pad_s, dpad_s):
        pad_s[pl.ds(0, CONV_PAD), :] = jnp.zeros((CONV_PAD, TN_COLS), F32)
        pad_s[pl.ds(CONV_PAD, t), :] = x_ref[...]
        dpad_s[pl.ds(t, CONV_PAD), :] = jnp.zeros((CONV_PAD, TN_COLS), F32)
        cwv = cw_ref[...]
        acc = [jnp.zeros((1, TN_COLS), F32)] * CONV_WIDTH
        for c in range(t // CONV_ROWS):
            pre, xs = _conv_pre(pad_s, cwv, c)
            sg = _sigmoid(pre)
            dpre = dy_ref[pl.ds(c * CONV_ROWS, CONV_ROWS), :] * (sg * (1.0 + pre * (1.0 - sg)))
            dpad_s[pl.ds(c * CONV_ROWS, CONV_ROWS), :] = dpre
            acc = [acc[k] + jnp.sum(dpre * xs[k], axis=0, keepdims=True) for k in range(CONV_WIDTH)]
        for k in range(CONV_WIDTH):
            dcw_ref[0, pl.ds(k, 1), :] = acc[k]
        for c in range(t // CONV_ROWS):
            dx = dpad_s[pl.ds(c * CONV_ROWS + CONV_WIDTH - 1, CONV_ROWS), :] * cwv[0:1, :]
            for k in range(1, CONV_WIDTH):
                dx = dx + dpad_s[pl.ds(c * CONV_ROWS + CONV_WIDTH - 1 - k, CONV_ROWS), :] * cwv[k:k + 1, :]
            dx_ref[pl.ds(c * CONV_ROWS, CONV_ROWS), :] = dx

    blk = pl.BlockSpec((t, TN_COLS), lambda b, j: (b, j))
    return pl.pallas_call(
        body, name="conv_bwd", grid=(nseq, ch // TN_COLS),
        in_specs=[blk, pl.BlockSpec((CONV_WIDTH, TN_COLS), lambda b, j: (0, j)), blk],
        out_specs=[blk, pl.BlockSpec((1, CONV_WIDTH, TN_COLS), lambda b, j: (b, 0, j))],
        out_shape=[jax.ShapeDtypeStruct((n, ch), F32), jax.ShapeDtypeStruct((nseq, CONV_WIDTH, ch), F32)],
        scratch_shapes=[pltpu.VMEM((t + CONV_PAD, TN_COLS), F32)] * 2,
        compiler_params=_cparams("parallel", "parallel"),
    )(proj, cw, dy)


def _gate_specs(d):
    wide = pl.BlockSpec((TM_ROWS, d), lambda i: (i, 0))
    oa = pl.BlockSpec((TM_ROWS, A_WIDTH), lambda i: (i, 0))
    ob = pl.BlockSpec((TM_ROWS, B_WIDTH), lambda i: (i, 0))
    oc = pl.BlockSpec((TM_ROWS, C_WIDTH), lambda i: (i, 0))
    za = pl.BlockSpec((TM_ROWS, A_WIDTH), lambda i: (i, COL_AZ // A_WIDTH))
    zb = pl.BlockSpec((TM_ROWS, B_WIDTH), lambda i: (i, (COL_B + 3 * B_WIDTH) // B_WIDTH))
    zc = pl.BlockSpec((TM_ROWS, C_WIDTH), lambda i: (i, (COL_C + 3 * C_WIDTH) // C_WIDTH))
    return wide, oa, ob, oc, za, zb, zc


BRANCH_COLS = ((0, A_WIDTH), (A_WIDTH, A_WIDTH + B_WIDTH), (A_WIDTH + B_WIDTH, D_MODEL))


def outproj_fwd(x, oa, ob, oc, proj, w):
    n, d = x.shape

    def body(x_ref, oa_ref, ob_ref, oc_ref, za_ref, zb_ref, zc_ref, w_ref, y_ref, m_ref):
        for (lo, hi), o_ref, z_ref in zip(BRANCH_COLS, (oa_ref, ob_ref, oc_ref), (za_ref, zb_ref, zc_ref)):
            zv = z_ref[...]
            m_ref[:, lo:hi] = (o_ref[...] * (zv * _sigmoid(zv))).astype(BF16)
        y_ref[...] = x_ref[...] + jnp.dot(m_ref[...], w_ref[...], preferred_element_type=F32)

    wide, sa, sb, sc, za, zb, zc = _gate_specs(d)
    return pl.pallas_call(
        body, name="outproj_fwd", grid=(n // TM_ROWS,),
        in_specs=[wide, sa, sb, sc, za, zb, zc, pl.BlockSpec((d, d), lambda i: (0, 0))],
        out_specs=[wide, wide],
        out_shape=[jax.ShapeDtypeStruct((n, d), F32), jax.ShapeDtypeStruct((n, d), BF16)],
        compiler_params=_cparams("parallel"),
    )(x, oa, ob, oc, proj, proj, proj, w)


def outproj_bwd(dy, oa, ob, oc, proj, w):
    n, d = dy.shape

    def body(dy_ref, oa_ref, ob_ref, oc_ref, za_ref, zb_ref, zc_ref, w_ref, doa_ref, dob_ref, doc_ref, dza_ref, dzb_ref, dzc_ref):
        dm = lax.dot_general(dy_ref[...].astype(BF16), w_ref[...], NT, preferred_element_type=F32)
        for (lo, hi), o_ref, z_ref, do_ref, dz_ref in zip(BRANCH_COLS, (oa_ref, ob_ref, oc_ref), (za_ref, zb_ref, zc_ref),
                                                          (doa_ref, dob_ref, doc_ref), (dza_ref, dzb_ref, dzc_ref)):
            zv = z_ref[...]
            sg = _sigmoid(zv)
            dmv = dm[:, lo:hi]
            do_ref[...] = dmv * (zv * sg)
            dz_ref[...] = dmv * o_ref[...] * (sg * (1.0 + zv * (1.0 - sg)))

    wide, sa, sb, sc, za, zb, zc = _gate_specs(d)
    sd = jax.ShapeDtypeStruct
    outs = [sd((n, A_WIDTH), F32), sd((n, B_WIDTH), F32), sd((n, C_WIDTH), F32)]
    return pl.pallas_call(
        body, name="outproj_bwd", grid=(n // TM_ROWS,),
        in_specs=[wide, sa, sb, sc, za, zb, zc, pl.BlockSpec((d, d), lambda i: (0, 0))],
        out_specs=[sa, sb, sc, sa, sb, sc],
        out_shape=outs + outs,
        compiler_params=_cparams("parallel"),
    )(dy, oa, ob, oc, proj, proj, proj, w)


def loss_fwd_bwd(y, target):
    n, d = y.shape

    def body(y_ref, t_ref, dy_ref, part_ref):
        e = y_ref[...] - t_ref[...]
        dy_ref[...] = e * (1.0 / d)
        part_ref[...] = jnp.zeros_like(part_ref) + 0.5 * jnp.sum(e * e) * (1.0 / d)

    blk = pl.BlockSpec((TM_ROWS, d), lambda i: (i, 0))
    return pl.pallas_call(
        body, name="loss", grid=(n // TM_ROWS,),
        in_specs=[blk, blk],
        out_specs=[blk, pl.BlockSpec((1, 8, BLOCK), lambda i: (i, 0, 0))],
        out_shape=[jax.ShapeDtypeStruct((n, d), F32), jax.ShapeDtypeStruct((n // TM_ROWS, 8, BLOCK), F32)],
        compiler_params=_cparams("parallel"),
    )(y, target)


ADAM_LR, ADAM_B1, ADAM_B2, ADAM_EPS, ADAM_WD, ADAM_STEP = 0.001, 0.9, 0.999, 1e-08, 0.01, 10


def adamw(w, g, m, v):
    r, c = w.shape
    tr = r if r <= 256 else 256

    def body(w_ref, g_ref, m_ref, v_ref, d_ref, nm_ref, nv_ref):
        gv = g_ref[...]
        nm = ADAM_B1 * m_ref[...] + (1.0 - ADAM_B1) * gv
        nv = ADAM_B2 * v_ref[...] + (1.0 - ADAM_B2) * (gv * gv)
        m_hat = nm / (1.0 - ADAM_B1 ** ADAM_STEP)
        v_hat = nv / (1.0 - ADAM_B2 ** ADAM_STEP)
        d_ref[...] = -ADAM_LR * (m_hat / (jnp.sqrt(v_hat) + ADAM_EPS) + ADAM_WD * w_ref[...])
        nm_ref[...] = nm
        nv_ref[...] = nv

    blk = pl.BlockSpec((tr, c), lambda i: (i, 0))
    return pl.pallas_call(
        body, name="adamw", grid=(r // tr,),
        in_specs=[blk] * 4, out_specs=[blk] * 3,
        out_shape=[jax.ShapeDtypeStruct((r, c), F32)] * 3,
        compiler_params=_cparams("parallel"),
    )(w, g, m, v)


SB_G = N_HEADS_B


def _sb_tile(q, k, v, carry, qpos, kpos, tri):
    z = _mm(q * (HEAD_DIM ** -0.5), k, NT)
    earlier = kpos < qpos
    sp = jnp.log(1.0 + jnp.exp(-jnp.abs(z)))
    ls_pos = jnp.minimum(z, 0.0) - sp
    ls_neg = jnp.minimum(-z, 0.0) - sp
    log_keep = jnp.where(earlier, ls_neg, 0.0)
    hi = log_keep.astype(BF16)
    lo = lax.stop_gradient(log_keep - hi.astype(F32)).astype(BF16)
    within = lax.dot_general(jnp.concatenate([hi, lo], axis=1), tri, NN, preferred_element_type=F32)
    wts = jnp.where(earlier, jnp.exp(jnp.where(earlier, ls_pos + within + carry, 0.0)), 0.0)
    return _mm(wts, v), jnp.sum(log_keep, axis=1, keepdims=True)


_sb_tiles = jax.vmap(_sb_tile, in_axes=(0, 0, 0, 0, None, None, None))


def _sb_consts():
    qi = lax.broadcasted_iota(jnp.int32, (BLOCK, 1), 0)
    kj = lax.broadcasted_iota(jnp.int32, (1, BLOCK), 1)
    r = lax.broadcasted_iota(jnp.int32, (2 * BLOCK, BLOCK), 0) % BLOCK
    c = lax.broadcasted_iota(jnp.int32, (2 * BLOCK, BLOCK), 1)
    tri = jnp.where(r > c, 1.0, 0.0).astype(BF16)
    return qi, kj, tri


def _sb_specs(t, nq):
    cb = COL_B // B_WIDTH
    q = pl.BlockSpec((BLOCK, B_WIDTH), lambda b, i: (b * nq + i, cb))
    k = pl.BlockSpec((t, B_WIDTH), lambda b, i: (b, cb + 1))
    v = pl.BlockSpec((t, B_WIDTH), lambda b, i: (b, cb + 2))
    blk = pl.BlockSpec((BLOCK, B_WIDTH), lambda b, i: (b * nq + i, 0))
    full = pl.BlockSpec((t, B_WIDTH), lambda b, i: (b, 0))
    carry = pl.BlockSpec((1, 1, nq, BLOCK, SB_G), lambda b, i: (b, i, 0, 0, 0))
    return q, k, v, blk, full, carry


def sb_fwd(proj, nseq):
    n = proj.shape[0]
    t = n // nseq
    nq = t // BLOCK
    g = SB_G

    def body(q_ref, k_ref, v_ref, o_ref, carry_ref):
        i = pl.program_id(1)
        qi, kj, tri = _sb_consts()
        qv = _heads(q_ref[...], g)
        qpos = i * BLOCK + qi

        def step(it, st):
            o_acc, c = st
            j = i - it
            rows = pl.ds(pl.multiple_of(j * BLOCK, BLOCK), BLOCK)
            carry_ref[0, 0, j] = jnp.concatenate([c[h] for h in range(g)], axis=1)
            o, tot = _sb_tiles(qv, _heads(k_ref[rows, :], g), _heads(v_ref[rows, :], g), c, qpos, j * BLOCK + kj, tri)
            return o_acc + o, c + tot

        o_acc, _ = lax.fori_loop(0, i + 1, step, (jnp.zeros((g, BLOCK, HEAD_DIM), F32), jnp.zeros((g, BLOCK, 1), F32)))
        o_ref[...] = _unheads(o_acc)

    q, k, v, blk, _, carry = _sb_specs(t, nq)
    return pl.pallas_call(
        body, name="sb_fwd", grid=(nseq, nq),
        in_specs=[q, k, v],
        out_specs=[blk, carry],
        out_shape=[jax.ShapeDtypeStruct((n, B_WIDTH), F32),
                   jax.ShapeDtypeStruct((nseq, nq, nq, BLOCK, g), F32)],
        compiler_params=_cparams("parallel", "arbitrary"),
    )(proj, proj, proj)


def sb_bwd(proj, carries, do, nseq):
    n = proj.shape[0]
    t = n // nseq
    nq = t // BLOCK
    g = SB_G

    def body(q_ref, k_ref, v_ref, carry_ref, do_ref, dq_ref, dk_ref, dv_ref):
        i = pl.program_id(1)

        @pl.when(i == 0)
        def _():
            dk_ref[...] = jnp.zeros_like(dk_ref)
            dv_ref[...] = jnp.zeros_like(dv_ref)

        qi, kj, tri = _sb_consts()
        qv = _heads(q_ref[...], g)
        dov = _heads(do_ref[...], g)
        qpos = i * BLOCK + qi

        def step(j, st):
            dq_acc, dc = st
            rows = pl.ds(pl.multiple_of(j * BLOCK, BLOCK), BLOCK)
            kpos = j * BLOCK + kj
            cj = carry_ref[0, 0, j]
            f = lambda q_, k_, v_, c_: _sb_tiles(q_, k_, v_, c_, qpos, kpos, tri)
            _, vjp = jax.vjp(f, qv, _heads(k_ref[rows, :], g), _heads(v_ref[rows, :], g),
                             jnp.stack([cj[:, h:h + 1] for h in range(g)]))
            dq, dk, dv, dcj = vjp((dov, dc))
            dk_ref[rows, :] += _unheads(dk)
            dv_ref[rows, :] += _unheads(dv)
            return dq_acc + dq, dc + dcj

        dq_acc, _ = lax.fori_loop(0, i + 1, step, (jnp.zeros((g, BLOCK, HEAD_DIM), F32), jnp.zeros((g, BLOCK, 1), F32)))
        dq_ref[...] = _unheads(dq_acc)

    q, k, v, blk, full, carry = _sb_specs(t, nq)
    return pl.pallas_call(
        body, name="sb_bwd", grid=(nseq, nq),
        in_specs=[q, k, v, carry, blk],
        out_specs=[blk, full, full],
        out_shape=[jax.ShapeDtypeStruct((n, B_WIDTH), F32)] * 3,
        compiler_params=_cparams("parallel", "arbitrary"),
    )(proj, proj, proj, carries, do)


@jax.custom_vjp
def _unit_lower_inverse(a):
    n = a.shape[0]
    eye = jnp.where(lax.broadcasted_iota(jnp.int32, (n, n), 0) == lax.broadcasted_iota(jnp.int32, (n, n), 1), 1.0, 0.0)
    tmat = eye.astype(F32) - a
    p = a
    for _ in range(5):
        p = _mm32(p, p)
        tmat = tmat + _mm32(tmat, p)
    return tmat


def _unit_lower_inverse_fwd(a):
    tmat = _unit_lower_inverse(a)
    return tmat, tmat


def _unit_lower_inverse_bwd(tmat, g):
    return (-_mm32(_mm32(tmat, g, TN), tmat, NT),)


_unit_lower_inverse.defvjp(_unit_lower_inverse_fwd, _unit_lower_inverse_bwd)


def _gdn_chunk(q, k, v, al_c, al_r, br_c, alog, dtb, nw, s):
    c = GDN_CHUNK
    ri = lax.broadcasted_iota(jnp.int32, (c, c), 0)
    ci = lax.broadcasted_iota(jnp.int32, (c, c), 1)
    incl, strict = ri >= ci, ri > ci
    eye = jnp.where(ri == ci, 1.0, 0.0).astype(F32)
    rate = -jnp.exp(alog)
    g_c = rate * _softplus(al_c + dtb)
    g_r = rate * _softplus(al_r + dtb)
    beta = _sigmoid(br_c)
    gc_c = jnp.sum(jnp.where(incl, g_r, 0.0), axis=1, keepdims=True)
    gc_r = jnp.sum(jnp.where(ri <= ci, g_c, 0.0), axis=0, keepdims=True)
    gl = jnp.sum(g_r, axis=1, keepdims=True)
    decay = jnp.where(incl, jnp.exp(jnp.where(incl, gc_c - gc_r, 0.0)), 0.0)
    qn = q * lax.rsqrt(jnp.sum(q * q, axis=-1, keepdims=True) + RMS_EPS) * (HEAD_DIM ** -0.5)
    kn = k * lax.rsqrt(jnp.sum(k * k, axis=-1, keepdims=True) + RMS_EPS)
    kb = kn * beta
    a = jnp.where(strict, _mm(kb, kn, NT) * decay, 0.0)
    tmat = _unit_lower_inverse(a)
    u = _mm(tmat, v * beta)
    w = _mm(tmat, kb * jnp.exp(gc_c))
    qk = _mm(qn, kn, NT) * decay
    v_new = u - _mm(w, s)
    o = _mm(qn * jnp.exp(gc_c), s) + _mm(qk, v_new)
    s_new = s * jnp.exp(gl) + _mm(kn * jnp.exp(gl - gc_c), v_new, TN)
    o = o * lax.rsqrt(jnp.mean(o * o, axis=-1, keepdims=True) + RMS_EPS) * nw
    return o, s_new


_gdn_chunks = jax.vmap(_gdn_chunk, in_axes=(0, 0, 0, 0, 0, 0, 0, 0, None, 0))

GDN_TB = 256
GDN_SEQ_FWD = 2
GDN_SEQ_BWD = 2


def _gdn_block(q3, k3, v3, ba, alog, dtb, nw, s):
    nh = N_HEADS_A
    ns = q3.shape[0]
    bat = [ba[b].T for b in range(ns)]
    br_c = jnp.stack([ba[b][:, h:h + 1] for b in range(ns) for h in range(nh)])
    al_c = jnp.stack([ba[b][:, nh + h:nh + h + 1] for b in range(ns) for h in range(nh)])
    al_r = jnp.stack([bat[b][nh + h:nh + h + 1, :] for b in range(ns) for h in range(nh)])
    heads = lambda a: jnp.concatenate([_heads(a[b], nh) for b in range(ns)])
    o, s_new = _gdn_chunks(heads(q3), heads(k3), heads(v3), al_c, al_r, br_c,
                           jnp.concatenate([alog] * ns), jnp.concatenate([dtb] * ns), nw, s)
    return jnp.stack([_unheads(o[b * nh:(b + 1) * nh]) for b in range(ns)]), s_new


def _gdn_specs(nt, sq, rev):
    tpos = (lambda i: nt - 1 - i) if rev else (lambda i: i)
    ncb = GDN_TB // GDN_CHUNK
    qkv = [pl.BlockSpec((sq, GDN_TB, A_WIDTH), lambda b, i, j=j: (b, tpos(i), j)) for j in range(3)]
    ba = pl.BlockSpec((sq, GDN_TB, BLOCK), lambda b, i: (b, tpos(i), COL_BA // BLOCK))
    one = pl.BlockSpec((N_HEADS_A, 1, 1), lambda b, i: (0, 0, 0))
    vec = pl.BlockSpec((1, HEAD_DIM), lambda b, i: (0, 0))
    st = pl.BlockSpec((sq, ncb, N_HEADS_A, HEAD_DIM, HEAD_DIM), lambda b, i: (b, tpos(i), 0, 0, 0))
    oa = pl.BlockSpec((sq, GDN_TB, A_WIDTH), lambda b, i: (b, tpos(i), 0))
    return qkv, ba, one, vec, st, oa, tpos


def gdn_fwd(ya, proj, alog, dtb, nw, nseq):
    n = ya.shape[0]
    t = n // nseq
    nc, nt, ncb = t // GDN_CHUNK, t // GDN_TB, GDN_TB // GDN_CHUNK
    sq = GDN_SEQ_FWD
    nh = N_HEADS_A

    def body(q_ref, k_ref, v_ref, ba_ref, alog_ref, dtb_ref, nw_ref, o_ref, st_ref, s_s):
        @pl.when(pl.program_id(1) == 0)
        def _():
            s_s[...] = jnp.zeros_like(s_s)

        def step(c, s):
            rows = pl.ds(pl.multiple_of(c * GDN_CHUNK, GDN_CHUNK), GDN_CHUNK)
            for b in range(sq):
                st_ref[b, c] = s[b * nh:(b + 1) * nh]
            o, s_new = _gdn_block(q_ref[:, rows, :], k_ref[:, rows, :], v_ref[:, rows, :], ba_ref[:, rows, :],
                                  alog_ref[...], dtb_ref[...], nw_ref[...], s)
            o_ref[:, rows, :] = o
            return s_new

        s_s[...] = lax.fori_loop(0, ncb, step, s_s[...])

    qkv, ba, one, vec, st, oa, _ = _gdn_specs(nt, sq, False)
    ya3, proj3 = ya.reshape(nseq, t, -1), proj.reshape(nseq, t, -1)
    o, states = pl.pallas_call(
        body, name="gdn_fwd", grid=(nseq // sq, nt),
        in_specs=qkv + [ba, one, one, vec],
        out_specs=[oa, st],
        out_shape=[jax.ShapeDtypeStruct((nseq, t, A_WIDTH), F32),
                   jax.ShapeDtypeStruct((nseq, nc, nh, HEAD_DIM, HEAD_DIM), F32)],
        scratch_shapes=[pltpu.VMEM((sq * nh, HEAD_DIM, HEAD_DIM), F32)],
        compiler_params=_cparams("parallel", "arbitrary"),
    )(ya3, ya3, ya3, proj3, alog, dtb, nw)
    return o.reshape(n, A_WIDTH), states


def gdn_bwd(ya, proj, alog, dtb, nw, states, do, nseq):
    n = ya.shape[0]
    t = n // nseq
    nt, ncb = t // GDN_TB, GDN_TB // GDN_CHUNK
    nh = N_HEADS_A
    sq = GDN_SEQ_BWD

    def body(q_ref, k_ref, v_ref, ba_ref, alog_ref, dtb_ref, nw_ref, st_ref, do_ref,
             dya_ref, dba_ref, dalog_ref, ddtb_ref, dnw_ref, ds_s):
        @pl.when(pl.program_id(1) == 0)
        def _():
            ds_s[...] = jnp.zeros_like(ds_s)
            dalog_ref[...] = jnp.zeros_like(dalog_ref)
            ddtb_ref[...] = jnp.zeros_like(ddtb_ref)
            dnw_ref[...] = jnp.zeros_like(dnw_ref)

        def step(it, carry):
            ds, dalog, ddtb, dnw = carry
            c = ncb - 1 - it
            rows = pl.ds(pl.multiple_of(c * GDN_CHUNK, GDN_CHUNK), GDN_CHUNK)
            s_in = jnp.concatenate([st_ref[b, c] for b in range(sq)])
            _, vjp = jax.vjp(_gdn_block, q_ref[:, rows, :], k_ref[:, rows, :], v_ref[:, rows, :], ba_ref[:, rows, :],
                             alog_ref[...], dtb_ref[...], nw_ref[...], s_in)
            dq, dk, dv, dba, da, dd, dn, ds = vjp((do_ref[:, rows, :], ds))
            dya_ref[:, rows, 0:A_WIDTH] = dq
            dya_ref[:, rows, A_WIDTH:2 * A_WIDTH] = dk
            dya_ref[:, rows, 2 * A_WIDTH:3 * A_WIDTH] = dv
            dba_ref[:, rows, :] = dba
            return ds, dalog + da, ddtb + dd, dnw + dn

        z11 = jnp.zeros((nh, 1, 1), F32)
        ds, dalog, ddtb, dnw = lax.fori_loop(0, ncb, step, (ds_s[...], z11, z11, jnp.zeros((1, HEAD_DIM), F32)))
        ds_s[...] = ds
        dalog_ref[0] += dalog
        ddtb_ref[0] += ddtb
        dnw_ref[0] += dnw

    qkv, ba, one, vec, st, oa, tpos = _gdn_specs(nt, sq, True)
    per_grp = pl.BlockSpec((1, nh, 1, 1), lambda b, i: (b, 0, 0, 0))
    sd = jax.ShapeDtypeStruct
    ya3, proj3, do3 = ya.reshape(nseq, t, -1), proj.reshape(nseq, t, -1), do.reshape(nseq, t, -1)
    dya, dba, dalog, ddtb, dnw = pl.pallas_call(
        body, name="gdn_bwd", grid=(nseq // sq, nt),
        in_specs=qkv + [ba, one, one, vec, st, oa],
        out_specs=[pl.BlockSpec((sq, GDN_TB, 3 * A_WIDTH), lambda b, i: (b, tpos(i), 0)),
                   pl.BlockSpec((sq, GDN_TB, BLOCK), lambda b, i: (b, tpos(i), 0)),
                   per_grp, per_grp, pl.BlockSpec((1, 1, HEAD_DIM), lambda b, i: (b, 0, 0))],
        out_shape=[sd((nseq, t, 3 * A_WIDTH), F32), sd((nseq, t, BLOCK), F32), sd((nseq // sq, nh, 1, 1), F32),
                   sd((nseq // sq, nh, 1, 1), F32), sd((nseq // sq, 1, HEAD_DIM), F32)],
        scratch_shapes=[pltpu.VMEM((sq * nh, HEAD_DIM, HEAD_DIM), F32)],
        compiler_params=_cparams("parallel", "arbitrary"),
    )(ya3, ya3, ya3, proj3, alog, dtb, nw, states, do3)
    return dya.reshape(n, 3 * A_WIDTH), dba.reshape(n, BLOCK), dalog, ddtb, dnw


DIL_NB = tuple((SEQ // d) // BLOCK for _, d in DILATED_PAIRS)
DIL_D = tuple(d for _, d in DILATED_PAIRS)
DIL_STEPS = tuple(w // d for w, d in DILATED_PAIRS)
DIL_B = 4


def _rope_tables(t):
    half = ROPE_DIM // 2
    inv_freq = ROPE_THETA ** (-jnp.arange(half, dtype=F32) / half)
    ang = jnp.arange(t, dtype=F32)[:, None] * inv_freq[None, :]
    ones = jnp.ones((t, HEAD_DIM - ROPE_DIM), F32)
    cs = jnp.concatenate([jnp.cos(ang), jnp.cos(ang), ones], axis=1)
    sn = jnp.concatenate([jnp.sin(ang), jnp.sin(ang), 0.0 * ones], axis=1)
    i = jnp.arange(HEAD_DIM)[:, None]
    j = jnp.arange(HEAD_DIM)[None, :]
    pm = (jnp.where((j < half) & (i == j + half), -1.0, 0.0)
          + jnp.where((j >= half) & (j < ROPE_DIM) & (i == j - half), 1.0, 0.0))
    return cs, sn, jnp.concatenate([pm, pm]).astype(BF16)


def _dil_prep(x, w, cs, sn, pm):
    y = x * lax.rsqrt(jnp.mean(x * x, axis=-1, keepdims=True) + RMS_EPS) * w
    hi = y.astype(BF16)
    lo = lax.stop_gradient(y - hi.astype(F32)).astype(BF16)
    return y * cs + lax.dot_general(jnp.concatenate([hi, lo], axis=1), pm, NN, preferred_element_type=F32) * sn


def _dil_tile(qn, kk, vv, mask):
    s = jnp.where(mask, _mm(qn * (HEAD_DIM ** -0.5), kk, NT), NEG)
    m = lax.stop_gradient(jnp.max(s, axis=-1, keepdims=True))
    p = jnp.exp(s - m)
    denom = jnp.sum(p, axis=-1, keepdims=True)
    return _mm(p, vv) / denom, m + jnp.log(denom)


_dil_tiles = jax.vmap(_dil_tile)


def _dil_mix(o1, o2, o3, l1, l2, l3):
    m = lax.stop_gradient(jnp.maximum(jnp.maximum(l1, l2), l3))
    e1, e2, e3 = jnp.exp(l1 - m), jnp.exp(l2 - m), jnp.exp(l3 - m)
    return (e1 * o1 + e2 * o2 + e3 * o3) / (e1 + e2 + e3)


def _dil_mask(it, g):
    nb = DIL_NB[g]
    n = it % nb
    r = it // nb
    kstart = jnp.maximum(it - 1, 0) * BLOCK
    iq = n * BLOCK + lax.broadcasted_iota(jnp.int32, (BLOCK, 1), 0)
    ik = kstart - r * (nb * BLOCK) + lax.broadcasted_iota(jnp.int32, (1, 2 * BLOCK), 1)
    mask = (ik >= 0) & (iq >= ik) & (iq - ik <= DIL_STEPS[g])
    return mask, pl.ds(pl.multiple_of(it * BLOCK, BLOCK), BLOCK), pl.ds(pl.multiple_of(kstart, BLOCK), 2 * BLOCK)


def _dil_gather(src, dst, d):
    t = src.shape[0]
    ln = t // d
    for r in range(d):
        dst[pl.ds(r * ln, ln), :] = src[pl.ds(r, ln, stride=d), :]


def _dil_scatter(src, dst, d):
    t = src.shape[0]
    ln = t // d
    for r in range(d):
        dst[pl.ds(r, ln, stride=d), :] = src[pl.ds(r * ln, ln), :]


def _half(x, h):
    return jnp.where(h == 0, x[:, :HEAD_DIM], x[:, HEAD_DIM:])


def _store_half(ref, rows, val, h):
    @pl.when(h == 0)
    def _():
        ref[rows, 0:HEAD_DIM] = val

    @pl.when(h == 1)
    def _():
        ref[rows, HEAD_DIM:2 * HEAD_DIM] = val


def _dil_forward_parts(h, q_ref, k_ref, v_ref, qw, kw, cs_ref, sn_ref, pm, qn_s, kn_s, v_s, dl_s, od_s, ld_s, on_s, ln_s):
    t = qn_s.shape[0]

    def prep(c, _):
        rows = pl.ds(pl.multiple_of(c * ROWS, ROWS), ROWS)
        qn_s[rows, :] = _dil_prep(_half(q_ref[rows, :], h), qw, cs_ref[rows, :], sn_ref[rows, :], pm)
        kn_s[rows, :] = _dil_prep(_half(k_ref[rows, :], h), kw, cs_ref[rows, :], sn_ref[rows, :], pm)
        v_s[rows, :] = _half(v_ref[rows, :], h)
        return 0

    lax.fori_loop(0, t // ROWS, prep, 0)
    for g in (1, 2):
        _dil_gather(qn_s, dl_s.at[g - 1, 0], DIL_D[g])
        _dil_gather(kn_s, dl_s.at[g - 1, 1], DIL_D[g])
        _dil_gather(v_s, dl_s.at[g - 1, 2], DIL_D[g])
    for g in range(3):
        qs = qn_s if g == 0 else dl_s.at[g - 1, 0]
        ks = kn_s if g == 0 else dl_s.at[g - 1, 1]
        vs = v_s if g == 0 else dl_s.at[g - 1, 2]

        def tiles(i, _, g=g, qs=qs, ks=ks, vs=vs):
            where = [_dil_mask(i * DIL_B + b, g) for b in range(DIL_B)]
            o, lse = _dil_tiles(jnp.stack([qs[qr, :] for _, qr, _ in where]), jnp.stack([ks[kr, :] for _, _, kr in where]),
                                jnp.stack([vs[kr, :] for _, _, kr in where]), jnp.stack([m for m, _, _ in where]))
            for b, (_, qr, _) in enumerate(where):
                od_s[g, qr, :] = o[b]
                ld_s[g, qr, :] = lse[b]
            return 0

        lax.fori_loop(0, t // BLOCK // DIL_B, tiles, 0)
    for g in (1, 2):
        _dil_scatter(od_s.at[g], on_s.at[g - 1], DIL_D[g])
        _dil_scatter(ld_s.at[g], ln_s.at[g - 1], DIL_D[g])


def _dil_scratch(t):
    return [pltpu.VMEM((t, HEAD_DIM), F32), pltpu.VMEM((t, HEAD_DIM), F32), pltpu.VMEM((t, HEAD_DIM), F32),
            pltpu.VMEM((2, 3, t, HEAD_DIM), F32),
            pltpu.VMEM((3, t, HEAD_DIM), F32), pltpu.VMEM((3, t, 1), F32),
            pltpu.VMEM((2, t, HEAD_DIM), F32), pltpu.VMEM((2, t, 1), F32)]


def _dil_specs(t):
    cb = COL_C // BLOCK
    per = C_WIDTH // BLOCK
    qkv = [pl.BlockSpec((t, BLOCK), lambda b, p, h, j=j: (b, cb + j * per + p)) for j in range(3)]
    vec = pl.BlockSpec((1, HEAD_DIM), lambda b, p, h: (0, 0))
    tab = pl.BlockSpec((t, HEAD_DIM), lambda b, p, h: (0, 0))
    mat = pl.BlockSpec((2 * HEAD_DIM, HEAD_DIM), lambda b, p, h: (0, 0))
    pair = pl.BlockSpec((t, BLOCK), lambda b, p, h: (b, p))
    return qkv, vec, tab, mat, pair


def dil_fwd(proj, qw, kw, cs, sn, pm, nseq):
    n = proj.shape[0]
    t = n // nseq

    def body(q_ref, k_ref, v_ref, qw_ref, kw_ref, cs_ref, sn_ref, pm_ref, o_ref, qn_s, kn_s, v_s, dl_s, od_s, ld_s, on_s, ln_s):
        h = pl.program_id(2)
        _dil_forward_parts(h, q_ref, k_ref, v_ref, qw_ref[...], kw_ref[...], cs_ref, sn_ref, pm_ref[...],
                           qn_s, kn_s, v_s, dl_s, od_s, ld_s, on_s, ln_s)

        def mix(c, _):
            rows = pl.ds(pl.multiple_of(c * ROWS, ROWS), ROWS)
            _store_half(o_ref, rows, _dil_mix(od_s[0, rows, :], on_s[0, rows, :], on_s[1, rows, :],
                                              ld_s[0, rows, :], ln_s[0, rows, :], ln_s[1, rows, :]), h)
            return 0

        lax.fori_loop(0, t // ROWS, mix, 0)

    qkv, vec, tab, mat, pair = _dil_specs(t)
    return pl.pallas_call(
        body, name="dil_fwd", grid=(nseq, C_WIDTH // BLOCK, 2),
        in_specs=qkv + [vec, vec, tab, tab, mat],
        out_specs=pair,
        out_shape=jax.ShapeDtypeStruct((n, C_WIDTH), F32),
        scratch_shapes=_dil_scratch(t),
        compiler_params=_cparams("parallel", "parallel", "arbitrary"),
    )(proj, proj, proj, qw, kw, cs, sn, pm)


def dil_bwd(proj, qw, kw, cs, sn, pm, do, nseq):
    n = proj.shape[0]
    t = n // nseq
    dh = HEAD_DIM

    def body(q_ref, k_ref, v_ref, qw_ref, kw_ref, cs_ref, sn_ref, pm_ref, do_ref,
             dq_ref, dk_ref, dv_ref, dqw_ref, dkw_ref,
             qn_s, kn_s, v_s, dl_s, od_s, ld_s, on_s, ln_s, tq_s, tk_s, tv_s, dv_s):
        h = pl.program_id(2)
        qw, kw, pm = qw_ref[...], kw_ref[...], pm_ref[...]
        _dil_forward_parts(h, q_ref, k_ref, v_ref, qw, kw, cs_ref, sn_ref, pm, qn_s, kn_s, v_s, dl_s, od_s, ld_s, on_s, ln_s)

        def mix(c, _):
            rows = pl.ds(pl.multiple_of(c * ROWS, ROWS), ROWS)
            _, vjp = jax.vjp(_dil_mix, od_s[0, rows, :], on_s[0, rows, :], on_s[1, rows, :],
                             ld_s[0, rows, :], ln_s[0, rows, :], ln_s[1, rows, :])
            d1, d2, d3, e1, e2, e3 = vjp(_half(do_ref[rows, :], h))
            od_s[0, rows, :] = d1
            on_s[0, rows, :] = d2
            on_s[1, rows, :] = d3
            ld_s[0, rows, :] = e1
            ln_s[0, rows, :] = e2
            ln_s[1, rows, :] = e3
            return 0

        lax.fori_loop(0, t // ROWS, mix, 0)
        for g in (1, 2):
            _dil_gather(on_s.at[g - 1], od_s.at[g], DIL_D[g])
            _dil_gather(ln_s.at[g - 1], ld_s.at[g], DIL_D[g])
        on_s[...] = jnp.zeros_like(on_s)
        dv_s[...] = jnp.zeros_like(dv_s)
        for g in range(3):
            qs = qn_s if g == 0 else dl_s.at[g - 1, 0]
            ks = kn_s if g == 0 else dl_s.at[g - 1, 1]
            vs = v_s if g == 0 else dl_s.at[g - 1, 2]
            gq = on_s.at[0] if g == 0 else tq_s
            gk = on_s.at[1] if g == 0 else tk_s
            gv = dv_s if g == 0 else tv_s
            if g > 0:
                tk_s[...] = jnp.zeros_like(tk_s)
                tv_s[...] = jnp.zeros_like(tv_s)

            def tiles(i, _, g=g, qs=qs, ks=ks, vs=vs, gq=gq, gk=gk, gv=gv):
                where = [_dil_mask(i * DIL_B + b, g) for b in range(DIL_B)]
                masks = jnp.stack([m for m, _, _ in where])
                _, vjp = jax.vjp(lambda q_, k_, v_: _dil_tiles(q_, k_, v_, masks),
                                 jnp.stack([qs[qr, :] for _, qr, _ in where]), jnp.stack([ks[kr, :] for _, _, kr in where]),
                                 jnp.stack([vs[kr, :] for _, _, kr in where]))
                dq, dkk, dvv = vjp((jnp.stack([od_s[g, qr, :] for _, qr, _ in where]),
                                    jnp.stack([ld_s[g, qr, :] for _, qr, _ in where])))
                for b, (_, qr, kr) in enumerate(where):
                    gq[qr, :] = dq[b]
                    gk[kr, :] += dkk[b]
                    gv[kr, :] += dvv[b]
                return 0

            lax.fori_loop(0, t // BLOCK // DIL_B, tiles, 0)
            if g > 0:
                d = DIL_D[g]
                ln = t // d
                for r in range(d):
                    nat, dil = pl.ds(r, ln, stride=d), pl.ds(r * ln, ln)
                    on_s[0, nat, :] += tq_s[dil, :]
                    on_s[1, nat, :] += tk_s[dil, :]
                    dv_s[nat, :] += tv_s[dil, :]

        def prep(c, acc):
            rows = pl.ds(pl.multiple_of(c * ROWS, ROWS), ROWS)
            f = lambda x, w: _dil_prep(x, w, cs_ref[rows, :], sn_ref[rows, :], pm)
            _, vq = jax.vjp(f, _half(q_ref[rows, :], h), qw)
            _, vk = jax.vjp(f, _half(k_ref[rows, :], h), kw)
            dq, dqw = vq(on_s[0, rows, :])
            dk, dkw = vk(on_s[1, rows, :])
            _store_half(dq_ref, rows, dq, h)
            _store_half(dk_ref, rows, dk, h)
            _store_half(dv_ref, rows, dv_s[rows, :], h)
            return acc[0] + dqw, acc[1] + dkw

        dqw, dkw = lax.fori_loop(0, t // ROWS, prep, (jnp.zeros((1, dh), F32), jnp.zeros((1, dh), F32)))
        dqw_ref[0] = dqw
        dkw_ref[0] = dkw

    qkv, vec, tab, mat, pair = _dil_specs(t)
    per = C_WIDTH // BLOCK
    wout = pl.BlockSpec((1, 1, dh), lambda b, p, h: ((b * per + p) * 2 + h, 0, 0))
    return pl.pallas_call(
        body, name="dil_bwd", grid=(nseq, per, 2),
        in_specs=qkv + [vec, vec, tab, tab, mat, pair],
        out_specs=[pair, pair, pair, wout, wout],
        out_shape=[jax.ShapeDtypeStruct((n, C_WIDTH), F32)] * 3 + [jax.ShapeDtypeStruct((nseq * N_HEADS_C, 1, dh), F32)] * 2,
        scratch_shapes=_dil_scratch(t) + [pltpu.VMEM((t, dh), F32)] * 4,
        compiler_params=_cparams("parallel", "parallel", "arbitrary"),
    )(proj, proj, proj, qw, kw, cs, sn, pm, do)


N_CHIPS = 4
SUM_ROWS = 432
MESH_IDS = pl.DeviceIdType.MESH
ANY = pl.BlockSpec(memory_space=pl.ANY)


def plane_exchange(src, all_to_all):
    blk_shape = src.shape[1:] if all_to_all else src.shape

    def body(src_ref, out_ref, send_sems, recv_sems, local_sem):
        x, y, c = lax.axis_index("x"), lax.axis_index("y"), lax.axis_index("c")
        me = 2 * x + y
        mine = pltpu.make_async_copy(src_ref.at[me] if all_to_all else src_ref, out_ref.at[me], local_sem)
        mine.start()
        sends = []
        for k in (1, 2, 3):
            px = 1 - x if k & 2 else x
            py = 1 - y if k & 1 else y
            peer = 2 * px + py
            cp = pltpu.make_async_remote_copy(
                src_ref=src_ref.at[peer] if all_to_all else src_ref, dst_ref=out_ref.at[me],
                send_sem=send_sems.at[k - 1], recv_sem=recv_sems.at[k - 1],
                device_id=(px, py, c), device_id_type=MESH_IDS)
            cp.start()
            sends.append((cp, peer, (px, py, c)))
        for k, (cp, peer, dev) in enumerate(sends):
            pltpu.make_async_remote_copy(
                src_ref=out_ref.at[me], dst_ref=out_ref.at[peer],
                send_sem=send_sems.at[k], recv_sem=recv_sems.at[k],
                device_id=dev, device_id_type=MESH_IDS).wait_recv()
        for cp, _, _ in sends:
            cp.wait_send()
        mine.wait()

    return pl.pallas_call(
        body, name="plane_all_to_all" if all_to_all else "plane_all_gather",
        in_specs=[ANY], out_specs=ANY,
        out_shape=jax.ShapeDtypeStruct((N_CHIPS,) + blk_shape, src.dtype),
        scratch_shapes=[pltpu.SemaphoreType.DMA((3,)), pltpu.SemaphoreType.DMA((3,)), pltpu.SemaphoreType.DMA],
    )(src)


def sibling_swap(src):
    def body(src_ref, out_ref, send_sem, recv_sem):
        x, y, c = lax.axis_index("x"), lax.axis_index("y"), lax.axis_index("c")
        cp = pltpu.make_async_remote_copy(src_ref=src_ref, dst_ref=out_ref, send_sem=send_sem, recv_sem=recv_sem,
                                          device_id=(x, y, 1 - c), device_id_type=MESH_IDS)
        cp.start()
        cp.wait()

    return pl.pallas_call(
        body, name="sibling_swap", in_specs=[ANY], out_specs=ANY,
        out_shape=jax.ShapeDtypeStruct(src.shape, src.dtype),
        scratch_shapes=[pltpu.SemaphoreType.DMA, pltpu.SemaphoreType.DMA],
    )(src)


def sum4(a):
    _, r, c = a.shape
    tr = SUM_ROWS

    def body(a_ref, o_ref):
        o_ref[...] = (a_ref[0] + a_ref[1]) + (a_ref[2] + a_ref[3])

    return pl.pallas_call(
        body, name="sum4", grid=(r // tr,),
        in_specs=[pl.BlockSpec((N_CHIPS, tr, c), lambda i: (0, i, 0))],
        out_specs=pl.BlockSpec((tr, c), lambda i: (i, 0)),
        out_shape=jax.ShapeDtypeStruct((r, c), F32),
        compiler_params=_cparams("parallel"),
    )(a)


def add2(a, b):
    r, c = a.shape
    tr = SUM_ROWS

    def body(a_ref, b_ref, o_ref):
        o_ref[...] = a_ref[...] + b_ref[...]

    blk = pl.BlockSpec((tr, c), lambda i: (i, 0))
    return pl.pallas_call(
        body, name="add2", grid=(r // tr,), in_specs=[blk, blk], out_specs=blk,
        out_shape=jax.ShapeDtypeStruct((r, c), F32), compiler_params=_cparams("parallel"),
    )(a, b)


PACK_COLS = 1152
PACK_ROWS = 2592
ROW_TILE = 16


def _pack(parts):
    blocks = []
    for p in parts:
        p2 = p.reshape(-1, p.shape[-1])
        blocks.append(jnp.pad(p2, ((0, -p2.shape[0] % ROW_TILE), (0, PACK_COLS - p2.shape[1]))))
    rows = sum(b.shape[0] for b in blocks)
    blocks.append(jnp.zeros((PACK_ROWS - rows, PACK_COLS), blocks[0].dtype))
    return jnp.concatenate(blocks)


def _unpack(buf, shapes):
    out, at = [], 0
    for s in shapes:
        rows = math.prod(s[:-1])
        out.append(buf[at:at + rows, :s[-1]].reshape(s))
        at += rows + (-rows % ROW_TILE)
    return out


def _pack_small(g):
    blk = jnp.zeros((ROW_TILE, PACK_COLS), F32)
    for i, k in enumerate(SMALL):
        blk = blk.at[2 * i:2 * i + 2, :g[k].shape[1]].set(g[k])
    return blk


def _unpack_small(blk, shapes):
    return [blk[2 * i:2 * i + 2, :s[1]] for i, s in enumerate(shapes)]


def _layer_fwd(x, p, nseq, tabs):
    proj, hdn = inproj_fwd(x, p["norm_w"][None], p["w_in"])
    ya = conv_fwd(proj, p["conv_w"], nseq)
    oa, states = gdn_fwd(ya, proj, p["a_log"].reshape(N_HEADS_A, 1, 1), p["dt_bias"].reshape(N_HEADS_A, 1, 1),
                         p["gdn_norm_w"][None], nseq)
    ob, carries = sb_fwd(proj, nseq)
    oc = dil_fwd(proj, p["q_norm_w"][None], p["k_norm_w"][None], *tabs, nseq)
    y, mixed = outproj_fwd(x, oa, ob, oc, proj, p["w_out"])
    return y, dict(x=x, hdn=hdn, proj=proj, ya=ya, states=states, carries=carries, oa=oa, ob=ob, oc=oc, mixed=mixed)


def _layer_bwd(dy, p, res, nseq, tabs):
    proj = res["proj"]
    g = {}
    g["w_out"] = mat_tn(res["mixed"], [dy])[0]
    doa, dob, doc, dza, dzb, dzc = outproj_bwd(dy, res["oa"], res["ob"], res["oc"], proj, p["w_out"])
    dqc, dkc, dvc, dqw, dkw = dil_bwd(proj, p["q_norm_w"][None], p["k_norm_w"][None], *tabs, doc, nseq)
    g["q_norm_w"], g["k_norm_w"] = dqw.sum((0, 1)), dkw.sum((0, 1))
    dqb, dkb, dvb = sb_bwd(proj, res["carries"], dob, nseq)
    dya, dba, dalog, ddtb, dnw = gdn_bwd(res["ya"], proj, p["a_log"].reshape(N_HEADS_A, 1, 1),
                                         p["dt_bias"].reshape(N_HEADS_A, 1, 1), p["gdn_norm_w"][None], res["states"], doa, nseq)
    g["a_log"], g["dt_bias"], g["gdn_norm_w"] = dalog.sum(0).reshape(-1), ddtb.sum(0).reshape(-1), dnw.sum((0, 1))
    dqkv, dcw = conv_bwd(proj, p["conv_w"], dya, nseq)
    g["conv_w"] = dcw.sum(0)
    slabs = [dqkv, dza, dqc, dkc, dvc, dzc, dqb, dkb, dvb, dzb, dba]
    hdn = res["hdn"]
    g["w_in"] = jnp.concatenate(mat_tn(hdn, slabs[:2]) + mat_tn(hdn, slabs[2:6]) + mat_tn(hdn, slabs[6:]), axis=1)
    dx, dnw_tiles = inproj_bwd(slabs, p["w_in"], res["x"], p["norm_w"][None], dy)
    g["norm_w"] = dnw_tiles.sum((0, 1))
    return dx, g


SMALL = ("norm_w", "a_log", "dt_bias", "gdn_norm_w", "q_norm_w", "k_norm_w")


def _local_step(x, target, full):
    nseq, t, d = x.shape
    tabs = _rope_tables(t)
    h = x.reshape(nseq * t, d)
    saved = []
    for l in range(DEPTH):
        p = {k: v[l] for k, v in full.items()}
        h, res = _layer_fwd(h, p, nseq, tabs)
        saved.append((p, res))
    dy, parts = loss_fwd_bwd(h, target.reshape(nseq * t, d))
    loss = parts[:, 0, 0].sum()
    grads = [None] * DEPTH
    for l in reversed(range(DEPTH)):
        p, res = saved[l]
        dy, grads[l] = _layer_bwd(dy, p, res, nseq, tabs)
    return loss, dy.reshape(nseq, t, d), {k: jnp.stack([g[k] for g in grads]) for k in grads[0]}


def _pad_cols(w):
    b0 = ORIG_A + ORIG_BA
    c0 = b0 + ORIG_B
    zeros = jnp.zeros(w.shape[:-1] + (BLOCK - ORIG_BA,), w.dtype)
    return jnp.concatenate([w[..., :ORIG_A], w[..., c0:], w[..., b0:c0], w[..., ORIG_A:b0], zeros], axis=-1)


def _unpad_cols(w):
    return jnp.concatenate([w[..., :COL_C], w[..., COL_BA:COL_BA + ORIG_BA], w[..., COL_B:COL_BA], w[..., COL_C:COL_B]],
                           axis=-1)


def kernel(x, norm_w, w_in, conv_w, a_log, dt_bias, gdn_norm_w, q_norm_w, k_norm_w, w_out, loss_target, m_norm_w, m_w_in, m_conv_w, m_a_log, m_dt_bias, m_gdn_norm_w, m_q_norm_w, m_k_norm_w, m_w_out, v_norm_w, v_w_in, v_conv_w, v_a_log, v_dt_bias, v_gdn_norm_w, v_q_norm_w, v_k_norm_w, v_w_out):
    weights = dict(norm_w=norm_w, w_in=w_in, conv_w=conv_w, a_log=a_log, dt_bias=dt_bias, gdn_norm_w=gdn_norm_w,
                   q_norm_w=q_norm_w, k_norm_w=k_norm_w, w_out=w_out)
    moms = dict(norm_w=m_norm_w, w_in=m_w_in, conv_w=m_conv_w, a_log=m_a_log, dt_bias=m_dt_bias,
                gdn_norm_w=m_gdn_norm_w, q_norm_w=m_q_norm_w, k_norm_w=m_k_norm_w, w_out=m_w_out)
    vars_ = dict(norm_w=v_norm_w, w_in=v_w_in, conv_w=v_conv_w, a_log=v_a_log, dt_bias=v_dt_bias,
                 gdn_norm_w=v_gdn_norm_w, q_norm_w=v_q_norm_w, k_norm_w=v_k_norm_w, w_out=v_w_out)
    names = list(weights)
    sharded = ("w_in", "w_out", "conv_w")
    shard_shapes = [weights[k].shape for k in sharded]

    conv_bits = lax.bitcast_convert_type(conv_w, BF16).reshape(conv_w.shape[:2] + (2 * conv_w.shape[2],))
    got = plane_exchange(_pack([w_in.astype(BF16), w_out.astype(BF16), conv_bits]), all_to_all=False)
    per_chip = [_unpack(got[i], shard_shapes[:2] + [conv_bits.shape]) for i in range(N_CHIPS)]
    full = {k: weights[k] for k in SMALL}
    full["w_in"] = _pad_cols(jnp.concatenate([pc[0] for pc in per_chip], axis=2))
    full["w_out"] = jnp.concatenate([pc[1] for pc in per_chip], axis=1)
    full["conv_w"] = jnp.concatenate(
        [lax.bitcast_convert_type(pc[2].reshape(conv_w.shape + (2,)), F32) for pc in per_chip], axis=2)

    loss, grad_x, g = _local_step(x, loss_target, full)

    gw_in = _unpad_cols(g["w_in"])
    cols, rows = w_in.shape[2], w_out.shape[1]
    small = _pack_small(g)
    send = jnp.stack([_pack([gw_in[:, :, i * cols:(i + 1) * cols], g["w_out"][:, i * rows:(i + 1) * rows],
                             g["conv_w"][:, :, i * conv_w.shape[2]:(i + 1) * conv_w.shape[2]], small])
                      for i in range(N_CHIPS)])
    c = lax.axis_index("c")
    half = PACK_ROWS // 2
    keep = lax.dynamic_slice_in_dim(send, c * half, half, axis=1)
    give = lax.dynamic_slice_in_dim(send, (1 - c) * half, half, axis=1)
    chip_sum = add2(keep.reshape(N_CHIPS * half, PACK_COLS), sibling_swap(give).reshape(N_CHIPS * half, PACK_COLS))
    mine = sum4(plane_exchange(chip_sum.reshape(N_CHIPS, half, PACK_COLS), all_to_all=True))
    other = sibling_swap(mine)
    total = jnp.concatenate([jnp.where(c == 0, mine, other), jnp.where(c == 0, other, mine)])
    reduced = _unpack(total, shard_shapes + [(ROW_TILE, PACK_COLS)])
    grads = dict(zip(sharded, reduced[:3]))
    grads.update(zip(SMALL, _unpack_small(reduced[3], [weights[k].shape for k in SMALL])))
    loss = lax.psum(loss, ("x", "y", "c"))

    def two_d(a):
        return a.reshape(-1, a.shape[-1])

    delta, new_m, new_v = {}, {}, {}
    for k in names:
        d_, m_, v_ = adamw(two_d(weights[k]), two_d(grads[k]), two_d(moms[k]), two_d(vars_[k]))
        delta[k], new_m[k], new_v[k] = (a.reshape(weights[k].shape) for a in (d_, m_, v_))
    return (loss, grad_x, *[grads[k] for k in names], *[delta[k] for k in names],
            *[new_m[k] for k in names], *[new_v[k] for k in names])
```

```python
import functools
import math

import jax
import jax.numpy as jnp
from jax import lax
from jax.experimental import pallas as pl
from jax.experimental.pallas import tpu as pltpu

F32 = jnp.float32
BF16 = jnp.bfloat16

D_MODEL = 1024
SEQ = 2048
DEPTH = 2
HEAD_DIM = 64
N_HEADS_A, N_HEADS_B, N_HEADS_C = 6, 4, 6
A_WIDTH, B_WIDTH, C_WIDTH = N_HEADS_A * HEAD_DIM, N_HEADS_B * HEAD_DIM, N_HEADS_C * HEAD_DIM
CONV_WIDTH = 4
GDN_CHUNK = 64
BLOCK = 128
ROPE_DIM = 16
ROPE_THETA = 500000.0
DILATED_PAIRS = ((128, 1), (512, 4), (2048, 16))
RMS_EPS = 1e-6
NEG = -1e30

NT = (((1,), (1,)), ((), ()))
NN = (((1,), (0,)), ((), ()))
TN = (((0,), (0,)), ((), ()))

VMEM_LIMIT = 48 * 1024 * 1024

ORIG_A = 4 * A_WIDTH
ORIG_BA = 2 * N_HEADS_A
ORIG_B = 4 * B_WIDTH
COL_AZ = 3 * A_WIDTH
COL_C = 4 * A_WIDTH
COL_B = COL_C + 4 * C_WIDTH
COL_BA = COL_B + 4 * B_WIDTH
P_COLS = COL_BA + BLOCK
TN_COLS = 384
INPROJ_COLS = P_COLS // 3
TM_ROWS = 512
ROWS = 256


def _mm(a, b, dims=NN):
    return lax.dot_general(a.astype(BF16), b.astype(BF16), dims, preferred_element_type=F32)


def _mm32(a, b, dims=NN):
    return lax.dot_general(a, b, dims, precision=lax.Precision.HIGH, preferred_element_type=F32)


def _cparams(*sem):
    return pltpu.CompilerParams(dimension_semantics=sem, vmem_limit_bytes=VMEM_LIMIT)


def _sigmoid(x):
    return 0.5 * (jnp.tanh(0.5 * x) + 1.0)


def _softplus(x):
    return jnp.maximum(x, 0.0) + jnp.log(1.0 + jnp.exp(-jnp.abs(x)))


def _rms(x, w):
    return x * lax.rsqrt(jnp.mean(x * x, axis=-1, keepdims=True) + RMS_EPS) * w


def _heads(a, n):
    return jnp.stack([a[:, h * HEAD_DIM:(h + 1) * HEAD_DIM] for h in range(n)])


def _unheads(a):
    return jnp.concatenate([a[h] for h in range(a.shape[0])], axis=1)


def _row_chunks(t):
    return [pl.ds(c * ROWS, ROWS) for c in range(t // ROWS)]


def inproj_fwd(x, nw, w):
    n, d = x.shape
    p = w.shape[1]

    def body(x_ref, nw_ref, w_ref, proj_ref, hdn_ref):
        @pl.when(pl.program_id(1) == 0)
        def _():
            hdn_ref[...] = _rms(x_ref[...], nw_ref[...]).astype(BF16)

        proj_ref[...] = jnp.dot(hdn_ref[...], w_ref[...], preferred_element_type=F32)

    return pl.pallas_call(
        body, name="inproj_fwd", grid=(n // TM_ROWS, p // INPROJ_COLS),
        in_specs=[pl.BlockSpec((TM_ROWS, d), lambda i, j: (i, 0)), pl.BlockSpec((1, d), lambda i, j: (0, 0)),
                  pl.BlockSpec((d, INPROJ_COLS), lambda i, j: (0, j))],
        out_specs=[pl.BlockSpec((TM_ROWS, INPROJ_COLS), lambda i, j: (i, j)), pl.BlockSpec((TM_ROWS, d), lambda i, j: (i, 0))],
        out_shape=[jax.ShapeDtypeStruct((n, p), F32), jax.ShapeDtypeStruct((n, d), BF16)],
        compiler_params=_cparams("parallel", "arbitrary"),
    )(x, nw, w)


def mat_tn(a, slabs):
    n, ka = a.shape
    ns = len(slabs)

    def body(*refs):
        a_ref, s_refs, o_refs = refs[0], refs[1:1 + ns], refs[1 + ns:]

        @pl.when(pl.program_id(0) == 0)
        def _():
            for o_ref in o_refs:
                o_ref[...] = jnp.zeros_like(o_ref)

        av = a_ref[...]
        for s_ref, o_ref in zip(s_refs, o_refs):
            o_ref[...] += lax.dot_general(av, s_ref[...].astype(BF16), TN, preferred_element_type=F32)

    return pl.pallas_call(
        body, name="mat_tn", grid=(n // TM_ROWS,),
        in_specs=[pl.BlockSpec((TM_ROWS, ka), lambda k: (k, 0))]
                 + [pl.BlockSpec((TM_ROWS, s.shape[1]), lambda k: (k, 0)) for s in slabs],
        out_specs=[pl.BlockSpec((ka, s.shape[1]), lambda k: (0, 0)) for s in slabs],
        out_shape=[jax.ShapeDtypeStruct((ka, s.shape[1]), F32) for s in slabs],
        compiler_params=_cparams("arbitrary"),
    )(a, *slabs)


def inproj_bwd(slabs, w, x, nw, dy):
    n, d = x.shape
    p = w.shape[1]
    tm = 256
    ns = len(slabs)

    def body(*refs):
        s_refs = refs[:ns]
        w_ref, x_ref, nw_ref, dy_ref, dx_ref, dnw_ref = refs[ns:]
        dh = jnp.zeros((tm, d), F32)
        at = 0
        for s_ref in s_refs:
            wd = s_ref.shape[1]
            dh = dh + lax.dot_general(s_ref[...].astype(BF16), w_ref[:, at:at + wd], NT, preferred_element_type=F32)
            at += wd
        _, vjp = jax.vjp(_rms, x_ref[...], nw_ref[...])
        dx, dnw = vjp(dh)
        dx_ref[...] = dx + dy_ref[...]
        dnw_ref[0] = dnw

    return pl.pallas_call(
        body, name="inproj_bwd", grid=(n // tm,),
        in_specs=[pl.BlockSpec((tm, s.shape[1]), lambda i: (i, 0)) for s in slabs]
                 + [pl.BlockSpec((d, p), lambda i: (0, 0)), pl.BlockSpec((tm, d), lambda i: (i, 0)),
                    pl.BlockSpec((1, d), lambda i: (0, 0)), pl.BlockSpec((tm, d), lambda i: (i, 0))],
        out_specs=[pl.BlockSpec((tm, d), lambda i: (i, 0)), pl.BlockSpec((1, 1, d), lambda i: (i, 0, 0))],
        out_shape=[jax.ShapeDtypeStruct((n, d), F32), jax.ShapeDtypeStruct((n // tm, 1, d), F32)],
        compiler_params=_cparams("parallel"),
    )(*slabs, w, x, nw, dy)


CONV_PAD = 8
CONV_ROWS = 256


def _conv_pre(pad_s, cw, c):
    xs = [pad_s[pl.ds(c * CONV_ROWS + CONV_PAD - (CONV_WIDTH - 1) + k, CONV_ROWS), :] for k in range(CONV_WIDTH)]
    pre = xs[0] * cw[0:1, :]
    for k in range(1, CONV_WIDTH):
        pre = pre + xs[k] * cw[k:k + 1, :]
    return pre, xs


def conv_fwd(proj, cw, nseq):
    n = proj.shape[0]
    t = n // nseq
    ch = cw.shape[1]

    def body(x_ref, cw_ref, y_ref, pad_s):
        pad_s[pl.ds(0, CONV_PAD), :] = jnp.zeros((CONV_PAD, TN_COLS), F32)
        pad_s[pl.ds(CONV_PAD, t), :] = x_ref[...]
        cwv = cw_ref[...]
        for c in range(t // CONV_ROWS):
            pre, _ = _conv_pre(pad_s, cwv, c)
            y_ref[pl.ds(c * CONV_ROWS, CONV_ROWS), :] = pre * _sigmoid(pre)

    return pl.pallas_call(
        body, name="conv_fwd", grid=(nseq, ch // TN_COLS),
        in_specs=[pl.BlockSpec((t, TN_COLS), lambda b, j: (b, j)), pl.BlockSpec((CONV_WIDTH, TN_COLS), lambda b, j: (0, j))],
        out_specs=pl.BlockSpec((t, TN_COLS), lambda b, j: (b, j)),
        out_shape=jax.ShapeDtypeStruct((n, ch), F32),
        scratch_shapes=[pltpu.VMEM((t + CONV_PAD, TN_COLS), F32)],
        compiler_params=_cparams("parallel", "parallel"),
    )(proj, cw)


def conv_bwd(proj, cw, dy, nseq):
    n = proj.shape[0]
    t = n // nseq
    ch = cw.shape[1]

    def body(x_ref, cw_ref, dy_ref, dx_ref, dcw_ref, pad_s, dpad_s):
        pad_s[pl.ds(0, CONV_PAD), :] = jnp.zeros((CONV_PAD, TN_COLS), F32)
        pad_s[pl.ds(CONV_PAD, t), :] = x_ref[...]
        dpad_s[pl.ds(t, CONV_PAD), :] = jnp.zeros((CONV_PAD, TN_COLS), F32)
        cwv = cw_ref[...]
        acc = [jnp.zeros((1, TN_COLS), F32)] * CONV_WIDTH
        for c in range(t // CONV_ROWS):
            pre, xs = _conv_pre(pad_s, cwv, c)
            sg = _sigmoid(pre)
            dpre = dy_ref[pl.ds(c * CONV_ROWS, CONV_ROWS), :] * (sg * (1.0 + pre * (1.0 - sg)))
            dpad_s[pl.ds(c * CONV_ROWS, CONV_ROWS), :] = dpre
            acc = [acc[k] + jnp.sum(dpre * xs[k], axis=0, keepdims=True) for k in range(CONV_WIDTH)]
        for k in range(CONV_WIDTH):
            dcw_ref[0, pl.ds(k, 1), :] = acc[k]
        for c in range(t // CONV_ROWS):
            dx = dpad_s[pl.ds(c * CONV_ROWS + CONV_WIDTH - 1, CONV_ROWS), :] * cwv[0:1, :]
            for k in range(1, CONV_WIDTH):
                dx = dx + dpad_s[pl.ds(c * CONV_ROWS + CONV_WIDTH - 1 - k, CONV_ROWS), :] * cwv[k:k + 1, :]
            dx_ref[pl.ds(c * CONV_ROWS, CONV_ROWS), :] = dx

    blk = pl.BlockSpec((t, TN_COLS), lambda b, j: (b, j))
    return pl.pallas_call(
        body, name="conv_bwd", grid=(nseq, ch // TN_COLS),
        in_specs=[blk, pl.BlockSpec((CONV_WIDTH, TN_COLS), lambda b, j: (0, j)), blk],
        out_specs=[blk, pl.BlockSpec((1, CONV_WIDTH, TN_COLS), lambda b, j: (b, 0, j))],
        out_shape=[jax.ShapeDtypeStruct((n, ch), F32), jax.ShapeDtypeStruct((nseq, CONV_WIDTH, ch), F32)],
        scratch_shapes=[pltpu.VMEM((t + CONV_PAD, TN_COLS), F32)] * 2,
        compiler_params=_cparams("parallel", "parallel"),
    )(proj, cw, dy)


def _gate_specs(d):
    wide = pl.BlockSpec((TM_ROWS, d), lambda i: (i, 0))
    oa = pl.BlockSpec((TM_ROWS, A_WIDTH), lambda i: (i, 0))
    ob = pl.BlockSpec((TM_ROWS, B_WIDTH), lambda i: (i, 0))
    oc = pl.BlockSpec((TM_ROWS, C_WIDTH), lambda i: (i, 0))
    za = pl.BlockSpec((TM_ROWS, A_WIDTH), lambda i: (i, COL_AZ // A_WIDTH))
    zb = pl.BlockSpec((TM_ROWS, B_WIDTH), lambda i: (i, (COL_B + 3 * B_WIDTH) // B_WIDTH))
    zc = pl.BlockSpec((TM_ROWS, C_WIDTH), lambda i: (i, (COL_C + 3 * C_WIDTH) // C_WIDTH))
    return wide, oa, ob, oc, za, zb, zc


BRANCH_COLS = ((0, A_WIDTH), (A_WIDTH, A_WIDTH + B_WIDTH), (A_WIDTH + B_WIDTH, D_MODEL))


def outproj_fwd(x, oa, ob, oc, proj, w):
    n, d = x.shape

    def body(x_ref, oa_ref, ob_ref, oc_ref, za_ref, zb_ref, zc_ref, w_ref, y_ref, m_ref):
        for (lo, hi), o_ref, z_ref in zip(BRANCH_COLS, (oa_ref, ob_ref, oc_ref), (za_ref, zb_ref, zc_ref)):
            zv = z_ref[...]
            m_ref[:, lo:hi] = (o_ref[...] * (zv * _sigmoid(zv))).astype(BF16)
        y_ref[...] = x_ref[...] + jnp.dot(m_ref[...], w_ref[...], preferred_element_type=F32)

    wide, sa, sb, sc, za, zb, zc = _gate_specs(d)
    return pl.pallas_call(
        body, name="outproj_fwd", grid=(n // TM_ROWS,),
        in_specs=[wide, sa, sb, sc, za, zb, zc, pl.BlockSpec((d, d), lambda i: (0, 0))],
        out_specs=[wide, wide],
        out_shape=[jax.ShapeDtypeStruct((n, d), F32), jax.ShapeDtypeStruct((n, d), BF16)],
        compiler_params=_cparams("parallel"),
    )(x, oa, ob, oc, proj, proj, proj, w)


def outproj_bwd(dy, oa, ob, oc, proj, w):
    n, d = dy.shape

    def body(dy_ref, oa_ref, ob_ref, oc_ref, za_ref, zb_ref, zc_ref, w_ref, doa_ref, dob_ref, doc_ref, dza_ref, dzb_ref, dzc_ref):
        dm = lax.dot_general(dy_ref[...].astype(BF16), w_ref[...], NT, preferred_element_type=F32)
        for (lo, hi), o_ref, z_ref, do_ref, dz_ref in zip(BRANCH_COLS, (oa_ref, ob_ref, oc_ref), (za_ref, zb_ref, zc_ref),
                                                          (doa_ref, dob_ref, doc_ref), (dza_ref, dzb_ref, dzc_ref)):
            zv = z_ref[...]
            sg = _sigmoid(zv)
            dmv = dm[:, lo:hi]
            do_ref[...] = dmv * (zv * sg)
            dz_ref[...] = dmv * o_ref[...] * (sg * (1.0 + zv * (1.0 - sg)))

    wide, sa, sb, sc, za, zb, zc = _gate_specs(d)
    sd = jax.ShapeDtypeStruct
    outs = [sd((n, A_WIDTH), F32), sd((n, B_WIDTH), F32), sd((n, C_WIDTH), F32)]
    return pl.pallas_call(
        body, name="outproj_bwd", grid=(n // TM_ROWS,),
        in_specs=[wide, sa, sb, sc, za, zb, zc, pl.BlockSpec((d, d), lambda i: (0, 0))],
        out_specs=[sa, sb, sc, sa, sb, sc],
        out_shape=outs + outs,
        compiler_params=_cparams("parallel"),
    )(dy, oa, ob, oc, proj, proj, proj, w)


def loss_fwd_bwd(y, target):
    n, d = y.shape

    def body(y_ref, t_ref, dy_ref, part_ref):
        e = y_ref[...] - t_ref[...]
        dy_ref[...] = e * (1.0 / d)
        part_ref[...] = jnp.zeros_like(part_ref) + 0.5 * jnp.sum(e * e) * (1.0 / d)

    blk = pl.BlockSpec((TM_ROWS, d), lambda i: (i, 0))
    return pl.pallas_call(
        body, name="loss", grid=(n // TM_ROWS,),
        in_specs=[blk, blk],
        out_specs=[blk, pl.BlockSpec((1, 8, BLOCK), lambda i: (i, 0, 0))],
        out_shape=[jax.ShapeDtypeStruct((n, d), F32), jax.ShapeDtypeStruct((n // TM_ROWS, 8, BLOCK), F32)],
        compiler_params=_cparams("parallel"),
    )(y, target)


ADAM_LR, ADAM_B1, ADAM_B2, ADAM_EPS, ADAM_WD, ADAM_STEP = 0.001, 0.9, 0.999, 1e-08, 0.01, 10


def adamw(w, g, m, v):
    r, c = w.shape
    tr = r if r <= 256 else 256

    def body(w_ref, g_ref, m_ref, v_ref, d_ref, nm_ref, nv_ref):
        gv = g_ref[...]
        nm = ADAM_B1 * m_ref[...] + (1.0 - ADAM_B1) * gv
        nv = ADAM_B2 * v_ref[...] + (1.0 - ADAM_B2) * (gv * gv)
        m_hat = nm / (1.0 - ADAM_B1 ** ADAM_STEP)
        v_hat = nv / (1.0 - ADAM_B2 ** ADAM_STEP)
        d_ref[...] = -ADAM_LR * (m_hat / (jnp.sqrt(v_hat) + ADAM_EPS) + ADAM_WD * w_ref[...])
        nm_ref[...] = nm
        nv_ref[...] = nv

    blk = pl.BlockSpec((tr, c), lambda i: (i, 0))
    return pl.pallas_call(
        body, name="adamw", grid=(r // tr,),
        in_specs=[blk] * 4, out_specs=[blk] * 3,
        out_shape=[jax.ShapeDtypeStruct((r, c), F32)] * 3,
        compiler_params=_cparams("parallel"),
    )(w, g, m, v)


SB_G = N_HEADS_B


def _sb_tile(q, k, v, carry, tri, diag):
    z = _mm(q * (HEAD_DIM ** -0.5), k, NT)
    sp = jnp.log(1.0 + jnp.exp(-jnp.abs(z)))
    ls_pos = jnp.minimum(z, 0.0) - sp
    ls_neg = jnp.minimum(-z, 0.0) - sp
    if diag:
        earlier = lax.broadcasted_iota(jnp.int32, z.shape, 1) < lax.broadcasted_iota(jnp.int32, z.shape, 0)
        log_keep = jnp.where(earlier, ls_neg, 0.0)
    else:
        log_keep = ls_neg
    hi = log_keep.astype(BF16)
    lo = lax.stop_gradient(log_keep - hi.astype(F32)).astype(BF16)
    within = lax.dot_general(jnp.concatenate([hi, lo], axis=1), tri, NN, preferred_element_type=F32)
    arg = ls_pos + within + carry
    wts = jnp.where(earlier, jnp.exp(jnp.where(earlier, arg, 0.0)), 0.0) if diag else jnp.exp(arg)
    return _mm(wts, v), jnp.sum(log_keep, axis=1, keepdims=True)


_sb_tiles_diag = jax.vmap(functools.partial(_sb_tile, diag=True), in_axes=(0, 0, 0, 0, None))
_sb_tiles_off = jax.vmap(functools.partial(_sb_tile, diag=False), in_axes=(0, 0, 0, 0, None))


def _sb_tri():
    r = lax.broadcasted_iota(jnp.int32, (2 * BLOCK, BLOCK), 0) % BLOCK
    c = lax.broadcasted_iota(jnp.int32, (2 * BLOCK, BLOCK), 1)
    return jnp.where(r > c, 1.0, 0.0).astype(BF16)


def _sb_specs(t, nq):
    cb = COL_B // B_WIDTH
    q = pl.BlockSpec((BLOCK, B_WIDTH), lambda b, i: (b * nq + i, cb))
    k = pl.BlockSpec((t, B_WIDTH), lambda b, i: (b, cb + 1))
    v = pl.BlockSpec((t, B_WIDTH), lambda b, i: (b, cb + 2))
    blk = pl.BlockSpec((BLOCK, B_WIDTH), lambda b, i: (b * nq + i, 0))
    full = pl.BlockSpec((t, B_WIDTH), lambda b, i: (b, 0))
    carry = pl.BlockSpec((1, 1, nq, BLOCK, SB_G), lambda b, i: (b, i, 0, 0, 0))
    return q, k, v, blk, full, carry


def sb_fwd(proj, nseq):
    n = proj.shape[0]
    t = n // nseq
    nq = t // BLOCK
    g = SB_G

    def body(q_ref, k_ref, v_ref, o_ref, carry_ref):
        i = pl.program_id(1)
        tri = _sb_tri()
        qv = _heads(q_ref[...], g)

        def tile(j, c, fn):
            rows = pl.ds(pl.multiple_of(j * BLOCK, BLOCK), BLOCK)
            carry_ref[0, 0, j] = jnp.concatenate([c[h] for h in range(g)], axis=1)
            return fn(qv, _heads(k_ref[rows, :], g), _heads(v_ref[rows, :], g), c, tri)

        def step(it, st):
            o_acc, c = st
            o, tot = tile(i - 1 - it, c, _sb_tiles_off)
            return o_acc + o, c + tot

        o_acc, _ = lax.fori_loop(0, i, step, tile(i, jnp.zeros((g, BLOCK, 1), F32), _sb_tiles_diag))
        o_ref[...] = _unheads(o_acc)

    q, k, v, blk, _, carry = _sb_specs(t, nq)
    return pl.pallas_call(
        body, name="sb_fwd", grid=(nseq, nq),
        in_specs=[q, k, v],
        out_specs=[blk, carry],
        out_shape=[jax.ShapeDtypeStruct((n, B_WIDTH), F32),
                   jax.ShapeDtypeStruct((nseq, nq, nq, BLOCK, g), F32)],
        compiler_params=_cparams("parallel", "arbitrary"),
    )(proj, proj, proj)


def sb_bwd(proj, carries, do, nseq):
    n = proj.shape[0]
    t = n // nseq
    nq = t // BLOCK
    g = SB_G

    def body(q_ref, k_ref, v_ref, carry_ref, do_ref, dq_ref, dk_ref, dv_ref):
        i = pl.program_id(1)

        @pl.when(i == 0)
        def _():
            dk_ref[...] = jnp.zeros_like(dk_ref)
            dv_ref[...] = jnp.zeros_like(dv_ref)

        tri = _sb_tri()
        qv = _heads(q_ref[...], g)
        dov = _heads(do_ref[...], g)

        def tile(j, st, fn):
            dq_acc, dc = st
            rows = pl.ds(pl.multiple_of(j * BLOCK, BLOCK), BLOCK)
            cj = carry_ref[0, 0, j]
            _, vjp = jax.vjp(lambda q_, k_, v_, c_: fn(q_, k_, v_, c_, tri), qv, _heads(k_ref[rows, :], g),
                             _heads(v_ref[rows, :], g), jnp.stack([cj[:, h:h + 1] for h in range(g)]))
            dq, dk, dv, dcj = vjp((dov, dc))
            dk_ref[rows, :] += _unheads(dk)
            dv_ref[rows, :] += _unheads(dv)
            return dq_acc + dq, dc + dcj

        st = lax.fori_loop(0, i, lambda j, st: tile(j, st, _sb_tiles_off),
                           (jnp.zeros((g, BLOCK, HEAD_DIM), F32), jnp.zeros((g, BLOCK, 1), F32)))
        dq_acc, _ = tile(i, st, _sb_tiles_diag)
        dq_ref[...] = _unheads(dq_acc)

    q, k, v, blk, full, carry = _sb_specs(t, nq)
    return pl.pallas_call(
        body, name="sb_bwd", grid=(nseq, nq),
        in_specs=[q, k, v, carry, blk],
        out_specs=[blk, full, full],
        out_shape=[jax.ShapeDtypeStruct((n, B_WIDTH), F32)] * 3,
        compiler_params=_cparams("parallel", "arbitrary"),
    )(proj, proj, proj, carries, do)


@jax.custom_vjp
def _unit_lower_inverse(a):
    n = a.shape[0]
    eye = jnp.where(lax.broadcasted_iota(jnp.int32, (n, n), 0) == lax.broadcasted_iota(jnp.int32, (n, n), 1), 1.0, 0.0)
    tmat = eye.astype(F32) - a
    p = a
    for _ in range(5):
        p = _mm32(p, p)
        tmat = tmat + _mm32(tmat, p)
    return tmat


def _unit_lower_inverse_fwd(a):
    tmat = _unit_lower_inverse(a)
    return tmat, tmat


def _unit_lower_inverse_bwd(tmat, g):
    return (-_mm32(_mm32(tmat, g, TN), tmat, NT),)


_unit_lower_inverse.defvjp(_unit_lower_inverse_fwd, _unit_lower_inverse_bwd)


def _gdn_chunk(q, k, v, al_c, al_r, br_c, alog, dtb, nw, s):
    c = GDN_CHUNK
    ri = lax.broadcasted_iota(jnp.int32, (c, c), 0)
    ci = lax.broadcasted_iota(jnp.int32, (c, c), 1)
    incl, strict = ri >= ci, ri > ci
    eye = jnp.where(ri == ci, 1.0, 0.0).astype(F32)
    rate = -jnp.exp(alog)
    g_c = rate * _softplus(al_c + dtb)
    g_r = rate * _softplus(al_r + dtb)
    beta = _sigmoid(br_c)
    gc_c = jnp.sum(jnp.where(incl, g_r, 0.0), axis=1, keepdims=True)
    gc_r = jnp.sum(jnp.where(ri <= ci, g_c, 0.0), axis=0, keepdims=True)
    gl = jnp.sum(g_r, axis=1, keepdims=True)
    decay = jnp.where(incl, jnp.exp(jnp.where(incl, gc_c - gc_r, 0.0)), 0.0)
    qn = q * lax.rsqrt(jnp.sum(q * q, axis=-1, keepdims=True) + RMS_EPS) * (HEAD_DIM ** -0.5)
    kn = k * lax.rsqrt(jnp.sum(k * k, axis=-1, keepdims=True) + RMS_EPS)
    kb = kn * beta
    a = jnp.where(strict, _mm(kb, kn, NT) * decay, 0.0)
    tmat = _unit_lower_inverse(a)
    u = _mm(tmat, v * beta)
    w = _mm(tmat, kb * jnp.exp(gc_c))
    qk = _mm(qn, kn, NT) * decay
    v_new = u - _mm(w, s)
    o = _mm(qn * jnp.exp(gc_c), s) + _mm(qk, v_new)
    s_new = s * jnp.exp(gl) + _mm(kn * jnp.exp(gl - gc_c), v_new, TN)
    o = o * lax.rsqrt(jnp.mean(o * o, axis=-1, keepdims=True) + RMS_EPS) * nw
    return o, s_new


_gdn_chunks = jax.vmap(_gdn_chunk, in_axes=(0, 0, 0, 0, 0, 0, 0, 0, None, 0))

GDN_TB = 256
GDN_SEQ_FWD = 2
GDN_SEQ_BWD = 2


def _gdn_block(q3, k3, v3, ba, alog, dtb, nw, s):
    nh = N_HEADS_A
    ns = q3.shape[0]
    bat = [ba[b].T for b in range(ns)]
    br_c = jnp.stack([ba[b][:, h:h + 1] for b in range(ns) for h in range(nh)])
    al_c = jnp.stack([ba[b][:, nh + h:nh + h + 1] for b in range(ns) for h in range(nh)])
    al_r = jnp.stack([bat[b][nh + h:nh + h + 1, :] for b in range(ns) for h in range(nh)])
    heads = lambda a: jnp.concatenate([_heads(a[b], nh) for b in range(ns)])
    o, s_new = _gdn_chunks(heads(q3), heads(k3), heads(v3), al_c, al_r, br_c,
                           jnp.concatenate([alog] * ns), jnp.concatenate([dtb] * ns), nw, s)
    return jnp.stack([_unheads(o[b * nh:(b + 1) * nh]) for b in range(ns)]), s_new


def _gdn_specs(nt, sq, rev):
    tpos = (lambda i: nt - 1 - i) if rev else (lambda i: i)
    ncb = GDN_TB // GDN_CHUNK
    qkv = [pl.BlockSpec((sq, GDN_TB, A_WIDTH), lambda b, i, j=j: (b, tpos(i), j)) for j in range(3)]
    ba = pl.BlockSpec((sq, GDN_TB, BLOCK), lambda b, i: (b, tpos(i), COL_BA // BLOCK))
    one = pl.BlockSpec((N_HEADS_A, 1, 1), lambda b, i: (0, 0, 0))
    vec = pl.BlockSpec((1, HEAD_DIM), lambda b, i: (0, 0))
    st = pl.BlockSpec((sq, ncb, N_HEADS_A, HEAD_DIM, HEAD_DIM), lambda b, i: (b, tpos(i), 0, 0, 0))
    oa = pl.BlockSpec((sq, GDN_TB, A_WIDTH), lambda b, i: (b, tpos(i), 0))
    return qkv, ba, one, vec, st, oa, tpos


def gdn_fwd(ya, proj, alog, dtb, nw, nseq):
    n = ya.shape[0]
    t = n // nseq
    nc, nt, ncb = t // GDN_CHUNK, t // GDN_TB, GDN_TB // GDN_CHUNK
    sq = GDN_SEQ_FWD
    nh = N_HEADS_A

    def body(q_ref, k_ref, v_ref, ba_ref, alog_ref, dtb_ref, nw_ref, o_ref, st_ref, s_s):
        @pl.when(pl.program_id(1) == 0)
        def _():
            s_s[...] = jnp.zeros_like(s_s)

        def step(c, s):
            rows = pl.ds(pl.multiple_of(c * GDN_CHUNK, GDN_CHUNK), GDN_CHUNK)
            for b in range(sq):
                st_ref[b, c] = s[b * nh:(b + 1) * nh]
            o, s_new = _gdn_block(q_ref[:, rows, :], k_ref[:, rows, :], v_ref[:, rows, :], ba_ref[:, rows, :],
                                  alog_ref[...], dtb_ref[...], nw_ref[...], s)
            o_ref[:, rows, :] = o
            return s_new

        s_s[...] = lax.fori_loop(0, ncb, step, s_s[...])

    qkv, ba, one, vec, st, oa, _ = _gdn_specs(nt, sq, False)
    ya3, proj3 = ya.reshape(nseq, t, -1), proj.reshape(nseq, t, -1)
    o, states = pl.pallas_call(
        body, name="gdn_fwd", grid=(nseq // sq, nt),
        in_specs=qkv + [ba, one, one, vec],
        out_specs=[oa, st],
        out_shape=[jax.ShapeDtypeStruct((nseq, t, A_WIDTH), F32),
                   jax.ShapeDtypeStruct((nseq, nc, nh, HEAD_DIM, HEAD_DIM), F32)],
        scratch_shapes=[pltpu.VMEM((sq * nh, HEAD_DIM, HEAD_DIM), F32)],
        compiler_params=_cparams("parallel", "arbitrary"),
    )(ya3, ya3, ya3, proj3, alog, dtb, nw)
    return o.reshape(n, A_WIDTH), states


def gdn_bwd(ya, proj, alog, dtb, nw, states, do, nseq):
    n = ya.shape[0]
    t = n // nseq
    nt, ncb = t // GDN_TB, GDN_TB // GDN_CHUNK
    nh = N_HEADS_A
    sq = GDN_SEQ_BWD

    def body(q_ref, k_ref, v_ref, ba_ref, alog_ref, dtb_ref, nw_ref, st_ref, do_ref,
             dya_ref, dba_ref, dalog_ref, ddtb_ref, dnw_ref, ds_s):
        @pl.when(pl.program_id(1) == 0)
        def _():
            ds_s[...] = jnp.zeros_like(ds_s)
            dalog_ref[...] = jnp.zeros_like(dalog_ref)
            ddtb_ref[...] = jnp.zeros_like(ddtb_ref)
            dnw_ref[...] = jnp.zeros_like(dnw_ref)

        def step(it, carry):
            ds, dalog, ddtb, dnw = carry
            c = ncb - 1 - it
            rows = pl.ds(pl.multiple_of(c * GDN_CHUNK, GDN_CHUNK), GDN_CHUNK)
            s_in = jnp.concatenate([st_ref[b, c] for b in range(sq)])
            _, vjp = jax.vjp(_gdn_block, q_ref[:, rows, :], k_ref[:, rows, :], v_ref[:, rows, :], ba_ref[:, rows, :],
                             alog_ref[...], dtb_ref[...], nw_ref[...], s_in)
            dq, dk, dv, dba, da, dd, dn, ds = vjp((do_ref[:, rows, :], ds))
            dya_ref[:, rows, 0:A_WIDTH] = dq
            dya_ref[:, rows, A_WIDTH:2 * A_WIDTH] = dk
            dya_ref[:, rows, 2 * A_WIDTH:3 * A_WIDTH] = dv
            dba_ref[:, rows, :] = dba
            return ds, dalog + da, ddtb + dd, dnw + dn

        z11 = jnp.zeros((nh, 1, 1), F32)
        ds, dalog, ddtb, dnw = lax.fori_loop(0, ncb, step, (ds_s[...], z11, z11, jnp.zeros((1, HEAD_DIM), F32)))
        ds_s[...] = ds
        dalog_ref[0] += dalog
        ddtb_ref[0] += ddtb
        dnw_ref[0] += dnw

    qkv, ba, one, vec, st, oa, tpos = _gdn_specs(nt, sq, True)
    per_grp = pl.BlockSpec((1, nh, 1, 1), lambda b, i: (b, 0, 0, 0))
    sd = jax.ShapeDtypeStruct
    ya3, proj3, do3 = ya.reshape(nseq, t, -1), proj.reshape(nseq, t, -1), do.reshape(nseq, t, -1)
    dya, dba, dalog, ddtb, dnw = pl.pallas_call(
        body, name="gdn_bwd", grid=(nseq // sq, nt),
        in_specs=qkv + [ba, one, one, vec, st, oa],
        out_specs=[pl.BlockSpec((sq, GDN_TB, 3 * A_WIDTH), lambda b, i: (b, tpos(i), 0)),
                   pl.BlockSpec((sq, GDN_TB, BLOCK), lambda b, i: (b, tpos(i), 0)),
                   per_grp, per_grp, pl.BlockSpec((1, 1, HEAD_DIM), lambda b, i: (b, 0, 0))],
        out_shape=[sd((nseq, t, 3 * A_WIDTH), F32), sd((nseq, t, BLOCK), F32), sd((nseq // sq, nh, 1, 1), F32),
                   sd((nseq // sq, nh, 1, 1), F32), sd((nseq // sq, 1, HEAD_DIM), F32)],
        scratch_shapes=[pltpu.VMEM((sq * nh, HEAD_DIM, HEAD_DIM), F32)],
        compiler_params=_cparams("parallel", "arbitrary"),
    )(ya3, ya3, ya3, proj3, alog, dtb, nw, states, do3)
    return dya.reshape(n, 3 * A_WIDTH), dba.reshape(n, BLOCK), dalog, ddtb, dnw


DIL_NB = tuple((SEQ // d) // BLOCK for _, d in DILATED_PAIRS)
DIL_D = tuple(d for _, d in DILATED_PAIRS)
DIL_STEPS = tuple(w // d for w, d in DILATED_PAIRS)
DIL_B = 4


def _rope_tables(t):
    half = ROPE_DIM // 2
    inv_freq = ROPE_THETA ** (-jnp.arange(half, dtype=F32) / half)
    ang = jnp.arange(t, dtype=F32)[:, None] * inv_freq[None, :]
    ones = jnp.ones((t, HEAD_DIM - ROPE_DIM), F32)
    cs = jnp.concatenate([jnp.cos(ang), jnp.cos(ang), ones], axis=1)
    sn = jnp.concatenate([jnp.sin(ang), jnp.sin(ang), 0.0 * ones], axis=1)
    i = jnp.arange(HEAD_DIM)[:, None]
    j = jnp.arange(HEAD_DIM)[None, :]
    pm = (jnp.where((j < half) & (i == j + half), -1.0, 0.0)
          + jnp.where((j >= half) & (j < ROPE_DIM) & (i == j - half), 1.0, 0.0))
    return cs, sn, jnp.concatenate([pm, pm]).astype(BF16)


def _dil_prep(x, w, cs, sn, pm):
    y = x * lax.rsqrt(jnp.mean(x * x, axis=-1, keepdims=True) + RMS_EPS) * w
    hi = y.astype(BF16)
    lo = lax.stop_gradient(y - hi.astype(F32)).astype(BF16)
    return y * cs + lax.dot_general(jnp.concatenate([hi, lo], axis=1), pm, NN, preferred_element_type=F32) * sn


def _dil_tile(qn, kk, vv, bias):
    s = _mm(qn * (HEAD_DIM ** -0.5), kk, NT) + bias
    m = lax.stop_gradient(jnp.max(s, axis=-1, keepdims=True))
    p = jnp.exp(s - m)
    denom = jnp.sum(p, axis=-1, keepdims=True)
    return _mm(p, vv) / denom, m + jnp.log(denom)


_dil_tiles = jax.vmap(_dil_tile)


def _dil_mix(o1, o2, o3, l1, l2, l3):
    m = lax.stop_gradient(jnp.maximum(jnp.maximum(l1, l2), l3))
    e1, e2, e3 = jnp.exp(l1 - m), jnp.exp(l2 - m), jnp.exp(l3 - m)
    return (e1 * o1 + e2 * o2 + e3 * o3) / (e1 + e2 + e3)


def _dil_fill_biases(bias_s):
    steps, = set(DIL_STEPS)
    qi = lax.broadcasted_iota(jnp.int32, (BLOCK, 1), 0)
    kj = lax.broadcasted_iota(jnp.int32, (1, 2 * BLOCK), 1)
    rel = qi - kj + BLOCK
    inside = (rel >= 0) & (rel <= steps)
    bias_s[0] = jnp.where(inside, 0.0, NEG)
    bias_s[1] = jnp.where(inside & (kj >= BLOCK), 0.0, NEG)
    bias_s[2] = jnp.where((qi >= kj) & (qi - kj <= steps), 0.0, NEG)


def _dil_mask(it, g, bias_s):
    which = jnp.where(it == 0, 2, jnp.where(it % DIL_NB[g] == 0, 1, 0))
    kstart = jnp.maximum(it - 1, 0) * BLOCK
    return bias_s[which], pl.ds(pl.multiple_of(it * BLOCK, BLOCK), BLOCK), pl.ds(pl.multiple_of(kstart, BLOCK), 2 * BLOCK)


def _dil_gather(src, dst, d):
    t = src.shape[0]
    ln = t // d
    for r in range(d):
        dst[pl.ds(r * ln, ln), :] = src[pl.ds(r, ln, stride=d), :]


def _dil_scatter(src, dst, d):
    t = src.shape[0]
    ln = t // d
    for r in range(d):
        dst[pl.ds(r, ln, stride=d), :] = src[pl.ds(r * ln, ln), :]


def _half(x, h):
    return jnp.where(h == 0, x[:, :HEAD_DIM], x[:, HEAD_DIM:])


def _store_half(ref, rows, val, h):
    @pl.when(h == 0)
    def _():
        ref[rows, 0:HEAD_DIM] = val

    @pl.when(h == 1)
    def _():
        ref[rows, HEAD_DIM:2 * HEAD_DIM] = val


def _dil_forward_parts(h, q_ref, k_ref, v_ref, qw, kw, cs_ref, sn_ref, pm, qn_s, kn_s, v_s, dl_s, od_s, ld_s, on_s, ln_s,
                       bias_s):
    t = qn_s.shape[0]
    _dil_fill_biases(bias_s)

    def prep(c, _):
        rows = pl.ds(pl.multiple_of(c * ROWS, ROWS), ROWS)
        qn_s[rows, :] = _dil_prep(_half(q_ref[rows, :], h), qw, cs_ref[rows, :], sn_ref[rows, :], pm)
        kn_s[rows, :] = _dil_prep(_half(k_ref[rows, :], h), kw, cs_ref[rows, :], sn_ref[rows, :], pm)
        v_s[rows, :] = _half(v_ref[rows, :], h)
        return 0

    lax.fori_loop(0, t // ROWS, prep, 0)
    for g in (1, 2):
        _dil_gather(qn_s, dl_s.at[g - 1, 0], DIL_D[g])
        _dil_gather(kn_s, dl_s.at[g - 1, 1], DIL_D[g])
        _dil_gather(v_s, dl_s.at[g - 1, 2], DIL_D[g])
    for g in range(3):
        qs = qn_s if g == 0 else dl_s.at[g - 1, 0]
        ks = kn_s if g == 0 else dl_s.at[g - 1, 1]
        vs = v_s if g == 0 else dl_s.at[g - 1, 2]

        def tiles(i, _, g=g, qs=qs, ks=ks, vs=vs):
            where = [_dil_mask(i * DIL_B + b, g, bias_s) for b in range(DIL_B)]
            o, lse = _dil_tiles(jnp.stack([qs[qr, :] for _, qr, _ in where]), jnp.stack([ks[kr, :] for _, _, kr in where]),
                                jnp.stack([vs[kr, :] for _, _, kr in where]), jnp.stack([m for m, _, _ in where]))
            for b, (_, qr, _) in enumerate(where):
                od_s[g, qr, :] = o[b]
                ld_s[g, qr, :] = lse[b]
            return 0

        lax.fori_loop(0, t // BLOCK // DIL_B, tiles, 0)
    for g in (1, 2):
        _dil_scatter(od_s.at[g], on_s.at[g - 1], DIL_D[g])
        _dil_scatter(ld_s.at[g], ln_s.at[g - 1], DIL_D[g])


def _dil_scratch(t):
    return [pltpu.VMEM((t, HEAD_DIM), F32), pltpu.VMEM((t, HEAD_DIM), F32), pltpu.VMEM((t, HEAD_DIM), F32),
            pltpu.VMEM((2, 3, t, HEAD_DIM), F32),
            pltpu.VMEM((3, t, HEAD_DIM), F32), pltpu.VMEM((3, t, 1), F32),
            pltpu.VMEM((2, t, HEAD_DIM), F32), pltpu.VMEM((2, t, 1), F32),
            pltpu.VMEM((3, BLOCK, 2 * BLOCK), F32)]


def _dil_specs(t):
    cb = COL_C // BLOCK
    per = C_WIDTH // BLOCK
    qkv = [pl.BlockSpec((t, BLOCK), lambda b, p, h, j=j: (b, cb + j * per + p)) for j in range(3)]
    vec = pl.BlockSpec((1, HEAD_DIM), lambda b, p, h: (0, 0))
    tab = pl.BlockSpec((t, HEAD_DIM), lambda b, p, h: (0, 0))
    mat = pl.BlockSpec((2 * HEAD_DIM, HEAD_DIM), lambda b, p, h: (0, 0))
    pair = pl.BlockSpec((t, BLOCK), lambda b, p, h: (b, p))
    return qkv, vec, tab, mat, pair


def dil_fwd(proj, qw, kw, cs, sn, pm, nseq):
    n = proj.shape[0]
    t = n // nseq

    def body(q_ref, k_ref, v_ref, qw_ref, kw_ref, cs_ref, sn_ref, pm_ref, o_ref, qn_s, kn_s, v_s, dl_s, od_s, ld_s, on_s, ln_s,
             bias_s):
        h = pl.program_id(2)
        _dil_forward_parts(h, q_ref, k_ref, v_ref, qw_ref[...], kw_ref[...], cs_ref, sn_ref, pm_ref[...],
                           qn_s, kn_s, v_s, dl_s, od_s, ld_s, on_s, ln_s, bias_s)

        def mix(c, _):
            rows = pl.ds(pl.multiple_of(c * ROWS, ROWS), ROWS)
            _store_half(o_ref, rows, _dil_mix(od_s[0, rows, :], on_s[0, rows, :], on_s[1, rows, :],
                                              ld_s[0, rows, :], ln_s[0, rows, :], ln_s[1, rows, :]), h)
            return 0

        lax.fori_loop(0, t // ROWS, mix, 0)

    qkv, vec, tab, mat, pair = _dil_specs(t)
    return pl.pallas_call(
        body, name="dil_fwd", grid=(nseq, C_WIDTH // BLOCK, 2),
        in_specs=qkv + [vec, vec, tab, tab, mat],
        out_specs=pair,
        out_shape=jax.ShapeDtypeStruct((n, C_WIDTH), F32),
        scratch_shapes=_dil_scratch(t),
        compiler_params=_cparams("parallel", "parallel", "arbitrary"),
    )(proj, proj, proj, qw, kw, cs, sn, pm)


def dil_bwd(proj, qw, kw, cs, sn, pm, do, nseq):
    n = proj.shape[0]
    t = n // nseq
    dh = HEAD_DIM

    def body(q_ref, k_ref, v_ref, qw_ref, kw_ref, cs_ref, sn_ref, pm_ref, do_ref,
             dq_ref, dk_ref, dv_ref, dqw_ref, dkw_ref,
             qn_s, kn_s, v_s, dl_s, od_s, ld_s, on_s, ln_s, bias_s, tq_s, tk_s, tv_s, dv_s):
        h = pl.program_id(2)
        qw, kw, pm = qw_ref[...], kw_ref[...], pm_ref[...]
        _dil_forward_parts(h, q_ref, k_ref, v_ref, qw, kw, cs_ref, sn_ref, pm, qn_s, kn_s, v_s, dl_s, od_s, ld_s, on_s, ln_s,
                           bias_s)

        def mix(c, _):
            rows = pl.ds(pl.multiple_of(c * ROWS, ROWS), ROWS)
            _, vjp = jax.vjp(_dil_mix, od_s[0, rows, :], on_s[0, rows, :], on_s[1, rows, :],
                             ld_s[0, rows, :], ln_s[0, rows, :], ln_s[1, rows, :])
            d1, d2, d3, e1, e2, e3 = vjp(_half(do_ref[rows, :], h))
            od_s[0, rows, :] = d1
            on_s[0, rows, :] = d2
            on_s[1, rows, :] = d3
            ld_s[0, rows, :] = e1
            ln_s[0, rows, :] = e2
            ln_s[1, rows, :] = e3
            return 0

        lax.fori_loop(0, t // ROWS, mix, 0)
        for g in (1, 2):
            _dil_gather(on_s.at[g - 1], od_s.at[g], DIL_D[g])
            _dil_gather(ln_s.at[g - 1], ld_s.at[g], DIL_D[g])
        on_s[...] = jnp.zeros_like(on_s)
        dv_s[...] = jnp.zeros_like(dv_s)
        for g in range(3):
            qs = qn_s if g == 0 else dl_s.at[g - 1, 0]
            ks = kn_s if g == 0 else dl_s.at[g - 1, 1]
            vs = v_s if g == 0 else dl_s.at[g - 1, 2]
            gq = on_s.at[0] if g == 0 else tq_s
            gk = on_s.at[1] if g == 0 else tk_s
            gv = dv_s if g == 0 else tv_s
            if g > 0:
                tk_s[...] = jnp.zeros_like(tk_s)
                tv_s[...] = jnp.zeros_like(tv_s)

            def tiles(i, _, g=g, qs=qs, ks=ks, vs=vs, gq=gq, gk=gk, gv=gv):
                where = [_dil_mask(i * DIL_B + b, g, bias_s) for b in range(DIL_B)]
                biases = jnp.stack([m for m, _, _ in where])
                _, vjp = jax.vjp(lambda q_, k_, v_: _dil_tiles(q_, k_, v_, biases),
                                 jnp.stack([qs[qr, :] for _, qr, _ in where]), jnp.stack([ks[kr, :] for _, _, kr in where]),
                                 jnp.stack([vs[kr, :] for _, _, kr in where]))
                dq, dkk, dvv = vjp((jnp.stack([od_s[g, qr, :] for _, qr, _ in where]),
                                    jnp.stack([ld_s[g, qr, :] for _, qr, _ in where])))
                for b, (_, qr, kr) in enumerate(where):
                    gq[qr, :] = dq[b]
                    gk[kr, :] += dkk[b]
                    gv[kr, :] += dvv[b]
                return 0

            lax.fori_loop(0, t // BLOCK // DIL_B, tiles, 0)
            if g > 0:
                d = DIL_D[g]
                ln = t // d
                for r in range(d):
                    nat, dil = pl.ds(r, ln, stride=d), pl.ds(r * ln, ln)
                    on_s[0, nat, :] += tq_s[dil, :]
                    on_s[1, nat, :] += tk_s[dil, :]
                    dv_s[nat, :] += tv_s[dil, :]

        def prep(c, acc):
            rows = pl.ds(pl.multiple_of(c * ROWS, ROWS), ROWS)
            f = lambda x, w: _dil_prep(x, w, cs_ref[rows, :], sn_ref[rows, :], pm)
            _, vq = jax.vjp(f, _half(q_ref[rows, :], h), qw)
            _, vk = jax.vjp(f, _half(k_ref[rows, :], h), kw)
            dq, dqw = vq(on_s[0, rows, :])
            dk, dkw = vk(on_s[1, rows, :])
            _store_half(dq_ref, rows, dq, h)
            _store_half(dk_ref, rows, dk, h)
            _store_half(dv_ref, rows, dv_s[rows, :], h)
            return acc[0] + dqw, acc[1] + dkw

        dqw, dkw = lax.fori_loop(0, t // ROWS, prep, (jnp.zeros((1, dh), F32), jnp.zeros((1, dh), F32)))
        dqw_ref[0] = dqw
        dkw_ref[0] = dkw

    qkv, vec, tab, mat, pair = _dil_specs(t)
    per = C_WIDTH // BLOCK
    wout = pl.BlockSpec((1, 1, dh), lambda b, p, h: ((b * per + p) * 2 + h, 0, 0))
    return pl.pallas_call(
        body, name="dil_bwd", grid=(nseq, per, 2),
        in_specs=qkv + [vec, vec, tab, tab, mat, pair],
        out_specs=[pair, pair, pair, wout, wout],
        out_shape=[jax.ShapeDtypeStruct((n, C_WIDTH), F32)] * 3 + [jax.ShapeDtypeStruct((nseq * N_HEADS_C, 1, dh), F32)] * 2,
        scratch_shapes=_dil_scratch(t) + [pltpu.VMEM((t, dh), F32)] * 4,
        compiler_params=_cparams("parallel", "parallel", "arbitrary"),
    )(proj, proj, proj, qw, kw, cs, sn, pm, do)


N_CHIPS = 4
SUM_ROWS = 432
MESH_IDS = pl.DeviceIdType.MESH
ANY = pl.BlockSpec(memory_space=pl.ANY)


def plane_exchange(src, all_to_all):
    blk_shape = src.shape[1:] if all_to_all else src.shape

    def body(src_ref, out_ref, send_sems, recv_sems, local_sem):
        x, y, c = lax.axis_index("x"), lax.axis_index("y"), lax.axis_index("c")
        me = 2 * x + y
        mine = pltpu.make_async_copy(src_ref.at[me] if all_to_all else src_ref, out_ref.at[me], local_sem)
        mine.start()
        sends = []
        for k in (1, 2, 3):
            px = 1 - x if k & 2 else x
            py = 1 - y if k & 1 else y
            peer = 2 * px + py
            cp = pltpu.make_async_remote_copy(
                src_ref=src_ref.at[peer] if all_to_all else src_ref, dst_ref=out_ref.at[me],
                send_sem=send_sems.at[k - 1], recv_sem=recv_sems.at[k - 1],
                device_id=(px, py, c), device_id_type=MESH_IDS)
            cp.start()
            sends.append((cp, peer, (px, py, c)))
        for k, (cp, peer, dev) in enumerate(sends):
            pltpu.make_async_remote_copy(
                src_ref=out_ref.at[me], dst_ref=out_ref.at[peer],
                send_sem=send_sems.at[k], recv_sem=recv_sems.at[k],
                device_id=dev, device_id_type=MESH_IDS).wait_recv()
        for cp, _, _ in sends:
            cp.wait_send()
        mine.wait()

    return pl.pallas_call(
        body, name="plane_all_to_all" if all_to_all else "plane_all_gather",
        in_specs=[ANY], out_specs=ANY,
        out_shape=jax.ShapeDtypeStruct((N_CHIPS,) + blk_shape, src.dtype),
        scratch_shapes=[pltpu.SemaphoreType.DMA((3,)), pltpu.SemaphoreType.DMA((3,)), pltpu.SemaphoreType.DMA],
    )(src)


def sibling_swap(src):
    def body(src_ref, out_ref, send_sem, recv_sem):
        x, y, c = lax.axis_index("x"), lax.axis_index("y"), lax.axis_index("c")
        cp = pltpu.make_async_remote_copy(src_ref=src_ref, dst_ref=out_ref, send_sem=send_sem, recv_sem=recv_sem,
                                          device_id=(x, y, 1 - c), device_id_type=MESH_IDS)
        cp.start()
        cp.wait()

    return pl.pallas_call(
        body, name="sibling_swap", in_specs=[ANY], out_specs=ANY,
        out_shape=jax.ShapeDtypeStruct(src.shape, src.dtype),
        scratch_shapes=[pltpu.SemaphoreType.DMA, pltpu.SemaphoreType.DMA],
    )(src)


def sum4(a):
    _, r, c = a.shape
    tr = SUM_ROWS

    def body(a_ref, o_ref):
        p = [a_ref[i].astype(F32) for i in range(N_CHIPS)]
        o_ref[...] = (p[0] + p[1]) + (p[2] + p[3])

    return pl.pallas_call(
        body, name="sum4", grid=(r // tr,),
        in_specs=[pl.BlockSpec((N_CHIPS, tr, c), lambda i: (0, i, 0))],
        out_specs=pl.BlockSpec((tr, c), lambda i: (i, 0)),
        out_shape=jax.ShapeDtypeStruct((r, c), F32),
        compiler_params=_cparams("parallel"),
    )(a)


def add2(a, b, dtype):
    r, c = a.shape
    tr = SUM_ROWS

    def body(a_ref, b_ref, o_ref):
        o_ref[...] = (a_ref[...] + b_ref[...]).astype(dtype)

    blk = pl.BlockSpec((tr, c), lambda i: (i, 0))
    return pl.pallas_call(
        body, name="add2", grid=(r // tr,), in_specs=[blk, blk], out_specs=blk,
        out_shape=jax.ShapeDtypeStruct((r, c), dtype), compiler_params=_cparams("parallel"),
    )(a, b)


PACK_COLS = 1152
PACK_ROWS = 2592
ROW_TILE = 16


def _pack(parts):
    blocks = []
    for p in parts:
        p2 = p.reshape(-1, p.shape[-1])
        blocks.append(jnp.pad(p2, ((0, -p2.shape[0] % ROW_TILE), (0, PACK_COLS - p2.shape[1]))))
    rows = sum(b.shape[0] for b in blocks)
    blocks.append(jnp.zeros((PACK_ROWS - rows, PACK_COLS), blocks[0].dtype))
    return jnp.concatenate(blocks)


def _unpack(buf, shapes):
    out, at = [], 0
    for s in shapes:
        rows = math.prod(s[:-1])
        out.append(buf[at:at + rows, :s[-1]].reshape(s))
        at += rows + (-rows % ROW_TILE)
    return out


def _pack_small(g):
    blk = jnp.zeros((ROW_TILE, PACK_COLS), F32)
    for i, k in enumerate(SMALL):
        blk = blk.at[2 * i:2 * i + 2, :g[k].shape[1]].set(g[k])
    return blk


def _unpack_small(blk, shapes):
    return [blk[2 * i:2 * i + 2, :s[1]] for i, s in enumerate(shapes)]


def _layer_fwd(x, p, nseq, tabs):
    proj, hdn = inproj_fwd(x, p["norm_w"][None], p["w_in"])
    ya = conv_fwd(proj, p["conv_w"], nseq)
    oa, states = gdn_fwd(ya, proj, p["a_log"].reshape(N_HEADS_A, 1, 1), p["dt_bias"].reshape(N_HEADS_A, 1, 1),
                         p["gdn_norm_w"][None], nseq)
    ob, carries = sb_fwd(proj, nseq)
    oc = dil_fwd(proj, p["q_norm_w"][None], p["k_norm_w"][None], *tabs, nseq)
    y, mixed = outproj_fwd(x, oa, ob, oc, proj, p["w_out"])
    return y, dict(x=x, hdn=hdn, proj=proj, ya=ya, states=states, carries=carries, oa=oa, ob=ob, oc=oc, mixed=mixed)


def _layer_bwd(dy, p, res, nseq, tabs):
    proj = res["proj"]
    g = {}
    g["w_out"] = mat_tn(res["mixed"], [dy])[0]
    doa, dob, doc, dza, dzb, dzc = outproj_bwd(dy, res["oa"], res["ob"], res["oc"], proj, p["w_out"])
    dqc, dkc, dvc, dqw, dkw = dil_bwd(proj, p["q_norm_w"][None], p["k_norm_w"][None], *tabs, doc, nseq)
    g["q_norm_w"], g["k_norm_w"] = dqw.sum((0, 1)), dkw.sum((0, 1))
    dqb, dkb, dvb = sb_bwd(proj, res["carries"], dob, nseq)
    dya, dba, dalog, ddtb, dnw = gdn_bwd(res["ya"], proj, p["a_log"].reshape(N_HEADS_A, 1, 1),
                                         p["dt_bias"].reshape(N_HEADS_A, 1, 1), p["gdn_norm_w"][None], res["states"], doa, nseq)
    g["a_log"], g["dt_bias"], g["gdn_norm_w"] = dalog.sum(0).reshape(-1), ddtb.sum(0).reshape(-1), dnw.sum((0, 1))
    dqkv, dcw = conv_bwd(proj, p["conv_w"], dya, nseq)
    g["conv_w"] = dcw.sum(0)
    slabs = [dqkv, dza, dqc, dkc, dvc, dzc, dqb, dkb, dvb, dzb, dba]
    hdn = res["hdn"]
    g["w_in"] = jnp.concatenate(mat_tn(hdn, slabs[:2]) + mat_tn(hdn, slabs[2:6]) + mat_tn(hdn, slabs[6:]), axis=1)
    dx, dnw_tiles = inproj_bwd(slabs, p["w_in"], res["x"], p["norm_w"][None], dy)
    g["norm_w"] = dnw_tiles.sum((0, 1))
    return dx, g


SMALL = ("norm_w", "a_log", "dt_bias", "gdn_norm_w", "q_norm_w", "k_norm_w")


def _local_step(x, target, full):
    nseq, t, d = x.shape
    tabs = _rope_tables(t)
    h = x.reshape(nseq * t, d)
    saved = []
    for l in range(DEPTH):
        p = {k: v[l] for k, v in full.items()}
        h, res = _layer_fwd(h, p, nseq, tabs)
        saved.append((p, res))
    dy, parts = loss_fwd_bwd(h, target.reshape(nseq * t, d))
    loss = parts[:, 0, 0].sum()
    grads = [None] * DEPTH
    for l in reversed(range(DEPTH)):
        p, res = saved[l]
        dy, grads[l] = _layer_bwd(dy, p, res, nseq, tabs)
    return loss, dy.reshape(nseq, t, d), {k: jnp.stack([g[k] for g in grads]) for k in grads[0]}


def _pad_cols(w):
    b0 = ORIG_A + ORIG_BA
    c0 = b0 + ORIG_B
    zeros = jnp.zeros(w.shape[:-1] + (BLOCK - ORIG_BA,), w.dtype)
    return jnp.concatenate([w[..., :ORIG_A], w[..., c0:], w[..., b0:c0], w[..., ORIG_A:b0], zeros], axis=-1)


def _unpad_cols(w):
    return jnp.concatenate([w[..., :COL_C], w[..., COL_BA:COL_BA + ORIG_BA], w[..., COL_B:COL_BA], w[..., COL_C:COL_B]],
                           axis=-1)


def kernel(x, norm_w, w_in, conv_w, a_log, dt_bias, gdn_norm_w, q_norm_w, k_norm_w, w_out, loss_target, m_norm_w, m_w_in, m_conv_w, m_a_log, m_dt_bias, m_gdn_norm_w, m_q_norm_w, m_k_norm_w, m_w_out, v_norm_w, v_w_in, v_conv_w, v_a_log, v_dt_bias, v_gdn_norm_w, v_q_norm_w, v_k_norm_w, v_w_out):
    weights = dict(norm_w=norm_w, w_in=w_in, conv_w=conv_w, a_log=a_log, dt_bias=dt_bias, gdn_norm_w=gdn_norm_w,
                   q_norm_w=q_norm_w, k_norm_w=k_norm_w, w_out=w_out)
    moms = dict(norm_w=m_norm_w, w_in=m_w_in, conv_w=m_conv_w, a_log=m_a_log, dt_bias=m_dt_bias,
                gdn_norm_w=m_gdn_norm_w, q_norm_w=m_q_norm_w, k_norm_w=m_k_norm_w, w_out=m_w_out)
    vars_ = dict(norm_w=v_norm_w, w_in=v_w_in, conv_w=v_conv_w, a_log=v_a_log, dt_bias=v_dt_bias,
                 gdn_norm_w=v_gdn_norm_w, q_norm_w=v_q_norm_w, k_norm_w=v_k_norm_w, w_out=v_w_out)
    names = list(weights)
    sharded = ("w_in", "w_out", "conv_w")
    shard_shapes = [weights[k].shape for k in sharded]

    c = lax.axis_index("c")
    half = PACK_ROWS // 2
    conv_bits = lax.bitcast_convert_type(conv_w, BF16).reshape(conv_w.shape[:2] + (2 * conv_w.shape[2],))
    shard = _pack([w_in.astype(BF16), w_out.astype(BF16), conv_bits])
    mine = plane_exchange(lax.dynamic_slice_in_dim(shard, c * half, half, axis=0), all_to_all=False)
    other = sibling_swap(mine)
    got = jnp.concatenate([jnp.where(c == 0, mine, other), jnp.where(c == 0, other, mine)], axis=1)
    per_chip = [_unpack(got[i], shard_shapes[:2] + [conv_bits.shape]) for i in range(N_CHIPS)]
    full = {k: weights[k] for k in SMALL}
    full["w_in"] = _pad_cols(jnp.concatenate([pc[0] for pc in per_chip], axis=2))
    full["w_out"] = jnp.concatenate([pc[1] for pc in per_chip], axis=1)
    full["conv_w"] = jnp.concatenate(
        [lax.bitcast_convert_type(pc[2].reshape(conv_w.shape + (2,)), F32) for pc in per_chip], axis=2)

    loss, grad_x, g = _local_step(x, loss_target, full)

    gw_in = _unpad_cols(g["w_in"])
    cols, rows = w_in.shape[2], w_out.shape[1]
    small = _pack_small(g)
    send = jnp.stack([_pack([gw_in[:, :, i * cols:(i + 1) * cols], g["w_out"][:, i * rows:(i + 1) * rows],
                             g["conv_w"][:, :, i * conv_w.shape[2]:(i + 1) * conv_w.shape[2]], small])
                      for i in range(N_CHIPS)])
    keep = lax.dynamic_slice_in_dim(send, c * half, half, axis=1)
    give = lax.dynamic_slice_in_dim(send, (1 - c) * half, half, axis=1)
    chip_sum = add2(keep.reshape(N_CHIPS * half, PACK_COLS), sibling_swap(give).reshape(N_CHIPS * half, PACK_COLS), BF16)
    mine = sum4(plane_exchange(chip_sum.reshape(N_CHIPS, half, PACK_COLS), all_to_all=True))
    other = sibling_swap(mine)
    total = jnp.concatenate([jnp.where(c == 0, mine, other), jnp.where(c == 0, other, mine)])
    reduced = _unpack(total, shard_shapes + [(ROW_TILE, PACK_COLS)])
    grads = dict(zip(sharded, reduced[:3]))
    grads.update(zip(SMALL, _unpack_small(reduced[3], [weights[k].shape for k in SMALL])))
    loss = lax.psum(loss, ("x", "y", "c"))

    def two_d(a):
        return a.reshape(-1, a.shape[-1])

    delta, new_m, new_v = {}, {}, {}
    for k in names:
        d_, m_, v_ = adamw(two_d(weights[k]), two_d(grads[k]), two_d(moms[k]), two_d(vars_[k]))
        delta[k], new_m[k], new_v[k] = (a.reshape(weights[k].shape) for a in (d_, m_, v_))
    return (loss, grad_x, *[grads[k] for k in names], *[delta[k] for k in names],
            *[new_m[k] for k in names], *[new_v[k] for k in names])
```

```python
import functools
import math

import jax
import jax.numpy as jnp
from jax import lax
from jax.experimental import pallas as pl
from jax.experimental.pallas import tpu as pltpu

F32 = jnp.float32
BF16 = jnp.bfloat16

D_MODEL = 1024
SEQ = 2048
DEPTH = 2
HEAD_DIM = 64
N_HEADS_A, N_HEADS_B, N_HEADS_C = 6, 4, 6
A_WIDTH, B_WIDTH, C_WIDTH = N_HEADS_A * HEAD_DIM, N_HEADS_B * HEAD_DIM, N_HEADS_C * HEAD_DIM
CONV_WIDTH = 4
GDN_CHUNK = 64
BLOCK = 128
ROPE_DIM = 16
ROPE_THETA = 500000.0
DILATED_PAIRS = ((128, 1), (512, 4), (2048, 16))
RMS_EPS = 1e-6
NEG = -1e30

NT = (((1,), (1,)), ((), ()))
NN = (((1,), (0,)), ((), ()))
TN = (((0,), (0,)), ((), ()))

VMEM_LIMIT = 48 * 1024 * 1024

ORIG_A = 4 * A_WIDTH
ORIG_BA = 2 * N_HEADS_A
ORIG_B = 4 * B_WIDTH
COL_AZ = 3 * A_WIDTH
COL_C = 4 * A_WIDTH
COL_B = COL_C + 4 * C_WIDTH
COL_BA = COL_B + 4 * B_WIDTH
P_COLS = COL_BA + BLOCK
TN_COLS = 384
INPROJ_COLS = P_COLS // 3
TM_ROWS = 512
ROWS = 256


def _mm(a, b, dims=NN):
    return lax.dot_general(a.astype(BF16), b.astype(BF16), dims, preferred_element_type=F32)


def _mm32(a, b, dims=NN):
    return lax.dot_general(a, b, dims, precision=lax.Precision.HIGH, preferred_element_type=F32)


def _cparams(*sem):
    return pltpu.CompilerParams(dimension_semantics=sem, vmem_limit_bytes=VMEM_LIMIT)


def _sigmoid(x):
    return 0.5 * (jnp.tanh(0.5 * x) + 1.0)


def _softplus(x):
    return jnp.maximum(x, 0.0) + jnp.log(1.0 + jnp.exp(-jnp.abs(x)))


def _rms(x, w):
    return x * lax.rsqrt(jnp.mean(x * x, axis=-1, keepdims=True) + RMS_EPS) * w


def _heads(a, n):
    return jnp.stack([a[:, h * HEAD_DIM:(h + 1) * HEAD_DIM] for h in range(n)])


def _unheads(a):
    return jnp.concatenate([a[h] for h in range(a.shape[0])], axis=1)


def _row_chunks(t):
    return [pl.ds(c * ROWS, ROWS) for c in range(t // ROWS)]


def inproj_fwd(x, nw, w):
    n, d = x.shape
    p = w.shape[1]

    def body(x_ref, nw_ref, w_ref, proj_ref, hdn_ref):
        @pl.when(pl.program_id(1) == 0)
        def _():
            hdn_ref[...] = _rms(x_ref[...], nw_ref[...]).astype(BF16)

        proj_ref[...] = jnp.dot(hdn_ref[...], w_ref[...], preferred_element_type=F32)

    return pl.pallas_call(
        body, name="inproj_fwd", grid=(n // TM_ROWS, p // INPROJ_COLS),
        in_specs=[pl.BlockSpec((TM_ROWS, d), lambda i, j: (i, 0)), pl.BlockSpec((1, d), lambda i, j: (0, 0)),
                  pl.BlockSpec((d, INPROJ_COLS), lambda i, j: (0, j))],
        out_specs=[pl.BlockSpec((TM_ROWS, INPROJ_COLS), lambda i, j: (i, j)), pl.BlockSpec((TM_ROWS, d), lambda i, j: (i, 0))],
        out_shape=[jax.ShapeDtypeStruct((n, p), F32), jax.ShapeDtypeStruct((n, d), BF16)],
        compiler_params=_cparams("parallel", "arbitrary"),
    )(x, nw, w)


def mat_tn(a, slabs):
    n, ka = a.shape
    ns = len(slabs)

    def body(*refs):
        a_ref, s_refs, o_refs = refs[0], refs[1:1 + ns], refs[1 + ns:]

        @pl.when(pl.program_id(0) == 0)
        def _():
            for o_ref in o_refs:
                o_ref[...] = jnp.zeros_like(o_ref)

        av = a_ref[...]
        for s_ref, o_ref in zip(s_refs, o_refs):
            o_ref[...] += lax.dot_general(av, s_ref[...].astype(BF16), TN, preferred_element_type=F32)

    return pl.pallas_call(
        body, name="mat_tn", grid=(n // TM_ROWS,),
        in_specs=[pl.BlockSpec((TM_ROWS, ka), lambda k: (k, 0))]
                 + [pl.BlockSpec((TM_ROWS, s.shape[1]), lambda k: (k, 0)) for s in slabs],
        out_specs=[pl.BlockSpec((ka, s.shape[1]), lambda k: (0, 0)) for s in slabs],
        out_shape=[jax.ShapeDtypeStruct((ka, s.shape[1]), F32) for s in slabs],
        compiler_params=_cparams("arbitrary"),
    )(a, *slabs)


def inproj_bwd(slabs, w, x, nw, dy):
    n, d = x.shape
    p = w.shape[1]
    tm = 256
    ns = len(slabs)

    def body(*refs):
        s_refs = refs[:ns]
        w_ref, x_ref, nw_ref, dy_ref, dx_ref, dnw_ref = refs[ns:]
        dh = jnp.zeros((tm, d), F32)
        at = 0
        for s_ref in s_refs:
            wd = s_ref.shape[1]
            dh = dh + lax.dot_general(s_ref[...].astype(BF16), w_ref[:, at:at + wd], NT, preferred_element_type=F32)
            at += wd
        _, vjp = jax.vjp(_rms, x_ref[...], nw_ref[...])
        dx, dnw = vjp(dh)
        dx_ref[...] = dx + dy_ref[...]
        dnw_ref[0] = dnw

    return pl.pallas_call(
        body, name="inproj_bwd", grid=(n // tm,),
        in_specs=[pl.BlockSpec((tm, s.shape[1]), lambda i: (i, 0)) for s in slabs]
                 + [pl.BlockSpec((d, p), lambda i: (0, 0)), pl.BlockSpec((tm, d), lambda i: (i, 0)),
                    pl.BlockSpec((1, d), lambda i: (0, 0)), pl.BlockSpec((tm, d), lambda i: (i, 0))],
        out_specs=[pl.BlockSpec((tm, d), lambda i: (i, 0)), pl.BlockSpec((1, 1, d), lambda i: (i, 0, 0))],
        out_shape=[jax.ShapeDtypeStruct((n, d), F32), jax.ShapeDtypeStruct((n // tm, 1, d), F32)],
        compiler_params=_cparams("parallel"),
    )(*slabs, w, x, nw, dy)


CONV_PAD = 8
CONV_ROWS = 256


def _conv_pre(pad_s, cw, c):
    xs = [pad_s[pl.ds(c * CONV_ROWS + CONV_PAD - (CONV_WIDTH - 1) + k, CONV_ROWS), :] for k in range(CONV_WIDTH)]
    pre = xs[0] * cw[0:1, :]
    for k in range(1, CONV_WIDTH):
        pre = pre + xs[k] * cw[k:k + 1, :]
    return pre, xs


def conv_fwd(proj, cw, nseq):
    n = proj.shape[0]
    t = n // nseq
    ch = cw.shape[1]

    def body(x_ref, cw_ref, y_ref, pad_s):
        pad_s[pl.ds(0, CONV_PAD), :] = jnp.zeros((CONV_PAD, TN_COLS), F32)
        pad_s[pl.ds(CONV_PAD, t), :] = x_ref[...]
        cwv = cw_ref[...]
        for c in range(t // CONV_ROWS):
            pre, _ = _conv_pre(pad_s, cwv, c)
            y_ref[pl.ds(c * CONV_ROWS, CONV_ROWS), :] = pre * _sigmoid(pre)

    return pl.pallas_call(
        body, name="conv_fwd", grid=(nseq, ch // TN_COLS),
        in_specs=[pl.BlockSpec((t, TN_COLS), lambda b, j: (b, j)), pl.BlockSpec((CONV_WIDTH, TN_COLS), lambda b, j: (0, j))],
        out_specs=pl.BlockSpec((t, TN_COLS), lambda b, j: (b, j)),
        out_shape=jax.ShapeDtypeStruct((n, ch), F32),
        scratch_shapes=[pltpu.VMEM((t + CONV_PAD, TN_COLS), F32)],
        compiler_params=_cparams("parallel", "parallel"),
    )(proj, cw)


def conv_bwd(proj, cw, dy, nseq):
    n = proj.shape[0]
    t = n // nseq
    ch = cw.shape[1]

    def body(x_ref, cw_ref, dy_ref, dx_ref, dcw_ref, pad_s, dpad_s):
        pad_s[pl.ds(0, CONV_PAD), :] = jnp.zeros((CONV_PAD, TN_COLS), F32)
        pad_s[pl.ds(CONV_PAD, t), :] = x_ref[...]
        dpad_s[pl.ds(t, CONV_PAD), :] = jnp.zeros((CONV_PAD, TN_COLS), F32)
        cwv = cw_ref[...]
        acc = [jnp.zeros((1, TN_COLS), F32)] * CONV_WIDTH
        for c in range(t // CONV_ROWS):
            pre, xs = _conv_pre(pad_s, cwv, c)
            sg = _sigmoid(pre)
            dpre = dy_ref[pl.ds(c * CONV_ROWS, CONV_ROWS), :] * (sg * (1.0 + pre * (1.0 - sg)))
            dpad_s[pl.ds(c * CONV_ROWS, CONV_ROWS), :] = dpre
            acc = [acc[k] + jnp.sum(dpre * xs[k], axis=0, keepdims=True) for k in range(CONV_WIDTH)]
        for k in range(CONV_WIDTH):
            dcw_ref[0, pl.ds(k, 1), :] = acc[k]
        for c in range(t // CONV_ROWS):
            dx = dpad_s[pl.ds(c * CONV_ROWS + CONV_WIDTH - 1, CONV_ROWS), :] * cwv[0:1, :]
            for k in range(1, CONV_WIDTH):
                dx = dx + dpad_s[pl.ds(c * CONV_ROWS + CONV_WIDTH - 1 - k, CONV_ROWS), :] * cwv[k:k + 1, :]
            dx_ref[pl.ds(c * CONV_ROWS, CONV_ROWS), :] = dx

    blk = pl.BlockSpec((t, TN_COLS), lambda b, j: (b, j))
    return pl.pallas_call(
        body, name="conv_bwd", grid=(nseq, ch // TN_COLS),
        in_specs=[blk, pl.BlockSpec((CONV_WIDTH, TN_COLS), lambda b, j: (0, j)), blk],
        out_specs=[blk, pl.BlockSpec((1, CONV_WIDTH, TN_COLS), lambda b, j: (b, 0, j))],
        out_shape=[jax.ShapeDtypeStruct((n, ch), F32), jax.ShapeDtypeStruct((nseq, CONV_WIDTH, ch), F32)],
        scratch_shapes=[pltpu.VMEM((t + CONV_PAD, TN_COLS), F32)] * 2,
        compiler_params=_cparams("parallel", "parallel"),
    )(proj, cw, dy)


def _gate_specs(d):
    wide = pl.BlockSpec((TM_ROWS, d), lambda i: (i, 0))
    oa = pl.BlockSpec((TM_ROWS, A_WIDTH), lambda i: (i, 0))
    ob = pl.BlockSpec((TM_ROWS, B_WIDTH), lambda i: (i, 0))
    oc = pl.BlockSpec((TM_ROWS, C_WIDTH), lambda i: (i, 0))
    za = pl.BlockSpec((TM_ROWS, A_WIDTH), lambda i: (i, COL_AZ // A_WIDTH))
    zb = pl.BlockSpec((TM_ROWS, B_WIDTH), lambda i: (i, (COL_B + 3 * B_WIDTH) // B_WIDTH))
    zc = pl.BlockSpec((TM_ROWS, C_WIDTH), lambda i: (i, (COL_C + 3 * C_WIDTH) // C_WIDTH))
    return wide, oa, ob, oc, za, zb, zc


BRANCH_COLS = ((0, A_WIDTH), (A_WIDTH, A_WIDTH + B_WIDTH), (A_WIDTH + B_WIDTH, D_MODEL))


def outproj_fwd(x, oa, ob, oc, proj, w):
    n, d = x.shape

    def body(x_ref, oa_ref, ob_ref, oc_ref, za_ref, zb_ref, zc_ref, w_ref, y_ref, m_ref):
        for (lo, hi), o_ref, z_ref in zip(BRANCH_COLS, (oa_ref, ob_ref, oc_ref), (za_ref, zb_ref, zc_ref)):
            zv = z_ref[...]
            m_ref[:, lo:hi] = (o_ref[...] * (zv * _sigmoid(zv))).astype(BF16)
        y_ref[...] = x_ref[...] + jnp.dot(m_ref[...], w_ref[...], preferred_element_type=F32)

    wide, sa, sb, sc, za, zb, zc = _gate_specs(d)
    return pl.pallas_call(
        body, name="outproj_fwd", grid=(n // TM_ROWS,),
        in_specs=[wide, sa, sb, sc, za, zb, zc, pl.BlockSpec((d, d), lambda i: (0, 0))],
        out_specs=[wide, wide],
        out_shape=[jax.ShapeDtypeStruct((n, d), F32), jax.ShapeDtypeStruct((n, d), BF16)],
        compiler_params=_cparams("parallel"),
    )(x, oa, ob, oc, proj, proj, proj, w)


def outproj_bwd(dy, oa, ob, oc, proj, w):
    n, d = dy.shape

    def body(dy_ref, oa_ref, ob_ref, oc_ref, za_ref, zb_ref, zc_ref, w_ref, doa_ref, dob_ref, doc_ref, dza_ref, dzb_ref, dzc_ref):
        dm = lax.dot_general(dy_ref[...].astype(BF16), w_ref[...], NT, preferred_element_type=F32)
        for (lo, hi), o_ref, z_ref, do_ref, dz_ref in zip(BRANCH_COLS, (oa_ref, ob_ref, oc_ref), (za_ref, zb_ref, zc_ref),
                                                          (doa_ref, dob_ref, doc_ref), (dza_ref, dzb_ref, dzc_ref)):
            zv = z_ref[...]
            sg = _sigmoid(zv)
            dmv = dm[:, lo:hi]
            do_ref[...] = dmv * (zv * sg)
            dz_ref[...] = dmv * o_ref[...] * (sg * (1.0 + zv * (1.0 - sg)))

    wide, sa, sb, sc, za, zb, zc = _gate_specs(d)
    sd = jax.ShapeDtypeStruct
    outs = [sd((n, A_WIDTH), F32), sd((n, B_WIDTH), F32), sd((n, C_WIDTH), F32)]
    return pl.pallas_call(
        body, name="outproj_bwd", grid=(n // TM_ROWS,),
        in_specs=[wide, sa, sb, sc, za, zb, zc, pl.BlockSpec((d, d), lambda i: (0, 0))],
        out_specs=[sa, sb, sc, sa, sb, sc],
        out_shape=outs + outs,
        compiler_params=_cparams("parallel"),
    )(dy, oa, ob, oc, proj, proj, proj, w)


def loss_fwd_bwd(y, target):
    n, d = y.shape

    def body(y_ref, t_ref, dy_ref, part_ref):
        e = y_ref[...] - t_ref[...]
        dy_ref[...] = e * (1.0 / d)
        part_ref[...] = jnp.zeros_like(part_ref) + 0.5 * jnp.sum(e * e) * (1.0 / d)

    blk = pl.BlockSpec((TM_ROWS, d), lambda i: (i, 0))
    return pl.pallas_call(
        body, name="loss", grid=(n // TM_ROWS,),
        in_specs=[blk, blk],
        out_specs=[blk, pl.BlockSpec((1, 8, BLOCK), lambda i: (i, 0, 0))],
        out_shape=[jax.ShapeDtypeStruct((n, d), F32), jax.ShapeDtypeStruct((n // TM_ROWS, 8, BLOCK), F32)],
        compiler_params=_cparams("parallel"),
    )(y, target)


ADAM_LR, ADAM_B1, ADAM_B2, ADAM_EPS, ADAM_WD, ADAM_STEP = 0.001, 0.9, 0.999, 1e-08, 0.01, 10


def adamw(w, g, m, v):
    r, c = w.shape
    tr = r if r <= 256 else 256

    def body(w_ref, g_ref, m_ref, v_ref, d_ref, nm_ref, nv_ref):
        gv = g_ref[...]
        nm = ADAM_B1 * m_ref[...] + (1.0 - ADAM_B1) * gv
        nv = ADAM_B2 * v_ref[...] + (1.0 - ADAM_B2) * (gv * gv)
        m_hat = nm / (1.0 - ADAM_B1 ** ADAM_STEP)
        v_hat = nv / (1.0 - ADAM_B2 ** ADAM_STEP)
        d_ref[...] = -ADAM_LR * (m_hat / (jnp.sqrt(v_hat) + ADAM_EPS) + ADAM_WD * w_ref[...])
        nm_ref[...] = nm
        nv_ref[...] = nv

    blk = pl.BlockSpec((tr, c), lambda i: (i, 0))
    return pl.pallas_call(
        body, name="adamw", grid=(r // tr,),
        in_specs=[blk] * 4, out_specs=[blk] * 3,
        out_shape=[jax.ShapeDtypeStruct((r, c), F32)] * 3,
        compiler_params=_cparams("parallel"),
    )(w, g, m, v)


SB_G = N_HEADS_B


def _sb_tile(q, k, v, carry, tri, diag):
    z = _mm(q * (HEAD_DIM ** -0.5), k, NT)
    sp = jnp.log(1.0 + jnp.exp(-jnp.abs(z)))
    ls_pos = jnp.minimum(z, 0.0) - sp
    ls_neg = jnp.minimum(-z, 0.0) - sp
    if diag:
        earlier = lax.broadcasted_iota(jnp.int32, z.shape, 1) < lax.broadcasted_iota(jnp.int32, z.shape, 0)
        log_keep = jnp.where(earlier, ls_neg, 0.0)
    else:
        log_keep = ls_neg
    hi = log_keep.astype(BF16)
    lo = lax.stop_gradient(log_keep - hi.astype(F32)).astype(BF16)
    within = lax.dot_general(jnp.concatenate([hi, lo], axis=1), tri, NN, preferred_element_type=F32)
    arg = ls_pos + within + carry
    wts = jnp.where(earlier, jnp.exp(jnp.where(earlier, arg, 0.0)), 0.0) if diag else jnp.exp(arg)
    return _mm(wts, v), jnp.sum(log_keep, axis=1, keepdims=True)


_sb_tiles_diag = jax.vmap(functools.partial(_sb_tile, diag=True), in_axes=(0, 0, 0, 0, None))
_sb_tiles_off = jax.vmap(functools.partial(_sb_tile, diag=False), in_axes=(0, 0, 0, 0, None))


def _sb_tri():
    r = lax.broadcasted_iota(jnp.int32, (2 * BLOCK, BLOCK), 0) % BLOCK
    c = lax.broadcasted_iota(jnp.int32, (2 * BLOCK, BLOCK), 1)
    return jnp.where(r > c, 1.0, 0.0).astype(BF16)


SB_SEQ = 2


def _sb_specs(t, nq):
    cb = COL_B // B_WIDTH
    sq = SB_SEQ
    q = pl.BlockSpec((sq, BLOCK, B_WIDTH), lambda b, i: (b, i, cb))
    k = pl.BlockSpec((sq, t, B_WIDTH), lambda b, i: (b, 0, cb + 1))
    v = pl.BlockSpec((sq, t, B_WIDTH), lambda b, i: (b, 0, cb + 2))
    blk = pl.BlockSpec((sq, BLOCK, B_WIDTH), lambda b, i: (b, i, 0))
    full = pl.BlockSpec((sq, t, B_WIDTH), lambda b, i: (b, 0, 0))
    carry = pl.BlockSpec((sq, 1, nq, BLOCK, SB_G), lambda b, i: (b, i, 0, 0, 0))
    return q, k, v, blk, full, carry


def _sb_heads(ref, rows):
    return jnp.concatenate([_heads(ref[b, rows, :], SB_G) for b in range(SB_SEQ)])


def _sb_unheads(a):
    return [_unheads(a[b * SB_G:(b + 1) * SB_G]) for b in range(SB_SEQ)]


def sb_fwd(proj, nseq):
    n = proj.shape[0]
    t = n // nseq
    nq = t // BLOCK
    g, sq = SB_G, SB_SEQ
    everything = pl.ds(0, BLOCK)

    def body(q_ref, k_ref, v_ref, o_ref, carry_ref):
        i = pl.program_id(1)
        tri = _sb_tri()
        qv = _sb_heads(q_ref, everything)

        def tile(j, c, fn):
            rows = pl.ds(pl.multiple_of(j * BLOCK, BLOCK), BLOCK)
            for b in range(sq):
                carry_ref[b, 0, j] = jnp.concatenate([c[b * g + h] for h in range(g)], axis=1)
            return fn(qv, _sb_heads(k_ref, rows), _sb_heads(v_ref, rows), c, tri)

        def step(it, st):
            o_acc, c = st
            o, tot = tile(i - 1 - it, c, _sb_tiles_off)
            return o_acc + o, c + tot

        o_acc, _ = lax.fori_loop(0, i, step, tile(i, jnp.zeros((sq * g, BLOCK, 1), F32), _sb_tiles_diag))
        for b, o in enumerate(_sb_unheads(o_acc)):
            o_ref[b] = o

    q, k, v, blk, _, carry = _sb_specs(t, nq)
    proj3 = proj.reshape(nseq, t, -1)
    o, carries = pl.pallas_call(
        body, name="sb_fwd", grid=(nseq // sq, nq),
        in_specs=[q, k, v],
        out_specs=[blk, carry],
        out_shape=[jax.ShapeDtypeStruct((nseq, t, B_WIDTH), F32),
                   jax.ShapeDtypeStruct((nseq, nq, nq, BLOCK, g), F32)],
        compiler_params=_cparams("parallel", "arbitrary"),
    )(proj3, proj3, proj3)
    return o.reshape(n, B_WIDTH), carries


def sb_bwd(proj, carries, do, nseq):
    n = proj.shape[0]
    t = n // nseq
    nq = t // BLOCK
    g, sq = SB_G, SB_SEQ
    everything = pl.ds(0, BLOCK)

    def body(q_ref, k_ref, v_ref, carry_ref, do_ref, dq_ref, dk_ref, dv_ref):
        i = pl.program_id(1)

        @pl.when(i == 0)
        def _():
            dk_ref[...] = jnp.zeros_like(dk_ref)
            dv_ref[...] = jnp.zeros_like(dv_ref)

        tri = _sb_tri()
        qv = _sb_heads(q_ref, everything)
        dov = _sb_heads(do_ref, everything)

        def tile(j, st, fn):
            dq_acc, dc = st
            rows = pl.ds(pl.multiple_of(j * BLOCK, BLOCK), BLOCK)
            cj = [carry_ref[b, 0, j] for b in range(sq)]
            _, vjp = jax.vjp(lambda q_, k_, v_, c_: fn(q_, k_, v_, c_, tri), qv, _sb_heads(k_ref, rows), _sb_heads(v_ref, rows),
                             jnp.stack([cj[b][:, h:h + 1] for b in range(sq) for h in range(g)]))
            dq, dk, dv, dcj = vjp((dov, dc))
            for b, (dkb, dvb) in enumerate(zip(_sb_unheads(dk), _sb_unheads(dv))):
                dk_ref[b, rows, :] += dkb
                dv_ref[b, rows, :] += dvb
            return dq_acc + dq, dc + dcj

        st = lax.fori_loop(0, i, lambda j, st: tile(j, st, _sb_tiles_off),
                           (jnp.zeros((sq * g, BLOCK, HEAD_DIM), F32), jnp.zeros((sq * g, BLOCK, 1), F32)))
        dq_acc, _ = tile(i, st, _sb_tiles_diag)
        for b, dq in enumerate(_sb_unheads(dq_acc)):
            dq_ref[b] = dq

    q, k, v, blk, full, carry = _sb_specs(t, nq)
    proj3, do3 = proj.reshape(nseq, t, -1), do.reshape(nseq, t, -1)
    grads = pl.pallas_call(
        body, name="sb_bwd", grid=(nseq // sq, nq),
        in_specs=[q, k, v, carry, blk],
        out_specs=[blk, full, full],
        out_shape=[jax.ShapeDtypeStruct((nseq, t, B_WIDTH), F32)] * 3,
        compiler_params=_cparams("parallel", "arbitrary"),
    )(proj3, proj3, proj3, carries, do3)
    return [a.reshape(n, B_WIDTH) for a in grads]


@jax.custom_vjp
def _unit_lower_inverse(a):
    n = a.shape[0]
    eye = jnp.where(lax.broadcasted_iota(jnp.int32, (n, n), 0) == lax.broadcasted_iota(jnp.int32, (n, n), 1), 1.0, 0.0)
    tmat = eye.astype(F32) - a
    p = a
    for _ in range(5):
        p = _mm32(p, p)
        tmat = tmat + _mm32(tmat, p)
    return tmat


def _unit_lower_inverse_fwd(a):
    tmat = _unit_lower_inverse(a)
    return tmat, tmat


def _unit_lower_inverse_bwd(tmat, g):
    return (-_mm32(_mm32(tmat, g, TN), tmat, NT),)


_unit_lower_inverse.defvjp(_unit_lower_inverse_fwd, _unit_lower_inverse_bwd)


def _gdn_chunk(q, k, v, al_c, al_r, br_c, alog, dtb, nw, s):
    c = GDN_CHUNK
    ri = lax.broadcasted_iota(jnp.int32, (c, c), 0)
    ci = lax.broadcasted_iota(jnp.int32, (c, c), 1)
    incl, strict = ri >= ci, ri > ci
    eye = jnp.where(ri == ci, 1.0, 0.0).astype(F32)
    rate = -jnp.exp(alog)
    g_c = rate * _softplus(al_c + dtb)
    g_r = rate * _softplus(al_r + dtb)
    beta = _sigmoid(br_c)
    gc_c = jnp.sum(jnp.where(incl, g_r, 0.0), axis=1, keepdims=True)
    gc_r = jnp.sum(jnp.where(ri <= ci, g_c, 0.0), axis=0, keepdims=True)
    gl = jnp.sum(g_r, axis=1, keepdims=True)
    decay = jnp.where(incl, jnp.exp(jnp.where(incl, gc_c - gc_r, 0.0)), 0.0)
    qn = q * lax.rsqrt(jnp.sum(q * q, axis=-1, keepdims=True) + RMS_EPS) * (HEAD_DIM ** -0.5)
    kn = k * lax.rsqrt(jnp.sum(k * k, axis=-1, keepdims=True) + RMS_EPS)
    kb = kn * beta
    a = jnp.where(strict, _mm(kb, kn, NT) * decay, 0.0)
    tmat = _unit_lower_inverse(a)
    u = _mm(tmat, v * beta)
    w = _mm(tmat, kb * jnp.exp(gc_c))
    qk = _mm(qn, kn, NT) * decay
    v_new = u - _mm(w, s)
    o = _mm(qn * jnp.exp(gc_c), s) + _mm(qk, v_new)
    s_new = s * jnp.exp(gl) + _mm(kn * jnp.exp(gl - gc_c), v_new, TN)
    o = o * lax.rsqrt(jnp.mean(o * o, axis=-1, keepdims=True) + RMS_EPS) * nw
    return o, s_new


_gdn_chunks = jax.vmap(_gdn_chunk, in_axes=(0, 0, 0, 0, 0, 0, 0, 0, None, 0))

GDN_TB = 256
GDN_SEQ_FWD = 2
GDN_SEQ_BWD = 2


def _gdn_block(q3, k3, v3, ba, alog, dtb, nw, s):
    nh = N_HEADS_A
    ns = q3.shape[0]
    bat = [ba[b].T for b in range(ns)]
    br_c = jnp.stack([ba[b][:, h:h + 1] for b in range(ns) for h in range(nh)])
    al_c = jnp.stack([ba[b][:, nh + h:nh + h + 1] for b in range(ns) for h in range(nh)])
    al_r = jnp.stack([bat[b][nh + h:nh + h + 1, :] for b in range(ns) for h in range(nh)])
    heads = lambda a: jnp.concatenate([_heads(a[b], nh) for b in range(ns)])
    o, s_new = _gdn_chunks(heads(q3), heads(k3), heads(v3), al_c, al_r, br_c,
                           jnp.concatenate([alog] * ns), jnp.concatenate([dtb] * ns), nw, s)
    return jnp.stack([_unheads(o[b * nh:(b + 1) * nh]) for b in range(ns)]), s_new


def _gdn_specs(nt, sq, rev):
    tpos = (lambda i: nt - 1 - i) if rev else (lambda i: i)
    ncb = GDN_TB // GDN_CHUNK
    qkv = [pl.BlockSpec((sq, GDN_TB, A_WIDTH), lambda b, i, j=j: (b, tpos(i), j)) for j in range(3)]
    ba = pl.BlockSpec((sq, GDN_TB, BLOCK), lambda b, i: (b, tpos(i), COL_BA // BLOCK))
    one = pl.BlockSpec((N_HEADS_A, 1, 1), lambda b, i: (0, 0, 0))
    vec = pl.BlockSpec((1, HEAD_DIM), lambda b, i: (0, 0))
    st = pl.BlockSpec((sq, ncb, N_HEADS_A, HEAD_DIM, HEAD_DIM), lambda b, i: (b, tpos(i), 0, 0, 0))
    oa = pl.BlockSpec((sq, GDN_TB, A_WIDTH), lambda b, i: (b, tpos(i), 0))
    return qkv, ba, one, vec, st, oa, tpos


def gdn_fwd(ya, proj, alog, dtb, nw, nseq):
    n = ya.shape[0]
    t = n // nseq
    nc, nt, ncb = t // GDN_CHUNK, t // GDN_TB, GDN_TB // GDN_CHUNK
    sq = GDN_SEQ_FWD
    nh = N_HEADS_A

    def body(q_ref, k_ref, v_ref, ba_ref, alog_ref, dtb_ref, nw_ref, o_ref, st_ref, s_s):
        @pl.when(pl.program_id(1) == 0)
        def _():
            s_s[...] = jnp.zeros_like(s_s)

        def step(c, s):
            rows = pl.ds(pl.multiple_of(c * GDN_CHUNK, GDN_CHUNK), GDN_CHUNK)
            for b in range(sq):
                st_ref[b, c] = s[b * nh:(b + 1) * nh]
            o, s_new = _gdn_block(q_ref[:, rows, :], k_ref[:, rows, :], v_ref[:, rows, :], ba_ref[:, rows, :],
                                  alog_ref[...], dtb_ref[...], nw_ref[...], s)
            o_ref[:, rows, :] = o
            return s_new

        s_s[...] = lax.fori_loop(0, ncb, step, s_s[...])

    qkv, ba, one, vec, st, oa, _ = _gdn_specs(nt, sq, False)
    ya3, proj3 = ya.reshape(nseq, t, -1), proj.reshape(nseq, t, -1)
    o, states = pl.pallas_call(
        body, name="gdn_fwd", grid=(nseq // sq, nt),
        in_specs=qkv + [ba, one, one, vec],
        out_specs=[oa, st],
        out_shape=[jax.ShapeDtypeStruct((nseq, t, A_WIDTH), F32),
                   jax.ShapeDtypeStruct((nseq, nc, nh, HEAD_DIM, HEAD_DIM), F32)],
        scratch_shapes=[pltpu.VMEM((sq * nh, HEAD_DIM, HEAD_DIM), F32)],
        compiler_params=_cparams("parallel", "arbitrary"),
    )(ya3, ya3, ya3, proj3, alog, dtb, nw)
    return o.reshape(n, A_WIDTH), states


def gdn_bwd(ya, proj, alog, dtb, nw, states, do, nseq):
    n = ya.shape[0]
    t = n // nseq
    nt, ncb = t // GDN_TB, GDN_TB // GDN_CHUNK
    nh = N_HEADS_A
    sq = GDN_SEQ_BWD

    def body(q_ref, k_ref, v_ref, ba_ref, alog_ref, dtb_ref, nw_ref, st_ref, do_ref,
             dya_ref, dba_ref, dalog_ref, ddtb_ref, dnw_ref, ds_s):
        @pl.when(pl.program_id(1) == 0)
        def _():
            ds_s[...] = jnp.zeros_like(ds_s)
            dalog_ref[...] = jnp.zeros_like(dalog_ref)
            ddtb_ref[...] = jnp.zeros_like(ddtb_ref)
            dnw_ref[...] = jnp.zeros_like(dnw_ref)

        def step(it, carry):
            ds, dalog, ddtb, dnw = carry
            c = ncb - 1 - it
            rows = pl.ds(pl.multiple_of(c * GDN_CHUNK, GDN_CHUNK), GDN_CHUNK)
            s_in = jnp.concatenate([st_ref[b, c] for b in range(sq)])
            _, vjp = jax.vjp(_gdn_block, q_ref[:, rows, :], k_ref[:, rows, :], v_ref[:, rows, :], ba_ref[:, rows, :],
                             alog_ref[...], dtb_ref[...], nw_ref[...], s_in)
            dq, dk, dv, dba, da, dd, dn, ds = vjp((do_ref[:, rows, :], ds))
            dya_ref[:, rows, 0:A_WIDTH] = dq
            dya_ref[:, rows, A_WIDTH:2 * A_WIDTH] = dk
            dya_ref[:, rows, 2 * A_WIDTH:3 * A_WIDTH] = dv
            dba_ref[:, rows, :] = dba
            return ds, dalog + da, ddtb + dd, dnw + dn

        z11 = jnp.zeros((nh, 1, 1), F32)
        ds, dalog, ddtb, dnw = lax.fori_loop(0, ncb, step, (ds_s[...], z11, z11, jnp.zeros((1, HEAD_DIM), F32)))
        ds_s[...] = ds
        dalog_ref[0] += dalog
        ddtb_ref[0] += ddtb
        dnw_ref[0] += dnw

    qkv, ba, one, vec, st, oa, tpos = _gdn_specs(nt, sq, True)
    per_grp = pl.BlockSpec((1, nh, 1, 1), lambda b, i: (b, 0, 0, 0))
    sd = jax.ShapeDtypeStruct
    ya3, proj3, do3 = ya.reshape(nseq, t, -1), proj.reshape(nseq, t, -1), do.reshape(nseq, t, -1)
    dya, dba, dalog, ddtb, dnw = pl.pallas_call(
        body, name="gdn_bwd", grid=(nseq // sq, nt),
        in_specs=qkv + [ba, one, one, vec, st, oa],
        out_specs=[pl.BlockSpec((sq, GDN_TB, 3 * A_WIDTH), lambda b, i: (b, tpos(i), 0)),
                   pl.BlockSpec((sq, GDN_TB, BLOCK), lambda b, i: (b, tpos(i), 0)),
                   per_grp, per_grp, pl.BlockSpec((1, 1, HEAD_DIM), lambda b, i: (b, 0, 0))],
        out_shape=[sd((nseq, t, 3 * A_WIDTH), F32), sd((nseq, t, BLOCK), F32), sd((nseq // sq, nh, 1, 1), F32),
                   sd((nseq // sq, nh, 1, 1), F32), sd((nseq // sq, 1, HEAD_DIM), F32)],
        scratch_shapes=[pltpu.VMEM((sq * nh, HEAD_DIM, HEAD_DIM), F32)],
        compiler_params=_cparams("parallel", "arbitrary"),
    )(ya3, ya3, ya3, proj3, alog, dtb, nw, states, do3)
    return dya.reshape(n, 3 * A_WIDTH), dba.reshape(n, BLOCK), dalog, ddtb, dnw


DIL_NB = tuple((SEQ // d) // BLOCK for _, d in DILATED_PAIRS)
DIL_D = tuple(d for _, d in DILATED_PAIRS)
DIL_STEPS = tuple(w // d for w, d in DILATED_PAIRS)
DIL_B = 4


def _rope_tables(t):
    half = ROPE_DIM // 2
    inv_freq = ROPE_THETA ** (-jnp.arange(half, dtype=F32) / half)
    ang = jnp.arange(t, dtype=F32)[:, None] * inv_freq[None, :]
    ones = jnp.ones((t, HEAD_DIM - ROPE_DIM), F32)
    cs = jnp.concatenate([jnp.cos(ang), jnp.cos(ang), ones], axis=1)
    sn = jnp.concatenate([jnp.sin(ang), jnp.sin(ang), 0.0 * ones], axis=1)
    i = jnp.arange(HEAD_DIM)[:, None]
    j = jnp.arange(HEAD_DIM)[None, :]
    pm = (jnp.where((j < half) & (i == j + half), -1.0, 0.0)
          + jnp.where((j >= half) & (j < ROPE_DIM) & (i == j - half), 1.0, 0.0))
    return cs, sn, jnp.concatenate([pm, pm]).astype(BF16)


def _dil_prep(x, w, cs, sn, pm):
    y = x * lax.rsqrt(jnp.mean(x * x, axis=-1, keepdims=True) + RMS_EPS) * w
    hi = y.astype(BF16)
    lo = lax.stop_gradient(y - hi.astype(F32)).astype(BF16)
    return y * cs + lax.dot_general(jnp.concatenate([hi, lo], axis=1), pm, NN, preferred_element_type=F32) * sn


def _dil_tile(qn, kk, vv, bias):
    s = _mm(qn * (HEAD_DIM ** -0.5), kk, NT) + bias
    m = lax.stop_gradient(jnp.max(s, axis=-1, keepdims=True))
    p = jnp.exp(s - m)
    denom = jnp.sum(p, axis=-1, keepdims=True)
    return _mm(p, vv) / denom, m + jnp.log(denom)


_dil_tiles = jax.vmap(_dil_tile)


def _dil_mix(o1, o2, o3, l1, l2, l3):
    m = lax.stop_gradient(jnp.maximum(jnp.maximum(l1, l2), l3))
    e1, e2, e3 = jnp.exp(l1 - m), jnp.exp(l2 - m), jnp.exp(l3 - m)
    return (e1 * o1 + e2 * o2 + e3 * o3) / (e1 + e2 + e3)


def _dil_fill_biases(bias_s):
    steps, = set(DIL_STEPS)
    qi = lax.broadcasted_iota(jnp.int32, (BLOCK, 1), 0)
    kj = lax.broadcasted_iota(jnp.int32, (1, 2 * BLOCK), 1)
    rel = qi - kj + BLOCK
    inside = (rel >= 0) & (rel <= steps)
    bias_s[0] = jnp.where(inside, 0.0, NEG)
    bias_s[1] = jnp.where(inside & (kj >= BLOCK), 0.0, NEG)
    bias_s[2] = jnp.where((qi >= kj) & (qi - kj <= steps), 0.0, NEG)


def _dil_mask(it, g, bias_s):
    qrows = pl.ds(pl.multiple_of(it * BLOCK, BLOCK), BLOCK)
    if DIL_NB[g] == 1:
        return bias_s[2, :, 0:BLOCK], qrows, qrows
    which = jnp.where(it == 0, 2, jnp.where(it % DIL_NB[g] == 0, 1, 0))
    kstart = jnp.maximum(it - 1, 0) * BLOCK
    return bias_s[which], pl.ds(pl.multiple_of(it * BLOCK, BLOCK), BLOCK), pl.ds(pl.multiple_of(kstart, BLOCK), 2 * BLOCK)


def _dil_gather(src, dst, d):
    t = src.shape[0]
    ln = t // d
    for r in range(d):
        dst[pl.ds(r * ln, ln), :] = src[pl.ds(r, ln, stride=d), :]


def _dil_scatter(src, dst, d):
    t = src.shape[0]
    ln = t // d
    for r in range(d):
        dst[pl.ds(r, ln, stride=d), :] = src[pl.ds(r * ln, ln), :]


def _half(x, h):
    return jnp.where(h == 0, x[:, :HEAD_DIM], x[:, HEAD_DIM:])


def _store_half(ref, rows, val, h):
    @pl.when(h == 0)
    def _():
        ref[rows, 0:HEAD_DIM] = val

    @pl.when(h == 1)
    def _():
        ref[rows, HEAD_DIM:2 * HEAD_DIM] = val


def _dil_forward_parts(h, q_ref, k_ref, v_ref, qw, kw, cs_ref, sn_ref, pm, qn_s, kn_s, v_s, dl_s, od_s, ld_s, on_s, ln_s,
                       bias_s):
    t = qn_s.shape[0]
    _dil_fill_biases(bias_s)

    def prep(c, _):
        rows = pl.ds(pl.multiple_of(c * ROWS, ROWS), ROWS)
        qn_s[rows, :] = _dil_prep(_half(q_ref[rows, :], h), qw, cs_ref[rows, :], sn_ref[rows, :], pm)
        kn_s[rows, :] = _dil_prep(_half(k_ref[rows, :], h), kw, cs_ref[rows, :], sn_ref[rows, :], pm)
        v_s[rows, :] = _half(v_ref[rows, :], h)
        return 0

    lax.fori_loop(0, t // ROWS, prep, 0)
    for g in (1, 2):
        _dil_gather(qn_s, dl_s.at[g - 1, 0], DIL_D[g])
        _dil_gather(kn_s, dl_s.at[g - 1, 1], DIL_D[g])
        _dil_gather(v_s, dl_s.at[g - 1, 2], DIL_D[g])
    for g in range(3):
        qs = qn_s if g == 0 else dl_s.at[g - 1, 0]
        ks = kn_s if g == 0 else dl_s.at[g - 1, 1]
        vs = v_s if g == 0 else dl_s.at[g - 1, 2]

        def tiles(i, _, g=g, qs=qs, ks=ks, vs=vs):
            where = [_dil_mask(i * DIL_B + b, g, bias_s) for b in range(DIL_B)]
            o, lse = _dil_tiles(jnp.stack([qs[qr, :] for _, qr, _ in where]), jnp.stack([ks[kr, :] for _, _, kr in where]),
                                jnp.stack([vs[kr, :] for _, _, kr in where]), jnp.stack([m for m, _, _ in where]))
            for b, (_, qr, _) in enumerate(where):
                od_s[g, qr, :] = o[b]
                ld_s[g, qr, :] = lse[b]
            return 0

        lax.fori_loop(0, t // BLOCK // DIL_B, tiles, 0)
    for g in (1, 2):
        _dil_scatter(od_s.at[g], on_s.at[g - 1], DIL_D[g])
        _dil_scatter(ld_s.at[g], ln_s.at[g - 1], DIL_D[g])


def _dil_scratch(t):
    return [pltpu.VMEM((t, HEAD_DIM), F32), pltpu.VMEM((t, HEAD_DIM), F32), pltpu.VMEM((t, HEAD_DIM), F32),
            pltpu.VMEM((2, 3, t, HEAD_DIM), F32),
            pltpu.VMEM((3, t, HEAD_DIM), F32), pltpu.VMEM((3, t, 1), F32),
            pltpu.VMEM((2, t, HEAD_DIM), F32), pltpu.VMEM((2, t, 1), F32),
            pltpu.VMEM((3, BLOCK, 2 * BLOCK), F32)]


def _dil_specs(t):
    cb = COL_C // BLOCK
    per = C_WIDTH // BLOCK
    qkv = [pl.BlockSpec((t, BLOCK), lambda b, p, h, j=j: (b, cb + j * per + p)) for j in range(3)]
    vec = pl.BlockSpec((1, HEAD_DIM), lambda b, p, h: (0, 0))
    tab = pl.BlockSpec((t, HEAD_DIM), lambda b, p, h: (0, 0))
    mat = pl.BlockSpec((2 * HEAD_DIM, HEAD_DIM), lambda b, p, h: (0, 0))
    pair = pl.BlockSpec((t, BLOCK), lambda b, p, h: (b, p))
    return qkv, vec, tab, mat, pair


def dil_fwd(proj, qw, kw, cs, sn, pm, nseq):
    n = proj.shape[0]
    t = n // nseq

    def body(q_ref, k_ref, v_ref, qw_ref, kw_ref, cs_ref, sn_ref, pm_ref, o_ref, qn_s, kn_s, v_s, dl_s, od_s, ld_s, on_s, ln_s,
             bias_s):
        h = pl.program_id(2)
        _dil_forward_parts(h, q_ref, k_ref, v_ref, qw_ref[...], kw_ref[...], cs_ref, sn_ref, pm_ref[...],
                           qn_s, kn_s, v_s, dl_s, od_s, ld_s, on_s, ln_s, bias_s)

        def mix(c, _):
            rows = pl.ds(pl.multiple_of(c * ROWS, ROWS), ROWS)
            _store_half(o_ref, rows, _dil_mix(od_s[0, rows, :], on_s[0, rows, :], on_s[1, rows, :],
                                              ld_s[0, rows, :], ln_s[0, rows, :], ln_s[1, rows, :]), h)
            return 0

        lax.fori_loop(0, t // ROWS, mix, 0)

    qkv, vec, tab, mat, pair = _dil_specs(t)
    return pl.pallas_call(
        body, name="dil_fwd", grid=(nseq, C_WIDTH // BLOCK, 2),
        in_specs=qkv + [vec, vec, tab, tab, mat],
        out_specs=pair,
        out_shape=jax.ShapeDtypeStruct((n, C_WIDTH), F32),
        scratch_shapes=_dil_scratch(t),
        compiler_params=_cparams("parallel", "parallel", "arbitrary"),
    )(proj, proj, proj, qw, kw, cs, sn, pm)


def dil_bwd(proj, qw, kw, cs, sn, pm, do, nseq):
    n = proj.shape[0]
    t = n // nseq
    dh = HEAD_DIM

    def body(q_ref, k_ref, v_ref, qw_ref, kw_ref, cs_ref, sn_ref, pm_ref, do_ref,
             dq_ref, dk_ref, dv_ref, dqw_ref, dkw_ref,
             qn_s, kn_s, v_s, dl_s, od_s, ld_s, on_s, ln_s, bias_s, tq_s, tk_s, tv_s, dv_s):
        h = pl.program_id(2)
        qw, kw, pm = qw_ref[...], kw_ref[...], pm_ref[...]
        _dil_forward_parts(h, q_ref, k_ref, v_ref, qw, kw, cs_ref, sn_ref, pm, qn_s, kn_s, v_s, dl_s, od_s, ld_s, on_s, ln_s,
                           bias_s)

        def mix(c, _):
            rows = pl.ds(pl.multiple_of(c * ROWS, ROWS), ROWS)
            _, vjp = jax.vjp(_dil_mix, od_s[0, rows, :], on_s[0, rows, :], on_s[1, rows, :],
                             ld_s[0, rows, :], ln_s[0, rows, :], ln_s[1, rows, :])
            d1, d2, d3, e1, e2, e3 = vjp(_half(do_ref[rows, :], h))
            od_s[0, rows, :] = d1
            on_s[0, rows, :] = d2
            on_s[1, rows, :] = d3
            ld_s[0, rows, :] = e1
            ln_s[0, rows, :] = e2
            ln_s[1, rows, :] = e3
            return 0

        lax.fori_loop(0, t // ROWS, mix, 0)
        for g in (1, 2):
            _dil_gather(on_s.at[g - 1], od_s.at[g], DIL_D[g])
            _dil_gather(ln_s.at[g - 1], ld_s.at[g], DIL_D[g])
        on_s[...] = jnp.zeros_like(on_s)
        dv_s[...] = jnp.zeros_like(dv_s)
        for g in range(3):
            qs = qn_s if g == 0 else dl_s.at[g - 1, 0]
            ks = kn_s if g == 0 else dl_s.at[g - 1, 1]
            vs = v_s if g == 0 else dl_s.at[g - 1, 2]
            gq = on_s.at[0] if g == 0 else tq_s
            gk = on_s.at[1] if g == 0 else tk_s
            gv = dv_s if g == 0 else tv_s
            if g > 0:
                tk_s[...] = jnp.zeros_like(tk_s)
                tv_s[...] = jnp.zeros_like(tv_s)

            def tiles(i, _, g=g, qs=qs, ks=ks, vs=vs, gq=gq, gk=gk, gv=gv):
                where = [_dil_mask(i * DIL_B + b, g, bias_s) for b in range(DIL_B)]
                biases = jnp.stack([m for m, _, _ in where])
                _, vjp = jax.vjp(lambda q_, k_, v_: _dil_tiles(q_, k_, v_, biases),
                                 jnp.stack([qs[qr, :] for _, qr, _ in where]), jnp.stack([ks[kr, :] for _, _, kr in where]),
                                 jnp.stack([vs[kr, :] for _, _, kr in where]))
                dq, dkk, dvv = vjp((jnp.stack([od_s[g, qr, :] for _, qr, _ in where]),
                                    jnp.stack([ld_s[g, qr, :] for _, qr, _ in where])))
                for b, (_, qr, kr) in enumerate(where):
                    gq[qr, :] = dq[b]
                    gk[kr, :] += dkk[b]
                    gv[kr, :] += dvv[b]
                return 0

            lax.fori_loop(0, t // BLOCK // DIL_B, tiles, 0)
            if g > 0:
                d = DIL_D[g]
                ln = t // d
                for r in range(d):
                    nat, dil = pl.ds(r, ln, stride=d), pl.ds(r * ln, ln)
                    on_s[0, nat, :] += tq_s[dil, :]
                    on_s[1, nat, :] += tk_s[dil, :]
                    dv_s[nat, :] += tv_s[dil, :]

        def prep(c, acc):
            rows = pl.ds(pl.multiple_of(c * ROWS, ROWS), ROWS)
            f = lambda x, w: _dil_prep(x, w, cs_ref[rows, :], sn_ref[rows, :], pm)
            _, vq = jax.vjp(f, _half(q_ref[rows, :], h), qw)
            _, vk = jax.vjp(f, _half(k_ref[rows, :], h), kw)
            dq, dqw = vq(on_s[0, rows, :])
            dk, dkw = vk(on_s[1, rows, :])
            _store_half(dq_ref, rows, dq, h)
            _store_half(dk_ref, rows, dk, h)
            _store_half(dv_ref, rows, dv_s[rows, :], h)
            return acc[0] + dqw, acc[1] + dkw

        dqw, dkw = lax.fori_loop(0, t // ROWS, prep, (jnp.zeros((1, dh), F32), jnp.zeros((1, dh), F32)))
        dqw_ref[0] = dqw
        dkw_ref[0] = dkw

    qkv, vec, tab, mat, pair = _dil_specs(t)
    per = C_WIDTH // BLOCK
    wout = pl.BlockSpec((1, 1, dh), lambda b, p, h: ((b * per + p) * 2 + h, 0, 0))
    return pl.pallas_call(
        body, name="dil_bwd", grid=(nseq, per, 2),
        in_specs=qkv + [vec, vec, tab, tab, mat, pair],
        out_specs=[pair, pair, pair, wout, wout],
        out_shape=[jax.ShapeDtypeStruct((n, C_WIDTH), F32)] * 3 + [jax.ShapeDtypeStruct((nseq * N_HEADS_C, 1, dh), F32)] * 2,
        scratch_shapes=_dil_scratch(t) + [pltpu.VMEM((t, dh), F32)] * 4,
        compiler_params=_cparams("parallel", "parallel", "arbitrary"),
    )(proj, proj, proj, qw, kw, cs, sn, pm, do)


N_CHIPS = 4
SUM_ROWS = 432
MESH_IDS = pl.DeviceIdType.MESH
ANY = pl.BlockSpec(memory_space=pl.ANY)


def plane_exchange(src, all_to_all):
    blk_shape = src.shape[1:] if all_to_all else src.shape

    def body(src_ref, out_ref, send_sems, recv_sems, local_sem):
        x, y, c = lax.axis_index("x"), lax.axis_index("y"), lax.axis_index("c")
        me = 2 * x + y
        mine = pltpu.make_async_copy(src_ref.at[me] if all_to_all else src_ref, out_ref.at[me], local_sem)
        mine.start()
        sends = []
        for k in (1, 2, 3):
            px = 1 - x if k & 2 else x
            py = 1 - y if k & 1 else y
            peer = 2 * px + py
            cp = pltpu.make_async_remote_copy(
                src_ref=src_ref.at[peer] if all_to_all else src_ref, dst_ref=out_ref.at[me],
                send_sem=send_sems.at[k - 1], recv_sem=recv_sems.at[k - 1],
                device_id=(px, py, c), device_id_type=MESH_IDS)
            cp.start()
            sends.append((cp, peer, (px, py, c)))
        for k, (cp, peer, dev) in enumerate(sends):
            pltpu.make_async_remote_copy(
                src_ref=out_ref.at[me], dst_ref=out_ref.at[peer],
                send_sem=send_sems.at[k], recv_sem=recv_sems.at[k],
                device_id=dev, device_id_type=MESH_IDS).wait_recv()
        for cp, _, _ in sends:
            cp.wait_send()
        mine.wait()

    return pl.pallas_call(
        body, name="plane_all_to_all" if all_to_all else "plane_all_gather",
        in_specs=[ANY], out_specs=ANY,
        out_shape=jax.ShapeDtypeStruct((N_CHIPS,) + blk_shape, src.dtype),
        scratch_shapes=[pltpu.SemaphoreType.DMA((3,)), pltpu.SemaphoreType.DMA((3,)), pltpu.SemaphoreType.DMA],
    )(src)


def sibling_swap(src):
    def body(src_ref, out_ref, send_sem, recv_sem):
        x, y, c = lax.axis_index("x"), lax.axis_index("y"), lax.axis_index("c")
        cp = pltpu.make_async_remote_copy(src_ref=src_ref, dst_ref=out_ref, send_sem=send_sem, recv_sem=recv_sem,
                                          device_id=(x, y, 1 - c), device_id_type=MESH_IDS)
        cp.start()
        cp.wait()

    return pl.pallas_call(
        body, name="sibling_swap", in_specs=[ANY], out_specs=ANY,
        out_shape=jax.ShapeDtypeStruct(src.shape, src.dtype),
        scratch_shapes=[pltpu.SemaphoreType.DMA, pltpu.SemaphoreType.DMA],
    )(src)


def sum4(a):
    _, r, c = a.shape
    tr = SUM_ROWS

    def body(a_ref, o_ref):
        p = [a_ref[i].astype(F32) for i in range(N_CHIPS)]
        o_ref[...] = (p[0] + p[1]) + (p[2] + p[3])

    return pl.pallas_call(
        body, name="sum4", grid=(r // tr,),
        in_specs=[pl.BlockSpec((N_CHIPS, tr, c), lambda i: (0, i, 0))],
        out_specs=pl.BlockSpec((tr, c), lambda i: (i, 0)),
        out_shape=jax.ShapeDtypeStruct((r, c), F32),
        compiler_params=_cparams("parallel"),
    )(a)


def add2(a, b, dtype):
    r, c = a.shape
    tr = SUM_ROWS

    def body(a_ref, b_ref, o_ref):
        o_ref[...] = (a_ref[...] + b_ref[...]).astype(dtype)

    blk = pl.BlockSpec((tr, c), lambda i: (i, 0))
    return pl.pallas_call(
        body, name="add2", grid=(r // tr,), in_specs=[blk, blk], out_specs=blk,
        out_shape=jax.ShapeDtypeStruct((r, c), dtype), compiler_params=_cparams("parallel"),
    )(a, b)


PACK_COLS = 1152
PACK_ROWS = 2592
ROW_TILE = 16


def _pack(parts):
    blocks = []
    for p in parts:
        p2 = p.reshape(-1, p.shape[-1])
        blocks.append(jnp.pad(p2, ((0, -p2.shape[0] % ROW_TILE), (0, PACK_COLS - p2.shape[1]))))
    rows = sum(b.shape[0] for b in blocks)
    blocks.append(jnp.zeros((PACK_ROWS - rows, PACK_COLS), blocks[0].dtype))
    return jnp.concatenate(blocks)


def _unpack(buf, shapes):
    out, at = [], 0
    for s in shapes:
        rows = math.prod(s[:-1])
        out.append(buf[at:at + rows, :s[-1]].reshape(s))
        at += rows + (-rows % ROW_TILE)
    return out


def _pack_small(g):
    blk = jnp.zeros((ROW_TILE, PACK_COLS), F32)
    for i, k in enumerate(SMALL):
        blk = blk.at[2 * i:2 * i + 2, :g[k].shape[1]].set(g[k])
    return blk


def _unpack_small(blk, shapes):
    return [blk[2 * i:2 * i + 2, :s[1]] for i, s in enumerate(shapes)]


def _layer_fwd(x, p, nseq, tabs):
    proj, hdn = inproj_fwd(x, p["norm_w"][None], p["w_in"])
    ya = conv_fwd(proj, p["conv_w"], nseq)
    oa, states = gdn_fwd(ya, proj, p["a_log"].reshape(N_HEADS_A, 1, 1), p["dt_bias"].reshape(N_HEADS_A, 1, 1),
                         p["gdn_norm_w"][None], nseq)
    ob, carries = sb_fwd(proj, nseq)
    oc = dil_fwd(proj, p["q_norm_w"][None], p["k_norm_w"][None], *tabs, nseq)
    y, mixed = outproj_fwd(x, oa, ob, oc, proj, p["w_out"])
    return y, dict(x=x, hdn=hdn, proj=proj, ya=ya, states=states, carries=carries, oa=oa, ob=ob, oc=oc, mixed=mixed)


def _layer_bwd(dy, p, res, nseq, tabs):
    proj = res["proj"]
    g = {}
    g["w_out"] = mat_tn(res["mixed"], [dy])[0]
    doa, dob, doc, dza, dzb, dzc = outproj_bwd(dy, res["oa"], res["ob"], res["oc"], proj, p["w_out"])
    dqc, dkc, dvc, dqw, dkw = dil_bwd(proj, p["q_norm_w"][None], p["k_norm_w"][None], *tabs, doc, nseq)
    g["q_norm_w"], g["k_norm_w"] = dqw.sum((0, 1)), dkw.sum((0, 1))
    dqb, dkb, dvb = sb_bwd(proj, res["carries"], dob, nseq)
    dya, dba, dalog, ddtb, dnw = gdn_bwd(res["ya"], proj, p["a_log"].reshape(N_HEADS_A, 1, 1),
                                         p["dt_bias"].reshape(N_HEADS_A, 1, 1), p["gdn_norm_w"][None], res["states"], doa, nseq)
    g["a_log"], g["dt_bias"], g["gdn_norm_w"] = dalog.sum(0).reshape(-1), ddtb.sum(0).reshape(-1), dnw.sum((0, 1))
    dqkv, dcw = conv_bwd(proj, p["conv_w"], dya, nseq)
    g["conv_w"] = dcw.sum(0)
    slabs = [dqkv, dza, dqc, dkc, dvc, dzc, dqb, dkb, dvb, dzb, dba]
    hdn = res["hdn"]
    g["w_in"] = jnp.concatenate(mat_tn(hdn, slabs[:2]) + mat_tn(hdn, slabs[2:6]) + mat_tn(hdn, slabs[6:]), axis=1)
    dx, dnw_tiles = inproj_bwd(slabs, p["w_in"], res["x"], p["norm_w"][None], dy)
    g["norm_w"] = dnw_tiles.sum((0, 1))
    return dx, g


SMALL = ("norm_w", "a_log", "dt_bias", "gdn_norm_w", "q_norm_w", "k_norm_w")


def _local_step(x, target, full):
    nseq, t, d = x.shape
    tabs = _rope_tables(t)
    h = x.reshape(nseq * t, d)
    saved = []
    for l in range(DEPTH):
        p = {k: v[l] for k, v in full.items()}
        h, res = _layer_fwd(h, p, nseq, tabs)
        saved.append((p, res))
    dy, parts = loss_fwd_bwd(h, target.reshape(nseq * t, d))
    loss = parts[:, 0, 0].sum()
    grads = [None] * DEPTH
    for l in reversed(range(DEPTH)):
        p, res = saved[l]
        dy, grads[l] = _layer_bwd(dy, p, res, nseq, tabs)
    return loss, dy.reshape(nseq, t, d), {k: jnp.stack([g[k] for g in grads]) for k in grads[0]}


def _pad_cols(w):
    b0 = ORIG_A + ORIG_BA
    c0 = b0 + ORIG_B
    zeros = jnp.zeros(w.shape[:-1] + (BLOCK - ORIG_BA,), w.dtype)
    return jnp.concatenate([w[..., :ORIG_A], w[..., c0:], w[..., b0:c0], w[..., ORIG_A:b0], zeros], axis=-1)


def _unpad_cols(w):
    return jnp.concatenate([w[..., :COL_C], w[..., COL_BA:COL_BA + ORIG_BA], w[..., COL_B:COL_BA], w[..., COL_C:COL_B]],
                           axis=-1)


def kernel(x, norm_w, w_in, conv_w, a_log, dt_bias, gdn_norm_w, q_norm_w, k_norm_w, w_out, loss_target, m_norm_w, m_w_in, m_conv_w, m_a_log, m_dt_bias, m_gdn_norm_w, m_q_norm_w, m_k_norm_w, m_w_out, v_norm_w, v_w_in, v_conv_w, v_a_log, v_dt_bias, v_gdn_norm_w, v_q_norm_w, v_k_norm_w, v_w_out):
    weights = dict(norm_w=norm_w, w_in=w_in, conv_w=conv_w, a_log=a_log, dt_bias=dt_bias, gdn_norm_w=gdn_norm_w,
                   q_norm_w=q_norm_w, k_norm_w=k_norm_w, w_out=w_out)
    moms = dict(norm_w=m_norm_w, w_in=m_w_in, conv_w=m_conv_w, a_log=m_a_log, dt_bias=m_dt_bias,
                gdn_norm_w=m_gdn_norm_w, q_norm_w=m_q_norm_w, k_norm_w=m_k_norm_w, w_out=m_w_out)
    vars_ = dict(norm_w=v_norm_w, w_in=v_w_in, conv_w=v_conv_w, a_log=v_a_log, dt_bias=v_dt_bias,
                 gdn_norm_w=v_gdn_norm_w, q_norm_w=v_q_norm_w, k_norm_w=v_k_norm_w, w_out=v_w_out)
    names = list(weights)
    sharded = ("w_in", "w_out", "conv_w")
    shard_shapes = [weights[k].shape for k in sharded]

    c = lax.axis_index("c")
    half = PACK_ROWS // 2
    conv_bits = lax.bitcast_convert_type(conv_w, BF16).reshape(conv_w.shape[:2] + (2 * conv_w.shape[2],))
    shard = _pack([w_in.astype(BF16), w_out.astype(BF16), conv_bits])
    mine = plane_exchange(lax.dynamic_slice_in_dim(shard, c * half, half, axis=0), all_to_all=False)
    other = sibling_swap(mine)
    got = jnp.concatenate([jnp.where(c == 0, mine, other), jnp.where(c == 0, other, mine)], axis=1)
    per_chip = [_unpack(got[i], shard_shapes[:2] + [conv_bits.shape]) for i in range(N_CHIPS)]
    full = {k: weights[k] for k in SMALL}
    full["w_in"] = _pad_cols(jnp.concatenate([pc[0] for pc in per_chip], axis=2))
    full["w_out"] = jnp.concatenate([pc[1] for pc in per_chip], axis=1)
    full["conv_w"] = jnp.concatenate(
        [lax.bitcast_convert_type(pc[2].reshape(conv_w.shape + (2,)), F32) for pc in per_chip], axis=2)

    loss, grad_x, g = _local_step(x, loss_target, full)

    gw_in = _unpad_cols(g["w_in"])
    cols, rows = w_in.shape[2], w_out.shape[1]
    small = _pack_small(g)
    send = jnp.stack([_pack([gw_in[:, :, i * cols:(i + 1) * cols], g["w_out"][:, i * rows:(i + 1) * rows],
                             g["conv_w"][:, :, i * conv_w.shape[2]:(i + 1) * conv_w.shape[2]], small])
                      for i in range(N_CHIPS)])
    keep = lax.dynamic_slice_in_dim(send, c * half, half, axis=1)
    give = lax.dynamic_slice_in_dim(send, (1 - c) * half, half, axis=1)
    chip_sum = add2(keep.reshape(N_CHIPS * half, PACK_COLS), sibling_swap(give).reshape(N_CHIPS * half, PACK_COLS), BF16)
    mine = sum4(plane_exchange(chip_sum.reshape(N_CHIPS, half, PACK_COLS), all_to_all=True))
    other = sibling_swap(mine)
    total = jnp.concatenate([jnp.where(c == 0, mine, other), jnp.where(c == 0, other, mine)])
    reduced = _unpack(total, shard_shapes + [(ROW_TILE, PACK_COLS)])
    grads = dict(zip(sharded, reduced[:3]))
    grads.update(zip(SMALL, _unpack_small(reduced[3], [weights[k].shape for k in SMALL])))
    loss = lax.psum(loss, ("x", "y", "c"))

    def two_d(a):
        return a.reshape(-1, a.shape[-1])

    delta, new_m, new_v = {}, {}, {}
    for k in names:
        d_, m_, v_ = adamw(two_d(weights[k]), two_d(grads[k]), two_d(moms[k]), two_d(vars_[k]))
        delta[k], new_m[k], new_v[k] = (a.reshape(weights[k].shape) for a in (d_, m_, v_))
    return (loss, grad_x, *[grads[k] for k in names], *[delta[k] for k in names],
            *[new_m[k] for k in names], *[new_v[k] for k in names])
```

```python
import functools
import math

import jax
import jax.numpy as jnp
from jax import lax
from jax.experimental import pallas as pl
from jax.experimental.pallas import tpu as pltpu

F32 = jnp.float32
BF16 = jnp.bfloat16

D_MODEL = 1024
SEQ = 2048
DEPTH = 2
HEAD_DIM = 64
N_HEADS_A, N_HEADS_B, N_HEADS_C = 6, 4, 6
A_WIDTH, B_WIDTH, C_WIDTH = N_HEADS_A * HEAD_DIM, N_HEADS_B * HEAD_DIM, N_HEADS_C * HEAD_DIM
CONV_WIDTH = 4
GDN_CHUNK = 64
BLOCK = 128
ROPE_DIM = 16
ROPE_THETA = 500000.0
DILATED_PAIRS = ((128, 1), (512, 4), (2048, 16))
RMS_EPS = 1e-6
NEG = -1e30

NT = (((1,), (1,)), ((), ()))
NN = (((1,), (0,)), ((), ()))
TN = (((0,), (0,)), ((), ()))

VMEM_LIMIT = 48 * 1024 * 1024

ORIG_A = 4 * A_WIDTH
ORIG_BA = 2 * N_HEADS_A
ORIG_B = 4 * B_WIDTH
COL_AZ = 3 * A_WIDTH
COL_C = 4 * A_WIDTH
COL_B = COL_C + 4 * C_WIDTH
COL_BA = COL_B + 4 * B_WIDTH
P_COLS = COL_BA + BLOCK
TN_COLS = 384
INPROJ_COLS = P_COLS // 3
TM_ROWS = 512
ROWS = 256


def _mm(a, b, dims=NN):
    return lax.dot_general(a.astype(BF16), b.astype(BF16), dims, preferred_element_type=F32)


def _mm32(a, b, dims=NN):
    return lax.dot_general(a, b, dims, precision=lax.Precision.HIGH, preferred_element_type=F32)


def _cparams(*sem):
    return pltpu.CompilerParams(dimension_semantics=sem, vmem_limit_bytes=VMEM_LIMIT)


def _sigmoid(x):
    return 0.5 * (jnp.tanh(0.5 * x) + 1.0)


def _softplus(x):
    return jnp.maximum(x, 0.0) + jnp.log(1.0 + jnp.exp(-jnp.abs(x)))


def _rms(x, w):
    return x * lax.rsqrt(jnp.mean(x * x, axis=-1, keepdims=True) + RMS_EPS) * w


def _heads(a, n):
    return jnp.stack([a[:, h * HEAD_DIM:(h + 1) * HEAD_DIM] for h in range(n)])


def _unheads(a):
    return jnp.concatenate([a[h] for h in range(a.shape[0])], axis=1)


def _row_chunks(t):
    return [pl.ds(c * ROWS, ROWS) for c in range(t // ROWS)]


def inproj_fwd(x, nw, w):
    n, d = x.shape
    p = w.shape[1]

    def body(x_ref, nw_ref, w_ref, proj_ref, hdn_ref):
        @pl.when(pl.program_id(1) == 0)
        def _():
            hdn_ref[...] = _rms(x_ref[...], nw_ref[...]).astype(BF16)

        proj_ref[...] = jnp.dot(hdn_ref[...], w_ref[...], preferred_element_type=F32)

    return pl.pallas_call(
        body, name="inproj_fwd", grid=(n // TM_ROWS, p // INPROJ_COLS),
        in_specs=[pl.BlockSpec((TM_ROWS, d), lambda i, j: (i, 0)), pl.BlockSpec((1, d), lambda i, j: (0, 0)),
                  pl.BlockSpec((d, INPROJ_COLS), lambda i, j: (0, j))],
        out_specs=[pl.BlockSpec((TM_ROWS, INPROJ_COLS), lambda i, j: (i, j)), pl.BlockSpec((TM_ROWS, d), lambda i, j: (i, 0))],
        out_shape=[jax.ShapeDtypeStruct((n, p), F32), jax.ShapeDtypeStruct((n, d), BF16)],
        compiler_params=_cparams("parallel", "arbitrary"),
    )(x, nw, w)


def mat_tn(a, slabs):
    n, ka = a.shape
    ns = len(slabs)

    def body(*refs):
        a_ref, s_refs, o_refs = refs[0], refs[1:1 + ns], refs[1 + ns:]

        @pl.when(pl.program_id(0) == 0)
        def _():
            for o_ref in o_refs:
                o_ref[...] = jnp.zeros_like(o_ref)

        av = a_ref[...]
        for s_ref, o_ref in zip(s_refs, o_refs):
            o_ref[...] += lax.dot_general(av, s_ref[...].astype(BF16), TN, preferred_element_type=F32)

    return pl.pallas_call(
        body, name="mat_tn", grid=(n // TM_ROWS,),
        in_specs=[pl.BlockSpec((TM_ROWS, ka), lambda k: (k, 0))]
                 + [pl.BlockSpec((TM_ROWS, s.shape[1]), lambda k: (k, 0)) for s in slabs],
        out_specs=[pl.BlockSpec((ka, s.shape[1]), lambda k: (0, 0)) for s in slabs],
        out_shape=[jax.ShapeDtypeStruct((ka, s.shape[1]), F32) for s in slabs],
        compiler_params=_cparams("arbitrary"),
    )(a, *slabs)


def inproj_bwd(slabs, w, x, nw, dy):
    n, d = x.shape
    p = w.shape[1]
    tm = 256
    ns = len(slabs)

    def body(*refs):
        s_refs = refs[:ns]
        w_ref, x_ref, nw_ref, dy_ref, dx_ref, dnw_ref = refs[ns:]
        dh = jnp.zeros((tm, d), F32)
        at = 0
        for s_ref in s_refs:
            wd = s_ref.shape[1]
            dh = dh + lax.dot_general(s_ref[...].astype(BF16), w_ref[:, at:at + wd], NT, preferred_element_type=F32)
            at += wd
        _, vjp = jax.vjp(_rms, x_ref[...], nw_ref[...])
        dx, dnw = vjp(dh)
        dx_ref[...] = dx + dy_ref[...]
        dnw_ref[0] = dnw

    return pl.pallas_call(
        body, name="inproj_bwd", grid=(n // tm,),
        in_specs=[pl.BlockSpec((tm, s.shape[1]), lambda i: (i, 0)) for s in slabs]
                 + [pl.BlockSpec((d, p), lambda i: (0, 0)), pl.BlockSpec((tm, d), lambda i: (i, 0)),
                    pl.BlockSpec((1, d), lambda i: (0, 0)), pl.BlockSpec((tm, d), lambda i: (i, 0))],
        out_specs=[pl.BlockSpec((tm, d), lambda i: (i, 0)), pl.BlockSpec((1, 1, d), lambda i: (i, 0, 0))],
        out_shape=[jax.ShapeDtypeStruct((n, d), F32), jax.ShapeDtypeStruct((n // tm, 1, d), F32)],
        compiler_params=_cparams("parallel"),
    )(*slabs, w, x, nw, dy)


CONV_PAD = 8
CONV_ROWS = 256


def _conv_pre(pad_s, cw, c):
    xs = [pad_s[pl.ds(c * CONV_ROWS + CONV_PAD - (CONV_WIDTH - 1) + k, CONV_ROWS), :] for k in range(CONV_WIDTH)]
    pre = xs[0] * cw[0:1, :]
    for k in range(1, CONV_WIDTH):
        pre = pre + xs[k] * cw[k:k + 1, :]
    return pre, xs


def conv_fwd(proj, cw, nseq):
    n = proj.shape[0]
    t = n // nseq
    ch = cw.shape[1]

    def body(x_ref, cw_ref, y_ref, pad_s):
        pad_s[pl.ds(0, CONV_PAD), :] = jnp.zeros((CONV_PAD, TN_COLS), F32)
        pad_s[pl.ds(CONV_PAD, t), :] = x_ref[...]
        cwv = cw_ref[...]
        for c in range(t // CONV_ROWS):
            pre, _ = _conv_pre(pad_s, cwv, c)
            y_ref[pl.ds(c * CONV_ROWS, CONV_ROWS), :] = pre * _sigmoid(pre)

    return pl.pallas_call(
        body, name="conv_fwd", grid=(nseq, ch // TN_COLS),
        in_specs=[pl.BlockSpec((t, TN_COLS), lambda b, j: (b, j)), pl.BlockSpec((CONV_WIDTH, TN_COLS), lambda b, j: (0, j))],
        out_specs=pl.BlockSpec((t, TN_COLS), lambda b, j: (b, j)),
        out_shape=jax.ShapeDtypeStruct((n, ch), F32),
        scratch_shapes=[pltpu.VMEM((t + CONV_PAD, TN_COLS), F32)],
        compiler_params=_cparams("parallel", "parallel"),
    )(proj, cw)


def conv_bwd(proj, cw, dy, nseq):
    n = proj.shape[0]
    t = n // nseq
    ch = cw.shape[1]

    def body(x_ref, cw_ref, dy_ref, dx_ref, dcw_ref, pad_s, dpad_s):
        pad_s[pl.ds(0, CONV_PAD), :] = jnp.zeros((CONV_PAD, TN_COLS), F32)
        pad_s[pl.ds(CONV_PAD, t), :] = x_ref[...]
        dpad_s[pl.ds(t, CONV_PAD), :] = jnp.zeros((CONV_PAD, TN_COLS), F32)
        cwv = cw_ref[...]
        acc = [jnp.zeros((1, TN_COLS), F32)] * CONV_WIDTH
        for c in range(t // CONV_ROWS):
            pre, xs = _conv_pre(pad_s, cwv, c)
            sg = _sigmoid(pre)
            dpre = dy_ref[pl.ds(c * CONV_ROWS, CONV_ROWS), :] * (sg * (1.0 + pre * (1.0 - sg)))
            dpad_s[pl.ds(c * CONV_ROWS, CONV_ROWS), :] = dpre
            acc = [acc[k] + jnp.sum(dpre * xs[k], axis=0, keepdims=True) for k in range(CONV_WIDTH)]
        for k in range(CONV_WIDTH):
            dcw_ref[0, pl.ds(k, 1), :] = acc[k]
        for c in range(t // CONV_ROWS):
            dx = dpad_s[pl.ds(c * CONV_ROWS + CONV_WIDTH - 1, CONV_ROWS), :] * cwv[0:1, :]
            for k in range(1, CONV_WIDTH):
                dx = dx + dpad_s[pl.ds(c * CONV_ROWS + CONV_WIDTH - 1 - k, CONV_ROWS), :] * cwv[k:k + 1, :]
            dx_ref[pl.ds(c * CONV_ROWS, CONV_ROWS), :] = dx

    blk = pl.BlockSpec((t, TN_COLS), lambda b, j: (b, j))
    return pl.pallas_call(
        body, name="conv_bwd", grid=(nseq, ch // TN_COLS),
        in_specs=[blk, pl.BlockSpec((CONV_WIDTH, TN_COLS), lambda b, j: (0, j)), blk],
        out_specs=[blk, pl.BlockSpec((1, CONV_WIDTH, TN_COLS), lambda b, j: (b, 0, j))],
        out_shape=[jax.ShapeDtypeStruct((n, ch), F32), jax.ShapeDtypeStruct((nseq, CONV_WIDTH, ch), F32)],
        scratch_shapes=[pltpu.VMEM((t + CONV_PAD, TN_COLS), F32)] * 2,
        compiler_params=_cparams("parallel", "parallel"),
    )(proj, cw, dy)


def _gate_specs(d):
    wide = pl.BlockSpec((TM_ROWS, d), lambda i: (i, 0))
    oa = pl.BlockSpec((TM_ROWS, A_WIDTH), lambda i: (i, 0))
    ob = pl.BlockSpec((TM_ROWS, B_WIDTH), lambda i: (i, 0))
    oc = pl.BlockSpec((TM_ROWS, C_WIDTH), lambda i: (i, 0))
    za = pl.BlockSpec((TM_ROWS, A_WIDTH), lambda i: (i, COL_AZ // A_WIDTH))
    zb = pl.BlockSpec((TM_ROWS, B_WIDTH), lambda i: (i, (COL_B + 3 * B_WIDTH) // B_WIDTH))
    zc = pl.BlockSpec((TM_ROWS, C_WIDTH), lambda i: (i, (COL_C + 3 * C_WIDTH) // C_WIDTH))
    return wide, oa, ob, oc, za, zb, zc


BRANCH_COLS = ((0, A_WIDTH), (A_WIDTH, A_WIDTH + B_WIDTH), (A_WIDTH + B_WIDTH, D_MODEL))


def outproj_fwd(x, oa, ob, oc, proj, w):
    n, d = x.shape

    def body(x_ref, oa_ref, ob_ref, oc_ref, za_ref, zb_ref, zc_ref, w_ref, y_ref, m_ref):
        for (lo, hi), o_ref, z_ref in zip(BRANCH_COLS, (oa_ref, ob_ref, oc_ref), (za_ref, zb_ref, zc_ref)):
            zv = z_ref[...]
            m_ref[:, lo:hi] = (o_ref[...] * (zv * _sigmoid(zv))).astype(BF16)
        y_ref[...] = x_ref[...] + jnp.dot(m_ref[...], w_ref[...], preferred_element_type=F32)

    wide, sa, sb, sc, za, zb, zc = _gate_specs(d)
    return pl.pallas_call(
        body, name="outproj_fwd", grid=(n // TM_ROWS,),
        in_specs=[wide, sa, sb, sc, za, zb, zc, pl.BlockSpec((d, d), lambda i: (0, 0))],
        out_specs=[wide, wide],
        out_shape=[jax.ShapeDtypeStruct((n, d), F32), jax.ShapeDtypeStruct((n, d), BF16)],
        compiler_params=_cparams("parallel"),
    )(x, oa, ob, oc, proj, proj, proj, w)


def outproj_bwd(dy, oa, ob, oc, proj, w):
    n, d = dy.shape

    def body(dy_ref, oa_ref, ob_ref, oc_ref, za_ref, zb_ref, zc_ref, w_ref, doa_ref, dob_ref, doc_ref, dza_ref, dzb_ref, dzc_ref):
        dm = lax.dot_general(dy_ref[...].astype(BF16), w_ref[...], NT, preferred_element_type=F32)
        for (lo, hi), o_ref, z_ref, do_ref, dz_ref in zip(BRANCH_COLS, (oa_ref, ob_ref, oc_ref), (za_ref, zb_ref, zc_ref),
                                                          (doa_ref, dob_ref, doc_ref), (dza_ref, dzb_ref, dzc_ref)):
            zv = z_ref[...]
            sg = _sigmoid(zv)
            dmv = dm[:, lo:hi]
            do_ref[...] = dmv * (zv * sg)
            dz_ref[...] = dmv * o_ref[...] * (sg * (1.0 + zv * (1.0 - sg)))

    wide, sa, sb, sc, za, zb, zc = _gate_specs(d)
    sd = jax.ShapeDtypeStruct
    outs = [sd((n, A_WIDTH), F32), sd((n, B_WIDTH), F32), sd((n, C_WIDTH), F32)]
    return pl.pallas_call(
        body, name="outproj_bwd", grid=(n // TM_ROWS,),
        in_specs=[wide, sa, sb, sc, za, zb, zc, pl.BlockSpec((d, d), lambda i: (0, 0))],
        out_specs=[sa, sb, sc, sa, sb, sc],
        out_shape=outs + outs,
        compiler_params=_cparams("parallel"),
    )(dy, oa, ob, oc, proj, proj, proj, w)


def loss_fwd_bwd(y, target):
    n, d = y.shape

    def body(y_ref, t_ref, dy_ref, part_ref):
        e = y_ref[...] - t_ref[...]
        dy_ref[...] = e * (1.0 / d)
        part_ref[...] = jnp.zeros_like(part_ref) + 0.5 * jnp.sum(e * e) * (1.0 / d)

    blk = pl.BlockSpec((TM_ROWS, d), lambda i: (i, 0))
    return pl.pallas_call(
        body, name="loss", grid=(n // TM_ROWS,),
        in_specs=[blk, blk],
        out_specs=[blk, pl.BlockSpec((1, 8, BLOCK), lambda i: (i, 0, 0))],
        out_shape=[jax.ShapeDtypeStruct((n, d), F32), jax.ShapeDtypeStruct((n // TM_ROWS, 8, BLOCK), F32)],
        compiler_params=_cparams("parallel"),
    )(y, target)


ADAM_LR, ADAM_B1, ADAM_B2, ADAM_EPS, ADAM_WD, ADAM_STEP = 0.001, 0.9, 0.999, 1e-08, 0.01, 10


def adamw(w, g, m, v):
    r, c = w.shape
    tr = r if r <= 256 else 256

    def body(w_ref, g_ref, m_ref, v_ref, d_ref, nm_ref, nv_ref):
        gv = g_ref[...]
        nm = ADAM_B1 * m_ref[...] + (1.0 - ADAM_B1) * gv
        nv = ADAM_B2 * v_ref[...] + (1.0 - ADAM_B2) * (gv * gv)
        m_hat = nm / (1.0 - ADAM_B1 ** ADAM_STEP)
        v_hat = nv / (1.0 - ADAM_B2 ** ADAM_STEP)
        d_ref[...] = -ADAM_LR * (m_hat / (jnp.sqrt(v_hat) + ADAM_EPS) + ADAM_WD * w_ref[...])
        nm_ref[...] = nm
        nv_ref[...] = nv

    blk = pl.BlockSpec((tr, c), lambda i: (i, 0))
    return pl.pallas_call(
        body, name="adamw", grid=(r // tr,),
        in_specs=[blk] * 4, out_specs=[blk] * 3,
        out_shape=[jax.ShapeDtypeStruct((r, c), F32)] * 3,
        compiler_params=_cparams("parallel"),
    )(w, g, m, v)


SB_G = N_HEADS_B


def _sb_tile(q, k, v, carry, tri, diag):
    z = _mm(q * (HEAD_DIM ** -0.5), k, NT)
    sp = jnp.log(1.0 + jnp.exp(-jnp.abs(z)))
    ls_pos = jnp.minimum(z, 0.0) - sp
    ls_neg = jnp.minimum(-z, 0.0) - sp
    if diag:
        earlier = lax.broadcasted_iota(jnp.int32, z.shape, 1) < lax.broadcasted_iota(jnp.int32, z.shape, 0)
        log_keep = jnp.where(earlier, ls_neg, 0.0)
    else:
        log_keep = ls_neg
    hi = log_keep.astype(BF16)
    lo = lax.stop_gradient(log_keep - hi.astype(F32)).astype(BF16)
    within = lax.dot_general(jnp.concatenate([hi, lo], axis=1), tri, NN, preferred_element_type=F32)
    arg = ls_pos + within + carry
    wts = jnp.where(earlier, jnp.exp(jnp.where(earlier, arg, 0.0)), 0.0) if diag else jnp.exp(arg)
    return _mm(wts, v), jnp.sum(log_keep, axis=1, keepdims=True)


_sb_tiles_diag = jax.vmap(functools.partial(_sb_tile, diag=True), in_axes=(0, 0, 0, 0, None))
_sb_tiles_off = jax.vmap(functools.partial(_sb_tile, diag=False), in_axes=(0, 0, 0, 0, None))


def _sb_tri():
    r = lax.broadcasted_iota(jnp.int32, (2 * BLOCK, BLOCK), 0) % BLOCK
    c = lax.broadcasted_iota(jnp.int32, (2 * BLOCK, BLOCK), 1)
    return jnp.where(r > c, 1.0, 0.0).astype(BF16)


SB_SEQ = 2


def _sb_specs(t, nq):
    cb = COL_B // B_WIDTH
    sq = SB_SEQ
    q = pl.BlockSpec((sq, BLOCK, B_WIDTH), lambda b, i: (b, i, cb))
    k = pl.BlockSpec((sq, t, B_WIDTH), lambda b, i: (b, 0, cb + 1))
    v = pl.BlockSpec((sq, t, B_WIDTH), lambda b, i: (b, 0, cb + 2))
    blk = pl.BlockSpec((sq, BLOCK, B_WIDTH), lambda b, i: (b, i, 0))
    full = pl.BlockSpec((sq, t, B_WIDTH), lambda b, i: (b, 0, 0))
    carry = pl.BlockSpec((sq, 1, nq, BLOCK, SB_G), lambda b, i: (b, i, 0, 0, 0))
    return q, k, v, blk, full, carry


def _sb_heads(ref, rows):
    return jnp.concatenate([_heads(ref[b, rows, :], SB_G) for b in range(SB_SEQ)])


def _sb_unheads(a):
    return [_unheads(a[b * SB_G:(b + 1) * SB_G]) for b in range(SB_SEQ)]


def sb_fwd(proj, nseq):
    n = proj.shape[0]
    t = n // nseq
    nq = t // BLOCK
    g, sq = SB_G, SB_SEQ
    everything = pl.ds(0, BLOCK)

    def body(q_ref, k_ref, v_ref, o_ref, carry_ref):
        i = pl.program_id(1)
        tri = _sb_tri()
        qv = _sb_heads(q_ref, everything)

        def tile(j, c, fn):
            rows = pl.ds(pl.multiple_of(j * BLOCK, BLOCK), BLOCK)
            for b in range(sq):
                carry_ref[b, 0, j] = jnp.concatenate([c[b * g + h] for h in range(g)], axis=1)
            return fn(qv, _sb_heads(k_ref, rows), _sb_heads(v_ref, rows), c, tri)

        def step(it, st):
            o_acc, c = st
            o, tot = tile(i - 1 - it, c, _sb_tiles_off)
            return o_acc + o, c + tot

        o_acc, _ = lax.fori_loop(0, i, step, tile(i, jnp.zeros((sq * g, BLOCK, 1), F32), _sb_tiles_diag))
        for b, o in enumerate(_sb_unheads(o_acc)):
            o_ref[b] = o

    q, k, v, blk, _, carry = _sb_specs(t, nq)
    proj3 = proj.reshape(nseq, t, -1)
    o, carries = pl.pallas_call(
        body, name="sb_fwd", grid=(nseq // sq, nq),
        in_specs=[q, k, v],
        out_specs=[blk, carry],
        out_shape=[jax.ShapeDtypeStruct((nseq, t, B_WIDTH), F32),
                   jax.ShapeDtypeStruct((nseq, nq, nq, BLOCK, g), F32)],
        compiler_params=_cparams("parallel", "arbitrary"),
    )(proj3, proj3, proj3)
    return o.reshape(n, B_WIDTH), carries


def sb_bwd(proj, carries, do, nseq):
    n = proj.shape[0]
    t = n // nseq
    nq = t // BLOCK
    g, sq = SB_G, SB_SEQ
    everything = pl.ds(0, BLOCK)

    def body(q_ref, k_ref, v_ref, carry_ref, do_ref, dq_ref, dk_ref, dv_ref):
        i = pl.program_id(1)

        @pl.when(i == 0)
        def _():
            dk_ref[...] = jnp.zeros_like(dk_ref)
            dv_ref[...] = jnp.zeros_like(dv_ref)

        tri = _sb_tri()
        qv = _sb_heads(q_ref, everything)
        dov = _sb_heads(do_ref, everything)

        def tile(j, st, fn):
            dq_acc, dc = st
            rows = pl.ds(pl.multiple_of(j * BLOCK, BLOCK), BLOCK)
            cj = [carry_ref[b, 0, j] for b in range(sq)]
            _, vjp = jax.vjp(lambda q_, k_, v_, c_: fn(q_, k_, v_, c_, tri), qv, _sb_heads(k_ref, rows), _sb_heads(v_ref, rows),
                             jnp.stack([cj[b][:, h:h + 1] for b in range(sq) for h in range(g)]))
            dq, dk, dv, dcj = vjp((dov, dc))
            for b, (dkb, dvb) in enumerate(zip(_sb_unheads(dk), _sb_unheads(dv))):
                dk_ref[b, rows, :] += dkb
                dv_ref[b, rows, :] += dvb
            return dq_acc + dq, dc + dcj

        st = lax.fori_loop(0, i, lambda j, st: tile(j, st, _sb_tiles_off),
                           (jnp.zeros((sq * g, BLOCK, HEAD_DIM), F32), jnp.zeros((sq * g, BLOCK, 1), F32)))
        dq_acc, _ = tile(i, st, _sb_tiles_diag)
        for b, dq in enumerate(_sb_unheads(dq_acc)):
            dq_ref[b] = dq

    q, k, v, blk, full, carry = _sb_specs(t, nq)
    proj3, do3 = proj.reshape(nseq, t, -1), do.reshape(nseq, t, -1)
    grads = pl.pallas_call(
        body, name="sb_bwd", grid=(nseq // sq, nq),
        in_specs=[q, k, v, carry, blk],
        out_specs=[blk, full, full],
        out_shape=[jax.ShapeDtypeStruct((nseq, t, B_WIDTH), F32)] * 3,
        compiler_params=_cparams("parallel", "arbitrary"),
    )(proj3, proj3, proj3, carries, do3)
    return [a.reshape(n, B_WIDTH) for a in grads]


@jax.custom_vjp
def _unit_lower_inverse(a):
    n = a.shape[0]
    eye = jnp.where(lax.broadcasted_iota(jnp.int32, (n, n), 0) == lax.broadcasted_iota(jnp.int32, (n, n), 1), 1.0, 0.0)
    tmat = eye.astype(F32) - a
    p = a
    for _ in range(5):
        p = _mm32(p, p)
        tmat = tmat + _mm32(tmat, p)
    return tmat


def _unit_lower_inverse_fwd(a):
    tmat = _unit_lower_inverse(a)
    return tmat, tmat


def _unit_lower_inverse_bwd(tmat, g):
    return (-_mm32(_mm32(tmat, g, TN), tmat, NT),)


_unit_lower_inverse.defvjp(_unit_lower_inverse_fwd, _unit_lower_inverse_bwd)


def _gdn_chunk(q, k, v, al_c, al_r, br_c, alog, dtb, nw, s):
    c = GDN_CHUNK
    ri = lax.broadcasted_iota(jnp.int32, (c, c), 0)
    ci = lax.broadcasted_iota(jnp.int32, (c, c), 1)
    incl, strict = ri >= ci, ri > ci
    eye = jnp.where(ri == ci, 1.0, 0.0).astype(F32)
    rate = -jnp.exp(alog)
    g_c = rate * _softplus(al_c + dtb)
    g_r = rate * _softplus(al_r + dtb)
    beta = _sigmoid(br_c)
    gc_c = jnp.sum(jnp.where(incl, g_r, 0.0), axis=1, keepdims=True)
    gc_r = jnp.sum(jnp.where(ri <= ci, g_c, 0.0), axis=0, keepdims=True)
    gl = jnp.sum(g_r, axis=1, keepdims=True)
    decay = jnp.where(incl, jnp.exp(jnp.where(incl, gc_c - gc_r, 0.0)), 0.0)
    qn = q * lax.rsqrt(jnp.sum(q * q, axis=-1, keepdims=True) + RMS_EPS) * (HEAD_DIM ** -0.5)
    kn = k * lax.rsqrt(jnp.sum(k * k, axis=-1, keepdims=True) + RMS_EPS)
    kb = kn * beta
    a = jnp.where(strict, _mm(kb, kn, NT) * decay, 0.0)
    tmat = _unit_lower_inverse(a)
    u = _mm(tmat, v * beta)
    w = _mm(tmat, kb * jnp.exp(gc_c))
    qk = _mm(qn, kn, NT) * decay
    v_new = u - _mm(w, s)
    o = _mm(qn * jnp.exp(gc_c), s) + _mm(qk, v_new)
    s_new = s * jnp.exp(gl) + _mm(kn * jnp.exp(gl - gc_c), v_new, TN)
    o = o * lax.rsqrt(jnp.mean(o * o, axis=-1, keepdims=True) + RMS_EPS) * nw
    return o, s_new


_gdn_chunks = jax.vmap(_gdn_chunk, in_axes=(0, 0, 0, 0, 0, 0, 0, 0, None, 0))

GDN_TB = 256
GDN_SEQ_FWD = 2
GDN_SEQ_BWD = 2


def _gdn_block(q3, k3, v3, ba, alog, dtb, nw, s):
    nh = N_HEADS_A
    ns = q3.shape[0]
    bat = [ba[b].T for b in range(ns)]
    br_c = jnp.stack([ba[b][:, h:h + 1] for b in range(ns) for h in range(nh)])
    al_c = jnp.stack([ba[b][:, nh + h:nh + h + 1] for b in range(ns) for h in range(nh)])
    al_r = jnp.stack([bat[b][nh + h:nh + h + 1, :] for b in range(ns) for h in range(nh)])
    heads = lambda a: jnp.concatenate([_heads(a[b], nh) for b in range(ns)])
    o, s_new = _gdn_chunks(heads(q3), heads(k3), heads(v3), al_c, al_r, br_c,
                           jnp.concatenate([alog] * ns), jnp.concatenate([dtb] * ns), nw, s)
    return jnp.stack([_unheads(o[b * nh:(b + 1) * nh]) for b in range(ns)]), s_new


def _gdn_specs(nt, sq, rev):
    tpos = (lambda i: nt - 1 - i) if rev else (lambda i: i)
    ncb = GDN_TB // GDN_CHUNK
    qkv = [pl.BlockSpec((sq, GDN_TB, A_WIDTH), lambda b, i, j=j: (b, tpos(i), j)) for j in range(3)]
    ba = pl.BlockSpec((sq, GDN_TB, BLOCK), lambda b, i: (b, tpos(i), COL_BA // BLOCK))
    one = pl.BlockSpec((N_HEADS_A, 1, 1), lambda b, i: (0, 0, 0))
    vec = pl.BlockSpec((1, HEAD_DIM), lambda b, i: (0, 0))
    st = pl.BlockSpec((sq, ncb, N_HEADS_A, HEAD_DIM, HEAD_DIM), lambda b, i: (b, tpos(i), 0, 0, 0))
    oa = pl.BlockSpec((sq, GDN_TB, A_WIDTH), lambda b, i: (b, tpos(i), 0))
    return qkv, ba, one, vec, st, oa, tpos


def gdn_fwd(ya, proj, alog, dtb, nw, nseq):
    n = ya.shape[0]
    t = n // nseq
    nc, nt, ncb = t // GDN_CHUNK, t // GDN_TB, GDN_TB // GDN_CHUNK
    sq = GDN_SEQ_FWD
    nh = N_HEADS_A

    def body(q_ref, k_ref, v_ref, ba_ref, alog_ref, dtb_ref, nw_ref, o_ref, st_ref, s_s):
        @pl.when(pl.program_id(1) == 0)
        def _():
            s_s[...] = jnp.zeros_like(s_s)

        def step(c, s):
            rows = pl.ds(pl.multiple_of(c * GDN_CHUNK, GDN_CHUNK), GDN_CHUNK)
            for b in range(sq):
                st_ref[b, c] = s[b * nh:(b + 1) * nh]
            o, s_new = _gdn_block(q_ref[:, rows, :], k_ref[:, rows, :], v_ref[:, rows, :], ba_ref[:, rows, :],
                                  alog_ref[...], dtb_ref[...], nw_ref[...], s)
            o_ref[:, rows, :] = o
            return s_new

        s_s[...] = lax.fori_loop(0, ncb, step, s_s[...])

    qkv, ba, one, vec, st, oa, _ = _gdn_specs(nt, sq, False)
    ya3, proj3 = ya.reshape(nseq, t, -1), proj.reshape(nseq, t, -1)
    o, states = pl.pallas_call(
        body, name="gdn_fwd", grid=(nseq // sq, nt),
        in_specs=qkv + [ba, one, one, vec],
        out_specs=[oa, st],
        out_shape=[jax.ShapeDtypeStruct((nseq, t, A_WIDTH), F32),
                   jax.ShapeDtypeStruct((nseq, nc, nh, HEAD_DIM, HEAD_DIM), F32)],
        scratch_shapes=[pltpu.VMEM((sq * nh, HEAD_DIM, HEAD_DIM), F32)],
        compiler_params=_cparams("parallel", "arbitrary"),
    )(ya3, ya3, ya3, proj3, alog, dtb, nw)
    return o.reshape(n, A_WIDTH), states


def gdn_bwd(ya, proj, alog, dtb, nw, states, do, nseq):
    n = ya.shape[0]
    t = n // nseq
    nt, ncb = t // GDN_TB, GDN_TB // GDN_CHUNK
    nh = N_HEADS_A
    sq = GDN_SEQ_BWD

    def body(q_ref, k_ref, v_ref, ba_ref, alog_ref, dtb_ref, nw_ref, st_ref, do_ref,
             dya_ref, dba_ref, dalog_ref, ddtb_ref, dnw_ref, ds_s):
        @pl.when(pl.program_id(1) == 0)
        def _():
            ds_s[...] = jnp.zeros_like(ds_s)
            dalog_ref[...] = jnp.zeros_like(dalog_ref)
            ddtb_ref[...] = jnp.zeros_like(ddtb_ref)
            dnw_ref[...] = jnp.zeros_like(dnw_ref)

        def step(it, carry):
            ds, dalog, ddtb, dnw = carry
            c = ncb - 1 - it
            rows = pl.ds(pl.multiple_of(c * GDN_CHUNK, GDN_CHUNK), GDN_CHUNK)
            s_in = jnp.concatenate([st_ref[b, c] for b in range(sq)])
            _, vjp = jax.vjp(_gdn_block, q_ref[:, rows, :], k_ref[:, rows, :], v_ref[:, rows, :], ba_ref[:, rows, :],
                             alog_ref[...], dtb_ref[...], nw_ref[...], s_in)
            dq, dk, dv, dba, da, dd, dn, ds = vjp((do_ref[:, rows, :], ds))
            dya_ref[:, rows, 0:A_WIDTH] = dq
            dya_ref[:, rows, A_WIDTH:2 * A_WIDTH] = dk
            dya_ref[:, rows, 2 * A_WIDTH:3 * A_WIDTH] = dv
            dba_ref[:, rows, :] = dba
            return ds, dalog + da, ddtb + dd, dnw + dn

        z11 = jnp.zeros((nh, 1, 1), F32)
        ds, dalog, ddtb, dnw = lax.fori_loop(0, ncb, step, (ds_s[...], z11, z11, jnp.zeros((1, HEAD_DIM), F32)))
        ds_s[...] = ds
        dalog_ref[0] += dalog
        ddtb_ref[0] += ddtb
        dnw_ref[0] += dnw

    qkv, ba, one, vec, st, oa, tpos = _gdn_specs(nt, sq, True)
    per_grp = pl.BlockSpec((1, nh, 1, 1), lambda b, i: (b, 0, 0, 0))
    sd = jax.ShapeDtypeStruct
    ya3, proj3, do3 = ya.reshape(nseq, t, -1), proj.reshape(nseq, t, -1), do.reshape(nseq, t, -1)
    dya, dba, dalog, ddtb, dnw = pl.pallas_call(
        body, name="gdn_bwd", grid=(nseq // sq, nt),
        in_specs=qkv + [ba, one, one, vec, st, oa],
        out_specs=[pl.BlockSpec((sq, GDN_TB, 3 * A_WIDTH), lambda b, i: (b, tpos(i), 0)),
                   pl.BlockSpec((sq, GDN_TB, BLOCK), lambda b, i: (b, tpos(i), 0)),
                   per_grp, per_grp, pl.BlockSpec((1, 1, HEAD_DIM), lambda b, i: (b, 0, 0))],
        out_shape=[sd((nseq, t, 3 * A_WIDTH), F32), sd((nseq, t, BLOCK), F32), sd((nseq // sq, nh, 1, 1), F32),
                   sd((nseq // sq, nh, 1, 1), F32), sd((nseq // sq, 1, HEAD_DIM), F32)],
        scratch_shapes=[pltpu.VMEM((sq * nh, HEAD_DIM, HEAD_DIM), F32)],
        compiler_params=_cparams("parallel", "arbitrary"),
    )(ya3, ya3, ya3, proj3, alog, dtb, nw, states, do3)
    return dya.reshape(n, 3 * A_WIDTH), dba.reshape(n, BLOCK), dalog, ddtb, dnw


DIL_NB = tuple((SEQ // d) // BLOCK for _, d in DILATED_PAIRS)
DIL_D = tuple(d for _, d in DILATED_PAIRS)
DIL_STEPS = tuple(w // d for w, d in DILATED_PAIRS)
DIL_B = 4
PAIR = 2 * HEAD_DIM


def _rope_tables(t):
    half = ROPE_DIM // 2
    inv_freq = ROPE_THETA ** (-jnp.arange(half, dtype=F32) / half)
    ang = jnp.arange(t, dtype=F32)[:, None] * inv_freq[None, :]
    ones = jnp.ones((t, HEAD_DIM - ROPE_DIM), F32)
    cs = jnp.concatenate([jnp.cos(ang), jnp.cos(ang), ones], axis=1)
    sn = jnp.concatenate([jnp.sin(ang), jnp.sin(ang), 0.0 * ones], axis=1)
    i = jnp.arange(PAIR)[:, None]
    j = jnp.arange(PAIR)[None, :]
    same = (i // HEAD_DIM) == (j // HEAD_DIM)
    ih, jh = i % HEAD_DIM, j % HEAD_DIM
    pm = (jnp.where(same & (jh < half) & (ih == jh + half), -1.0, 0.0)
          + jnp.where(same & (jh >= half) & (jh < ROPE_DIM) & (ih == jh - half), 1.0, 0.0))
    mean = jnp.where(same, 1.0 / HEAD_DIM, 0.0)
    twice = lambda m: jnp.concatenate([m, m]).astype(BF16)
    return jnp.tile(cs, (1, 2)), jnp.tile(sn, (1, 2)), twice(mean), twice(pm)


def _split_dot(x, w2):
    hi = x.astype(BF16)
    lo = lax.stop_gradient(x - hi.astype(F32)).astype(BF16)
    return lax.dot_general(jnp.concatenate([hi, lo], axis=1), w2, NN, preferred_element_type=F32)


def _dil_prep(x, w, cs, sn, mean2, pm2):
    y = x * lax.rsqrt(_split_dot(x * x, mean2) + RMS_EPS) * w
    return y * cs + _split_dot(y, pm2) * sn


def _dil_tile(qn, kk, vv, bias):
    lane = lax.broadcasted_iota(jnp.int32, (1, PAIR), 1)
    outs, lses = [], []
    for h in range(2):
        s = _mm(jnp.where(lane // HEAD_DIM == h, qn, 0.0) * (HEAD_DIM ** -0.5), kk, NT) + bias
        m = lax.stop_gradient(jnp.max(s, axis=-1, keepdims=True))
        p = jnp.exp(s - m)
        denom = jnp.sum(p, axis=-1, keepdims=True)
        outs.append(_mm(p, vv) / denom)
        lses.append(m + jnp.log(denom))
    return jnp.where(lane < HEAD_DIM, outs[0], outs[1]), jnp.concatenate(lses, axis=1)


_dil_tiles = jax.vmap(_dil_tile)


def _spread(a):
    lane = lax.broadcasted_iota(jnp.int32, (a.shape[0], PAIR), 1)
    return jnp.where(lane < HEAD_DIM, a[:, 0:1], a[:, 1:2])


def _dil_mix(o1, o2, o3, l1, l2, l3):
    m = lax.stop_gradient(jnp.maximum(jnp.maximum(l1, l2), l3))
    e1, e2, e3 = jnp.exp(l1 - m), jnp.exp(l2 - m), jnp.exp(l3 - m)
    r = 1.0 / (e1 + e2 + e3)
    return _spread(e1 * r) * o1 + _spread(e2 * r) * o2 + _spread(e3 * r) * o3


def _dil_fill_biases(bias_s):
    steps, = set(DIL_STEPS)
    qi = lax.broadcasted_iota(jnp.int32, (BLOCK, 1), 0)
    kj = lax.broadcasted_iota(jnp.int32, (1, 2 * BLOCK), 1)
    rel = qi - kj + BLOCK
    inside = (rel >= 0) & (rel <= steps)
    bias_s[0] = jnp.where(inside, 0.0, NEG)
    bias_s[1] = jnp.where(inside & (kj >= BLOCK), 0.0, NEG)
    bias_s[2] = jnp.where((qi >= kj) & (qi - kj <= steps), 0.0, NEG)


def _dil_mask(it, g, bias_s):
    qrows = pl.ds(pl.multiple_of(it * BLOCK, BLOCK), BLOCK)
    if DIL_NB[g] == 1:
        return bias_s[2, :, 0:BLOCK], qrows, qrows
    which = jnp.where(it == 0, 2, jnp.where(it % DIL_NB[g] == 0, 1, 0))
    kstart = jnp.maximum(it - 1, 0) * BLOCK
    return bias_s[which], qrows, pl.ds(pl.multiple_of(kstart, BLOCK), 2 * BLOCK)


def _dil_gather(src, dst, d):
    t = src.shape[0]
    ln = t // d
    for r in range(d):
        dst[pl.ds(r * ln, ln), :] = src[pl.ds(r, ln, stride=d), :]


def _dil_scatter(src, dst, d):
    t = src.shape[0]
    ln = t // d
    for r in range(d):
        dst[pl.ds(r, ln, stride=d), :] = src[pl.ds(r * ln, ln), :]


def _dil_forward_parts(q_ref, k_ref, v_ref, qw, kw, cs_ref, sn_ref, mean2, pm2, qn_s, kn_s, dl_s, od_s, ld_s, on_s, ln_s, bias_s):
    t = qn_s.shape[0]
    _dil_fill_biases(bias_s)

    def prep(c, _):
        rows = pl.ds(pl.multiple_of(c * ROWS, ROWS), ROWS)
        qn_s[rows, :] = _dil_prep(q_ref[rows, :], qw, cs_ref[rows, :], sn_ref[rows, :], mean2, pm2)
        kn_s[rows, :] = _dil_prep(k_ref[rows, :], kw, cs_ref[rows, :], sn_ref[rows, :], mean2, pm2)
        return 0

    lax.fori_loop(0, t // ROWS, prep, 0)
    for g in (1, 2):
        _dil_gather(qn_s, dl_s.at[g - 1, 0], DIL_D[g])
        _dil_gather(kn_s, dl_s.at[g - 1, 1], DIL_D[g])
        _dil_gather(v_ref, dl_s.at[g - 1, 2], DIL_D[g])
    for g in range(3):
        qs = qn_s if g == 0 else dl_s.at[g - 1, 0]
        ks = kn_s if g == 0 else dl_s.at[g - 1, 1]
        vs = v_ref if g == 0 else dl_s.at[g - 1, 2]

        def tiles(i, _, g=g, qs=qs, ks=ks, vs=vs):
            where = [_dil_mask(i * DIL_B + b, g, bias_s) for b in range(DIL_B)]
            o, lse = _dil_tiles(jnp.stack([qs[qr, :] for _, qr, _ in where]), jnp.stack([ks[kr, :] for _, _, kr in where]),
                                jnp.stack([vs[kr, :] for _, _, kr in where]), jnp.stack([m for m, _, _ in where]))
            for b, (_, qr, _) in enumerate(where):
                od_s[g, qr, :] = o[b]
                ld_s[g, qr, :] = lse[b]
            return 0

        lax.fori_loop(0, t // BLOCK // DIL_B, tiles, 0)
    for g in (1, 2):
        _dil_scatter(od_s.at[g], on_s.at[g - 1], DIL_D[g])
        _dil_scatter(ld_s.at[g], ln_s.at[g - 1], DIL_D[g])


def _dil_scratch(t):
    return [pltpu.VMEM((t, PAIR), F32), pltpu.VMEM((t, PAIR), F32),
            pltpu.VMEM((2, 3, t, PAIR), F32),
            pltpu.VMEM((3, t, PAIR), F32), pltpu.VMEM((3, t, 2), F32),
            pltpu.VMEM((2, t, PAIR), F32), pltpu.VMEM((2, t, 2), F32),
            pltpu.VMEM((3, BLOCK, 2 * BLOCK), F32)]


def _dil_specs(t):
    cb = COL_C // BLOCK
    per = C_WIDTH // BLOCK
    qkv = [pl.BlockSpec((t, BLOCK), lambda b, p, j=j: (b, cb + j * per + p)) for j in range(3)]
    vec = pl.BlockSpec((1, PAIR), lambda b, p: (0, 0))
    tab = pl.BlockSpec((t, PAIR), lambda b, p: (0, 0))
    mat = pl.BlockSpec((2 * PAIR, PAIR), lambda b, p: (0, 0))
    pair = pl.BlockSpec((t, BLOCK), lambda b, p: (b, p))
    return qkv, vec, tab, mat, pair


def dil_fwd(proj, qw, kw, cs, sn, mean2, pm2, nseq):
    n = proj.shape[0]
    t = n // nseq

    def body(q_ref, k_ref, v_ref, qw_ref, kw_ref, cs_ref, sn_ref, mean_ref, pm_ref, o_ref,
             qn_s, kn_s, dl_s, od_s, ld_s, on_s, ln_s, bias_s):
        _dil_forward_parts(q_ref, k_ref, v_ref, qw_ref[...], kw_ref[...], cs_ref, sn_ref, mean_ref[...], pm_ref[...],
                           qn_s, kn_s, dl_s, od_s, ld_s, on_s, ln_s, bias_s)

        def mix(c, _):
            rows = pl.ds(pl.multiple_of(c * ROWS, ROWS), ROWS)
            o_ref[rows, :] = _dil_mix(od_s[0, rows, :], on_s[0, rows, :], on_s[1, rows, :],
                                      ld_s[0, rows, :], ln_s[0, rows, :], ln_s[1, rows, :])
            return 0

        lax.fori_loop(0, t // ROWS, mix, 0)

    qkv, vec, tab, mat, pair = _dil_specs(t)
    return pl.pallas_call(
        body, name="dil_fwd", grid=(nseq, C_WIDTH // BLOCK),
        in_specs=qkv + [vec, vec, tab, tab, mat, mat],
        out_specs=pair,
        out_shape=jax.ShapeDtypeStruct((n, C_WIDTH), F32),
        scratch_shapes=_dil_scratch(t),
        compiler_params=_cparams("parallel", "parallel"),
    )(proj, proj, proj, qw, kw, cs, sn, mean2, pm2)


def dil_bwd(proj, qw, kw, cs, sn, mean2, pm2, do, nseq):
    n = proj.shape[0]
    t = n // nseq

    def body(q_ref, k_ref, v_ref, qw_ref, kw_ref, cs_ref, sn_ref, mean_ref, pm_ref, do_ref,
             dq_ref, dk_ref, dv_ref, dqw_ref, dkw_ref,
             qn_s, kn_s, dl_s, od_s, ld_s, on_s, ln_s, bias_s, tq_s, tk_s, tv_s):
        qw, kw, mean2, pm2 = qw_ref[...], kw_ref[...], mean_ref[...], pm_ref[...]
        _dil_forward_parts(q_ref, k_ref, v_ref, qw, kw, cs_ref, sn_ref, mean2, pm2, qn_s, kn_s, dl_s, od_s, ld_s, on_s, ln_s, bias_s)

        def mix(c, _):
            rows = pl.ds(pl.multiple_of(c * ROWS, ROWS), ROWS)
            _, vjp = jax.vjp(_dil_mix, od_s[0, rows, :], on_s[0, rows, :], on_s[1, rows, :],
                             ld_s[0, rows, :], ln_s[0, rows, :], ln_s[1, rows, :])
            d1, d2, d3, e1, e2, e3 = vjp(do_ref[rows, :])
            od_s[0, rows, :] = d1
            on_s[0, rows, :] = d2
            on_s[1, rows, :] = d3
            ld_s[0, rows, :] = e1
            ln_s[0, rows, :] = e2
            ln_s[1, rows, :] = e3
            return 0

        lax.fori_loop(0, t // ROWS, mix, 0)
        for g in (1, 2):
            _dil_gather(on_s.at[g - 1], od_s.at[g], DIL_D[g])
            _dil_gather(ln_s.at[g - 1], ld_s.at[g], DIL_D[g])
        on_s[...] = jnp.zeros_like(on_s)
        dv_ref[...] = jnp.zeros_like(dv_ref)
        for g in range(3):
            qs = qn_s if g == 0 else dl_s.at[g - 1, 0]
            ks = kn_s if g == 0 else dl_s.at[g - 1, 1]
            vs = v_ref if g == 0 else dl_s.at[g - 1, 2]
            gq = on_s.at[0] if g == 0 else tq_s
            gk = on_s.at[1] if g == 0 else tk_s
            gv = dv_ref if g == 0 else tv_s
            if g > 0:
                tk_s[...] = jnp.zeros_like(tk_s)
                tv_s[...] = jnp.zeros_like(tv_s)

            def tiles(i, _, g=g, qs=qs, ks=ks, vs=vs, gq=gq, gk=gk, gv=gv):
                where = [_dil_mask(i * DIL_B + b, g, bias_s) for b in range(DIL_B)]
                biases = jnp.stack([m for m, _, _ in where])
                _, vjp = jax.vjp(lambda q_, k_, v_: _dil_tiles(q_, k_, v_, biases),
                                 jnp.stack([qs[qr, :] for _, qr, _ in where]), jnp.stack([ks[kr, :] for _, _, kr in where]),
                                 jnp.stack([vs[kr, :] for _, _, kr in where]))
                dq, dkk, dvv = vjp((jnp.stack([od_s[g, qr, :] for _, qr, _ in where]),
                                    jnp.stack([ld_s[g, qr, :] for _, qr, _ in where])))
                for b, (_, qr, kr) in enumerate(where):
                    gq[qr, :] = dq[b]
                    gk[kr, :] += dkk[b]
                    gv[kr, :] += dvv[b]
                return 0

            lax.fori_loop(0, t // BLOCK // DIL_B, tiles, 0)
            if g > 0:
                d = DIL_D[g]
                ln = t // d
                for r in range(d):
                    nat, dil = pl.ds(r, ln, stride=d), pl.ds(r * ln, ln)
                    on_s[0, nat, :] += tq_s[dil, :]
                    on_s[1, nat, :] += tk_s[dil, :]
                    dv_ref[nat, :] += tv_s[dil, :]

        def prep(c, acc):
            rows = pl.ds(pl.multiple_of(c * ROWS, ROWS), ROWS)
            f = lambda x, w: _dil_prep(x, w, cs_ref[rows, :], sn_ref[rows, :], mean2, pm2)
            _, vq = jax.vjp(f, q_ref[rows, :], qw)
            _, vk = jax.vjp(f, k_ref[rows, :], kw)
            dq, dqw = vq(on_s[0, rows, :])
            dk, dkw = vk(on_s[1, rows, :])
            dq_ref[rows, :] = dq
            dk_ref[rows, :] = dk
            return acc[0] + dqw, acc[1] + dkw

        dqw, dkw = lax.fori_loop(0, t // ROWS, prep, (jnp.zeros((1, PAIR), F32), jnp.zeros((1, PAIR), F32)))
        dqw_ref[0] = dqw
        dkw_ref[0] = dkw

    qkv, vec, tab, mat, pair = _dil_specs(t)
    per = C_WIDTH // BLOCK
    wout = pl.BlockSpec((1, 1, PAIR), lambda b, p: (b * per + p, 0, 0))
    return pl.pallas_call(
        body, name="dil_bwd", grid=(nseq, per),
        in_specs=qkv + [vec, vec, tab, tab, mat, mat, pair],
        out_specs=[pair, pair, pair, wout, wout],
        out_shape=[jax.ShapeDtypeStruct((n, C_WIDTH), F32)] * 3 + [jax.ShapeDtypeStruct((nseq * per, 1, PAIR), F32)] * 2,
        scratch_shapes=_dil_scratch(t) + [pltpu.VMEM((t, PAIR), F32)] * 3,
        compiler_params=_cparams("parallel", "parallel"),
    )(proj, proj, proj, qw, kw, cs, sn, mean2, pm2, do)


N_CHIPS = 4
SUM_ROWS = 432
MESH_IDS = pl.DeviceIdType.MESH
ANY = pl.BlockSpec(memory_space=pl.ANY)


def plane_exchange(src, all_to_all):
    blk_shape = src.shape[1:] if all_to_all else src.shape

    def body(src_ref, out_ref, send_sems, recv_sems, local_sem):
        x, y, c = lax.axis_index("x"), lax.axis_index("y"), lax.axis_index("c")
        me = 2 * x + y
        mine = pltpu.make_async_copy(src_ref.at[me] if all_to_all else src_ref, out_ref.at[me], local_sem)
        mine.start()
        sends = []
        for k in (1, 2, 3):
            px = 1 - x if k & 2 else x
            py = 1 - y if k & 1 else y
            peer = 2 * px + py
            cp = pltpu.make_async_remote_copy(
                src_ref=src_ref.at[peer] if all_to_all else src_ref, dst_ref=out_ref.at[me],
                send_sem=send_sems.at[k - 1], recv_sem=recv_sems.at[k - 1],
                device_id=(px, py, c), device_id_type=MESH_IDS)
            cp.start()
            sends.append((cp, peer, (px, py, c)))
        for k, (cp, peer, dev) in enumerate(sends):
            pltpu.make_async_remote_copy(
                src_ref=out_ref.at[me], dst_ref=out_ref.at[peer],
                send_sem=send_sems.at[k], recv_sem=recv_sems.at[k],
                device_id=dev, device_id_type=MESH_IDS).wait_recv()
        for cp, _, _ in sends:
            cp.wait_send()
        mine.wait()

    return pl.pallas_call(
        body, name="plane_all_to_all" if all_to_all else "plane_all_gather",
        in_specs=[ANY], out_specs=ANY,
        out_shape=jax.ShapeDtypeStruct((N_CHIPS,) + blk_shape, src.dtype),
        scratch_shapes=[pltpu.SemaphoreType.DMA((3,)), pltpu.SemaphoreType.DMA((3,)), pltpu.SemaphoreType.DMA],
    )(src)


def sibling_swap(src):
    def body(src_ref, out_ref, send_sem, recv_sem):
        x, y, c = lax.axis_index("x"), lax.axis_index("y"), lax.axis_index("c")
        cp = pltpu.make_async_remote_copy(src_ref=src_ref, dst_ref=out_ref, send_sem=send_sem, recv_sem=recv_sem,
                                          device_id=(x, y, 1 - c), device_id_type=MESH_IDS)
        cp.start()
        cp.wait()

    return pl.pallas_call(
        body, name="sibling_swap", in_specs=[ANY], out_specs=ANY,
        out_shape=jax.ShapeDtypeStruct(src.shape, src.dtype),
        scratch_shapes=[pltpu.SemaphoreType.DMA, pltpu.SemaphoreType.DMA],
    )(src)


def sum4(a):
    _, r, c = a.shape
    tr = SUM_ROWS

    def body(a_ref, o_ref):
        p = [a_ref[i].astype(F32) for i in range(N_CHIPS)]
        o_ref[...] = (p[0] + p[1]) + (p[2] + p[3])

    return pl.pallas_call(
        body, name="sum4", grid=(r // tr,),
        in_specs=[pl.BlockSpec((N_CHIPS, tr, c), lambda i: (0, i, 0))],
        out_specs=pl.BlockSpec((tr, c), lambda i: (i, 0)),
        out_shape=jax.ShapeDtypeStruct((r, c), F32),
        compiler_params=_cparams("parallel"),
    )(a)


def add2(a, b, dtype):
    r, c = a.shape
    tr = SUM_ROWS

    def body(a_ref, b_ref, o_ref):
        o_ref[...] = (a_ref[...] + b_ref[...]).astype(dtype)

    blk = pl.BlockSpec((tr, c), lambda i: (i, 0))
    return pl.pallas_call(
        body, name="add2", grid=(r // tr,), in_specs=[blk, blk], out_specs=blk,
        out_shape=jax.ShapeDtypeStruct((r, c), dtype), compiler_params=_cparams("parallel"),
    )(a, b)


PACK_COLS = 1152
PACK_ROWS = 2592
ROW_TILE = 16


def _pack(parts):
    blocks = []
    for p in parts:
        p2 = p.reshape(-1, p.shape[-1])
        blocks.append(jnp.pad(p2, ((0, -p2.shape[0] % ROW_TILE), (0, PACK_COLS - p2.shape[1]))))
    rows = sum(b.shape[0] for b in blocks)
    blocks.append(jnp.zeros((PACK_ROWS - rows, PACK_COLS), blocks[0].dtype))
    return jnp.concatenate(blocks)


def _unpack(buf, shapes):
    out, at = [], 0
    for s in shapes:
        rows = math.prod(s[:-1])
        out.append(buf[at:at + rows, :s[-1]].reshape(s))
        at += rows + (-rows % ROW_TILE)
    return out


def _pack_small(g):
    blk = jnp.zeros((ROW_TILE, PACK_COLS), F32)
    for i, k in enumerate(SMALL):
        blk = blk.at[2 * i:2 * i + 2, :g[k].shape[1]].set(g[k])
    return blk


def _unpack_small(blk, shapes):
    return [blk[2 * i:2 * i + 2, :s[1]] for i, s in enumerate(shapes)]


def _layer_fwd(x, p, nseq, tabs):
    proj, hdn = inproj_fwd(x, p["norm_w"][None], p["w_in"])
    ya = conv_fwd(proj, p["conv_w"], nseq)
    oa, states = gdn_fwd(ya, proj, p["a_log"].reshape(N_HEADS_A, 1, 1), p["dt_bias"].reshape(N_HEADS_A, 1, 1),
                         p["gdn_norm_w"][None], nseq)
    ob, carries = sb_fwd(proj, nseq)
    oc = dil_fwd(proj, jnp.tile(p["q_norm_w"], 2)[None], jnp.tile(p["k_norm_w"], 2)[None], *tabs, nseq)
    y, mixed = outproj_fwd(x, oa, ob, oc, proj, p["w_out"])
    return y, dict(x=x, hdn=hdn, proj=proj, ya=ya, states=states, carries=carries, oa=oa, ob=ob, oc=oc, mixed=mixed)


def _layer_bwd(dy, p, res, nseq, tabs):
    proj = res["proj"]
    g = {}
    g["w_out"] = mat_tn(res["mixed"], [dy])[0]
    doa, dob, doc, dza, dzb, dzc = outproj_bwd(dy, res["oa"], res["ob"], res["oc"], proj, p["w_out"])
    dqc, dkc, dvc, dqw, dkw = dil_bwd(proj, jnp.tile(p["q_norm_w"], 2)[None], jnp.tile(p["k_norm_w"], 2)[None], *tabs, doc,
                                      nseq)
    g["q_norm_w"], g["k_norm_w"] = dqw.reshape(-1, HEAD_DIM).sum(0), dkw.reshape(-1, HEAD_DIM).sum(0)
    dqb, dkb, dvb = sb_bwd(proj, res["carries"], dob, nseq)
    dya, dba, dalog, ddtb, dnw = gdn_bwd(res["ya"], proj, p["a_log"].reshape(N_HEADS_A, 1, 1),
                                         p["dt_bias"].reshape(N_HEADS_A, 1, 1), p["gdn_norm_w"][None], res["states"], doa, nseq)
    g["a_log"], g["dt_bias"], g["gdn_norm_w"] = dalog.sum(0).reshape(-1), ddtb.sum(0).reshape(-1), dnw.sum((0, 1))
    dqkv, dcw = conv_bwd(proj, p["conv_w"], dya, nseq)
    g["conv_w"] = dcw.sum(0)
    slabs = [dqkv, dza, dqc, dkc, dvc, dzc, dqb, dkb, dvb, dzb, dba]
    hdn = res["hdn"]
    g["w_in"] = jnp.concatenate(mat_tn(hdn, slabs[:2]) + mat_tn(hdn, slabs[2:6]) + mat_tn(hdn, slabs[6:]), axis=1)
    dx, dnw_tiles = inproj_bwd(slabs, p["w_in"], res["x"], p["norm_w"][None], dy)
    g["norm_w"] = dnw_tiles.sum((0, 1))
    return dx, g


SMALL = ("norm_w", "a_log", "dt_bias", "gdn_norm_w", "q_norm_w", "k_norm_w")


def _local_step(x, target, full):
    nseq, t, d = x.shape
    tabs = _rope_tables(t)
    h = x.reshape(nseq * t, d)
    saved = []
    for l in range(DEPTH):
        p = {k: v[l] for k, v in full.items()}
        h, res = _layer_fwd(h, p, nseq, tabs)
        saved.append((p, res))
    dy, parts = loss_fwd_bwd(h, target.reshape(nseq * t, d))
    loss = parts[:, 0, 0].sum()
    grads = [None] * DEPTH
    for l in reversed(range(DEPTH)):
        p, res = saved[l]
        dy, grads[l] = _layer_bwd(dy, p, res, nseq, tabs)
    return loss, dy.reshape(nseq, t, d), {k: jnp.stack([g[k] for g in grads]) for k in grads[0]}


def _pad_cols(w):
    b0 = ORIG_A + ORIG_BA
    c0 = b0 + ORIG_B
    zeros = jnp.zeros(w.shape[:-1] + (BLOCK - ORIG_BA,), w.dtype)
    return jnp.concatenate([w[..., :ORIG_A], w[..., c0:], w[..., b0:c0], w[..., ORIG_A:b0], zeros], axis=-1)


def _unpad_cols(w):
    return jnp.concatenate([w[..., :COL_C], w[..., COL_BA:COL_BA + ORIG_BA], w[..., COL_B:COL_BA], w[..., COL_C:COL_B]],
                           axis=-1)


def kernel(x, norm_w, w_in, conv_w, a_log, dt_bias, gdn_norm_w, q_norm_w, k_norm_w, w_out, loss_target, m_norm_w, m_w_in, m_conv_w, m_a_log, m_dt_bias, m_gdn_norm_w, m_q_norm_w, m_k_norm_w, m_w_out, v_norm_w, v_w_in, v_conv_w, v_a_log, v_dt_bias, v_gdn_norm_w, v_q_norm_w, v_k_norm_w, v_w_out):
    weights = dict(norm_w=norm_w, w_in=w_in, conv_w=conv_w, a_log=a_log, dt_bias=dt_bias, gdn_norm_w=gdn_norm_w,
                   q_norm_w=q_norm_w, k_norm_w=k_norm_w, w_out=w_out)
    moms = dict(norm_w=m_norm_w, w_in=m_w_in, conv_w=m_conv_w, a_log=m_a_log, dt_bias=m_dt_bias,
                gdn_norm_w=m_gdn_norm_w, q_norm_w=m_q_norm_w, k_norm_w=m_k_norm_w, w_out=m_w_out)
    vars_ = dict(norm_w=v_norm_w, w_in=v_w_in, conv_w=v_conv_w, a_log=v_a_log, dt_bias=v_dt_bias,
                 gdn_norm_w=v_gdn_norm_w, q_norm_w=v_q_norm_w, k_norm_w=v_k_norm_w, w_out=v_w_out)
    names = list(weights)
    sharded = ("w_in", "w_out", "conv_w")
    shard_shapes = [weights[k].shape for k in sharded]

    c = lax.axis_index("c")
    half = PACK_ROWS // 2
    conv_bits = lax.bitcast_convert_type(conv_w, BF16).reshape(conv_w.shape[:2] + (2 * conv_w.shape[2],))
    shard = _pack([w_in.astype(BF16), w_out.astype(BF16), conv_bits])
    mine = plane_exchange(lax.dynamic_slice_in_dim(shard, c * half, half, axis=0), all_to_all=False)
    other = sibling_swap(mine)
    got = jnp.concatenate([jnp.where(c == 0, mine, other), jnp.where(c == 0, other, mine)], axis=1)
    per_chip = [_unpack(got[i], shard_shapes[:2] + [conv_bits.shape]) for i in range(N_CHIPS)]
    full = {k: weights[k] for k in SMALL}
    full["w_in"] = _pad_cols(jnp.concatenate([pc[0] for pc in per_chip], axis=2))
    full["w_out"] = jnp.concatenate([pc[1] for pc in per_chip], axis=1)
    full["conv_w"] = jnp.concatenate(
        [lax.bitcast_convert_type(pc[2].reshape(conv_w.shape + (2,)), F32) for pc in per_chip], axis=2)

    loss, grad_x, g = _local_step(x, loss_target, full)

    gw_in = _unpad_cols(g["w_in"])
    cols, rows = w_in.shape[2], w_out.shape[1]
    small = _pack_small(g)
    send = jnp.stack([_pack([gw_in[:, :, i * cols:(i + 1) * cols], g["w_out"][:, i * rows:(i + 1) * rows],
                             g["conv_w"][:, :, i * conv_w.shape[2]:(i + 1) * conv_w.shape[2]], small])
                      for i in range(N_CHIPS)])
    keep = lax.dynamic_slice_in_dim(send, c * half, half, axis=1)
    give = lax.dynamic_slice_in_dim(send, (1 - c) * half, half, axis=1)
    chip_sum = add2(keep.reshape(N_CHIPS * half, PACK_COLS), sibling_swap(give).reshape(N_CHIPS * half, PACK_COLS), BF16)
    mine = sum4(plane_exchange(chip_sum.reshape(N_CHIPS, half, PACK_COLS), all_to_all=True))
    other = sibling_swap(mine)
    total = jnp.concatenate([jnp.where(c == 0, mine, other), jnp.where(c == 0, other, mine)])
    reduced = _unpack(total, shard_shapes + [(ROW_TILE, PACK_COLS)])
    grads = dict(zip(sharded, reduced[:3]))
    grads.update(zip(SMALL, _unpack_small(reduced[3], [weights[k].shape for k in SMALL])))
    loss = lax.psum(loss, ("x", "y", "c"))

    def two_d(a):
        return a.reshape(-1, a.shape[-1])

    delta, new_m, new_v = {}, {}, {}
    for k in names:
        d_, m_, v_ = adamw(two_d(weights[k]), two_d(grads[k]), two_d(moms[k]), two_d(vars_[k]))
        delta[k], new_m[k], new_v[k] = (a.reshape(weights[k].shape) for a in (d_, m_, v_))
    return (loss, grad_x, *[grads[k] for k in names], *[delta[k] for k in names],
            *[new_m[k] for k in names], *[new_v[k] for k in names])
```

```python
import functools
import math

import jax
import jax.numpy as jnp
from jax import lax
from jax.experimental import pallas as pl
from jax.experimental.pallas import tpu as pltpu

F32 = jnp.float32
BF16 = jnp.bfloat16

D_MODEL = 1024
SEQ = 2048
DEPTH = 2
HEAD_DIM = 64
N_HEADS_A, N_HEADS_B, N_HEADS_C = 6, 4, 6
A_WIDTH, B_WIDTH, C_WIDTH = N_HEADS_A * HEAD_DIM, N_HEADS_B * HEAD_DIM, N_HEADS_C * HEAD_DIM
CONV_WIDTH = 4
GDN_CHUNK = 64
BLOCK = 128
ROPE_DIM = 16
ROPE_THETA = 500000.0
DILATED_PAIRS = ((128, 1), (512, 4), (2048, 16))
RMS_EPS = 1e-6
NEG = -1e30

NT = (((1,), (1,)), ((), ()))
NN = (((1,), (0,)), ((), ()))
TN = (((0,), (0,)), ((), ()))

VMEM_LIMIT = 48 * 1024 * 1024

ORIG_A = 4 * A_WIDTH
ORIG_BA = 2 * N_HEADS_A
ORIG_B = 4 * B_WIDTH
COL_AZ = 3 * A_WIDTH
COL_C = 4 * A_WIDTH
COL_B = COL_C + 4 * C_WIDTH
COL_BA = COL_B + 4 * B_WIDTH
P_COLS = COL_BA + BLOCK
TN_COLS = 384
INPROJ_COLS = P_COLS // 3
TM_ROWS = 512
ROWS = 256


def _mm(a, b, dims=NN):
    return lax.dot_general(a.astype(BF16), b.astype(BF16), dims, preferred_element_type=F32)


def _mm32(a, b, dims=NN):
    return lax.dot_general(a, b, dims, precision=lax.Precision.HIGH, preferred_element_type=F32)


def _cparams(*sem):
    return pltpu.CompilerParams(dimension_semantics=sem, vmem_limit_bytes=VMEM_LIMIT)


def _sigmoid(x):
    return 0.5 * (jnp.tanh(0.5 * x) + 1.0)


def _softplus(x):
    return jnp.maximum(x, 0.0) + jnp.log(1.0 + jnp.exp(-jnp.abs(x)))


def _rms(x, w):
    return x * lax.rsqrt(jnp.mean(x * x, axis=-1, keepdims=True) + RMS_EPS) * w


def _heads(a, n):
    return jnp.stack([a[:, h * HEAD_DIM:(h + 1) * HEAD_DIM] for h in range(n)])


def _unheads(a):
    return jnp.concatenate([a[h] for h in range(a.shape[0])], axis=1)


def _row_chunks(t):
    return [pl.ds(c * ROWS, ROWS) for c in range(t // ROWS)]


def inproj_fwd(x, nw, w):
    n, d = x.shape
    p = w.shape[1]

    def body(x_ref, nw_ref, w_ref, proj_ref, hdn_ref):
        @pl.when(pl.program_id(1) == 0)
        def _():
            hdn_ref[...] = _rms(x_ref[...], nw_ref[...]).astype(BF16)

        proj_ref[...] = jnp.dot(hdn_ref[...], w_ref[...], preferred_element_type=F32)

    return pl.pallas_call(
        body, name="inproj_fwd", grid=(n // TM_ROWS, p // INPROJ_COLS),
        in_specs=[pl.BlockSpec((TM_ROWS, d), lambda i, j: (i, 0)), pl.BlockSpec((1, d), lambda i, j: (0, 0)),
                  pl.BlockSpec((d, INPROJ_COLS), lambda i, j: (0, j))],
        out_specs=[pl.BlockSpec((TM_ROWS, INPROJ_COLS), lambda i, j: (i, j)), pl.BlockSpec((TM_ROWS, d), lambda i, j: (i, 0))],
        out_shape=[jax.ShapeDtypeStruct((n, p), F32), jax.ShapeDtypeStruct((n, d), BF16)],
        compiler_params=_cparams("parallel", "arbitrary"),
    )(x, nw, w)


def mat_tn(a, slabs):
    n, ka = a.shape
    ns = len(slabs)

    def body(*refs):
        a_ref, s_refs, o_refs = refs[0], refs[1:1 + ns], refs[1 + ns:]

        @pl.when(pl.program_id(0) == 0)
        def _():
            for o_ref in o_refs:
                o_ref[...] = jnp.zeros_like(o_ref)

        av = a_ref[...]
        for s_ref, o_ref in zip(s_refs, o_refs):
            o_ref[...] += lax.dot_general(av, s_ref[...].astype(BF16), TN, preferred_element_type=F32)

    return pl.pallas_call(
        body, name="mat_tn", grid=(n // TM_ROWS,),
        in_specs=[pl.BlockSpec((TM_ROWS, ka), lambda k: (k, 0))]
                 + [pl.BlockSpec((TM_ROWS, s.shape[1]), lambda k: (k, 0)) for s in slabs],
        out_specs=[pl.BlockSpec((ka, s.shape[1]), lambda k: (0, 0)) for s in slabs],
        out_shape=[jax.ShapeDtypeStruct((ka, s.shape[1]), F32) for s in slabs],
        compiler_params=_cparams("arbitrary"),
    )(a, *slabs)


def inproj_bwd(slabs, w, x, nw, dy):
    n, d = x.shape
    p = w.shape[1]
    tm = 256
    ns = len(slabs)

    def body(*refs):
        s_refs = refs[:ns]
        w_ref, x_ref, nw_ref, dy_ref, dx_ref, dnw_ref = refs[ns:]
        dh = jnp.zeros((tm, d), F32)
        at = 0
        for s_ref in s_refs:
            wd = s_ref.shape[1]
            dh = dh + lax.dot_general(s_ref[...].astype(BF16), w_ref[:, at:at + wd], NT, preferred_element_type=F32)
            at += wd
        _, vjp = jax.vjp(_rms, x_ref[...], nw_ref[...])
        dx, dnw = vjp(dh)
        dx_ref[...] = dx + dy_ref[...]
        dnw_ref[0] = dnw

    return pl.pallas_call(
        body, name="inproj_bwd", grid=(n // tm,),
        in_specs=[pl.BlockSpec((tm, s.shape[1]), lambda i: (i, 0)) for s in slabs]
                 + [pl.BlockSpec((d, p), lambda i: (0, 0)), pl.BlockSpec((tm, d), lambda i: (i, 0)),
                    pl.BlockSpec((1, d), lambda i: (0, 0)), pl.BlockSpec((tm, d), lambda i: (i, 0))],
        out_specs=[pl.BlockSpec((tm, d), lambda i: (i, 0)), pl.BlockSpec((1, 1, d), lambda i: (i, 0, 0))],
        out_shape=[jax.ShapeDtypeStruct((n, d), F32), jax.ShapeDtypeStruct((n // tm, 1, d), F32)],
        compiler_params=_cparams("parallel"),
    )(*slabs, w, x, nw, dy)


CONV_PAD = 8
CONV_ROWS = 256


def _conv_pre(pad_s, cw, c):
    xs = [pad_s[pl.ds(c * CONV_ROWS + CONV_PAD - (CONV_WIDTH - 1) + k, CONV_ROWS), :] for k in range(CONV_WIDTH)]
    pre = xs[0] * cw[0:1, :]
    for k in range(1, CONV_WIDTH):
        pre = pre + xs[k] * cw[k:k + 1, :]
    return pre, xs


def conv_fwd(proj, cw, nseq):
    n = proj.shape[0]
    t = n // nseq
    ch = cw.shape[1]

    def body(x_ref, cw_ref, y_ref, pad_s):
        pad_s[pl.ds(0, CONV_PAD), :] = jnp.zeros((CONV_PAD, TN_COLS), F32)
        pad_s[pl.ds(CONV_PAD, t), :] = x_ref[...]
        cwv = cw_ref[...]
        for c in range(t // CONV_ROWS):
            pre, _ = _conv_pre(pad_s, cwv, c)
            y_ref[pl.ds(c * CONV_ROWS, CONV_ROWS), :] = pre * _sigmoid(pre)

    return pl.pallas_call(
        body, name="conv_fwd", grid=(nseq, ch // TN_COLS),
        in_specs=[pl.BlockSpec((t, TN_COLS), lambda b, j: (b, j)), pl.BlockSpec((CONV_WIDTH, TN_COLS), lambda b, j: (0, j))],
        out_specs=pl.BlockSpec((t, TN_COLS), lambda b, j: (b, j)),
        out_shape=jax.ShapeDtypeStruct((n, ch), F32),
        scratch_shapes=[pltpu.VMEM((t + CONV_PAD, TN_COLS), F32)],
        compiler_params=_cparams("parallel", "parallel"),
    )(proj, cw)


def conv_bwd(proj, cw, dy, nseq):
    n = proj.shape[0]
    t = n // nseq
    ch = cw.shape[1]

    def body(x_ref, cw_ref, dy_ref, dx_ref, dcw_ref, pad_s, dpad_s):
        pad_s[pl.ds(0, CONV_PAD), :] = jnp.zeros((CONV_PAD, TN_COLS), F32)
        pad_s[pl.ds(CONV_PAD, t), :] = x_ref[...]
        dpad_s[pl.ds(t, CONV_PAD), :] = jnp.zeros((CONV_PAD, TN_COLS), F32)
        cwv = cw_ref[...]
        acc = [jnp.zeros((1, TN_COLS), F32)] * CONV_WIDTH
        for c in range(t // CONV_ROWS):
            pre, xs = _conv_pre(pad_s, cwv, c)
            sg = _sigmoid(pre)
            dpre = dy_ref[pl.ds(c * CONV_ROWS, CONV_ROWS), :] * (sg * (1.0 + pre * (1.0 - sg)))
            dpad_s[pl.ds(c * CONV_ROWS, CONV_ROWS), :] = dpre
            acc = [acc[k] + jnp.sum(dpre * xs[k], axis=0, keepdims=True) for k in range(CONV_WIDTH)]
        for k in range(CONV_WIDTH):
            dcw_ref[0, pl.ds(k, 1), :] = acc[k]
        for c in range(t // CONV_ROWS):
            dx = dpad_s[pl.ds(c * CONV_ROWS + CONV_WIDTH - 1, CONV_ROWS), :] * cwv[0:1, :]
            for k in range(1, CONV_WIDTH):
                dx = dx + dpad_s[pl.ds(c * CONV_ROWS + CONV_WIDTH - 1 - k, CONV_ROWS), :] * cwv[k:k + 1, :]
            dx_ref[pl.ds(c * CONV_ROWS, CONV_ROWS), :] = dx

    blk = pl.BlockSpec((t, TN_COLS), lambda b, j: (b, j))
    return pl.pallas_call(
        body, name="conv_bwd", grid=(nseq, ch // TN_COLS),
        in_specs=[blk, pl.BlockSpec((CONV_WIDTH, TN_COLS), lambda b, j: (0, j)), blk],
        out_specs=[blk, pl.BlockSpec((1, CONV_WIDTH, TN_COLS), lambda b, j: (b, 0, j))],
        out_shape=[jax.ShapeDtypeStruct((n, ch), F32), jax.ShapeDtypeStruct((nseq, CONV_WIDTH, ch), F32)],
        scratch_shapes=[pltpu.VMEM((t + CONV_PAD, TN_COLS), F32)] * 2,
        compiler_params=_cparams("parallel", "parallel"),
    )(proj, cw, dy)


def _gate_specs(d):
    wide = pl.BlockSpec((TM_ROWS, d), lambda i: (i, 0))
    oa = pl.BlockSpec((TM_ROWS, A_WIDTH), lambda i: (i, 0))
    ob = pl.BlockSpec((TM_ROWS, B_WIDTH), lambda i: (i, 0))
    oc = pl.BlockSpec((TM_ROWS, C_WIDTH), lambda i: (i, 0))
    za = pl.BlockSpec((TM_ROWS, A_WIDTH), lambda i: (i, COL_AZ // A_WIDTH))
    zb = pl.BlockSpec((TM_ROWS, B_WIDTH), lambda i: (i, (COL_B + 3 * B_WIDTH) // B_WIDTH))
    zc = pl.BlockSpec((TM_ROWS, C_WIDTH), lambda i: (i, (COL_C + 3 * C_WIDTH) // C_WIDTH))
    return wide, oa, ob, oc, za, zb, zc


BRANCH_COLS = ((0, A_WIDTH), (A_WIDTH, A_WIDTH + B_WIDTH), (A_WIDTH + B_WIDTH, D_MODEL))


def outproj_fwd(x, oa, ob, oc, proj, w):
    n, d = x.shape

    def body(x_ref, oa_ref, ob_ref, oc_ref, za_ref, zb_ref, zc_ref, w_ref, y_ref, m_ref):
        for (lo, hi), o_ref, z_ref in zip(BRANCH_COLS, (oa_ref, ob_ref, oc_ref), (za_ref, zb_ref, zc_ref)):
            zv = z_ref[...]
            m_ref[:, lo:hi] = (o_ref[...] * (zv * _sigmoid(zv))).astype(BF16)
        y_ref[...] = x_ref[...] + jnp.dot(m_ref[...], w_ref[...], preferred_element_type=F32)

    wide, sa, sb, sc, za, zb, zc = _gate_specs(d)
    return pl.pallas_call(
        body, name="outproj_fwd", grid=(n // TM_ROWS,),
        in_specs=[wide, sa, sb, sc, za, zb, zc, pl.BlockSpec((d, d), lambda i: (0, 0))],
        out_specs=[wide, wide],
        out_shape=[jax.ShapeDtypeStruct((n, d), F32), jax.ShapeDtypeStruct((n, d), BF16)],
        compiler_params=_cparams("parallel"),
    )(x, oa, ob, oc, proj, proj, proj, w)


def outproj_bwd(dy, oa, ob, oc, proj, w):
    n, d = dy.shape

    def body(dy_ref, oa_ref, ob_ref, oc_ref, za_ref, zb_ref, zc_ref, w_ref, doa_ref, dob_ref, doc_ref, dza_ref, dzb_ref, dzc_ref):
        dm = lax.dot_general(dy_ref[...].astype(BF16), w_ref[...], NT, preferred_element_type=F32)
        for (lo, hi), o_ref, z_ref, do_ref, dz_ref in zip(BRANCH_COLS, (oa_ref, ob_ref, oc_ref), (za_ref, zb_ref, zc_ref),
                                                          (doa_ref, dob_ref, doc_ref), (dza_ref, dzb_ref, dzc_ref)):
            zv = z_ref[...]
            sg = _sigmoid(zv)
            dmv = dm[:, lo:hi]
            do_ref[...] = dmv * (zv * sg)
            dz_ref[...] = dmv * o_ref[...] * (sg * (1.0 + zv * (1.0 - sg)))

    wide, sa, sb, sc, za, zb, zc = _gate_specs(d)
    sd = jax.ShapeDtypeStruct
    outs = [sd((n, A_WIDTH), F32), sd((n, B_WIDTH), F32), sd((n, C_WIDTH), F32)]
    return pl.pallas_call(
        body, name="outproj_bwd", grid=(n // TM_ROWS,),
        in_specs=[wide, sa, sb, sc, za, zb, zc, pl.BlockSpec((d, d), lambda i: (0, 0))],
        out_specs=[sa, sb, sc, sa, sb, sc],
        out_shape=outs + outs,
        compiler_params=_cparams("parallel"),
    )(dy, oa, ob, oc, proj, proj, proj, w)


def loss_fwd_bwd(y, target):
    n, d = y.shape

    def body(y_ref, t_ref, dy_ref, part_ref):
        e = y_ref[...] - t_ref[...]
        dy_ref[...] = e * (1.0 / d)
        part_ref[...] = jnp.zeros_like(part_ref) + 0.5 * jnp.sum(e * e) * (1.0 / d)

    blk = pl.BlockSpec((TM_ROWS, d), lambda i: (i, 0))
    return pl.pallas_call(
        body, name="loss", grid=(n // TM_ROWS,),
        in_specs=[blk, blk],
        out_specs=[blk, pl.BlockSpec((1, 8, BLOCK), lambda i: (i, 0, 0))],
        out_shape=[jax.ShapeDtypeStruct((n, d), F32), jax.ShapeDtypeStruct((n // TM_ROWS, 8, BLOCK), F32)],
        compiler_params=_cparams("parallel"),
    )(y, target)


ADAM_LR, ADAM_B1, ADAM_B2, ADAM_EPS, ADAM_WD, ADAM_STEP = 0.001, 0.9, 0.999, 1e-08, 0.01, 10


def adamw(w, g, m, v):
    r, c = w.shape
    tr = r if r <= 256 else 256

    def body(w_ref, g_ref, m_ref, v_ref, d_ref, nm_ref, nv_ref):
        gv = g_ref[...]
        nm = ADAM_B1 * m_ref[...] + (1.0 - ADAM_B1) * gv
        nv = ADAM_B2 * v_ref[...] + (1.0 - ADAM_B2) * (gv * gv)
        m_hat = nm / (1.0 - ADAM_B1 ** ADAM_STEP)
        v_hat = nv / (1.0 - ADAM_B2 ** ADAM_STEP)
        d_ref[...] = -ADAM_LR * (m_hat / (jnp.sqrt(v_hat) + ADAM_EPS) + ADAM_WD * w_ref[...])
        nm_ref[...] = nm
        nv_ref[...] = nv

    blk = pl.BlockSpec((tr, c), lambda i: (i, 0))
    return pl.pallas_call(
        body, name="adamw", grid=(r // tr,),
        in_specs=[blk] * 4, out_specs=[blk] * 3,
        out_shape=[jax.ShapeDtypeStruct((r, c), F32)] * 3,
        compiler_params=_cparams("parallel"),
    )(w, g, m, v)


SB_G = N_HEADS_B


def _sb_tile(q, k, v, carry, tri, diag):
    z = _mm(q * (HEAD_DIM ** -0.5), k, NT)
    sp = jnp.log(1.0 + jnp.exp(-jnp.abs(z)))
    ls_pos = jnp.minimum(z, 0.0) - sp
    ls_neg = jnp.minimum(-z, 0.0) - sp
    if diag:
        earlier = lax.broadcasted_iota(jnp.int32, z.shape, 1) < lax.broadcasted_iota(jnp.int32, z.shape, 0)
        log_keep = jnp.where(earlier, ls_neg, 0.0)
    else:
        log_keep = ls_neg
    hi = log_keep.astype(BF16)
    lo = lax.stop_gradient(log_keep - hi.astype(F32)).astype(BF16)
    within = lax.dot_general(jnp.concatenate([hi, lo], axis=1), tri, NN, preferred_element_type=F32)
    arg = ls_pos + within + carry
    wts = jnp.where(earlier, jnp.exp(jnp.where(earlier, arg, 0.0)), 0.0) if diag else jnp.exp(arg)
    return _mm(wts, v), jnp.sum(log_keep, axis=1, keepdims=True)


_sb_tiles_diag = jax.vmap(functools.partial(_sb_tile, diag=True), in_axes=(0, 0, 0, 0, None))
_sb_tiles_off = jax.vmap(functools.partial(_sb_tile, diag=False), in_axes=(0, 0, 0, 0, None))


def _sb_tri():
    r = lax.broadcasted_iota(jnp.int32, (2 * BLOCK, BLOCK), 0) % BLOCK
    c = lax.broadcasted_iota(jnp.int32, (2 * BLOCK, BLOCK), 1)
    return jnp.where(r > c, 1.0, 0.0).astype(BF16)


SB_SEQ = 2


def _sb_specs(t, nq):
    cb = COL_B // B_WIDTH
    sq = SB_SEQ
    q = pl.BlockSpec((sq, BLOCK, B_WIDTH), lambda b, i: (b, i, cb))
    k = pl.BlockSpec((sq, t, B_WIDTH), lambda b, i: (b, 0, cb + 1))
    v = pl.BlockSpec((sq, t, B_WIDTH), lambda b, i: (b, 0, cb + 2))
    blk = pl.BlockSpec((sq, BLOCK, B_WIDTH), lambda b, i: (b, i, 0))
    full = pl.BlockSpec((sq, t, B_WIDTH), lambda b, i: (b, 0, 0))
    carry = pl.BlockSpec((sq, 1, nq, BLOCK, SB_G), lambda b, i: (b, i, 0, 0, 0))
    return q, k, v, blk, full, carry


def _sb_heads(ref, rows):
    return jnp.concatenate([_heads(ref[b, rows, :], SB_G) for b in range(SB_SEQ)])


def _sb_unheads(a):
    return [_unheads(a[b * SB_G:(b + 1) * SB_G]) for b in range(SB_SEQ)]


def sb_fwd(proj, nseq):
    n = proj.shape[0]
    t = n // nseq
    nq = t // BLOCK
    g, sq = SB_G, SB_SEQ
    everything = pl.ds(0, BLOCK)

    def body(q_ref, k_ref, v_ref, o_ref, carry_ref):
        i = pl.program_id(1)
        tri = _sb_tri()
        qv = _sb_heads(q_ref, everything)

        def tile(j, c, fn):
            rows = pl.ds(pl.multiple_of(j * BLOCK, BLOCK), BLOCK)
            for b in range(sq):
                carry_ref[b, 0, j] = jnp.concatenate([c[b * g + h] for h in range(g)], axis=1)
            return fn(qv, _sb_heads(k_ref, rows), _sb_heads(v_ref, rows), c, tri)

        def step(it, st):
            o_acc, c = st
            o, tot = tile(i - 1 - it, c, _sb_tiles_off)
            return o_acc + o, c + tot

        o_acc, _ = lax.fori_loop(0, i, step, tile(i, jnp.zeros((sq * g, BLOCK, 1), F32), _sb_tiles_diag))
        for b, o in enumerate(_sb_unheads(o_acc)):
            o_ref[b] = o

    q, k, v, blk, _, carry = _sb_specs(t, nq)
    proj3 = proj.reshape(nseq, t, -1)
    o, carries = pl.pallas_call(
        body, name="sb_fwd", grid=(nseq // sq, nq),
        in_specs=[q, k, v],
        out_specs=[blk, carry],
        out_shape=[jax.ShapeDtypeStruct((nseq, t, B_WIDTH), F32),
                   jax.ShapeDtypeStruct((nseq, nq, nq, BLOCK, g), F32)],
        compiler_params=_cparams("parallel", "arbitrary"),
    )(proj3, proj3, proj3)
    return o.reshape(n, B_WIDTH), carries


def sb_bwd(proj, carries, do, nseq):
    n = proj.shape[0]
    t = n // nseq
    nq = t // BLOCK
    g, sq = SB_G, SB_SEQ
    everything = pl.ds(0, BLOCK)

    def body(q_ref, k_ref, v_ref, carry_ref, do_ref, dq_ref, dk_ref, dv_ref):
        i = pl.program_id(1)

        @pl.when(i == 0)
        def _():
            dk_ref[...] = jnp.zeros_like(dk_ref)
            dv_ref[...] = jnp.zeros_like(dv_ref)

        tri = _sb_tri()
        qv = _sb_heads(q_ref, everything)
        dov = _sb_heads(do_ref, everything)

        def tile(j, st, fn):
            dq_acc, dc = st
            rows = pl.ds(pl.multiple_of(j * BLOCK, BLOCK), BLOCK)
            cj = [carry_ref[b, 0, j] for b in range(sq)]
            _, vjp = jax.vjp(lambda q_, k_, v_, c_: fn(q_, k_, v_, c_, tri), qv, _sb_heads(k_ref, rows), _sb_heads(v_ref, rows),
                             jnp.stack([cj[b][:, h:h + 1] for b in range(sq) for h in range(g)]))
            dq, dk, dv, dcj = vjp((dov, dc))
            for b, (dkb, dvb) in enumerate(zip(_sb_unheads(dk), _sb_unheads(dv))):
                dk_ref[b, rows, :] += dkb
                dv_ref[b, rows, :] += dvb
            return dq_acc + dq, dc + dcj

        st = lax.fori_loop(0, i, lambda j, st: tile(j, st, _sb_tiles_off),
                           (jnp.zeros((sq * g, BLOCK, HEAD_DIM), F32), jnp.zeros((sq * g, BLOCK, 1), F32)))
        dq_acc, _ = tile(i, st, _sb_tiles_diag)
        for b, dq in enumerate(_sb_unheads(dq_acc)):
            dq_ref[b] = dq

    q, k, v, blk, full, carry = _sb_specs(t, nq)
    proj3, do3 = proj.reshape(nseq, t, -1), do.reshape(nseq, t, -1)
    grads = pl.pallas_call(
        body, name="sb_bwd", grid=(nseq // sq, nq),
        in_specs=[q, k, v, carry, blk],
        out_specs=[blk, full, full],
        out_shape=[jax.ShapeDtypeStruct((nseq, t, B_WIDTH), F32)] * 3,
        compiler_params=_cparams("parallel", "arbitrary"),
    )(proj3, proj3, proj3, carries, do3)
    return [a.reshape(n, B_WIDTH) for a in grads]


def _unit_lower_inverse(a):
    n = a.shape[0]
    eye = jnp.where(lax.broadcasted_iota(jnp.int32, (n, n), 0) == lax.broadcasted_iota(jnp.int32, (n, n), 1), 1.0, 0.0)
    tmat = eye.astype(F32) - a
    p = a
    for _ in range(5):
        p = _mm32(p, p)
        tmat = tmat + _mm32(tmat, p)
    return tmat


@jax.custom_vjp
def _known_inverse(a, tmat):
    return tmat


def _known_inverse_fwd(a, tmat):
    return tmat, tmat


def _known_inverse_bwd(tmat, g):
    return -_mm32(_mm32(tmat, g, TN), tmat, NT), jnp.zeros_like(tmat)


_known_inverse.defvjp(_known_inverse_fwd, _known_inverse_bwd)


def _gdn_chunk(q, k, v, al_c, al_r, br_c, alog, dtb, nw, s, tmat_in):
    c = GDN_CHUNK
    ri = lax.broadcasted_iota(jnp.int32, (c, c), 0)
    ci = lax.broadcasted_iota(jnp.int32, (c, c), 1)
    incl, strict = ri >= ci, ri > ci
    rate = -jnp.exp(alog)
    g_c = rate * _softplus(al_c + dtb)
    g_r = rate * _softplus(al_r + dtb)
    beta = _sigmoid(br_c)
    gc_c = jnp.sum(jnp.where(incl, g_r, 0.0), axis=1, keepdims=True)
    gc_r = jnp.sum(jnp.where(ri <= ci, g_c, 0.0), axis=0, keepdims=True)
    gl = jnp.sum(g_r, axis=1, keepdims=True)
    decay = jnp.where(incl, jnp.exp(jnp.where(incl, gc_c - gc_r, 0.0)), 0.0)
    qn = q * lax.rsqrt(jnp.sum(q * q, axis=-1, keepdims=True) + RMS_EPS) * (HEAD_DIM ** -0.5)
    kn = k * lax.rsqrt(jnp.sum(k * k, axis=-1, keepdims=True) + RMS_EPS)
    kb = kn * beta
    a = jnp.where(strict, _mm(kb, kn, NT) * decay, 0.0)
    tmat = _unit_lower_inverse(a) if tmat_in is None else _known_inverse(a, tmat_in)
    u = _mm(tmat, v * beta)
    w = _mm(tmat, kb * jnp.exp(gc_c))
    qk = _mm(qn, kn, NT) * decay
    v_new = u - _mm(w, s)
    o = _mm(qn * jnp.exp(gc_c), s) + _mm(qk, v_new)
    s_new = s * jnp.exp(gl) + _mm(kn * jnp.exp(gl - gc_c), v_new, TN)
    o = o * lax.rsqrt(jnp.mean(o * o, axis=-1, keepdims=True) + RMS_EPS) * nw
    return o, s_new, tmat


_gdn_chunks_fwd = jax.vmap(functools.partial(_gdn_chunk, tmat_in=None), in_axes=(0, 0, 0, 0, 0, 0, 0, 0, None, 0))
_gdn_chunks_bwd = jax.vmap(_gdn_chunk, in_axes=(0, 0, 0, 0, 0, 0, 0, 0, None, 0, 0))

GDN_TB = 256
GDN_SEQ_FWD = 2
GDN_SEQ_BWD = 2


def _gdn_block(q3, k3, v3, ba, alog, dtb, nw, s, tm=None):
    nh = N_HEADS_A
    ns = q3.shape[0]
    bat = [ba[b].T for b in range(ns)]
    br_c = jnp.stack([ba[b][:, h:h + 1] for b in range(ns) for h in range(nh)])
    al_c = jnp.stack([ba[b][:, nh + h:nh + h + 1] for b in range(ns) for h in range(nh)])
    al_r = jnp.stack([bat[b][nh + h:nh + h + 1, :] for b in range(ns) for h in range(nh)])
    heads = lambda a: jnp.concatenate([_heads(a[b], nh) for b in range(ns)])
    args = (heads(q3), heads(k3), heads(v3), al_c, al_r, br_c, jnp.concatenate([alog] * ns), jnp.concatenate([dtb] * ns), nw, s)
    o, s_new, tmat = _gdn_chunks_fwd(*args) if tm is None else _gdn_chunks_bwd(*args, tm)
    o3 = jnp.stack([_unheads(o[b * nh:(b + 1) * nh]) for b in range(ns)])
    return (o3, s_new, tmat) if tm is None else (o3, s_new)


def _gdn_specs(nt, sq, rev):
    tpos = (lambda i: nt - 1 - i) if rev else (lambda i: i)
    ncb = GDN_TB // GDN_CHUNK
    qkv = [pl.BlockSpec((sq, GDN_TB, A_WIDTH), lambda b, i, j=j: (b, tpos(i), j)) for j in range(3)]
    ba = pl.BlockSpec((sq, GDN_TB, BLOCK), lambda b, i: (b, tpos(i), COL_BA // BLOCK))
    one = pl.BlockSpec((N_HEADS_A, 1, 1), lambda b, i: (0, 0, 0))
    vec = pl.BlockSpec((1, HEAD_DIM), lambda b, i: (0, 0))
    st = pl.BlockSpec((sq, ncb, N_HEADS_A, HEAD_DIM, HEAD_DIM), lambda b, i: (b, tpos(i), 0, 0, 0))
    oa = pl.BlockSpec((sq, GDN_TB, A_WIDTH), lambda b, i: (b, tpos(i), 0))
    return qkv, ba, one, vec, st, oa, tpos


def gdn_fwd(ya, proj, alog, dtb, nw, nseq):
    n = ya.shape[0]
    t = n // nseq
    nc, nt, ncb = t // GDN_CHUNK, t // GDN_TB, GDN_TB // GDN_CHUNK
    sq = GDN_SEQ_FWD
    nh = N_HEADS_A

    def body(q_ref, k_ref, v_ref, ba_ref, alog_ref, dtb_ref, nw_ref, o_ref, st_ref, tm_ref, s_s):
        @pl.when(pl.program_id(1) == 0)
        def _():
            s_s[...] = jnp.zeros_like(s_s)

        def step(c, s):
            rows = pl.ds(pl.multiple_of(c * GDN_CHUNK, GDN_CHUNK), GDN_CHUNK)
            o, s_new, tmat = _gdn_block(q_ref[:, rows, :], k_ref[:, rows, :], v_ref[:, rows, :], ba_ref[:, rows, :],
                                        alog_ref[...], dtb_ref[...], nw_ref[...], s)
            for b in range(sq):
                st_ref[b, c] = s[b * nh:(b + 1) * nh]
                tm_ref[b, c] = tmat[b * nh:(b + 1) * nh]
            o_ref[:, rows, :] = o
            return s_new

        s_s[...] = lax.fori_loop(0, ncb, step, s_s[...])

    qkv, ba, one, vec, st, oa, _ = _gdn_specs(nt, sq, False)
    ya3, proj3 = ya.reshape(nseq, t, -1), proj.reshape(nseq, t, -1)
    per_chunk = jax.ShapeDtypeStruct((nseq, nc, nh, HEAD_DIM, HEAD_DIM), F32)
    o, states, inverses = pl.pallas_call(
        body, name="gdn_fwd", grid=(nseq // sq, nt),
        in_specs=qkv + [ba, one, one, vec],
        out_specs=[oa, st, st],
        out_shape=[jax.ShapeDtypeStruct((nseq, t, A_WIDTH), F32), per_chunk, per_chunk],
        scratch_shapes=[pltpu.VMEM((sq * nh, HEAD_DIM, HEAD_DIM), F32)],
        compiler_params=_cparams("parallel", "arbitrary"),
    )(ya3, ya3, ya3, proj3, alog, dtb, nw)
    return o.reshape(n, A_WIDTH), states, inverses


def gdn_bwd(ya, proj, alog, dtb, nw, states, inverses, do, nseq):
    n = ya.shape[0]
    t = n // nseq
    nt, ncb = t // GDN_TB, GDN_TB // GDN_CHUNK
    nh = N_HEADS_A
    sq = GDN_SEQ_BWD

    def body(q_ref, k_ref, v_ref, ba_ref, alog_ref, dtb_ref, nw_ref, st_ref, tm_ref, do_ref,
             dya_ref, dba_ref, dalog_ref, ddtb_ref, dnw_ref, ds_s):
        @pl.when(pl.program_id(1) == 0)
        def _():
            ds_s[...] = jnp.zeros_like(ds_s)
            dalog_ref[...] = jnp.zeros_like(dalog_ref)
            ddtb_ref[...] = jnp.zeros_like(ddtb_ref)
            dnw_ref[...] = jnp.zeros_like(dnw_ref)

        def step(it, carry):
            ds, dalog, ddtb, dnw = carry
            c = ncb - 1 - it
            rows = pl.ds(pl.multiple_of(c * GDN_CHUNK, GDN_CHUNK), GDN_CHUNK)
            s_in = jnp.concatenate([st_ref[b, c] for b in range(sq)])
            tm_in = jnp.concatenate([tm_ref[b, c] for b in range(sq)])
            _, vjp = jax.vjp(functools.partial(_gdn_block, tm=tm_in), q_ref[:, rows, :], k_ref[:, rows, :], v_ref[:, rows, :],
                             ba_ref[:, rows, :], alog_ref[...], dtb_ref[...], nw_ref[...], s_in)
            dq, dk, dv, dba, da, dd, dn, ds = vjp((do_ref[:, rows, :], ds))
            dya_ref[:, rows, 0:A_WIDTH] = dq
            dya_ref[:, rows, A_WIDTH:2 * A_WIDTH] = dk
            dya_ref[:, rows, 2 * A_WIDTH:3 * A_WIDTH] = dv
            dba_ref[:, rows, :] = dba
            return ds, dalog + da, ddtb + dd, dnw + dn

        z11 = jnp.zeros((nh, 1, 1), F32)
        ds, dalog, ddtb, dnw = lax.fori_loop(0, ncb, step, (ds_s[...], z11, z11, jnp.zeros((1, HEAD_DIM), F32)))
        ds_s[...] = ds
        dalog_ref[0] += dalog
        ddtb_ref[0] += ddtb
        dnw_ref[0] += dnw

    qkv, ba, one, vec, st, oa, tpos = _gdn_specs(nt, sq, True)
    per_grp = pl.BlockSpec((1, nh, 1, 1), lambda b, i: (b, 0, 0, 0))
    sd = jax.ShapeDtypeStruct
    ya3, proj3, do3 = ya.reshape(nseq, t, -1), proj.reshape(nseq, t, -1), do.reshape(nseq, t, -1)
    dya, dba, dalog, ddtb, dnw = pl.pallas_call(
        body, name="gdn_bwd", grid=(nseq // sq, nt),
        in_specs=qkv + [ba, one, one, vec, st, st, oa],
        out_specs=[pl.BlockSpec((sq, GDN_TB, 3 * A_WIDTH), lambda b, i: (b, tpos(i), 0)),
                   pl.BlockSpec((sq, GDN_TB, BLOCK), lambda b, i: (b, tpos(i), 0)),
                   per_grp, per_grp, pl.BlockSpec((1, 1, HEAD_DIM), lambda b, i: (b, 0, 0))],
        out_shape=[sd((nseq, t, 3 * A_WIDTH), F32), sd((nseq, t, BLOCK), F32), sd((nseq // sq, nh, 1, 1), F32),
                   sd((nseq // sq, nh, 1, 1), F32), sd((nseq // sq, 1, HEAD_DIM), F32)],
        scratch_shapes=[pltpu.VMEM((sq * nh, HEAD_DIM, HEAD_DIM), F32)],
        compiler_params=_cparams("parallel", "arbitrary"),
    )(ya3, ya3, ya3, proj3, alog, dtb, nw, states, inverses, do3)
    return dya.reshape(n, 3 * A_WIDTH), dba.reshape(n, BLOCK), dalog, ddtb, dnw


DIL_NB = tuple((SEQ // d) // BLOCK for _, d in DILATED_PAIRS)
DIL_D = tuple(d for _, d in DILATED_PAIRS)
DIL_STEPS = tuple(w // d for w, d in DILATED_PAIRS)
DIL_B = 4
PAIR = 2 * HEAD_DIM


def _rope_tables(t):
    half = ROPE_DIM // 2
    inv_freq = ROPE_THETA ** (-jnp.arange(half, dtype=F32) / half)
    ang = jnp.arange(t, dtype=F32)[:, None] * inv_freq[None, :]
    ones = jnp.ones((t, HEAD_DIM - ROPE_DIM), F32)
    cs = jnp.concatenate([jnp.cos(ang), jnp.cos(ang), ones], axis=1)
    sn = jnp.concatenate([jnp.sin(ang), jnp.sin(ang), 0.0 * ones], axis=1)
    i = jnp.arange(PAIR)[:, None]
    j = jnp.arange(PAIR)[None, :]
    same = (i // HEAD_DIM) == (j // HEAD_DIM)
    ih, jh = i % HEAD_DIM, j % HEAD_DIM
    pm = (jnp.where(same & (jh < half) & (ih == jh + half), -1.0, 0.0)
          + jnp.where(same & (jh >= half) & (jh < ROPE_DIM) & (ih == jh - half), 1.0, 0.0))
    mean = jnp.where(same, 1.0 / HEAD_DIM, 0.0)
    twice = lambda m: jnp.concatenate([m, m]).astype(BF16)
    return jnp.tile(cs, (1, 2)), jnp.tile(sn, (1, 2)), twice(mean), twice(pm)


def _split_dot(x, w2):
    hi = x.astype(BF16)
    lo = lax.stop_gradient(x - hi.astype(F32)).astype(BF16)
    return lax.dot_general(jnp.concatenate([hi, lo], axis=1), w2, NN, preferred_element_type=F32)


def _dil_prep(x, w, cs, sn, mean2, pm2):
    y = x * lax.rsqrt(_split_dot(x * x, mean2) + RMS_EPS) * w
    return y * cs + _split_dot(y, pm2) * sn


def _dil_tile(qn, kk, vv, bias):
    lane = lax.broadcasted_iota(jnp.int32, (1, PAIR), 1)
    outs, lses = [], []
    for h in range(2):
        s = _mm(jnp.where(lane // HEAD_DIM == h, qn, 0.0) * (HEAD_DIM ** -0.5), kk, NT) + bias
        m = lax.stop_gradient(jnp.max(s, axis=-1, keepdims=True))
        p = jnp.exp(s - m)
        denom = jnp.sum(p, axis=-1, keepdims=True)
        outs.append(_mm(p, vv) / denom)
        lses.append(m + jnp.log(denom))
    return jnp.where(lane < HEAD_DIM, outs[0], outs[1]), jnp.concatenate(lses, axis=1)


_dil_tiles = jax.vmap(_dil_tile)


def _spread(a):
    lane = lax.broadcasted_iota(jnp.int32, (a.shape[0], PAIR), 1)
    return jnp.where(lane < HEAD_DIM, a[:, 0:1], a[:, 1:2])


def _dil_mix(o1, o2, o3, l1, l2, l3):
    m = lax.stop_gradient(jnp.maximum(jnp.maximum(l1, l2), l3))
    e1, e2, e3 = jnp.exp(l1 - m), jnp.exp(l2 - m), jnp.exp(l3 - m)
    r = 1.0 / (e1 + e2 + e3)
    return _spread(e1 * r) * o1 + _spread(e2 * r) * o2 + _spread(e3 * r) * o3


def _dil_fill_biases(bias_s):
    steps, = set(DIL_STEPS)
    qi = lax.broadcasted_iota(jnp.int32, (BLOCK, 1), 0)
    kj = lax.broadcasted_iota(jnp.int32, (1, 2 * BLOCK), 1)
    rel = qi - kj + BLOCK
    inside = (rel >= 0) & (rel <= steps)
    bias_s[0] = jnp.where(inside, 0.0, NEG)
    bias_s[1] = jnp.where(inside & (kj >= BLOCK), 0.0, NEG)
    bias_s[2] = jnp.where((qi >= kj) & (qi - kj <= steps), 0.0, NEG)


def _dil_mask(it, g, bias_s):
    qrows = pl.ds(pl.multiple_of(it * BLOCK, BLOCK), BLOCK)
    if DIL_NB[g] == 1:
        return bias_s[2, :, 0:BLOCK], qrows, qrows
    which = jnp.where(it == 0, 2, jnp.where(it % DIL_NB[g] == 0, 1, 0))
    kstart = jnp.maximum(it - 1, 0) * BLOCK
    return bias_s[which], qrows, pl.ds(pl.multiple_of(kstart, BLOCK), 2 * BLOCK)


def _dil_gather(src, dst, d):
    t = src.shape[0]
    ln = t // d
    for r in range(d):
        dst[pl.ds(r * ln, ln), :] = src[pl.ds(r, ln, stride=d), :]


def _dil_scatter(src, dst, d):
    t = src.shape[0]
    ln = t // d
    for r in range(d):
        dst[pl.ds(r, ln, stride=d), :] = src[pl.ds(r * ln, ln), :]


def _dil_forward_parts(q_ref, k_ref, v_ref, qw, kw, cs_ref, sn_ref, mean2, pm2, qn_s, kn_s, dl_s, od_s, ld_s, on_s, ln_s, bias_s):
    t = qn_s.shape[0]
    _dil_fill_biases(bias_s)

    def prep(c, _):
        rows = pl.ds(pl.multiple_of(c * ROWS, ROWS), ROWS)
        qn_s[rows, :] = _dil_prep(q_ref[rows, :], qw, cs_ref[rows, :], sn_ref[rows, :], mean2, pm2)
        kn_s[rows, :] = _dil_prep(k_ref[rows, :], kw, cs_ref[rows, :], sn_ref[rows, :], mean2, pm2)
        return 0

    lax.fori_loop(0, t // ROWS, prep, 0)
    for g in (1, 2):
        _dil_gather(qn_s, dl_s.at[g - 1, 0], DIL_D[g])
        _dil_gather(kn_s, dl_s.at[g - 1, 1], DIL_D[g])
        _dil_gather(v_ref, dl_s.at[g - 1, 2], DIL_D[g])
    for g in range(3):
        qs = qn_s if g == 0 else dl_s.at[g - 1, 0]
        ks = kn_s if g == 0 else dl_s.at[g - 1, 1]
        vs = v_ref if g == 0 else dl_s.at[g - 1, 2]

        def tiles(i, _, g=g, qs=qs, ks=ks, vs=vs):
            where = [_dil_mask(i * DIL_B + b, g, bias_s) for b in range(DIL_B)]
            o, lse = _dil_tiles(jnp.stack([qs[qr, :] for _, qr, _ in where]), jnp.stack([ks[kr, :] for _, _, kr in where]),
                                jnp.stack([vs[kr, :] for _, _, kr in where]), jnp.stack([m for m, _, _ in where]))
            for b, (_, qr, _) in enumerate(where):
                od_s[g, qr, :] = o[b]
                ld_s[g, qr, :] = lse[b]
            return 0

        lax.fori_loop(0, t // BLOCK // DIL_B, tiles, 0)
    for g in (1, 2):
        _dil_scatter(od_s.at[g], on_s.at[g - 1], DIL_D[g])
        _dil_scatter(ld_s.at[g], ln_s.at[g - 1], DIL_D[g])


def _dil_scratch(t):
    return [pltpu.VMEM((t, PAIR), F32), pltpu.VMEM((t, PAIR), F32),
            pltpu.VMEM((2, 3, t, PAIR), F32),
            pltpu.VMEM((3, t, PAIR), F32), pltpu.VMEM((3, t, 2), F32),
            pltpu.VMEM((2, t, PAIR), F32), pltpu.VMEM((2, t, 2), F32),
            pltpu.VMEM((3, BLOCK, 2 * BLOCK), F32)]


def _dil_specs(t):
    cb = COL_C // BLOCK
    per = C_WIDTH // BLOCK
    qkv = [pl.BlockSpec((t, BLOCK), lambda b, p, j=j: (b, cb + j * per + p)) for j in range(3)]
    vec = pl.BlockSpec((1, PAIR), lambda b, p: (0, 0))
    tab = pl.BlockSpec((t, PAIR), lambda b, p: (0, 0))
    mat = pl.BlockSpec((2 * PAIR, PAIR), lambda b, p: (0, 0))
    pair = pl.BlockSpec((t, BLOCK), lambda b, p: (b, p))
    return qkv, vec, tab, mat, pair


def dil_fwd(proj, qw, kw, cs, sn, mean2, pm2, nseq):
    n = proj.shape[0]
    t = n // nseq

    def body(q_ref, k_ref, v_ref, qw_ref, kw_ref, cs_ref, sn_ref, mean_ref, pm_ref, o_ref,
             qn_s, kn_s, dl_s, od_s, ld_s, on_s, ln_s, bias_s):
        _dil_forward_parts(q_ref, k_ref, v_ref, qw_ref[...], kw_ref[...], cs_ref, sn_ref, mean_ref[...], pm_ref[...],
                           qn_s, kn_s, dl_s, od_s, ld_s, on_s, ln_s, bias_s)

        def mix(c, _):
            rows = pl.ds(pl.multiple_of(c * ROWS, ROWS), ROWS)
            o_ref[rows, :] = _dil_mix(od_s[0, rows, :], on_s[0, rows, :], on_s[1, rows, :],
                                      ld_s[0, rows, :], ln_s[0, rows, :], ln_s[1, rows, :])
            return 0

        lax.fori_loop(0, t // ROWS, mix, 0)

    qkv, vec, tab, mat, pair = _dil_specs(t)
    return pl.pallas_call(
        body, name="dil_fwd", grid=(nseq, C_WIDTH // BLOCK),
        in_specs=qkv + [vec, vec, tab, tab, mat, mat],
        out_specs=pair,
        out_shape=jax.ShapeDtypeStruct((n, C_WIDTH), F32),
        scratch_shapes=_dil_scratch(t),
        compiler_params=_cparams("parallel", "parallel"),
    )(proj, proj, proj, qw, kw, cs, sn, mean2, pm2)


def dil_bwd(proj, qw, kw, cs, sn, mean2, pm2, do, nseq):
    n = proj.shape[0]
    t = n // nseq

    def body(q_ref, k_ref, v_ref, qw_ref, kw_ref, cs_ref, sn_ref, mean_ref, pm_ref, do_ref,
             dq_ref, dk_ref, dv_ref, dqw_ref, dkw_ref,
             qn_s, kn_s, dl_s, od_s, ld_s, on_s, ln_s, bias_s, tq_s, tk_s, tv_s):
        qw, kw, mean2, pm2 = qw_ref[...], kw_ref[...], mean_ref[...], pm_ref[...]
        _dil_forward_parts(q_ref, k_ref, v_ref, qw, kw, cs_ref, sn_ref, mean2, pm2, qn_s, kn_s, dl_s, od_s, ld_s, on_s, ln_s, bias_s)

        def mix(c, _):
            rows = pl.ds(pl.multiple_of(c * ROWS, ROWS), ROWS)
            _, vjp = jax.vjp(_dil_mix, od_s[0, rows, :], on_s[0, rows, :], on_s[1, rows, :],
                             ld_s[0, rows, :], ln_s[0, rows, :], ln_s[1, rows, :])
            d1, d2, d3, e1, e2, e3 = vjp(do_ref[rows, :])
            od_s[0, rows, :] = d1
            on_s[0, rows, :] = d2
            on_s[1, rows, :] = d3
            ld_s[0, rows, :] = e1
            ln_s[0, rows, :] = e2
            ln_s[1, rows, :] = e3
            return 0

        lax.fori_loop(0, t // ROWS, mix, 0)
        for g in (1, 2):
            _dil_gather(on_s.at[g - 1], od_s.at[g], DIL_D[g])
            _dil_gather(ln_s.at[g - 1], ld_s.at[g], DIL_D[g])
        on_s[...] = jnp.zeros_like(on_s)
        dv_ref[...] = jnp.zeros_like(dv_ref)
        for g in range(3):
            qs = qn_s if g == 0 else dl_s.at[g - 1, 0]
            ks = kn_s if g == 0 else dl_s.at[g - 1, 1]
            vs = v_ref if g == 0 else dl_s.at[g - 1, 2]
            gq = on_s.at[0] if g == 0 else tq_s
            gk = on_s.at[1] if g == 0 else tk_s
            gv = dv_ref if g == 0 else tv_s
            if g > 0:
                tk_s[...] = jnp.zeros_like(tk_s)
                tv_s[...] = jnp.zeros_like(tv_s)

            def tiles(i, _, g=g, qs=qs, ks=ks, vs=vs, gq=gq, gk=gk, gv=gv):
                where = [_dil_mask(i * DIL_B + b, g, bias_s) for b in range(DIL_B)]
                biases = jnp.stack([m for m, _, _ in where])
                _, vjp = jax.vjp(lambda q_, k_, v_: _dil_tiles(q_, k_, v_, biases),
                                 jnp.stack([qs[qr, :] for _, qr, _ in where]), jnp.stack([ks[kr, :] for _, _, kr in where]),
                                 jnp.stack([vs[kr, :] for _, _, kr in where]))
                dq, dkk, dvv = vjp((jnp.stack([od_s[g, qr, :] for _, qr, _ in where]),
                                    jnp.stack([ld_s[g, qr, :] for _, qr, _ in where])))
                for b, (_, qr, kr) in enumerate(where):
                    gq[qr, :] = dq[b]
                    gk[kr, :] += dkk[b]
                    gv[kr, :] += dvv[b]
                return 0

            lax.fori_loop(0, t // BLOCK // DIL_B, tiles, 0)
            if g > 0:
                d = DIL_D[g]
                ln = t // d
                for r in range(d):
                    nat, dil = pl.ds(r, ln, stride=d), pl.ds(r * ln, ln)
                    on_s[0, nat, :] += tq_s[dil, :]
                    on_s[1, nat, :] += tk_s[dil, :]
                    dv_ref[nat, :] += tv_s[dil, :]

        def prep(c, acc):
            rows = pl.ds(pl.multiple_of(c * ROWS, ROWS), ROWS)
            f = lambda x, w: _dil_prep(x, w, cs_ref[rows, :], sn_ref[rows, :], mean2, pm2)
            _, vq = jax.vjp(f, q_ref[rows, :], qw)
            _, vk = jax.vjp(f, k_ref[rows, :], kw)
            dq, dqw = vq(on_s[0, rows, :])
            dk, dkw = vk(on_s[1, rows, :])
            dq_ref[rows, :] = dq
            dk_ref[rows, :] = dk
            return acc[0] + dqw, acc[1] + dkw

        dqw, dkw = lax.fori_loop(0, t // ROWS, prep, (jnp.zeros((1, PAIR), F32), jnp.zeros((1, PAIR), F32)))
        dqw_ref[0] = dqw
        dkw_ref[0] = dkw

    qkv, vec, tab, mat, pair = _dil_specs(t)
    per = C_WIDTH // BLOCK
    wout = pl.BlockSpec((1, 1, PAIR), lambda b, p: (b * per + p, 0, 0))
    return pl.pallas_call(
        body, name="dil_bwd", grid=(nseq, per),
        in_specs=qkv + [vec, vec, tab, tab, mat, mat, pair],
        out_specs=[pair, pair, pair, wout, wout],
        out_shape=[jax.ShapeDtypeStruct((n, C_WIDTH), F32)] * 3 + [jax.ShapeDtypeStruct((nseq * per, 1, PAIR), F32)] * 2,
        scratch_shapes=_dil_scratch(t) + [pltpu.VMEM((t, PAIR), F32)] * 3,
        compiler_params=_cparams("parallel", "parallel"),
    )(proj, proj, proj, qw, kw, cs, sn, mean2, pm2, do)


N_CHIPS = 4
SUM_ROWS = 432
MESH_IDS = pl.DeviceIdType.MESH
ANY = pl.BlockSpec(memory_space=pl.ANY)


def plane_exchange(src, all_to_all):
    blk_shape = src.shape[1:] if all_to_all else src.shape

    def body(src_ref, out_ref, send_sems, recv_sems, local_sem):
        x, y, c = lax.axis_index("x"), lax.axis_index("y"), lax.axis_index("c")
        me = 2 * x + y
        mine = pltpu.make_async_copy(src_ref.at[me] if all_to_all else src_ref, out_ref.at[me], local_sem)
        mine.start()
        sends = []
        for k in (1, 2, 3):
            px = 1 - x if k & 2 else x
            py = 1 - y if k & 1 else y
            peer = 2 * px + py
            cp = pltpu.make_async_remote_copy(
                src_ref=src_ref.at[peer] if all_to_all else src_ref, dst_ref=out_ref.at[me],
                send_sem=send_sems.at[k - 1], recv_sem=recv_sems.at[k - 1],
                device_id=(px, py, c), device_id_type=MESH_IDS)
            cp.start()
            sends.append((cp, peer, (px, py, c)))
        for k, (cp, peer, dev) in enumerate(sends):
            pltpu.make_async_remote_copy(
                src_ref=out_ref.at[me], dst_ref=out_ref.at[peer],
                send_sem=send_sems.at[k], recv_sem=recv_sems.at[k],
                device_id=dev, device_id_type=MESH_IDS).wait_recv()
        for cp, _, _ in sends:
            cp.wait_send()
        mine.wait()

    return pl.pallas_call(
        body, name="plane_all_to_all" if all_to_all else "plane_all_gather",
        in_specs=[ANY], out_specs=ANY,
        out_shape=jax.ShapeDtypeStruct((N_CHIPS,) + blk_shape, src.dtype),
        scratch_shapes=[pltpu.SemaphoreType.DMA((3,)), pltpu.SemaphoreType.DMA((3,)), pltpu.SemaphoreType.DMA],
    )(src)


def sibling_swap(src):
    def body(src_ref, out_ref, send_sem, recv_sem):
        x, y, c = lax.axis_index("x"), lax.axis_index("y"), lax.axis_index("c")
        cp = pltpu.make_async_remote_copy(src_ref=src_ref, dst_ref=out_ref, send_sem=send_sem, recv_sem=recv_sem,
                                          device_id=(x, y, 1 - c), device_id_type=MESH_IDS)
        cp.start()
        cp.wait()

    return pl.pallas_call(
        body, name="sibling_swap", in_specs=[ANY], out_specs=ANY,
        out_shape=jax.ShapeDtypeStruct(src.shape, src.dtype),
        scratch_shapes=[pltpu.SemaphoreType.DMA, pltpu.SemaphoreType.DMA],
    )(src)


def sum4(a):
    _, r, c = a.shape
    tr = SUM_ROWS

    def body(a_ref, o_ref):
        p = [a_ref[i].astype(F32) for i in range(N_CHIPS)]
        o_ref[...] = (p[0] + p[1]) + (p[2] + p[3])

    return pl.pallas_call(
        body, name="sum4", grid=(r // tr,),
        in_specs=[pl.BlockSpec((N_CHIPS, tr, c), lambda i: (0, i, 0))],
        out_specs=pl.BlockSpec((tr, c), lambda i: (i, 0)),
        out_shape=jax.ShapeDtypeStruct((r, c), F32),
        compiler_params=_cparams("parallel"),
    )(a)


def add2(a, b, dtype):
    r, c = a.shape
    tr = SUM_ROWS

    def body(a_ref, b_ref, o_ref):
        o_ref[...] = (a_ref[...] + b_ref[...]).astype(dtype)

    blk = pl.BlockSpec((tr, c), lambda i: (i, 0))
    return pl.pallas_call(
        body, name="add2", grid=(r // tr,), in_specs=[blk, blk], out_specs=blk,
        out_shape=jax.ShapeDtypeStruct((r, c), dtype), compiler_params=_cparams("parallel"),
    )(a, b)


PACK_COLS = 1152
PACK_ROWS = 2592
ROW_TILE = 16


def _pack(parts):
    blocks = []
    for p in parts:
        p2 = p.reshape(-1, p.shape[-1])
        blocks.append(jnp.pad(p2, ((0, -p2.shape[0] % ROW_TILE), (0, PACK_COLS - p2.shape[1]))))
    rows = sum(b.shape[0] for b in blocks)
    blocks.append(jnp.zeros((PACK_ROWS - rows, PACK_COLS), blocks[0].dtype))
    return jnp.concatenate(blocks)


def _unpack(buf, shapes):
    out, at = [], 0
    for s in shapes:
        rows = math.prod(s[:-1])
        out.append(buf[at:at + rows, :s[-1]].reshape(s))
        at += rows + (-rows % ROW_TILE)
    return out


def _pack_small(g):
    blk = jnp.zeros((ROW_TILE, PACK_COLS), F32)
    for i, k in enumerate(SMALL):
        blk = blk.at[2 * i:2 * i + 2, :g[k].shape[1]].set(g[k])
    return blk


def _unpack_small(blk, shapes):
    return [blk[2 * i:2 * i + 2, :s[1]] for i, s in enumerate(shapes)]


def _layer_fwd(x, p, nseq, tabs):
    proj, hdn = inproj_fwd(x, p["norm_w"][None], p["w_in"])
    ya = conv_fwd(proj, p["conv_w"], nseq)
    oa, states, inverses = gdn_fwd(ya, proj, p["a_log"].reshape(N_HEADS_A, 1, 1), p["dt_bias"].reshape(N_HEADS_A, 1, 1),
                         p["gdn_norm_w"][None], nseq)
    ob, carries = sb_fwd(proj, nseq)
    oc = dil_fwd(proj, jnp.tile(p["q_norm_w"], 2)[None], jnp.tile(p["k_norm_w"], 2)[None], *tabs, nseq)
    y, mixed = outproj_fwd(x, oa, ob, oc, proj, p["w_out"])
    return y, dict(x=x, hdn=hdn, proj=proj, ya=ya, states=states, inverses=inverses, carries=carries, oa=oa, ob=ob, oc=oc, mixed=mixed)


def _layer_bwd(dy, p, res, nseq, tabs):
    proj = res["proj"]
    g = {}
    g["w_out"] = mat_tn(res["mixed"], [dy])[0]
    doa, dob, doc, dza, dzb, dzc = outproj_bwd(dy, res["oa"], res["ob"], res["oc"], proj, p["w_out"])
    dqc, dkc, dvc, dqw, dkw = dil_bwd(proj, jnp.tile(p["q_norm_w"], 2)[None], jnp.tile(p["k_norm_w"], 2)[None], *tabs, doc,
                                      nseq)
    g["q_norm_w"], g["k_norm_w"] = dqw.reshape(-1, HEAD_DIM).sum(0), dkw.reshape(-1, HEAD_DIM).sum(0)
    dqb, dkb, dvb = sb_bwd(proj, res["carries"], dob, nseq)
    dya, dba, dalog, ddtb, dnw = gdn_bwd(res["ya"], proj, p["a_log"].reshape(N_HEADS_A, 1, 1),
                                         p["dt_bias"].reshape(N_HEADS_A, 1, 1), p["gdn_norm_w"][None], res["states"], res["inverses"], doa,
                                         nseq)
    g["a_log"], g["dt_bias"], g["gdn_norm_w"] = dalog.sum(0).reshape(-1), ddtb.sum(0).reshape(-1), dnw.sum((0, 1))
    dqkv, dcw = conv_bwd(proj, p["conv_w"], dya, nseq)
    g["conv_w"] = dcw.sum(0)
    slabs = [dqkv, dza, dqc, dkc, dvc, dzc, dqb, dkb, dvb, dzb, dba]
    hdn = res["hdn"]
    g["w_in"] = jnp.concatenate(mat_tn(hdn, slabs[:2]) + mat_tn(hdn, slabs[2:6]) + mat_tn(hdn, slabs[6:]), axis=1)
    dx, dnw_tiles = inproj_bwd(slabs, p["w_in"], res["x"], p["norm_w"][None], dy)
    g["norm_w"] = dnw_tiles.sum((0, 1))
    return dx, g


SMALL = ("norm_w", "a_log", "dt_bias", "gdn_norm_w", "q_norm_w", "k_norm_w")


def _local_step(x, target, full):
    nseq, t, d = x.shape
    tabs = _rope_tables(t)
    h = x.reshape(nseq * t, d)
    saved = []
    for l in range(DEPTH):
        p = {k: v[l] for k, v in full.items()}
        h, res = _layer_fwd(h, p, nseq, tabs)
        saved.append((p, res))
    dy, parts = loss_fwd_bwd(h, target.reshape(nseq * t, d))
    loss = parts[:, 0, 0].sum()
    grads = [None] * DEPTH
    for l in reversed(range(DEPTH)):
        p, res = saved[l]
        dy, grads[l] = _layer_bwd(dy, p, res, nseq, tabs)
    return loss, dy.reshape(nseq, t, d), {k: jnp.stack([g[k] for g in grads]) for k in grads[0]}


def _pad_cols(w):
    b0 = ORIG_A + ORIG_BA
    c0 = b0 + ORIG_B
    zeros = jnp.zeros(w.shape[:-1] + (BLOCK - ORIG_BA,), w.dtype)
    return jnp.concatenate([w[..., :ORIG_A], w[..., c0:], w[..., b0:c0], w[..., ORIG_A:b0], zeros], axis=-1)


def _unpad_cols(w):
    return jnp.concatenate([w[..., :COL_C], w[..., COL_BA:COL_BA + ORIG_BA], w[..., COL_B:COL_BA], w[..., COL_C:COL_B]],
                           axis=-1)


def kernel(x, norm_w, w_in, conv_w, a_log, dt_bias, gdn_norm_w, q_norm_w, k_norm_w, w_out, loss_target, m_norm_w, m_w_in, m_conv_w, m_a_log, m_dt_bias, m_gdn_norm_w, m_q_norm_w, m_k_norm_w, m_w_out, v_norm_w, v_w_in, v_conv_w, v_a_log, v_dt_bias, v_gdn_norm_w, v_q_norm_w, v_k_norm_w, v_w_out):
    weights = dict(norm_w=norm_w, w_in=w_in, conv_w=conv_w, a_log=a_log, dt_bias=dt_bias, gdn_norm_w=gdn_norm_w,
                   q_norm_w=q_norm_w, k_norm_w=k_norm_w, w_out=w_out)
    moms = dict(norm_w=m_norm_w, w_in=m_w_in, conv_w=m_conv_w, a_log=m_a_log, dt_bias=m_dt_bias,
                gdn_norm_w=m_gdn_norm_w, q_norm_w=m_q_norm_w, k_norm_w=m_k_norm_w, w_out=m_w_out)
    vars_ = dict(norm_w=v_norm_w, w_in=v_w_in, conv_w=v_conv_w, a_log=v_a_log, dt_bias=v_dt_bias,
                 gdn_norm_w=v_gdn_norm_w, q_norm_w=v_q_norm_w, k_norm_w=v_k_norm_w, w_out=v_w_out)
    names = list(weights)
    sharded = ("w_in", "w_out", "conv_w")
    shard_shapes = [weights[k].shape for k in sharded]

    c = lax.axis_index("c")
    half = PACK_ROWS // 2
    conv_bits = lax.bitcast_convert_type(conv_w, BF16).reshape(conv_w.shape[:2] + (2 * conv_w.shape[2],))
    shard = _pack([w_in.astype(BF16), w_out.astype(BF16), conv_bits])
    mine = plane_exchange(lax.dynamic_slice_in_dim(shard, c * half, half, axis=0), all_to_all=False)
    other = sibling_swap(mine)
    got = jnp.concatenate([jnp.where(c == 0, mine, other), jnp.where(c == 0, other, mine)], axis=1)
    per_chip = [_unpack(got[i], shard_shapes[:2] + [conv_bits.shape]) for i in range(N_CHIPS)]
    full = {k: weights[k] for k in SMALL}
    full["w_in"] = _pad_cols(jnp.concatenate([pc[0] for pc in per_chip], axis=2))
    full["w_out"] = jnp.concatenate([pc[1] for pc in per_chip], axis=1)
    full["conv_w"] = jnp.concatenate(
        [lax.bitcast_convert_type(pc[2].reshape(conv_w.shape + (2,)), F32) for pc in per_chip], axis=2)

    loss, grad_x, g = _local_step(x, loss_target, full)

    gw_in = _unpad_cols(g["w_in"])
    cols, rows = w_in.shape[2], w_out.shape[1]
    small = _pack_small(g)
    send = jnp.stack([_pack([gw_in[:, :, i * cols:(i + 1) * cols], g["w_out"][:, i * rows:(i + 1) * rows],
                             g["conv_w"][:, :, i * conv_w.shape[2]:(i + 1) * conv_w.shape[2]], small])
                      for i in range(N_CHIPS)])
    keep = lax.dynamic_slice_in_dim(send, c * half, half, axis=1)
    give = lax.dynamic_slice_in_dim(send, (1 - c) * half, half, axis=1)
    chip_sum = add2(keep.reshape(N_CHIPS * half, PACK_COLS), sibling_swap(give).reshape(N_CHIPS * half, PACK_COLS), BF16)
    mine = sum4(plane_exchange(chip_sum.reshape(N_CHIPS, half, PACK_COLS), all_to_all=True))
    other = sibling_swap(mine)
    total = jnp.concatenate([jnp.where(c == 0, mine, other), jnp.where(c == 0, other, mine)])
    reduced = _unpack(total, shard_shapes + [(ROW_TILE, PACK_COLS)])
    grads = dict(zip(sharded, reduced[:3]))
    grads.update(zip(SMALL, _unpack_small(reduced[3], [weights[k].shape for k in SMALL])))
    loss = lax.psum(loss, ("x", "y", "c"))

    def two_d(a):
        return a.reshape(-1, a.shape[-1])

    delta, new_m, new_v = {}, {}, {}
    for k in names:
        d_, m_, v_ = adamw(two_d(weights[k]), two_d(grads[k]), two_d(moms[k]), two_d(vars_[k]))
        delta[k], new_m[k], new_v[k] = (a.reshape(weights[k].shape) for a in (d_, m_, v_))
    return (loss, grad_x, *[grads[k] for k in names], *[delta[k] for k in names],
            *[new_m[k] for k in names], *[new_v[k] for k in names])
```

```python
import functools
import math

import jax
import jax.numpy as jnp
from jax import lax
from jax.experimental import pallas as pl
from jax.experimental.pallas import tpu as pltpu

F32 = jnp.float32
BF16 = jnp.bfloat16

D_MODEL = 1024
SEQ = 2048
DEPTH = 2
HEAD_DIM = 64
N_HEADS_A, N_HEADS_B, N_HEADS_C = 6, 4, 6
A_WIDTH, B_WIDTH, C_WIDTH = N_HEADS_A * HEAD_DIM, N_HEADS_B * HEAD_DIM, N_HEADS_C * HEAD_DIM
CONV_WIDTH = 4
GDN_CHUNK = 64
BLOCK = 128
ROPE_DIM = 16
ROPE_THETA = 500000.0
DILATED_PAIRS = ((128, 1), (512, 4), (2048, 16))
RMS_EPS = 1e-6
NEG = -1e30

NT = (((1,), (1,)), ((), ()))
NN = (((1,), (0,)), ((), ()))
TN = (((0,), (0,)), ((), ()))

VMEM_LIMIT = 48 * 1024 * 1024

ORIG_A = 4 * A_WIDTH
ORIG_BA = 2 * N_HEADS_A
ORIG_B = 4 * B_WIDTH
COL_AZ = 3 * A_WIDTH
COL_C = 4 * A_WIDTH
COL_B = COL_C + 4 * C_WIDTH
COL_BA = COL_B + 4 * B_WIDTH
P_COLS = COL_BA + BLOCK
TN_COLS = 384
INPROJ_COLS = P_COLS // 3
TM_ROWS = 512
ROWS = 256


def _mm(a, b, dims=NN):
    return lax.dot_general(a.astype(BF16), b.astype(BF16), dims, preferred_element_type=F32)


def _mm32(a, b, dims=NN):
    return lax.dot_general(a, b, dims, precision=lax.Precision.HIGH, preferred_element_type=F32)


def _cparams(*sem):
    return pltpu.CompilerParams(dimension_semantics=sem, vmem_limit_bytes=VMEM_LIMIT)


def _sigmoid(x):
    return 0.5 * (jnp.tanh(0.5 * x) + 1.0)


def _softplus(x):
    return jnp.maximum(x, 0.0) + jnp.log(1.0 + jnp.exp(-jnp.abs(x)))


def _rms(x, w):
    return x * lax.rsqrt(jnp.mean(x * x, axis=-1, keepdims=True) + RMS_EPS) * w


def _heads(a, n):
    return jnp.stack([a[:, h * HEAD_DIM:(h + 1) * HEAD_DIM] for h in range(n)])


def _unheads(a):
    return jnp.concatenate([a[h] for h in range(a.shape[0])], axis=1)


def _row_chunks(t):
    return [pl.ds(c * ROWS, ROWS) for c in range(t // ROWS)]


def inproj_fwd(x, nw, w):
    n, d = x.shape
    p = w.shape[1]

    def body(x_ref, nw_ref, w_ref, proj_ref, hdn_ref):
        @pl.when(pl.program_id(1) == 0)
        def _():
            hdn_ref[...] = _rms(x_ref[...], nw_ref[...]).astype(BF16)

        proj_ref[...] = jnp.dot(hdn_ref[...], w_ref[...], preferred_element_type=F32)

    return pl.pallas_call(
        body, name="inproj_fwd", grid=(n // TM_ROWS, p // INPROJ_COLS),
        in_specs=[pl.BlockSpec((TM_ROWS, d), lambda i, j: (i, 0)), pl.BlockSpec((1, d), lambda i, j: (0, 0)),
                  pl.BlockSpec((d, INPROJ_COLS), lambda i, j: (0, j))],
        out_specs=[pl.BlockSpec((TM_ROWS, INPROJ_COLS), lambda i, j: (i, j)), pl.BlockSpec((TM_ROWS, d), lambda i, j: (i, 0))],
        out_shape=[jax.ShapeDtypeStruct((n, p), F32), jax.ShapeDtypeStruct((n, d), BF16)],
        compiler_params=_cparams("parallel", "arbitrary"),
    )(x, nw, w)


def mat_tn(a, slabs):
    n, ka = a.shape
    ns = len(slabs)

    def body(*refs):
        a_ref, s_refs, o_refs = refs[0], refs[1:1 + ns], refs[1 + ns:]

        @pl.when(pl.program_id(0) == 0)
        def _():
            for o_ref in o_refs:
                o_ref[...] = jnp.zeros_like(o_ref)

        av = a_ref[...]
        for s_ref, o_ref in zip(s_refs, o_refs):
            o_ref[...] += lax.dot_general(av, s_ref[...].astype(BF16), TN, preferred_element_type=F32)

    return pl.pallas_call(
        body, name="mat_tn", grid=(n // TM_ROWS,),
        in_specs=[pl.BlockSpec((TM_ROWS, ka), lambda k: (k, 0))]
                 + [pl.BlockSpec((TM_ROWS, s.shape[1]), lambda k: (k, 0)) for s in slabs],
        out_specs=[pl.BlockSpec((ka, s.shape[1]), lambda k: (0, 0)) for s in slabs],
        out_shape=[jax.ShapeDtypeStruct((ka, s.shape[1]), F32) for s in slabs],
        compiler_params=_cparams("arbitrary"),
    )(a, *slabs)


def inproj_bwd(slabs, w, x, nw, dy):
    n, d = x.shape
    p = w.shape[1]
    tm = 256
    ns = len(slabs)

    def body(*refs):
        s_refs = refs[:ns]
        w_ref, x_ref, nw_ref, dy_ref, dx_ref, dnw_ref = refs[ns:]
        dh = jnp.zeros((tm, d), F32)
        at = 0
        for s_ref in s_refs:
            wd = s_ref.shape[1]
            dh = dh + lax.dot_general(s_ref[...].astype(BF16), w_ref[:, at:at + wd], NT, preferred_element_type=F32)
            at += wd
        _, vjp = jax.vjp(_rms, x_ref[...], nw_ref[...])
        dx, dnw = vjp(dh)
        dx_ref[...] = dx + dy_ref[...]
        dnw_ref[0] = dnw

    return pl.pallas_call(
        body, name="inproj_bwd", grid=(n // tm,),
        in_specs=[pl.BlockSpec((tm, s.shape[1]), lambda i: (i, 0)) for s in slabs]
                 + [pl.BlockSpec((d, p), lambda i: (0, 0)), pl.BlockSpec((tm, d), lambda i: (i, 0)),
                    pl.BlockSpec((1, d), lambda i: (0, 0)), pl.BlockSpec((tm, d), lambda i: (i, 0))],
        out_specs=[pl.BlockSpec((tm, d), lambda i: (i, 0)), pl.BlockSpec((1, 1, d), lambda i: (i, 0, 0))],
        out_shape=[jax.ShapeDtypeStruct((n, d), F32), jax.ShapeDtypeStruct((n // tm, 1, d), F32)],
        compiler_params=_cparams("parallel"),
    )(*slabs, w, x, nw, dy)


CONV_PAD = 8
CONV_ROWS = 256


def _conv_pre(pad_s, cw, c):
    xs = [pad_s[pl.ds(c * CONV_ROWS + CONV_PAD - (CONV_WIDTH - 1) + k, CONV_ROWS), :] for k in range(CONV_WIDTH)]
    pre = xs[0] * cw[0:1, :]
    for k in range(1, CONV_WIDTH):
        pre = pre + xs[k] * cw[k:k + 1, :]
    return pre, xs


def conv_fwd(proj, cw, nseq):
    n = proj.shape[0]
    t = n // nseq
    ch = cw.shape[1]

    def body(x_ref, cw_ref, y_ref, pad_s):
        pad_s[pl.ds(0, CONV_PAD), :] = jnp.zeros((CONV_PAD, TN_COLS), F32)
        pad_s[pl.ds(CONV_PAD, t), :] = x_ref[...]
        cwv = cw_ref[...]
        for c in range(t // CONV_ROWS):
            pre, _ = _conv_pre(pad_s, cwv, c)
            y_ref[pl.ds(c * CONV_ROWS, CONV_ROWS), :] = pre * _sigmoid(pre)

    return pl.pallas_call(
        body, name="conv_fwd", grid=(nseq, ch // TN_COLS),
        in_specs=[pl.BlockSpec((t, TN_COLS), lambda b, j: (b, j)), pl.BlockSpec((CONV_WIDTH, TN_COLS), lambda b, j: (0, j))],
        out_specs=pl.BlockSpec((t, TN_COLS), lambda b, j: (b, j)),
        out_shape=jax.ShapeDtypeStruct((n, ch), F32),
        scratch_shapes=[pltpu.VMEM((t + CONV_PAD, TN_COLS), F32)],
        compiler_params=_cparams("parallel", "parallel"),
    )(proj, cw)


def conv_bwd(proj, cw, dy, nseq):
    n = proj.shape[0]
    t = n // nseq
    ch = cw.shape[1]

    def body(x_ref, cw_ref, dy_ref, dx_ref, dcw_ref, pad_s, dpad_s):
        pad_s[pl.ds(0, CONV_PAD), :] = jnp.zeros((CONV_PAD, TN_COLS), F32)
        pad_s[pl.ds(CONV_PAD, t), :] = x_ref[...]
        dpad_s[pl.ds(t, CONV_PAD), :] = jnp.zeros((CONV_PAD, TN_COLS), F32)
        cwv = cw_ref[...]
        acc = [jnp.zeros((1, TN_COLS), F32)] * CONV_WIDTH
        for c in range(t // CONV_ROWS):
            pre, xs = _conv_pre(pad_s, cwv, c)
            sg = _sigmoid(pre)
            dpre = dy_ref[pl.ds(c * CONV_ROWS, CONV_ROWS), :] * (sg * (1.0 + pre * (1.0 - sg)))
            dpad_s[pl.ds(c * CONV_ROWS, CONV_ROWS), :] = dpre
            acc = [acc[k] + jnp.sum(dpre * xs[k], axis=0, keepdims=True) for k in range(CONV_WIDTH)]
        for k in range(CONV_WIDTH):
            dcw_ref[0, pl.ds(k, 1), :] = acc[k]
        for c in range(t // CONV_ROWS):
            dx = dpad_s[pl.ds(c * CONV_ROWS + CONV_WIDTH - 1, CONV_ROWS), :] * cwv[0:1, :]
            for k in range(1, CONV_WIDTH):
                dx = dx + dpad_s[pl.ds(c * CONV_ROWS + CONV_WIDTH - 1 - k, CONV_ROWS), :] * cwv[k:k + 1, :]
            dx_ref[pl.ds(c * CONV_ROWS, CONV_ROWS), :] = dx

    blk = pl.BlockSpec((t, TN_COLS), lambda b, j: (b, j))
    return pl.pallas_call(
        body, name="conv_bwd", grid=(nseq, ch // TN_COLS),
        in_specs=[blk, pl.BlockSpec((CONV_WIDTH, TN_COLS), lambda b, j: (0, j)), blk],
        out_specs=[blk, pl.BlockSpec((1, CONV_WIDTH, TN_COLS), lambda b, j: (b, 0, j))],
        out_shape=[jax.ShapeDtypeStruct((n, ch), F32), jax.ShapeDtypeStruct((nseq, CONV_WIDTH, ch), F32)],
        scratch_shapes=[pltpu.VMEM((t + CONV_PAD, TN_COLS), F32)] * 2,
        compiler_params=_cparams("parallel", "parallel"),
    )(proj, cw, dy)


def _gate_specs(d):
    wide = pl.BlockSpec((TM_ROWS, d), lambda i: (i, 0))
    oa = pl.BlockSpec((TM_ROWS, A_WIDTH), lambda i: (i, 0))
    ob = pl.BlockSpec((TM_ROWS, B_WIDTH), lambda i: (i, 0))
    oc = pl.BlockSpec((TM_ROWS, C_WIDTH), lambda i: (i, 0))
    za = pl.BlockSpec((TM_ROWS, A_WIDTH), lambda i: (i, COL_AZ // A_WIDTH))
    zb = pl.BlockSpec((TM_ROWS, B_WIDTH), lambda i: (i, (COL_B + 3 * B_WIDTH) // B_WIDTH))
    zc = pl.BlockSpec((TM_ROWS, C_WIDTH), lambda i: (i, (COL_C + 3 * C_WIDTH) // C_WIDTH))
    return wide, oa, ob, oc, za, zb, zc


BRANCH_COLS = ((0, A_WIDTH), (A_WIDTH, A_WIDTH + B_WIDTH), (A_WIDTH + B_WIDTH, D_MODEL))


def outproj_fwd(x, oa, ob, oc, proj, w):
    n, d = x.shape

    def body(x_ref, oa_ref, ob_ref, oc_ref, za_ref, zb_ref, zc_ref, w_ref, y_ref, m_ref):
        for (lo, hi), o_ref, z_ref in zip(BRANCH_COLS, (oa_ref, ob_ref, oc_ref), (za_ref, zb_ref, zc_ref)):
            zv = z_ref[...]
            m_ref[:, lo:hi] = (o_ref[...] * (zv * _sigmoid(zv))).astype(BF16)
        y_ref[...] = x_ref[...] + jnp.dot(m_ref[...], w_ref[...], preferred_element_type=F32)

    wide, sa, sb, sc, za, zb, zc = _gate_specs(d)
    return pl.pallas_call(
        body, name="outproj_fwd", grid=(n // TM_ROWS,),
        in_specs=[wide, sa, sb, sc, za, zb, zc, pl.BlockSpec((d, d), lambda i: (0, 0))],
        out_specs=[wide, wide],
        out_shape=[jax.ShapeDtypeStruct((n, d), F32), jax.ShapeDtypeStruct((n, d), BF16)],
        compiler_params=_cparams("parallel"),
    )(x, oa, ob, oc, proj, proj, proj, w)


def outproj_bwd(dy, oa, ob, oc, proj, w):
    n, d = dy.shape

    def body(dy_ref, oa_ref, ob_ref, oc_ref, za_ref, zb_ref, zc_ref, w_ref, doa_ref, dob_ref, doc_ref, dza_ref, dzb_ref, dzc_ref):
        dm = lax.dot_general(dy_ref[...].astype(BF16), w_ref[...], NT, preferred_element_type=F32)
        for (lo, hi), o_ref, z_ref, do_ref, dz_ref in zip(BRANCH_COLS, (oa_ref, ob_ref, oc_ref), (za_ref, zb_ref, zc_ref),
                                                          (doa_ref, dob_ref, doc_ref), (dza_ref, dzb_ref, dzc_ref)):
            zv = z_ref[...]
            sg = _sigmoid(zv)
            dmv = dm[:, lo:hi]
            do_ref[...] = dmv * (zv * sg)
            dz_ref[...] = dmv * o_ref[...] * (sg * (1.0 + zv * (1.0 - sg)))

    wide, sa, sb, sc, za, zb, zc = _gate_specs(d)
    sd = jax.ShapeDtypeStruct
    outs = [sd((n, A_WIDTH), F32), sd((n, B_WIDTH), F32), sd((n, C_WIDTH), F32)]
    return pl.pallas_call(
        body, name="outproj_bwd", grid=(n // TM_ROWS,),
        in_specs=[wide, sa, sb, sc, za, zb, zc, pl.BlockSpec((d, d), lambda i: (0, 0))],
        out_specs=[sa, sb, sc, sa, sb, sc],
        out_shape=outs + outs,
        compiler_params=_cparams("parallel"),
    )(dy, oa, ob, oc, proj, proj, proj, w)


def loss_fwd_bwd(y, target):
    n, d = y.shape

    def body(y_ref, t_ref, dy_ref, part_ref):
        e = y_ref[...] - t_ref[...]
        dy_ref[...] = e * (1.0 / d)
        part_ref[...] = jnp.zeros_like(part_ref) + 0.5 * jnp.sum(e * e) * (1.0 / d)

    blk = pl.BlockSpec((TM_ROWS, d), lambda i: (i, 0))
    return pl.pallas_call(
        body, name="loss", grid=(n // TM_ROWS,),
        in_specs=[blk, blk],
        out_specs=[blk, pl.BlockSpec((1, 8, BLOCK), lambda i: (i, 0, 0))],
        out_shape=[jax.ShapeDtypeStruct((n, d), F32), jax.ShapeDtypeStruct((n // TM_ROWS, 8, BLOCK), F32)],
        compiler_params=_cparams("parallel"),
    )(y, target)


ADAM_LR, ADAM_B1, ADAM_B2, ADAM_EPS, ADAM_WD, ADAM_STEP = 0.001, 0.9, 0.999, 1e-08, 0.01, 10


def adamw(w, g, m, v):
    r, c = w.shape
    tr = r if r <= 256 else 256

    def body(w_ref, g_ref, m_ref, v_ref, d_ref, nm_ref, nv_ref):
        gv = g_ref[...]
        nm = ADAM_B1 * m_ref[...] + (1.0 - ADAM_B1) * gv
        nv = ADAM_B2 * v_ref[...] + (1.0 - ADAM_B2) * (gv * gv)
        m_hat = nm / (1.0 - ADAM_B1 ** ADAM_STEP)
        v_hat = nv / (1.0 - ADAM_B2 ** ADAM_STEP)
        d_ref[...] = -ADAM_LR * (m_hat / (jnp.sqrt(v_hat) + ADAM_EPS) + ADAM_WD * w_ref[...])
        nm_ref[...] = nm
        nv_ref[...] = nv

    blk = pl.BlockSpec((tr, c), lambda i: (i, 0))
    return pl.pallas_call(
        body, name="adamw", grid=(r // tr,),
        in_specs=[blk] * 4, out_specs=[blk] * 3,
        out_shape=[jax.ShapeDtypeStruct((r, c), F32)] * 3,
        compiler_params=_cparams("parallel"),
    )(w, g, m, v)


SB_G = N_HEADS_B


def _sb_tile(q, k, v, carry, tri, diag):
    z = _mm(q * (HEAD_DIM ** -0.5), k, NT)
    sp = jnp.log(1.0 + jnp.exp(-jnp.abs(z)))
    ls_pos = jnp.minimum(z, 0.0) - sp
    ls_neg = jnp.minimum(-z, 0.0) - sp
    if diag:
        earlier = lax.broadcasted_iota(jnp.int32, z.shape, 1) < lax.broadcasted_iota(jnp.int32, z.shape, 0)
        log_keep = jnp.where(earlier, ls_neg, 0.0)
    else:
        log_keep = ls_neg
    hi = log_keep.astype(BF16)
    lo = lax.stop_gradient(log_keep - hi.astype(F32)).astype(BF16)
    within = lax.dot_general(jnp.concatenate([hi, lo], axis=1), tri, NN, preferred_element_type=F32)
    arg = ls_pos + within + carry
    wts = jnp.where(earlier, jnp.exp(jnp.where(earlier, arg, 0.0)), 0.0) if diag else jnp.exp(arg)
    return _mm(wts, v), jnp.sum(log_keep, axis=1, keepdims=True)


_sb_tiles_diag = jax.vmap(functools.partial(_sb_tile, diag=True), in_axes=(0, 0, 0, 0, None))
_sb_tiles_off = jax.vmap(functools.partial(_sb_tile, diag=False), in_axes=(0, 0, 0, 0, None))


def _sb_tri():
    r = lax.broadcasted_iota(jnp.int32, (2 * BLOCK, BLOCK), 0) % BLOCK
    c = lax.broadcasted_iota(jnp.int32, (2 * BLOCK, BLOCK), 1)
    return jnp.where(r > c, 1.0, 0.0).astype(BF16)


SB_SEQ = 2


def _sb_specs(t, nq):
    cb = COL_B // B_WIDTH
    sq = SB_SEQ
    q = pl.BlockSpec((sq, BLOCK, B_WIDTH), lambda b, i: (b, i, cb))
    k = pl.BlockSpec((sq, t, B_WIDTH), lambda b, i: (b, 0, cb + 1))
    v = pl.BlockSpec((sq, t, B_WIDTH), lambda b, i: (b, 0, cb + 2))
    blk = pl.BlockSpec((sq, BLOCK, B_WIDTH), lambda b, i: (b, i, 0))
    full = pl.BlockSpec((sq, t, B_WIDTH), lambda b, i: (b, 0, 0))
    carry = pl.BlockSpec((sq, 1, nq, BLOCK, SB_G), lambda b, i: (b, i, 0, 0, 0))
    return q, k, v, blk, full, carry


def _sb_heads(ref, rows):
    return jnp.concatenate([_heads(ref[b, rows, :], SB_G) for b in range(SB_SEQ)])


def _sb_unheads(a):
    return [_unheads(a[b * SB_G:(b + 1) * SB_G]) for b in range(SB_SEQ)]


def sb_fwd(proj, nseq):
    n = proj.shape[0]
    t = n // nseq
    nq = t // BLOCK
    g, sq = SB_G, SB_SEQ
    everything = pl.ds(0, BLOCK)

    def body(q_ref, k_ref, v_ref, o_ref, carry_ref):
        i = pl.program_id(1)
        tri = _sb_tri()
        qv = _sb_heads(q_ref, everything)

        def tile(j, c, fn):
            rows = pl.ds(pl.multiple_of(j * BLOCK, BLOCK), BLOCK)
            for b in range(sq):
                carry_ref[b, 0, j] = jnp.concatenate([c[b * g + h] for h in range(g)], axis=1)
            return fn(qv, _sb_heads(k_ref, rows), _sb_heads(v_ref, rows), c, tri)

        def step(it, st):
            o_acc, c = st
            o, tot = tile(i - 1 - it, c, _sb_tiles_off)
            return o_acc + o, c + tot

        o_acc, _ = lax.fori_loop(0, i, step, tile(i, jnp.zeros((sq * g, BLOCK, 1), F32), _sb_tiles_diag))
        for b, o in enumerate(_sb_unheads(o_acc)):
            o_ref[b] = o

    q, k, v, blk, _, carry = _sb_specs(t, nq)
    proj3 = proj.reshape(nseq, t, -1)
    o, carries = pl.pallas_call(
        body, name="sb_fwd", grid=(nseq // sq, nq),
        in_specs=[q, k, v],
        out_specs=[blk, carry],
        out_shape=[jax.ShapeDtypeStruct((nseq, t, B_WIDTH), F32),
                   jax.ShapeDtypeStruct((nseq, nq, nq, BLOCK, g), F32)],
        compiler_params=_cparams("parallel", "arbitrary"),
    )(proj3, proj3, proj3)
    return o.reshape(n, B_WIDTH), carries


def sb_bwd(proj, carries, do, nseq):
    n = proj.shape[0]
    t = n // nseq
    nq = t // BLOCK
    g, sq = SB_G, SB_SEQ
    everything = pl.ds(0, BLOCK)

    def body(q_ref, k_ref, v_ref, carry_ref, do_ref, dq_ref, dk_ref, dv_ref):
        i = pl.program_id(1)

        @pl.when(i == 0)
        def _():
            dk_ref[...] = jnp.zeros_like(dk_ref)
            dv_ref[...] = jnp.zeros_like(dv_ref)

        tri = _sb_tri()
        qv = _sb_heads(q_ref, everything)
        dov = _sb_heads(do_ref, everything)

        def tile(j, st, fn):
            dq_acc, dc = st
            rows = pl.ds(pl.multiple_of(j * BLOCK, BLOCK), BLOCK)
            cj = [carry_ref[b, 0, j] for b in range(sq)]
            _, vjp = jax.vjp(lambda q_, k_, v_, c_: fn(q_, k_, v_, c_, tri), qv, _sb_heads(k_ref, rows), _sb_heads(v_ref, rows),
                             jnp.stack([cj[b][:, h:h + 1] for b in range(sq) for h in range(g)]))
            dq, dk, dv, dcj = vjp((dov, dc))
            for b, (dkb, dvb) in enumerate(zip(_sb_unheads(dk), _sb_unheads(dv))):
                dk_ref[b, rows, :] += dkb
                dv_ref[b, rows, :] += dvb
            return dq_acc + dq, dc + dcj

        st = lax.fori_loop(0, i, lambda j, st: tile(j, st, _sb_tiles_off),
                           (jnp.zeros((sq * g, BLOCK, HEAD_DIM), F32), jnp.zeros((sq * g, BLOCK, 1), F32)))
        dq_acc, _ = tile(i, st, _sb_tiles_diag)
        for b, dq in enumerate(_sb_unheads(dq_acc)):
            dq_ref[b] = dq

    q, k, v, blk, full, carry = _sb_specs(t, nq)
    proj3, do3 = proj.reshape(nseq, t, -1), do.reshape(nseq, t, -1)
    grads = pl.pallas_call(
        body, name="sb_bwd", grid=(nseq // sq, nq),
        in_specs=[q, k, v, carry, blk],
        out_specs=[blk, full, full],
        out_shape=[jax.ShapeDtypeStruct((nseq, t, B_WIDTH), F32)] * 3,
        compiler_params=_cparams("parallel", "arbitrary"),
    )(proj3, proj3, proj3, carries, do3)
    return [a.reshape(n, B_WIDTH) for a in grads]


def _unit_lower_inverse(a):
    n = a.shape[0]
    eye = jnp.where(lax.broadcasted_iota(jnp.int32, (n, n), 0) == lax.broadcasted_iota(jnp.int32, (n, n), 1), 1.0, 0.0)
    tmat = eye.astype(F32) - a
    p = a
    for _ in range(5):
        p = _mm32(p, p)
        tmat = tmat + _mm32(tmat, p)
    return tmat


@jax.custom_vjp
def _known_inverse(a, tmat):
    return tmat


def _known_inverse_fwd(a, tmat):
    return tmat, tmat


def _known_inverse_bwd(tmat, g):
    return -_mm32(_mm32(tmat, g, TN), tmat, NT), jnp.zeros_like(tmat)


_known_inverse.defvjp(_known_inverse_fwd, _known_inverse_bwd)


def _gdn_chunk(q, k, v, al_c, al_r, br_c, alog, dtb, nw, s, tmat_in):
    c = GDN_CHUNK
    ri = lax.broadcasted_iota(jnp.int32, (c, c), 0)
    ci = lax.broadcasted_iota(jnp.int32, (c, c), 1)
    incl, strict = ri >= ci, ri > ci
    rate = -jnp.exp(alog)
    g_c = rate * _softplus(al_c + dtb)
    g_r = rate * _softplus(al_r + dtb)
    beta = _sigmoid(br_c)
    gc_c = jnp.sum(jnp.where(incl, g_r, 0.0), axis=1, keepdims=True)
    gc_r = jnp.sum(jnp.where(ri <= ci, g_c, 0.0), axis=0, keepdims=True)
    gl = jnp.sum(g_r, axis=1, keepdims=True)
    decay = jnp.where(incl, jnp.exp(jnp.where(incl, gc_c - gc_r, 0.0)), 0.0)
    qn = q * lax.rsqrt(jnp.sum(q * q, axis=-1, keepdims=True) + RMS_EPS) * (HEAD_DIM ** -0.5)
    kn = k * lax.rsqrt(jnp.sum(k * k, axis=-1, keepdims=True) + RMS_EPS)
    kb = kn * beta
    a = jnp.where(strict, _mm(kb, kn, NT) * decay, 0.0)
    tmat = _unit_lower_inverse(a) if tmat_in is None else _known_inverse(a, tmat_in)
    u = _mm(tmat, v * beta)
    w = _mm(tmat, kb * jnp.exp(gc_c))
    qk = _mm(qn, kn, NT) * decay
    v_new = u - _mm(w, s)
    o = _mm(qn * jnp.exp(gc_c), s) + _mm(qk, v_new)
    s_new = s * jnp.exp(gl) + _mm(kn * jnp.exp(gl - gc_c), v_new, TN)
    o = o * lax.rsqrt(jnp.mean(o * o, axis=-1, keepdims=True) + RMS_EPS) * nw
    return o, s_new, tmat


_gdn_chunks_fwd = jax.vmap(functools.partial(_gdn_chunk, tmat_in=None), in_axes=(0, 0, 0, 0, 0, 0, 0, 0, None, 0))
_gdn_chunks_bwd = jax.vmap(_gdn_chunk, in_axes=(0, 0, 0, 0, 0, 0, 0, 0, None, 0, 0))

GDN_TB = 256
GDN_SEQ_FWD = 2
GDN_SEQ_BWD = 2


def _gdn_block(q3, k3, v3, ba, alog, dtb, nw, s, tm=None):
    nh = N_HEADS_A
    ns = q3.shape[0]
    bat = [ba[b].T for b in range(ns)]
    br_c = jnp.stack([ba[b][:, h:h + 1] for b in range(ns) for h in range(nh)])
    al_c = jnp.stack([ba[b][:, nh + h:nh + h + 1] for b in range(ns) for h in range(nh)])
    al_r = jnp.stack([bat[b][nh + h:nh + h + 1, :] for b in range(ns) for h in range(nh)])
    heads = lambda a: jnp.concatenate([_heads(a[b], nh) for b in range(ns)])
    args = (heads(q3), heads(k3), heads(v3), al_c, al_r, br_c, jnp.concatenate([alog] * ns), jnp.concatenate([dtb] * ns), nw, s)
    o, s_new, tmat = _gdn_chunks_fwd(*args) if tm is None else _gdn_chunks_bwd(*args, tm)
    o3 = jnp.stack([_unheads(o[b * nh:(b + 1) * nh]) for b in range(ns)])
    return (o3, s_new, tmat) if tm is None else (o3, s_new)


def _gdn_specs(nt, sq, rev):
    tpos = (lambda i: nt - 1 - i) if rev else (lambda i: i)
    ncb = GDN_TB // GDN_CHUNK
    qkv = [pl.BlockSpec((sq, GDN_TB, A_WIDTH), lambda b, i, j=j: (b, tpos(i), j)) for j in range(3)]
    ba = pl.BlockSpec((sq, GDN_TB, BLOCK), lambda b, i: (b, tpos(i), COL_BA // BLOCK))
    one = pl.BlockSpec((N_HEADS_A, 1, 1), lambda b, i: (0, 0, 0))
    vec = pl.BlockSpec((1, HEAD_DIM), lambda b, i: (0, 0))
    st = pl.BlockSpec((sq, ncb, N_HEADS_A, HEAD_DIM, HEAD_DIM), lambda b, i: (b, tpos(i), 0, 0, 0))
    oa = pl.BlockSpec((sq, GDN_TB, A_WIDTH), lambda b, i: (b, tpos(i), 0))
    return qkv, ba, one, vec, st, oa, tpos


def gdn_fwd(ya, proj, alog, dtb, nw, nseq):
    n = ya.shape[0]
    t = n // nseq
    nc, nt, ncb = t // GDN_CHUNK, t // GDN_TB, GDN_TB // GDN_CHUNK
    sq = GDN_SEQ_FWD
    nh = N_HEADS_A

    def body(q_ref, k_ref, v_ref, ba_ref, alog_ref, dtb_ref, nw_ref, o_ref, st_ref, tm_ref, s_s):
        @pl.when(pl.program_id(1) == 0)
        def _():
            s_s[...] = jnp.zeros_like(s_s)

        def step(c, s):
            rows = pl.ds(pl.multiple_of(c * GDN_CHUNK, GDN_CHUNK), GDN_CHUNK)
            o, s_new, tmat = _gdn_block(q_ref[:, rows, :], k_ref[:, rows, :], v_ref[:, rows, :], ba_ref[:, rows, :],
                                        alog_ref[...], dtb_ref[...], nw_ref[...], s)
            for b in range(sq):
                st_ref[b, c] = s[b * nh:(b + 1) * nh]
                tm_ref[b, c] = tmat[b * nh:(b + 1) * nh]
            o_ref[:, rows, :] = o
            return s_new

        s_s[...] = lax.fori_loop(0, ncb, step, s_s[...])

    qkv, ba, one, vec, st, oa, _ = _gdn_specs(nt, sq, False)
    ya3, proj3 = ya.reshape(nseq, t, -1), proj.reshape(nseq, t, -1)
    per_chunk = jax.ShapeDtypeStruct((nseq, nc, nh, HEAD_DIM, HEAD_DIM), F32)
    o, states, inverses = pl.pallas_call(
        body, name="gdn_fwd", grid=(nseq // sq, nt),
        in_specs=qkv + [ba, one, one, vec],
        out_specs=[oa, st, st],
        out_shape=[jax.ShapeDtypeStruct((nseq, t, A_WIDTH), F32), per_chunk, per_chunk],
        scratch_shapes=[pltpu.VMEM((sq * nh, HEAD_DIM, HEAD_DIM), F32)],
        compiler_params=_cparams("parallel", "arbitrary"),
    )(ya3, ya3, ya3, proj3, alog, dtb, nw)
    return o.reshape(n, A_WIDTH), states, inverses


def gdn_bwd(ya, proj, alog, dtb, nw, states, inverses, do, nseq):
    n = ya.shape[0]
    t = n // nseq
    nt, ncb = t // GDN_TB, GDN_TB // GDN_CHUNK
    nh = N_HEADS_A
    sq = GDN_SEQ_BWD

    def body(q_ref, k_ref, v_ref, ba_ref, alog_ref, dtb_ref, nw_ref, st_ref, tm_ref, do_ref,
             dya_ref, dba_ref, dalog_ref, ddtb_ref, dnw_ref, ds_s):
        @pl.when(pl.program_id(1) == 0)
        def _():
            ds_s[...] = jnp.zeros_like(ds_s)
            dalog_ref[...] = jnp.zeros_like(dalog_ref)
            ddtb_ref[...] = jnp.zeros_like(ddtb_ref)
            dnw_ref[...] = jnp.zeros_like(dnw_ref)

        def step(it, carry):
            ds, dalog, ddtb, dnw = carry
            c = ncb - 1 - it
            rows = pl.ds(pl.multiple_of(c * GDN_CHUNK, GDN_CHUNK), GDN_CHUNK)
            s_in = jnp.concatenate([st_ref[b, c] for b in range(sq)])
            tm_in = jnp.concatenate([tm_ref[b, c] for b in range(sq)])
            _, vjp = jax.vjp(functools.partial(_gdn_block, tm=tm_in), q_ref[:, rows, :], k_ref[:, rows, :], v_ref[:, rows, :],
                             ba_ref[:, rows, :], alog_ref[...], dtb_ref[...], nw_ref[...], s_in)
            dq, dk, dv, dba, da, dd, dn, ds = vjp((do_ref[:, rows, :], ds))
            dya_ref[:, rows, 0:A_WIDTH] = dq
            dya_ref[:, rows, A_WIDTH:2 * A_WIDTH] = dk
            dya_ref[:, rows, 2 * A_WIDTH:3 * A_WIDTH] = dv
            dba_ref[:, rows, :] = dba
            return ds, dalog + da, ddtb + dd, dnw + dn

        z11 = jnp.zeros((nh, 1, 1), F32)
        ds, dalog, ddtb, dnw = lax.fori_loop(0, ncb, step, (ds_s[...], z11, z11, jnp.zeros((1, HEAD_DIM), F32)))
        ds_s[...] = ds
        dalog_ref[0] += dalog
        ddtb_ref[0] += ddtb
        dnw_ref[0] += dnw

    qkv, ba, one, vec, st, oa, tpos = _gdn_specs(nt, sq, True)
    per_grp = pl.BlockSpec((1, nh, 1, 1), lambda b, i: (b, 0, 0, 0))
    sd = jax.ShapeDtypeStruct
    ya3, proj3, do3 = ya.reshape(nseq, t, -1), proj.reshape(nseq, t, -1), do.reshape(nseq, t, -1)
    dya, dba, dalog, ddtb, dnw = pl.pallas_call(
        body, name="gdn_bwd", grid=(nseq // sq, nt),
        in_specs=qkv + [ba, one, one, vec, st, st, oa],
        out_specs=[pl.BlockSpec((sq, GDN_TB, 3 * A_WIDTH), lambda b, i: (b, tpos(i), 0)),
                   pl.BlockSpec((sq, GDN_TB, BLOCK), lambda b, i: (b, tpos(i), 0)),
                   per_grp, per_grp, pl.BlockSpec((1, 1, HEAD_DIM), lambda b, i: (b, 0, 0))],
        out_shape=[sd((nseq, t, 3 * A_WIDTH), F32), sd((nseq, t, BLOCK), F32), sd((nseq // sq, nh, 1, 1), F32),
                   sd((nseq // sq, nh, 1, 1), F32), sd((nseq // sq, 1, HEAD_DIM), F32)],
        scratch_shapes=[pltpu.VMEM((sq * nh, HEAD_DIM, HEAD_DIM), F32)],
        compiler_params=_cparams("parallel", "arbitrary"),
    )(ya3, ya3, ya3, proj3, alog, dtb, nw, states, inverses, do3)
    return dya.reshape(n, 3 * A_WIDTH), dba.reshape(n, BLOCK), dalog, ddtb, dnw


DIL_NB = tuple((SEQ // d) // BLOCK for _, d in DILATED_PAIRS)
DIL_D = tuple(d for _, d in DILATED_PAIRS)
DIL_STEPS = tuple(w // d for w, d in DILATED_PAIRS)
DIL_B = 4
PAIR = 2 * HEAD_DIM


def _rope_tables(t):
    half = ROPE_DIM // 2
    inv_freq = ROPE_THETA ** (-jnp.arange(half, dtype=F32) / half)
    ang = jnp.arange(t, dtype=F32)[:, None] * inv_freq[None, :]
    ones = jnp.ones((t, HEAD_DIM - ROPE_DIM), F32)
    cs = jnp.concatenate([jnp.cos(ang), jnp.cos(ang), ones], axis=1)
    sn = jnp.concatenate([jnp.sin(ang), jnp.sin(ang), 0.0 * ones], axis=1)
    i = jnp.arange(PAIR)[:, None]
    j = jnp.arange(PAIR)[None, :]
    same = (i // HEAD_DIM) == (j // HEAD_DIM)
    ih, jh = i % HEAD_DIM, j % HEAD_DIM
    pm = (jnp.where(same & (jh < half) & (ih == jh + half), -1.0, 0.0)
          + jnp.where(same & (jh >= half) & (jh < ROPE_DIM) & (ih == jh - half), 1.0, 0.0))
    mean = jnp.where(same, 1.0 / HEAD_DIM, 0.0)
    twice = lambda m: jnp.concatenate([m, m]).astype(BF16)
    return jnp.tile(cs, (1, 2)), jnp.tile(sn, (1, 2)), twice(mean), twice(pm)


def _split_dot(x, w2):
    hi = x.astype(BF16)
    lo = lax.stop_gradient(x - hi.astype(F32)).astype(BF16)
    return lax.dot_general(jnp.concatenate([hi, lo], axis=1), w2, NN, preferred_element_type=F32)


def _dil_prep(x, w, cs, sn, mean2, pm2):
    y = x * lax.rsqrt(_split_dot(x * x, mean2) + RMS_EPS) * w
    return y * cs + _split_dot(y, pm2) * sn


def _dil_tile(qn, kk, vv, bias):
    lane = lax.broadcasted_iota(jnp.int32, (1, PAIR), 1)
    outs, lses = [], []
    for h in range(2):
        s = _mm(jnp.where(lane // HEAD_DIM == h, qn, 0.0) * (HEAD_DIM ** -0.5), kk, NT) + bias
        m = lax.stop_gradient(jnp.max(s, axis=-1, keepdims=True))
        p = jnp.exp(s - m)
        denom = jnp.sum(p, axis=-1, keepdims=True)
        outs.append(_mm(p, vv) / denom)
        lses.append(m + jnp.log(denom))
    return jnp.where(lane < HEAD_DIM, outs[0], outs[1]), jnp.concatenate(lses, axis=1)


_dil_tiles = jax.vmap(_dil_tile)


def _spread(a):
    lane = lax.broadcasted_iota(jnp.int32, (a.shape[0], PAIR), 1)
    return jnp.where(lane < HEAD_DIM, a[:, 0:1], a[:, 1:2])


def _dil_mix(o1, o2, o3, l1, l2, l3):
    m = lax.stop_gradient(jnp.maximum(jnp.maximum(l1, l2), l3))
    e1, e2, e3 = jnp.exp(l1 - m), jnp.exp(l2 - m), jnp.exp(l3 - m)
    r = 1.0 / (e1 + e2 + e3)
    return _spread(e1 * r) * o1 + _spread(e2 * r) * o2 + _spread(e3 * r) * o3


def _dil_fill_biases(bias_s):
    steps, = set(DIL_STEPS)
    qi = lax.broadcasted_iota(jnp.int32, (BLOCK, 1), 0)
    kj = lax.broadcasted_iota(jnp.int32, (1, 2 * BLOCK), 1)
    rel = qi - kj + BLOCK
    inside = (rel >= 0) & (rel <= steps)
    bias_s[0] = jnp.where(inside, 0.0, NEG)
    bias_s[1] = jnp.where(inside & (kj >= BLOCK), 0.0, NEG)
    bias_s[2] = jnp.where((qi >= kj) & (qi - kj <= steps), 0.0, NEG)


def _dil_mask(it, g, bias_s):
    qrows = pl.ds(pl.multiple_of(it * BLOCK, BLOCK), BLOCK)
    if DIL_NB[g] == 1:
        return bias_s[2, :, 0:BLOCK], qrows, qrows
    which = jnp.where(it == 0, 2, jnp.where(it % DIL_NB[g] == 0, 1, 0))
    kstart = jnp.maximum(it - 1, 0) * BLOCK
    return bias_s[which], qrows, pl.ds(pl.multiple_of(kstart, BLOCK), 2 * BLOCK)


def _dil_gather(src, dst, d):
    t = src.shape[0]
    ln = t // d
    for r in range(d):
        dst[pl.ds(r * ln, ln), :] = src[pl.ds(r, ln, stride=d), :]


def _dil_scatter(src, dst, d):
    t = src.shape[0]
    ln = t // d
    for r in range(d):
        dst[pl.ds(r, ln, stride=d), :] = src[pl.ds(r * ln, ln), :]


def _dil_forward_parts(q_ref, k_ref, v_ref, qw, kw, cs_ref, sn_ref, mean2, pm2, qn_s, kn_s, dl_s, od_s, ld_s, on_s, ln_s, bias_s):
    t = qn_s.shape[0]
    _dil_fill_biases(bias_s)

    def prep(c, _):
        rows = pl.ds(pl.multiple_of(c * ROWS, ROWS), ROWS)
        qn_s[rows, :] = _dil_prep(q_ref[rows, :], qw, cs_ref[rows, :], sn_ref[rows, :], mean2, pm2)
        kn_s[rows, :] = _dil_prep(k_ref[rows, :], kw, cs_ref[rows, :], sn_ref[rows, :], mean2, pm2)
        return 0

    lax.fori_loop(0, t // ROWS, prep, 0)
    for g in (1, 2):
        _dil_gather(qn_s, dl_s.at[g - 1, 0], DIL_D[g])
        _dil_gather(kn_s, dl_s.at[g - 1, 1], DIL_D[g])
        _dil_gather(v_ref, dl_s.at[g - 1, 2], DIL_D[g])
    for g in range(3):
        qs = qn_s if g == 0 else dl_s.at[g - 1, 0]
        ks = kn_s if g == 0 else dl_s.at[g - 1, 1]
        vs = v_ref if g == 0 else dl_s.at[g - 1, 2]

        def tiles(i, _, g=g, qs=qs, ks=ks, vs=vs):
            where = [_dil_mask(i * DIL_B + b, g, bias_s) for b in range(DIL_B)]
            o, lse = _dil_tiles(jnp.stack([qs[qr, :] for _, qr, _ in where]), jnp.stack([ks[kr, :] for _, _, kr in where]),
                                jnp.stack([vs[kr, :] for _, _, kr in where]), jnp.stack([m for m, _, _ in where]))
            for b, (_, qr, _) in enumerate(where):
                od_s[g, qr, :] = o[b]
                ld_s[g, qr, :] = lse[b]
            return 0

        lax.fori_loop(0, t // BLOCK // DIL_B, tiles, 0)
    for g in (1, 2):
        _dil_scatter(od_s.at[g], on_s.at[g - 1], DIL_D[g])
        _dil_scatter(ld_s.at[g], ln_s.at[g - 1], DIL_D[g])


def _dil_scratch(t):
    return [pltpu.VMEM((t, PAIR), F32), pltpu.VMEM((t, PAIR), F32),
            pltpu.VMEM((2, 3, t, PAIR), F32),
            pltpu.VMEM((3, t, PAIR), F32), pltpu.VMEM((3, t, 2), F32),
            pltpu.VMEM((2, t, PAIR), F32), pltpu.VMEM((2, t, 2), F32),
            pltpu.VMEM((3, BLOCK, 2 * BLOCK), F32)]


def _dil_specs(t):
    cb = COL_C // BLOCK
    per = C_WIDTH // BLOCK
    qkv = [pl.BlockSpec((t, BLOCK), lambda b, p, j=j: (b, cb + j * per + p)) for j in range(3)]
    vec = pl.BlockSpec((1, PAIR), lambda b, p: (0, 0))
    tab = pl.BlockSpec((t, PAIR), lambda b, p: (0, 0))
    mat = pl.BlockSpec((2 * PAIR, PAIR), lambda b, p: (0, 0))
    pair = pl.BlockSpec((t, BLOCK), lambda b, p: (b, p))
    return qkv, vec, tab, mat, pair


def dil_fwd(proj, qw, kw, cs, sn, mean2, pm2, nseq):
    n = proj.shape[0]
    t = n // nseq

    def body(q_ref, k_ref, v_ref, qw_ref, kw_ref, cs_ref, sn_ref, mean_ref, pm_ref, o_ref,
             qn_s, kn_s, dl_s, od_s, ld_s, on_s, ln_s, bias_s):
        _dil_forward_parts(q_ref, k_ref, v_ref, qw_ref[...], kw_ref[...], cs_ref, sn_ref, mean_ref[...], pm_ref[...],
                           qn_s, kn_s, dl_s, od_s, ld_s, on_s, ln_s, bias_s)

        def mix(c, _):
            rows = pl.ds(pl.multiple_of(c * ROWS, ROWS), ROWS)
            o_ref[rows, :] = _dil_mix(od_s[0, rows, :], on_s[0, rows, :], on_s[1, rows, :],
                                      ld_s[0, rows, :], ln_s[0, rows, :], ln_s[1, rows, :])
            return 0

        lax.fori_loop(0, t // ROWS, mix, 0)

    qkv, vec, tab, mat, pair = _dil_specs(t)
    return pl.pallas_call(
        body, name="dil_fwd", grid=(nseq, C_WIDTH // BLOCK),
        in_specs=qkv + [vec, vec, tab, tab, mat, mat],
        out_specs=pair,
        out_shape=jax.ShapeDtypeStruct((n, C_WIDTH), F32),
        scratch_shapes=_dil_scratch(t),
        compiler_params=_cparams("parallel", "parallel"),
    )(proj, proj, proj, qw, kw, cs, sn, mean2, pm2)


def dil_bwd(proj, qw, kw, cs, sn, mean2, pm2, do, nseq):
    n = proj.shape[0]
    t = n // nseq

    def body(q_ref, k_ref, v_ref, qw_ref, kw_ref, cs_ref, sn_ref, mean_ref, pm_ref, do_ref,
             dq_ref, dk_ref, dv_ref, dqw_ref, dkw_ref,
             qn_s, kn_s, dl_s, od_s, ld_s, on_s, ln_s, bias_s, tq_s, tk_s, tv_s):
        qw, kw, mean2, pm2 = qw_ref[...], kw_ref[...], mean_ref[...], pm_ref[...]
        _dil_forward_parts(q_ref, k_ref, v_ref, qw, kw, cs_ref, sn_ref, mean2, pm2, qn_s, kn_s, dl_s, od_s, ld_s, on_s, ln_s, bias_s)

        def mix(c, _):
            rows = pl.ds(pl.multiple_of(c * ROWS, ROWS), ROWS)
            _, vjp = jax.vjp(_dil_mix, od_s[0, rows, :], on_s[0, rows, :], on_s[1, rows, :],
                             ld_s[0, rows, :], ln_s[0, rows, :], ln_s[1, rows, :])
            d1, d2, d3, e1, e2, e3 = vjp(do_ref[rows, :])
            od_s[0, rows, :] = d1
            on_s[0, rows, :] = d2
            on_s[1, rows, :] = d3
            ld_s[0, rows, :] = e1
            ln_s[0, rows, :] = e2
            ln_s[1, rows, :] = e3
            return 0

        lax.fori_loop(0, t // ROWS, mix, 0)
        for g in (1, 2):
            _dil_gather(on_s.at[g - 1], od_s.at[g], DIL_D[g])
            _dil_gather(ln_s.at[g - 1], ld_s.at[g], DIL_D[g])
        on_s[...] = jnp.zeros_like(on_s)
        dv_ref[...] = jnp.zeros_like(dv_ref)
        for g in range(3):
            qs = qn_s if g == 0 else dl_s.at[g - 1, 0]
            ks = kn_s if g == 0 else dl_s.at[g - 1, 1]
            vs = v_ref if g == 0 else dl_s.at[g - 1, 2]
            gq = on_s.at[0] if g == 0 else tq_s
            gk = on_s.at[1] if g == 0 else tk_s
            gv = dv_ref if g == 0 else tv_s
            if g > 0:
                tk_s[...] = jnp.zeros_like(tk_s)
                tv_s[...] = jnp.zeros_like(tv_s)

            def tiles(i, _, g=g, qs=qs, ks=ks, vs=vs, gq=gq, gk=gk, gv=gv):
                where = [_dil_mask(i * DIL_B + b, g, bias_s) for b in range(DIL_B)]
                biases = jnp.stack([m for m, _, _ in where])
                _, vjp = jax.vjp(lambda q_, k_, v_: _dil_tiles(q_, k_, v_, biases),
                                 jnp.stack([qs[qr, :] for _, qr, _ in where]), jnp.stack([ks[kr, :] for _, _, kr in where]),
                                 jnp.stack([vs[kr, :] for _, _, kr in where]))
                dq, dkk, dvv = vjp((jnp.stack([od_s[g, qr, :] for _, qr, _ in where]),
                                    jnp.stack([ld_s[g, qr, :] for _, qr, _ in where])))
                for b, (_, qr, kr) in enumerate(where):
                    gq[qr, :] = dq[b]
                    gk[kr, :] += dkk[b]
                    gv[kr, :] += dvv[b]
                return 0

            lax.fori_loop(0, t // BLOCK // DIL_B, tiles, 0)
            if g > 0:
                d = DIL_D[g]
                ln = t // d
                for r in range(d):
                    nat, dil = pl.ds(r, ln, stride=d), pl.ds(r * ln, ln)
                    on_s[0, nat, :] += tq_s[dil, :]
                    on_s[1, nat, :] += tk_s[dil, :]
                    dv_ref[nat, :] += tv_s[dil, :]

        def prep(c, acc):
            rows = pl.ds(pl.multiple_of(c * ROWS, ROWS), ROWS)
            f = lambda x, w: _dil_prep(x, w, cs_ref[rows, :], sn_ref[rows, :], mean2, pm2)
            _, vq = jax.vjp(f, q_ref[rows, :], qw)
            _, vk = jax.vjp(f, k_ref[rows, :], kw)
            dq, dqw = vq(on_s[0, rows, :])
            dk, dkw = vk(on_s[1, rows, :])
            dq_ref[rows, :] = dq
            dk_ref[rows, :] = dk
            return acc[0] + dqw, acc[1] + dkw

        dqw, dkw = lax.fori_loop(0, t // ROWS, prep, (jnp.zeros((1, PAIR), F32), jnp.zeros((1, PAIR), F32)))
        dqw_ref[0] = dqw
        dkw_ref[0] = dkw

    qkv, vec, tab, mat, pair = _dil_specs(t)
    per = C_WIDTH // BLOCK
    wout = pl.BlockSpec((1, 1, PAIR), lambda b, p: (b * per + p, 0, 0))
    return pl.pallas_call(
        body, name="dil_bwd", grid=(nseq, per),
        in_specs=qkv + [vec, vec, tab, tab, mat, mat, pair],
        out_specs=[pair, pair, pair, wout, wout],
        out_shape=[jax.ShapeDtypeStruct((n, C_WIDTH), F32)] * 3 + [jax.ShapeDtypeStruct((nseq * per, 1, PAIR), F32)] * 2,
        scratch_shapes=_dil_scratch(t) + [pltpu.VMEM((t, PAIR), F32)] * 3,
        compiler_params=_cparams("parallel", "parallel"),
    )(proj, proj, proj, qw, kw, cs, sn, mean2, pm2, do)


N_CHIPS = 4
SUM_ROWS = 432
MESH_IDS = pl.DeviceIdType.MESH
ANY = pl.BlockSpec(memory_space=pl.ANY)


def plane_exchange(src, all_to_all):
    blk_shape = src.shape[1:] if all_to_all else src.shape

    def body(src_ref, out_ref, send_sems, recv_sems, local_sem):
        x, y, c = lax.axis_index("x"), lax.axis_index("y"), lax.axis_index("c")
        me = 2 * x + y
        mine = pltpu.make_async_copy(src_ref.at[me] if all_to_all else src_ref, out_ref.at[me], local_sem)
        mine.start()
        sends = []
        for k in (1, 2, 3):
            px = 1 - x if k & 2 else x
            py = 1 - y if k & 1 else y
            peer = 2 * px + py
            cp = pltpu.make_async_remote_copy(
                src_ref=src_ref.at[peer] if all_to_all else src_ref, dst_ref=out_ref.at[me],
                send_sem=send_sems.at[k - 1], recv_sem=recv_sems.at[k - 1],
                device_id=(px, py, c), device_id_type=MESH_IDS)
            cp.start()
            sends.append((cp, peer, (px, py, c)))
        for k, (cp, peer, dev) in enumerate(sends):
            pltpu.make_async_remote_copy(
                src_ref=out_ref.at[me], dst_ref=out_ref.at[peer],
                send_sem=send_sems.at[k], recv_sem=recv_sems.at[k],
                device_id=dev, device_id_type=MESH_IDS).wait_recv()
        for cp, _, _ in sends:
            cp.wait_send()
        mine.wait()

    return pl.pallas_call(
        body, name="plane_all_to_all" if all_to_all else "plane_all_gather",
        in_specs=[ANY], out_specs=ANY,
        out_shape=jax.ShapeDtypeStruct((N_CHIPS,) + blk_shape, src.dtype),
        scratch_shapes=[pltpu.SemaphoreType.DMA((3,)), pltpu.SemaphoreType.DMA((3,)), pltpu.SemaphoreType.DMA],
    )(src)


def sibling_swap(src, other_half=False):
    shape = (src.shape[0], src.shape[1] // 2) + src.shape[2:] if other_half else src.shape

    def body(src_ref, out_ref, send_sem, recv_sem):
        x, y, c = lax.axis_index("x"), lax.axis_index("y"), lax.axis_index("c")
        part = src_ref.at[:, pl.ds((1 - c) * shape[1], shape[1])] if other_half else src_ref
        cp = pltpu.make_async_remote_copy(src_ref=part, dst_ref=out_ref, send_sem=send_sem, recv_sem=recv_sem,
                                          device_id=(x, y, 1 - c), device_id_type=MESH_IDS)
        cp.start()
        cp.wait()

    return pl.pallas_call(
        body, name="sibling_swap", in_specs=[ANY], out_specs=ANY,
        out_shape=jax.ShapeDtypeStruct(shape, src.dtype),
        scratch_shapes=[pltpu.SemaphoreType.DMA, pltpu.SemaphoreType.DMA],
    )(src)


def sum4(a):
    _, r, c = a.shape
    tr = SUM_ROWS

    def body(a_ref, o_ref):
        p = [a_ref[i].astype(F32) for i in range(N_CHIPS)]
        o_ref[...] = (p[0] + p[1]) + (p[2] + p[3])

    return pl.pallas_call(
        body, name="sum4", grid=(r // tr,),
        in_specs=[pl.BlockSpec((N_CHIPS, tr, c), lambda i: (0, i, 0))],
        out_specs=pl.BlockSpec((tr, c), lambda i: (i, 0)),
        out_shape=jax.ShapeDtypeStruct((r, c), F32),
        compiler_params=_cparams("parallel"),
    )(a)


def add_my_half(mine, got, c):
    nchip, r2, cols = mine.shape
    nt = r2 // 2 // SUM_ROWS

    def body(c_ref, a_ref, b_ref, o_ref):
        o_ref[...] = (a_ref[...] + b_ref[...]).astype(BF16)

    blk = pl.BlockSpec((1, SUM_ROWS, cols), lambda j, i, c_ref: (j, i, 0))
    return pl.pallas_call(
        body, name="add_my_half",
        grid_spec=pltpu.PrefetchScalarGridSpec(
            num_scalar_prefetch=1, grid=(nchip, nt),
            in_specs=[pl.BlockSpec((1, SUM_ROWS, cols), lambda j, i, c_ref: (j, c_ref[0] * nt + i, 0)), blk],
            out_specs=blk),
        out_shape=jax.ShapeDtypeStruct((nchip, r2 // 2, cols), BF16),
        compiler_params=_cparams("parallel", "parallel"),
    )(jnp.reshape(c, (1,)).astype(jnp.int32), mine, got)


PACK_COLS = 1152
PACK_ROWS = 2592
ROW_TILE = 16


def _pack(parts):
    blocks = []
    for p in parts:
        p2 = p.reshape(-1, p.shape[-1])
        blocks.append(jnp.pad(p2, ((0, -p2.shape[0] % ROW_TILE), (0, PACK_COLS - p2.shape[1]))))
    rows = sum(b.shape[0] for b in blocks)
    blocks.append(jnp.zeros((PACK_ROWS - rows, PACK_COLS), blocks[0].dtype))
    return jnp.concatenate(blocks)


def _unpack(buf, shapes):
    out, at = [], 0
    for s in shapes:
        rows = math.prod(s[:-1])
        out.append(buf[at:at + rows, :s[-1]].reshape(s))
        at += rows + (-rows % ROW_TILE)
    return out


def _pack_small(g):
    blk = jnp.zeros((ROW_TILE, PACK_COLS), F32)
    for i, k in enumerate(SMALL):
        blk = blk.at[2 * i:2 * i + 2, :g[k].shape[1]].set(g[k])
    return blk


def _unpack_small(blk, shapes):
    return [blk[2 * i:2 * i + 2, :s[1]] for i, s in enumerate(shapes)]


def _layer_fwd(x, p, nseq, tabs):
    proj, hdn = inproj_fwd(x, p["norm_w"][None], p["w_in"])
    ya = conv_fwd(proj, p["conv_w"], nseq)
    oa, states, inverses = gdn_fwd(ya, proj, p["a_log"].reshape(N_HEADS_A, 1, 1), p["dt_bias"].reshape(N_HEADS_A, 1, 1),
                         p["gdn_norm_w"][None], nseq)
    ob, carries = sb_fwd(proj, nseq)
    oc = dil_fwd(proj, jnp.tile(p["q_norm_w"], 2)[None], jnp.tile(p["k_norm_w"], 2)[None], *tabs, nseq)
    y, mixed = outproj_fwd(x, oa, ob, oc, proj, p["w_out"])
    return y, dict(x=x, hdn=hdn, proj=proj, ya=ya, states=states, inverses=inverses, carries=carries, oa=oa, ob=ob, oc=oc, mixed=mixed)


def _layer_bwd(dy, p, res, nseq, tabs):
    proj = res["proj"]
    g = {}
    g["w_out"] = mat_tn(res["mixed"], [dy])[0]
    doa, dob, doc, dza, dzb, dzc = outproj_bwd(dy, res["oa"], res["ob"], res["oc"], proj, p["w_out"])
    dqc, dkc, dvc, dqw, dkw = dil_bwd(proj, jnp.tile(p["q_norm_w"], 2)[None], jnp.tile(p["k_norm_w"], 2)[None], *tabs, doc,
                                      nseq)
    g["q_norm_w"], g["k_norm_w"] = dqw.reshape(-1, HEAD_DIM).sum(0), dkw.reshape(-1, HEAD_DIM).sum(0)
    dqb, dkb, dvb = sb_bwd(proj, res["carries"], dob, nseq)
    dya, dba, dalog, ddtb, dnw = gdn_bwd(res["ya"], proj, p["a_log"].reshape(N_HEADS_A, 1, 1),
                                         p["dt_bias"].reshape(N_HEADS_A, 1, 1), p["gdn_norm_w"][None], res["states"], res["inverses"], doa,
                                         nseq)
    g["a_log"], g["dt_bias"], g["gdn_norm_w"] = dalog.sum(0).reshape(-1), ddtb.sum(0).reshape(-1), dnw.sum((0, 1))
    dqkv, dcw = conv_bwd(proj, p["conv_w"], dya, nseq)
    g["conv_w"] = dcw.sum(0)
    slabs = [dqkv, dza, dqc, dkc, dvc, dzc, dqb, dkb, dvb, dzb, dba]
    hdn = res["hdn"]
    g["w_in"] = jnp.concatenate(mat_tn(hdn, slabs[:6]) + mat_tn(hdn, slabs[6:]), axis=1)
    dx, dnw_tiles = inproj_bwd(slabs, p["w_in"], res["x"], p["norm_w"][None], dy)
    g["norm_w"] = dnw_tiles.sum((0, 1))
    return dx, g


SMALL = ("norm_w", "a_log", "dt_bias", "gdn_norm_w", "q_norm_w", "k_norm_w")


def _local_step(x, target, full):
    nseq, t, d = x.shape
    tabs = _rope_tables(t)
    h = x.reshape(nseq * t, d)
    saved = []
    for l in range(DEPTH):
        p = {k: v[l] for k, v in full.items()}
        h, res = _layer_fwd(h, p, nseq, tabs)
        saved.append((p, res))
    dy, parts = loss_fwd_bwd(h, target.reshape(nseq * t, d))
    loss = parts[:, 0, 0].sum()
    grads = [None] * DEPTH
    for l in reversed(range(DEPTH)):
        p, res = saved[l]
        dy, grads[l] = _layer_bwd(dy, p, res, nseq, tabs)
    return loss, dy.reshape(nseq, t, d), {k: jnp.stack([g[k] for g in grads]) for k in grads[0]}


def _pad_cols(w):
    b0 = ORIG_A + ORIG_BA
    c0 = b0 + ORIG_B
    zeros = jnp.zeros(w.shape[:-1] + (BLOCK - ORIG_BA,), w.dtype)
    return jnp.concatenate([w[..., :ORIG_A], w[..., c0:], w[..., b0:c0], w[..., ORIG_A:b0], zeros], axis=-1)


def _unpad_cols(w):
    return jnp.concatenate([w[..., :COL_C], w[..., COL_BA:COL_BA + ORIG_BA], w[..., COL_B:COL_BA], w[..., COL_C:COL_B]],
                           axis=-1)


def kernel(x, norm_w, w_in, conv_w, a_log, dt_bias, gdn_norm_w, q_norm_w, k_norm_w, w_out, loss_target, m_norm_w, m_w_in, m_conv_w, m_a_log, m_dt_bias, m_gdn_norm_w, m_q_norm_w, m_k_norm_w, m_w_out, v_norm_w, v_w_in, v_conv_w, v_a_log, v_dt_bias, v_gdn_norm_w, v_q_norm_w, v_k_norm_w, v_w_out):
    weights = dict(norm_w=norm_w, w_in=w_in, conv_w=conv_w, a_log=a_log, dt_bias=dt_bias, gdn_norm_w=gdn_norm_w,
                   q_norm_w=q_norm_w, k_norm_w=k_norm_w, w_out=w_out)
    moms = dict(norm_w=m_norm_w, w_in=m_w_in, conv_w=m_conv_w, a_log=m_a_log, dt_bias=m_dt_bias,
                gdn_norm_w=m_gdn_norm_w, q_norm_w=m_q_norm_w, k_norm_w=m_k_norm_w, w_out=m_w_out)
    vars_ = dict(norm_w=v_norm_w, w_in=v_w_in, conv_w=v_conv_w, a_log=v_a_log, dt_bias=v_dt_bias,
                 gdn_norm_w=v_gdn_norm_w, q_norm_w=v_q_norm_w, k_norm_w=v_k_norm_w, w_out=v_w_out)
    names = list(weights)
    sharded = ("w_in", "w_out", "conv_w")
    shard_shapes = [weights[k].shape for k in sharded]

    c = lax.axis_index("c")
    half = PACK_ROWS // 2
    conv_bits = lax.bitcast_convert_type(conv_w, BF16).reshape(conv_w.shape[:2] + (2 * conv_w.shape[2],))
    shard = _pack([w_in.astype(BF16), w_out.astype(BF16), conv_bits])
    mine = plane_exchange(lax.dynamic_slice_in_dim(shard, c * half, half, axis=0), all_to_all=False)
    other = sibling_swap(mine)
    got = jnp.concatenate([jnp.where(c == 0, mine, other), jnp.where(c == 0, other, mine)], axis=1)
    per_chip = [_unpack(got[i], shard_shapes[:2] + [conv_bits.shape]) for i in range(N_CHIPS)]
    full = {k: weights[k] for k in SMALL}
    full["w_in"] = _pad_cols(jnp.concatenate([pc[0] for pc in per_chip], axis=2))
    full["w_out"] = jnp.concatenate([pc[1] for pc in per_chip], axis=1)
    full["conv_w"] = jnp.concatenate(
        [lax.bitcast_convert_type(pc[2].reshape(conv_w.shape + (2,)), F32) for pc in per_chip], axis=2)

    loss, grad_x, g = _local_step(x, loss_target, full)

    gw_in = _unpad_cols(g["w_in"])
    cols, rows = w_in.shape[2], w_out.shape[1]
    small = _pack_small(g)
    send = jnp.stack([_pack([gw_in[:, :, i * cols:(i + 1) * cols], g["w_out"][:, i * rows:(i + 1) * rows],
                             g["conv_w"][:, :, i * conv_w.shape[2]:(i + 1) * conv_w.shape[2]], small])
                      for i in range(N_CHIPS)])
    chip_sum = add_my_half(send, sibling_swap(send, other_half=True), c)
    mine = sum4(plane_exchange(chip_sum, all_to_all=True))
    other = sibling_swap(mine)
    total = jnp.concatenate([jnp.where(c == 0, mine, other), jnp.where(c == 0, other, mine)])
    reduced = _unpack(total, shard_shapes + [(ROW_TILE, PACK_COLS)])
    grads = dict(zip(sharded, reduced[:3]))
    grads.update(zip(SMALL, _unpack_small(reduced[3], [weights[k].shape for k in SMALL])))
    loss = lax.psum(loss, ("x", "y", "c"))

    def two_d(a):
        return a.reshape(-1, a.shape[-1])

    delta, new_m, new_v = {}, {}, {}
    for k in names:
        d_, m_, v_ = adamw(two_d(weights[k]), two_d(grads[k]), two_d(moms[k]), two_d(vars_[k]))
        delta[k], new_m[k], new_v[k] = (a.reshape(weights[k].shape) for a in (d_, m_, v_))
    return (loss, grad_x, *[grads[k] for k in names], *[delta[k] for k in names],
            *[new_m[k] for k in names], *[new_v[k] for k in names])
```

```python
import functools
import math

import jax
import jax.numpy as jnp
from jax import lax
from jax.experimental import pallas as pl
from jax.experimental.pallas import tpu as pltpu

F32 = jnp.float32
BF16 = jnp.bfloat16

D_MODEL = 1024
SEQ = 2048
DEPTH = 2
HEAD_DIM = 64
N_HEADS_A, N_HEADS_B, N_HEADS_C = 6, 4, 6
A_WIDTH, B_WIDTH, C_WIDTH = N_HEADS_A * HEAD_DIM, N_HEADS_B * HEAD_DIM, N_HEADS_C * HEAD_DIM
CONV_WIDTH = 4
GDN_CHUNK = 64
BLOCK = 128
ROPE_DIM = 16
ROPE_THETA = 500000.0
DILATED_PAIRS = ((128, 1), (512, 4), (2048, 16))
RMS_EPS = 1e-6
NEG = -1e30

NT = (((1,), (1,)), ((), ()))
NN = (((1,), (0,)), ((), ()))
TN = (((0,), (0,)), ((), ()))

VMEM_LIMIT = 48 * 1024 * 1024

ORIG_A = 4 * A_WIDTH
ORIG_BA = 2 * N_HEADS_A
ORIG_B = 4 * B_WIDTH
COL_AZ = 3 * A_WIDTH
COL_C = 4 * A_WIDTH
COL_B = COL_C + 4 * C_WIDTH
COL_BA = COL_B + 4 * B_WIDTH
P_COLS = COL_BA + BLOCK
TN_COLS = 384
INPROJ_COLS = P_COLS // 3
TM_ROWS = 512
ROWS = 256


def _mm(a, b, dims=NN):
    return lax.dot_general(a.astype(BF16), b.astype(BF16), dims, preferred_element_type=F32)


def _mm32(a, b, dims=NN):
    return lax.dot_general(a, b, dims, precision=lax.Precision.HIGH, preferred_element_type=F32)


def _cparams(*sem):
    return pltpu.CompilerParams(dimension_semantics=sem, vmem_limit_bytes=VMEM_LIMIT)


def _sigmoid(x):
    return 0.5 * (jnp.tanh(0.5 * x) + 1.0)


def _softplus(x):
    return jnp.maximum(x, 0.0) + jnp.log(1.0 + jnp.exp(-jnp.abs(x)))


def _rms(x, w):
    return x * lax.rsqrt(jnp.mean(x * x, axis=-1, keepdims=True) + RMS_EPS) * w


def _heads(a, n):
    return jnp.stack([a[:, h * HEAD_DIM:(h + 1) * HEAD_DIM] for h in range(n)])


def _unheads(a):
    return jnp.concatenate([a[h] for h in range(a.shape[0])], axis=1)


def _row_chunks(t):
    return [pl.ds(c * ROWS, ROWS) for c in range(t // ROWS)]


def inproj_fwd(x, nw, w):
    n, d = x.shape
    p = w.shape[1]

    def body(x_ref, nw_ref, w_ref, proj_ref, hdn_ref):
        @pl.when(pl.program_id(1) == 0)
        def _():
            hdn_ref[...] = _rms(x_ref[...], nw_ref[...]).astype(BF16)

        proj_ref[...] = jnp.dot(hdn_ref[...], w_ref[...], preferred_element_type=F32)

    return pl.pallas_call(
        body, name="inproj_fwd", grid=(n // TM_ROWS, p // INPROJ_COLS),
        in_specs=[pl.BlockSpec((TM_ROWS, d), lambda i, j: (i, 0)), pl.BlockSpec((1, d), lambda i, j: (0, 0)),
                  pl.BlockSpec((d, INPROJ_COLS), lambda i, j: (0, j))],
        out_specs=[pl.BlockSpec((TM_ROWS, INPROJ_COLS), lambda i, j: (i, j)), pl.BlockSpec((TM_ROWS, d), lambda i, j: (i, 0))],
        out_shape=[jax.ShapeDtypeStruct((n, p), F32), jax.ShapeDtypeStruct((n, d), BF16)],
        compiler_params=_cparams("parallel", "arbitrary"),
    )(x, nw, w)


def mat_tn(a, slabs):
    n, ka = a.shape
    ns = len(slabs)

    def body(*refs):
        a_ref, s_refs, o_refs = refs[0], refs[1:1 + ns], refs[1 + ns:]

        @pl.when(pl.program_id(0) == 0)
        def _():
            for o_ref in o_refs:
                o_ref[...] = jnp.zeros_like(o_ref)

        av = a_ref[...]
        for s_ref, o_ref in zip(s_refs, o_refs):
            o_ref[...] += lax.dot_general(av, s_ref[...].astype(BF16), TN, preferred_element_type=F32)

    return pl.pallas_call(
        body, name="mat_tn", grid=(n // TM_ROWS,),
        in_specs=[pl.BlockSpec((TM_ROWS, ka), lambda k: (k, 0))]
                 + [pl.BlockSpec((TM_ROWS, s.shape[1]), lambda k: (k, 0)) for s in slabs],
        out_specs=[pl.BlockSpec((ka, s.shape[1]), lambda k: (0, 0)) for s in slabs],
        out_shape=[jax.ShapeDtypeStruct((ka, s.shape[1]), F32) for s in slabs],
        compiler_params=_cparams("arbitrary"),
    )(a, *slabs)


def inproj_bwd(slabs, w, x, nw, dy):
    n, d = x.shape
    p = w.shape[1]
    tm = 256
    ns = len(slabs)

    def body(*refs):
        s_refs = refs[:ns]
        w_ref, x_ref, nw_ref, dy_ref, dx_ref, dnw_ref = refs[ns:]
        dh = jnp.zeros((tm, d), F32)
        at = 0
        for s_ref in s_refs:
            wd = s_ref.shape[1]
            dh = dh + lax.dot_general(s_ref[...].astype(BF16), w_ref[:, at:at + wd], NT, preferred_element_type=F32)
            at += wd
        _, vjp = jax.vjp(_rms, x_ref[...], nw_ref[...])
        dx, dnw = vjp(dh)
        dx_ref[...] = dx + dy_ref[...]
        dnw_ref[0] = dnw

    return pl.pallas_call(
        body, name="inproj_bwd", grid=(n // tm,),
        in_specs=[pl.BlockSpec((tm, s.shape[1]), lambda i: (i, 0)) for s in slabs]
                 + [pl.BlockSpec((d, p), lambda i: (0, 0)), pl.BlockSpec((tm, d), lambda i: (i, 0)),
                    pl.BlockSpec((1, d), lambda i: (0, 0)), pl.BlockSpec((tm, d), lambda i: (i, 0))],
        out_specs=[pl.BlockSpec((tm, d), lambda i: (i, 0)), pl.BlockSpec((1, 1, d), lambda i: (i, 0, 0))],
        out_shape=[jax.ShapeDtypeStruct((n, d), F32), jax.ShapeDtypeStruct((n // tm, 1, d), F32)],
        compiler_params=_cparams("parallel"),
    )(*slabs, w, x, nw, dy)


CONV_PAD = 8
CONV_ROWS = 256


def _conv_pre(pad_s, cw, c):
    xs = [pad_s[pl.ds(c * CONV_ROWS + CONV_PAD - (CONV_WIDTH - 1) + k, CONV_ROWS), :] for k in range(CONV_WIDTH)]
    pre = xs[0] * cw[0:1, :]
    for k in range(1, CONV_WIDTH):
        pre = pre + xs[k] * cw[k:k + 1, :]
    return pre, xs


def conv_fwd(proj, cw, nseq):
    n = proj.shape[0]
    t = n // nseq
    ch = cw.shape[1]

    def body(x_ref, cw_ref, y_ref, pad_s):
        pad_s[pl.ds(0, CONV_PAD), :] = jnp.zeros((CONV_PAD, TN_COLS), F32)
        pad_s[pl.ds(CONV_PAD, t), :] = x_ref[...]
        cwv = cw_ref[...]
        for c in range(t // CONV_ROWS):
            pre, _ = _conv_pre(pad_s, cwv, c)
            y_ref[pl.ds(c * CONV_ROWS, CONV_ROWS), :] = pre * _sigmoid(pre)

    return pl.pallas_call(
        body, name="conv_fwd", grid=(nseq, ch // TN_COLS),
        in_specs=[pl.BlockSpec((t, TN_COLS), lambda b, j: (b, j)), pl.BlockSpec((CONV_WIDTH, TN_COLS), lambda b, j: (0, j))],
        out_specs=pl.BlockSpec((t, TN_COLS), lambda b, j: (b, j)),
        out_shape=jax.ShapeDtypeStruct((n, ch), F32),
        scratch_shapes=[pltpu.VMEM((t + CONV_PAD, TN_COLS), F32)],
        compiler_params=_cparams("parallel", "parallel"),
    )(proj, cw)


def conv_bwd(proj, cw, dy, nseq):
    n = proj.shape[0]
    t = n // nseq
    ch = cw.shape[1]

    def body(x_ref, cw_ref, dy_ref, dx_ref, dcw_ref, pad_s, dpad_s):
        pad_s[pl.ds(0, CONV_PAD), :] = jnp.zeros((CONV_PAD, TN_COLS), F32)
        pad_s[pl.ds(CONV_PAD, t), :] = x_ref[...]
        dpad_s[pl.ds(t, CONV_PAD), :] = jnp.zeros((CONV_PAD, TN_COLS), F32)
        cwv = cw_ref[...]
        acc = [jnp.zeros((1, TN_COLS), F32)] * CONV_WIDTH
        for c in range(t // CONV_ROWS):
            pre, xs = _conv_pre(pad_s, cwv, c)
            sg = _sigmoid(pre)
            dpre = dy_ref[pl.ds(c * CONV_ROWS, CONV_ROWS), :] * (sg * (1.0 + pre * (1.0 - sg)))
            dpad_s[pl.ds(c * CONV_ROWS, CONV_ROWS), :] = dpre
            acc = [acc[k] + jnp.sum(dpre * xs[k], axis=0, keepdims=True) for k in range(CONV_WIDTH)]
        for k in range(CONV_WIDTH):
            dcw_ref[0, pl.ds(k, 1), :] = acc[k]
        for c in range(t // CONV_ROWS):
            dx = dpad_s[pl.ds(c * CONV_ROWS + CONV_WIDTH - 1, CONV_ROWS), :] * cwv[0:1, :]
            for k in range(1, CONV_WIDTH):
                dx = dx + dpad_s[pl.ds(c * CONV_ROWS + CONV_WIDTH - 1 - k, CONV_ROWS), :] * cwv[k:k + 1, :]
            dx_ref[pl.ds(c * CONV_ROWS, CONV_ROWS), :] = dx

    blk = pl.BlockSpec((t, TN_COLS), lambda b, j: (b, j))
    return pl.pallas_call(
        body, name="conv_bwd", grid=(nseq, ch // TN_COLS),
        in_specs=[blk, pl.BlockSpec((CONV_WIDTH, TN_COLS), lambda b, j: (0, j)), blk],
        out_specs=[blk, pl.BlockSpec((1, CONV_WIDTH, TN_COLS), lambda b, j: (b, 0, j))],
        out_shape=[jax.ShapeDtypeStruct((n, ch), F32), jax.ShapeDtypeStruct((nseq, CONV_WIDTH, ch), F32)],
        scratch_shapes=[pltpu.VMEM((t + CONV_PAD, TN_COLS), F32)] * 2,
        compiler_params=_cparams("parallel", "parallel"),
    )(proj, cw, dy)


def _gate_specs(d):
    wide = pl.BlockSpec((TM_ROWS, d), lambda i: (i, 0))
    oa = pl.BlockSpec((TM_ROWS, A_WIDTH), lambda i: (i, 0))
    ob = pl.BlockSpec((TM_ROWS, B_WIDTH), lambda i: (i, 0))
    oc = pl.BlockSpec((TM_ROWS, C_WIDTH), lambda i: (i, 0))
    za = pl.BlockSpec((TM_ROWS, A_WIDTH), lambda i: (i, COL_AZ // A_WIDTH))
    zb = pl.BlockSpec((TM_ROWS, B_WIDTH), lambda i: (i, (COL_B + 3 * B_WIDTH) // B_WIDTH))
    zc = pl.BlockSpec((TM_ROWS, C_WIDTH), lambda i: (i, (COL_C + 3 * C_WIDTH) // C_WIDTH))
    return wide, oa, ob, oc, za, zb, zc


BRANCH_COLS = ((0, A_WIDTH), (A_WIDTH, A_WIDTH + B_WIDTH), (A_WIDTH + B_WIDTH, D_MODEL))


def outproj_fwd(x, oa, ob, oc, proj, w):
    n, d = x.shape

    def body(x_ref, oa_ref, ob_ref, oc_ref, za_ref, zb_ref, zc_ref, w_ref, y_ref, m_ref):
        for (lo, hi), o_ref, z_ref in zip(BRANCH_COLS, (oa_ref, ob_ref, oc_ref), (za_ref, zb_ref, zc_ref)):
            zv = z_ref[...]
            m_ref[:, lo:hi] = (o_ref[...] * (zv * _sigmoid(zv))).astype(BF16)
        y_ref[...] = x_ref[...] + jnp.dot(m_ref[...], w_ref[...], preferred_element_type=F32)

    wide, sa, sb, sc, za, zb, zc = _gate_specs(d)
    return pl.pallas_call(
        body, name="outproj_fwd", grid=(n // TM_ROWS,),
        in_specs=[wide, sa, sb, sc, za, zb, zc, pl.BlockSpec((d, d), lambda i: (0, 0))],
        out_specs=[wide, wide],
        out_shape=[jax.ShapeDtypeStruct((n, d), F32), jax.ShapeDtypeStruct((n, d), BF16)],
        compiler_params=_cparams("parallel"),
    )(x, oa, ob, oc, proj, proj, proj, w)


def outproj_bwd(dy, oa, ob, oc, proj, w):
    n, d = dy.shape

    def body(dy_ref, oa_ref, ob_ref, oc_ref, za_ref, zb_ref, zc_ref, w_ref, doa_ref, dob_ref, doc_ref, dza_ref, dzb_ref, dzc_ref):
        dm = lax.dot_general(dy_ref[...].astype(BF16), w_ref[...], NT, preferred_element_type=F32)
        for (lo, hi), o_ref, z_ref, do_ref, dz_ref in zip(BRANCH_COLS, (oa_ref, ob_ref, oc_ref), (za_ref, zb_ref, zc_ref),
                                                          (doa_ref, dob_ref, doc_ref), (dza_ref, dzb_ref, dzc_ref)):
            zv = z_ref[...]
            sg = _sigmoid(zv)
            dmv = dm[:, lo:hi]
            do_ref[...] = dmv * (zv * sg)
            dz_ref[...] = dmv * o_ref[...] * (sg * (1.0 + zv * (1.0 - sg)))

    wide, sa, sb, sc, za, zb, zc = _gate_specs(d)
    sd = jax.ShapeDtypeStruct
    outs = [sd((n, A_WIDTH), F32), sd((n, B_WIDTH), F32), sd((n, C_WIDTH), F32)]
    return pl.pallas_call(
        body, name="outproj_bwd", grid=(n // TM_ROWS,),
        in_specs=[wide, sa, sb, sc, za, zb, zc, pl.BlockSpec((d, d), lambda i: (0, 0))],
        out_specs=[sa, sb, sc, sa, sb, sc],
        out_shape=outs + outs,
        compiler_params=_cparams("parallel"),
    )(dy, oa, ob, oc, proj, proj, proj, w)


def loss_fwd_bwd(y, target):
    n, d = y.shape

    def body(y_ref, t_ref, dy_ref, part_ref):
        e = y_ref[...] - t_ref[...]
        dy_ref[...] = e * (1.0 / d)
        part_ref[...] = jnp.zeros_like(part_ref) + 0.5 * jnp.sum(e * e) * (1.0 / d)

    blk = pl.BlockSpec((TM_ROWS, d), lambda i: (i, 0))
    return pl.pallas_call(
        body, name="loss", grid=(n // TM_ROWS,),
        in_specs=[blk, blk],
        out_specs=[blk, pl.BlockSpec((1, 8, BLOCK), lambda i: (i, 0, 0))],
        out_shape=[jax.ShapeDtypeStruct((n, d), F32), jax.ShapeDtypeStruct((n // TM_ROWS, 8, BLOCK), F32)],
        compiler_params=_cparams("parallel"),
    )(y, target)


ADAM_LR, ADAM_B1, ADAM_B2, ADAM_EPS, ADAM_WD, ADAM_STEP = 0.001, 0.9, 0.999, 1e-08, 0.01, 10


def adamw(w, g, m, v):
    r, c = w.shape
    tr = r if r <= 256 else 256

    def body(w_ref, g_ref, m_ref, v_ref, d_ref, nm_ref, nv_ref):
        gv = g_ref[...]
        nm = ADAM_B1 * m_ref[...] + (1.0 - ADAM_B1) * gv
        nv = ADAM_B2 * v_ref[...] + (1.0 - ADAM_B2) * (gv * gv)
        m_hat = nm / (1.0 - ADAM_B1 ** ADAM_STEP)
        v_hat = nv / (1.0 - ADAM_B2 ** ADAM_STEP)
        d_ref[...] = -ADAM_LR * (m_hat / (jnp.sqrt(v_hat) + ADAM_EPS) + ADAM_WD * w_ref[...])
        nm_ref[...] = nm
        nv_ref[...] = nv

    blk = pl.BlockSpec((tr, c), lambda i: (i, 0))
    return pl.pallas_call(
        body, name="adamw", grid=(r // tr,),
        in_specs=[blk] * 4, out_specs=[blk] * 3,
        out_shape=[jax.ShapeDtypeStruct((r, c), F32)] * 3,
        compiler_params=_cparams("parallel"),
    )(w, g, m, v)


SB_G = N_HEADS_B


def _sb_weights(qs, k, carry, tri, diag):
    z = _mm(qs, k, NT)
    sp = jnp.log(1.0 + jnp.exp(-jnp.abs(z)))
    ls_pos = jnp.minimum(z, 0.0) - sp
    ls_neg = jnp.minimum(-z, 0.0) - sp
    earlier = (lax.broadcasted_iota(jnp.int32, z.shape, 1) < lax.broadcasted_iota(jnp.int32, z.shape, 0)) if diag else None
    log_keep = jnp.where(earlier, ls_neg, 0.0) if diag else ls_neg
    hi = log_keep.astype(BF16)
    lo = (log_keep - hi.astype(F32)).astype(BF16)
    within = lax.dot_general(jnp.concatenate([hi, lo], axis=1), tri, NN, preferred_element_type=F32)
    arg = ls_pos + within + carry
    wts = jnp.where(earlier, jnp.exp(jnp.where(earlier, arg, 0.0)), 0.0) if diag else jnp.exp(arg)
    return ls_pos, ls_neg, log_keep, wts, earlier


def _sb_tile(q, k, v, carry, tri, diag):
    _, _, log_keep, wts, _ = _sb_weights(q * (HEAD_DIM ** -0.5), k, carry, tri, diag)
    return _mm(wts, v), jnp.sum(log_keep, axis=1, keepdims=True)


def _sb_tile_grads(q, k, v, carry, do, dtot, tri, diag):
    qs = q * (HEAD_DIM ** -0.5)
    ls_pos, ls_neg, _, wts, earlier = _sb_weights(qs, k, carry, tri, diag)
    dv = _mm(wts, do, TN)
    darg = _mm(do, v, NT) * wts
    dkeep = _mm(darg, tri[:BLOCK], NT) + dtot
    if diag:
        dkeep = jnp.where(earlier, dkeep, 0.0)
    dz = darg * jnp.exp(ls_neg) - dkeep * jnp.exp(ls_pos)
    return _mm(dz, k) * (HEAD_DIM ** -0.5), _mm(dz, qs, TN), dv, jnp.sum(darg, axis=1, keepdims=True)


_sb_tiles_diag = jax.vmap(functools.partial(_sb_tile, diag=True), in_axes=(0, 0, 0, 0, None))
_sb_tiles_off = jax.vmap(functools.partial(_sb_tile, diag=False), in_axes=(0, 0, 0, 0, None))
_sb_grads_diag = jax.vmap(functools.partial(_sb_tile_grads, diag=True), in_axes=(0, 0, 0, 0, 0, 0, None))
_sb_grads_off = jax.vmap(functools.partial(_sb_tile_grads, diag=False), in_axes=(0, 0, 0, 0, 0, 0, None))


def _sb_tri():
    r = lax.broadcasted_iota(jnp.int32, (2 * BLOCK, BLOCK), 0) % BLOCK
    c = lax.broadcasted_iota(jnp.int32, (2 * BLOCK, BLOCK), 1)
    return jnp.where(r > c, 1.0, 0.0).astype(BF16)


SB_SEQ = 2


def _sb_specs(t, nq):
    cb = COL_B // B_WIDTH
    sq = SB_SEQ
    q = pl.BlockSpec((sq, BLOCK, B_WIDTH), lambda b, i: (b, i, cb))
    k = pl.BlockSpec((sq, t, B_WIDTH), lambda b, i: (b, 0, cb + 1))
    v = pl.BlockSpec((sq, t, B_WIDTH), lambda b, i: (b, 0, cb + 2))
    blk = pl.BlockSpec((sq, BLOCK, B_WIDTH), lambda b, i: (b, i, 0))
    full = pl.BlockSpec((sq, t, B_WIDTH), lambda b, i: (b, 0, 0))
    carry = pl.BlockSpec((sq, 1, nq, BLOCK, SB_G), lambda b, i: (b, i, 0, 0, 0))
    return q, k, v, blk, full, carry


def _sb_heads(ref, rows):
    return jnp.concatenate([_heads(ref[b, rows, :], SB_G) for b in range(SB_SEQ)])


def _sb_unheads(a):
    return [_unheads(a[b * SB_G:(b + 1) * SB_G]) for b in range(SB_SEQ)]


def sb_fwd(proj, nseq):
    n = proj.shape[0]
    t = n // nseq
    nq = t // BLOCK
    g, sq = SB_G, SB_SEQ
    everything = pl.ds(0, BLOCK)

    def body(q_ref, k_ref, v_ref, o_ref, carry_ref):
        i = pl.program_id(1)
        tri = _sb_tri()
        qv = _sb_heads(q_ref, everything)

        def tile(j, c, fn):
            rows = pl.ds(pl.multiple_of(j * BLOCK, BLOCK), BLOCK)
            for b in range(sq):
                carry_ref[b, 0, j] = jnp.concatenate([c[b * g + h] for h in range(g)], axis=1)
            return fn(qv, _sb_heads(k_ref, rows), _sb_heads(v_ref, rows), c, tri)

        def step(it, st):
            o_acc, c = st
            o, tot = tile(i - 1 - it, c, _sb_tiles_off)
            return o_acc + o, c + tot

        o_acc, _ = lax.fori_loop(0, i, step, tile(i, jnp.zeros((sq * g, BLOCK, 1), F32), _sb_tiles_diag))
        for b, o in enumerate(_sb_unheads(o_acc)):
            o_ref[b] = o

    q, k, v, blk, _, carry = _sb_specs(t, nq)
    proj3 = proj.reshape(nseq, t, -1)
    o, carries = pl.pallas_call(
        body, name="sb_fwd", grid=(nseq // sq, nq),
        in_specs=[q, k, v],
        out_specs=[blk, carry],
        out_shape=[jax.ShapeDtypeStruct((nseq, t, B_WIDTH), F32),
                   jax.ShapeDtypeStruct((nseq, nq, nq, BLOCK, g), F32)],
        compiler_params=_cparams("parallel", "arbitrary"),
    )(proj3, proj3, proj3)
    return o.reshape(n, B_WIDTH), carries


def sb_bwd(proj, carries, do, nseq):
    n = proj.shape[0]
    t = n // nseq
    nq = t // BLOCK
    g, sq = SB_G, SB_SEQ
    everything = pl.ds(0, BLOCK)

    def body(q_ref, k_ref, v_ref, carry_ref, do_ref, dq_ref, dk_ref, dv_ref):
        i = pl.program_id(1)

        @pl.when(i == 0)
        def _():
            dk_ref[...] = jnp.zeros_like(dk_ref)
            dv_ref[...] = jnp.zeros_like(dv_ref)

        tri = _sb_tri()
        qv = _sb_heads(q_ref, everything)
        dov = _sb_heads(do_ref, everything)

        def tile(j, st, fn):
            dq_acc, dc = st
            rows = pl.ds(pl.multiple_of(j * BLOCK, BLOCK), BLOCK)
            cj = [carry_ref[b, 0, j] for b in range(sq)]
            dq, dk, dv, dcj = fn(qv, _sb_heads(k_ref, rows), _sb_heads(v_ref, rows),
                                 jnp.stack([cj[b][:, h:h + 1] for b in range(sq) for h in range(g)]), dov, dc, tri)
            for b, (dkb, dvb) in enumerate(zip(_sb_unheads(dk), _sb_unheads(dv))):
                dk_ref[b, rows, :] += dkb
                dv_ref[b, rows, :] += dvb
            return dq_acc + dq, dc + dcj

        st = lax.fori_loop(0, i, lambda j, st: tile(j, st, _sb_grads_off),
                           (jnp.zeros((sq * g, BLOCK, HEAD_DIM), F32), jnp.zeros((sq * g, BLOCK, 1), F32)))
        dq_acc, _ = tile(i, st, _sb_grads_diag)
        for b, dq in enumerate(_sb_unheads(dq_acc)):
            dq_ref[b] = dq

    q, k, v, blk, full, carry = _sb_specs(t, nq)
    proj3, do3 = proj.reshape(nseq, t, -1), do.reshape(nseq, t, -1)
    grads = pl.pallas_call(
        body, name="sb_bwd", grid=(nseq // sq, nq),
        in_specs=[q, k, v, carry, blk],
        out_specs=[blk, full, full],
        out_shape=[jax.ShapeDtypeStruct((nseq, t, B_WIDTH), F32)] * 3,
        compiler_params=_cparams("parallel", "arbitrary"),
    )(proj3, proj3, proj3, carries, do3)
    return [a.reshape(n, B_WIDTH) for a in grads]


def _unit_lower_inverse(a):
    n = a.shape[0]
    eye = jnp.where(lax.broadcasted_iota(jnp.int32, (n, n), 0) == lax.broadcasted_iota(jnp.int32, (n, n), 1), 1.0, 0.0)
    tmat = eye.astype(F32) - a
    p = a
    for _ in range(5):
        p = _mm32(p, p)
        tmat = tmat + _mm32(tmat, p)
    return tmat


@jax.custom_vjp
def _known_inverse(a, tmat):
    return tmat


def _known_inverse_fwd(a, tmat):
    return tmat, tmat


def _known_inverse_bwd(tmat, g):
    return -_mm32(_mm32(tmat, g, TN), tmat, NT), jnp.zeros_like(tmat)


_known_inverse.defvjp(_known_inverse_fwd, _known_inverse_bwd)


def _gdn_chunk(q, k, v, al_c, al_r, br_c, alog, dtb, nw, s, tmat_in):
    c = GDN_CHUNK
    ri = lax.broadcasted_iota(jnp.int32, (c, c), 0)
    ci = lax.broadcasted_iota(jnp.int32, (c, c), 1)
    incl, strict = ri >= ci, ri > ci
    rate = -jnp.exp(alog)
    g_c = rate * _softplus(al_c + dtb)
    g_r = rate * _softplus(al_r + dtb)
    beta = _sigmoid(br_c)
    gc_c = jnp.sum(jnp.where(incl, g_r, 0.0), axis=1, keepdims=True)
    gc_r = jnp.sum(jnp.where(ri <= ci, g_c, 0.0), axis=0, keepdims=True)
    gl = jnp.sum(g_r, axis=1, keepdims=True)
    decay = jnp.where(incl, jnp.exp(jnp.where(incl, gc_c - gc_r, 0.0)), 0.0)
    qn = q * lax.rsqrt(jnp.sum(q * q, axis=-1, keepdims=True) + RMS_EPS) * (HEAD_DIM ** -0.5)
    kn = k * lax.rsqrt(jnp.sum(k * k, axis=-1, keepdims=True) + RMS_EPS)
    kb = kn * beta
    a = jnp.where(strict, _mm(kb, kn, NT) * decay, 0.0)
    tmat = _unit_lower_inverse(a) if tmat_in is None else _known_inverse(a, tmat_in)
    u = _mm(tmat, v * beta)
    w = _mm(tmat, kb * jnp.exp(gc_c))
    qk = _mm(qn, kn, NT) * decay
    v_new = u - _mm(w, s)
    o = _mm(qn * jnp.exp(gc_c), s) + _mm(qk, v_new)
    s_new = s * jnp.exp(gl) + _mm(kn * jnp.exp(gl - gc_c), v_new, TN)
    o = o * lax.rsqrt(jnp.mean(o * o, axis=-1, keepdims=True) + RMS_EPS) * nw
    return o, s_new, tmat


_gdn_chunks_fwd = jax.vmap(functools.partial(_gdn_chunk, tmat_in=None), in_axes=(0, 0, 0, 0, 0, 0, 0, 0, None, 0))
_gdn_chunks_bwd = jax.vmap(_gdn_chunk, in_axes=(0, 0, 0, 0, 0, 0, 0, 0, None, 0, 0))

GDN_TB = 256
GDN_SEQ_FWD = 2
GDN_SEQ_BWD = 2


def _gdn_block(q3, k3, v3, ba, alog, dtb, nw, s, tm=None):
    nh = N_HEADS_A
    ns = q3.shape[0]
    bat = [ba[b].T for b in range(ns)]
    br_c = jnp.stack([ba[b][:, h:h + 1] for b in range(ns) for h in range(nh)])
    al_c = jnp.stack([ba[b][:, nh + h:nh + h + 1] for b in range(ns) for h in range(nh)])
    al_r = jnp.stack([bat[b][nh + h:nh + h + 1, :] for b in range(ns) for h in range(nh)])
    heads = lambda a: jnp.concatenate([_heads(a[b], nh) for b in range(ns)])
    args = (heads(q3), heads(k3), heads(v3), al_c, al_r, br_c, jnp.concatenate([alog] * ns), jnp.concatenate([dtb] * ns), nw, s)
    o, s_new, tmat = _gdn_chunks_fwd(*args) if tm is None else _gdn_chunks_bwd(*args, tm)
    o3 = jnp.stack([_unheads(o[b * nh:(b + 1) * nh]) for b in range(ns)])
    return (o3, s_new, tmat) if tm is None else (o3, s_new)


def _gdn_specs(nt, sq, rev):
    tpos = (lambda i: nt - 1 - i) if rev else (lambda i: i)
    ncb = GDN_TB // GDN_CHUNK
    qkv = [pl.BlockSpec((sq, GDN_TB, A_WIDTH), lambda b, i, j=j: (b, tpos(i), j)) for j in range(3)]
    ba = pl.BlockSpec((sq, GDN_TB, BLOCK), lambda b, i: (b, tpos(i), COL_BA // BLOCK))
    one = pl.BlockSpec((N_HEADS_A, 1, 1), lambda b, i: (0, 0, 0))
    vec = pl.BlockSpec((1, HEAD_DIM), lambda b, i: (0, 0))
    st = pl.BlockSpec((sq, ncb, N_HEADS_A, HEAD_DIM, HEAD_DIM), lambda b, i: (b, tpos(i), 0, 0, 0))
    oa = pl.BlockSpec((sq, GDN_TB, A_WIDTH), lambda b, i: (b, tpos(i), 0))
    return qkv, ba, one, vec, st, oa, tpos


def gdn_fwd(ya, proj, alog, dtb, nw, nseq):
    n = ya.shape[0]
    t = n // nseq
    nc, nt, ncb = t // GDN_CHUNK, t // GDN_TB, GDN_TB // GDN_CHUNK
    sq = GDN_SEQ_FWD
    nh = N_HEADS_A

    def body(q_ref, k_ref, v_ref, ba_ref, alog_ref, dtb_ref, nw_ref, o_ref, st_ref, tm_ref, s_s):
        @pl.when(pl.program_id(1) == 0)
        def _():
            s_s[...] = jnp.zeros_like(s_s)

        def step(c, s):
            rows = pl.ds(pl.multiple_of(c * GDN_CHUNK, GDN_CHUNK), GDN_CHUNK)
            o, s_new, tmat = _gdn_block(q_ref[:, rows, :], k_ref[:, rows, :], v_ref[:, rows, :], ba_ref[:, rows, :],
                                        alog_ref[...], dtb_ref[...], nw_ref[...], s)
            for b in range(sq):
                st_ref[b, c] = s[b * nh:(b + 1) * nh]
                tm_ref[b, c] = tmat[b * nh:(b + 1) * nh]
            o_ref[:, rows, :] = o
            return s_new

        s_s[...] = lax.fori_loop(0, ncb, step, s_s[...])

    qkv, ba, one, vec, st, oa, _ = _gdn_specs(nt, sq, False)
    ya3, proj3 = ya.reshape(nseq, t, -1), proj.reshape(nseq, t, -1)
    per_chunk = jax.ShapeDtypeStruct((nseq, nc, nh, HEAD_DIM, HEAD_DIM), F32)
    o, states, inverses = pl.pallas_call(
        body, name="gdn_fwd", grid=(nseq // sq, nt),
        in_specs=qkv + [ba, one, one, vec],
        out_specs=[oa, st, st],
        out_shape=[jax.ShapeDtypeStruct((nseq, t, A_WIDTH), F32), per_chunk, per_chunk],
        scratch_shapes=[pltpu.VMEM((sq * nh, HEAD_DIM, HEAD_DIM), F32)],
        compiler_params=_cparams("parallel", "arbitrary"),
    )(ya3, ya3, ya3, proj3, alog, dtb, nw)
    return o.reshape(n, A_WIDTH), states, inverses


def gdn_bwd(ya, proj, alog, dtb, nw, states, inverses, do, nseq):
    n = ya.shape[0]
    t = n // nseq
    nt, ncb = t // GDN_TB, GDN_TB // GDN_CHUNK
    nh = N_HEADS_A
    sq = GDN_SEQ_BWD

    def body(q_ref, k_ref, v_ref, ba_ref, alog_ref, dtb_ref, nw_ref, st_ref, tm_ref, do_ref,
             dya_ref, dba_ref, dalog_ref, ddtb_ref, dnw_ref, ds_s):
        @pl.when(pl.program_id(1) == 0)
        def _():
            ds_s[...] = jnp.zeros_like(ds_s)
            dalog_ref[...] = jnp.zeros_like(dalog_ref)
            ddtb_ref[...] = jnp.zeros_like(ddtb_ref)
            dnw_ref[...] = jnp.zeros_like(dnw_ref)

        def step(it, carry):
            ds, dalog, ddtb, dnw = carry
            c = ncb - 1 - it
            rows = pl.ds(pl.multiple_of(c * GDN_CHUNK, GDN_CHUNK), GDN_CHUNK)
            s_in = jnp.concatenate([st_ref[b, c] for b in range(sq)])
            tm_in = jnp.concatenate([tm_ref[b, c] for b in range(sq)])
            _, vjp = jax.vjp(functools.partial(_gdn_block, tm=tm_in), q_ref[:, rows, :], k_ref[:, rows, :], v_ref[:, rows, :],
                             ba_ref[:, rows, :], alog_ref[...], dtb_ref[...], nw_ref[...], s_in)
            dq, dk, dv, dba, da, dd, dn, ds = vjp((do_ref[:, rows, :], ds))
            dya_ref[:, rows, 0:A_WIDTH] = dq
            dya_ref[:, rows, A_WIDTH:2 * A_WIDTH] = dk
            dya_ref[:, rows, 2 * A_WIDTH:3 * A_WIDTH] = dv
            dba_ref[:, rows, :] = dba
            return ds, dalog + da, ddtb + dd, dnw + dn

        z11 = jnp.zeros((nh, 1, 1), F32)
        ds, dalog, ddtb, dnw = lax.fori_loop(0, ncb, step, (ds_s[...], z11, z11, jnp.zeros((1, HEAD_DIM), F32)))
        ds_s[...] = ds
        dalog_ref[0] += dalog
        ddtb_ref[0] += ddtb
        dnw_ref[0] += dnw

    qkv, ba, one, vec, st, oa, tpos = _gdn_specs(nt, sq, True)
    per_grp = pl.BlockSpec((1, nh, 1, 1), lambda b, i: (b, 0, 0, 0))
    sd = jax.ShapeDtypeStruct
    ya3, proj3, do3 = ya.reshape(nseq, t, -1), proj.reshape(nseq, t, -1), do.reshape(nseq, t, -1)
    dya, dba, dalog, ddtb, dnw = pl.pallas_call(
        body, name="gdn_bwd", grid=(nseq // sq, nt),
        in_specs=qkv + [ba, one, one, vec, st, st, oa],
        out_specs=[pl.BlockSpec((sq, GDN_TB, 3 * A_WIDTH), lambda b, i: (b, tpos(i), 0)),
                   pl.BlockSpec((sq, GDN_TB, BLOCK), lambda b, i: (b, tpos(i), 0)),
                   per_grp, per_grp, pl.BlockSpec((1, 1, HEAD_DIM), lambda b, i: (b, 0, 0))],
        out_shape=[sd((nseq, t, 3 * A_WIDTH), F32), sd((nseq, t, BLOCK), F32), sd((nseq // sq, nh, 1, 1), F32),
                   sd((nseq // sq, nh, 1, 1), F32), sd((nseq // sq, 1, HEAD_DIM), F32)],
        scratch_shapes=[pltpu.VMEM((sq * nh, HEAD_DIM, HEAD_DIM), F32)],
        compiler_params=_cparams("parallel", "arbitrary"),
    )(ya3, ya3, ya3, proj3, alog, dtb, nw, states, inverses, do3)
    return dya.reshape(n, 3 * A_WIDTH), dba.reshape(n, BLOCK), dalog, ddtb, dnw


DIL_NB = tuple((SEQ // d) // BLOCK for _, d in DILATED_PAIRS)
DIL_D = tuple(d for _, d in DILATED_PAIRS)
DIL_STEPS = tuple(w // d for w, d in DILATED_PAIRS)
DIL_B = 4
PAIR = 2 * HEAD_DIM


def _rope_tables(t):
    half = ROPE_DIM // 2
    inv_freq = ROPE_THETA ** (-jnp.arange(half, dtype=F32) / half)
    ang = jnp.arange(t, dtype=F32)[:, None] * inv_freq[None, :]
    ones = jnp.ones((t, HEAD_DIM - ROPE_DIM), F32)
    cs = jnp.concatenate([jnp.cos(ang), jnp.cos(ang), ones], axis=1)
    sn = jnp.concatenate([jnp.sin(ang), jnp.sin(ang), 0.0 * ones], axis=1)
    i = jnp.arange(PAIR)[:, None]
    j = jnp.arange(PAIR)[None, :]
    same = (i // HEAD_DIM) == (j // HEAD_DIM)
    ih, jh = i % HEAD_DIM, j % HEAD_DIM
    pm = (jnp.where(same & (jh < half) & (ih == jh + half), -1.0, 0.0)
          + jnp.where(same & (jh >= half) & (jh < ROPE_DIM) & (ih == jh - half), 1.0, 0.0))
    mean = jnp.where(same, 1.0 / HEAD_DIM, 0.0)
    twice = lambda m: jnp.concatenate([m, m]).astype(BF16)
    return jnp.tile(cs, (1, 2)), jnp.tile(sn, (1, 2)), twice(mean), twice(pm)


def _split_dot(x, w2):
    hi = x.astype(BF16)
    lo = lax.stop_gradient(x - hi.astype(F32)).astype(BF16)
    return lax.dot_general(jnp.concatenate([hi, lo], axis=1), w2, NN, preferred_element_type=F32)


def _dil_prep(x, w, cs, sn, mean2, pm2):
    y = x * lax.rsqrt(_split_dot(x * x, mean2) + RMS_EPS) * w
    return y * cs + _split_dot(y, pm2) * sn


def _dil_tile(qn, kk, vv, bias):
    lane = lax.broadcasted_iota(jnp.int32, (1, PAIR), 1)
    outs, lses = [], []
    for h in range(2):
        s = _mm(jnp.where(lane // HEAD_DIM == h, qn, 0.0) * (HEAD_DIM ** -0.5), kk, NT) + bias
        m = lax.stop_gradient(jnp.max(s, axis=-1, keepdims=True))
        p = jnp.exp(s - m)
        denom = jnp.sum(p, axis=-1, keepdims=True)
        outs.append(_mm(p, vv) / denom)
        lses.append(m + jnp.log(denom))
    return jnp.where(lane < HEAD_DIM, outs[0], outs[1]), jnp.concatenate(lses, axis=1)


_dil_tiles = jax.vmap(_dil_tile)


def _spread(a):
    lane = lax.broadcasted_iota(jnp.int32, (a.shape[0], PAIR), 1)
    return jnp.where(lane < HEAD_DIM, a[:, 0:1], a[:, 1:2])


def _dil_mix(o1, o2, o3, l1, l2, l3):
    m = lax.stop_gradient(jnp.maximum(jnp.maximum(l1, l2), l3))
    e1, e2, e3 = jnp.exp(l1 - m), jnp.exp(l2 - m), jnp.exp(l3 - m)
    r = 1.0 / (e1 + e2 + e3)
    return _spread(e1 * r) * o1 + _spread(e2 * r) * o2 + _spread(e3 * r) * o3


def _dil_fill_biases(bias_s):
    steps, = set(DIL_STEPS)
    qi = lax.broadcasted_iota(jnp.int32, (BLOCK, 1), 0)
    kj = lax.broadcasted_iota(jnp.int32, (1, 2 * BLOCK), 1)
    rel = qi - kj + BLOCK
    inside = (rel >= 0) & (rel <= steps)
    bias_s[0] = jnp.where(inside, 0.0, NEG)
    bias_s[1] = jnp.where(inside & (kj >= BLOCK), 0.0, NEG)
    bias_s[2] = jnp.where((qi >= kj) & (qi - kj <= steps), 0.0, NEG)


def _dil_mask(it, g, bias_s):
    qrows = pl.ds(pl.multiple_of(it * BLOCK, BLOCK), BLOCK)
    if DIL_NB[g] == 1:
        return bias_s[2, :, 0:BLOCK], qrows, qrows
    which = jnp.where(it == 0, 2, jnp.where(it % DIL_NB[g] == 0, 1, 0))
    kstart = jnp.maximum(it - 1, 0) * BLOCK
    return bias_s[which], qrows, pl.ds(pl.multiple_of(kstart, BLOCK), 2 * BLOCK)


def _dil_gather(src, dst, d):
    t = src.shape[0]
    ln = t // d
    for r in range(d):
        dst[pl.ds(r * ln, ln), :] = src[pl.ds(r, ln, stride=d), :]


def _dil_scatter(src, dst, d):
    t = src.shape[0]
    ln = t // d
    for r in range(d):
        dst[pl.ds(r, ln, stride=d), :] = src[pl.ds(r * ln, ln), :]


def _dil_forward_parts(q_ref, k_ref, v_ref, qw, kw, cs_ref, sn_ref, mean2, pm2, qn_s, kn_s, dl_s, od_s, ld_s, on_s, ln_s, bias_s):
    t = qn_s.shape[0]
    _dil_fill_biases(bias_s)

    def prep(c, _):
        rows = pl.ds(pl.multiple_of(c * ROWS, ROWS), ROWS)
        qn_s[rows, :] = _dil_prep(q_ref[rows, :], qw, cs_ref[rows, :], sn_ref[rows, :], mean2, pm2)
        kn_s[rows, :] = _dil_prep(k_ref[rows, :], kw, cs_ref[rows, :], sn_ref[rows, :], mean2, pm2)
        return 0

    lax.fori_loop(0, t // ROWS, prep, 0)
    for g in (1, 2):
        _dil_gather(qn_s, dl_s.at[g - 1, 0], DIL_D[g])
        _dil_gather(kn_s, dl_s.at[g - 1, 1], DIL_D[g])
        _dil_gather(v_ref, dl_s.at[g - 1, 2], DIL_D[g])
    for g in range(3):
        qs = qn_s if g == 0 else dl_s.at[g - 1, 0]
        ks = kn_s if g == 0 else dl_s.at[g - 1, 1]
        vs = v_ref if g == 0 else dl_s.at[g - 1, 2]

        def tiles(i, _, g=g, qs=qs, ks=ks, vs=vs):
            where = [_dil_mask(i * DIL_B + b, g, bias_s) for b in range(DIL_B)]
            o, lse = _dil_tiles(jnp.stack([qs[qr, :] for _, qr, _ in where]), jnp.stack([ks[kr, :] for _, _, kr in where]),
                                jnp.stack([vs[kr, :] for _, _, kr in where]), jnp.stack([m for m, _, _ in where]))
            for b, (_, qr, _) in enumerate(where):
                od_s[g, qr, :] = o[b]
                ld_s[g, qr, :] = lse[b]
            return 0

        lax.fori_loop(0, t // BLOCK // DIL_B, tiles, 0)
    for g in (1, 2):
        _dil_scatter(od_s.at[g], on_s.at[g - 1], DIL_D[g])
        _dil_scatter(ld_s.at[g], ln_s.at[g - 1], DIL_D[g])


def _dil_scratch(t):
    return [pltpu.VMEM((t, PAIR), F32), pltpu.VMEM((t, PAIR), F32),
            pltpu.VMEM((2, 3, t, PAIR), F32),
            pltpu.VMEM((3, t, PAIR), F32), pltpu.VMEM((3, t, 2), F32),
            pltpu.VMEM((2, t, PAIR), F32), pltpu.VMEM((2, t, 2), F32),
            pltpu.VMEM((3, BLOCK, 2 * BLOCK), F32)]


def _dil_specs(t):
    cb = COL_C // BLOCK
    per = C_WIDTH // BLOCK
    qkv = [pl.BlockSpec((t, BLOCK), lambda b, p, j=j: (b, cb + j * per + p)) for j in range(3)]
    vec = pl.BlockSpec((1, PAIR), lambda b, p: (0, 0))
    tab = pl.BlockSpec((t, PAIR), lambda b, p: (0, 0))
    mat = pl.BlockSpec((2 * PAIR, PAIR), lambda b, p: (0, 0))
    pair = pl.BlockSpec((t, BLOCK), lambda b, p: (b, p))
    return qkv, vec, tab, mat, pair


def dil_fwd(proj, qw, kw, cs, sn, mean2, pm2, nseq):
    n = proj.shape[0]
    t = n // nseq

    def body(q_ref, k_ref, v_ref, qw_ref, kw_ref, cs_ref, sn_ref, mean_ref, pm_ref, o_ref,
             qn_s, kn_s, dl_s, od_s, ld_s, on_s, ln_s, bias_s):
        _dil_forward_parts(q_ref, k_ref, v_ref, qw_ref[...], kw_ref[...], cs_ref, sn_ref, mean_ref[...], pm_ref[...],
                           qn_s, kn_s, dl_s, od_s, ld_s, on_s, ln_s, bias_s)

        def mix(c, _):
            rows = pl.ds(pl.multiple_of(c * ROWS, ROWS), ROWS)
            o_ref[rows, :] = _dil_mix(od_s[0, rows, :], on_s[0, rows, :], on_s[1, rows, :],
                                      ld_s[0, rows, :], ln_s[0, rows, :], ln_s[1, rows, :])
            return 0

        lax.fori_loop(0, t // ROWS, mix, 0)

    qkv, vec, tab, mat, pair = _dil_specs(t)
    return pl.pallas_call(
        body, name="dil_fwd", grid=(nseq, C_WIDTH // BLOCK),
        in_specs=qkv + [vec, vec, tab, tab, mat, mat],
        out_specs=pair,
        out_shape=jax.ShapeDtypeStruct((n, C_WIDTH), F32),
        scratch_shapes=_dil_scratch(t),
        compiler_params=_cparams("parallel", "parallel"),
    )(proj, proj, proj, qw, kw, cs, sn, mean2, pm2)


def dil_bwd(proj, qw, kw, cs, sn, mean2, pm2, do, nseq):
    n = proj.shape[0]
    t = n // nseq

    def body(q_ref, k_ref, v_ref, qw_ref, kw_ref, cs_ref, sn_ref, mean_ref, pm_ref, do_ref,
             dq_ref, dk_ref, dv_ref, dqw_ref, dkw_ref,
             qn_s, kn_s, dl_s, od_s, ld_s, on_s, ln_s, bias_s, tq_s, tk_s, tv_s):
        qw, kw, mean2, pm2 = qw_ref[...], kw_ref[...], mean_ref[...], pm_ref[...]
        _dil_forward_parts(q_ref, k_ref, v_ref, qw, kw, cs_ref, sn_ref, mean2, pm2, qn_s, kn_s, dl_s, od_s, ld_s, on_s, ln_s, bias_s)

        def mix(c, _):
            rows = pl.ds(pl.multiple_of(c * ROWS, ROWS), ROWS)
            _, vjp = jax.vjp(_dil_mix, od_s[0, rows, :], on_s[0, rows, :], on_s[1, rows, :],
                             ld_s[0, rows, :], ln_s[0, rows, :], ln_s[1, rows, :])
            d1, d2, d3, e1, e2, e3 = vjp(do_ref[rows, :])
            od_s[0, rows, :] = d1
            on_s[0, rows, :] = d2
            on_s[1, rows, :] = d3
            ld_s[0, rows, :] = e1
            ln_s[0, rows, :] = e2
            ln_s[1, rows, :] = e3
            return 0

        lax.fori_loop(0, t // ROWS, mix, 0)
        for g in (1, 2):
            _dil_gather(on_s.at[g - 1], od_s.at[g], DIL_D[g])
            _dil_gather(ln_s.at[g - 1], ld_s.at[g], DIL_D[g])
        on_s[...] = jnp.zeros_like(on_s)
        dv_ref[...] = jnp.zeros_like(dv_ref)
        for g in range(3):
            qs = qn_s if g == 0 else dl_s.at[g - 1, 0]
            ks = kn_s if g == 0 else dl_s.at[g - 1, 1]
            vs = v_ref if g == 0 else dl_s.at[g - 1, 2]
            gq = on_s.at[0] if g == 0 else tq_s
            gk = on_s.at[1] if g == 0 else tk_s
            gv = dv_ref if g == 0 else tv_s
            if g > 0:
                tk_s[...] = jnp.zeros_like(tk_s)
                tv_s[...] = jnp.zeros_like(tv_s)

            def tiles(i, _, g=g, qs=qs, ks=ks, vs=vs, gq=gq, gk=gk, gv=gv):
                where = [_dil_mask(i * DIL_B + b, g, bias_s) for b in range(DIL_B)]
                biases = jnp.stack([m for m, _, _ in where])
                _, vjp = jax.vjp(lambda q_, k_, v_: _dil_tiles(q_, k_, v_, biases),
                                 jnp.stack([qs[qr, :] for _, qr, _ in where]), jnp.stack([ks[kr, :] for _, _, kr in where]),
                                 jnp.stack([vs[kr, :] for _, _, kr in where]))
                dq, dkk, dvv = vjp((jnp.stack([od_s[g, qr, :] for _, qr, _ in where]),
                                    jnp.stack([ld_s[g, qr, :] for _, qr, _ in where])))
                for b, (_, qr, kr) in enumerate(where):
                    gq[qr, :] = dq[b]
                    gk[kr, :] += dkk[b]
                    gv[kr, :] += dvv[b]
                return 0

            lax.fori_loop(0, t // BLOCK // DIL_B, tiles, 0)
            if g > 0:
                d = DIL_D[g]
                ln = t // d
                for r in range(d):
                    nat, dil = pl.ds(r, ln, stride=d), pl.ds(r * ln, ln)
                    on_s[0, nat, :] += tq_s[dil, :]
                    on_s[1, nat, :] += tk_s[dil, :]
                    dv_ref[nat, :] += tv_s[dil, :]

        def prep(c, acc):
            rows = pl.ds(pl.multiple_of(c * ROWS, ROWS), ROWS)
            f = lambda x, w: _dil_prep(x, w, cs_ref[rows, :], sn_ref[rows, :], mean2, pm2)
            _, vq = jax.vjp(f, q_ref[rows, :], qw)
            _, vk = jax.vjp(f, k_ref[rows, :], kw)
            dq, dqw = vq(on_s[0, rows, :])
            dk, dkw = vk(on_s[1, rows, :])
            dq_ref[rows, :] = dq
            dk_ref[rows, :] = dk
            return acc[0] + dqw, acc[1] + dkw

        dqw, dkw = lax.fori_loop(0, t // ROWS, prep, (jnp.zeros((1, PAIR), F32), jnp.zeros((1, PAIR), F32)))
        dqw_ref[0] = dqw
        dkw_ref[0] = dkw

    qkv, vec, tab, mat, pair = _dil_specs(t)
    per = C_WIDTH // BLOCK
    wout = pl.BlockSpec((1, 1, PAIR), lambda b, p: (b * per + p, 0, 0))
    return pl.pallas_call(
        body, name="dil_bwd", grid=(nseq, per),
        in_specs=qkv + [vec, vec, tab, tab, mat, mat, pair],
        out_specs=[pair, pair, pair, wout, wout],
        out_shape=[jax.ShapeDtypeStruct((n, C_WIDTH), F32)] * 3 + [jax.ShapeDtypeStruct((nseq * per, 1, PAIR), F32)] * 2,
        scratch_shapes=_dil_scratch(t) + [pltpu.VMEM((t, PAIR), F32)] * 3,
        compiler_params=_cparams("parallel", "parallel"),
    )(proj, proj, proj, qw, kw, cs, sn, mean2, pm2, do)


N_CHIPS = 4
SUM_ROWS = 432
MESH_IDS = pl.DeviceIdType.MESH
ANY = pl.BlockSpec(memory_space=pl.ANY)


def plane_exchange(src, all_to_all):
    blk_shape = src.shape[1:] if all_to_all else src.shape

    def body(src_ref, out_ref, send_sems, recv_sems, local_sem):
        x, y, c = lax.axis_index("x"), lax.axis_index("y"), lax.axis_index("c")
        me = 2 * x + y
        mine = pltpu.make_async_copy(src_ref.at[me] if all_to_all else src_ref, out_ref.at[me], local_sem)
        mine.start()
        sends = []
        for k in (1, 2, 3):
            px = 1 - x if k & 2 else x
            py = 1 - y if k & 1 else y
            peer = 2 * px + py
            cp = pltpu.make_async_remote_copy(
                src_ref=src_ref.at[peer] if all_to_all else src_ref, dst_ref=out_ref.at[me],
                send_sem=send_sems.at[k - 1], recv_sem=recv_sems.at[k - 1],
                device_id=(px, py, c), device_id_type=MESH_IDS)
            cp.start()
            sends.append((cp, peer, (px, py, c)))
        for k, (cp, peer, dev) in enumerate(sends):
            pltpu.make_async_remote_copy(
                src_ref=out_ref.at[me], dst_ref=out_ref.at[peer],
                send_sem=send_sems.at[k], recv_sem=recv_sems.at[k],
                device_id=dev, device_id_type=MESH_IDS).wait_recv()
        for cp, _, _ in sends:
            cp.wait_send()
        mine.wait()

    return pl.pallas_call(
        body, name="plane_all_to_all" if all_to_all else "plane_all_gather",
        in_specs=[ANY], out_specs=ANY,
        out_shape=jax.ShapeDtypeStruct((N_CHIPS,) + blk_shape, src.dtype),
        scratch_shapes=[pltpu.SemaphoreType.DMA((3,)), pltpu.SemaphoreType.DMA((3,)), pltpu.SemaphoreType.DMA],
    )(src)


def sibling_swap(src, other_half=False):
    shape = (src.shape[0], src.shape[1] // 2) + src.shape[2:] if other_half else src.shape

    def body(src_ref, out_ref, send_sem, recv_sem):
        x, y, c = lax.axis_index("x"), lax.axis_index("y"), lax.axis_index("c")
        part = src_ref.at[:, pl.ds((1 - c) * shape[1], shape[1])] if other_half else src_ref
        cp = pltpu.make_async_remote_copy(src_ref=part, dst_ref=out_ref, send_sem=send_sem, recv_sem=recv_sem,
                                          device_id=(x, y, 1 - c), device_id_type=MESH_IDS)
        cp.start()
        cp.wait()

    return pl.pallas_call(
        body, name="sibling_swap", in_specs=[ANY], out_specs=ANY,
        out_shape=jax.ShapeDtypeStruct(shape, src.dtype),
        scratch_shapes=[pltpu.SemaphoreType.DMA, pltpu.SemaphoreType.DMA],
    )(src)


def sum4(a):
    _, r, c = a.shape
    tr = SUM_ROWS

    def body(a_ref, o_ref):
        p = [a_ref[i].astype(F32) for i in range(N_CHIPS)]
        o_ref[...] = (p[0] + p[1]) + (p[2] + p[3])

    return pl.pallas_call(
        body, name="sum4", grid=(r // tr,),
        in_specs=[pl.BlockSpec((N_CHIPS, tr, c), lambda i: (0, i, 0))],
        out_specs=pl.BlockSpec((tr, c), lambda i: (i, 0)),
        out_shape=jax.ShapeDtypeStruct((r, c), F32),
        compiler_params=_cparams("parallel"),
    )(a)


def add_my_half(mine, got, c):
    nchip, r2, cols = mine.shape
    nt = r2 // 2 // SUM_ROWS

    def body(c_ref, a_ref, b_ref, o_ref):
        o_ref[...] = (a_ref[...] + b_ref[...]).astype(BF16)

    blk = pl.BlockSpec((1, SUM_ROWS, cols), lambda j, i, c_ref: (j, i, 0))
    return pl.pallas_call(
        body, name="add_my_half",
        grid_spec=pltpu.PrefetchScalarGridSpec(
            num_scalar_prefetch=1, grid=(nchip, nt),
            in_specs=[pl.BlockSpec((1, SUM_ROWS, cols), lambda j, i, c_ref: (j, c_ref[0] * nt + i, 0)), blk],
            out_specs=blk),
        out_shape=jax.ShapeDtypeStruct((nchip, r2 // 2, cols), BF16),
        compiler_params=_cparams("parallel", "parallel"),
    )(jnp.reshape(c, (1,)).astype(jnp.int32), mine, got)


PACK_COLS = 1152
PACK_ROWS = 2592
ROW_TILE = 16


def _pack(parts):
    blocks = []
    for p in parts:
        p2 = p.reshape(-1, p.shape[-1])
        blocks.append(jnp.pad(p2, ((0, -p2.shape[0] % ROW_TILE), (0, PACK_COLS - p2.shape[1]))))
    rows = sum(b.shape[0] for b in blocks)
    blocks.append(jnp.zeros((PACK_ROWS - rows, PACK_COLS), blocks[0].dtype))
    return jnp.concatenate(blocks)


def _unpack(buf, shapes):
    out, at = [], 0
    for s in shapes:
        rows = math.prod(s[:-1])
        out.append(buf[at:at + rows, :s[-1]].reshape(s))
        at += rows + (-rows % ROW_TILE)
    return out


def _pack_small(g):
    blk = jnp.zeros((ROW_TILE, PACK_COLS), F32)
    for i, k in enumerate(SMALL):
        blk = blk.at[2 * i:2 * i + 2, :g[k].shape[1]].set(g[k])
    return blk


def _unpack_small(blk, shapes):
    return [blk[2 * i:2 * i + 2, :s[1]] for i, s in enumerate(shapes)]


def _layer_fwd(x, p, nseq, tabs):
    proj, hdn = inproj_fwd(x, p["norm_w"][None], p["w_in"])
    ya = conv_fwd(proj, p["conv_w"], nseq)
    oa, states, inverses = gdn_fwd(ya, proj, p["a_log"].reshape(N_HEADS_A, 1, 1), p["dt_bias"].reshape(N_HEADS_A, 1, 1),
                         p["gdn_norm_w"][None], nseq)
    ob, carries = sb_fwd(proj, nseq)
    oc = dil_fwd(proj, jnp.tile(p["q_norm_w"], 2)[None], jnp.tile(p["k_norm_w"], 2)[None], *tabs, nseq)
    y, mixed = outproj_fwd(x, oa, ob, oc, proj, p["w_out"])
    return y, dict(x=x, hdn=hdn, proj=proj, ya=ya, states=states, inverses=inverses, carries=carries, oa=oa, ob=ob, oc=oc, mixed=mixed)


def _layer_bwd(dy, p, res, nseq, tabs):
    proj = res["proj"]
    g = {}
    g["w_out"] = mat_tn(res["mixed"], [dy])[0]
    doa, dob, doc, dza, dzb, dzc = outproj_bwd(dy, res["oa"], res["ob"], res["oc"], proj, p["w_out"])
    dqc, dkc, dvc, dqw, dkw = dil_bwd(proj, jnp.tile(p["q_norm_w"], 2)[None], jnp.tile(p["k_norm_w"], 2)[None], *tabs, doc,
                                      nseq)
    g["q_norm_w"], g["k_norm_w"] = dqw.reshape(-1, HEAD_DIM).sum(0), dkw.reshape(-1, HEAD_DIM).sum(0)
    dqb, dkb, dvb = sb_bwd(proj, res["carries"], dob, nseq)
    dya, dba, dalog, ddtb, dnw = gdn_bwd(res["ya"], proj, p["a_log"].reshape(N_HEADS_A, 1, 1),
                                         p["dt_bias"].reshape(N_HEADS_A, 1, 1), p["gdn_norm_w"][None], res["states"], res["inverses"], doa,
                                         nseq)
    g["a_log"], g["dt_bias"], g["gdn_norm_w"] = dalog.sum(0).reshape(-1), ddtb.sum(0).reshape(-1), dnw.sum((0, 1))
    dqkv, dcw = conv_bwd(proj, p["conv_w"], dya, nseq)
    g["conv_w"] = dcw.sum(0)
    slabs = [dqkv, dza, dqc, dkc, dvc, dzc, dqb, dkb, dvb, dzb, dba]
    hdn = res["hdn"]
    g["w_in"] = jnp.concatenate(mat_tn(hdn, slabs[:6]) + mat_tn(hdn, slabs[6:]), axis=1)
    dx, dnw_tiles = inproj_bwd(slabs, p["w_in"], res["x"], p["norm_w"][None], dy)
    g["norm_w"] = dnw_tiles.sum((0, 1))
    return dx, g


SMALL = ("norm_w", "a_log", "dt_bias", "gdn_norm_w", "q_norm_w", "k_norm_w")


def _local_step(x, target, full):
    nseq, t, d = x.shape
    tabs = _rope_tables(t)
    h = x.reshape(nseq * t, d)
    saved = []
    for l in range(DEPTH):
        p = {k: v[l] for k, v in full.items()}
        h, res = _layer_fwd(h, p, nseq, tabs)
        saved.append((p, res))
    dy, parts = loss_fwd_bwd(h, target.reshape(nseq * t, d))
    loss = parts[:, 0, 0].sum()
    grads = [None] * DEPTH
    for l in reversed(range(DEPTH)):
        p, res = saved[l]
        dy, grads[l] = _layer_bwd(dy, p, res, nseq, tabs)
    return loss, dy.reshape(nseq, t, d), {k: jnp.stack([g[k] for g in grads]) for k in grads[0]}


def _pad_cols(w):
    b0 = ORIG_A + ORIG_BA
    c0 = b0 + ORIG_B
    zeros = jnp.zeros(w.shape[:-1] + (BLOCK - ORIG_BA,), w.dtype)
    return jnp.concatenate([w[..., :ORIG_A], w[..., c0:], w[..., b0:c0], w[..., ORIG_A:b0], zeros], axis=-1)


def _unpad_cols(w):
    return jnp.concatenate([w[..., :COL_C], w[..., COL_BA:COL_BA + ORIG_BA], w[..., COL_B:COL_BA], w[..., COL_C:COL_B]],
                           axis=-1)


def kernel(x, norm_w, w_in, conv_w, a_log, dt_bias, gdn_norm_w, q_norm_w, k_norm_w, w_out, loss_target, m_norm_w, m_w_in, m_conv_w, m_a_log, m_dt_bias, m_gdn_norm_w, m_q_norm_w, m_k_norm_w, m_w_out, v_norm_w, v_w_in, v_conv_w, v_a_log, v_dt_bias, v_gdn_norm_w, v_q_norm_w, v_k_norm_w, v_w_out):
    weights = dict(norm_w=norm_w, w_in=w_in, conv_w=conv_w, a_log=a_log, dt_bias=dt_bias, gdn_norm_w=gdn_norm_w,
                   q_norm_w=q_norm_w, k_norm_w=k_norm_w, w_out=w_out)
    moms = dict(norm_w=m_norm_w, w_in=m_w_in, conv_w=m_conv_w, a_log=m_a_log, dt_bias=m_dt_bias,
                gdn_norm_w=m_gdn_norm_w, q_norm_w=m_q_norm_w, k_norm_w=m_k_norm_w, w_out=m_w_out)
    vars_ = dict(norm_w=v_norm_w, w_in=v_w_in, conv_w=v_conv_w, a_log=v_a_log, dt_bias=v_dt_bias,
                 gdn_norm_w=v_gdn_norm_w, q_norm_w=v_q_norm_w, k_norm_w=v_k_norm_w, w_out=v_w_out)
    names = list(weights)
    sharded = ("w_in", "w_out", "conv_w")
    shard_shapes = [weights[k].shape for k in sharded]

    c = lax.axis_index("c")
    half = PACK_ROWS // 2
    conv_bits = lax.bitcast_convert_type(conv_w, BF16).reshape(conv_w.shape[:2] + (2 * conv_w.shape[2],))
    shard = _pack([w_in.astype(BF16), w_out.astype(BF16), conv_bits])
    mine = plane_exchange(lax.dynamic_slice_in_dim(shard, c * half, half, axis=0), all_to_all=False)
    other = sibling_swap(mine)
    got = jnp.concatenate([jnp.where(c == 0, mine, other), jnp.where(c == 0, other, mine)], axis=1)
    per_chip = [_unpack(got[i], shard_shapes[:2] + [conv_bits.shape]) for i in range(N_CHIPS)]
    full = {k: weights[k] for k in SMALL}
    full["w_in"] = _pad_cols(jnp.concatenate([pc[0] for pc in per_chip], axis=2))
    full["w_out"] = jnp.concatenate([pc[1] for pc in per_chip], axis=1)
    full["conv_w"] = jnp.concatenate(
        [lax.bitcast_convert_type(pc[2].reshape(conv_w.shape + (2,)), F32) for pc in per_chip], axis=2)

    loss, grad_x, g = _local_step(x, loss_target, full)

    gw_in = _unpad_cols(g["w_in"])
    cols, rows = w_in.shape[2], w_out.shape[1]
    small = _pack_small(g)
    send = jnp.stack([_pack([gw_in[:, :, i * cols:(i + 1) * cols], g["w_out"][:, i * rows:(i + 1) * rows],
                             g["conv_w"][:, :, i * conv_w.shape[2]:(i + 1) * conv_w.shape[2]], small])
                      for i in range(N_CHIPS)])
    chip_sum = add_my_half(send, sibling_swap(send, other_half=True), c)
    mine = sum4(plane_exchange(chip_sum, all_to_all=True))
    other = sibling_swap(mine)
    total = jnp.concatenate([jnp.where(c == 0, mine, other), jnp.where(c == 0, other, mine)])
    reduced = _unpack(total, shard_shapes + [(ROW_TILE, PACK_COLS)])
    grads = dict(zip(sharded, reduced[:3]))
    grads.update(zip(SMALL, _unpack_small(reduced[3], [weights[k].shape for k in SMALL])))
    loss = lax.psum(loss, ("x", "y", "c"))

    def two_d(a):
        return a.reshape(-1, a.shape[-1])

    delta, new_m, new_v = {}, {}, {}
    for k in names:
        d_, m_, v_ = adamw(two_d(weights[k]), two_d(grads[k]), two_d(moms[k]), two_d(vars_[k]))
        delta[k], new_m[k], new_v[k] = (a.reshape(weights[k].shape) for a in (d_, m_, v_))
    return (loss, grad_x, *[grads[k] for k in names], *[delta[k] for k in names],
            *[new_m[k] for k in names], *[new_v[k] for k in names])
```

```python
import functools
import math

import jax
import jax.numpy as jnp
from jax import lax
from jax.experimental import pallas as pl
from jax.experimental.pallas import tpu as pltpu

F32 = jnp.float32
BF16 = jnp.bfloat16

D_MODEL = 1024
SEQ = 2048
DEPTH = 2
HEAD_DIM = 64
N_HEADS_A, N_HEADS_B, N_HEADS_C = 6, 4, 6
A_WIDTH, B_WIDTH, C_WIDTH = N_HEADS_A * HEAD_DIM, N_HEADS_B * HEAD_DIM, N_HEADS_C * HEAD_DIM
CONV_WIDTH = 4
GDN_CHUNK = 64
BLOCK = 128
ROPE_DIM = 16
ROPE_THETA = 500000.0
DILATED_PAIRS = ((128, 1), (512, 4), (2048, 16))
RMS_EPS = 1e-6
NEG = -1e30

NT = (((1,), (1,)), ((), ()))
NN = (((1,), (0,)), ((), ()))
TN = (((0,), (0,)), ((), ()))

VMEM_LIMIT = 48 * 1024 * 1024

ORIG_A = 4 * A_WIDTH
ORIG_BA = 2 * N_HEADS_A
ORIG_B = 4 * B_WIDTH
COL_AZ = 3 * A_WIDTH
COL_C = 4 * A_WIDTH
COL_B = COL_C + 4 * C_WIDTH
COL_BA = COL_B + 4 * B_WIDTH
P_COLS = COL_BA + BLOCK
TN_COLS = 384
INPROJ_COLS = P_COLS // 3
TM_ROWS = 512
ROWS = 256


def _mm(a, b, dims=NN):
    return lax.dot_general(a.astype(BF16), b.astype(BF16), dims, preferred_element_type=F32)


def _mm32(a, b, dims=NN):
    return lax.dot_general(a, b, dims, precision=lax.Precision.HIGH, preferred_element_type=F32)


def _cparams(*sem):
    return pltpu.CompilerParams(dimension_semantics=sem, vmem_limit_bytes=VMEM_LIMIT)


def _sigmoid(x):
    return 0.5 * (jnp.tanh(0.5 * x) + 1.0)


def _softplus(x):
    return jnp.maximum(x, 0.0) + jnp.log(1.0 + jnp.exp(-jnp.abs(x)))


def _rms(x, w):
    return x * lax.rsqrt(jnp.mean(x * x, axis=-1, keepdims=True) + RMS_EPS) * w


def _heads(a, n):
    return jnp.stack([a[:, h * HEAD_DIM:(h + 1) * HEAD_DIM] for h in range(n)])


def _unheads(a):
    return jnp.concatenate([a[h] for h in range(a.shape[0])], axis=1)


def _row_chunks(t):
    return [pl.ds(c * ROWS, ROWS) for c in range(t // ROWS)]


def inproj_fwd(x, nw, w):
    n, d = x.shape
    p = w.shape[1]

    def body(x_ref, nw_ref, w_ref, proj_ref, hdn_ref):
        @pl.when(pl.program_id(1) == 0)
        def _():
            hdn_ref[...] = _rms(x_ref[...], nw_ref[...]).astype(BF16)

        proj_ref[...] = jnp.dot(hdn_ref[...], w_ref[...], preferred_element_type=F32)

    return pl.pallas_call(
        body, name="inproj_fwd", grid=(n // TM_ROWS, p // INPROJ_COLS),
        in_specs=[pl.BlockSpec((TM_ROWS, d), lambda i, j: (i, 0)), pl.BlockSpec((1, d), lambda i, j: (0, 0)),
                  pl.BlockSpec((d, INPROJ_COLS), lambda i, j: (0, j))],
        out_specs=[pl.BlockSpec((TM_ROWS, INPROJ_COLS), lambda i, j: (i, j)), pl.BlockSpec((TM_ROWS, d), lambda i, j: (i, 0))],
        out_shape=[jax.ShapeDtypeStruct((n, p), F32), jax.ShapeDtypeStruct((n, d), BF16)],
        compiler_params=_cparams("parallel", "arbitrary"),
    )(x, nw, w)


def mat_tn(a, slabs):
    n, ka = a.shape
    ns = len(slabs)

    def body(*refs):
        a_ref, s_refs, o_refs = refs[0], refs[1:1 + ns], refs[1 + ns:]

        @pl.when(pl.program_id(0) == 0)
        def _():
            for o_ref in o_refs:
                o_ref[...] = jnp.zeros_like(o_ref)

        av = a_ref[...]
        for s_ref, o_ref in zip(s_refs, o_refs):
            o_ref[...] += lax.dot_general(av, s_ref[...].astype(BF16), TN, preferred_element_type=F32)

    return pl.pallas_call(
        body, name="mat_tn", grid=(n // TM_ROWS,),
        in_specs=[pl.BlockSpec((TM_ROWS, ka), lambda k: (k, 0))]
                 + [pl.BlockSpec((TM_ROWS, s.shape[1]), lambda k: (k, 0)) for s in slabs],
        out_specs=[pl.BlockSpec((ka, s.shape[1]), lambda k: (0, 0)) for s in slabs],
        out_shape=[jax.ShapeDtypeStruct((ka, s.shape[1]), F32) for s in slabs],
        compiler_params=_cparams("arbitrary"),
    )(a, *slabs)


def inproj_bwd(slabs, w, x, nw, dy):
    n, d = x.shape
    p = w.shape[1]
    tm = 256
    ns = len(slabs)

    def body(*refs):
        s_refs = refs[:ns]
        w_ref, x_ref, nw_ref, dy_ref, dx_ref, dnw_ref = refs[ns:]
        dh = jnp.zeros((tm, d), F32)
        at = 0
        for s_ref in s_refs:
            wd = s_ref.shape[1]
            dh = dh + lax.dot_general(s_ref[...].astype(BF16), w_ref[:, at:at + wd], NT, preferred_element_type=F32)
            at += wd
        _, vjp = jax.vjp(_rms, x_ref[...], nw_ref[...])
        dx, dnw = vjp(dh)
        dx_ref[...] = dx + dy_ref[...]
        dnw_ref[0] = dnw

    return pl.pallas_call(
        body, name="inproj_bwd", grid=(n // tm,),
        in_specs=[pl.BlockSpec((tm, s.shape[1]), lambda i: (i, 0)) for s in slabs]
                 + [pl.BlockSpec((d, p), lambda i: (0, 0)), pl.BlockSpec((tm, d), lambda i: (i, 0)),
                    pl.BlockSpec((1, d), lambda i: (0, 0)), pl.BlockSpec((tm, d), lambda i: (i, 0))],
        out_specs=[pl.BlockSpec((tm, d), lambda i: (i, 0)), pl.BlockSpec((1, 1, d), lambda i: (i, 0, 0))],
        out_shape=[jax.ShapeDtypeStruct((n, d), F32), jax.ShapeDtypeStruct((n // tm, 1, d), F32)],
        compiler_params=_cparams("parallel"),
    )(*slabs, w, x, nw, dy)


CONV_PAD = 8
CONV_ROWS = 256


def _conv_pre(pad_s, cw, c):
    xs = [pad_s[pl.ds(c * CONV_ROWS + CONV_PAD - (CONV_WIDTH - 1) + k, CONV_ROWS), :] for k in range(CONV_WIDTH)]
    pre = xs[0] * cw[0:1, :]
    for k in range(1, CONV_WIDTH):
        pre = pre + xs[k] * cw[k:k + 1, :]
    return pre, xs


def conv_fwd(proj, cw, nseq):
    n = proj.shape[0]
    t = n // nseq
    ch = cw.shape[1]

    def body(x_ref, cw_ref, y_ref, pad_s):
        pad_s[pl.ds(0, CONV_PAD), :] = jnp.zeros((CONV_PAD, TN_COLS), F32)
        pad_s[pl.ds(CONV_PAD, t), :] = x_ref[...]
        cwv = cw_ref[...]
        for c in range(t // CONV_ROWS):
            pre, _ = _conv_pre(pad_s, cwv, c)
            y_ref[pl.ds(c * CONV_ROWS, CONV_ROWS), :] = pre * _sigmoid(pre)

    return pl.pallas_call(
        body, name="conv_fwd", grid=(nseq, ch // TN_COLS),
        in_specs=[pl.BlockSpec((t, TN_COLS), lambda b, j: (b, j)), pl.BlockSpec((CONV_WIDTH, TN_COLS), lambda b, j: (0, j))],
        out_specs=pl.BlockSpec((t, TN_COLS), lambda b, j: (b, j)),
        out_shape=jax.ShapeDtypeStruct((n, ch), F32),
        scratch_shapes=[pltpu.VMEM((t + CONV_PAD, TN_COLS), F32)],
        compiler_params=_cparams("parallel", "parallel"),
    )(proj, cw)


def conv_bwd(proj, cw, dy, nseq):
    n = proj.shape[0]
    t = n // nseq
    ch = cw.shape[1]

    def body(x_ref, cw_ref, dy_ref, dx_ref, dcw_ref, pad_s, dpad_s):
        pad_s[pl.ds(0, CONV_PAD), :] = jnp.zeros((CONV_PAD, TN_COLS), F32)
        pad_s[pl.ds(CONV_PAD, t), :] = x_ref[...]
        dpad_s[pl.ds(t, CONV_PAD), :] = jnp.zeros((CONV_PAD, TN_COLS), F32)
        cwv = cw_ref[...]
        acc = [jnp.zeros((1, TN_COLS), F32)] * CONV_WIDTH
        for c in range(t // CONV_ROWS):
            pre, xs = _conv_pre(pad_s, cwv, c)
            sg = _sigmoid(pre)
            dpre = dy_ref[pl.ds(c * CONV_ROWS, CONV_ROWS), :] * (sg * (1.0 + pre * (1.0 - sg)))
            dpad_s[pl.ds(c * CONV_ROWS, CONV_ROWS), :] = dpre
            acc = [acc[k] + jnp.sum(dpre * xs[k], axis=0, keepdims=True) for k in range(CONV_WIDTH)]
        for k in range(CONV_WIDTH):
            dcw_ref[0, pl.ds(k, 1), :] = acc[k]
        for c in range(t // CONV_ROWS):
            dx = dpad_s[pl.ds(c * CONV_ROWS + CONV_WIDTH - 1, CONV_ROWS), :] * cwv[0:1, :]
            for k in range(1, CONV_WIDTH):
                dx = dx + dpad_s[pl.ds(c * CONV_ROWS + CONV_WIDTH - 1 - k, CONV_ROWS), :] * cwv[k:k + 1, :]
            dx_ref[pl.ds(c * CONV_ROWS, CONV_ROWS), :] = dx

    blk = pl.BlockSpec((t, TN_COLS), lambda b, j: (b, j))
    return pl.pallas_call(
        body, name="conv_bwd", grid=(nseq, ch // TN_COLS),
        in_specs=[blk, pl.BlockSpec((CONV_WIDTH, TN_COLS), lambda b, j: (0, j)), blk],
        out_specs=[blk, pl.BlockSpec((1, CONV_WIDTH, TN_COLS), lambda b, j: (b, 0, j))],
        out_shape=[jax.ShapeDtypeStruct((n, ch), F32), jax.ShapeDtypeStruct((nseq, CONV_WIDTH, ch), F32)],
        scratch_shapes=[pltpu.VMEM((t + CONV_PAD, TN_COLS), F32)] * 2,
        compiler_params=_cparams("parallel", "parallel"),
    )(proj, cw, dy)


def _gate_specs(d):
    wide = pl.BlockSpec((TM_ROWS, d), lambda i: (i, 0))
    oa = pl.BlockSpec((TM_ROWS, A_WIDTH), lambda i: (i, 0))
    ob = pl.BlockSpec((TM_ROWS, B_WIDTH), lambda i: (i, 0))
    oc = pl.BlockSpec((TM_ROWS, C_WIDTH), lambda i: (i, 0))
    za = pl.BlockSpec((TM_ROWS, A_WIDTH), lambda i: (i, COL_AZ // A_WIDTH))
    zb = pl.BlockSpec((TM_ROWS, B_WIDTH), lambda i: (i, (COL_B + 3 * B_WIDTH) // B_WIDTH))
    zc = pl.BlockSpec((TM_ROWS, C_WIDTH), lambda i: (i, (COL_C + 3 * C_WIDTH) // C_WIDTH))
    return wide, oa, ob, oc, za, zb, zc


BRANCH_COLS = ((0, A_WIDTH), (A_WIDTH, A_WIDTH + B_WIDTH), (A_WIDTH + B_WIDTH, D_MODEL))


def outproj_fwd(x, oa, ob, oc, proj, w):
    n, d = x.shape

    def body(x_ref, oa_ref, ob_ref, oc_ref, za_ref, zb_ref, zc_ref, w_ref, y_ref, m_ref):
        for (lo, hi), o_ref, z_ref in zip(BRANCH_COLS, (oa_ref, ob_ref, oc_ref), (za_ref, zb_ref, zc_ref)):
            zv = z_ref[...]
            m_ref[:, lo:hi] = (o_ref[...] * (zv * _sigmoid(zv))).astype(BF16)
        y_ref[...] = x_ref[...] + jnp.dot(m_ref[...], w_ref[...], preferred_element_type=F32)

    wide, sa, sb, sc, za, zb, zc = _gate_specs(d)
    return pl.pallas_call(
        body, name="outproj_fwd", grid=(n // TM_ROWS,),
        in_specs=[wide, sa, sb, sc, za, zb, zc, pl.BlockSpec((d, d), lambda i: (0, 0))],
        out_specs=[wide, wide],
        out_shape=[jax.ShapeDtypeStruct((n, d), F32), jax.ShapeDtypeStruct((n, d), BF16)],
        compiler_params=_cparams("parallel"),
    )(x, oa, ob, oc, proj, proj, proj, w)


def outproj_bwd(dy, oa, ob, oc, proj, w):
    n, d = dy.shape

    def body(dy_ref, oa_ref, ob_ref, oc_ref, za_ref, zb_ref, zc_ref, w_ref, doa_ref, dob_ref, doc_ref, dza_ref, dzb_ref, dzc_ref):
        dm = lax.dot_general(dy_ref[...].astype(BF16), w_ref[...], NT, preferred_element_type=F32)
        for (lo, hi), o_ref, z_ref, do_ref, dz_ref in zip(BRANCH_COLS, (oa_ref, ob_ref, oc_ref), (za_ref, zb_ref, zc_ref),
                                                          (doa_ref, dob_ref, doc_ref), (dza_ref, dzb_ref, dzc_ref)):
            zv = z_ref[...]
            sg = _sigmoid(zv)
            dmv = dm[:, lo:hi]
            do_ref[...] = dmv * (zv * sg)
            dz_ref[...] = dmv * o_ref[...] * (sg * (1.0 + zv * (1.0 - sg)))

    wide, sa, sb, sc, za, zb, zc = _gate_specs(d)
    sd = jax.ShapeDtypeStruct
    outs = [sd((n, A_WIDTH), F32), sd((n, B_WIDTH), F32), sd((n, C_WIDTH), F32)]
    return pl.pallas_call(
        body, name="outproj_bwd", grid=(n // TM_ROWS,),
        in_specs=[wide, sa, sb, sc, za, zb, zc, pl.BlockSpec((d, d), lambda i: (0, 0))],
        out_specs=[sa, sb, sc, sa, sb, sc],
        out_shape=outs + outs,
        compiler_params=_cparams("parallel"),
    )(dy, oa, ob, oc, proj, proj, proj, w)


def loss_fwd_bwd(y, target):
    n, d = y.shape

    def body(y_ref, t_ref, dy_ref, part_ref):
        e = y_ref[...] - t_ref[...]
        dy_ref[...] = e * (1.0 / d)
        part_ref[...] = jnp.zeros_like(part_ref) + 0.5 * jnp.sum(e * e) * (1.0 / d)

    blk = pl.BlockSpec((TM_ROWS, d), lambda i: (i, 0))
    return pl.pallas_call(
        body, name="loss", grid=(n // TM_ROWS,),
        in_specs=[blk, blk],
        out_specs=[blk, pl.BlockSpec((1, 8, BLOCK), lambda i: (i, 0, 0))],
        out_shape=[jax.ShapeDtypeStruct((n, d), F32), jax.ShapeDtypeStruct((n // TM_ROWS, 8, BLOCK), F32)],
        compiler_params=_cparams("parallel"),
    )(y, target)


ADAM_LR, ADAM_B1, ADAM_B2, ADAM_EPS, ADAM_WD, ADAM_STEP = 0.001, 0.9, 0.999, 1e-08, 0.01, 10


def adamw(w, g, m, v):
    r, c = w.shape
    tr = r if r <= 256 else 256

    def body(w_ref, g_ref, m_ref, v_ref, d_ref, nm_ref, nv_ref):
        gv = g_ref[...]
        nm = ADAM_B1 * m_ref[...] + (1.0 - ADAM_B1) * gv
        nv = ADAM_B2 * v_ref[...] + (1.0 - ADAM_B2) * (gv * gv)
        m_hat = nm / (1.0 - ADAM_B1 ** ADAM_STEP)
        v_hat = nv / (1.0 - ADAM_B2 ** ADAM_STEP)
        d_ref[...] = -ADAM_LR * (m_hat / (jnp.sqrt(v_hat) + ADAM_EPS) + ADAM_WD * w_ref[...])
        nm_ref[...] = nm
        nv_ref[...] = nv

    blk = pl.BlockSpec((tr, c), lambda i: (i, 0))
    return pl.pallas_call(
        body, name="adamw", grid=(r // tr,),
        in_specs=[blk] * 4, out_specs=[blk] * 3,
        out_shape=[jax.ShapeDtypeStruct((r, c), F32)] * 3,
        compiler_params=_cparams("parallel"),
    )(w, g, m, v)


SB_G = N_HEADS_B


def _sb_weights(qs, k, carry, tri, diag):
    z = _mm(qs, k, NT)
    sp = jnp.log(1.0 + jnp.exp(-jnp.abs(z)))
    ls_pos = jnp.minimum(z, 0.0) - sp
    ls_neg = jnp.minimum(-z, 0.0) - sp
    earlier = (lax.broadcasted_iota(jnp.int32, z.shape, 1) < lax.broadcasted_iota(jnp.int32, z.shape, 0)) if diag else None
    log_keep = jnp.where(earlier, ls_neg, 0.0) if diag else ls_neg
    hi = log_keep.astype(BF16)
    lo = (log_keep - hi.astype(F32)).astype(BF16)
    within = lax.dot_general(jnp.concatenate([hi, lo], axis=1), tri, NN, preferred_element_type=F32)
    arg = ls_pos + within + carry
    wts = jnp.where(earlier, jnp.exp(jnp.where(earlier, arg, 0.0)), 0.0) if diag else jnp.exp(arg)
    return ls_pos, ls_neg, log_keep, wts, earlier


def _sb_tile(q, k, v, carry, tri, diag):
    _, _, log_keep, wts, _ = _sb_weights(q * (HEAD_DIM ** -0.5), k, carry, tri, diag)
    return _mm(wts, v), jnp.sum(log_keep, axis=1, keepdims=True)


def _sb_tile_grads(q, k, v, carry, do, dtot, tri, diag):
    qs = q * (HEAD_DIM ** -0.5)
    ls_pos, ls_neg, _, wts, earlier = _sb_weights(qs, k, carry, tri, diag)
    dv = _mm(wts, do, TN)
    darg = _mm(do, v, NT) * wts
    dkeep = _mm(darg, tri[:BLOCK], NT) + dtot
    if diag:
        dkeep = jnp.where(earlier, dkeep, 0.0)
    dz = darg * jnp.exp(ls_neg) - dkeep * jnp.exp(ls_pos)
    return _mm(dz, k) * (HEAD_DIM ** -0.5), _mm(dz, qs, TN), dv, jnp.sum(darg, axis=1, keepdims=True)


_sb_tiles_diag = jax.vmap(functools.partial(_sb_tile, diag=True), in_axes=(0, 0, 0, 0, None))
_sb_tiles_off = jax.vmap(functools.partial(_sb_tile, diag=False), in_axes=(0, 0, 0, 0, None))
_sb_grads_diag = jax.vmap(functools.partial(_sb_tile_grads, diag=True), in_axes=(0, 0, 0, 0, 0, 0, None))
_sb_grads_off = jax.vmap(functools.partial(_sb_tile_grads, diag=False), in_axes=(0, 0, 0, 0, 0, 0, None))


def _sb_tri():
    r = lax.broadcasted_iota(jnp.int32, (2 * BLOCK, BLOCK), 0) % BLOCK
    c = lax.broadcasted_iota(jnp.int32, (2 * BLOCK, BLOCK), 1)
    return jnp.where(r > c, 1.0, 0.0).astype(BF16)


SB_SEQ = 2


def _sb_specs(t, nq):
    cb = COL_B // B_WIDTH
    sq = SB_SEQ
    q = pl.BlockSpec((sq, BLOCK, B_WIDTH), lambda b, i: (b, i, cb))
    k = pl.BlockSpec((sq, t, B_WIDTH), lambda b, i: (b, 0, cb + 1))
    v = pl.BlockSpec((sq, t, B_WIDTH), lambda b, i: (b, 0, cb + 2))
    blk = pl.BlockSpec((sq, BLOCK, B_WIDTH), lambda b, i: (b, i, 0))
    full = pl.BlockSpec((sq, t, B_WIDTH), lambda b, i: (b, 0, 0))
    carry = pl.BlockSpec((sq, 1, nq, BLOCK, SB_G), lambda b, i: (b, i, 0, 0, 0))
    return q, k, v, blk, full, carry


def _sb_heads(ref, rows):
    return jnp.concatenate([_heads(ref[b, rows, :], SB_G) for b in range(SB_SEQ)])


def _sb_unheads(a):
    return [_unheads(a[b * SB_G:(b + 1) * SB_G]) for b in range(SB_SEQ)]


def sb_fwd(proj, nseq):
    n = proj.shape[0]
    t = n // nseq
    nq = t // BLOCK
    g, sq = SB_G, SB_SEQ
    everything = pl.ds(0, BLOCK)

    def body(q_ref, k_ref, v_ref, o_ref, carry_ref):
        i = pl.program_id(1)
        tri = _sb_tri()
        qv = _sb_heads(q_ref, everything)

        def tile(j, c, fn):
            rows = pl.ds(pl.multiple_of(j * BLOCK, BLOCK), BLOCK)
            for b in range(sq):
                carry_ref[b, 0, j] = jnp.concatenate([c[b * g + h] for h in range(g)], axis=1)
            return fn(qv, _sb_heads(k_ref, rows), _sb_heads(v_ref, rows), c, tri)

        def step(it, st):
            o_acc, c = st
            o, tot = tile(i - 1 - it, c, _sb_tiles_off)
            return o_acc + o, c + tot

        o_acc, _ = lax.fori_loop(0, i, step, tile(i, jnp.zeros((sq * g, BLOCK, 1), F32), _sb_tiles_diag))
        for b, o in enumerate(_sb_unheads(o_acc)):
            o_ref[b] = o

    q, k, v, blk, _, carry = _sb_specs(t, nq)
    proj3 = proj.reshape(nseq, t, -1)
    o, carries = pl.pallas_call(
        body, name="sb_fwd", grid=(nseq // sq, nq),
        in_specs=[q, k, v],
        out_specs=[blk, carry],
        out_shape=[jax.ShapeDtypeStruct((nseq, t, B_WIDTH), F32),
                   jax.ShapeDtypeStruct((nseq, nq, nq, BLOCK, g), F32)],
        compiler_params=_cparams("parallel", "arbitrary"),
    )(proj3, proj3, proj3)
    return o.reshape(n, B_WIDTH), carries


def sb_bwd(proj, carries, do, nseq):
    n = proj.shape[0]
    t = n // nseq
    nq = t // BLOCK
    g, sq = SB_G, SB_SEQ
    everything = pl.ds(0, BLOCK)

    def body(q_ref, k_ref, v_ref, carry_ref, do_ref, dq_ref, dk_ref, dv_ref):
        i = pl.program_id(1)

        @pl.when(i == 0)
        def _():
            dk_ref[...] = jnp.zeros_like(dk_ref)
            dv_ref[...] = jnp.zeros_like(dv_ref)

        tri = _sb_tri()
        qv = _sb_heads(q_ref, everything)
        dov = _sb_heads(do_ref, everything)

        def tile(j, st, fn):
            dq_acc, dc = st
            rows = pl.ds(pl.multiple_of(j * BLOCK, BLOCK), BLOCK)
            cj = [carry_ref[b, 0, j] for b in range(sq)]
            dq, dk, dv, dcj = fn(qv, _sb_heads(k_ref, rows), _sb_heads(v_ref, rows),
                                 jnp.stack([cj[b][:, h:h + 1] for b in range(sq) for h in range(g)]), dov, dc, tri)
            for b, (dkb, dvb) in enumerate(zip(_sb_unheads(dk), _sb_unheads(dv))):
                dk_ref[b, rows, :] += dkb
                dv_ref[b, rows, :] += dvb
            return dq_acc + dq, dc + dcj

        st = lax.fori_loop(0, i, lambda j, st: tile(j, st, _sb_grads_off),
                           (jnp.zeros((sq * g, BLOCK, HEAD_DIM), F32), jnp.zeros((sq * g, BLOCK, 1), F32)))
        dq_acc, _ = tile(i, st, _sb_grads_diag)
        for b, dq in enumerate(_sb_unheads(dq_acc)):
            dq_ref[b] = dq

    q, k, v, blk, full, carry = _sb_specs(t, nq)
    proj3, do3 = proj.reshape(nseq, t, -1), do.reshape(nseq, t, -1)
    grads = pl.pallas_call(
        body, name="sb_bwd", grid=(nseq // sq, nq),
        in_specs=[q, k, v, carry, blk],
        out_specs=[blk, full, full],
        out_shape=[jax.ShapeDtypeStruct((nseq, t, B_WIDTH), F32)] * 3,
        compiler_params=_cparams("parallel", "arbitrary"),
    )(proj3, proj3, proj3, carries, do3)
    return [a.reshape(n, B_WIDTH) for a in grads]


def _unit_lower_inverse(a):
    n = a.shape[0]
    eye = jnp.where(lax.broadcasted_iota(jnp.int32, (n, n), 0) == lax.broadcasted_iota(jnp.int32, (n, n), 1), 1.0, 0.0)
    tmat = eye.astype(F32) - a
    p = a
    for _ in range(5):
        p = _mm32(p, p)
        tmat = tmat + _mm32(tmat, p)
    return tmat


@jax.custom_vjp
def _known_inverse(a, tmat):
    return tmat


def _known_inverse_fwd(a, tmat):
    return tmat, tmat


def _known_inverse_bwd(tmat, g):
    return -_mm32(_mm32(tmat, g, TN), tmat, NT), jnp.zeros_like(tmat)


_known_inverse.defvjp(_known_inverse_fwd, _known_inverse_bwd)


def _gdn_chunk(q, k, v, al_c, al_r, br_c, alog, dtb, nw, s, tmat_in):
    c = GDN_CHUNK
    ri = lax.broadcasted_iota(jnp.int32, (c, c), 0)
    ci = lax.broadcasted_iota(jnp.int32, (c, c), 1)
    incl, strict = ri >= ci, ri > ci
    rate = -jnp.exp(alog)
    g_c = rate * _softplus(al_c + dtb)
    g_r = rate * _softplus(al_r + dtb)
    beta = _sigmoid(br_c)
    gc_c = jnp.sum(jnp.where(incl, g_r, 0.0), axis=1, keepdims=True)
    gc_r = jnp.sum(jnp.where(ri <= ci, g_c, 0.0), axis=0, keepdims=True)
    gl = jnp.sum(g_r, axis=1, keepdims=True)
    decay = jnp.where(incl, jnp.exp(jnp.where(incl, gc_c - gc_r, 0.0)), 0.0)
    qn = q * lax.rsqrt(jnp.sum(q * q, axis=-1, keepdims=True) + RMS_EPS) * (HEAD_DIM ** -0.5)
    kn = k * lax.rsqrt(jnp.sum(k * k, axis=-1, keepdims=True) + RMS_EPS)
    kb = kn * beta
    a = jnp.where(strict, _mm(kb, kn, NT) * decay, 0.0)
    tmat = _unit_lower_inverse(a) if tmat_in is None else _known_inverse(a, tmat_in)
    u = _mm(tmat, v * beta)
    w = _mm(tmat, kb * jnp.exp(gc_c))
    qk = _mm(qn, kn, NT) * decay
    v_new = u - _mm(w, s)
    o = _mm(qn * jnp.exp(gc_c), s) + _mm(qk, v_new)
    s_new = s * jnp.exp(gl) + _mm(kn * jnp.exp(gl - gc_c), v_new, TN)
    o = o * lax.rsqrt(jnp.mean(o * o, axis=-1, keepdims=True) + RMS_EPS) * nw
    return o, s_new, tmat


_gdn_chunks_fwd = jax.vmap(functools.partial(_gdn_chunk, tmat_in=None), in_axes=(0, 0, 0, 0, 0, 0, 0, 0, None, 0))
_gdn_chunks_bwd = jax.vmap(_gdn_chunk, in_axes=(0, 0, 0, 0, 0, 0, 0, 0, None, 0, 0))

GDN_TB = 256
GDN_SEQ_FWD = 2
GDN_SEQ_BWD = 2


def _gdn_block(q3, k3, v3, ba, alog, dtb, nw, s, tm=None):
    nh = N_HEADS_A
    ns = q3.shape[0]
    bat = [ba[b].T for b in range(ns)]
    br_c = jnp.stack([ba[b][:, h:h + 1] for b in range(ns) for h in range(nh)])
    al_c = jnp.stack([ba[b][:, nh + h:nh + h + 1] for b in range(ns) for h in range(nh)])
    al_r = jnp.stack([bat[b][nh + h:nh + h + 1, :] for b in range(ns) for h in range(nh)])
    heads = lambda a: jnp.concatenate([_heads(a[b], nh) for b in range(ns)])
    args = (heads(q3), heads(k3), heads(v3), al_c, al_r, br_c, jnp.concatenate([alog] * ns), jnp.concatenate([dtb] * ns), nw, s)
    o, s_new, tmat = _gdn_chunks_fwd(*args) if tm is None else _gdn_chunks_bwd(*args, tm)
    o3 = jnp.stack([_unheads(o[b * nh:(b + 1) * nh]) for b in range(ns)])
    return (o3, s_new, tmat) if tm is None else (o3, s_new)


def _gdn_specs(nt, sq, rev):
    tpos = (lambda i: nt - 1 - i) if rev else (lambda i: i)
    ncb = GDN_TB // GDN_CHUNK
    qkv = [pl.BlockSpec((sq, GDN_TB, A_WIDTH), lambda b, i, j=j: (b, tpos(i), j)) for j in range(3)]
    ba = pl.BlockSpec((sq, GDN_TB, BLOCK), lambda b, i: (b, tpos(i), COL_BA // BLOCK))
    one = pl.BlockSpec((N_HEADS_A, 1, 1), lambda b, i: (0, 0, 0))
    vec = pl.BlockSpec((1, HEAD_DIM), lambda b, i: (0, 0))
    st = pl.BlockSpec((sq, ncb, N_HEADS_A, HEAD_DIM, HEAD_DIM), lambda b, i: (b, tpos(i), 0, 0, 0))
    oa = pl.BlockSpec((sq, GDN_TB, A_WIDTH), lambda b, i: (b, tpos(i), 0))
    return qkv, ba, one, vec, st, oa, tpos


def gdn_fwd(ya, proj, alog, dtb, nw, nseq):
    n = ya.shape[0]
    t = n // nseq
    nc, nt, ncb = t // GDN_CHUNK, t // GDN_TB, GDN_TB // GDN_CHUNK
    sq = GDN_SEQ_FWD
    nh = N_HEADS_A

    def body(q_ref, k_ref, v_ref, ba_ref, alog_ref, dtb_ref, nw_ref, o_ref, st_ref, tm_ref, s_s):
        @pl.when(pl.program_id(1) == 0)
        def _():
            s_s[...] = jnp.zeros_like(s_s)

        def step(c, s):
            rows = pl.ds(pl.multiple_of(c * GDN_CHUNK, GDN_CHUNK), GDN_CHUNK)
            o, s_new, tmat = _gdn_block(q_ref[:, rows, :], k_ref[:, rows, :], v_ref[:, rows, :], ba_ref[:, rows, :],
                                        alog_ref[...], dtb_ref[...], nw_ref[...], s)
            for b in range(sq):
                st_ref[b, c] = s[b * nh:(b + 1) * nh]
                tm_ref[b, c] = tmat[b * nh:(b + 1) * nh]
            o_ref[:, rows, :] = o
            return s_new

        s_s[...] = lax.fori_loop(0, ncb, step, s_s[...])

    qkv, ba, one, vec, st, oa, _ = _gdn_specs(nt, sq, False)
    ya3, proj3 = ya.reshape(nseq, t, -1), proj.reshape(nseq, t, -1)
    per_chunk = jax.ShapeDtypeStruct((nseq, nc, nh, HEAD_DIM, HEAD_DIM), F32)
    o, states, inverses = pl.pallas_call(
        body, name="gdn_fwd", grid=(nseq // sq, nt),
        in_specs=qkv + [ba, one, one, vec],
        out_specs=[oa, st, st],
        out_shape=[jax.ShapeDtypeStruct((nseq, t, A_WIDTH), F32), per_chunk, per_chunk],
        scratch_shapes=[pltpu.VMEM((sq * nh, HEAD_DIM, HEAD_DIM), F32)],
        compiler_params=_cparams("parallel", "arbitrary"),
    )(ya3, ya3, ya3, proj3, alog, dtb, nw)
    return o.reshape(n, A_WIDTH), states, inverses


def gdn_bwd(ya, proj, alog, dtb, nw, states, inverses, do, nseq):
    n = ya.shape[0]
    t = n // nseq
    nt, ncb = t // GDN_TB, GDN_TB // GDN_CHUNK
    nh = N_HEADS_A
    sq = GDN_SEQ_BWD

    def body(q_ref, k_ref, v_ref, ba_ref, alog_ref, dtb_ref, nw_ref, st_ref, tm_ref, do_ref,
             dya_ref, dba_ref, dalog_ref, ddtb_ref, dnw_ref, ds_s):
        @pl.when(pl.program_id(1) == 0)
        def _():
            ds_s[...] = jnp.zeros_like(ds_s)
            dalog_ref[...] = jnp.zeros_like(dalog_ref)
            ddtb_ref[...] = jnp.zeros_like(ddtb_ref)
            dnw_ref[...] = jnp.zeros_like(dnw_ref)

        def step(it, carry):
            ds, dalog, ddtb, dnw = carry
            c = ncb - 1 - it
            rows = pl.ds(pl.multiple_of(c * GDN_CHUNK, GDN_CHUNK), GDN_CHUNK)
            s_in = jnp.concatenate([st_ref[b, c] for b in range(sq)])
            tm_in = jnp.concatenate([tm_ref[b, c] for b in range(sq)])
            _, vjp = jax.vjp(functools.partial(_gdn_block, tm=tm_in), q_ref[:, rows, :], k_ref[:, rows, :], v_ref[:, rows, :],
                             ba_ref[:, rows, :], alog_ref[...], dtb_ref[...], nw_ref[...], s_in)
            dq, dk, dv, dba, da, dd, dn, ds = vjp((do_ref[:, rows, :], ds))
            dya_ref[:, rows, 0:A_WIDTH] = dq
            dya_ref[:, rows, A_WIDTH:2 * A_WIDTH] = dk
            dya_ref[:, rows, 2 * A_WIDTH:3 * A_WIDTH] = dv
            dba_ref[:, rows, :] = dba
            return ds, dalog + da, ddtb + dd, dnw + dn

        z11 = jnp.zeros((nh, 1, 1), F32)
        ds, dalog, ddtb, dnw = lax.fori_loop(0, ncb, step, (ds_s[...], z11, z11, jnp.zeros((1, HEAD_DIM), F32)))
        ds_s[...] = ds
        dalog_ref[0] += dalog
        ddtb_ref[0] += ddtb
        dnw_ref[0] += dnw

    qkv, ba, one, vec, st, oa, tpos = _gdn_specs(nt, sq, True)
    per_grp = pl.BlockSpec((1, nh, 1, 1), lambda b, i: (b, 0, 0, 0))
    sd = jax.ShapeDtypeStruct
    ya3, proj3, do3 = ya.reshape(nseq, t, -1), proj.reshape(nseq, t, -1), do.reshape(nseq, t, -1)
    dya, dba, dalog, ddtb, dnw = pl.pallas_call(
        body, name="gdn_bwd", grid=(nseq // sq, nt),
        in_specs=qkv + [ba, one, one, vec, st, st, oa],
        out_specs=[pl.BlockSpec((sq, GDN_TB, 3 * A_WIDTH), lambda b, i: (b, tpos(i), 0)),
                   pl.BlockSpec((sq, GDN_TB, BLOCK), lambda b, i: (b, tpos(i), 0)),
                   per_grp, per_grp, pl.BlockSpec((1, 1, HEAD_DIM), lambda b, i: (b, 0, 0))],
        out_shape=[sd((nseq, t, 3 * A_WIDTH), F32), sd((nseq, t, BLOCK), F32), sd((nseq // sq, nh, 1, 1), F32),
                   sd((nseq // sq, nh, 1, 1), F32), sd((nseq // sq, 1, HEAD_DIM), F32)],
        scratch_shapes=[pltpu.VMEM((sq * nh, HEAD_DIM, HEAD_DIM), F32)],
        compiler_params=_cparams("parallel", "arbitrary"),
    )(ya3, ya3, ya3, proj3, alog, dtb, nw, states, inverses, do3)
    return dya.reshape(n, 3 * A_WIDTH), dba.reshape(n, BLOCK), dalog, ddtb, dnw


DIL_NB = tuple((SEQ // d) // BLOCK for _, d in DILATED_PAIRS)
DIL_D = tuple(d for _, d in DILATED_PAIRS)
DIL_STEPS = tuple(w // d for w, d in DILATED_PAIRS)
DIL_B = 8
PAIR = 2 * HEAD_DIM


def _rope_tables(t):
    half = ROPE_DIM // 2
    inv_freq = ROPE_THETA ** (-jnp.arange(half, dtype=F32) / half)
    ang = jnp.arange(t, dtype=F32)[:, None] * inv_freq[None, :]
    ones = jnp.ones((t, HEAD_DIM - ROPE_DIM), F32)
    cs = jnp.concatenate([jnp.cos(ang), jnp.cos(ang), ones], axis=1)
    sn = jnp.concatenate([jnp.sin(ang), jnp.sin(ang), 0.0 * ones], axis=1)
    i = jnp.arange(PAIR)[:, None]
    j = jnp.arange(PAIR)[None, :]
    same = (i // HEAD_DIM) == (j // HEAD_DIM)
    ih, jh = i % HEAD_DIM, j % HEAD_DIM
    pm = (jnp.where(same & (jh < half) & (ih == jh + half), -1.0, 0.0)
          + jnp.where(same & (jh >= half) & (jh < ROPE_DIM) & (ih == jh - half), 1.0, 0.0))
    mean = jnp.where(same, 1.0 / HEAD_DIM, 0.0)
    twice = lambda m: jnp.concatenate([m, m]).astype(BF16)
    return jnp.tile(cs, (1, 2)), jnp.tile(sn, (1, 2)), twice(mean), twice(pm)


def _split_dot(x, w2):
    hi = x.astype(BF16)
    lo = lax.stop_gradient(x - hi.astype(F32)).astype(BF16)
    return lax.dot_general(jnp.concatenate([hi, lo], axis=1), w2, NN, preferred_element_type=F32)


def _dil_prep(x, w, cs, sn, mean2, pm2):
    y = x * lax.rsqrt(_split_dot(x * x, mean2) + RMS_EPS) * w
    return y * cs + _split_dot(y, pm2) * sn


def _dil_tile(qn, kk, vv, bias):
    lane = lax.broadcasted_iota(jnp.int32, (1, PAIR), 1)
    outs, lses = [], []
    for h in range(2):
        s = _mm(jnp.where(lane // HEAD_DIM == h, qn, 0.0) * (HEAD_DIM ** -0.5), kk, NT) + bias
        m = lax.stop_gradient(jnp.max(s, axis=-1, keepdims=True))
        p = jnp.exp(s - m)
        denom = jnp.sum(p, axis=-1, keepdims=True)
        outs.append(_mm(p, vv) / denom)
        lses.append(m + jnp.log(denom))
    return jnp.where(lane < HEAD_DIM, outs[0], outs[1]), jnp.concatenate(lses, axis=1)


_dil_tiles = jax.vmap(_dil_tile)


def _spread(a):
    lane = lax.broadcasted_iota(jnp.int32, (a.shape[0], PAIR), 1)
    return jnp.where(lane < HEAD_DIM, a[:, 0:1], a[:, 1:2])


def _dil_mix(o1, o2, o3, l1, l2, l3):
    m = lax.stop_gradient(jnp.maximum(jnp.maximum(l1, l2), l3))
    e1, e2, e3 = jnp.exp(l1 - m), jnp.exp(l2 - m), jnp.exp(l3 - m)
    r = 1.0 / (e1 + e2 + e3)
    return _spread(e1 * r) * o1 + _spread(e2 * r) * o2 + _spread(e3 * r) * o3


def _dil_fill_biases(bias_s):
    steps, = set(DIL_STEPS)
    qi = lax.broadcasted_iota(jnp.int32, (BLOCK, 1), 0)
    kj = lax.broadcasted_iota(jnp.int32, (1, 2 * BLOCK), 1)
    rel = qi - kj + BLOCK
    inside = (rel >= 0) & (rel <= steps)
    bias_s[0] = jnp.where(inside, 0.0, NEG)
    bias_s[1] = jnp.where(inside & (kj >= BLOCK), 0.0, NEG)
    bias_s[2] = jnp.where((qi >= kj) & (qi - kj <= steps), 0.0, NEG)


def _dil_mask(it, g, bias_s):
    qrows = pl.ds(pl.multiple_of(it * BLOCK, BLOCK), BLOCK)
    if DIL_NB[g] == 1:
        return bias_s[2, :, 0:BLOCK], qrows, qrows
    which = jnp.where(it == 0, 2, jnp.where(it % DIL_NB[g] == 0, 1, 0))
    kstart = jnp.maximum(it - 1, 0) * BLOCK
    return bias_s[which], qrows, pl.ds(pl.multiple_of(kstart, BLOCK), 2 * BLOCK)


def _dil_gather(src, dst, d):
    t = src.shape[0]
    ln = t // d
    for r in range(d):
        dst[pl.ds(r * ln, ln), :] = src[pl.ds(r, ln, stride=d), :]


def _dil_scatter(src, dst, d):
    t = src.shape[0]
    ln = t // d
    for r in range(d):
        dst[pl.ds(r, ln, stride=d), :] = src[pl.ds(r * ln, ln), :]


def _dil_forward_parts(q_ref, k_ref, v_ref, qw, kw, cs_ref, sn_ref, mean2, pm2, qn_s, kn_s, dl_s, od_s, ld_s, on_s, ln_s, bias_s):
    t = qn_s.shape[0]
    _dil_fill_biases(bias_s)

    def prep(c, _):
        rows = pl.ds(pl.multiple_of(c * ROWS, ROWS), ROWS)
        qn_s[rows, :] = _dil_prep(q_ref[rows, :], qw, cs_ref[rows, :], sn_ref[rows, :], mean2, pm2)
        kn_s[rows, :] = _dil_prep(k_ref[rows, :], kw, cs_ref[rows, :], sn_ref[rows, :], mean2, pm2)
        return 0

    lax.fori_loop(0, t // ROWS, prep, 0)
    for g in (1, 2):
        _dil_gather(qn_s, dl_s.at[g - 1, 0], DIL_D[g])
        _dil_gather(kn_s, dl_s.at[g - 1, 1], DIL_D[g])
        _dil_gather(v_ref, dl_s.at[g - 1, 2], DIL_D[g])
    for g in range(3):
        qs = qn_s if g == 0 else dl_s.at[g - 1, 0]
        ks = kn_s if g == 0 else dl_s.at[g - 1, 1]
        vs = v_ref if g == 0 else dl_s.at[g - 1, 2]

        def tiles(i, _, g=g, qs=qs, ks=ks, vs=vs):
            where = [_dil_mask(i * DIL_B + b, g, bias_s) for b in range(DIL_B)]
            o, lse = _dil_tiles(jnp.stack([qs[qr, :] for _, qr, _ in where]), jnp.stack([ks[kr, :] for _, _, kr in where]),
                                jnp.stack([vs[kr, :] for _, _, kr in where]), jnp.stack([m for m, _, _ in where]))
            for b, (_, qr, _) in enumerate(where):
                od_s[g, qr, :] = o[b]
                ld_s[g, qr, :] = lse[b]
            return 0

        lax.fori_loop(0, t // BLOCK // DIL_B, tiles, 0)
    for g in (1, 2):
        _dil_scatter(od_s.at[g], on_s.at[g - 1], DIL_D[g])
        _dil_scatter(ld_s.at[g], ln_s.at[g - 1], DIL_D[g])


def _dil_scratch(t):
    return [pltpu.VMEM((t, PAIR), F32), pltpu.VMEM((t, PAIR), F32),
            pltpu.VMEM((2, 3, t, PAIR), F32),
            pltpu.VMEM((3, t, PAIR), F32), pltpu.VMEM((3, t, 2), F32),
            pltpu.VMEM((2, t, PAIR), F32), pltpu.VMEM((2, t, 2), F32),
            pltpu.VMEM((3, BLOCK, 2 * BLOCK), F32)]


def _dil_specs(t):
    cb = COL_C // BLOCK
    per = C_WIDTH // BLOCK
    qkv = [pl.BlockSpec((t, BLOCK), lambda b, p, j=j: (b, cb + j * per + p)) for j in range(3)]
    vec = pl.BlockSpec((1, PAIR), lambda b, p: (0, 0))
    tab = pl.BlockSpec((t, PAIR), lambda b, p: (0, 0))
    mat = pl.BlockSpec((2 * PAIR, PAIR), lambda b, p: (0, 0))
    pair = pl.BlockSpec((t, BLOCK), lambda b, p: (b, p))
    return qkv, vec, tab, mat, pair


def dil_fwd(proj, qw, kw, cs, sn, mean2, pm2, nseq):
    n = proj.shape[0]
    t = n // nseq

    def body(q_ref, k_ref, v_ref, qw_ref, kw_ref, cs_ref, sn_ref, mean_ref, pm_ref, o_ref,
             qn_s, kn_s, dl_s, od_s, ld_s, on_s, ln_s, bias_s):
        _dil_forward_parts(q_ref, k_ref, v_ref, qw_ref[...], kw_ref[...], cs_ref, sn_ref, mean_ref[...], pm_ref[...],
                           qn_s, kn_s, dl_s, od_s, ld_s, on_s, ln_s, bias_s)

        def mix(c, _):
            rows = pl.ds(pl.multiple_of(c * ROWS, ROWS), ROWS)
            o_ref[rows, :] = _dil_mix(od_s[0, rows, :], on_s[0, rows, :], on_s[1, rows, :],
                                      ld_s[0, rows, :], ln_s[0, rows, :], ln_s[1, rows, :])
            return 0

        lax.fori_loop(0, t // ROWS, mix, 0)

    qkv, vec, tab, mat, pair = _dil_specs(t)
    return pl.pallas_call(
        body, name="dil_fwd", grid=(nseq, C_WIDTH // BLOCK),
        in_specs=qkv + [vec, vec, tab, tab, mat, mat],
        out_specs=pair,
        out_shape=jax.ShapeDtypeStruct((n, C_WIDTH), F32),
        scratch_shapes=_dil_scratch(t),
        compiler_params=_cparams("parallel", "parallel"),
    )(proj, proj, proj, qw, kw, cs, sn, mean2, pm2)


def dil_bwd(proj, qw, kw, cs, sn, mean2, pm2, do, nseq):
    n = proj.shape[0]
    t = n // nseq

    def body(q_ref, k_ref, v_ref, qw_ref, kw_ref, cs_ref, sn_ref, mean_ref, pm_ref, do_ref,
             dq_ref, dk_ref, dv_ref, dqw_ref, dkw_ref,
             qn_s, kn_s, dl_s, od_s, ld_s, on_s, ln_s, bias_s, tq_s, tk_s, tv_s):
        qw, kw, mean2, pm2 = qw_ref[...], kw_ref[...], mean_ref[...], pm_ref[...]
        _dil_forward_parts(q_ref, k_ref, v_ref, qw, kw, cs_ref, sn_ref, mean2, pm2, qn_s, kn_s, dl_s, od_s, ld_s, on_s, ln_s, bias_s)

        def mix(c, _):
            rows = pl.ds(pl.multiple_of(c * ROWS, ROWS), ROWS)
            _, vjp = jax.vjp(_dil_mix, od_s[0, rows, :], on_s[0, rows, :], on_s[1, rows, :],
                             ld_s[0, rows, :], ln_s[0, rows, :], ln_s[1, rows, :])
            d1, d2, d3, e1, e2, e3 = vjp(do_ref[rows, :])
            od_s[0, rows, :] = d1
            on_s[0, rows, :] = d2
            on_s[1, rows, :] = d3
            ld_s[0, rows, :] = e1
            ln_s[0, rows, :] = e2
            ln_s[1, rows, :] = e3
            return 0

        lax.fori_loop(0, t // ROWS, mix, 0)
        for g in (1, 2):
            _dil_gather(on_s.at[g - 1], od_s.at[g], DIL_D[g])
            _dil_gather(ln_s.at[g - 1], ld_s.at[g], DIL_D[g])
        on_s[...] = jnp.zeros_like(on_s)
        dv_ref[...] = jnp.zeros_like(dv_ref)
        for g in range(3):
            qs = qn_s if g == 0 else dl_s.at[g - 1, 0]
            ks = kn_s if g == 0 else dl_s.at[g - 1, 1]
            vs = v_ref if g == 0 else dl_s.at[g - 1, 2]
            gq = on_s.at[0] if g == 0 else tq_s
            gk = on_s.at[1] if g == 0 else tk_s
            gv = dv_ref if g == 0 else tv_s
            if g > 0:
                tk_s[...] = jnp.zeros_like(tk_s)
                tv_s[...] = jnp.zeros_like(tv_s)

            def tiles(i, _, g=g, qs=qs, ks=ks, vs=vs, gq=gq, gk=gk, gv=gv):
                where = [_dil_mask(i * DIL_B + b, g, bias_s) for b in range(DIL_B)]
                biases = jnp.stack([m for m, _, _ in where])
                _, vjp = jax.vjp(lambda q_, k_, v_: _dil_tiles(q_, k_, v_, biases),
                                 jnp.stack([qs[qr, :] for _, qr, _ in where]), jnp.stack([ks[kr, :] for _, _, kr in where]),
                                 jnp.stack([vs[kr, :] for _, _, kr in where]))
                dq, dkk, dvv = vjp((jnp.stack([od_s[g, qr, :] for _, qr, _ in where]),
                                    jnp.stack([ld_s[g, qr, :] for _, qr, _ in where])))
                for b, (_, qr, kr) in enumerate(where):
                    gq[qr, :] = dq[b]
                    gk[kr, :] += dkk[b]
                    gv[kr, :] += dvv[b]
                return 0

            lax.fori_loop(0, t // BLOCK // DIL_B, tiles, 0)
            if g > 0:
                d = DIL_D[g]
                ln = t // d
                for r in range(d):
                    nat, dil = pl.ds(r, ln, stride=d), pl.ds(r * ln, ln)
                    on_s[0, nat, :] += tq_s[dil, :]
                    on_s[1, nat, :] += tk_s[dil, :]
                    dv_ref[nat, :] += tv_s[dil, :]

        def prep(c, acc):
            rows = pl.ds(pl.multiple_of(c * ROWS, ROWS), ROWS)
            f = lambda x, w: _dil_prep(x, w, cs_ref[rows, :], sn_ref[rows, :], mean2, pm2)
            _, vq = jax.vjp(f, q_ref[rows, :], qw)
            _, vk = jax.vjp(f, k_ref[rows, :], kw)
            dq, dqw = vq(on_s[0, rows, :])
            dk, dkw = vk(on_s[1, rows, :])
            dq_ref[rows, :] = dq
            dk_ref[rows, :] = dk
            return acc[0] + dqw, acc[1] + dkw

        dqw, dkw = lax.fori_loop(0, t // ROWS, prep, (jnp.zeros((1, PAIR), F32), jnp.zeros((1, PAIR), F32)))
        dqw_ref[0] = dqw
        dkw_ref[0] = dkw

    qkv, vec, tab, mat, pair = _dil_specs(t)
    per = C_WIDTH // BLOCK
    wout = pl.BlockSpec((1, 1, PAIR), lambda b, p: (b * per + p, 0, 0))
    return pl.pallas_call(
        body, name="dil_bwd", grid=(nseq, per),
        in_specs=qkv + [vec, vec, tab, tab, mat, mat, pair],
        out_specs=[pair, pair, pair, wout, wout],
        out_shape=[jax.ShapeDtypeStruct((n, C_WIDTH), F32)] * 3 + [jax.ShapeDtypeStruct((nseq * per, 1, PAIR), F32)] * 2,
        scratch_shapes=_dil_scratch(t) + [pltpu.VMEM((t, PAIR), F32)] * 3,
        compiler_params=_cparams("parallel", "parallel"),
    )(proj, proj, proj, qw, kw, cs, sn, mean2, pm2, do)


N_CHIPS = 4
SUM_ROWS = 432
MESH_IDS = pl.DeviceIdType.MESH
ANY = pl.BlockSpec(memory_space=pl.ANY)


def plane_exchange(src, all_to_all):
    blk_shape = src.shape[1:] if all_to_all else src.shape

    def body(src_ref, out_ref, send_sems, recv_sems, local_sem):
        x, y, c = lax.axis_index("x"), lax.axis_index("y"), lax.axis_index("c")
        me = 2 * x + y
        mine = pltpu.make_async_copy(src_ref.at[me] if all_to_all else src_ref, out_ref.at[me], local_sem)
        mine.start()
        sends = []
        for k in (1, 2, 3):
            px = 1 - x if k & 2 else x
            py = 1 - y if k & 1 else y
            peer = 2 * px + py
            cp = pltpu.make_async_remote_copy(
                src_ref=src_ref.at[peer] if all_to_all else src_ref, dst_ref=out_ref.at[me],
                send_sem=send_sems.at[k - 1], recv_sem=recv_sems.at[k - 1],
                device_id=(px, py, c), device_id_type=MESH_IDS)
            cp.start()
            sends.append((cp, peer, (px, py, c)))
        for k, (cp, peer, dev) in enumerate(sends):
            pltpu.make_async_remote_copy(
                src_ref=out_ref.at[me], dst_ref=out_ref.at[peer],
                send_sem=send_sems.at[k], recv_sem=recv_sems.at[k],
                device_id=dev, device_id_type=MESH_IDS).wait_recv()
        for cp, _, _ in sends:
            cp.wait_send()
        mine.wait()

    return pl.pallas_call(
        body, name="plane_all_to_all" if all_to_all else "plane_all_gather",
        in_specs=[ANY], out_specs=ANY,
        out_shape=jax.ShapeDtypeStruct((N_CHIPS,) + blk_shape, src.dtype),
        scratch_shapes=[pltpu.SemaphoreType.DMA((3,)), pltpu.SemaphoreType.DMA((3,)), pltpu.SemaphoreType.DMA],
    )(src)


def sibling_swap(src, other_half=False):
    shape = (src.shape[0], src.shape[1] // 2) + src.shape[2:] if other_half else src.shape

    def body(src_ref, out_ref, send_sem, recv_sem):
        x, y, c = lax.axis_index("x"), lax.axis_index("y"), lax.axis_index("c")
        part = src_ref.at[:, pl.ds((1 - c) * shape[1], shape[1])] if other_half else src_ref
        cp = pltpu.make_async_remote_copy(src_ref=part, dst_ref=out_ref, send_sem=send_sem, recv_sem=recv_sem,
                                          device_id=(x, y, 1 - c), device_id_type=MESH_IDS)
        cp.start()
        cp.wait()

    return pl.pallas_call(
        body, name="sibling_swap", in_specs=[ANY], out_specs=ANY,
        out_shape=jax.ShapeDtypeStruct(shape, src.dtype),
        scratch_shapes=[pltpu.SemaphoreType.DMA, pltpu.SemaphoreType.DMA],
    )(src)


def sum4(a):
    _, r, c = a.shape
    tr = SUM_ROWS

    def body(a_ref, o_ref):
        p = [a_ref[i].astype(F32) for i in range(N_CHIPS)]
        o_ref[...] = (p[0] + p[1]) + (p[2] + p[3])

    return pl.pallas_call(
        body, name="sum4", grid=(r // tr,),
        in_specs=[pl.BlockSpec((N_CHIPS, tr, c), lambda i: (0, i, 0))],
        out_specs=pl.BlockSpec((tr, c), lambda i: (i, 0)),
        out_shape=jax.ShapeDtypeStruct((r, c), F32),
        compiler_params=_cparams("parallel"),
    )(a)


def add_my_half(mine, got, c):
    nchip, r2, cols = mine.shape
    nt = r2 // 2 // SUM_ROWS

    def body(c_ref, a_ref, b_ref, o_ref):
        o_ref[...] = (a_ref[...] + b_ref[...]).astype(BF16)

    blk = pl.BlockSpec((1, SUM_ROWS, cols), lambda j, i, c_ref: (j, i, 0))
    return pl.pallas_call(
        body, name="add_my_half",
        grid_spec=pltpu.PrefetchScalarGridSpec(
            num_scalar_prefetch=1, grid=(nchip, nt),
            in_specs=[pl.BlockSpec((1, SUM_ROWS, cols), lambda j, i, c_ref: (j, c_ref[0] * nt + i, 0)), blk],
            out_specs=blk),
        out_shape=jax.ShapeDtypeStruct((nchip, r2 // 2, cols), BF16),
        compiler_params=_cparams("parallel", "parallel"),
    )(jnp.reshape(c, (1,)).astype(jnp.int32), mine, got)


PACK_COLS = 1152
PACK_ROWS = 2592
ROW_TILE = 16


def _pack(parts):
    blocks = []
    for p in parts:
        p2 = p.reshape(-1, p.shape[-1])
        blocks.append(jnp.pad(p2, ((0, -p2.shape[0] % ROW_TILE), (0, PACK_COLS - p2.shape[1]))))
    rows = sum(b.shape[0] for b in blocks)
    blocks.append(jnp.zeros((PACK_ROWS - rows, PACK_COLS), blocks[0].dtype))
    return jnp.concatenate(blocks)


def _unpack(buf, shapes):
    out, at = [], 0
    for s in shapes:
        rows = math.prod(s[:-1])
        out.append(buf[at:at + rows, :s[-1]].reshape(s))
        at += rows + (-rows % ROW_TILE)
    return out


def _pack_small(g):
    blk = jnp.zeros((ROW_TILE, PACK_COLS), F32)
    for i, k in enumerate(SMALL):
        blk = blk.at[2 * i:2 * i + 2, :g[k].shape[1]].set(g[k])
    return blk


def _unpack_small(blk, shapes):
    return [blk[2 * i:2 * i + 2, :s[1]] for i, s in enumerate(shapes)]


def _layer_fwd(x, p, nseq, tabs):
    proj, hdn = inproj_fwd(x, p["norm_w"][None], p["w_in"])
    ya = conv_fwd(proj, p["conv_w"], nseq)
    oa, states, inverses = gdn_fwd(ya, proj, p["a_log"].reshape(N_HEADS_A, 1, 1), p["dt_bias"].reshape(N_HEADS_A, 1, 1),
                         p["gdn_norm_w"][None], nseq)
    ob, carries = sb_fwd(proj, nseq)
    oc = dil_fwd(proj, jnp.tile(p["q_norm_w"], 2)[None], jnp.tile(p["k_norm_w"], 2)[None], *tabs, nseq)
    y, mixed = outproj_fwd(x, oa, ob, oc, proj, p["w_out"])
    return y, dict(x=x, hdn=hdn, proj=proj, ya=ya, states=states, inverses=inverses, carries=carries, oa=oa, ob=ob, oc=oc, mixed=mixed)


def _layer_bwd(dy, p, res, nseq, tabs):
    proj = res["proj"]
    g = {}
    g["w_out"] = mat_tn(res["mixed"], [dy])[0]
    doa, dob, doc, dza, dzb, dzc = outproj_bwd(dy, res["oa"], res["ob"], res["oc"], proj, p["w_out"])
    dqc, dkc, dvc, dqw, dkw = dil_bwd(proj, jnp.tile(p["q_norm_w"], 2)[None], jnp.tile(p["k_norm_w"], 2)[None], *tabs, doc,
                                      nseq)
    g["q_norm_w"], g["k_norm_w"] = dqw.reshape(-1, HEAD_DIM).sum(0), dkw.reshape(-1, HEAD_DIM).sum(0)
    dqb, dkb, dvb = sb_bwd(proj, res["carries"], dob, nseq)
    dya, dba, dalog, ddtb, dnw = gdn_bwd(res["ya"], proj, p["a_log"].reshape(N_HEADS_A, 1, 1),
                                         p["dt_bias"].reshape(N_HEADS_A, 1, 1), p["gdn_norm_w"][None], res["states"], res["inverses"], doa,
                                         nseq)
    g["a_log"], g["dt_bias"], g["gdn_norm_w"] = dalog.sum(0).reshape(-1), ddtb.sum(0).reshape(-1), dnw.sum((0, 1))
    dqkv, dcw = conv_bwd(proj, p["conv_w"], dya, nseq)
    g["conv_w"] = dcw.sum(0)
    slabs = [dqkv, dza, dqc, dkc, dvc, dzc, dqb, dkb, dvb, dzb, dba]
    hdn = res["hdn"]
    g["w_in"] = jnp.concatenate(mat_tn(hdn, slabs[:6]) + mat_tn(hdn, slabs[6:]), axis=1)
    dx, dnw_tiles = inproj_bwd(slabs, p["w_in"], res["x"], p["norm_w"][None], dy)
    g["norm_w"] = dnw_tiles.sum((0, 1))
    return dx, g


SMALL = ("norm_w", "a_log", "dt_bias", "gdn_norm_w", "q_norm_w", "k_norm_w")


def _local_step(x, target, full):
    nseq, t, d = x.shape
    tabs = _rope_tables(t)
    h = x.reshape(nseq * t, d)
    saved = []
    for l in range(DEPTH):
        p = {k: v[l] for k, v in full.items()}
        h, res = _layer_fwd(h, p, nseq, tabs)
        saved.append((p, res))
    dy, parts = loss_fwd_bwd(h, target.reshape(nseq * t, d))
    loss = parts[:, 0, 0].sum()
    grads = [None] * DEPTH
    for l in reversed(range(DEPTH)):
        p, res = saved[l]
        dy, grads[l] = _layer_bwd(dy, p, res, nseq, tabs)
    return loss, dy.reshape(nseq, t, d), {k: jnp.stack([g[k] for g in grads]) for k in grads[0]}


def _pad_cols(w):
    b0 = ORIG_A + ORIG_BA
    c0 = b0 + ORIG_B
    zeros = jnp.zeros(w.shape[:-1] + (BLOCK - ORIG_BA,), w.dtype)
    return jnp.concatenate([w[..., :ORIG_A], w[..., c0:], w[..., b0:c0], w[..., ORIG_A:b0], zeros], axis=-1)


def _unpad_cols(w):
    return jnp.concatenate([w[..., :COL_C], w[..., COL_BA:COL_BA + ORIG_BA], w[..., COL_B:COL_BA], w[..., COL_C:COL_B]],
                           axis=-1)


def kernel(x, norm_w, w_in, conv_w, a_log, dt_bias, gdn_norm_w, q_norm_w, k_norm_w, w_out, loss_target, m_norm_w, m_w_in, m_conv_w, m_a_log, m_dt_bias, m_gdn_norm_w, m_q_norm_w, m_k_norm_w, m_w_out, v_norm_w, v_w_in, v_conv_w, v_a_log, v_dt_bias, v_gdn_norm_w, v_q_norm_w, v_k_norm_w, v_w_out):
    weights = dict(norm_w=norm_w, w_in=w_in, conv_w=conv_w, a_log=a_log, dt_bias=dt_bias, gdn_norm_w=gdn_norm_w,
                   q_norm_w=q_norm_w, k_norm_w=k_norm_w, w_out=w_out)
    moms = dict(norm_w=m_norm_w, w_in=m_w_in, conv_w=m_conv_w, a_log=m_a_log, dt_bias=m_dt_bias,
                gdn_norm_w=m_gdn_norm_w, q_norm_w=m_q_norm_w, k_norm_w=m_k_norm_w, w_out=m_w_out)
    vars_ = dict(norm_w=v_norm_w, w_in=v_w_in, conv_w=v_conv_w, a_log=v_a_log, dt_bias=v_dt_bias,
                 gdn_norm_w=v_gdn_norm_w, q_norm_w=v_q_norm_w, k_norm_w=v_k_norm_w, w_out=v_w_out)
    names = list(weights)
    sharded = ("w_in", "w_out", "conv_w")
    shard_shapes = [weights[k].shape for k in sharded]

    c = lax.axis_index("c")
    half = PACK_ROWS // 2
    conv_bits = lax.bitcast_convert_type(conv_w, BF16).reshape(conv_w.shape[:2] + (2 * conv_w.shape[2],))
    shard = _pack([w_in.astype(BF16), w_out.astype(BF16), conv_bits])
    mine = plane_exchange(lax.dynamic_slice_in_dim(shard, c * half, half, axis=0), all_to_all=False)
    other = sibling_swap(mine)
    got = jnp.concatenate([jnp.where(c == 0, mine, other), jnp.where(c == 0, other, mine)], axis=1)
    per_chip = [_unpack(got[i], shard_shapes[:2] + [conv_bits.shape]) for i in range(N_CHIPS)]
    full = {k: weights[k] for k in SMALL}
    full["w_in"] = _pad_cols(jnp.concatenate([pc[0] for pc in per_chip], axis=2))
    full["w_out"] = jnp.concatenate([pc[1] for pc in per_chip], axis=1)
    full["conv_w"] = jnp.concatenate(
        [lax.bitcast_convert_type(pc[2].reshape(conv_w.shape + (2,)), F32) for pc in per_chip], axis=2)

    loss, grad_x, g = _local_step(x, loss_target, full)

    gw_in = _unpad_cols(g["w_in"])
    cols, rows = w_in.shape[2], w_out.shape[1]
    small = _pack_small(g)
    send = jnp.stack([_pack([gw_in[:, :, i * cols:(i + 1) * cols], g["w_out"][:, i * rows:(i + 1) * rows],
                             g["conv_w"][:, :, i * conv_w.shape[2]:(i + 1) * conv_w.shape[2]], small])
                      for i in range(N_CHIPS)])
    chip_sum = add_my_half(send, sibling_swap(send, other_half=True), c)
    mine = sum4(plane_exchange(chip_sum, all_to_all=True))
    other = sibling_swap(mine)
    total = jnp.concatenate([jnp.where(c == 0, mine, other), jnp.where(c == 0, other, mine)])
    reduced = _unpack(total, shard_shapes + [(ROW_TILE, PACK_COLS)])
    grads = dict(zip(sharded, reduced[:3]))
    grads.update(zip(SMALL, _unpack_small(reduced[3], [weights[k].shape for k in SMALL])))
    loss = lax.psum(loss, ("x", "y", "c"))

    def two_d(a):
        return a.reshape(-1, a.shape[-1])

    delta, new_m, new_v = {}, {}, {}
    for k in names:
        d_, m_, v_ = adamw(two_d(weights[k]), two_d(grads[k]), two_d(moms[k]), two_d(vars_[k]))
        delta[k], new_m[k], new_v[k] = (a.reshape(weights[k].shape) for a in (d_, m_, v_))
    return (loss, grad_x, *[grads[k] for k in names], *[delta[k] for k in names],
            *[new_m[k] for k in names], *[new_v[k] for k in names])
```

```python
import functools
import math

import jax
import jax.numpy as jnp
from jax import lax
from jax.experimental import pallas as pl
from jax.experimental.pallas import tpu as pltpu

F32 = jnp.float32
BF16 = jnp.bfloat16

D_MODEL = 1024
SEQ = 2048
DEPTH = 2
HEAD_DIM = 64
N_HEADS_A, N_HEADS_B, N_HEADS_C = 6, 4, 6
A_WIDTH, B_WIDTH, C_WIDTH = N_HEADS_A * HEAD_DIM, N_HEADS_B * HEAD_DIM, N_HEADS_C * HEAD_DIM
CONV_WIDTH = 4
GDN_CHUNK = 64
BLOCK = 128
ROPE_DIM = 16
ROPE_THETA = 500000.0
DILATED_PAIRS = ((128, 1), (512, 4), (2048, 16))
RMS_EPS = 1e-6
NEG = -1e30

NT = (((1,), (1,)), ((), ()))
NN = (((1,), (0,)), ((), ()))
TN = (((0,), (0,)), ((), ()))

VMEM_LIMIT = 48 * 1024 * 1024

ORIG_A = 4 * A_WIDTH
ORIG_BA = 2 * N_HEADS_A
ORIG_B = 4 * B_WIDTH
COL_AZ = 3 * A_WIDTH
COL_C = 4 * A_WIDTH
COL_B = COL_C + 4 * C_WIDTH
COL_BA = COL_B + 4 * B_WIDTH
P_COLS = COL_BA + BLOCK
TN_COLS = 384
INPROJ_COLS = P_COLS // 3
TM_ROWS = 512
ROWS = 512


def _mm(a, b, dims=NN):
    return lax.dot_general(a.astype(BF16), b.astype(BF16), dims, preferred_element_type=F32)


def _mm32(a, b, dims=NN):
    return lax.dot_general(a, b, dims, precision=lax.Precision.HIGH, preferred_element_type=F32)


def _cparams(*sem):
    return pltpu.CompilerParams(dimension_semantics=sem, vmem_limit_bytes=VMEM_LIMIT)


def _sigmoid(x):
    return 0.5 * (jnp.tanh(0.5 * x) + 1.0)


def _softplus(x):
    return jnp.maximum(x, 0.0) + jnp.log(1.0 + jnp.exp(-jnp.abs(x)))


def _rms(x, w):
    return x * lax.rsqrt(jnp.mean(x * x, axis=-1, keepdims=True) + RMS_EPS) * w


def _heads(a, n):
    return jnp.stack([a[:, h * HEAD_DIM:(h + 1) * HEAD_DIM] for h in range(n)])


def _unheads(a):
    return jnp.concatenate([a[h] for h in range(a.shape[0])], axis=1)


def inproj_fwd(x, nw, w):
    n, d = x.shape
    p = w.shape[1]

    def body(x_ref, nw_ref, w_ref, proj_ref, hdn_ref):
        @pl.when(pl.program_id(1) == 0)
        def _():
            hdn_ref[...] = _rms(x_ref[...], nw_ref[...]).astype(BF16)

        proj_ref[...] = jnp.dot(hdn_ref[...], w_ref[...], preferred_element_type=F32)

    return pl.pallas_call(
        body, name="inproj_fwd", grid=(n // TM_ROWS, p // INPROJ_COLS),
        in_specs=[pl.BlockSpec((TM_ROWS, d), lambda i, j: (i, 0)), pl.BlockSpec((1, d), lambda i, j: (0, 0)),
                  pl.BlockSpec((d, INPROJ_COLS), lambda i, j: (0, j))],
        out_specs=[pl.BlockSpec((TM_ROWS, INPROJ_COLS), lambda i, j: (i, j)), pl.BlockSpec((TM_ROWS, d), lambda i, j: (i, 0))],
        out_shape=[jax.ShapeDtypeStruct((n, p), F32), jax.ShapeDtypeStruct((n, d), BF16)],
        compiler_params=_cparams("parallel", "arbitrary"),
    )(x, nw, w)


def mat_tn(a, slabs):
    n, ka = a.shape
    ns = len(slabs)

    def body(*refs):
        a_ref, s_refs, o_refs = refs[0], refs[1:1 + ns], refs[1 + ns:]

        @pl.when(pl.program_id(0) == 0)
        def _():
            for o_ref in o_refs:
                o_ref[...] = jnp.zeros_like(o_ref)

        av = a_ref[...]
        for s_ref, o_ref in zip(s_refs, o_refs):
            o_ref[...] += lax.dot_general(av, s_ref[...].astype(BF16), TN, preferred_element_type=F32)

    return pl.pallas_call(
        body, name="mat_tn", grid=(n // TM_ROWS,),
        in_specs=[pl.BlockSpec((TM_ROWS, ka), lambda k: (k, 0))]
                 + [pl.BlockSpec((TM_ROWS, s.shape[1]), lambda k: (k, 0)) for s in slabs],
        out_specs=[pl.BlockSpec((ka, s.shape[1]), lambda k: (0, 0)) for s in slabs],
        out_shape=[jax.ShapeDtypeStruct((ka, s.shape[1]), F32) for s in slabs],
        compiler_params=_cparams("arbitrary"),
    )(a, *slabs)


def inproj_bwd(slabs, w, x, nw, dy):
    n, d = x.shape
    p = w.shape[1]
    tm = 256
    ns = len(slabs)

    def body(*refs):
        s_refs = refs[:ns]
        w_ref, x_ref, nw_ref, dy_ref, dx_ref, dnw_ref = refs[ns:]
        dh = jnp.zeros((tm, d), F32)
        at = 0
        for s_ref in s_refs:
            wd = s_ref.shape[1]
            dh = dh + lax.dot_general(s_ref[...].astype(BF16), w_ref[:, at:at + wd], NT, preferred_element_type=F32)
            at += wd
        _, vjp = jax.vjp(_rms, x_ref[...], nw_ref[...])
        dx, dnw = vjp(dh)
        dx_ref[...] = dx + dy_ref[...]
        dnw_ref[0] = dnw

    return pl.pallas_call(
        body, name="inproj_bwd", grid=(n // tm,),
        in_specs=[pl.BlockSpec((tm, s.shape[1]), lambda i: (i, 0)) for s in slabs]
                 + [pl.BlockSpec((d, p), lambda i: (0, 0)), pl.BlockSpec((tm, d), lambda i: (i, 0)),
                    pl.BlockSpec((1, d), lambda i: (0, 0)), pl.BlockSpec((tm, d), lambda i: (i, 0))],
        out_specs=[pl.BlockSpec((tm, d), lambda i: (i, 0)), pl.BlockSpec((1, 1, d), lambda i: (i, 0, 0))],
        out_shape=[jax.ShapeDtypeStruct((n, d), F32), jax.ShapeDtypeStruct((n // tm, 1, d), F32)],
        compiler_params=_cparams("parallel"),
    )(*slabs, w, x, nw, dy)


CONV_PAD = 8
CONV_ROWS = 256


def _conv_pre(pad_s, cw, c):
    xs = [pad_s[pl.ds(c * CONV_ROWS + CONV_PAD - (CONV_WIDTH - 1) + k, CONV_ROWS), :] for k in range(CONV_WIDTH)]
    pre = xs[0] * cw[0:1, :]
    for k in range(1, CONV_WIDTH):
        pre = pre + xs[k] * cw[k:k + 1, :]
    return pre, xs


def conv_fwd(proj, cw, nseq):
    n = proj.shape[0]
    t = n // nseq
    ch = cw.shape[1]

    def body(x_ref, cw_ref, y_ref, pad_s):
        pad_s[pl.ds(0, CONV_PAD), :] = jnp.zeros((CONV_PAD, TN_COLS), F32)
        pad_s[pl.ds(CONV_PAD, t), :] = x_ref[...]
        cwv = cw_ref[...]
        for c in range(t // CONV_ROWS):
            pre, _ = _conv_pre(pad_s, cwv, c)
            y_ref[pl.ds(c * CONV_ROWS, CONV_ROWS), :] = pre * _sigmoid(pre)

    return pl.pallas_call(
        body, name="conv_fwd", grid=(nseq, ch // TN_COLS),
        in_specs=[pl.BlockSpec((t, TN_COLS), lambda b, j: (b, j)), pl.BlockSpec((CONV_WIDTH, TN_COLS), lambda b, j: (0, j))],
        out_specs=pl.BlockSpec((t, TN_COLS), lambda b, j: (b, j)),
        out_shape=jax.ShapeDtypeStruct((n, ch), F32),
        scratch_shapes=[pltpu.VMEM((t + CONV_PAD, TN_COLS), F32)],
        compiler_params=_cparams("parallel", "parallel"),
    )(proj, cw)


def conv_bwd(proj, cw, dy, nseq):
    n = proj.shape[0]
    t = n // nseq
    ch = cw.shape[1]

    def body(x_ref, cw_ref, dy_ref, dx_ref, dcw_ref, pad_s, dpad_s):
        pad_s[pl.ds(0, CONV_PAD), :] = jnp.zeros((CONV_PAD, TN_COLS), F32)
        pad_s[pl.ds(CONV_PAD, t), :] = x_ref[...]
        dpad_s[pl.ds(t, CONV_PAD), :] = jnp.zeros((CONV_PAD, TN_COLS), F32)
        cwv = cw_ref[...]
        acc = [jnp.zeros((1, TN_COLS), F32)] * CONV_WIDTH
        for c in range(t // CONV_ROWS):
            pre, xs = _conv_pre(pad_s, cwv, c)
            sg = _sigmoid(pre)
            dpre = dy_ref[pl.ds(c * CONV_ROWS, CONV_ROWS), :] * (sg * (1.0 + pre * (1.0 - sg)))
            dpad_s[pl.ds(c * CONV_ROWS, CONV_ROWS), :] = dpre
            acc = [acc[k] + jnp.sum(dpre * xs[k], axis=0, keepdims=True) for k in range(CONV_WIDTH)]
        for k in range(CONV_WIDTH):
            dcw_ref[0, pl.ds(k, 1), :] = acc[k]
        for c in range(t // CONV_ROWS):
            dx = dpad_s[pl.ds(c * CONV_ROWS + CONV_WIDTH - 1, CONV_ROWS), :] * cwv[0:1, :]
            for k in range(1, CONV_WIDTH):
                dx = dx + dpad_s[pl.ds(c * CONV_ROWS + CONV_WIDTH - 1 - k, CONV_ROWS), :] * cwv[k:k + 1, :]
            dx_ref[pl.ds(c * CONV_ROWS, CONV_ROWS), :] = dx

    blk = pl.BlockSpec((t, TN_COLS), lambda b, j: (b, j))
    return pl.pallas_call(
        body, name="conv_bwd", grid=(nseq, ch // TN_COLS),
        in_specs=[blk, pl.BlockSpec((CONV_WIDTH, TN_COLS), lambda b, j: (0, j)), blk],
        out_specs=[blk, pl.BlockSpec((1, CONV_WIDTH, TN_COLS), lambda b, j: (b, 0, j))],
        out_shape=[jax.ShapeDtypeStruct((n, ch), F32), jax.ShapeDtypeStruct((nseq, CONV_WIDTH, ch), F32)],
        scratch_shapes=[pltpu.VMEM((t + CONV_PAD, TN_COLS), F32)] * 2,
        compiler_params=_cparams("parallel", "parallel"),
    )(proj, cw, dy)


def _gate_specs(d):
    wide = pl.BlockSpec((TM_ROWS, d), lambda i: (i, 0))
    oa = pl.BlockSpec((TM_ROWS, A_WIDTH), lambda i: (i, 0))
    ob = pl.BlockSpec((TM_ROWS, B_WIDTH), lambda i: (i, 0))
    oc = pl.BlockSpec((TM_ROWS, C_WIDTH), lambda i: (i, 0))
    za = pl.BlockSpec((TM_ROWS, A_WIDTH), lambda i: (i, COL_AZ // A_WIDTH))
    zb = pl.BlockSpec((TM_ROWS, B_WIDTH), lambda i: (i, (COL_B + 3 * B_WIDTH) // B_WIDTH))
    zc = pl.BlockSpec((TM_ROWS, C_WIDTH), lambda i: (i, (COL_C + 3 * C_WIDTH) // C_WIDTH))
    return wide, oa, ob, oc, za, zb, zc


BRANCH_COLS = ((0, A_WIDTH), (A_WIDTH, A_WIDTH + B_WIDTH), (A_WIDTH + B_WIDTH, D_MODEL))


def outproj_fwd(x, oa, ob, oc, proj, w):
    n, d = x.shape

    def body(x_ref, oa_ref, ob_ref, oc_ref, za_ref, zb_ref, zc_ref, w_ref, y_ref, m_ref):
        for (lo, hi), o_ref, z_ref in zip(BRANCH_COLS, (oa_ref, ob_ref, oc_ref), (za_ref, zb_ref, zc_ref)):
            zv = z_ref[...]
            m_ref[:, lo:hi] = (o_ref[...] * (zv * _sigmoid(zv))).astype(BF16)
        y_ref[...] = x_ref[...] + jnp.dot(m_ref[...], w_ref[...], preferred_element_type=F32)

    wide, sa, sb, sc, za, zb, zc = _gate_specs(d)
    return pl.pallas_call(
        body, name="outproj_fwd", grid=(n // TM_ROWS,),
        in_specs=[wide, sa, sb, sc, za, zb, zc, pl.BlockSpec((d, d), lambda i: (0, 0))],
        out_specs=[wide, wide],
        out_shape=[jax.ShapeDtypeStruct((n, d), F32), jax.ShapeDtypeStruct((n, d), BF16)],
        compiler_params=_cparams("parallel"),
    )(x, oa, ob, oc, proj, proj, proj, w)


def outproj_bwd(dy, oa, ob, oc, proj, w):
    n, d = dy.shape

    def body(dy_ref, oa_ref, ob_ref, oc_ref, za_ref, zb_ref, zc_ref, w_ref, doa_ref, dob_ref, doc_ref, dza_ref, dzb_ref, dzc_ref):
        dm = lax.dot_general(dy_ref[...].astype(BF16), w_ref[...], NT, preferred_element_type=F32)
        for (lo, hi), o_ref, z_ref, do_ref, dz_ref in zip(BRANCH_COLS, (oa_ref, ob_ref, oc_ref), (za_ref, zb_ref, zc_ref),
                                                          (doa_ref, dob_ref, doc_ref), (dza_ref, dzb_ref, dzc_ref)):
            zv = z_ref[...]
            sg = _sigmoid(zv)
            dmv = dm[:, lo:hi]
            do_ref[...] = dmv * (zv * sg)
            dz_ref[...] = dmv * o_ref[...] * (sg * (1.0 + zv * (1.0 - sg)))

    wide, sa, sb, sc, za, zb, zc = _gate_specs(d)
    sd = jax.ShapeDtypeStruct
    outs = [sd((n, A_WIDTH), F32), sd((n, B_WIDTH), F32), sd((n, C_WIDTH), F32)]
    return pl.pallas_call(
        body, name="outproj_bwd", grid=(n // TM_ROWS,),
        in_specs=[wide, sa, sb, sc, za, zb, zc, pl.BlockSpec((d, d), lambda i: (0, 0))],
        out_specs=[sa, sb, sc, sa, sb, sc],
        out_shape=outs + outs,
        compiler_params=_cparams("parallel"),
    )(dy, oa, ob, oc, proj, proj, proj, w)


def loss_fwd_bwd(y, target):
    n, d = y.shape

    def body(y_ref, t_ref, dy_ref, part_ref):
        e = y_ref[...] - t_ref[...]
        dy_ref[...] = e * (1.0 / d)
        part_ref[...] = jnp.zeros_like(part_ref) + 0.5 * jnp.sum(e * e) * (1.0 / d)

    blk = pl.BlockSpec((TM_ROWS, d), lambda i: (i, 0))
    return pl.pallas_call(
        body, name="loss", grid=(n // TM_ROWS,),
        in_specs=[blk, blk],
        out_specs=[blk, pl.BlockSpec((1, 8, BLOCK), lambda i: (i, 0, 0))],
        out_shape=[jax.ShapeDtypeStruct((n, d), F32), jax.ShapeDtypeStruct((n // TM_ROWS, 8, BLOCK), F32)],
        compiler_params=_cparams("parallel"),
    )(y, target)


ADAM_LR, ADAM_B1, ADAM_B2, ADAM_EPS, ADAM_WD, ADAM_STEP = 0.001, 0.9, 0.999, 1e-08, 0.01, 10


def adamw(w, g, m, v):
    r, c = w.shape
    tr = r if r <= 256 else 256

    def body(w_ref, g_ref, m_ref, v_ref, d_ref, nm_ref, nv_ref):
        gv = g_ref[...]
        nm = ADAM_B1 * m_ref[...] + (1.0 - ADAM_B1) * gv
        nv = ADAM_B2 * v_ref[...] + (1.0 - ADAM_B2) * (gv * gv)
        m_hat = nm / (1.0 - ADAM_B1 ** ADAM_STEP)
        v_hat = nv / (1.0 - ADAM_B2 ** ADAM_STEP)
        d_ref[...] = -ADAM_LR * (m_hat / (jnp.sqrt(v_hat) + ADAM_EPS) + ADAM_WD * w_ref[...])
        nm_ref[...] = nm
        nv_ref[...] = nv

    blk = pl.BlockSpec((tr, c), lambda i: (i, 0))
    return pl.pallas_call(
        body, name="adamw", grid=(r // tr,),
        in_specs=[blk] * 4, out_specs=[blk] * 3,
        out_shape=[jax.ShapeDtypeStruct((r, c), F32)] * 3,
        compiler_params=_cparams("parallel"),
    )(w, g, m, v)


SB_G = N_HEADS_B


def _sb_weights(qs, k, carry, tri, diag):
    z = _mm(qs, k, NT)
    sp = jnp.log(1.0 + jnp.exp(-jnp.abs(z)))
    ls_pos = jnp.minimum(z, 0.0) - sp
    ls_neg = jnp.minimum(-z, 0.0) - sp
    earlier = (lax.broadcasted_iota(jnp.int32, z.shape, 1) < lax.broadcasted_iota(jnp.int32, z.shape, 0)) if diag else None
    log_keep = jnp.where(earlier, ls_neg, 0.0) if diag else ls_neg
    hi = log_keep.astype(BF16)
    lo = (log_keep - hi.astype(F32)).astype(BF16)
    within = lax.dot_general(jnp.concatenate([hi, lo], axis=1), tri, NN, preferred_element_type=F32)
    arg = ls_pos + within + carry
    wts = jnp.where(earlier, jnp.exp(jnp.where(earlier, arg, 0.0)), 0.0) if diag else jnp.exp(arg)
    return ls_pos, ls_neg, log_keep, wts, earlier


def _sb_tile(q, k, v, carry, tri, diag):
    _, _, log_keep, wts, _ = _sb_weights(q * (HEAD_DIM ** -0.5), k, carry, tri, diag)
    return _mm(wts, v), jnp.sum(log_keep, axis=1, keepdims=True)


def _sb_tile_grads(q, k, v, carry, do, dtot, tri, diag):
    qs = q * (HEAD_DIM ** -0.5)
    ls_pos, ls_neg, _, wts, earlier = _sb_weights(qs, k, carry, tri, diag)
    dv = _mm(wts, do, TN)
    darg = _mm(do, v, NT) * wts
    dkeep = _mm(darg, tri[:BLOCK], NT) + dtot
    if diag:
        dkeep = jnp.where(earlier, dkeep, 0.0)
    dz = darg * jnp.exp(ls_neg) - dkeep * jnp.exp(ls_pos)
    return _mm(dz, k) * (HEAD_DIM ** -0.5), _mm(dz, qs, TN), dv, jnp.sum(darg, axis=1, keepdims=True)


_sb_tiles_diag = jax.vmap(functools.partial(_sb_tile, diag=True), in_axes=(0, 0, 0, 0, None))
_sb_tiles_off = jax.vmap(functools.partial(_sb_tile, diag=False), in_axes=(0, 0, 0, 0, None))
_sb_grads_diag = jax.vmap(functools.partial(_sb_tile_grads, diag=True), in_axes=(0, 0, 0, 0, 0, 0, None))
_sb_grads_off = jax.vmap(functools.partial(_sb_tile_grads, diag=False), in_axes=(0, 0, 0, 0, 0, 0, None))


def _sb_tri():
    r = lax.broadcasted_iota(jnp.int32, (2 * BLOCK, BLOCK), 0) % BLOCK
    c = lax.broadcasted_iota(jnp.int32, (2 * BLOCK, BLOCK), 1)
    return jnp.where(r > c, 1.0, 0.0).astype(BF16)


SB_SEQ = 2


def _sb_specs(t, nq):
    cb = COL_B // B_WIDTH
    sq = SB_SEQ
    q = pl.BlockSpec((sq, BLOCK, B_WIDTH), lambda b, i: (b, i, cb))
    k = pl.BlockSpec((sq, t, B_WIDTH), lambda b, i: (b, 0, cb + 1))
    v = pl.BlockSpec((sq, t, B_WIDTH), lambda b, i: (b, 0, cb + 2))
    blk = pl.BlockSpec((sq, BLOCK, B_WIDTH), lambda b, i: (b, i, 0))
    full = pl.BlockSpec((sq, t, B_WIDTH), lambda b, i: (b, 0, 0))
    carry = pl.BlockSpec((sq, 1, nq, BLOCK, SB_G), lambda b, i: (b, i, 0, 0, 0))
    return q, k, v, blk, full, carry


def _sb_heads(ref, rows):
    return jnp.concatenate([_heads(ref[b, rows, :], SB_G) for b in range(SB_SEQ)])


def _sb_unheads(a):
    return [_unheads(a[b * SB_G:(b + 1) * SB_G]) for b in range(SB_SEQ)]


def sb_fwd(proj, nseq):
    n = proj.shape[0]
    t = n // nseq
    nq = t // BLOCK
    g, sq = SB_G, SB_SEQ
    everything = pl.ds(0, BLOCK)

    def body(q_ref, k_ref, v_ref, o_ref, carry_ref):
        i = pl.program_id(1)
        tri = _sb_tri()
        qv = _sb_heads(q_ref, everything)

        def tile(j, c, fn):
            rows = pl.ds(pl.multiple_of(j * BLOCK, BLOCK), BLOCK)
            for b in range(sq):
                carry_ref[b, 0, j] = jnp.concatenate([c[b * g + h] for h in range(g)], axis=1)
            return fn(qv, _sb_heads(k_ref, rows), _sb_heads(v_ref, rows), c, tri)

        def step(it, st):
            o_acc, c = st
            o, tot = tile(i - 1 - it, c, _sb_tiles_off)
            return o_acc + o, c + tot

        o_acc, _ = lax.fori_loop(0, i, step, tile(i, jnp.zeros((sq * g, BLOCK, 1), F32), _sb_tiles_diag))
        for b, o in enumerate(_sb_unheads(o_acc)):
            o_ref[b] = o

    q, k, v, blk, _, carry = _sb_specs(t, nq)
    proj3 = proj.reshape(nseq, t, -1)
    o, carries = pl.pallas_call(
        body, name="sb_fwd", grid=(nseq // sq, nq),
        in_specs=[q, k, v],
        out_specs=[blk, carry],
        out_shape=[jax.ShapeDtypeStruct((nseq, t, B_WIDTH), F32),
                   jax.ShapeDtypeStruct((nseq, nq, nq, BLOCK, g), F32)],
        compiler_params=_cparams("parallel", "arbitrary"),
    )(proj3, proj3, proj3)
    return o.reshape(n, B_WIDTH), carries


def sb_bwd(proj, carries, do, nseq):
    n = proj.shape[0]
    t = n // nseq
    nq = t // BLOCK
    g, sq = SB_G, SB_SEQ
    everything = pl.ds(0, BLOCK)

    def body(q_ref, k_ref, v_ref, carry_ref, do_ref, dq_ref, dk_ref, dv_ref):
        i = pl.program_id(1)

        @pl.when(i == 0)
        def _():
            dk_ref[...] = jnp.zeros_like(dk_ref)
            dv_ref[...] = jnp.zeros_like(dv_ref)

        tri = _sb_tri()
        qv = _sb_heads(q_ref, everything)
        dov = _sb_heads(do_ref, everything)

        def tile(j, st, fn):
            dq_acc, dc = st
            rows = pl.ds(pl.multiple_of(j * BLOCK, BLOCK), BLOCK)
            cj = [carry_ref[b, 0, j] for b in range(sq)]
            dq, dk, dv, dcj = fn(qv, _sb_heads(k_ref, rows), _sb_heads(v_ref, rows),
                                 jnp.stack([cj[b][:, h:h + 1] for b in range(sq) for h in range(g)]), dov, dc, tri)
            for b, (dkb, dvb) in enumerate(zip(_sb_unheads(dk), _sb_unheads(dv))):
                dk_ref[b, rows, :] += dkb
                dv_ref[b, rows, :] += dvb
            return dq_acc + dq, dc + dcj

        st = lax.fori_loop(0, i, lambda j, st: tile(j, st, _sb_grads_off),
                           (jnp.zeros((sq * g, BLOCK, HEAD_DIM), F32), jnp.zeros((sq * g, BLOCK, 1), F32)))
        dq_acc, _ = tile(i, st, _sb_grads_diag)
        for b, dq in enumerate(_sb_unheads(dq_acc)):
            dq_ref[b] = dq

    q, k, v, blk, full, carry = _sb_specs(t, nq)
    proj3, do3 = proj.reshape(nseq, t, -1), do.reshape(nseq, t, -1)
    grads = pl.pallas_call(
        body, name="sb_bwd", grid=(nseq // sq, nq),
        in_specs=[q, k, v, carry, blk],
        out_specs=[blk, full, full],
        out_shape=[jax.ShapeDtypeStruct((nseq, t, B_WIDTH), F32)] * 3,
        compiler_params=_cparams("parallel", "arbitrary"),
    )(proj3, proj3, proj3, carries, do3)
    return [a.reshape(n, B_WIDTH) for a in grads]


def _unit_lower_inverse(a):
    n = a.shape[0]
    eye = jnp.where(lax.broadcasted_iota(jnp.int32, (n, n), 0) == lax.broadcasted_iota(jnp.int32, (n, n), 1), 1.0, 0.0)
    tmat = eye.astype(F32) - a
    p = a
    for _ in range(5):
        p = _mm32(p, p)
        tmat = tmat + _mm32(tmat, p)
    return tmat


@jax.custom_vjp
def _known_inverse(a, tmat):
    return tmat


def _known_inverse_fwd(a, tmat):
    return tmat, tmat


def _known_inverse_bwd(tmat, g):
    return -_mm32(_mm32(tmat, g, TN), tmat, NT), jnp.zeros_like(tmat)


_known_inverse.defvjp(_known_inverse_fwd, _known_inverse_bwd)


def _gdn_chunk(q, k, v, al_c, al_r, br_c, alog, dtb, nw, s, tmat_in):
    c = GDN_CHUNK
    ri = lax.broadcasted_iota(jnp.int32, (c, c), 0)
    ci = lax.broadcasted_iota(jnp.int32, (c, c), 1)
    incl, strict = ri >= ci, ri > ci
    rate = -jnp.exp(alog)
    g_c = rate * _softplus(al_c + dtb)
    g_r = rate * _softplus(al_r + dtb)
    beta = _sigmoid(br_c)
    gc_c = jnp.sum(jnp.where(incl, g_r, 0.0), axis=1, keepdims=True)
    gc_r = jnp.sum(jnp.where(ri <= ci, g_c, 0.0), axis=0, keepdims=True)
    gl = jnp.sum(g_r, axis=1, keepdims=True)
    decay = jnp.where(incl, jnp.exp(jnp.where(incl, gc_c - gc_r, 0.0)), 0.0)
    qn = q * lax.rsqrt(jnp.sum(q * q, axis=-1, keepdims=True) + RMS_EPS) * (HEAD_DIM ** -0.5)
    kn = k * lax.rsqrt(jnp.sum(k * k, axis=-1, keepdims=True) + RMS_EPS)
    kb = kn * beta
    a = jnp.where(strict, _mm(kb, kn, NT) * decay, 0.0)
    tmat = _unit_lower_inverse(a) if tmat_in is None else _known_inverse(a, tmat_in)
    u = _mm(tmat, v * beta)
    w = _mm(tmat, kb * jnp.exp(gc_c))
    qk = _mm(qn, kn, NT) * decay
    v_new = u - _mm(w, s)
    o = _mm(qn * jnp.exp(gc_c), s) + _mm(qk, v_new)
    s_new = s * jnp.exp(gl) + _mm(kn * jnp.exp(gl - gc_c), v_new, TN)
    o = o * lax.rsqrt(jnp.mean(o * o, axis=-1, keepdims=True) + RMS_EPS) * nw
    return o, s_new, tmat


_gdn_chunks_fwd = jax.vmap(functools.partial(_gdn_chunk, tmat_in=None), in_axes=(0, 0, 0, 0, 0, 0, 0, 0, None, 0))
_gdn_chunks_bwd = jax.vmap(_gdn_chunk, in_axes=(0, 0, 0, 0, 0, 0, 0, 0, None, 0, 0))

GDN_TB = 256
GDN_SEQ_FWD = 2
GDN_SEQ_BWD = 2


def _gdn_block(q3, k3, v3, ba, alog, dtb, nw, s, tm=None):
    nh = N_HEADS_A
    ns = q3.shape[0]
    bat = [ba[b].T for b in range(ns)]
    br_c = jnp.stack([ba[b][:, h:h + 1] for b in range(ns) for h in range(nh)])
    al_c = jnp.stack([ba[b][:, nh + h:nh + h + 1] for b in range(ns) for h in range(nh)])
    al_r = jnp.stack([bat[b][nh + h:nh + h + 1, :] for b in range(ns) for h in range(nh)])
    heads = lambda a: jnp.concatenate([_heads(a[b], nh) for b in range(ns)])
    args = (heads(q3), heads(k3), heads(v3), al_c, al_r, br_c, jnp.concatenate([alog] * ns), jnp.concatenate([dtb] * ns), nw, s)
    o, s_new, tmat = _gdn_chunks_fwd(*args) if tm is None else _gdn_chunks_bwd(*args, tm)
    o3 = jnp.stack([_unheads(o[b * nh:(b + 1) * nh]) for b in range(ns)])
    return (o3, s_new, tmat) if tm is None else (o3, s_new)


def _gdn_specs(nt, sq, rev):
    tpos = (lambda i: nt - 1 - i) if rev else (lambda i: i)
    ncb = GDN_TB // GDN_CHUNK
    qkv = [pl.BlockSpec((sq, GDN_TB, A_WIDTH), lambda b, i, j=j: (b, tpos(i), j)) for j in range(3)]
    ba = pl.BlockSpec((sq, GDN_TB, BLOCK), lambda b, i: (b, tpos(i), COL_BA // BLOCK))
    one = pl.BlockSpec((N_HEADS_A, 1, 1), lambda b, i: (0, 0, 0))
    vec = pl.BlockSpec((1, HEAD_DIM), lambda b, i: (0, 0))
    st = pl.BlockSpec((sq, ncb, N_HEADS_A, HEAD_DIM, HEAD_DIM), lambda b, i: (b, tpos(i), 0, 0, 0))
    oa = pl.BlockSpec((sq, GDN_TB, A_WIDTH), lambda b, i: (b, tpos(i), 0))
    return qkv, ba, one, vec, st, oa, tpos


def gdn_fwd(ya, proj, alog, dtb, nw, nseq):
    n = ya.shape[0]
    t = n // nseq
    nc, nt, ncb = t // GDN_CHUNK, t // GDN_TB, GDN_TB // GDN_CHUNK
    sq = GDN_SEQ_FWD
    nh = N_HEADS_A

    def body(q_ref, k_ref, v_ref, ba_ref, alog_ref, dtb_ref, nw_ref, o_ref, st_ref, tm_ref, s_s):
        @pl.when(pl.program_id(1) == 0)
        def _():
            s_s[...] = jnp.zeros_like(s_s)

        def step(c, s):
            rows = pl.ds(pl.multiple_of(c * GDN_CHUNK, GDN_CHUNK), GDN_CHUNK)
            o, s_new, tmat = _gdn_block(q_ref[:, rows, :], k_ref[:, rows, :], v_ref[:, rows, :], ba_ref[:, rows, :],
                                        alog_ref[...], dtb_ref[...], nw_ref[...], s)
            for b in range(sq):
                st_ref[b, c] = s[b * nh:(b + 1) * nh]
                tm_ref[b, c] = tmat[b * nh:(b + 1) * nh]
            o_ref[:, rows, :] = o
            return s_new

        s_s[...] = lax.fori_loop(0, ncb, step, s_s[...])

    qkv, ba, one, vec, st, oa, _ = _gdn_specs(nt, sq, False)
    ya3, proj3 = ya.reshape(nseq, t, -1), proj.reshape(nseq, t, -1)
    per_chunk = jax.ShapeDtypeStruct((nseq, nc, nh, HEAD_DIM, HEAD_DIM), F32)
    o, states, inverses = pl.pallas_call(
        body, name="gdn_fwd", grid=(nseq // sq, nt),
        in_specs=qkv + [ba, one, one, vec],
        out_specs=[oa, st, st],
        out_shape=[jax.ShapeDtypeStruct((nseq, t, A_WIDTH), F32), per_chunk, per_chunk],
        scratch_shapes=[pltpu.VMEM((sq * nh, HEAD_DIM, HEAD_DIM), F32)],
        compiler_params=_cparams("parallel", "arbitrary"),
    )(ya3, ya3, ya3, proj3, alog, dtb, nw)
    return o.reshape(n, A_WIDTH), states, inverses


def gdn_bwd(ya, proj, alog, dtb, nw, states, inverses, do, nseq):
    n = ya.shape[0]
    t = n // nseq
    nt, ncb = t // GDN_TB, GDN_TB // GDN_CHUNK
    nh = N_HEADS_A
    sq = GDN_SEQ_BWD

    def body(q_ref, k_ref, v_ref, ba_ref, alog_ref, dtb_ref, nw_ref, st_ref, tm_ref, do_ref,
             dya_ref, dba_ref, dalog_ref, ddtb_ref, dnw_ref, ds_s):
        @pl.when(pl.program_id(1) == 0)
        def _():
            ds_s[...] = jnp.zeros_like(ds_s)
            dalog_ref[...] = jnp.zeros_like(dalog_ref)
            ddtb_ref[...] = jnp.zeros_like(ddtb_ref)
            dnw_ref[...] = jnp.zeros_like(dnw_ref)

        def step(it, carry):
            ds, dalog, ddtb, dnw = carry
            c = ncb - 1 - it
            rows = pl.ds(pl.multiple_of(c * GDN_CHUNK, GDN_CHUNK), GDN_CHUNK)
            s_in = jnp.concatenate([st_ref[b, c] for b in range(sq)])
            tm_in = jnp.concatenate([tm_ref[b, c] for b in range(sq)])
            _, vjp = jax.vjp(functools.partial(_gdn_block, tm=tm_in), q_ref[:, rows, :], k_ref[:, rows, :], v_ref[:, rows, :],
                             ba_ref[:, rows, :], alog_ref[...], dtb_ref[...], nw_ref[...], s_in)
            dq, dk, dv, dba, da, dd, dn, ds = vjp((do_ref[:, rows, :], ds))
            dya_ref[:, rows, 0:A_WIDTH] = dq
            dya_ref[:, rows, A_WIDTH:2 * A_WIDTH] = dk
            dya_ref[:, rows, 2 * A_WIDTH:3 * A_WIDTH] = dv
            dba_ref[:, rows, :] = dba
            return ds, dalog + da, ddtb + dd, dnw + dn

        z11 = jnp.zeros((nh, 1, 1), F32)
        ds, dalog, ddtb, dnw = lax.fori_loop(0, ncb, step, (ds_s[...], z11, z11, jnp.zeros((1, HEAD_DIM), F32)))
        ds_s[...] = ds
        dalog_ref[0] += dalog
        ddtb_ref[0] += ddtb
        dnw_ref[0] += dnw

    qkv, ba, one, vec, st, oa, tpos = _gdn_specs(nt, sq, True)
    per_grp = pl.BlockSpec((1, nh, 1, 1), lambda b, i: (b, 0, 0, 0))
    sd = jax.ShapeDtypeStruct
    ya3, proj3, do3 = ya.reshape(nseq, t, -1), proj.reshape(nseq, t, -1), do.reshape(nseq, t, -1)
    dya, dba, dalog, ddtb, dnw = pl.pallas_call(
        body, name="gdn_bwd", grid=(nseq // sq, nt),
        in_specs=qkv + [ba, one, one, vec, st, st, oa],
        out_specs=[pl.BlockSpec((sq, GDN_TB, 3 * A_WIDTH), lambda b, i: (b, tpos(i), 0)),
                   pl.BlockSpec((sq, GDN_TB, BLOCK), lambda b, i: (b, tpos(i), 0)),
                   per_grp, per_grp, pl.BlockSpec((1, 1, HEAD_DIM), lambda b, i: (b, 0, 0))],
        out_shape=[sd((nseq, t, 3 * A_WIDTH), F32), sd((nseq, t, BLOCK), F32), sd((nseq // sq, nh, 1, 1), F32),
                   sd((nseq // sq, nh, 1, 1), F32), sd((nseq // sq, 1, HEAD_DIM), F32)],
        scratch_shapes=[pltpu.VMEM((sq * nh, HEAD_DIM, HEAD_DIM), F32)],
        compiler_params=_cparams("parallel", "arbitrary"),
    )(ya3, ya3, ya3, proj3, alog, dtb, nw, states, inverses, do3)
    return dya.reshape(n, 3 * A_WIDTH), dba.reshape(n, BLOCK), dalog, ddtb, dnw


DIL_NB = tuple((SEQ // d) // BLOCK for _, d in DILATED_PAIRS)
DIL_D = tuple(d for _, d in DILATED_PAIRS)
DIL_STEPS = tuple(w // d for w, d in DILATED_PAIRS)
DIL_B = 8
PAIR = 2 * HEAD_DIM


def _rope_tables(t):
    half = ROPE_DIM // 2
    inv_freq = ROPE_THETA ** (-jnp.arange(half, dtype=F32) / half)
    ang = jnp.arange(t, dtype=F32)[:, None] * inv_freq[None, :]
    ones = jnp.ones((t, HEAD_DIM - ROPE_DIM), F32)
    cs = jnp.concatenate([jnp.cos(ang), jnp.cos(ang), ones], axis=1)
    sn = jnp.concatenate([jnp.sin(ang), jnp.sin(ang), 0.0 * ones], axis=1)
    i = jnp.arange(PAIR)[:, None]
    j = jnp.arange(PAIR)[None, :]
    same = (i // HEAD_DIM) == (j // HEAD_DIM)
    ih, jh = i % HEAD_DIM, j % HEAD_DIM
    pm = (jnp.where(same & (jh < half) & (ih == jh + half), -1.0, 0.0)
          + jnp.where(same & (jh >= half) & (jh < ROPE_DIM) & (ih == jh - half), 1.0, 0.0))
    mean = jnp.where(same, 1.0 / HEAD_DIM, 0.0)
    twice = lambda m: jnp.concatenate([m, m]).astype(BF16)
    return jnp.tile(cs, (1, 2)), jnp.tile(sn, (1, 2)), twice(mean), twice(pm)


def _split_dot(x, w2):
    hi = x.astype(BF16)
    lo = lax.stop_gradient(x - hi.astype(F32)).astype(BF16)
    return lax.dot_general(jnp.concatenate([hi, lo], axis=1), w2, NN, preferred_element_type=F32)


def _dil_prep(x, w, cs, sn, mean2, pm2):
    y = x * lax.rsqrt(_split_dot(x * x, mean2) + RMS_EPS) * w
    return y * cs + _split_dot(y, pm2) * sn


def _dil_tile(qn, kk, vv, bias):
    lane = lax.broadcasted_iota(jnp.int32, (1, PAIR), 1)
    outs, lses = [], []
    for h in range(2):
        s = _mm(jnp.where(lane // HEAD_DIM == h, qn, 0.0) * (HEAD_DIM ** -0.5), kk, NT) + bias
        m = lax.stop_gradient(jnp.max(s, axis=-1, keepdims=True))
        p = jnp.exp(s - m)
        denom = jnp.sum(p, axis=-1, keepdims=True)
        outs.append(_mm(p, vv) / denom)
        lses.append(m + jnp.log(denom))
    return jnp.where(lane < HEAD_DIM, outs[0], outs[1]), jnp.concatenate(lses, axis=1)


_dil_tiles = jax.vmap(_dil_tile)


def _spread(a):
    lane = lax.broadcasted_iota(jnp.int32, (a.shape[0], PAIR), 1)
    return jnp.where(lane < HEAD_DIM, a[:, 0:1], a[:, 1:2])


def _dil_mix(o1, o2, o3, l1, l2, l3):
    m = lax.stop_gradient(jnp.maximum(jnp.maximum(l1, l2), l3))
    e1, e2, e3 = jnp.exp(l1 - m), jnp.exp(l2 - m), jnp.exp(l3 - m)
    r = 1.0 / (e1 + e2 + e3)
    return _spread(e1 * r) * o1 + _spread(e2 * r) * o2 + _spread(e3 * r) * o3


def _dil_fill_biases(bias_s):
    steps, = set(DIL_STEPS)
    qi = lax.broadcasted_iota(jnp.int32, (BLOCK, 1), 0)
    kj = lax.broadcasted_iota(jnp.int32, (1, 2 * BLOCK), 1)
    rel = qi - kj + BLOCK
    inside = (rel >= 0) & (rel <= steps)
    bias_s[0] = jnp.where(inside, 0.0, NEG)
    bias_s[1] = jnp.where(inside & (kj >= BLOCK), 0.0, NEG)
    bias_s[2] = jnp.where((qi >= kj) & (qi - kj <= steps), 0.0, NEG)


def _dil_mask(it, g, bias_s):
    qrows = pl.ds(pl.multiple_of(it * BLOCK, BLOCK), BLOCK)
    if DIL_NB[g] == 1:
        return bias_s[2, :, 0:BLOCK], qrows, qrows
    which = jnp.where(it == 0, 2, jnp.where(it % DIL_NB[g] == 0, 1, 0))
    kstart = jnp.maximum(it - 1, 0) * BLOCK
    return bias_s[which], qrows, pl.ds(pl.multiple_of(kstart, BLOCK), 2 * BLOCK)


def _dil_gather(src, dst, d):
    t = src.shape[0]
    ln = t // d
    for r in range(d):
        dst[pl.ds(r * ln, ln), :] = src[pl.ds(r, ln, stride=d), :]


def _dil_scatter(src, dst, d):
    t = src.shape[0]
    ln = t // d
    for r in range(d):
        dst[pl.ds(r, ln, stride=d), :] = src[pl.ds(r * ln, ln), :]


def _dil_forward_parts(q_ref, k_ref, v_ref, qw, kw, cs_ref, sn_ref, mean2, pm2, qn_s, kn_s, dl_s, od_s, ld_s, on_s, ln_s, bias_s):
    t = qn_s.shape[0]
    _dil_fill_biases(bias_s)

    def prep(c, _):
        rows = pl.ds(pl.multiple_of(c * ROWS, ROWS), ROWS)
        qn_s[rows, :] = _dil_prep(q_ref[rows, :], qw, cs_ref[rows, :], sn_ref[rows, :], mean2, pm2)
        kn_s[rows, :] = _dil_prep(k_ref[rows, :], kw, cs_ref[rows, :], sn_ref[rows, :], mean2, pm2)
        return 0

    lax.fori_loop(0, t // ROWS, prep, 0)
    for g in (1, 2):
        _dil_gather(qn_s, dl_s.at[g - 1, 0], DIL_D[g])
        _dil_gather(kn_s, dl_s.at[g - 1, 1], DIL_D[g])
        _dil_gather(v_ref, dl_s.at[g - 1, 2], DIL_D[g])
    for g in range(3):
        qs = qn_s if g == 0 else dl_s.at[g - 1, 0]
        ks = kn_s if g == 0 else dl_s.at[g - 1, 1]
        vs = v_ref if g == 0 else dl_s.at[g - 1, 2]

        def tiles(i, _, g=g, qs=qs, ks=ks, vs=vs):
            where = [_dil_mask(i * DIL_B + b, g, bias_s) for b in range(DIL_B)]
            o, lse = _dil_tiles(jnp.stack([qs[qr, :] for _, qr, _ in where]), jnp.stack([ks[kr, :] for _, _, kr in where]),
                                jnp.stack([vs[kr, :] for _, _, kr in where]), jnp.stack([m for m, _, _ in where]))
            for b, (_, qr, _) in enumerate(where):
                od_s[g, qr, :] = o[b]
                ld_s[g, qr, :] = lse[b]
            return 0

        lax.fori_loop(0, t // BLOCK // DIL_B, tiles, 0)
    for g in (1, 2):
        _dil_scatter(od_s.at[g], on_s.at[g - 1], DIL_D[g])
        _dil_scatter(ld_s.at[g], ln_s.at[g - 1], DIL_D[g])


def _dil_scratch(t):
    return [pltpu.VMEM((t, PAIR), F32), pltpu.VMEM((t, PAIR), F32),
            pltpu.VMEM((2, 3, t, PAIR), F32),
            pltpu.VMEM((3, t, PAIR), F32), pltpu.VMEM((3, t, 2), F32),
            pltpu.VMEM((2, t, PAIR), F32), pltpu.VMEM((2, t, 2), F32),
            pltpu.VMEM((3, BLOCK, 2 * BLOCK), F32)]


def _dil_specs(t):
    cb = COL_C // BLOCK
    per = C_WIDTH // BLOCK
    qkv = [pl.BlockSpec((t, BLOCK), lambda b, p, j=j: (b, cb + j * per + p)) for j in range(3)]
    vec = pl.BlockSpec((1, PAIR), lambda b, p: (0, 0))
    tab = pl.BlockSpec((t, PAIR), lambda b, p: (0, 0))
    mat = pl.BlockSpec((2 * PAIR, PAIR), lambda b, p: (0, 0))
    pair = pl.BlockSpec((t, BLOCK), lambda b, p: (b, p))
    return qkv, vec, tab, mat, pair


def dil_fwd(proj, qw, kw, cs, sn, mean2, pm2, nseq):
    n = proj.shape[0]
    t = n // nseq

    def body(q_ref, k_ref, v_ref, qw_ref, kw_ref, cs_ref, sn_ref, mean_ref, pm_ref, o_ref,
             qn_s, kn_s, dl_s, od_s, ld_s, on_s, ln_s, bias_s):
        _dil_forward_parts(q_ref, k_ref, v_ref, qw_ref[...], kw_ref[...], cs_ref, sn_ref, mean_ref[...], pm_ref[...],
                           qn_s, kn_s, dl_s, od_s, ld_s, on_s, ln_s, bias_s)

        def mix(c, _):
            rows = pl.ds(pl.multiple_of(c * ROWS, ROWS), ROWS)
            o_ref[rows, :] = _dil_mix(od_s[0, rows, :], on_s[0, rows, :], on_s[1, rows, :],
                                      ld_s[0, rows, :], ln_s[0, rows, :], ln_s[1, rows, :])
            return 0

        lax.fori_loop(0, t // ROWS, mix, 0)

    qkv, vec, tab, mat, pair = _dil_specs(t)
    return pl.pallas_call(
        body, name="dil_fwd", grid=(nseq, C_WIDTH // BLOCK),
        in_specs=qkv + [vec, vec, tab, tab, mat, mat],
        out_specs=pair,
        out_shape=jax.ShapeDtypeStruct((n, C_WIDTH), F32),
        scratch_shapes=_dil_scratch(t),
        compiler_params=_cparams("parallel", "parallel"),
    )(proj, proj, proj, qw, kw, cs, sn, mean2, pm2)


def dil_bwd(proj, qw, kw, cs, sn, mean2, pm2, do, nseq):
    n = proj.shape[0]
    t = n // nseq

    def body(q_ref, k_ref, v_ref, qw_ref, kw_ref, cs_ref, sn_ref, mean_ref, pm_ref, do_ref,
             dq_ref, dk_ref, dv_ref, dqw_ref, dkw_ref,
             qn_s, kn_s, dl_s, od_s, ld_s, on_s, ln_s, bias_s, tq_s, tk_s, tv_s):
        qw, kw, mean2, pm2 = qw_ref[...], kw_ref[...], mean_ref[...], pm_ref[...]
        _dil_forward_parts(q_ref, k_ref, v_ref, qw, kw, cs_ref, sn_ref, mean2, pm2, qn_s, kn_s, dl_s, od_s, ld_s, on_s, ln_s, bias_s)

        def mix(c, _):
            rows = pl.ds(pl.multiple_of(c * ROWS, ROWS), ROWS)
            _, vjp = jax.vjp(_dil_mix, od_s[0, rows, :], on_s[0, rows, :], on_s[1, rows, :],
                             ld_s[0, rows, :], ln_s[0, rows, :], ln_s[1, rows, :])
            d1, d2, d3, e1, e2, e3 = vjp(do_ref[rows, :])
            od_s[0, rows, :] = d1
            on_s[0, rows, :] = d2
            on_s[1, rows, :] = d3
            ld_s[0, rows, :] = e1
            ln_s[0, rows, :] = e2
            ln_s[1, rows, :] = e3
            return 0

        lax.fori_loop(0, t // ROWS, mix, 0)
        for g in (1, 2):
            _dil_gather(on_s.at[g - 1], od_s.at[g], DIL_D[g])
            _dil_gather(ln_s.at[g - 1], ld_s.at[g], DIL_D[g])
        on_s[...] = jnp.zeros_like(on_s)
        dv_ref[...] = jnp.zeros_like(dv_ref)
        for g in range(3):
            qs = qn_s if g == 0 else dl_s.at[g - 1, 0]
            ks = kn_s if g == 0 else dl_s.at[g - 1, 1]
            vs = v_ref if g == 0 else dl_s.at[g - 1, 2]
            gq = on_s.at[0] if g == 0 else tq_s
            gk = on_s.at[1] if g == 0 else tk_s
            gv = dv_ref if g == 0 else tv_s
            if g > 0:
                tk_s[...] = jnp.zeros_like(tk_s)
                tv_s[...] = jnp.zeros_like(tv_s)

            def tiles(i, _, g=g, qs=qs, ks=ks, vs=vs, gq=gq, gk=gk, gv=gv):
                where = [_dil_mask(i * DIL_B + b, g, bias_s) for b in range(DIL_B)]
                biases = jnp.stack([m for m, _, _ in where])
                _, vjp = jax.vjp(lambda q_, k_, v_: _dil_tiles(q_, k_, v_, biases),
                                 jnp.stack([qs[qr, :] for _, qr, _ in where]), jnp.stack([ks[kr, :] for _, _, kr in where]),
                                 jnp.stack([vs[kr, :] for _, _, kr in where]))
                dq, dkk, dvv = vjp((jnp.stack([od_s[g, qr, :] for _, qr, _ in where]),
                                    jnp.stack([ld_s[g, qr, :] for _, qr, _ in where])))
                for b, (_, qr, kr) in enumerate(where):
                    gq[qr, :] = dq[b]
                    gk[kr, :] += dkk[b]
                    gv[kr, :] += dvv[b]
                return 0

            lax.fori_loop(0, t // BLOCK // DIL_B, tiles, 0)
            if g > 0:
                d = DIL_D[g]
                ln = t // d
                for r in range(d):
                    nat, dil = pl.ds(r, ln, stride=d), pl.ds(r * ln, ln)
                    on_s[0, nat, :] += tq_s[dil, :]
                    on_s[1, nat, :] += tk_s[dil, :]
                    dv_ref[nat, :] += tv_s[dil, :]

        def prep(c, acc):
            rows = pl.ds(pl.multiple_of(c * ROWS, ROWS), ROWS)
            f = lambda x, w: _dil_prep(x, w, cs_ref[rows, :], sn_ref[rows, :], mean2, pm2)
            _, vq = jax.vjp(f, q_ref[rows, :], qw)
            _, vk = jax.vjp(f, k_ref[rows, :], kw)
            dq, dqw = vq(on_s[0, rows, :])
            dk, dkw = vk(on_s[1, rows, :])
            dq_ref[rows, :] = dq
            dk_ref[rows, :] = dk
            return acc[0] + dqw, acc[1] + dkw

        dqw, dkw = lax.fori_loop(0, t // ROWS, prep, (jnp.zeros((1, PAIR), F32), jnp.zeros((1, PAIR), F32)))
        dqw_ref[0] = dqw
        dkw_ref[0] = dkw

    qkv, vec, tab, mat, pair = _dil_specs(t)
    per = C_WIDTH // BLOCK
    wout = pl.BlockSpec((1, 1, PAIR), lambda b, p: (b * per + p, 0, 0))
    return pl.pallas_call(
        body, name="dil_bwd", grid=(nseq, per),
        in_specs=qkv + [vec, vec, tab, tab, mat, mat, pair],
        out_specs=[pair, pair, pair, wout, wout],
        out_shape=[jax.ShapeDtypeStruct((n, C_WIDTH), F32)] * 3 + [jax.ShapeDtypeStruct((nseq * per, 1, PAIR), F32)] * 2,
        scratch_shapes=_dil_scratch(t) + [pltpu.VMEM((t, PAIR), F32)] * 3,
        compiler_params=_cparams("parallel", "parallel"),
    )(proj, proj, proj, qw, kw, cs, sn, mean2, pm2, do)


N_CHIPS = 4
SUM_ROWS = 432
MESH_IDS = pl.DeviceIdType.MESH
ANY = pl.BlockSpec(memory_space=pl.ANY)


def plane_exchange(src, all_to_all):
    blk_shape = src.shape[1:] if all_to_all else src.shape

    def body(src_ref, out_ref, send_sems, recv_sems, local_sem):
        x, y, c = lax.axis_index("x"), lax.axis_index("y"), lax.axis_index("c")
        me = 2 * x + y
        mine = pltpu.make_async_copy(src_ref.at[me] if all_to_all else src_ref, out_ref.at[me], local_sem)
        mine.start()
        sends = []
        for k in (1, 2, 3):
            px = 1 - x if k & 2 else x
            py = 1 - y if k & 1 else y
            peer = 2 * px + py
            cp = pltpu.make_async_remote_copy(
                src_ref=src_ref.at[peer] if all_to_all else src_ref, dst_ref=out_ref.at[me],
                send_sem=send_sems.at[k - 1], recv_sem=recv_sems.at[k - 1],
                device_id=(px, py, c), device_id_type=MESH_IDS)
            cp.start()
            sends.append((cp, peer, (px, py, c)))
        for k, (cp, peer, dev) in enumerate(sends):
            pltpu.make_async_remote_copy(
                src_ref=out_ref.at[me], dst_ref=out_ref.at[peer],
                send_sem=send_sems.at[k], recv_sem=recv_sems.at[k],
                device_id=dev, device_id_type=MESH_IDS).wait_recv()
        for cp, _, _ in sends:
            cp.wait_send()
        mine.wait()

    return pl.pallas_call(
        body, name="plane_all_to_all" if all_to_all else "plane_all_gather",
        in_specs=[ANY], out_specs=ANY,
        out_shape=jax.ShapeDtypeStruct((N_CHIPS,) + blk_shape, src.dtype),
        scratch_shapes=[pltpu.SemaphoreType.DMA((3,)), pltpu.SemaphoreType.DMA((3,)), pltpu.SemaphoreType.DMA],
    )(src)


def sibling_swap(src, other_half=False):
    shape = (src.shape[0], src.shape[1] // 2) + src.shape[2:] if other_half else src.shape

    def body(src_ref, out_ref, send_sem, recv_sem):
        x, y, c = lax.axis_index("x"), lax.axis_index("y"), lax.axis_index("c")
        part = src_ref.at[:, pl.ds((1 - c) * shape[1], shape[1])] if other_half else src_ref
        cp = pltpu.make_async_remote_copy(src_ref=part, dst_ref=out_ref, send_sem=send_sem, recv_sem=recv_sem,
                                          device_id=(x, y, 1 - c), device_id_type=MESH_IDS)
        cp.start()
        cp.wait()

    return pl.pallas_call(
        body, name="sibling_swap", in_specs=[ANY], out_specs=ANY,
        out_shape=jax.ShapeDtypeStruct(shape, src.dtype),
        scratch_shapes=[pltpu.SemaphoreType.DMA, pltpu.SemaphoreType.DMA],
    )(src)


def sum4(a):
    _, r, c = a.shape
    tr = SUM_ROWS

    def body(a_ref, o_ref):
        p = [a_ref[i].astype(F32) for i in range(N_CHIPS)]
        o_ref[...] = (p[0] + p[1]) + (p[2] + p[3])

    return pl.pallas_call(
        body, name="sum4", grid=(r // tr,),
        in_specs=[pl.BlockSpec((N_CHIPS, tr, c), lambda i: (0, i, 0))],
        out_specs=pl.BlockSpec((tr, c), lambda i: (i, 0)),
        out_shape=jax.ShapeDtypeStruct((r, c), F32),
        compiler_params=_cparams("parallel"),
    )(a)


def add_my_half(mine, got, c):
    nchip, r2, cols = mine.shape
    nt = r2 // 2 // SUM_ROWS

    def body(c_ref, a_ref, b_ref, o_ref):
        o_ref[...] = (a_ref[...] + b_ref[...]).astype(BF16)

    blk = pl.BlockSpec((1, SUM_ROWS, cols), lambda j, i, c_ref: (j, i, 0))
    return pl.pallas_call(
        body, name="add_my_half",
        grid_spec=pltpu.PrefetchScalarGridSpec(
            num_scalar_prefetch=1, grid=(nchip, nt),
            in_specs=[pl.BlockSpec((1, SUM_ROWS, cols), lambda j, i, c_ref: (j, c_ref[0] * nt + i, 0)), blk],
            out_specs=blk),
        out_shape=jax.ShapeDtypeStruct((nchip, r2 // 2, cols), BF16),
        compiler_params=_cparams("parallel", "parallel"),
    )(jnp.reshape(c, (1,)).astype(jnp.int32), mine, got)


PACK_COLS = 1152
PACK_ROWS = 2592
ROW_TILE = 16


def _pack(parts):
    blocks = []
    for p in parts:
        p2 = p.reshape(-1, p.shape[-1])
        blocks.append(jnp.pad(p2, ((0, -p2.shape[0] % ROW_TILE), (0, PACK_COLS - p2.shape[1]))))
    rows = sum(b.shape[0] for b in blocks)
    blocks.append(jnp.zeros((PACK_ROWS - rows, PACK_COLS), blocks[0].dtype))
    return jnp.concatenate(blocks)


def _unpack(buf, shapes):
    out, at = [], 0
    for s in shapes:
        rows = math.prod(s[:-1])
        out.append(buf[at:at + rows, :s[-1]].reshape(s))
        at += rows + (-rows % ROW_TILE)
    return out


def _pack_small(g):
    blk = jnp.zeros((ROW_TILE, PACK_COLS), F32)
    for i, k in enumerate(SMALL):
        blk = blk.at[2 * i:2 * i + 2, :g[k].shape[1]].set(g[k])
    return blk


def _unpack_small(blk, shapes):
    return [blk[2 * i:2 * i + 2, :s[1]] for i, s in enumerate(shapes)]


def _layer_fwd(x, p, nseq, tabs):
    proj, hdn = inproj_fwd(x, p["norm_w"][None], p["w_in"])
    ya = conv_fwd(proj, p["conv_w"], nseq)
    oa, states, inverses = gdn_fwd(ya, proj, p["a_log"].reshape(N_HEADS_A, 1, 1), p["dt_bias"].reshape(N_HEADS_A, 1, 1),
                         p["gdn_norm_w"][None], nseq)
    ob, carries = sb_fwd(proj, nseq)
    oc = dil_fwd(proj, jnp.tile(p["q_norm_w"], 2)[None], jnp.tile(p["k_norm_w"], 2)[None], *tabs, nseq)
    y, mixed = outproj_fwd(x, oa, ob, oc, proj, p["w_out"])
    return y, dict(x=x, hdn=hdn, proj=proj, ya=ya, states=states, inverses=inverses, carries=carries, oa=oa, ob=ob, oc=oc, mixed=mixed)


def _layer_bwd(dy, p, res, nseq, tabs):
    proj = res["proj"]
    g = {}
    g["w_out"] = mat_tn(res["mixed"], [dy])[0]
    doa, dob, doc, dza, dzb, dzc = outproj_bwd(dy, res["oa"], res["ob"], res["oc"], proj, p["w_out"])
    dqc, dkc, dvc, dqw, dkw = dil_bwd(proj, jnp.tile(p["q_norm_w"], 2)[None], jnp.tile(p["k_norm_w"], 2)[None], *tabs, doc,
                                      nseq)
    g["q_norm_w"], g["k_norm_w"] = dqw.reshape(-1, HEAD_DIM).sum(0), dkw.reshape(-1, HEAD_DIM).sum(0)
    dqb, dkb, dvb = sb_bwd(proj, res["carries"], dob, nseq)
    dya, dba, dalog, ddtb, dnw = gdn_bwd(res["ya"], proj, p["a_log"].reshape(N_HEADS_A, 1, 1),
                                         p["dt_bias"].reshape(N_HEADS_A, 1, 1), p["gdn_norm_w"][None], res["states"], res["inverses"], doa,
                                         nseq)
    g["a_log"], g["dt_bias"], g["gdn_norm_w"] = dalog.sum(0).reshape(-1), ddtb.sum(0).reshape(-1), dnw.sum((0, 1))
    dqkv, dcw = conv_bwd(proj, p["conv_w"], dya, nseq)
    g["conv_w"] = dcw.sum(0)
    slabs = [dqkv, dza, dqc, dkc, dvc, dzc, dqb, dkb, dvb, dzb, dba]
    hdn = res["hdn"]
    g["w_in"] = jnp.concatenate(mat_tn(hdn, slabs[:6]) + mat_tn(hdn, slabs[6:]), axis=1)
    dx, dnw_tiles = inproj_bwd(slabs, p["w_in"], res["x"], p["norm_w"][None], dy)
    g["norm_w"] = dnw_tiles.sum((0, 1))
    return dx, g


SMALL = ("norm_w", "a_log", "dt_bias", "gdn_norm_w", "q_norm_w", "k_norm_w")


def _local_step(x, target, full):
    nseq, t, d = x.shape
    tabs = _rope_tables(t)
    h = x.reshape(nseq * t, d)
    saved = []
    for l in range(DEPTH):
        p = {k: v[l] for k, v in full.items()}
        h, res = _layer_fwd(h, p, nseq, tabs)
        saved.append((p, res))
    dy, parts = loss_fwd_bwd(h, target.reshape(nseq * t, d))
    loss = parts[:, 0, 0].sum()
    grads = [None] * DEPTH
    for l in reversed(range(DEPTH)):
        p, res = saved[l]
        dy, grads[l] = _layer_bwd(dy, p, res, nseq, tabs)
    return loss, dy.reshape(nseq, t, d), {k: jnp.stack([g[k] for g in grads]) for k in grads[0]}


def _pad_cols(w):
    b0 = ORIG_A + ORIG_BA
    c0 = b0 + ORIG_B
    zeros = jnp.zeros(w.shape[:-1] + (BLOCK - ORIG_BA,), w.dtype)
    return jnp.concatenate([w[..., :ORIG_A], w[..., c0:], w[..., b0:c0], w[..., ORIG_A:b0], zeros], axis=-1)


def _unpad_cols(w):
    return jnp.concatenate([w[..., :COL_C], w[..., COL_BA:COL_BA + ORIG_BA], w[..., COL_B:COL_BA], w[..., COL_C:COL_B]],
                           axis=-1)


def kernel(x, norm_w, w_in, conv_w, a_log, dt_bias, gdn_norm_w, q_norm_w, k_norm_w, w_out, loss_target, m_norm_w, m_w_in, m_conv_w, m_a_log, m_dt_bias, m_gdn_norm_w, m_q_norm_w, m_k_norm_w, m_w_out, v_norm_w, v_w_in, v_conv_w, v_a_log, v_dt_bias, v_gdn_norm_w, v_q_norm_w, v_k_norm_w, v_w_out):
    weights = dict(norm_w=norm_w, w_in=w_in, conv_w=conv_w, a_log=a_log, dt_bias=dt_bias, gdn_norm_w=gdn_norm_w,
                   q_norm_w=q_norm_w, k_norm_w=k_norm_w, w_out=w_out)
    moms = dict(norm_w=m_norm_w, w_in=m_w_in, conv_w=m_conv_w, a_log=m_a_log, dt_bias=m_dt_bias,
                gdn_norm_w=m_gdn_norm_w, q_norm_w=m_q_norm_w, k_norm_w=m_k_norm_w, w_out=m_w_out)
    vars_ = dict(norm_w=v_norm_w, w_in=v_w_in, conv_w=v_conv_w, a_log=v_a_log, dt_bias=v_dt_bias,
                 gdn_norm_w=v_gdn_norm_w, q_norm_w=v_q_norm_w, k_norm_w=v_k_norm_w, w_out=v_w_out)
    names = list(weights)
    sharded = ("w_in", "w_out", "conv_w")
    shard_shapes = [weights[k].shape for k in sharded]

    c = lax.axis_index("c")
    half = PACK_ROWS // 2
    conv_bits = lax.bitcast_convert_type(conv_w, BF16).reshape(conv_w.shape[:2] + (2 * conv_w.shape[2],))
    shard = _pack([w_in.astype(BF16), w_out.astype(BF16), conv_bits])
    mine = plane_exchange(lax.dynamic_slice_in_dim(shard, c * half, half, axis=0), all_to_all=False)
    other = sibling_swap(mine)
    got = jnp.concatenate([jnp.where(c == 0, mine, other), jnp.where(c == 0, other, mine)], axis=1)
    per_chip = [_unpack(got[i], shard_shapes[:2] + [conv_bits.shape]) for i in range(N_CHIPS)]
    full = {k: weights[k] for k in SMALL}
    full["w_in"] = _pad_cols(jnp.concatenate([pc[0] for pc in per_chip], axis=2))
    full["w_out"] = jnp.concatenate([pc[1] for pc in per_chip], axis=1)
    full["conv_w"] = jnp.concatenate(
        [lax.bitcast_convert_type(pc[2].reshape(conv_w.shape + (2,)), F32) for pc in per_chip], axis=2)

    loss, grad_x, g = _local_step(x, loss_target, full)

    gw_in = _unpad_cols(g["w_in"])
    cols, rows = w_in.shape[2], w_out.shape[1]
    small = _pack_small(g)
    send = jnp.stack([_pack([gw_in[:, :, i * cols:(i + 1) * cols], g["w_out"][:, i * rows:(i + 1) * rows],
                             g["conv_w"][:, :, i * conv_w.shape[2]:(i + 1) * conv_w.shape[2]], small])
                      for i in range(N_CHIPS)])
    chip_sum = add_my_half(send, sibling_swap(send, other_half=True), c)
    mine = sum4(plane_exchange(chip_sum, all_to_all=True))
    other = sibling_swap(mine)
    total = jnp.concatenate([jnp.where(c == 0, mine, other), jnp.where(c == 0, other, mine)])
    reduced = _unpack(total, shard_shapes + [(ROW_TILE, PACK_COLS)])
    grads = dict(zip(sharded, reduced[:3]))
    grads.update(zip(SMALL, _unpack_small(reduced[3], [weights[k].shape for k in SMALL])))
    loss = lax.psum(loss, ("x", "y", "c"))

    def two_d(a):
        return a.reshape(-1, a.shape[-1])

    delta, new_m, new_v = {}, {}, {}
    for k in names:
        d_, m_, v_ = adamw(two_d(weights[k]), two_d(grads[k]), two_d(moms[k]), two_d(vars_[k]))
        delta[k], new_m[k], new_v[k] = (a.reshape(weights[k].shape) for a in (d_, m_, v_))
    return (loss, grad_x, *[grads[k] for k in names], *[delta[k] for k in names],
            *[new_m[k] for k in names], *[new_v[k] for k in names])
```

```python
import functools
import math

import jax
import jax.numpy as jnp
from jax import lax
from jax.experimental import pallas as pl
from jax.experimental.pallas import tpu as pltpu

F32 = jnp.float32
BF16 = jnp.bfloat16

D_MODEL = 1024
SEQ = 2048
DEPTH = 2
HEAD_DIM = 64
N_HEADS_A, N_HEADS_B, N_HEADS_C = 6, 4, 6
A_WIDTH, B_WIDTH, C_WIDTH = N_HEADS_A * HEAD_DIM, N_HEADS_B * HEAD_DIM, N_HEADS_C * HEAD_DIM
CONV_WIDTH = 4
GDN_CHUNK = 64
BLOCK = 128
ROPE_DIM = 16
ROPE_THETA = 500000.0
DILATED_PAIRS = ((128, 1), (512, 4), (2048, 16))
RMS_EPS = 1e-6
NEG = -1e30

NT = (((1,), (1,)), ((), ()))
NN = (((1,), (0,)), ((), ()))
TN = (((0,), (0,)), ((), ()))

VMEM_LIMIT = 48 * 1024 * 1024

ORIG_A = 4 * A_WIDTH
ORIG_BA = 2 * N_HEADS_A
ORIG_B = 4 * B_WIDTH
COL_AZ = 3 * A_WIDTH
COL_C = 4 * A_WIDTH
COL_B = COL_C + 4 * C_WIDTH
COL_BA = COL_B + 4 * B_WIDTH
P_COLS = COL_BA + BLOCK
TN_COLS = 384
INPROJ_COLS = P_COLS // 3
TM_ROWS = 512
ROWS = 1024


def _mm(a, b, dims=NN):
    return lax.dot_general(a.astype(BF16), b.astype(BF16), dims, preferred_element_type=F32)


def _mm32(a, b, dims=NN):
    return lax.dot_general(a, b, dims, precision=lax.Precision.HIGH, preferred_element_type=F32)


def _cparams(*sem):
    return pltpu.CompilerParams(dimension_semantics=sem, vmem_limit_bytes=VMEM_LIMIT)


def _sigmoid(x):
    return 0.5 * (jnp.tanh(0.5 * x) + 1.0)


def _softplus(x):
    return jnp.maximum(x, 0.0) + jnp.log(1.0 + jnp.exp(-jnp.abs(x)))


def _rms(x, w):
    return x * lax.rsqrt(jnp.mean(x * x, axis=-1, keepdims=True) + RMS_EPS) * w


def _heads(a, n):
    return jnp.stack([a[:, h * HEAD_DIM:(h + 1) * HEAD_DIM] for h in range(n)])


def _unheads(a):
    return jnp.concatenate([a[h] for h in range(a.shape[0])], axis=1)


def inproj_fwd(x, nw, w):
    n, d = x.shape
    p = w.shape[1]

    def body(x_ref, nw_ref, w_ref, proj_ref, hdn_ref):
        @pl.when(pl.program_id(1) == 0)
        def _():
            hdn_ref[...] = _rms(x_ref[...], nw_ref[...]).astype(BF16)

        proj_ref[...] = jnp.dot(hdn_ref[...], w_ref[...], preferred_element_type=F32)

    return pl.pallas_call(
        body, name="inproj_fwd", grid=(n // TM_ROWS, p // INPROJ_COLS),
        in_specs=[pl.BlockSpec((TM_ROWS, d), lambda i, j: (i, 0)), pl.BlockSpec((1, d), lambda i, j: (0, 0)),
                  pl.BlockSpec((d, INPROJ_COLS), lambda i, j: (0, j))],
        out_specs=[pl.BlockSpec((TM_ROWS, INPROJ_COLS), lambda i, j: (i, j)), pl.BlockSpec((TM_ROWS, d), lambda i, j: (i, 0))],
        out_shape=[jax.ShapeDtypeStruct((n, p), F32), jax.ShapeDtypeStruct((n, d), BF16)],
        compiler_params=_cparams("parallel", "arbitrary"),
    )(x, nw, w)


def mat_tn(a, slabs):
    n, ka = a.shape
    ns = len(slabs)

    def body(*refs):
        a_ref, s_refs, o_refs = refs[0], refs[1:1 + ns], refs[1 + ns:]

        @pl.when(pl.program_id(0) == 0)
        def _():
            for o_ref in o_refs:
                o_ref[...] = jnp.zeros_like(o_ref)

        av = a_ref[...]
        for s_ref, o_ref in zip(s_refs, o_refs):
            o_ref[...] += lax.dot_general(av, s_ref[...].astype(BF16), TN, preferred_element_type=F32)

    return pl.pallas_call(
        body, name="mat_tn", grid=(n // TM_ROWS,),
        in_specs=[pl.BlockSpec((TM_ROWS, ka), lambda k: (k, 0))]
                 + [pl.BlockSpec((TM_ROWS, s.shape[1]), lambda k: (k, 0)) for s in slabs],
        out_specs=[pl.BlockSpec((ka, s.shape[1]), lambda k: (0, 0)) for s in slabs],
        out_shape=[jax.ShapeDtypeStruct((ka, s.shape[1]), F32) for s in slabs],
        compiler_params=_cparams("arbitrary"),
    )(a, *slabs)


def inproj_bwd(slabs, w, x, nw, dy):
    n, d = x.shape
    p = w.shape[1]
    tm = 256
    ns = len(slabs)

    def body(*refs):
        s_refs = refs[:ns]
        w_ref, x_ref, nw_ref, dy_ref, dx_ref, dnw_ref = refs[ns:]
        dh = jnp.zeros((tm, d), F32)
        at = 0
        for s_ref in s_refs:
            wd = s_ref.shape[1]
            dh = dh + lax.dot_general(s_ref[...].astype(BF16), w_ref[:, at:at + wd], NT, preferred_element_type=F32)
            at += wd
        _, vjp = jax.vjp(_rms, x_ref[...], nw_ref[...])
        dx, dnw = vjp(dh)
        dx_ref[...] = dx + dy_ref[...]
        dnw_ref[0] = dnw

    return pl.pallas_call(
        body, name="inproj_bwd", grid=(n // tm,),
        in_specs=[pl.BlockSpec((tm, s.shape[1]), lambda i: (i, 0)) for s in slabs]
                 + [pl.BlockSpec((d, p), lambda i: (0, 0)), pl.BlockSpec((tm, d), lambda i: (i, 0)),
                    pl.BlockSpec((1, d), lambda i: (0, 0)), pl.BlockSpec((tm, d), lambda i: (i, 0))],
        out_specs=[pl.BlockSpec((tm, d), lambda i: (i, 0)), pl.BlockSpec((1, 1, d), lambda i: (i, 0, 0))],
        out_shape=[jax.ShapeDtypeStruct((n, d), F32), jax.ShapeDtypeStruct((n // tm, 1, d), F32)],
        compiler_params=_cparams("parallel"),
    )(*slabs, w, x, nw, dy)


CONV_PAD = 8
CONV_ROWS = 256


def _conv_pre(pad_s, cw, c):
    xs = [pad_s[pl.ds(c * CONV_ROWS + CONV_PAD - (CONV_WIDTH - 1) + k, CONV_ROWS), :] for k in range(CONV_WIDTH)]
    pre = xs[0] * cw[0:1, :]
    for k in range(1, CONV_WIDTH):
        pre = pre + xs[k] * cw[k:k + 1, :]
    return pre, xs


def conv_fwd(proj, cw, nseq):
    n = proj.shape[0]
    t = n // nseq
    ch = cw.shape[1]

    def body(x_ref, cw_ref, y_ref, pad_s):
        pad_s[pl.ds(0, CONV_PAD), :] = jnp.zeros((CONV_PAD, TN_COLS), F32)
        pad_s[pl.ds(CONV_PAD, t), :] = x_ref[...]
        cwv = cw_ref[...]
        for c in range(t // CONV_ROWS):
            pre, _ = _conv_pre(pad_s, cwv, c)
            y_ref[pl.ds(c * CONV_ROWS, CONV_ROWS), :] = pre * _sigmoid(pre)

    return pl.pallas_call(
        body, name="conv_fwd", grid=(nseq, ch // TN_COLS),
        in_specs=[pl.BlockSpec((t, TN_COLS), lambda b, j: (b, j)), pl.BlockSpec((CONV_WIDTH, TN_COLS), lambda b, j: (0, j))],
        out_specs=pl.BlockSpec((t, TN_COLS), lambda b, j: (b, j)),
        out_shape=jax.ShapeDtypeStruct((n, ch), F32),
        scratch_shapes=[pltpu.VMEM((t + CONV_PAD, TN_COLS), F32)],
        compiler_params=_cparams("parallel", "parallel"),
    )(proj, cw)


def conv_bwd(proj, cw, dy, nseq):
    n = proj.shape[0]
    t = n // nseq
    ch = cw.shape[1]

    def body(x_ref, cw_ref, dy_ref, dx_ref, dcw_ref, pad_s, dpad_s):
        pad_s[pl.ds(0, CONV_PAD), :] = jnp.zeros((CONV_PAD, TN_COLS), F32)
        pad_s[pl.ds(CONV_PAD, t), :] = x_ref[...]
        dpad_s[pl.ds(t, CONV_PAD), :] = jnp.zeros((CONV_PAD, TN_COLS), F32)
        cwv = cw_ref[...]
        acc = [jnp.zeros((1, TN_COLS), F32)] * CONV_WIDTH
        for c in range(t // CONV_ROWS):
            pre, xs = _conv_pre(pad_s, cwv, c)
            sg = _sigmoid(pre)
            dpre = dy_ref[pl.ds(c * CONV_ROWS, CONV_ROWS), :] * (sg * (1.0 + pre * (1.0 - sg)))
            dpad_s[pl.ds(c * CONV_ROWS, CONV_ROWS), :] = dpre
            acc = [acc[k] + jnp.sum(dpre * xs[k], axis=0, keepdims=True) for k in range(CONV_WIDTH)]
        for k in range(CONV_WIDTH):
            dcw_ref[0, pl.ds(k, 1), :] = acc[k]
        for c in range(t // CONV_ROWS):
            dx = dpad_s[pl.ds(c * CONV_ROWS + CONV_WIDTH - 1, CONV_ROWS), :] * cwv[0:1, :]
            for k in range(1, CONV_WIDTH):
                dx = dx + dpad_s[pl.ds(c * CONV_ROWS + CONV_WIDTH - 1 - k, CONV_ROWS), :] * cwv[k:k + 1, :]
            dx_ref[pl.ds(c * CONV_ROWS, CONV_ROWS), :] = dx

    blk = pl.BlockSpec((t, TN_COLS), lambda b, j: (b, j))
    return pl.pallas_call(
        body, name="conv_bwd", grid=(nseq, ch // TN_COLS),
        in_specs=[blk, pl.BlockSpec((CONV_WIDTH, TN_COLS), lambda b, j: (0, j)), blk],
        out_specs=[blk, pl.BlockSpec((1, CONV_WIDTH, TN_COLS), lambda b, j: (b, 0, j))],
        out_shape=[jax.ShapeDtypeStruct((n, ch), F32), jax.ShapeDtypeStruct((nseq, CONV_WIDTH, ch), F32)],
        scratch_shapes=[pltpu.VMEM((t + CONV_PAD, TN_COLS), F32)] * 2,
        compiler_params=_cparams("parallel", "parallel"),
    )(proj, cw, dy)


def _gate_specs(d):
    wide = pl.BlockSpec((TM_ROWS, d), lambda i: (i, 0))
    oa = pl.BlockSpec((TM_ROWS, A_WIDTH), lambda i: (i, 0))
    ob = pl.BlockSpec((TM_ROWS, B_WIDTH), lambda i: (i, 0))
    oc = pl.BlockSpec((TM_ROWS, C_WIDTH), lambda i: (i, 0))
    za = pl.BlockSpec((TM_ROWS, A_WIDTH), lambda i: (i, COL_AZ // A_WIDTH))
    zb = pl.BlockSpec((TM_ROWS, B_WIDTH), lambda i: (i, (COL_B + 3 * B_WIDTH) // B_WIDTH))
    zc = pl.BlockSpec((TM_ROWS, C_WIDTH), lambda i: (i, (COL_C + 3 * C_WIDTH) // C_WIDTH))
    return wide, oa, ob, oc, za, zb, zc


BRANCH_COLS = ((0, A_WIDTH), (A_WIDTH, A_WIDTH + B_WIDTH), (A_WIDTH + B_WIDTH, D_MODEL))


def outproj_fwd(x, oa, ob, oc, proj, w):
    n, d = x.shape

    def body(x_ref, oa_ref, ob_ref, oc_ref, za_ref, zb_ref, zc_ref, w_ref, y_ref, m_ref):
        for (lo, hi), o_ref, z_ref in zip(BRANCH_COLS, (oa_ref, ob_ref, oc_ref), (za_ref, zb_ref, zc_ref)):
            zv = z_ref[...]
            m_ref[:, lo:hi] = (o_ref[...] * (zv * _sigmoid(zv))).astype(BF16)
        y_ref[...] = x_ref[...] + jnp.dot(m_ref[...], w_ref[...], preferred_element_type=F32)

    wide, sa, sb, sc, za, zb, zc = _gate_specs(d)
    return pl.pallas_call(
        body, name="outproj_fwd", grid=(n // TM_ROWS,),
        in_specs=[wide, sa, sb, sc, za, zb, zc, pl.BlockSpec((d, d), lambda i: (0, 0))],
        out_specs=[wide, wide],
        out_shape=[jax.ShapeDtypeStruct((n, d), F32), jax.ShapeDtypeStruct((n, d), BF16)],
        compiler_params=_cparams("parallel"),
    )(x, oa, ob, oc, proj, proj, proj, w)


def outproj_bwd(dy, oa, ob, oc, proj, w):
    n, d = dy.shape

    def body(dy_ref, oa_ref, ob_ref, oc_ref, za_ref, zb_ref, zc_ref, w_ref, doa_ref, dob_ref, doc_ref, dza_ref, dzb_ref, dzc_ref):
        dm = lax.dot_general(dy_ref[...].astype(BF16), w_ref[...], NT, preferred_element_type=F32)
        for (lo, hi), o_ref, z_ref, do_ref, dz_ref in zip(BRANCH_COLS, (oa_ref, ob_ref, oc_ref), (za_ref, zb_ref, zc_ref),
                                                          (doa_ref, dob_ref, doc_ref), (dza_ref, dzb_ref, dzc_ref)):
            zv = z_ref[...]
            sg = _sigmoid(zv)
            dmv = dm[:, lo:hi]
            do_ref[...] = dmv * (zv * sg)
            dz_ref[...] = dmv * o_ref[...] * (sg * (1.0 + zv * (1.0 - sg)))

    wide, sa, sb, sc, za, zb, zc = _gate_specs(d)
    sd = jax.ShapeDtypeStruct
    outs = [sd((n, A_WIDTH), F32), sd((n, B_WIDTH), F32), sd((n, C_WIDTH), F32)]
    return pl.pallas_call(
        body, name="outproj_bwd", grid=(n // TM_ROWS,),
        in_specs=[wide, sa, sb, sc, za, zb, zc, pl.BlockSpec((d, d), lambda i: (0, 0))],
        out_specs=[sa, sb, sc, sa, sb, sc],
        out_shape=outs + outs,
        compiler_params=_cparams("parallel"),
    )(dy, oa, ob, oc, proj, proj, proj, w)


def loss_fwd_bwd(y, target):
    n, d = y.shape

    def body(y_ref, t_ref, dy_ref, part_ref):
        e = y_ref[...] - t_ref[...]
        dy_ref[...] = e * (1.0 / d)
        part_ref[...] = jnp.zeros_like(part_ref) + 0.5 * jnp.sum(e * e) * (1.0 / d)

    blk = pl.BlockSpec((TM_ROWS, d), lambda i: (i, 0))
    return pl.pallas_call(
        body, name="loss", grid=(n // TM_ROWS,),
        in_specs=[blk, blk],
        out_specs=[blk, pl.BlockSpec((1, 8, BLOCK), lambda i: (i, 0, 0))],
        out_shape=[jax.ShapeDtypeStruct((n, d), F32), jax.ShapeDtypeStruct((n // TM_ROWS, 8, BLOCK), F32)],
        compiler_params=_cparams("parallel"),
    )(y, target)


ADAM_LR, ADAM_B1, ADAM_B2, ADAM_EPS, ADAM_WD, ADAM_STEP = 0.001, 0.9, 0.999, 1e-08, 0.01, 10


def adamw(w, g, m, v):
    r, c = w.shape
    tr = r if r <= 256 else 256

    def body(w_ref, g_ref, m_ref, v_ref, d_ref, nm_ref, nv_ref):
        gv = g_ref[...]
        nm = ADAM_B1 * m_ref[...] + (1.0 - ADAM_B1) * gv
        nv = ADAM_B2 * v_ref[...] + (1.0 - ADAM_B2) * (gv * gv)
        m_hat = nm / (1.0 - ADAM_B1 ** ADAM_STEP)
        v_hat = nv / (1.0 - ADAM_B2 ** ADAM_STEP)
        d_ref[...] = -ADAM_LR * (m_hat / (jnp.sqrt(v_hat) + ADAM_EPS) + ADAM_WD * w_ref[...])
        nm_ref[...] = nm
        nv_ref[...] = nv

    blk = pl.BlockSpec((tr, c), lambda i: (i, 0))
    return pl.pallas_call(
        body, name="adamw", grid=(r // tr,),
        in_specs=[blk] * 4, out_specs=[blk] * 3,
        out_shape=[jax.ShapeDtypeStruct((r, c), F32)] * 3,
        compiler_params=_cparams("parallel"),
    )(w, g, m, v)


SB_G = N_HEADS_B


def _sb_weights(qs, k, carry, tri, diag):
    z = _mm(qs, k, NT)
    sp = jnp.log(1.0 + jnp.exp(-jnp.abs(z)))
    ls_pos = jnp.minimum(z, 0.0) - sp
    ls_neg = jnp.minimum(-z, 0.0) - sp
    earlier = (lax.broadcasted_iota(jnp.int32, z.shape, 1) < lax.broadcasted_iota(jnp.int32, z.shape, 0)) if diag else None
    log_keep = jnp.where(earlier, ls_neg, 0.0) if diag else ls_neg
    hi = log_keep.astype(BF16)
    lo = (log_keep - hi.astype(F32)).astype(BF16)
    within = lax.dot_general(jnp.concatenate([hi, lo], axis=1), tri, NN, preferred_element_type=F32)
    arg = ls_pos + within + carry
    wts = jnp.where(earlier, jnp.exp(jnp.where(earlier, arg, 0.0)), 0.0) if diag else jnp.exp(arg)
    return ls_pos, ls_neg, log_keep, wts, earlier


def _sb_tile(q, k, v, carry, tri, diag):
    _, _, log_keep, wts, _ = _sb_weights(q * (HEAD_DIM ** -0.5), k, carry, tri, diag)
    return _mm(wts, v), jnp.sum(log_keep, axis=1, keepdims=True)


def _sb_tile_grads(q, k, v, carry, do, dtot, tri, diag):
    qs = q * (HEAD_DIM ** -0.5)
    ls_pos, ls_neg, _, wts, earlier = _sb_weights(qs, k, carry, tri, diag)
    dv = _mm(wts, do, TN)
    darg = _mm(do, v, NT) * wts
    dkeep = _mm(darg, tri[:BLOCK], NT) + dtot
    if diag:
        dkeep = jnp.where(earlier, dkeep, 0.0)
    dz = darg * jnp.exp(ls_neg) - dkeep * jnp.exp(ls_pos)
    return _mm(dz, k) * (HEAD_DIM ** -0.5), _mm(dz, qs, TN), dv, jnp.sum(darg, axis=1, keepdims=True)


_sb_tiles_diag = jax.vmap(functools.partial(_sb_tile, diag=True), in_axes=(0, 0, 0, 0, None))
_sb_tiles_off = jax.vmap(functools.partial(_sb_tile, diag=False), in_axes=(0, 0, 0, 0, None))
_sb_grads_diag = jax.vmap(functools.partial(_sb_tile_grads, diag=True), in_axes=(0, 0, 0, 0, 0, 0, None))
_sb_grads_off = jax.vmap(functools.partial(_sb_tile_grads, diag=False), in_axes=(0, 0, 0, 0, 0, 0, None))


def _sb_tri():
    r = lax.broadcasted_iota(jnp.int32, (2 * BLOCK, BLOCK), 0) % BLOCK
    c = lax.broadcasted_iota(jnp.int32, (2 * BLOCK, BLOCK), 1)
    return jnp.where(r > c, 1.0, 0.0).astype(BF16)


SB_SEQ = 2


def _sb_specs(t, nq):
    cb = COL_B // B_WIDTH
    sq = SB_SEQ
    q = pl.BlockSpec((sq, BLOCK, B_WIDTH), lambda b, i: (b, i, cb))
    k = pl.BlockSpec((sq, t, B_WIDTH), lambda b, i: (b, 0, cb + 1))
    v = pl.BlockSpec((sq, t, B_WIDTH), lambda b, i: (b, 0, cb + 2))
    blk = pl.BlockSpec((sq, BLOCK, B_WIDTH), lambda b, i: (b, i, 0))
    full = pl.BlockSpec((sq, t, B_WIDTH), lambda b, i: (b, 0, 0))
    carry = pl.BlockSpec((sq, 1, nq, BLOCK, SB_G), lambda b, i: (b, i, 0, 0, 0))
    return q, k, v, blk, full, carry


def _sb_heads(ref, rows):
    return jnp.concatenate([_heads(ref[b, rows, :], SB_G) for b in range(SB_SEQ)])


def _sb_unheads(a):
    return [_unheads(a[b * SB_G:(b + 1) * SB_G]) for b in range(SB_SEQ)]


def sb_fwd(proj, nseq):
    n = proj.shape[0]
    t = n // nseq
    nq = t // BLOCK
    g, sq = SB_G, SB_SEQ
    everything = pl.ds(0, BLOCK)

    def body(q_ref, k_ref, v_ref, o_ref, carry_ref):
        i = pl.program_id(1)
        tri = _sb_tri()
        qv = _sb_heads(q_ref, everything)

        def tile(j, c, fn):
            rows = pl.ds(pl.multiple_of(j * BLOCK, BLOCK), BLOCK)
            for b in range(sq):
                carry_ref[b, 0, j] = jnp.concatenate([c[b * g + h] for h in range(g)], axis=1)
            return fn(qv, _sb_heads(k_ref, rows), _sb_heads(v_ref, rows), c, tri)

        def step(it, st):
            o_acc, c = st
            o, tot = tile(i - 1 - it, c, _sb_tiles_off)
            return o_acc + o, c + tot

        o_acc, _ = lax.fori_loop(0, i, step, tile(i, jnp.zeros((sq * g, BLOCK, 1), F32), _sb_tiles_diag))
        for b, o in enumerate(_sb_unheads(o_acc)):
            o_ref[b] = o

    q, k, v, blk, _, carry = _sb_specs(t, nq)
    proj3 = proj.reshape(nseq, t, -1)
    o, carries = pl.pallas_call(
        body, name="sb_fwd", grid=(nseq // sq, nq),
        in_specs=[q, k, v],
        out_specs=[blk, carry],
        out_shape=[jax.ShapeDtypeStruct((nseq, t, B_WIDTH), F32),
                   jax.ShapeDtypeStruct((nseq, nq, nq, BLOCK, g), F32)],
        compiler_params=_cparams("parallel", "arbitrary"),
    )(proj3, proj3, proj3)
    return o.reshape(n, B_WIDTH), carries


def sb_bwd(proj, carries, do, nseq):
    n = proj.shape[0]
    t = n // nseq
    nq = t // BLOCK
    g, sq = SB_G, SB_SEQ
    everything = pl.ds(0, BLOCK)

    def body(q_ref, k_ref, v_ref, carry_ref, do_ref, dq_ref, dk_ref, dv_ref):
        i = pl.program_id(1)

        @pl.when(i == 0)
        def _():
            dk_ref[...] = jnp.zeros_like(dk_ref)
            dv_ref[...] = jnp.zeros_like(dv_ref)

        tri = _sb_tri()
        qv = _sb_heads(q_ref, everything)
        dov = _sb_heads(do_ref, everything)

        def tile(j, st, fn):
            dq_acc, dc = st
            rows = pl.ds(pl.multiple_of(j * BLOCK, BLOCK), BLOCK)
            cj = [carry_ref[b, 0, j] for b in range(sq)]
            dq, dk, dv, dcj = fn(qv, _sb_heads(k_ref, rows), _sb_heads(v_ref, rows),
                                 jnp.stack([cj[b][:, h:h + 1] for b in range(sq) for h in range(g)]), dov, dc, tri)
            for b, (dkb, dvb) in enumerate(zip(_sb_unheads(dk), _sb_unheads(dv))):
                dk_ref[b, rows, :] += dkb
                dv_ref[b, rows, :] += dvb
            return dq_acc + dq, dc + dcj

        st = lax.fori_loop(0, i, lambda j, st: tile(j, st, _sb_grads_off),
                           (jnp.zeros((sq * g, BLOCK, HEAD_DIM), F32), jnp.zeros((sq * g, BLOCK, 1), F32)))
        dq_acc, _ = tile(i, st, _sb_grads_diag)
        for b, dq in enumerate(_sb_unheads(dq_acc)):
            dq_ref[b] = dq

    q, k, v, blk, full, carry = _sb_specs(t, nq)
    proj3, do3 = proj.reshape(nseq, t, -1), do.reshape(nseq, t, -1)
    grads = pl.pallas_call(
        body, name="sb_bwd", grid=(nseq // sq, nq),
        in_specs=[q, k, v, carry, blk],
        out_specs=[blk, full, full],
        out_shape=[jax.ShapeDtypeStruct((nseq, t, B_WIDTH), F32)] * 3,
        compiler_params=_cparams("parallel", "arbitrary"),
    )(proj3, proj3, proj3, carries, do3)
    return [a.reshape(n, B_WIDTH) for a in grads]


def _unit_lower_inverse(a):
    n = a.shape[0]
    eye = jnp.where(lax.broadcasted_iota(jnp.int32, (n, n), 0) == lax.broadcasted_iota(jnp.int32, (n, n), 1), 1.0, 0.0)
    tmat = eye.astype(F32) - a
    p = a
    for _ in range(5):
        p = _mm32(p, p)
        tmat = tmat + _mm32(tmat, p)
    return tmat


@jax.custom_vjp
def _known_inverse(a, tmat):
    return tmat


def _known_inverse_fwd(a, tmat):
    return tmat, tmat


def _known_inverse_bwd(tmat, g):
    return -_mm32(_mm32(tmat, g, TN), tmat, NT), jnp.zeros_like(tmat)


_known_inverse.defvjp(_known_inverse_fwd, _known_inverse_bwd)


def _gdn_chunk(q, k, v, al_c, al_r, br_c, alog, dtb, nw, s, tmat_in):
    c = GDN_CHUNK
    ri = lax.broadcasted_iota(jnp.int32, (c, c), 0)
    ci = lax.broadcasted_iota(jnp.int32, (c, c), 1)
    incl, strict = ri >= ci, ri > ci
    rate = -jnp.exp(alog)
    g_c = rate * _softplus(al_c + dtb)
    g_r = rate * _softplus(al_r + dtb)
    beta = _sigmoid(br_c)
    gc_c = jnp.sum(jnp.where(incl, g_r, 0.0), axis=1, keepdims=True)
    gc_r = jnp.sum(jnp.where(ri <= ci, g_c, 0.0), axis=0, keepdims=True)
    gl = jnp.sum(g_r, axis=1, keepdims=True)
    decay = jnp.where(incl, jnp.exp(jnp.where(incl, gc_c - gc_r, 0.0)), 0.0)
    qn = q * lax.rsqrt(jnp.sum(q * q, axis=-1, keepdims=True) + RMS_EPS) * (HEAD_DIM ** -0.5)
    kn = k * lax.rsqrt(jnp.sum(k * k, axis=-1, keepdims=True) + RMS_EPS)
    kb = kn * beta
    a = jnp.where(strict, _mm(kb, kn, NT) * decay, 0.0)
    tmat = _unit_lower_inverse(a) if tmat_in is None else _known_inverse(a, tmat_in)
    u = _mm(tmat, v * beta)
    w = _mm(tmat, kb * jnp.exp(gc_c))
    qk = _mm(qn, kn, NT) * decay
    v_new = u - _mm(w, s)
    o = _mm(qn * jnp.exp(gc_c), s) + _mm(qk, v_new)
    s_new = s * jnp.exp(gl) + _mm(kn * jnp.exp(gl - gc_c), v_new, TN)
    o = o * lax.rsqrt(jnp.mean(o * o, axis=-1, keepdims=True) + RMS_EPS) * nw
    return o, s_new, tmat


_gdn_chunks_fwd = jax.vmap(functools.partial(_gdn_chunk, tmat_in=None), in_axes=(0, 0, 0, 0, 0, 0, 0, 0, None, 0))
_gdn_chunks_bwd = jax.vmap(_gdn_chunk, in_axes=(0, 0, 0, 0, 0, 0, 0, 0, None, 0, 0))

GDN_TB = 256
GDN_SEQ_FWD = 2
GDN_SEQ_BWD = 2


def _gdn_block(q3, k3, v3, ba, alog, dtb, nw, s, tm=None):
    nh = N_HEADS_A
    ns = q3.shape[0]
    bat = [ba[b].T for b in range(ns)]
    br_c = jnp.stack([ba[b][:, h:h + 1] for b in range(ns) for h in range(nh)])
    al_c = jnp.stack([ba[b][:, nh + h:nh + h + 1] for b in range(ns) for h in range(nh)])
    al_r = jnp.stack([bat[b][nh + h:nh + h + 1, :] for b in range(ns) for h in range(nh)])
    heads = lambda a: jnp.concatenate([_heads(a[b], nh) for b in range(ns)])
    args = (heads(q3), heads(k3), heads(v3), al_c, al_r, br_c, jnp.concatenate([alog] * ns), jnp.concatenate([dtb] * ns), nw, s)
    o, s_new, tmat = _gdn_chunks_fwd(*args) if tm is None else _gdn_chunks_bwd(*args, tm)
    o3 = jnp.stack([_unheads(o[b * nh:(b + 1) * nh]) for b in range(ns)])
    return (o3, s_new, tmat) if tm is None else (o3, s_new)


def _gdn_specs(nt, sq, rev):
    tpos = (lambda i: nt - 1 - i) if rev else (lambda i: i)
    ncb = GDN_TB // GDN_CHUNK
    qkv = [pl.BlockSpec((sq, GDN_TB, A_WIDTH), lambda b, i, j=j: (b, tpos(i), j)) for j in range(3)]
    ba = pl.BlockSpec((sq, GDN_TB, BLOCK), lambda b, i: (b, tpos(i), COL_BA // BLOCK))
    one = pl.BlockSpec((N_HEADS_A, 1, 1), lambda b, i: (0, 0, 0))
    vec = pl.BlockSpec((1, HEAD_DIM), lambda b, i: (0, 0))
    st = pl.BlockSpec((sq, ncb, N_HEADS_A, HEAD_DIM, HEAD_DIM), lambda b, i: (b, tpos(i), 0, 0, 0))
    oa = pl.BlockSpec((sq, GDN_TB, A_WIDTH), lambda b, i: (b, tpos(i), 0))
    return qkv, ba, one, vec, st, oa, tpos


def gdn_fwd(ya, proj, alog, dtb, nw, nseq):
    n = ya.shape[0]
    t = n // nseq
    nc, nt, ncb = t // GDN_CHUNK, t // GDN_TB, GDN_TB // GDN_CHUNK
    sq = GDN_SEQ_FWD
    nh = N_HEADS_A

    def body(q_ref, k_ref, v_ref, ba_ref, alog_ref, dtb_ref, nw_ref, o_ref, st_ref, tm_ref, s_s):
        @pl.when(pl.program_id(1) == 0)
        def _():
            s_s[...] = jnp.zeros_like(s_s)

        def step(c, s):
            rows = pl.ds(pl.multiple_of(c * GDN_CHUNK, GDN_CHUNK), GDN_CHUNK)
            o, s_new, tmat = _gdn_block(q_ref[:, rows, :], k_ref[:, rows, :], v_ref[:, rows, :], ba_ref[:, rows, :],
                                        alog_ref[...], dtb_ref[...], nw_ref[...], s)
            for b in range(sq):
                st_ref[b, c] = s[b * nh:(b + 1) * nh]
                tm_ref[b, c] = tmat[b * nh:(b + 1) * nh]
            o_ref[:, rows, :] = o
            return s_new

        s_s[...] = lax.fori_loop(0, ncb, step, s_s[...])

    qkv, ba, one, vec, st, oa, _ = _gdn_specs(nt, sq, False)
    ya3, proj3 = ya.reshape(nseq, t, -1), proj.reshape(nseq, t, -1)
    per_chunk = jax.ShapeDtypeStruct((nseq, nc, nh, HEAD_DIM, HEAD_DIM), F32)
    o, states, inverses = pl.pallas_call(
        body, name="gdn_fwd", grid=(nseq // sq, nt),
        in_specs=qkv + [ba, one, one, vec],
        out_specs=[oa, st, st],
        out_shape=[jax.ShapeDtypeStruct((nseq, t, A_WIDTH), F32), per_chunk, per_chunk],
        scratch_shapes=[pltpu.VMEM((sq * nh, HEAD_DIM, HEAD_DIM), F32)],
        compiler_params=_cparams("parallel", "arbitrary"),
    )(ya3, ya3, ya3, proj3, alog, dtb, nw)
    return o.reshape(n, A_WIDTH), states, inverses


def gdn_bwd(ya, proj, alog, dtb, nw, states, inverses, do, nseq):
    n = ya.shape[0]
    t = n // nseq
    nt, ncb = t // GDN_TB, GDN_TB // GDN_CHUNK
    nh = N_HEADS_A
    sq = GDN_SEQ_BWD

    def body(q_ref, k_ref, v_ref, ba_ref, alog_ref, dtb_ref, nw_ref, st_ref, tm_ref, do_ref,
             dya_ref, dba_ref, dalog_ref, ddtb_ref, dnw_ref, ds_s):
        @pl.when(pl.program_id(1) == 0)
        def _():
            ds_s[...] = jnp.zeros_like(ds_s)
            dalog_ref[...] = jnp.zeros_like(dalog_ref)
            ddtb_ref[...] = jnp.zeros_like(ddtb_ref)
            dnw_ref[...] = jnp.zeros_like(dnw_ref)

        def step(it, carry):
            ds, dalog, ddtb, dnw = carry
            c = ncb - 1 - it
            rows = pl.ds(pl.multiple_of(c * GDN_CHUNK, GDN_CHUNK), GDN_CHUNK)
            s_in = jnp.concatenate([st_ref[b, c] for b in range(sq)])
            tm_in = jnp.concatenate([tm_ref[b, c] for b in range(sq)])
            _, vjp = jax.vjp(functools.partial(_gdn_block, tm=tm_in), q_ref[:, rows, :], k_ref[:, rows, :], v_ref[:, rows, :],
                             ba_ref[:, rows, :], alog_ref[...], dtb_ref[...], nw_ref[...], s_in)
            dq, dk, dv, dba, da, dd, dn, ds = vjp((do_ref[:, rows, :], ds))
            dya_ref[:, rows, 0:A_WIDTH] = dq
            dya_ref[:, rows, A_WIDTH:2 * A_WIDTH] = dk
            dya_ref[:, rows, 2 * A_WIDTH:3 * A_WIDTH] = dv
            dba_ref[:, rows, :] = dba
            return ds, dalog + da, ddtb + dd, dnw + dn

        z11 = jnp.zeros((nh, 1, 1), F32)
        ds, dalog, ddtb, dnw = lax.fori_loop(0, ncb, step, (ds_s[...], z11, z11, jnp.zeros((1, HEAD_DIM), F32)))
        ds_s[...] = ds
        dalog_ref[0] += dalog
        ddtb_ref[0] += ddtb
        dnw_ref[0] += dnw

    qkv, ba, one, vec, st, oa, tpos = _gdn_specs(nt, sq, True)
    per_grp = pl.BlockSpec((1, nh, 1, 1), lambda b, i: (b, 0, 0, 0))
    sd = jax.ShapeDtypeStruct
    ya3, proj3, do3 = ya.reshape(nseq, t, -1), proj.reshape(nseq, t, -1), do.reshape(nseq, t, -1)
    dya, dba, dalog, ddtb, dnw = pl.pallas_call(
        body, name="gdn_bwd", grid=(nseq // sq, nt),
        in_specs=qkv + [ba, one, one, vec, st, st, oa],
        out_specs=[pl.BlockSpec((sq, GDN_TB, 3 * A_WIDTH), lambda b, i: (b, tpos(i), 0)),
                   pl.BlockSpec((sq, GDN_TB, BLOCK), lambda b, i: (b, tpos(i), 0)),
                   per_grp, per_grp, pl.BlockSpec((1, 1, HEAD_DIM), lambda b, i: (b, 0, 0))],
        out_shape=[sd((nseq, t, 3 * A_WIDTH), F32), sd((nseq, t, BLOCK), F32), sd((nseq // sq, nh, 1, 1), F32),
                   sd((nseq // sq, nh, 1, 1), F32), sd((nseq // sq, 1, HEAD_DIM), F32)],
        scratch_shapes=[pltpu.VMEM((sq * nh, HEAD_DIM, HEAD_DIM), F32)],
        compiler_params=_cparams("parallel", "arbitrary"),
    )(ya3, ya3, ya3, proj3, alog, dtb, nw, states, inverses, do3)
    return dya.reshape(n, 3 * A_WIDTH), dba.reshape(n, BLOCK), dalog, ddtb, dnw


DIL_NB = tuple((SEQ // d) // BLOCK for _, d in DILATED_PAIRS)
DIL_D = tuple(d for _, d in DILATED_PAIRS)
DIL_STEPS = tuple(w // d for w, d in DILATED_PAIRS)
DIL_B = 8
PAIR = 2 * HEAD_DIM


def _rope_tables(t):
    half = ROPE_DIM // 2
    inv_freq = ROPE_THETA ** (-jnp.arange(half, dtype=F32) / half)
    ang = jnp.arange(t, dtype=F32)[:, None] * inv_freq[None, :]
    ones = jnp.ones((t, HEAD_DIM - ROPE_DIM), F32)
    cs = jnp.concatenate([jnp.cos(ang), jnp.cos(ang), ones], axis=1)
    sn = jnp.concatenate([jnp.sin(ang), jnp.sin(ang), 0.0 * ones], axis=1)
    i = jnp.arange(PAIR)[:, None]
    j = jnp.arange(PAIR)[None, :]
    same = (i // HEAD_DIM) == (j // HEAD_DIM)
    ih, jh = i % HEAD_DIM, j % HEAD_DIM
    pm = (jnp.where(same & (jh < half) & (ih == jh + half), -1.0, 0.0)
          + jnp.where(same & (jh >= half) & (jh < ROPE_DIM) & (ih == jh - half), 1.0, 0.0))
    mean = jnp.where(same, 1.0 / HEAD_DIM, 0.0)
    twice = lambda m: jnp.concatenate([m, m]).astype(BF16)
    return jnp.tile(cs, (1, 2)), jnp.tile(sn, (1, 2)), twice(mean), twice(pm)


def _split_dot(x, w2):
    hi = x.astype(BF16)
    lo = lax.stop_gradient(x - hi.astype(F32)).astype(BF16)
    return lax.dot_general(jnp.concatenate([hi, lo], axis=1), w2, NN, preferred_element_type=F32)


def _dil_prep(x, w, cs, sn, mean2, pm2):
    y = x * lax.rsqrt(_split_dot(x * x, mean2) + RMS_EPS) * w
    return y * cs + _split_dot(y, pm2) * sn


def _dil_tile(qn, kk, vv, bias):
    lane = lax.broadcasted_iota(jnp.int32, (1, PAIR), 1)
    outs, lses = [], []
    for h in range(2):
        s = _mm(jnp.where(lane // HEAD_DIM == h, qn, 0.0) * (HEAD_DIM ** -0.5), kk, NT) + bias
        m = lax.stop_gradient(jnp.max(s, axis=-1, keepdims=True))
        p = jnp.exp(s - m)
        denom = jnp.sum(p, axis=-1, keepdims=True)
        outs.append(_mm(p, vv) / denom)
        lses.append(m + jnp.log(denom))
    return jnp.where(lane < HEAD_DIM, outs[0], outs[1]), jnp.concatenate(lses, axis=1)


_dil_tiles = jax.vmap(_dil_tile)


def _spread(a):
    lane = lax.broadcasted_iota(jnp.int32, (a.shape[0], PAIR), 1)
    return jnp.where(lane < HEAD_DIM, a[:, 0:1], a[:, 1:2])


def _dil_mix(o1, o2, o3, l1, l2, l3):
    m = lax.stop_gradient(jnp.maximum(jnp.maximum(l1, l2), l3))
    e1, e2, e3 = jnp.exp(l1 - m), jnp.exp(l2 - m), jnp.exp(l3 - m)
    r = 1.0 / (e1 + e2 + e3)
    return _spread(e1 * r) * o1 + _spread(e2 * r) * o2 + _spread(e3 * r) * o3


def _dil_fill_biases(bias_s):
    steps, = set(DIL_STEPS)
    qi = lax.broadcasted_iota(jnp.int32, (BLOCK, 1), 0)
    kj = lax.broadcasted_iota(jnp.int32, (1, 2 * BLOCK), 1)
    rel = qi - kj + BLOCK
    inside = (rel >= 0) & (rel <= steps)
    bias_s[0] = jnp.where(inside, 0.0, NEG)
    bias_s[1] = jnp.where(inside & (kj >= BLOCK), 0.0, NEG)
    bias_s[2] = jnp.where((qi >= kj) & (qi - kj <= steps), 0.0, NEG)


def _dil_mask(it, g, bias_s):
    qrows = pl.ds(pl.multiple_of(it * BLOCK, BLOCK), BLOCK)
    if DIL_NB[g] == 1:
        return bias_s[2, :, 0:BLOCK], qrows, qrows
    which = jnp.where(it == 0, 2, jnp.where(it % DIL_NB[g] == 0, 1, 0))
    kstart = jnp.maximum(it - 1, 0) * BLOCK
    return bias_s[which], qrows, pl.ds(pl.multiple_of(kstart, BLOCK), 2 * BLOCK)


def _dil_gather(src, dst, d):
    t = src.shape[0]
    ln = t // d
    for r in range(d):
        dst[pl.ds(r * ln, ln), :] = src[pl.ds(r, ln, stride=d), :]


def _dil_scatter(src, dst, d):
    t = src.shape[0]
    ln = t // d
    for r in range(d):
        dst[pl.ds(r, ln, stride=d), :] = src[pl.ds(r * ln, ln), :]


def _dil_forward_parts(q_ref, k_ref, v_ref, qw, kw, cs_ref, sn_ref, mean2, pm2, qn_s, kn_s, dl_s, od_s, ld_s, on_s, ln_s, bias_s):
    t = qn_s.shape[0]
    _dil_fill_biases(bias_s)

    def prep(c, _):
        rows = pl.ds(pl.multiple_of(c * ROWS, ROWS), ROWS)
        qn_s[rows, :] = _dil_prep(q_ref[rows, :], qw, cs_ref[rows, :], sn_ref[rows, :], mean2, pm2)
        kn_s[rows, :] = _dil_prep(k_ref[rows, :], kw, cs_ref[rows, :], sn_ref[rows, :], mean2, pm2)
        return 0

    lax.fori_loop(0, t // ROWS, prep, 0)
    for g in (1, 2):
        _dil_gather(qn_s, dl_s.at[g - 1, 0], DIL_D[g])
        _dil_gather(kn_s, dl_s.at[g - 1, 1], DIL_D[g])
        _dil_gather(v_ref, dl_s.at[g - 1, 2], DIL_D[g])
    for g in range(3):
        qs = qn_s if g == 0 else dl_s.at[g - 1, 0]
        ks = kn_s if g == 0 else dl_s.at[g - 1, 1]
        vs = v_ref if g == 0 else dl_s.at[g - 1, 2]

        def tiles(i, _, g=g, qs=qs, ks=ks, vs=vs):
            where = [_dil_mask(i * DIL_B + b, g, bias_s) for b in range(DIL_B)]
            o, lse = _dil_tiles(jnp.stack([qs[qr, :] for _, qr, _ in where]), jnp.stack([ks[kr, :] for _, _, kr in where]),
                                jnp.stack([vs[kr, :] for _, _, kr in where]), jnp.stack([m for m, _, _ in where]))
            for b, (_, qr, _) in enumerate(where):
                od_s[g, qr, :] = o[b]
                ld_s[g, qr, :] = lse[b]
            return 0

        lax.fori_loop(0, t // BLOCK // DIL_B, tiles, 0)
    for g in (1, 2):
        _dil_scatter(od_s.at[g], on_s.at[g - 1], DIL_D[g])
        _dil_scatter(ld_s.at[g], ln_s.at[g - 1], DIL_D[g])


def _dil_scratch(t):
    return [pltpu.VMEM((t, PAIR), F32), pltpu.VMEM((t, PAIR), F32),
            pltpu.VMEM((2, 3, t, PAIR), F32),
            pltpu.VMEM((3, t, PAIR), F32), pltpu.VMEM((3, t, 2), F32),
            pltpu.VMEM((2, t, PAIR), F32), pltpu.VMEM((2, t, 2), F32),
            pltpu.VMEM((3, BLOCK, 2 * BLOCK), F32)]


def _dil_specs(t):
    cb = COL_C // BLOCK
    per = C_WIDTH // BLOCK
    qkv = [pl.BlockSpec((t, BLOCK), lambda b, p, j=j: (b, cb + j * per + p)) for j in range(3)]
    vec = pl.BlockSpec((1, PAIR), lambda b, p: (0, 0))
    tab = pl.BlockSpec((t, PAIR), lambda b, p: (0, 0))
    mat = pl.BlockSpec((2 * PAIR, PAIR), lambda b, p: (0, 0))
    pair = pl.BlockSpec((t, BLOCK), lambda b, p: (b, p))
    return qkv, vec, tab, mat, pair


def dil_fwd(proj, qw, kw, cs, sn, mean2, pm2, nseq):
    n = proj.shape[0]
    t = n // nseq

    def body(q_ref, k_ref, v_ref, qw_ref, kw_ref, cs_ref, sn_ref, mean_ref, pm_ref, o_ref,
             qn_s, kn_s, dl_s, od_s, ld_s, on_s, ln_s, bias_s):
        _dil_forward_parts(q_ref, k_ref, v_ref, qw_ref[...], kw_ref[...], cs_ref, sn_ref, mean_ref[...], pm_ref[...],
                           qn_s, kn_s, dl_s, od_s, ld_s, on_s, ln_s, bias_s)

        def mix(c, _):
            rows = pl.ds(pl.multiple_of(c * ROWS, ROWS), ROWS)
            o_ref[rows, :] = _dil_mix(od_s[0, rows, :], on_s[0, rows, :], on_s[1, rows, :],
                                      ld_s[0, rows, :], ln_s[0, rows, :], ln_s[1, rows, :])
            return 0

        lax.fori_loop(0, t // ROWS, mix, 0)

    qkv, vec, tab, mat, pair = _dil_specs(t)
    return pl.pallas_call(
        body, name="dil_fwd", grid=(nseq, C_WIDTH // BLOCK),
        in_specs=qkv + [vec, vec, tab, tab, mat, mat],
        out_specs=pair,
        out_shape=jax.ShapeDtypeStruct((n, C_WIDTH), F32),
        scratch_shapes=_dil_scratch(t),
        compiler_params=_cparams("parallel", "parallel"),
    )(proj, proj, proj, qw, kw, cs, sn, mean2, pm2)


def dil_bwd(proj, qw, kw, cs, sn, mean2, pm2, do, nseq):
    n = proj.shape[0]
    t = n // nseq

    def body(q_ref, k_ref, v_ref, qw_ref, kw_ref, cs_ref, sn_ref, mean_ref, pm_ref, do_ref,
             dq_ref, dk_ref, dv_ref, dqw_ref, dkw_ref,
             qn_s, kn_s, dl_s, od_s, ld_s, on_s, ln_s, bias_s, tq_s, tk_s, tv_s):
        qw, kw, mean2, pm2 = qw_ref[...], kw_ref[...], mean_ref[...], pm_ref[...]
        _dil_forward_parts(q_ref, k_ref, v_ref, qw, kw, cs_ref, sn_ref, mean2, pm2, qn_s, kn_s, dl_s, od_s, ld_s, on_s, ln_s, bias_s)

        def mix(c, _):
            rows = pl.ds(pl.multiple_of(c * ROWS, ROWS), ROWS)
            _, vjp = jax.vjp(_dil_mix, od_s[0, rows, :], on_s[0, rows, :], on_s[1, rows, :],
                             ld_s[0, rows, :], ln_s[0, rows, :], ln_s[1, rows, :])
            d1, d2, d3, e1, e2, e3 = vjp(do_ref[rows, :])
            od_s[0, rows, :] = d1
            on_s[0, rows, :] = d2
            on_s[1, rows, :] = d3
            ld_s[0, rows, :] = e1
            ln_s[0, rows, :] = e2
            ln_s[1, rows, :] = e3
            return 0

        lax.fori_loop(0, t // ROWS, mix, 0)
        for g in (1, 2):
            _dil_gather(on_s.at[g - 1], od_s.at[g], DIL_D[g])
            _dil_gather(ln_s.at[g - 1], ld_s.at[g], DIL_D[g])
        on_s[...] = jnp.zeros_like(on_s)
        dv_ref[...] = jnp.zeros_like(dv_ref)
        for g in range(3):
            qs = qn_s if g == 0 else dl_s.at[g - 1, 0]
            ks = kn_s if g == 0 else dl_s.at[g - 1, 1]
            vs = v_ref if g == 0 else dl_s.at[g - 1, 2]
            gq = on_s.at[0] if g == 0 else tq_s
            gk = on_s.at[1] if g == 0 else tk_s
            gv = dv_ref if g == 0 else tv_s
            if g > 0:
                tk_s[...] = jnp.zeros_like(tk_s)
                tv_s[...] = jnp.zeros_like(tv_s)

            def tiles(i, _, g=g, qs=qs, ks=ks, vs=vs, gq=gq, gk=gk, gv=gv):
                where = [_dil_mask(i * DIL_B + b, g, bias_s) for b in range(DIL_B)]
                biases = jnp.stack([m for m, _, _ in where])
                _, vjp = jax.vjp(lambda q_, k_, v_: _dil_tiles(q_, k_, v_, biases),
                                 jnp.stack([qs[qr, :] for _, qr, _ in where]), jnp.stack([ks[kr, :] for _, _, kr in where]),
                                 jnp.stack([vs[kr, :] for _, _, kr in where]))
                dq, dkk, dvv = vjp((jnp.stack([od_s[g, qr, :] for _, qr, _ in where]),
                                    jnp.stack([ld_s[g, qr, :] for _, qr, _ in where])))
                for b, (_, qr, kr) in enumerate(where):
                    gq[qr, :] = dq[b]
                    gk[kr, :] += dkk[b]
                    gv[kr, :] += dvv[b]
                return 0

            lax.fori_loop(0, t // BLOCK // DIL_B, tiles, 0)
            if g > 0:
                d = DIL_D[g]
                ln = t // d
                for r in range(d):
                    nat, dil = pl.ds(r, ln, stride=d), pl.ds(r * ln, ln)
                    on_s[0, nat, :] += tq_s[dil, :]
                    on_s[1, nat, :] += tk_s[dil, :]
                    dv_ref[nat, :] += tv_s[dil, :]

        def prep(c, acc):
            rows = pl.ds(pl.multiple_of(c * ROWS, ROWS), ROWS)
            f = lambda x, w: _dil_prep(x, w, cs_ref[rows, :], sn_ref[rows, :], mean2, pm2)
            _, vq = jax.vjp(f, q_ref[rows, :], qw)
            _, vk = jax.vjp(f, k_ref[rows, :], kw)
            dq, dqw = vq(on_s[0, rows, :])
            dk, dkw = vk(on_s[1, rows, :])
            dq_ref[rows, :] = dq
            dk_ref[rows, :] = dk
            return acc[0] + dqw, acc[1] + dkw

        dqw, dkw = lax.fori_loop(0, t // ROWS, prep, (jnp.zeros((1, PAIR), F32), jnp.zeros((1, PAIR), F32)))
        dqw_ref[0] = dqw
        dkw_ref[0] = dkw

    qkv, vec, tab, mat, pair = _dil_specs(t)
    per = C_WIDTH // BLOCK
    wout = pl.BlockSpec((1, 1, PAIR), lambda b, p: (b * per + p, 0, 0))
    return pl.pallas_call(
        body, name="dil_bwd", grid=(nseq, per),
        in_specs=qkv + [vec, vec, tab, tab, mat, mat, pair],
        out_specs=[pair, pair, pair, wout, wout],
        out_shape=[jax.ShapeDtypeStruct((n, C_WIDTH), F32)] * 3 + [jax.ShapeDtypeStruct((nseq * per, 1, PAIR), F32)] * 2,
        scratch_shapes=_dil_scratch(t) + [pltpu.VMEM((t, PAIR), F32)] * 3,
        compiler_params=_cparams("parallel", "parallel"),
    )(proj, proj, proj, qw, kw, cs, sn, mean2, pm2, do)


N_CHIPS = 4
SUM_ROWS = 432
MESH_IDS = pl.DeviceIdType.MESH
ANY = pl.BlockSpec(memory_space=pl.ANY)


def plane_exchange(src, all_to_all):
    blk_shape = src.shape[1:] if all_to_all else src.shape

    def body(src_ref, out_ref, send_sems, recv_sems, local_sem):
        x, y, c = lax.axis_index("x"), lax.axis_index("y"), lax.axis_index("c")
        me = 2 * x + y
        mine = pltpu.make_async_copy(src_ref.at[me] if all_to_all else src_ref, out_ref.at[me], local_sem)
        mine.start()
        sends = []
        for k in (1, 2, 3):
            px = 1 - x if k & 2 else x
            py = 1 - y if k & 1 else y
            peer = 2 * px + py
            cp = pltpu.make_async_remote_copy(
                src_ref=src_ref.at[peer] if all_to_all else src_ref, dst_ref=out_ref.at[me],
                send_sem=send_sems.at[k - 1], recv_sem=recv_sems.at[k - 1],
                device_id=(px, py, c), device_id_type=MESH_IDS)
            cp.start()
            sends.append((cp, peer, (px, py, c)))
        for k, (cp, peer, dev) in enumerate(sends):
            pltpu.make_async_remote_copy(
                src_ref=out_ref.at[me], dst_ref=out_ref.at[peer],
                send_sem=send_sems.at[k], recv_sem=recv_sems.at[k],
                device_id=dev, device_id_type=MESH_IDS).wait_recv()
        for cp, _, _ in sends:
            cp.wait_send()
        mine.wait()

    return pl.pallas_call(
        body, name="plane_all_to_all" if all_to_all else "plane_all_gather",
        in_specs=[ANY], out_specs=ANY,
        out_shape=jax.ShapeDtypeStruct((N_CHIPS,) + blk_shape, src.dtype),
        scratch_shapes=[pltpu.SemaphoreType.DMA((3,)), pltpu.SemaphoreType.DMA((3,)), pltpu.SemaphoreType.DMA],
    )(src)


def sibling_swap(src, other_half=False):
    shape = (src.shape[0], src.shape[1] // 2) + src.shape[2:] if other_half else src.shape

    def body(src_ref, out_ref, send_sem, recv_sem):
        x, y, c = lax.axis_index("x"), lax.axis_index("y"), lax.axis_index("c")
        part = src_ref.at[:, pl.ds((1 - c) * shape[1], shape[1])] if other_half else src_ref
        cp = pltpu.make_async_remote_copy(src_ref=part, dst_ref=out_ref, send_sem=send_sem, recv_sem=recv_sem,
                                          device_id=(x, y, 1 - c), device_id_type=MESH_IDS)
        cp.start()
        cp.wait()

    return pl.pallas_call(
        body, name="sibling_swap", in_specs=[ANY], out_specs=ANY,
        out_shape=jax.ShapeDtypeStruct(shape, src.dtype),
        scratch_shapes=[pltpu.SemaphoreType.DMA, pltpu.SemaphoreType.DMA],
    )(src)


def sum4(a):
    _, r, c = a.shape
    tr = SUM_ROWS

    def body(a_ref, o_ref):
        p = [a_ref[i].astype(F32) for i in range(N_CHIPS)]
        o_ref[...] = (p[0] + p[1]) + (p[2] + p[3])

    return pl.pallas_call(
        body, name="sum4", grid=(r // tr,),
        in_specs=[pl.BlockSpec((N_CHIPS, tr, c), lambda i: (0, i, 0))],
        out_specs=pl.BlockSpec((tr, c), lambda i: (i, 0)),
        out_shape=jax.ShapeDtypeStruct((r, c), F32),
        compiler_params=_cparams("parallel"),
    )(a)


def add_my_half(mine, got, c):
    nchip, r2, cols = mine.shape
    nt = r2 // 2 // SUM_ROWS

    def body(c_ref, a_ref, b_ref, o_ref):
        o_ref[...] = (a_ref[...] + b_ref[...]).astype(BF16)

    blk = pl.BlockSpec((1, SUM_ROWS, cols), lambda j, i, c_ref: (j, i, 0))
    return pl.pallas_call(
        body, name="add_my_half",
        grid_spec=pltpu.PrefetchScalarGridSpec(
            num_scalar_prefetch=1, grid=(nchip, nt),
            in_specs=[pl.BlockSpec((1, SUM_ROWS, cols), lambda j, i, c_ref: (j, c_ref[0] * nt + i, 0)), blk],
            out_specs=blk),
        out_shape=jax.ShapeDtypeStruct((nchip, r2 // 2, cols), BF16),
        compiler_params=_cparams("parallel", "parallel"),
    )(jnp.reshape(c, (1,)).astype(jnp.int32), mine, got)


PACK_COLS = 1152
PACK_ROWS = 2592
ROW_TILE = 16


def _pack(parts):
    blocks = []
    for p in parts:
        p2 = p.reshape(-1, p.shape[-1])
        blocks.append(jnp.pad(p2, ((0, -p2.shape[0] % ROW_TILE), (0, PACK_COLS - p2.shape[1]))))
    rows = sum(b.shape[0] for b in blocks)
    blocks.append(jnp.zeros((PACK_ROWS - rows, PACK_COLS), blocks[0].dtype))
    return jnp.concatenate(blocks)


def _unpack(buf, shapes):
    out, at = [], 0
    for s in shapes:
        rows = math.prod(s[:-1])
        out.append(buf[at:at + rows, :s[-1]].reshape(s))
        at += rows + (-rows % ROW_TILE)
    return out


def _pack_small(g):
    blk = jnp.zeros((ROW_TILE, PACK_COLS), F32)
    for i, k in enumerate(SMALL):
        blk = blk.at[2 * i:2 * i + 2, :g[k].shape[1]].set(g[k])
    return blk


def _unpack_small(blk, shapes):
    return [blk[2 * i:2 * i + 2, :s[1]] for i, s in enumerate(shapes)]


def _layer_fwd(x, p, nseq, tabs):
    proj, hdn = inproj_fwd(x, p["norm_w"][None], p["w_in"])
    ya = conv_fwd(proj, p["conv_w"], nseq)
    oa, states, inverses = gdn_fwd(ya, proj, p["a_log"].reshape(N_HEADS_A, 1, 1), p["dt_bias"].reshape(N_HEADS_A, 1, 1),
                         p["gdn_norm_w"][None], nseq)
    ob, carries = sb_fwd(proj, nseq)
    oc = dil_fwd(proj, jnp.tile(p["q_norm_w"], 2)[None], jnp.tile(p["k_norm_w"], 2)[None], *tabs, nseq)
    y, mixed = outproj_fwd(x, oa, ob, oc, proj, p["w_out"])
    return y, dict(x=x, hdn=hdn, proj=proj, ya=ya, states=states, inverses=inverses, carries=carries, oa=oa, ob=ob, oc=oc, mixed=mixed)


def _layer_bwd(dy, p, res, nseq, tabs):
    proj = res["proj"]
    g = {}
    g["w_out"] = mat_tn(res["mixed"], [dy])[0]
    doa, dob, doc, dza, dzb, dzc = outproj_bwd(dy, res["oa"], res["ob"], res["oc"], proj, p["w_out"])
    dqc, dkc, dvc, dqw, dkw = dil_bwd(proj, jnp.tile(p["q_norm_w"], 2)[None], jnp.tile(p["k_norm_w"], 2)[None], *tabs, doc,
                                      nseq)
    g["q_norm_w"], g["k_norm_w"] = dqw.reshape(-1, HEAD_DIM).sum(0), dkw.reshape(-1, HEAD_DIM).sum(0)
    dqb, dkb, dvb = sb_bwd(proj, res["carries"], dob, nseq)
    dya, dba, dalog, ddtb, dnw = gdn_bwd(res["ya"], proj, p["a_log"].reshape(N_HEADS_A, 1, 1),
                                         p["dt_bias"].reshape(N_HEADS_A, 1, 1), p["gdn_norm_w"][None], res["states"], res["inverses"], doa,
                                         nseq)
    g["a_log"], g["dt_bias"], g["gdn_norm_w"] = dalog.sum(0).reshape(-1), ddtb.sum(0).reshape(-1), dnw.sum((0, 1))
    dqkv, dcw = conv_bwd(proj, p["conv_w"], dya, nseq)
    g["conv_w"] = dcw.sum(0)
    slabs = [dqkv, dza, dqc, dkc, dvc, dzc, dqb, dkb, dvb, dzb, dba]
    hdn = res["hdn"]
    g["w_in"] = jnp.concatenate(mat_tn(hdn, slabs[:6]) + mat_tn(hdn, slabs[6:]), axis=1)
    dx, dnw_tiles = inproj_bwd(slabs, p["w_in"], res["x"], p["norm_w"][None], dy)
    g["norm_w"] = dnw_tiles.sum((0, 1))
    return dx, g


SMALL = ("norm_w", "a_log", "dt_bias", "gdn_norm_w", "q_norm_w", "k_norm_w")


def _local_step(x, target, full):
    nseq, t, d = x.shape
    tabs = _rope_tables(t)
    h = x.reshape(nseq * t, d)
    saved = []
    for l in range(DEPTH):
        p = {k: v[l] for k, v in full.items()}
        h, res = _layer_fwd(h, p, nseq, tabs)
        saved.append((p, res))
    dy, parts = loss_fwd_bwd(h, target.reshape(nseq * t, d))
    loss = parts[:, 0, 0].sum()
    grads = [None] * DEPTH
    for l in reversed(range(DEPTH)):
        p, res = saved[l]
        dy, grads[l] = _layer_bwd(dy, p, res, nseq, tabs)
    return loss, dy.reshape(nseq, t, d), {k: jnp.stack([g[k] for g in grads]) for k in grads[0]}


def _pad_cols(w):
    b0 = ORIG_A + ORIG_BA
    c0 = b0 + ORIG_B
    zeros = jnp.zeros(w.shape[:-1] + (BLOCK - ORIG_BA,), w.dtype)
    return jnp.concatenate([w[..., :ORIG_A], w[..., c0:], w[..., b0:c0], w[..., ORIG_A:b0], zeros], axis=-1)


def _unpad_cols(w):
    return jnp.concatenate([w[..., :COL_C], w[..., COL_BA:COL_BA + ORIG_BA], w[..., COL_B:COL_BA], w[..., COL_C:COL_B]],
                           axis=-1)


def kernel(x, norm_w, w_in, conv_w, a_log, dt_bias, gdn_norm_w, q_norm_w, k_norm_w, w_out, loss_target, m_norm_w, m_w_in, m_conv_w, m_a_log, m_dt_bias, m_gdn_norm_w, m_q_norm_w, m_k_norm_w, m_w_out, v_norm_w, v_w_in, v_conv_w, v_a_log, v_dt_bias, v_gdn_norm_w, v_q_norm_w, v_k_norm_w, v_w_out):
    weights = dict(norm_w=norm_w, w_in=w_in, conv_w=conv_w, a_log=a_log, dt_bias=dt_bias, gdn_norm_w=gdn_norm_w,
                   q_norm_w=q_norm_w, k_norm_w=k_norm_w, w_out=w_out)
    moms = dict(norm_w=m_norm_w, w_in=m_w_in, conv_w=m_conv_w, a_log=m_a_log, dt_bias=m_dt_bias,
                gdn_norm_w=m_gdn_norm_w, q_norm_w=m_q_norm_w, k_norm_w=m_k_norm_w, w_out=m_w_out)
    vars_ = dict(norm_w=v_norm_w, w_in=v_w_in, conv_w=v_conv_w, a_log=v_a_log, dt_bias=v_dt_bias,
                 gdn_norm_w=v_gdn_norm_w, q_norm_w=v_q_norm_w, k_norm_w=v_k_norm_w, w_out=v_w_out)
    names = list(weights)
    sharded = ("w_in", "w_out", "conv_w")
    shard_shapes = [weights[k].shape for k in sharded]

    c = lax.axis_index("c")
    half = PACK_ROWS // 2
    conv_bits = lax.bitcast_convert_type(conv_w, BF16).reshape(conv_w.shape[:2] + (2 * conv_w.shape[2],))
    shard = _pack([w_in.astype(BF16), w_out.astype(BF16), conv_bits])
    mine = plane_exchange(lax.dynamic_slice_in_dim(shard, c * half, half, axis=0), all_to_all=False)
    other = sibling_swap(mine)
    got = jnp.concatenate([jnp.where(c == 0, mine, other), jnp.where(c == 0, other, mine)], axis=1)
    per_chip = [_unpack(got[i], shard_shapes[:2] + [conv_bits.shape]) for i in range(N_CHIPS)]
    full = {k: weights[k] for k in SMALL}
    full["w_in"] = _pad_cols(jnp.concatenate([pc[0] for pc in per_chip], axis=2))
    full["w_out"] = jnp.concatenate([pc[1] for pc in per_chip], axis=1)
    full["conv_w"] = jnp.concatenate(
        [lax.bitcast_convert_type(pc[2].reshape(conv_w.shape + (2,)), F32) for pc in per_chip], axis=2)

    loss, grad_x, g = _local_step(x, loss_target, full)

    gw_in = _unpad_cols(g["w_in"])
    cols, rows = w_in.shape[2], w_out.shape[1]
    small = _pack_small(g)
    send = jnp.stack([_pack([gw_in[:, :, i * cols:(i + 1) * cols], g["w_out"][:, i * rows:(i + 1) * rows],
                             g["conv_w"][:, :, i * conv_w.shape[2]:(i + 1) * conv_w.shape[2]], small])
                      for i in range(N_CHIPS)])
    chip_sum = add_my_half(send, sibling_swap(send, other_half=True), c)
    mine = sum4(plane_exchange(chip_sum, all_to_all=True))
    other = sibling_swap(mine)
    total = jnp.concatenate([jnp.where(c == 0, mine, other), jnp.where(c == 0, other, mine)])
    reduced = _unpack(total, shard_shapes + [(ROW_TILE, PACK_COLS)])
    grads = dict(zip(sharded, reduced[:3]))
    grads.update(zip(SMALL, _unpack_small(reduced[3], [weights[k].shape for k in SMALL])))
    loss = lax.psum(loss, ("x", "y", "c"))

    def two_d(a):
        return a.reshape(-1, a.shape[-1])

    delta, new_m, new_v = {}, {}, {}
    for k in names:
        d_, m_, v_ = adamw(two_d(weights[k]), two_d(grads[k]), two_d(moms[k]), two_d(vars_[k]))
        delta[k], new_m[k], new_v[k] = (a.reshape(weights[k].shape) for a in (d_, m_, v_))
    return (loss, grad_x, *[grads[k] for k in names], *[delta[k] for k in names],
            *[new_m[k] for k in names], *[new_v[k] for k in names])
```

```python
import functools
import math

import jax
import jax.numpy as jnp
from jax import lax
from jax.experimental import pallas as pl
from jax.experimental.pallas import tpu as pltpu

F32 = jnp.float32
BF16 = jnp.bfloat16

D_MODEL = 1024
SEQ = 2048
DEPTH = 2
HEAD_DIM = 64
N_HEADS_A, N_HEADS_B, N_HEADS_C = 6, 4, 6
A_WIDTH, B_WIDTH, C_WIDTH = N_HEADS_A * HEAD_DIM, N_HEADS_B * HEAD_DIM, N_HEADS_C * HEAD_DIM
CONV_WIDTH = 4
GDN_CHUNK = 64
BLOCK = 128
ROPE_DIM = 16
ROPE_THETA = 500000.0
DILATED_PAIRS = ((128, 1), (512, 4), (2048, 16))
RMS_EPS = 1e-6
NEG = -1e30

NT = (((1,), (1,)), ((), ()))
NN = (((1,), (0,)), ((), ()))
TN = (((0,), (0,)), ((), ()))

VMEM_LIMIT = 48 * 1024 * 1024

ORIG_A = 4 * A_WIDTH
ORIG_BA = 2 * N_HEADS_A
ORIG_B = 4 * B_WIDTH
COL_AZ = 3 * A_WIDTH
COL_C = 4 * A_WIDTH
COL_B = COL_C + 4 * C_WIDTH
COL_BA = COL_B + 4 * B_WIDTH
P_COLS = COL_BA + BLOCK
TN_COLS = 384
INPROJ_COLS = P_COLS // 3
TM_ROWS = 512
ROWS = 1024


def _mm(a, b, dims=NN):
    return lax.dot_general(a.astype(BF16), b.astype(BF16), dims, preferred_element_type=F32)


def _mm32(a, b, dims=NN):
    return lax.dot_general(a, b, dims, precision=lax.Precision.HIGH, preferred_element_type=F32)


def _cparams(*sem):
    return pltpu.CompilerParams(dimension_semantics=sem, vmem_limit_bytes=VMEM_LIMIT)


def _sigmoid(x):
    return 0.5 * (jnp.tanh(0.5 * x) + 1.0)


def _softplus(x):
    return jnp.maximum(x, 0.0) + jnp.log(1.0 + jnp.exp(-jnp.abs(x)))


def _rms(x, w):
    return x * lax.rsqrt(jnp.mean(x * x, axis=-1, keepdims=True) + RMS_EPS) * w


def _heads(a, n):
    return jnp.stack([a[:, h * HEAD_DIM:(h + 1) * HEAD_DIM] for h in range(n)])


def _unheads(a):
    return jnp.concatenate([a[h] for h in range(a.shape[0])], axis=1)


def inproj_fwd(x, nw, w):
    n, d = x.shape
    p = w.shape[1]

    def body(x_ref, nw_ref, w_ref, proj_ref, hdn_ref):
        @pl.when(pl.program_id(1) == 0)
        def _():
            hdn_ref[...] = _rms(x_ref[...], nw_ref[...]).astype(BF16)

        proj_ref[...] = jnp.dot(hdn_ref[...], w_ref[...], preferred_element_type=F32)

    return pl.pallas_call(
        body, name="inproj_fwd", grid=(n // TM_ROWS, p // INPROJ_COLS),
        in_specs=[pl.BlockSpec((TM_ROWS, d), lambda i, j: (i, 0)), pl.BlockSpec((1, d), lambda i, j: (0, 0)),
                  pl.BlockSpec((d, INPROJ_COLS), lambda i, j: (0, j))],
        out_specs=[pl.BlockSpec((TM_ROWS, INPROJ_COLS), lambda i, j: (i, j)), pl.BlockSpec((TM_ROWS, d), lambda i, j: (i, 0))],
        out_shape=[jax.ShapeDtypeStruct((n, p), F32), jax.ShapeDtypeStruct((n, d), BF16)],
        compiler_params=_cparams("parallel", "arbitrary"),
    )(x, nw, w)


def mat_tn(a, slabs):
    n, ka = a.shape
    ns = len(slabs)

    def body(*refs):
        a_ref, s_refs, o_refs = refs[0], refs[1:1 + ns], refs[1 + ns:]

        @pl.when(pl.program_id(0) == 0)
        def _():
            for o_ref in o_refs:
                o_ref[...] = jnp.zeros_like(o_ref)

        at = a_ref[...].T
        for s_ref, o_ref in zip(s_refs, o_refs):
            o_ref[...] += jnp.dot(at, s_ref[...].astype(BF16), preferred_element_type=F32)

    return pl.pallas_call(
        body, name="mat_tn", grid=(n // TM_ROWS,),
        in_specs=[pl.BlockSpec((TM_ROWS, ka), lambda k: (k, 0))]
                 + [pl.BlockSpec((TM_ROWS, s.shape[1]), lambda k: (k, 0)) for s in slabs],
        out_specs=[pl.BlockSpec((ka, s.shape[1]), lambda k: (0, 0)) for s in slabs],
        out_shape=[jax.ShapeDtypeStruct((ka, s.shape[1]), F32) for s in slabs],
        compiler_params=_cparams("arbitrary"),
    )(a, *slabs)


def inproj_bwd(slabs, w, x, nw, dy):
    n, d = x.shape
    p = w.shape[1]
    tm = 256
    ns = len(slabs)

    def body(*refs):
        s_refs = refs[:ns]
        w_ref, x_ref, nw_ref, dy_ref, dx_ref, dnw_ref = refs[ns:]
        dh = jnp.zeros((tm, d), F32)
        at = 0
        for s_ref in s_refs:
            wd = s_ref.shape[1]
            dh = dh + lax.dot_general(s_ref[...].astype(BF16), w_ref[:, at:at + wd], NT, preferred_element_type=F32)
            at += wd
        _, vjp = jax.vjp(_rms, x_ref[...], nw_ref[...])
        dx, dnw = vjp(dh)
        dx_ref[...] = dx + dy_ref[...]
        dnw_ref[0] = dnw

    return pl.pallas_call(
        body, name="inproj_bwd", grid=(n // tm,),
        in_specs=[pl.BlockSpec((tm, s.shape[1]), lambda i: (i, 0)) for s in slabs]
                 + [pl.BlockSpec((d, p), lambda i: (0, 0)), pl.BlockSpec((tm, d), lambda i: (i, 0)),
                    pl.BlockSpec((1, d), lambda i: (0, 0)), pl.BlockSpec((tm, d), lambda i: (i, 0))],
        out_specs=[pl.BlockSpec((tm, d), lambda i: (i, 0)), pl.BlockSpec((1, 1, d), lambda i: (i, 0, 0))],
        out_shape=[jax.ShapeDtypeStruct((n, d), F32), jax.ShapeDtypeStruct((n // tm, 1, d), F32)],
        compiler_params=_cparams("parallel"),
    )(*slabs, w, x, nw, dy)


CONV_PAD = 8
CONV_ROWS = 256


def _conv_pre(pad_s, cw, c):
    xs = [pad_s[pl.ds(c * CONV_ROWS + CONV_PAD - (CONV_WIDTH - 1) + k, CONV_ROWS), :] for k in range(CONV_WIDTH)]
    pre = xs[0] * cw[0:1, :]
    for k in range(1, CONV_WIDTH):
        pre = pre + xs[k] * cw[k:k + 1, :]
    return pre, xs


def conv_fwd(proj, cw, nseq):
    n = proj.shape[0]
    t = n // nseq
    ch = cw.shape[1]

    def body(x_ref, cw_ref, y_ref, pad_s):
        pad_s[pl.ds(0, CONV_PAD), :] = jnp.zeros((CONV_PAD, TN_COLS), F32)
        pad_s[pl.ds(CONV_PAD, t), :] = x_ref[...]
        cwv = cw_ref[...]
        for c in range(t // CONV_ROWS):
            pre, _ = _conv_pre(pad_s, cwv, c)
            y_ref[pl.ds(c * CONV_ROWS, CONV_ROWS), :] = pre * _sigmoid(pre)

    return pl.pallas_call(
        body, name="conv_fwd", grid=(nseq, ch // TN_COLS),
        in_specs=[pl.BlockSpec((t, TN_COLS), lambda b, j: (b, j)), pl.BlockSpec((CONV_WIDTH, TN_COLS), lambda b, j: (0, j))],
        out_specs=pl.BlockSpec((t, TN_COLS), lambda b, j: (b, j)),
        out_shape=jax.ShapeDtypeStruct((n, ch), F32),
        scratch_shapes=[pltpu.VMEM((t + CONV_PAD, TN_COLS), F32)],
        compiler_params=_cparams("parallel", "parallel"),
    )(proj, cw)


def conv_bwd(proj, cw, dy, nseq):
    n = proj.shape[0]
    t = n // nseq
    ch = cw.shape[1]

    def body(x_ref, cw_ref, dy_ref, dx_ref, dcw_ref, pad_s, dpad_s):
        pad_s[pl.ds(0, CONV_PAD), :] = jnp.zeros((CONV_PAD, TN_COLS), F32)
        pad_s[pl.ds(CONV_PAD, t), :] = x_ref[...]
        dpad_s[pl.ds(t, CONV_PAD), :] = jnp.zeros((CONV_PAD, TN_COLS), F32)
        cwv = cw_ref[...]
        acc = [jnp.zeros((1, TN_COLS), F32)] * CONV_WIDTH
        for c in range(t // CONV_ROWS):
            pre, xs = _conv_pre(pad_s, cwv, c)
            sg = _sigmoid(pre)
            dpre = dy_ref[pl.ds(c * CONV_ROWS, CONV_ROWS), :] * (sg * (1.0 + pre * (1.0 - sg)))
            dpad_s[pl.ds(c * CONV_ROWS, CONV_ROWS), :] = dpre
            acc = [acc[k] + jnp.sum(dpre * xs[k], axis=0, keepdims=True) for k in range(CONV_WIDTH)]
        for k in range(CONV_WIDTH):
            dcw_ref[0, pl.ds(k, 1), :] = acc[k]
        for c in range(t // CONV_ROWS):
            dx = dpad_s[pl.ds(c * CONV_ROWS + CONV_WIDTH - 1, CONV_ROWS), :] * cwv[0:1, :]
            for k in range(1, CONV_WIDTH):
                dx = dx + dpad_s[pl.ds(c * CONV_ROWS + CONV_WIDTH - 1 - k, CONV_ROWS), :] * cwv[k:k + 1, :]
            dx_ref[pl.ds(c * CONV_ROWS, CONV_ROWS), :] = dx

    blk = pl.BlockSpec((t, TN_COLS), lambda b, j: (b, j))
    return pl.pallas_call(
        body, name="conv_bwd", grid=(nseq, ch // TN_COLS),
        in_specs=[blk, pl.BlockSpec((CONV_WIDTH, TN_COLS), lambda b, j: (0, j)), blk],
        out_specs=[blk, pl.BlockSpec((1, CONV_WIDTH, TN_COLS), lambda b, j: (b, 0, j))],
        out_shape=[jax.ShapeDtypeStruct((n, ch), F32), jax.ShapeDtypeStruct((nseq, CONV_WIDTH, ch), F32)],
        scratch_shapes=[pltpu.VMEM((t + CONV_PAD, TN_COLS), F32)] * 2,
        compiler_params=_cparams("parallel", "parallel"),
    )(proj, cw, dy)


def _gate_specs(d):
    wide = pl.BlockSpec((TM_ROWS, d), lambda i: (i, 0))
    oa = pl.BlockSpec((TM_ROWS, A_WIDTH), lambda i: (i, 0))
    ob = pl.BlockSpec((TM_ROWS, B_WIDTH), lambda i: (i, 0))
    oc = pl.BlockSpec((TM_ROWS, C_WIDTH), lambda i: (i, 0))
    za = pl.BlockSpec((TM_ROWS, A_WIDTH), lambda i: (i, COL_AZ // A_WIDTH))
    zb = pl.BlockSpec((TM_ROWS, B_WIDTH), lambda i: (i, (COL_B + 3 * B_WIDTH) // B_WIDTH))
    zc = pl.BlockSpec((TM_ROWS, C_WIDTH), lambda i: (i, (COL_C + 3 * C_WIDTH) // C_WIDTH))
    return wide, oa, ob, oc, za, zb, zc


BRANCH_COLS = ((0, A_WIDTH), (A_WIDTH, A_WIDTH + B_WIDTH), (A_WIDTH + B_WIDTH, D_MODEL))


def outproj_fwd(x, oa, ob, oc, proj, w):
    n, d = x.shape

    def body(x_ref, oa_ref, ob_ref, oc_ref, za_ref, zb_ref, zc_ref, w_ref, y_ref, m_ref):
        for (lo, hi), o_ref, z_ref in zip(BRANCH_COLS, (oa_ref, ob_ref, oc_ref), (za_ref, zb_ref, zc_ref)):
            zv = z_ref[...]
            m_ref[:, lo:hi] = (o_ref[...] * (zv * _sigmoid(zv))).astype(BF16)
        y_ref[...] = x_ref[...] + jnp.dot(m_ref[...], w_ref[...], preferred_element_type=F32)

    wide, sa, sb, sc, za, zb, zc = _gate_specs(d)
    return pl.pallas_call(
        body, name="outproj_fwd", grid=(n // TM_ROWS,),
        in_specs=[wide, sa, sb, sc, za, zb, zc, pl.BlockSpec((d, d), lambda i: (0, 0))],
        out_specs=[wide, wide],
        out_shape=[jax.ShapeDtypeStruct((n, d), F32), jax.ShapeDtypeStruct((n, d), BF16)],
        compiler_params=_cparams("parallel"),
    )(x, oa, ob, oc, proj, proj, proj, w)


def outproj_bwd(dy, oa, ob, oc, proj, w):
    n, d = dy.shape

    def body(dy_ref, oa_ref, ob_ref, oc_ref, za_ref, zb_ref, zc_ref, w_ref, doa_ref, dob_ref, doc_ref, dza_ref, dzb_ref, dzc_ref):
        dm = lax.dot_general(dy_ref[...].astype(BF16), w_ref[...], NT, preferred_element_type=F32)
        for (lo, hi), o_ref, z_ref, do_ref, dz_ref in zip(BRANCH_COLS, (oa_ref, ob_ref, oc_ref), (za_ref, zb_ref, zc_ref),
                                                          (doa_ref, dob_ref, doc_ref), (dza_ref, dzb_ref, dzc_ref)):
            zv = z_ref[...]
            sg = _sigmoid(zv)
            dmv = dm[:, lo:hi]
            do_ref[...] = dmv * (zv * sg)
            dz_ref[...] = dmv * o_ref[...] * (sg * (1.0 + zv * (1.0 - sg)))

    wide, sa, sb, sc, za, zb, zc = _gate_specs(d)
    sd = jax.ShapeDtypeStruct
    outs = [sd((n, A_WIDTH), F32), sd((n, B_WIDTH), F32), sd((n, C_WIDTH), F32)]
    return pl.pallas_call(
        body, name="outproj_bwd", grid=(n // TM_ROWS,),
        in_specs=[wide, sa, sb, sc, za, zb, zc, pl.BlockSpec((d, d), lambda i: (0, 0))],
        out_specs=[sa, sb, sc, sa, sb, sc],
        out_shape=outs + outs,
        compiler_params=_cparams("parallel"),
    )(dy, oa, ob, oc, proj, proj, proj, w)


def loss_fwd_bwd(y, target):
    n, d = y.shape

    def body(y_ref, t_ref, dy_ref, part_ref):
        e = y_ref[...] - t_ref[...]
        dy_ref[...] = e * (1.0 / d)
        part_ref[...] = jnp.zeros_like(part_ref) + 0.5 * jnp.sum(e * e) * (1.0 / d)

    blk = pl.BlockSpec((TM_ROWS, d), lambda i: (i, 0))
    return pl.pallas_call(
        body, name="loss", grid=(n // TM_ROWS,),
        in_specs=[blk, blk],
        out_specs=[blk, pl.BlockSpec((1, 8, BLOCK), lambda i: (i, 0, 0))],
        out_shape=[jax.ShapeDtypeStruct((n, d), F32), jax.ShapeDtypeStruct((n // TM_ROWS, 8, BLOCK), F32)],
        compiler_params=_cparams("parallel"),
    )(y, target)


ADAM_LR, ADAM_B1, ADAM_B2, ADAM_EPS, ADAM_WD, ADAM_STEP = 0.001, 0.9, 0.999, 1e-08, 0.01, 10


def adamw(w, g, m, v):
    r, c = w.shape
    tr = r if r <= 256 else 256

    def body(w_ref, g_ref, m_ref, v_ref, d_ref, nm_ref, nv_ref):
        gv = g_ref[...]
        nm = ADAM_B1 * m_ref[...] + (1.0 - ADAM_B1) * gv
        nv = ADAM_B2 * v_ref[...] + (1.0 - ADAM_B2) * (gv * gv)
        m_hat = nm / (1.0 - ADAM_B1 ** ADAM_STEP)
        v_hat = nv / (1.0 - ADAM_B2 ** ADAM_STEP)
        d_ref[...] = -ADAM_LR * (m_hat / (jnp.sqrt(v_hat) + ADAM_EPS) + ADAM_WD * w_ref[...])
        nm_ref[...] = nm
        nv_ref[...] = nv

    blk = pl.BlockSpec((tr, c), lambda i: (i, 0))
    return pl.pallas_call(
        body, name="adamw", grid=(r // tr,),
        in_specs=[blk] * 4, out_specs=[blk] * 3,
        out_shape=[jax.ShapeDtypeStruct((r, c), F32)] * 3,
        compiler_params=_cparams("parallel"),
    )(w, g, m, v)


SB_G = N_HEADS_B


def _sb_weights(qs, k, carry, tri, diag):
    z = _mm(qs, k, NT)
    sp = jnp.log(1.0 + jnp.exp(-jnp.abs(z)))
    ls_pos = jnp.minimum(z, 0.0) - sp
    ls_neg = jnp.minimum(-z, 0.0) - sp
    earlier = (lax.broadcasted_iota(jnp.int32, z.shape, 1) < lax.broadcasted_iota(jnp.int32, z.shape, 0)) if diag else None
    log_keep = jnp.where(earlier, ls_neg, 0.0) if diag else ls_neg
    hi = log_keep.astype(BF16)
    lo = (log_keep - hi.astype(F32)).astype(BF16)
    within = lax.dot_general(jnp.concatenate([hi, lo], axis=1), tri, NN, preferred_element_type=F32)
    arg = ls_pos + within + carry
    wts = jnp.where(earlier, jnp.exp(jnp.where(earlier, arg, 0.0)), 0.0) if diag else jnp.exp(arg)
    return ls_pos, ls_neg, log_keep, wts, earlier


def _sb_tile(q, k, v, carry, tri, diag):
    _, _, log_keep, wts, _ = _sb_weights(q * (HEAD_DIM ** -0.5), k, carry, tri, diag)
    return _mm(wts, v), jnp.sum(log_keep, axis=1, keepdims=True)


def _sb_tile_grads(q, k, v, carry, do, dtot, tri, diag):
    qs = q * (HEAD_DIM ** -0.5)
    ls_pos, ls_neg, _, wts, earlier = _sb_weights(qs, k, carry, tri, diag)
    dv = _mm(wts, do, TN)
    darg = _mm(do, v, NT) * wts
    dkeep = _mm(darg, tri[:BLOCK], NT) + dtot
    if diag:
        dkeep = jnp.where(earlier, dkeep, 0.0)
    dz = darg * jnp.exp(ls_neg) - dkeep * jnp.exp(ls_pos)
    return _mm(dz, k) * (HEAD_DIM ** -0.5), _mm(dz, qs, TN), dv, jnp.sum(darg, axis=1, keepdims=True)


_sb_tiles_diag = jax.vmap(functools.partial(_sb_tile, diag=True), in_axes=(0, 0, 0, 0, None))
_sb_tiles_off = jax.vmap(functools.partial(_sb_tile, diag=False), in_axes=(0, 0, 0, 0, None))
_sb_grads_diag = jax.vmap(functools.partial(_sb_tile_grads, diag=True), in_axes=(0, 0, 0, 0, 0, 0, None))
_sb_grads_off = jax.vmap(functools.partial(_sb_tile_grads, diag=False), in_axes=(0, 0, 0, 0, 0, 0, None))


def _sb_tri():
    r = lax.broadcasted_iota(jnp.int32, (2 * BLOCK, BLOCK), 0) % BLOCK
    c = lax.broadcasted_iota(jnp.int32, (2 * BLOCK, BLOCK), 1)
    return jnp.where(r > c, 1.0, 0.0).astype(BF16)


SB_SEQ = 2


def _sb_specs(t, nq):
    cb = COL_B // B_WIDTH
    sq = SB_SEQ
    q = pl.BlockSpec((sq, BLOCK, B_WIDTH), lambda b, i: (b, i, cb))
    k = pl.BlockSpec((sq, t, B_WIDTH), lambda b, i: (b, 0, cb + 1))
    v = pl.BlockSpec((sq, t, B_WIDTH), lambda b, i: (b, 0, cb + 2))
    blk = pl.BlockSpec((sq, BLOCK, B_WIDTH), lambda b, i: (b, i, 0))
    full = pl.BlockSpec((sq, t, B_WIDTH), lambda b, i: (b, 0, 0))
    carry = pl.BlockSpec((sq, 1, nq, BLOCK, SB_G), lambda b, i: (b, i, 0, 0, 0))
    return q, k, v, blk, full, carry


def _sb_heads(ref, rows):
    return jnp.concatenate([_heads(ref[b, rows, :], SB_G) for b in range(SB_SEQ)])


def _sb_unheads(a):
    return [_unheads(a[b * SB_G:(b + 1) * SB_G]) for b in range(SB_SEQ)]


def sb_fwd(proj, nseq):
    n = proj.shape[0]
    t = n // nseq
    nq = t // BLOCK
    g, sq = SB_G, SB_SEQ
    everything = pl.ds(0, BLOCK)

    def body(q_ref, k_ref, v_ref, o_ref, carry_ref):
        i = pl.program_id(1)
        tri = _sb_tri()
        qv = _sb_heads(q_ref, everything)

        def tile(j, c, fn):
            rows = pl.ds(pl.multiple_of(j * BLOCK, BLOCK), BLOCK)
            for b in range(sq):
                carry_ref[b, 0, j] = jnp.concatenate([c[b * g + h] for h in range(g)], axis=1)
            return fn(qv, _sb_heads(k_ref, rows), _sb_heads(v_ref, rows), c, tri)

        def step(it, st):
            o_acc, c = st
            o, tot = tile(i - 1 - it, c, _sb_tiles_off)
            return o_acc + o, c + tot

        o_acc, _ = lax.fori_loop(0, i, step, tile(i, jnp.zeros((sq * g, BLOCK, 1), F32), _sb_tiles_diag))
        for b, o in enumerate(_sb_unheads(o_acc)):
            o_ref[b] = o

    q, k, v, blk, _, carry = _sb_specs(t, nq)
    proj3 = proj.reshape(nseq, t, -1)
    o, carries = pl.pallas_call(
        body, name="sb_fwd", grid=(nseq // sq, nq),
        in_specs=[q, k, v],
        out_specs=[blk, carry],
        out_shape=[jax.ShapeDtypeStruct((nseq, t, B_WIDTH), F32),
                   jax.ShapeDtypeStruct((nseq, nq, nq, BLOCK, g), F32)],
        compiler_params=_cparams("parallel", "arbitrary"),
    )(proj3, proj3, proj3)
    return o.reshape(n, B_WIDTH), carries


def sb_bwd(proj, carries, do, nseq):
    n = proj.shape[0]
    t = n // nseq
    nq = t // BLOCK
    g, sq = SB_G, SB_SEQ
    everything = pl.ds(0, BLOCK)

    def body(q_ref, k_ref, v_ref, carry_ref, do_ref, dq_ref, dk_ref, dv_ref):
        i = pl.program_id(1)

        @pl.when(i == 0)
        def _():
            dk_ref[...] = jnp.zeros_like(dk_ref)
            dv_ref[...] = jnp.zeros_like(dv_ref)

        tri = _sb_tri()
        qv = _sb_heads(q_ref, everything)
        dov = _sb_heads(do_ref, everything)

        def tile(j, st, fn):
            dq_acc, dc = st
            rows = pl.ds(pl.multiple_of(j * BLOCK, BLOCK), BLOCK)
            cj = [carry_ref[b, 0, j] for b in range(sq)]
            dq, dk, dv, dcj = fn(qv, _sb_heads(k_ref, rows), _sb_heads(v_ref, rows),
                                 jnp.stack([cj[b][:, h:h + 1] for b in range(sq) for h in range(g)]), dov, dc, tri)
            for b, (dkb, dvb) in enumerate(zip(_sb_unheads(dk), _sb_unheads(dv))):
                dk_ref[b, rows, :] += dkb
                dv_ref[b, rows, :] += dvb
            return dq_acc + dq, dc + dcj

        st = lax.fori_loop(0, i, lambda j, st: tile(j, st, _sb_grads_off),
                           (jnp.zeros((sq * g, BLOCK, HEAD_DIM), F32), jnp.zeros((sq * g, BLOCK, 1), F32)))
        dq_acc, _ = tile(i, st, _sb_grads_diag)
        for b, dq in enumerate(_sb_unheads(dq_acc)):
            dq_ref[b] = dq

    q, k, v, blk, full, carry = _sb_specs(t, nq)
    proj3, do3 = proj.reshape(nseq, t, -1), do.reshape(nseq, t, -1)
    grads = pl.pallas_call(
        body, name="sb_bwd", grid=(nseq // sq, nq),
        in_specs=[q, k, v, carry, blk],
        out_specs=[blk, full, full],
        out_shape=[jax.ShapeDtypeStruct((nseq, t, B_WIDTH), F32)] * 3,
        compiler_params=_cparams("parallel", "arbitrary"),
    )(proj3, proj3, proj3, carries, do3)
    return [a.reshape(n, B_WIDTH) for a in grads]


def _unit_lower_inverse(a):
    n = a.shape[0]
    eye = jnp.where(lax.broadcasted_iota(jnp.int32, (n, n), 0) == lax.broadcasted_iota(jnp.int32, (n, n), 1), 1.0, 0.0)
    tmat = eye.astype(F32) - a
    p = a
    for _ in range(5):
        p = _mm32(p, p)
        tmat = tmat + _mm32(tmat, p)
    return tmat


@jax.custom_vjp
def _known_inverse(a, tmat):
    return tmat


def _known_inverse_fwd(a, tmat):
    return tmat, tmat


def _known_inverse_bwd(tmat, g):
    return -_mm32(_mm32(tmat, g, TN), tmat, NT), jnp.zeros_like(tmat)


_known_inverse.defvjp(_known_inverse_fwd, _known_inverse_bwd)


def _gdn_chunk(q, k, v, al_c, al_r, br_c, alog, dtb, nw, s, tmat_in):
    c = GDN_CHUNK
    ri = lax.broadcasted_iota(jnp.int32, (c, c), 0)
    ci = lax.broadcasted_iota(jnp.int32, (c, c), 1)
    incl, strict = ri >= ci, ri > ci
    rate = -jnp.exp(alog)
    g_c = rate * _softplus(al_c + dtb)
    g_r = rate * _softplus(al_r + dtb)
    beta = _sigmoid(br_c)
    gc_c = jnp.sum(jnp.where(incl, g_r, 0.0), axis=1, keepdims=True)
    gc_r = jnp.sum(jnp.where(ri <= ci, g_c, 0.0), axis=0, keepdims=True)
    gl = jnp.sum(g_r, axis=1, keepdims=True)
    decay = jnp.where(incl, jnp.exp(jnp.where(incl, gc_c - gc_r, 0.0)), 0.0)
    qn = q * lax.rsqrt(jnp.sum(q * q, axis=-1, keepdims=True) + RMS_EPS) * (HEAD_DIM ** -0.5)
    kn = k * lax.rsqrt(jnp.sum(k * k, axis=-1, keepdims=True) + RMS_EPS)
    kb = kn * beta
    a = jnp.where(strict, _mm(kb, kn, NT) * decay, 0.0)
    tmat = _unit_lower_inverse(a) if tmat_in is None else _known_inverse(a, tmat_in)
    u = _mm(tmat, v * beta)
    w = _mm(tmat, kb * jnp.exp(gc_c))
    qk = _mm(qn, kn, NT) * decay
    v_new = u - _mm(w, s)
    o = _mm(qn * jnp.exp(gc_c), s) + _mm(qk, v_new)
    s_new = s * jnp.exp(gl) + _mm(kn * jnp.exp(gl - gc_c), v_new, TN)
    o = o * lax.rsqrt(jnp.mean(o * o, axis=-1, keepdims=True) + RMS_EPS) * nw
    return o, s_new, tmat


_gdn_chunks_fwd = jax.vmap(functools.partial(_gdn_chunk, tmat_in=None), in_axes=(0, 0, 0, 0, 0, 0, 0, 0, None, 0))
_gdn_chunks_bwd = jax.vmap(_gdn_chunk, in_axes=(0, 0, 0, 0, 0, 0, 0, 0, None, 0, 0))

GDN_TB = 256
GDN_SEQ_FWD = 2
GDN_SEQ_BWD = 2


def _gdn_block(q3, k3, v3, ba, alog, dtb, nw, s, tm=None):
    nh = N_HEADS_A
    ns = q3.shape[0]
    bat = [ba[b].T for b in range(ns)]
    br_c = jnp.stack([ba[b][:, h:h + 1] for b in range(ns) for h in range(nh)])
    al_c = jnp.stack([ba[b][:, nh + h:nh + h + 1] for b in range(ns) for h in range(nh)])
    al_r = jnp.stack([bat[b][nh + h:nh + h + 1, :] for b in range(ns) for h in range(nh)])
    heads = lambda a: jnp.concatenate([_heads(a[b], nh) for b in range(ns)])
    args = (heads(q3), heads(k3), heads(v3), al_c, al_r, br_c, jnp.concatenate([alog] * ns), jnp.concatenate([dtb] * ns), nw, s)
    o, s_new, tmat = _gdn_chunks_fwd(*args) if tm is None else _gdn_chunks_bwd(*args, tm)
    o3 = jnp.stack([_unheads(o[b * nh:(b + 1) * nh]) for b in range(ns)])
    return (o3, s_new, tmat) if tm is None else (o3, s_new)


def _gdn_specs(nt, sq, rev):
    tpos = (lambda i: nt - 1 - i) if rev else (lambda i: i)
    ncb = GDN_TB // GDN_CHUNK
    qkv = [pl.BlockSpec((sq, GDN_TB, A_WIDTH), lambda b, i, j=j: (b, tpos(i), j)) for j in range(3)]
    ba = pl.BlockSpec((sq, GDN_TB, BLOCK), lambda b, i: (b, tpos(i), COL_BA // BLOCK))
    one = pl.BlockSpec((N_HEADS_A, 1, 1), lambda b, i: (0, 0, 0))
    vec = pl.BlockSpec((1, HEAD_DIM), lambda b, i: (0, 0))
    st = pl.BlockSpec((sq, ncb, N_HEADS_A, HEAD_DIM, HEAD_DIM), lambda b, i: (b, tpos(i), 0, 0, 0))
    oa = pl.BlockSpec((sq, GDN_TB, A_WIDTH), lambda b, i: (b, tpos(i), 0))
    return qkv, ba, one, vec, st, oa, tpos


def gdn_fwd(ya, proj, alog, dtb, nw, nseq):
    n = ya.shape[0]
    t = n // nseq
    nc, nt, ncb = t // GDN_CHUNK, t // GDN_TB, GDN_TB // GDN_CHUNK
    sq = GDN_SEQ_FWD
    nh = N_HEADS_A

    def body(q_ref, k_ref, v_ref, ba_ref, alog_ref, dtb_ref, nw_ref, o_ref, st_ref, tm_ref, s_s):
        @pl.when(pl.program_id(1) == 0)
        def _():
            s_s[...] = jnp.zeros_like(s_s)

        def step(c, s):
            rows = pl.ds(pl.multiple_of(c * GDN_CHUNK, GDN_CHUNK), GDN_CHUNK)
            o, s_new, tmat = _gdn_block(q_ref[:, rows, :], k_ref[:, rows, :], v_ref[:, rows, :], ba_ref[:, rows, :],
                                        alog_ref[...], dtb_ref[...], nw_ref[...], s)
            for b in range(sq):
                st_ref[b, c] = s[b * nh:(b + 1) * nh]
                tm_ref[b, c] = tmat[b * nh:(b + 1) * nh]
            o_ref[:, rows, :] = o
            return s_new

        s_s[...] = lax.fori_loop(0, ncb, step, s_s[...])

    qkv, ba, one, vec, st, oa, _ = _gdn_specs(nt, sq, False)
    ya3, proj3 = ya.reshape(nseq, t, -1), proj.reshape(nseq, t, -1)
    per_chunk = jax.ShapeDtypeStruct((nseq, nc, nh, HEAD_DIM, HEAD_DIM), F32)
    o, states, inverses = pl.pallas_call(
        body, name="gdn_fwd", grid=(nseq // sq, nt),
        in_specs=qkv + [ba, one, one, vec],
        out_specs=[oa, st, st],
        out_shape=[jax.ShapeDtypeStruct((nseq, t, A_WIDTH), F32), per_chunk, per_chunk],
        scratch_shapes=[pltpu.VMEM((sq * nh, HEAD_DIM, HEAD_DIM), F32)],
        compiler_params=_cparams("parallel", "arbitrary"),
    )(ya3, ya3, ya3, proj3, alog, dtb, nw)
    return o.reshape(n, A_WIDTH), states, inverses


def gdn_bwd(ya, proj, alog, dtb, nw, states, inverses, do, nseq):
    n = ya.shape[0]
    t = n // nseq
    nt, ncb = t // GDN_TB, GDN_TB // GDN_CHUNK
    nh = N_HEADS_A
    sq = GDN_SEQ_BWD

    def body(q_ref, k_ref, v_ref, ba_ref, alog_ref, dtb_ref, nw_ref, st_ref, tm_ref, do_ref,
             dya_ref, dba_ref, dalog_ref, ddtb_ref, dnw_ref, ds_s):
        @pl.when(pl.program_id(1) == 0)
        def _():
            ds_s[...] = jnp.zeros_like(ds_s)
            dalog_ref[...] = jnp.zeros_like(dalog_ref)
            ddtb_ref[...] = jnp.zeros_like(ddtb_ref)
            dnw_ref[...] = jnp.zeros_like(dnw_ref)

        def step(it, carry):
            ds, dalog, ddtb, dnw = carry
            c = ncb - 1 - it
            rows = pl.ds(pl.multiple_of(c * GDN_CHUNK, GDN_CHUNK), GDN_CHUNK)
            s_in = jnp.concatenate([st_ref[b, c] for b in range(sq)])
            tm_in = jnp.concatenate([tm_ref[b, c] for b in range(sq)])
            _, vjp = jax.vjp(functools.partial(_gdn_block, tm=tm_in), q_ref[:, rows, :], k_ref[:, rows, :], v_ref[:, rows, :],
                             ba_ref[:, rows, :], alog_ref[...], dtb_ref[...], nw_ref[...], s_in)
            dq, dk, dv, dba, da, dd, dn, ds = vjp((do_ref[:, rows, :], ds))
            dya_ref[:, rows, 0:A_WIDTH] = dq
            dya_ref[:, rows, A_WIDTH:2 * A_WIDTH] = dk
            dya_ref[:, rows, 2 * A_WIDTH:3 * A_WIDTH] = dv
            dba_ref[:, rows, :] = dba
            return ds, dalog + da, ddtb + dd, dnw + dn

        z11 = jnp.zeros((nh, 1, 1), F32)
        ds, dalog, ddtb, dnw = lax.fori_loop(0, ncb, step, (ds_s[...], z11, z11, jnp.zeros((1, HEAD_DIM), F32)))
        ds_s[...] = ds
        dalog_ref[0] += dalog
        ddtb_ref[0] += ddtb
        dnw_ref[0] += dnw

    qkv, ba, one, vec, st, oa, tpos = _gdn_specs(nt, sq, True)
    per_grp = pl.BlockSpec((1, nh, 1, 1), lambda b, i: (b, 0, 0, 0))
    sd = jax.ShapeDtypeStruct
    ya3, proj3, do3 = ya.reshape(nseq, t, -1), proj.reshape(nseq, t, -1), do.reshape(nseq, t, -1)
    dya, dba, dalog, ddtb, dnw = pl.pallas_call(
        body, name="gdn_bwd", grid=(nseq // sq, nt),
        in_specs=qkv + [ba, one, one, vec, st, st, oa],
        out_specs=[pl.BlockSpec((sq, GDN_TB, 3 * A_WIDTH), lambda b, i: (b, tpos(i), 0)),
                   pl.BlockSpec((sq, GDN_TB, BLOCK), lambda b, i: (b, tpos(i), 0)),
                   per_grp, per_grp, pl.BlockSpec((1, 1, HEAD_DIM), lambda b, i: (b, 0, 0))],
        out_shape=[sd((nseq, t, 3 * A_WIDTH), F32), sd((nseq, t, BLOCK), F32), sd((nseq // sq, nh, 1, 1), F32),
                   sd((nseq // sq, nh, 1, 1), F32), sd((nseq // sq, 1, HEAD_DIM), F32)],
        scratch_shapes=[pltpu.VMEM((sq * nh, HEAD_DIM, HEAD_DIM), F32)],
        compiler_params=_cparams("parallel", "arbitrary"),
    )(ya3, ya3, ya3, proj3, alog, dtb, nw, states, inverses, do3)
    return dya.reshape(n, 3 * A_WIDTH), dba.reshape(n, BLOCK), dalog, ddtb, dnw


DIL_NB = tuple((SEQ // d) // BLOCK for _, d in DILATED_PAIRS)
DIL_D = tuple(d for _, d in DILATED_PAIRS)
DIL_STEPS = tuple(w // d for w, d in DILATED_PAIRS)
DIL_B = 8
PAIR = 2 * HEAD_DIM


def _rope_tables(t):
    half = ROPE_DIM // 2
    inv_freq = ROPE_THETA ** (-jnp.arange(half, dtype=F32) / half)
    ang = jnp.arange(t, dtype=F32)[:, None] * inv_freq[None, :]
    ones = jnp.ones((t, HEAD_DIM - ROPE_DIM), F32)
    cs = jnp.concatenate([jnp.cos(ang), jnp.cos(ang), ones], axis=1)
    sn = jnp.concatenate([jnp.sin(ang), jnp.sin(ang), 0.0 * ones], axis=1)
    i = jnp.arange(PAIR)[:, None]
    j = jnp.arange(PAIR)[None, :]
    same = (i // HEAD_DIM) == (j // HEAD_DIM)
    ih, jh = i % HEAD_DIM, j % HEAD_DIM
    pm = (jnp.where(same & (jh < half) & (ih == jh + half), -1.0, 0.0)
          + jnp.where(same & (jh >= half) & (jh < ROPE_DIM) & (ih == jh - half), 1.0, 0.0))
    mean = jnp.where(same, 1.0 / HEAD_DIM, 0.0)
    twice = lambda m: jnp.concatenate([m, m]).astype(BF16)
    return jnp.tile(cs, (1, 2)), jnp.tile(sn, (1, 2)), twice(mean), twice(pm)


def _split_dot(x, w2):
    hi = x.astype(BF16)
    lo = lax.stop_gradient(x - hi.astype(F32)).astype(BF16)
    return lax.dot_general(jnp.concatenate([hi, lo], axis=1), w2, NN, preferred_element_type=F32)


def _dil_prep(x, w, cs, sn, mean2, pm2):
    y = x * lax.rsqrt(_split_dot(x * x, mean2) + RMS_EPS) * w
    return y * cs + _split_dot(y, pm2) * sn


def _dil_tile(qn, kk, vv, bias):
    lane = lax.broadcasted_iota(jnp.int32, (1, PAIR), 1)
    outs, lses = [], []
    for h in range(2):
        s = _mm(jnp.where(lane // HEAD_DIM == h, qn, 0.0) * (HEAD_DIM ** -0.5), kk, NT) + bias
        m = lax.stop_gradient(jnp.max(s, axis=-1, keepdims=True))
        p = jnp.exp(s - m)
        denom = jnp.sum(p, axis=-1, keepdims=True)
        outs.append(_mm(p, vv) / denom)
        lses.append(m + jnp.log(denom))
    return jnp.where(lane < HEAD_DIM, outs[0], outs[1]), jnp.concatenate(lses, axis=1)


_dil_tiles = jax.vmap(_dil_tile)


def _spread(a):
    lane = lax.broadcasted_iota(jnp.int32, (a.shape[0], PAIR), 1)
    return jnp.where(lane < HEAD_DIM, a[:, 0:1], a[:, 1:2])


def _dil_mix(o1, o2, o3, l1, l2, l3):
    m = lax.stop_gradient(jnp.maximum(jnp.maximum(l1, l2), l3))
    e1, e2, e3 = jnp.exp(l1 - m), jnp.exp(l2 - m), jnp.exp(l3 - m)
    r = 1.0 / (e1 + e2 + e3)
    return _spread(e1 * r) * o1 + _spread(e2 * r) * o2 + _spread(e3 * r) * o3


def _dil_fill_biases(bias_s):
    steps, = set(DIL_STEPS)
    qi = lax.broadcasted_iota(jnp.int32, (BLOCK, 1), 0)
    kj = lax.broadcasted_iota(jnp.int32, (1, 2 * BLOCK), 1)
    rel = qi - kj + BLOCK
    inside = (rel >= 0) & (rel <= steps)
    bias_s[0] = jnp.where(inside, 0.0, NEG)
    bias_s[1] = jnp.where(inside & (kj >= BLOCK), 0.0, NEG)
    bias_s[2] = jnp.where((qi >= kj) & (qi - kj <= steps), 0.0, NEG)


def _dil_mask(it, g, bias_s):
    qrows = pl.ds(pl.multiple_of(it * BLOCK, BLOCK), BLOCK)
    if DIL_NB[g] == 1:
        return bias_s[2, :, 0:BLOCK], qrows, qrows
    which = jnp.where(it == 0, 2, jnp.where(it % DIL_NB[g] == 0, 1, 0))
    kstart = jnp.maximum(it - 1, 0) * BLOCK
    return bias_s[which], qrows, pl.ds(pl.multiple_of(kstart, BLOCK), 2 * BLOCK)


def _dil_gather(src, dst, d):
    t = src.shape[0]
    ln = t // d
    for r in range(d):
        dst[pl.ds(r * ln, ln), :] = src[pl.ds(r, ln, stride=d), :]


def _dil_scatter(src, dst, d):
    t = src.shape[0]
    ln = t // d
    for r in range(d):
        dst[pl.ds(r, ln, stride=d), :] = src[pl.ds(r * ln, ln), :]


def _dil_forward_parts(q_ref, k_ref, v_ref, qw, kw, cs_ref, sn_ref, mean2, pm2, qn_s, kn_s, dl_s, od_s, ld_s, on_s, ln_s, bias_s):
    t = qn_s.shape[0]
    _dil_fill_biases(bias_s)

    def prep(c, _):
        rows = pl.ds(pl.multiple_of(c * ROWS, ROWS), ROWS)
        qn_s[rows, :] = _dil_prep(q_ref[rows, :], qw, cs_ref[rows, :], sn_ref[rows, :], mean2, pm2)
        kn_s[rows, :] = _dil_prep(k_ref[rows, :], kw, cs_ref[rows, :], sn_ref[rows, :], mean2, pm2)
        return 0

    lax.fori_loop(0, t // ROWS, prep, 0)
    for g in (1, 2):
        _dil_gather(qn_s, dl_s.at[g - 1, 0], DIL_D[g])
        _dil_gather(kn_s, dl_s.at[g - 1, 1], DIL_D[g])
        _dil_gather(v_ref, dl_s.at[g - 1, 2], DIL_D[g])
    for g in range(3):
        qs = qn_s if g == 0 else dl_s.at[g - 1, 0]
        ks = kn_s if g == 0 else dl_s.at[g - 1, 1]
        vs = v_ref if g == 0 else dl_s.at[g - 1, 2]

        def tiles(i, _, g=g, qs=qs, ks=ks, vs=vs):
            where = [_dil_mask(i * DIL_B + b, g, bias_s) for b in range(DIL_B)]
            o, lse = _dil_tiles(jnp.stack([qs[qr, :] for _, qr, _ in where]), jnp.stack([ks[kr, :] for _, _, kr in where]),
                                jnp.stack([vs[kr, :] for _, _, kr in where]), jnp.stack([m for m, _, _ in where]))
            for b, (_, qr, _) in enumerate(where):
                od_s[g, qr, :] = o[b]
                ld_s[g, qr, :] = lse[b]
            return 0

        lax.fori_loop(0, t // BLOCK // DIL_B, tiles, 0)
    for g in (1, 2):
        _dil_scatter(od_s.at[g], on_s.at[g - 1], DIL_D[g])
        _dil_scatter(ld_s.at[g], ln_s.at[g - 1], DIL_D[g])


def _dil_scratch(t):
    return [pltpu.VMEM((t, PAIR), F32), pltpu.VMEM((t, PAIR), F32),
            pltpu.VMEM((2, 3, t, PAIR), F32),
            pltpu.VMEM((3, t, PAIR), F32), pltpu.VMEM((3, t, 2), F32),
            pltpu.VMEM((2, t, PAIR), F32), pltpu.VMEM((2, t, 2), F32),
            pltpu.VMEM((3, BLOCK, 2 * BLOCK), F32)]


def _dil_specs(t):
    cb = COL_C // BLOCK
    per = C_WIDTH // BLOCK
    qkv = [pl.BlockSpec((t, BLOCK), lambda b, p, j=j: (b, cb + j * per + p)) for j in range(3)]
    vec = pl.BlockSpec((1, PAIR), lambda b, p: (0, 0))
    tab = pl.BlockSpec((t, PAIR), lambda b, p: (0, 0))
    mat = pl.BlockSpec((2 * PAIR, PAIR), lambda b, p: (0, 0))
    pair = pl.BlockSpec((t, BLOCK), lambda b, p: (b, p))
    return qkv, vec, tab, mat, pair


def dil_fwd(proj, qw, kw, cs, sn, mean2, pm2, nseq):
    n = proj.shape[0]
    t = n // nseq

    def body(q_ref, k_ref, v_ref, qw_ref, kw_ref, cs_ref, sn_ref, mean_ref, pm_ref, o_ref,
             qn_s, kn_s, dl_s, od_s, ld_s, on_s, ln_s, bias_s):
        _dil_forward_parts(q_ref, k_ref, v_ref, qw_ref[...], kw_ref[...], cs_ref, sn_ref, mean_ref[...], pm_ref[...],
                           qn_s, kn_s, dl_s, od_s, ld_s, on_s, ln_s, bias_s)

        def mix(c, _):
            rows = pl.ds(pl.multiple_of(c * ROWS, ROWS), ROWS)
            o_ref[rows, :] = _dil_mix(od_s[0, rows, :], on_s[0, rows, :], on_s[1, rows, :],
                                      ld_s[0, rows, :], ln_s[0, rows, :], ln_s[1, rows, :])
            return 0

        lax.fori_loop(0, t // ROWS, mix, 0)

    qkv, vec, tab, mat, pair = _dil_specs(t)
    return pl.pallas_call(
        body, name="dil_fwd", grid=(nseq, C_WIDTH // BLOCK),
        in_specs=qkv + [vec, vec, tab, tab, mat, mat],
        out_specs=pair,
        out_shape=jax.ShapeDtypeStruct((n, C_WIDTH), F32),
        scratch_shapes=_dil_scratch(t),
        compiler_params=_cparams("parallel", "parallel"),
    )(proj, proj, proj, qw, kw, cs, sn, mean2, pm2)


def dil_bwd(proj, qw, kw, cs, sn, mean2, pm2, do, nseq):
    n = proj.shape[0]
    t = n // nseq

    def body(q_ref, k_ref, v_ref, qw_ref, kw_ref, cs_ref, sn_ref, mean_ref, pm_ref, do_ref,
             dq_ref, dk_ref, dv_ref, dqw_ref, dkw_ref,
             qn_s, kn_s, dl_s, od_s, ld_s, on_s, ln_s, bias_s, tq_s, tk_s, tv_s):
        qw, kw, mean2, pm2 = qw_ref[...], kw_ref[...], mean_ref[...], pm_ref[...]
        _dil_forward_parts(q_ref, k_ref, v_ref, qw, kw, cs_ref, sn_ref, mean2, pm2, qn_s, kn_s, dl_s, od_s, ld_s, on_s, ln_s, bias_s)

        def mix(c, _):
            rows = pl.ds(pl.multiple_of(c * ROWS, ROWS), ROWS)
            _, vjp = jax.vjp(_dil_mix, od_s[0, rows, :], on_s[0, rows, :], on_s[1, rows, :],
                             ld_s[0, rows, :], ln_s[0, rows, :], ln_s[1, rows, :])
            d1, d2, d3, e1, e2, e3 = vjp(do_ref[rows, :])
            od_s[0, rows, :] = d1
            on_s[0, rows, :] = d2
            on_s[1, rows, :] = d3
            ld_s[0, rows, :] = e1
            ln_s[0, rows, :] = e2
            ln_s[1, rows, :] = e3
            return 0

        lax.fori_loop(0, t // ROWS, mix, 0)
        for g in (1, 2):
            _dil_gather(on_s.at[g - 1], od_s.at[g], DIL_D[g])
            _dil_gather(ln_s.at[g - 1], ld_s.at[g], DIL_D[g])
        on_s[...] = jnp.zeros_like(on_s)
        dv_ref[...] = jnp.zeros_like(dv_ref)
        for g in range(3):
            qs = qn_s if g == 0 else dl_s.at[g - 1, 0]
            ks = kn_s if g == 0 else dl_s.at[g - 1, 1]
            vs = v_ref if g == 0 else dl_s.at[g - 1, 2]
            gq = on_s.at[0] if g == 0 else tq_s
            gk = on_s.at[1] if g == 0 else tk_s
            gv = dv_ref if g == 0 else tv_s
            if g > 0:
                tk_s[...] = jnp.zeros_like(tk_s)
                tv_s[...] = jnp.zeros_like(tv_s)

            def tiles(i, _, g=g, qs=qs, ks=ks, vs=vs, gq=gq, gk=gk, gv=gv):
                where = [_dil_mask(i * DIL_B + b, g, bias_s) for b in range(DIL_B)]
                biases = jnp.stack([m for m, _, _ in where])
                _, vjp = jax.vjp(lambda q_, k_, v_: _dil_tiles(q_, k_, v_, biases),
                                 jnp.stack([qs[qr, :] for _, qr, _ in where]), jnp.stack([ks[kr, :] for _, _, kr in where]),
                                 jnp.stack([vs[kr, :] for _, _, kr in where]))
                dq, dkk, dvv = vjp((jnp.stack([od_s[g, qr, :] for _, qr, _ in where]),
                                    jnp.stack([ld_s[g, qr, :] for _, qr, _ in where])))
                for b, (_, qr, kr) in enumerate(where):
                    gq[qr, :] = dq[b]
                    gk[kr, :] += dkk[b]
                    gv[kr, :] += dvv[b]
                return 0

            lax.fori_loop(0, t // BLOCK // DIL_B, tiles, 0)
            if g > 0:
                d = DIL_D[g]
                ln = t // d
                for r in range(d):
                    nat, dil = pl.ds(r, ln, stride=d), pl.ds(r * ln, ln)
                    on_s[0, nat, :] += tq_s[dil, :]
                    on_s[1, nat, :] += tk_s[dil, :]
                    dv_ref[nat, :] += tv_s[dil, :]

        def prep(c, acc):
            rows = pl.ds(pl.multiple_of(c * ROWS, ROWS), ROWS)
            f = lambda x, w: _dil_prep(x, w, cs_ref[rows, :], sn_ref[rows, :], mean2, pm2)
            _, vq = jax.vjp(f, q_ref[rows, :], qw)
            _, vk = jax.vjp(f, k_ref[rows, :], kw)
            dq, dqw = vq(on_s[0, rows, :])
            dk, dkw = vk(on_s[1, rows, :])
            dq_ref[rows, :] = dq
            dk_ref[rows, :] = dk
            return acc[0] + dqw, acc[1] + dkw

        dqw, dkw = lax.fori_loop(0, t // ROWS, prep, (jnp.zeros((1, PAIR), F32), jnp.zeros((1, PAIR), F32)))
        dqw_ref[0] = dqw
        dkw_ref[0] = dkw

    qkv, vec, tab, mat, pair = _dil_specs(t)
    per = C_WIDTH // BLOCK
    wout = pl.BlockSpec((1, 1, PAIR), lambda b, p: (b * per + p, 0, 0))
    return pl.pallas_call(
        body, name="dil_bwd", grid=(nseq, per),
        in_specs=qkv + [vec, vec, tab, tab, mat, mat, pair],
        out_specs=[pair, pair, pair, wout, wout],
        out_shape=[jax.ShapeDtypeStruct((n, C_WIDTH), F32)] * 3 + [jax.ShapeDtypeStruct((nseq * per, 1, PAIR), F32)] * 2,
        scratch_shapes=_dil_scratch(t) + [pltpu.VMEM((t, PAIR), F32)] * 3,
        compiler_params=_cparams("parallel", "parallel"),
    )(proj, proj, proj, qw, kw, cs, sn, mean2, pm2, do)


N_CHIPS = 4
SUM_ROWS = 432
MESH_IDS = pl.DeviceIdType.MESH
ANY = pl.BlockSpec(memory_space=pl.ANY)


def plane_exchange(src, all_to_all):
    blk_shape = src.shape[1:] if all_to_all else src.shape

    def body(src_ref, out_ref, send_sems, recv_sems, local_sem):
        x, y, c = lax.axis_index("x"), lax.axis_index("y"), lax.axis_index("c")
        me = 2 * x + y
        mine = pltpu.make_async_copy(src_ref.at[me] if all_to_all else src_ref, out_ref.at[me], local_sem)
        mine.start()
        sends = []
        for k in (1, 2, 3):
            px = 1 - x if k & 2 else x
            py = 1 - y if k & 1 else y
            peer = 2 * px + py
            cp = pltpu.make_async_remote_copy(
                src_ref=src_ref.at[peer] if all_to_all else src_ref, dst_ref=out_ref.at[me],
                send_sem=send_sems.at[k - 1], recv_sem=recv_sems.at[k - 1],
                device_id=(px, py, c), device_id_type=MESH_IDS)
            cp.start()
            sends.append((cp, peer, (px, py, c)))
        for k, (cp, peer, dev) in enumerate(sends):
            pltpu.make_async_remote_copy(
                src_ref=out_ref.at[me], dst_ref=out_ref.at[peer],
                send_sem=send_sems.at[k], recv_sem=recv_sems.at[k],
                device_id=dev, device_id_type=MESH_IDS).wait_recv()
        for cp, _, _ in sends:
            cp.wait_send()
        mine.wait()

    return pl.pallas_call(
        body, name="plane_all_to_all" if all_to_all else "plane_all_gather",
        in_specs=[ANY], out_specs=ANY,
        out_shape=jax.ShapeDtypeStruct((N_CHIPS,) + blk_shape, src.dtype),
        scratch_shapes=[pltpu.SemaphoreType.DMA((3,)), pltpu.SemaphoreType.DMA((3,)), pltpu.SemaphoreType.DMA],
    )(src)


def sibling_swap(src, other_half=False):
    shape = (src.shape[0], src.shape[1] // 2) + src.shape[2:] if other_half else src.shape

    def body(src_ref, out_ref, send_sem, recv_sem):
        x, y, c = lax.axis_index("x"), lax.axis_index("y"), lax.axis_index("c")
        part = src_ref.at[:, pl.ds((1 - c) * shape[1], shape[1])] if other_half else src_ref
        cp = pltpu.make_async_remote_copy(src_ref=part, dst_ref=out_ref, send_sem=send_sem, recv_sem=recv_sem,
                                          device_id=(x, y, 1 - c), device_id_type=MESH_IDS)
        cp.start()
        cp.wait()

    return pl.pallas_call(
        body, name="sibling_swap", in_specs=[ANY], out_specs=ANY,
        out_shape=jax.ShapeDtypeStruct(shape, src.dtype),
        scratch_shapes=[pltpu.SemaphoreType.DMA, pltpu.SemaphoreType.DMA],
    )(src)


def sum4(a):
    _, r, c = a.shape
    tr = SUM_ROWS

    def body(a_ref, o_ref):
        p = [a_ref[i].astype(F32) for i in range(N_CHIPS)]
        o_ref[...] = (p[0] + p[1]) + (p[2] + p[3])

    return pl.pallas_call(
        body, name="sum4", grid=(r // tr,),
        in_specs=[pl.BlockSpec((N_CHIPS, tr, c), lambda i: (0, i, 0))],
        out_specs=pl.BlockSpec((tr, c), lambda i: (i, 0)),
        out_shape=jax.ShapeDtypeStruct((r, c), F32),
        compiler_params=_cparams("parallel"),
    )(a)


def add_my_half(mine, got, c):
    nchip, r2, cols = mine.shape
    nt = r2 // 2 // SUM_ROWS

    def body(c_ref, a_ref, b_ref, o_ref):
        o_ref[...] = (a_ref[...] + b_ref[...]).astype(BF16)

    blk = pl.BlockSpec((1, SUM_ROWS, cols), lambda j, i, c_ref: (j, i, 0))
    return pl.pallas_call(
        body, name="add_my_half",
        grid_spec=pltpu.PrefetchScalarGridSpec(
            num_scalar_prefetch=1, grid=(nchip, nt),
            in_specs=[pl.BlockSpec((1, SUM_ROWS, cols), lambda j, i, c_ref: (j, c_ref[0] * nt + i, 0)), blk],
            out_specs=blk),
        out_shape=jax.ShapeDtypeStruct((nchip, r2 // 2, cols), BF16),
        compiler_params=_cparams("parallel", "parallel"),
    )(jnp.reshape(c, (1,)).astype(jnp.int32), mine, got)


PACK_COLS = 1152
PACK_ROWS = 2592
ROW_TILE = 16


def _pack(parts):
    blocks = []
    for p in parts:
        p2 = p.reshape(-1, p.shape[-1])
        blocks.append(jnp.pad(p2, ((0, -p2.shape[0] % ROW_TILE), (0, PACK_COLS - p2.shape[1]))))
    rows = sum(b.shape[0] for b in blocks)
    blocks.append(jnp.zeros((PACK_ROWS - rows, PACK_COLS), blocks[0].dtype))
    return jnp.concatenate(blocks)


def _unpack(buf, shapes):
    out, at = [], 0
    for s in shapes:
        rows = math.prod(s[:-1])
        out.append(buf[at:at + rows, :s[-1]].reshape(s))
        at += rows + (-rows % ROW_TILE)
    return out


def _pack_small(g):
    blk = jnp.zeros((ROW_TILE, PACK_COLS), F32)
    for i, k in enumerate(SMALL):
        blk = blk.at[2 * i:2 * i + 2, :g[k].shape[1]].set(g[k])
    return blk


def _unpack_small(blk, shapes):
    return [blk[2 * i:2 * i + 2, :s[1]] for i, s in enumerate(shapes)]


def _layer_fwd(x, p, nseq, tabs):
    proj, hdn = inproj_fwd(x, p["norm_w"][None], p["w_in"])
    ya = conv_fwd(proj, p["conv_w"], nseq)
    oa, states, inverses = gdn_fwd(ya, proj, p["a_log"].reshape(N_HEADS_A, 1, 1), p["dt_bias"].reshape(N_HEADS_A, 1, 1),
                         p["gdn_norm_w"][None], nseq)
    ob, carries = sb_fwd(proj, nseq)
    oc = dil_fwd(proj, jnp.tile(p["q_norm_w"], 2)[None], jnp.tile(p["k_norm_w"], 2)[None], *tabs, nseq)
    y, mixed = outproj_fwd(x, oa, ob, oc, proj, p["w_out"])
    return y, dict(x=x, hdn=hdn, proj=proj, ya=ya, states=states, inverses=inverses, carries=carries, oa=oa, ob=ob, oc=oc, mixed=mixed)


def _layer_bwd(dy, p, res, nseq, tabs):
    proj = res["proj"]
    g = {}
    g["w_out"] = mat_tn(res["mixed"], [dy])[0]
    doa, dob, doc, dza, dzb, dzc = outproj_bwd(dy, res["oa"], res["ob"], res["oc"], proj, p["w_out"])
    dqc, dkc, dvc, dqw, dkw = dil_bwd(proj, jnp.tile(p["q_norm_w"], 2)[None], jnp.tile(p["k_norm_w"], 2)[None], *tabs, doc,
                                      nseq)
    g["q_norm_w"], g["k_norm_w"] = dqw.reshape(-1, HEAD_DIM).sum(0), dkw.reshape(-1, HEAD_DIM).sum(0)
    dqb, dkb, dvb = sb_bwd(proj, res["carries"], dob, nseq)
    dya, dba, dalog, ddtb, dnw = gdn_bwd(res["ya"], proj, p["a_log"].reshape(N_HEADS_A, 1, 1),
                                         p["dt_bias"].reshape(N_HEADS_A, 1, 1), p["gdn_norm_w"][None], res["states"], res["inverses"], doa,
                                         nseq)
    g["a_log"], g["dt_bias"], g["gdn_norm_w"] = dalog.sum(0).reshape(-1), ddtb.sum(0).reshape(-1), dnw.sum((0, 1))
    dqkv, dcw = conv_bwd(proj, p["conv_w"], dya, nseq)
    g["conv_w"] = dcw.sum(0)
    slabs = [dqkv, dza, dqc, dkc, dvc, dzc, dqb, dkb, dvb, dzb, dba]
    hdn = res["hdn"]
    g["w_in"] = jnp.concatenate(mat_tn(hdn, slabs[:6]) + mat_tn(hdn, slabs[6:]), axis=1)
    dx, dnw_tiles = inproj_bwd(slabs, p["w_in"], res["x"], p["norm_w"][None], dy)
    g["norm_w"] = dnw_tiles.sum((0, 1))
    return dx, g


SMALL = ("norm_w", "a_log", "dt_bias", "gdn_norm_w", "q_norm_w", "k_norm_w")


def _local_step(x, target, full):
    nseq, t, d = x.shape
    tabs = _rope_tables(t)
    h = x.reshape(nseq * t, d)
    saved = []
    for l in range(DEPTH):
        p = {k: v[l] for k, v in full.items()}
        h, res = _layer_fwd(h, p, nseq, tabs)
        saved.append((p, res))
    dy, parts = loss_fwd_bwd(h, target.reshape(nseq * t, d))
    loss = parts[:, 0, 0].sum()
    grads = [None] * DEPTH
    for l in reversed(range(DEPTH)):
        p, res = saved[l]
        dy, grads[l] = _layer_bwd(dy, p, res, nseq, tabs)
    return loss, dy.reshape(nseq, t, d), {k: jnp.stack([g[k] for g in grads]) for k in grads[0]}


def _pad_cols(w):
    b0 = ORIG_A + ORIG_BA
    c0 = b0 + ORIG_B
    zeros = jnp.zeros(w.shape[:-1] + (BLOCK - ORIG_BA,), w.dtype)
    return jnp.concatenate([w[..., :ORIG_A], w[..., c0:], w[..., b0:c0], w[..., ORIG_A:b0], zeros], axis=-1)


def _unpad_cols(w):
    return jnp.concatenate([w[..., :COL_C], w[..., COL_BA:COL_BA + ORIG_BA], w[..., COL_B:COL_BA], w[..., COL_C:COL_B]],
                           axis=-1)


def kernel(x, norm_w, w_in, conv_w, a_log, dt_bias, gdn_norm_w, q_norm_w, k_norm_w, w_out, loss_target, m_norm_w, m_w_in, m_conv_w, m_a_log, m_dt_bias, m_gdn_norm_w, m_q_norm_w, m_k_norm_w, m_w_out, v_norm_w, v_w_in, v_conv_w, v_a_log, v_dt_bias, v_gdn_norm_w, v_q_norm_w, v_k_norm_w, v_w_out):
    weights = dict(norm_w=norm_w, w_in=w_in, conv_w=conv_w, a_log=a_log, dt_bias=dt_bias, gdn_norm_w=gdn_norm_w,
                   q_norm_w=q_norm_w, k_norm_w=k_norm_w, w_out=w_out)
    moms = dict(norm_w=m_norm_w, w_in=m_w_in, conv_w=m_conv_w, a_log=m_a_log, dt_bias=m_dt_bias,
                gdn_norm_w=m_gdn_norm_w, q_norm_w=m_q_norm_w, k_norm_w=m_k_norm_w, w_out=m_w_out)
    vars_ = dict(norm_w=v_norm_w, w_in=v_w_in, conv_w=v_conv_w, a_log=v_a_log, dt_bias=v_dt_bias,
                 gdn_norm_w=v_gdn_norm_w, q_norm_w=v_q_norm_w, k_norm_w=v_k_norm_w, w_out=v_w_out)
    names = list(weights)
    sharded = ("w_in", "w_out", "conv_w")
    shard_shapes = [weights[k].shape for k in sharded]

    c = lax.axis_index("c")
    half = PACK_ROWS // 2
    conv_bits = lax.bitcast_convert_type(conv_w, BF16).reshape(conv_w.shape[:2] + (2 * conv_w.shape[2],))
    shard = _pack([w_in.astype(BF16), w_out.astype(BF16), conv_bits])
    mine = plane_exchange(lax.dynamic_slice_in_dim(shard, c * half, half, axis=0), all_to_all=False)
    other = sibling_swap(mine)
    got = jnp.concatenate([jnp.where(c == 0, mine, other), jnp.where(c == 0, other, mine)], axis=1)
    per_chip = [_unpack(got[i], shard_shapes[:2] + [conv_bits.shape]) for i in range(N_CHIPS)]
    full = {k: weights[k] for k in SMALL}
    full["w_in"] = _pad_cols(jnp.concatenate([pc[0] for pc in per_chip], axis=2))
    full["w_out"] = jnp.concatenate([pc[1] for pc in per_chip], axis=1)
    full["conv_w"] = jnp.concatenate(
        [lax.bitcast_convert_type(pc[2].reshape(conv_w.shape + (2,)), F32) for pc in per_chip], axis=2)

    loss, grad_x, g = _local_step(x, loss_target, full)

    gw_in = _unpad_cols(g["w_in"])
    cols, rows = w_in.shape[2], w_out.shape[1]
    small = _pack_small(g)
    send = jnp.stack([_pack([gw_in[:, :, i * cols:(i + 1) * cols], g["w_out"][:, i * rows:(i + 1) * rows],
                             g["conv_w"][:, :, i * conv_w.shape[2]:(i + 1) * conv_w.shape[2]], small])
                      for i in range(N_CHIPS)])
    chip_sum = add_my_half(send, sibling_swap(send, other_half=True), c)
    mine = sum4(plane_exchange(chip_sum, all_to_all=True))
    other = sibling_swap(mine)
    total = jnp.concatenate([jnp.where(c == 0, mine, other), jnp.where(c == 0, other, mine)])
    reduced = _unpack(total, shard_shapes + [(ROW_TILE, PACK_COLS)])
    grads = dict(zip(sharded, reduced[:3]))
    grads.update(zip(SMALL, _unpack_small(reduced[3], [weights[k].shape for k in SMALL])))
    loss = lax.psum(loss, ("x", "y", "c"))

    def two_d(a):
        return a.reshape(-1, a.shape[-1])

    delta, new_m, new_v = {}, {}, {}
    for k in names:
        d_, m_, v_ = adamw(two_d(weights[k]), two_d(grads[k]), two_d(moms[k]), two_d(vars_[k]))
        delta[k], new_m[k], new_v[k] = (a.reshape(weights[k].shape) for a in (d_, m_, v_))
    return (loss, grad_x, *[grads[k] for k in names], *[delta[k] for k in names],
            *[new_m[k] for k in names], *[new_v[k] for k in names])
```

```python
import functools
import math

import jax
import jax.numpy as jnp
from jax import lax
from jax.experimental import pallas as pl
from jax.experimental.pallas import tpu as pltpu

F32 = jnp.float32
BF16 = jnp.bfloat16

D_MODEL = 1024
SEQ = 2048
DEPTH = 2
HEAD_DIM = 64
N_HEADS_A, N_HEADS_B, N_HEADS_C = 6, 4, 6
A_WIDTH, B_WIDTH, C_WIDTH = N_HEADS_A * HEAD_DIM, N_HEADS_B * HEAD_DIM, N_HEADS_C * HEAD_DIM
CONV_WIDTH = 4
GDN_CHUNK = 64
BLOCK = 128
ROPE_DIM = 16
ROPE_THETA = 500000.0
DILATED_PAIRS = ((128, 1), (512, 4), (2048, 16))
RMS_EPS = 1e-6
NEG = -1e30

NT = (((1,), (1,)), ((), ()))
NN = (((1,), (0,)), ((), ()))
TN = (((0,), (0,)), ((), ()))

VMEM_LIMIT = 48 * 1024 * 1024

ORIG_A = 4 * A_WIDTH
ORIG_BA = 2 * N_HEADS_A
ORIG_B = 4 * B_WIDTH
COL_AZ = 3 * A_WIDTH
COL_C = 4 * A_WIDTH
COL_B = COL_C + 4 * C_WIDTH
COL_BA = COL_B + 4 * B_WIDTH
P_COLS = COL_BA + BLOCK
TN_COLS = 384
INPROJ_COLS = P_COLS // 3
TM_ROWS = 512
ROWS = 1024


def _mm(a, b, dims=NN):
    return lax.dot_general(a.astype(BF16), b.astype(BF16), dims, preferred_element_type=F32)


def _mm32(a, b, dims=NN):
    return lax.dot_general(a, b, dims, precision=lax.Precision.HIGH, preferred_element_type=F32)


def _cparams(*sem):
    return pltpu.CompilerParams(dimension_semantics=sem, vmem_limit_bytes=VMEM_LIMIT)


def _sigmoid(x):
    return 0.5 * (jnp.tanh(0.5 * x) + 1.0)


def _softplus(x):
    return jnp.maximum(x, 0.0) + jnp.log(1.0 + jnp.exp(-jnp.abs(x)))


def _rms(x, w):
    return x * lax.rsqrt(jnp.mean(x * x, axis=-1, keepdims=True) + RMS_EPS) * w


def _heads(a, n):
    return jnp.stack([a[:, h * HEAD_DIM:(h + 1) * HEAD_DIM] for h in range(n)])


def _unheads(a):
    return jnp.concatenate([a[h] for h in range(a.shape[0])], axis=1)


def inproj_fwd(x, nw, w):
    n, d = x.shape
    p = w.shape[1]

    def body(x_ref, nw_ref, w_ref, proj_ref, hdn_ref):
        @pl.when(pl.program_id(1) == 0)
        def _():
            hdn_ref[...] = _rms(x_ref[...], nw_ref[...]).astype(BF16)

        proj_ref[...] = jnp.dot(hdn_ref[...], w_ref[...], preferred_element_type=F32)

    return pl.pallas_call(
        body, name="inproj_fwd", grid=(n // TM_ROWS, p // INPROJ_COLS),
        in_specs=[pl.BlockSpec((TM_ROWS, d), lambda i, j: (i, 0)), pl.BlockSpec((1, d), lambda i, j: (0, 0)),
                  pl.BlockSpec((d, INPROJ_COLS), lambda i, j: (0, j))],
        out_specs=[pl.BlockSpec((TM_ROWS, INPROJ_COLS), lambda i, j: (i, j)), pl.BlockSpec((TM_ROWS, d), lambda i, j: (i, 0))],
        out_shape=[jax.ShapeDtypeStruct((n, p), F32), jax.ShapeDtypeStruct((n, d), BF16)],
        compiler_params=_cparams("parallel", "arbitrary"),
    )(x, nw, w)


def mat_tn(a, slabs):
    n, ka = a.shape
    ns = len(slabs)

    def body(*refs):
        a_ref, s_refs, o_refs = refs[0], refs[1:1 + ns], refs[1 + ns:]

        @pl.when(pl.program_id(0) == 0)
        def _():
            for o_ref in o_refs:
                o_ref[...] = jnp.zeros_like(o_ref)

        at = a_ref[...].T
        for s_ref, o_ref in zip(s_refs, o_refs):
            o_ref[...] += jnp.dot(at, s_ref[...].astype(BF16), preferred_element_type=F32)

    return pl.pallas_call(
        body, name="mat_tn", grid=(n // TM_ROWS,),
        in_specs=[pl.BlockSpec((TM_ROWS, ka), lambda k: (k, 0))]
                 + [pl.BlockSpec((TM_ROWS, s.shape[1]), lambda k: (k, 0)) for s in slabs],
        out_specs=[pl.BlockSpec((ka, s.shape[1]), lambda k: (0, 0)) for s in slabs],
        out_shape=[jax.ShapeDtypeStruct((ka, s.shape[1]), F32) for s in slabs],
        compiler_params=_cparams("arbitrary"),
    )(a, *slabs)


def inproj_bwd(slabs, w, x, nw, dy):
    n, d = x.shape
    p = w.shape[1]
    tm = 256
    ns = len(slabs)

    def body(*refs):
        s_refs = refs[:ns]
        w_ref, x_ref, nw_ref, dy_ref, dx_ref, dnw_ref = refs[ns:]
        dh = jnp.zeros((tm, d), F32)
        at = 0
        for s_ref in s_refs:
            wd = s_ref.shape[1]
            dh = dh + lax.dot_general(s_ref[...].astype(BF16), w_ref[:, at:at + wd], NT, preferred_element_type=F32)
            at += wd
        _, vjp = jax.vjp(_rms, x_ref[...], nw_ref[...])
        dx, dnw = vjp(dh)
        dx_ref[...] = dx + dy_ref[...]
        dnw_ref[0] = dnw

    return pl.pallas_call(
        body, name="inproj_bwd", grid=(n // tm,),
        in_specs=[pl.BlockSpec((tm, s.shape[1]), lambda i: (i, 0)) for s in slabs]
                 + [pl.BlockSpec((d, p), lambda i: (0, 0)), pl.BlockSpec((tm, d), lambda i: (i, 0)),
                    pl.BlockSpec((1, d), lambda i: (0, 0)), pl.BlockSpec((tm, d), lambda i: (i, 0))],
        out_specs=[pl.BlockSpec((tm, d), lambda i: (i, 0)), pl.BlockSpec((1, 1, d), lambda i: (i, 0, 0))],
        out_shape=[jax.ShapeDtypeStruct((n, d), F32), jax.ShapeDtypeStruct((n // tm, 1, d), F32)],
        compiler_params=_cparams("parallel"),
    )(*slabs, w, x, nw, dy)


CONV_PAD = 8
CONV_ROWS = 256


def _conv_pre(pad_s, cw, c):
    xs = [pad_s[pl.ds(c * CONV_ROWS + CONV_PAD - (CONV_WIDTH - 1) + k, CONV_ROWS), :] for k in range(CONV_WIDTH)]
    pre = xs[0] * cw[0:1, :]
    for k in range(1, CONV_WIDTH):
        pre = pre + xs[k] * cw[k:k + 1, :]
    return pre, xs


def conv_fwd(proj, cw, nseq):
    n = proj.shape[0]
    t = n // nseq
    ch = cw.shape[1]

    def body(x_ref, cw_ref, y_ref, pad_s):
        pad_s[pl.ds(0, CONV_PAD), :] = jnp.zeros((CONV_PAD, TN_COLS), F32)
        pad_s[pl.ds(CONV_PAD, t), :] = x_ref[...]
        cwv = cw_ref[...]
        for c in range(t // CONV_ROWS):
            pre, _ = _conv_pre(pad_s, cwv, c)
            y_ref[pl.ds(c * CONV_ROWS, CONV_ROWS), :] = pre * _sigmoid(pre)

    return pl.pallas_call(
        body, name="conv_fwd", grid=(nseq, ch // TN_COLS),
        in_specs=[pl.BlockSpec((t, TN_COLS), lambda b, j: (b, j)), pl.BlockSpec((CONV_WIDTH, TN_COLS), lambda b, j: (0, j))],
        out_specs=pl.BlockSpec((t, TN_COLS), lambda b, j: (b, j)),
        out_shape=jax.ShapeDtypeStruct((n, ch), F32),
        scratch_shapes=[pltpu.VMEM((t + CONV_PAD, TN_COLS), F32)],
        compiler_params=_cparams("parallel", "parallel"),
    )(proj, cw)


def conv_bwd(proj, cw, dy, nseq):
    n = proj.shape[0]
    t = n // nseq
    ch = cw.shape[1]

    def body(x_ref, cw_ref, dy_ref, dx_ref, dcw_ref, pad_s, dpad_s):
        pad_s[pl.ds(0, CONV_PAD), :] = jnp.zeros((CONV_PAD, TN_COLS), F32)
        pad_s[pl.ds(CONV_PAD, t), :] = x_ref[...]
        dpad_s[pl.ds(t, CONV_PAD), :] = jnp.zeros((CONV_PAD, TN_COLS), F32)
        cwv = cw_ref[...]
        acc = [jnp.zeros((1, TN_COLS), F32)] * CONV_WIDTH
        for c in range(t // CONV_ROWS):
            pre, xs = _conv_pre(pad_s, cwv, c)
            sg = _sigmoid(pre)
            dpre = dy_ref[pl.ds(c * CONV_ROWS, CONV_ROWS), :] * (sg * (1.0 + pre * (1.0 - sg)))
            dpad_s[pl.ds(c * CONV_ROWS, CONV_ROWS), :] = dpre
            acc = [acc[k] + jnp.sum(dpre * xs[k], axis=0, keepdims=True) for k in range(CONV_WIDTH)]
        for k in range(CONV_WIDTH):
            dcw_ref[0, pl.ds(k, 1), :] = acc[k]
        for c in range(t // CONV_ROWS):
            dx = dpad_s[pl.ds(c * CONV_ROWS + CONV_WIDTH - 1, CONV_ROWS), :] * cwv[0:1, :]
            for k in range(1, CONV_WIDTH):
                dx = dx + dpad_s[pl.ds(c * CONV_ROWS + CONV_WIDTH - 1 - k, CONV_ROWS), :] * cwv[k:k + 1, :]
            dx_ref[pl.ds(c * CONV_ROWS, CONV_ROWS), :] = dx.astype(BF16)

    blk = pl.BlockSpec((t, TN_COLS), lambda b, j: (b, j))
    return pl.pallas_call(
        body, name="conv_bwd", grid=(nseq, ch // TN_COLS),
        in_specs=[blk, pl.BlockSpec((CONV_WIDTH, TN_COLS), lambda b, j: (0, j)), blk],
        out_specs=[blk, pl.BlockSpec((1, CONV_WIDTH, TN_COLS), lambda b, j: (b, 0, j))],
        out_shape=[jax.ShapeDtypeStruct((n, ch), BF16), jax.ShapeDtypeStruct((nseq, CONV_WIDTH, ch), F32)],
        scratch_shapes=[pltpu.VMEM((t + CONV_PAD, TN_COLS), F32)] * 2,
        compiler_params=_cparams("parallel", "parallel"),
    )(proj, cw, dy)


def _gate_specs(d):
    wide = pl.BlockSpec((TM_ROWS, d), lambda i: (i, 0))
    oa = pl.BlockSpec((TM_ROWS, A_WIDTH), lambda i: (i, 0))
    ob = pl.BlockSpec((TM_ROWS, B_WIDTH), lambda i: (i, 0))
    oc = pl.BlockSpec((TM_ROWS, C_WIDTH), lambda i: (i, 0))
    za = pl.BlockSpec((TM_ROWS, A_WIDTH), lambda i: (i, COL_AZ // A_WIDTH))
    zb = pl.BlockSpec((TM_ROWS, B_WIDTH), lambda i: (i, (COL_B + 3 * B_WIDTH) // B_WIDTH))
    zc = pl.BlockSpec((TM_ROWS, C_WIDTH), lambda i: (i, (COL_C + 3 * C_WIDTH) // C_WIDTH))
    return wide, oa, ob, oc, za, zb, zc


BRANCH_COLS = ((0, A_WIDTH), (A_WIDTH, A_WIDTH + B_WIDTH), (A_WIDTH + B_WIDTH, D_MODEL))


def outproj_fwd(x, oa, ob, oc, proj, w):
    n, d = x.shape

    def body(x_ref, oa_ref, ob_ref, oc_ref, za_ref, zb_ref, zc_ref, w_ref, y_ref, m_ref):
        for (lo, hi), o_ref, z_ref in zip(BRANCH_COLS, (oa_ref, ob_ref, oc_ref), (za_ref, zb_ref, zc_ref)):
            zv = z_ref[...]
            m_ref[:, lo:hi] = (o_ref[...] * (zv * _sigmoid(zv))).astype(BF16)
        y_ref[...] = x_ref[...] + jnp.dot(m_ref[...], w_ref[...], preferred_element_type=F32)

    wide, sa, sb, sc, za, zb, zc = _gate_specs(d)
    return pl.pallas_call(
        body, name="outproj_fwd", grid=(n // TM_ROWS,),
        in_specs=[wide, sa, sb, sc, za, zb, zc, pl.BlockSpec((d, d), lambda i: (0, 0))],
        out_specs=[wide, wide],
        out_shape=[jax.ShapeDtypeStruct((n, d), F32), jax.ShapeDtypeStruct((n, d), BF16)],
        compiler_params=_cparams("parallel"),
    )(x, oa, ob, oc, proj, proj, proj, w)


def outproj_bwd(dy, oa, ob, oc, proj, w):
    n, d = dy.shape

    def body(dy_ref, oa_ref, ob_ref, oc_ref, za_ref, zb_ref, zc_ref, w_ref, doa_ref, dob_ref, doc_ref, dza_ref, dzb_ref, dzc_ref):
        dm = lax.dot_general(dy_ref[...].astype(BF16), w_ref[...], NT, preferred_element_type=F32)
        for (lo, hi), o_ref, z_ref, do_ref, dz_ref in zip(BRANCH_COLS, (oa_ref, ob_ref, oc_ref), (za_ref, zb_ref, zc_ref),
                                                          (doa_ref, dob_ref, doc_ref), (dza_ref, dzb_ref, dzc_ref)):
            zv = z_ref[...]
            sg = _sigmoid(zv)
            dmv = dm[:, lo:hi]
            do_ref[...] = dmv * (zv * sg)
            dz_ref[...] = (dmv * o_ref[...] * (sg * (1.0 + zv * (1.0 - sg)))).astype(BF16)

    wide, sa, sb, sc, za, zb, zc = _gate_specs(d)
    sd = jax.ShapeDtypeStruct
    outs = [sd((n, A_WIDTH), F32), sd((n, B_WIDTH), F32), sd((n, C_WIDTH), F32)]
    return pl.pallas_call(
        body, name="outproj_bwd", grid=(n // TM_ROWS,),
        in_specs=[wide, sa, sb, sc, za, zb, zc, pl.BlockSpec((d, d), lambda i: (0, 0))],
        out_specs=[sa, sb, sc, sa, sb, sc],
        out_shape=outs + [sd(o.shape, BF16) for o in outs],
        compiler_params=_cparams("parallel"),
    )(dy, oa, ob, oc, proj, proj, proj, w)


def loss_fwd_bwd(y, target):
    n, d = y.shape

    def body(y_ref, t_ref, dy_ref, part_ref):
        e = y_ref[...] - t_ref[...]
        dy_ref[...] = e * (1.0 / d)
        part_ref[...] = jnp.zeros_like(part_ref) + 0.5 * jnp.sum(e * e) * (1.0 / d)

    blk = pl.BlockSpec((TM_ROWS, d), lambda i: (i, 0))
    return pl.pallas_call(
        body, name="loss", grid=(n // TM_ROWS,),
        in_specs=[blk, blk],
        out_specs=[blk, pl.BlockSpec((1, 8, BLOCK), lambda i: (i, 0, 0))],
        out_shape=[jax.ShapeDtypeStruct((n, d), F32), jax.ShapeDtypeStruct((n // TM_ROWS, 8, BLOCK), F32)],
        compiler_params=_cparams("parallel"),
    )(y, target)


ADAM_LR, ADAM_B1, ADAM_B2, ADAM_EPS, ADAM_WD, ADAM_STEP = 0.001, 0.9, 0.999, 1e-08, 0.01, 10


def adamw(w, g, m, v):
    r, c = w.shape
    tr = r if r <= 256 else 256

    def body(w_ref, g_ref, m_ref, v_ref, d_ref, nm_ref, nv_ref):
        gv = g_ref[...]
        nm = ADAM_B1 * m_ref[...] + (1.0 - ADAM_B1) * gv
        nv = ADAM_B2 * v_ref[...] + (1.0 - ADAM_B2) * (gv * gv)
        m_hat = nm / (1.0 - ADAM_B1 ** ADAM_STEP)
        v_hat = nv / (1.0 - ADAM_B2 ** ADAM_STEP)
        d_ref[...] = -ADAM_LR * (m_hat / (jnp.sqrt(v_hat) + ADAM_EPS) + ADAM_WD * w_ref[...])
        nm_ref[...] = nm
        nv_ref[...] = nv

    blk = pl.BlockSpec((tr, c), lambda i: (i, 0))
    return pl.pallas_call(
        body, name="adamw", grid=(r // tr,),
        in_specs=[blk] * 4, out_specs=[blk] * 3,
        out_shape=[jax.ShapeDtypeStruct((r, c), F32)] * 3,
        compiler_params=_cparams("parallel"),
    )(w, g, m, v)


SB_G = N_HEADS_B


def _sb_weights(qs, k, carry, tri, diag):
    z = _mm(qs, k, NT)
    sp = jnp.log(1.0 + jnp.exp(-jnp.abs(z)))
    ls_pos = jnp.minimum(z, 0.0) - sp
    ls_neg = jnp.minimum(-z, 0.0) - sp
    earlier = (lax.broadcasted_iota(jnp.int32, z.shape, 1) < lax.broadcasted_iota(jnp.int32, z.shape, 0)) if diag else None
    log_keep = jnp.where(earlier, ls_neg, 0.0) if diag else ls_neg
    hi = log_keep.astype(BF16)
    lo = (log_keep - hi.astype(F32)).astype(BF16)
    within = lax.dot_general(jnp.concatenate([hi, lo], axis=1), tri, NN, preferred_element_type=F32)
    arg = ls_pos + within + carry
    wts = jnp.where(earlier, jnp.exp(jnp.where(earlier, arg, 0.0)), 0.0) if diag else jnp.exp(arg)
    return ls_pos, ls_neg, log_keep, wts, earlier


def _sb_tile(q, k, v, carry, tri, diag):
    _, _, log_keep, wts, _ = _sb_weights(q * (HEAD_DIM ** -0.5), k, carry, tri, diag)
    return _mm(wts, v), jnp.sum(log_keep, axis=1, keepdims=True)


def _sb_tile_grads(q, k, v, carry, do, dtot, tri, diag):
    qs = q * (HEAD_DIM ** -0.5)
    ls_pos, ls_neg, _, wts, earlier = _sb_weights(qs, k, carry, tri, diag)
    dv = _mm(wts, do, TN)
    darg = _mm(do, v, NT) * wts
    dkeep = _mm(darg, tri[:BLOCK], NT) + dtot
    if diag:
        dkeep = jnp.where(earlier, dkeep, 0.0)
    dz = darg * jnp.exp(ls_neg) - dkeep * jnp.exp(ls_pos)
    return _mm(dz, k) * (HEAD_DIM ** -0.5), _mm(dz, qs, TN), dv, jnp.sum(darg, axis=1, keepdims=True)


_sb_tiles_diag = jax.vmap(functools.partial(_sb_tile, diag=True), in_axes=(0, 0, 0, 0, None))
_sb_tiles_off = jax.vmap(functools.partial(_sb_tile, diag=False), in_axes=(0, 0, 0, 0, None))
_sb_grads_diag = jax.vmap(functools.partial(_sb_tile_grads, diag=True), in_axes=(0, 0, 0, 0, 0, 0, None))
_sb_grads_off = jax.vmap(functools.partial(_sb_tile_grads, diag=False), in_axes=(0, 0, 0, 0, 0, 0, None))


def _sb_tri():
    r = lax.broadcasted_iota(jnp.int32, (2 * BLOCK, BLOCK), 0) % BLOCK
    c = lax.broadcasted_iota(jnp.int32, (2 * BLOCK, BLOCK), 1)
    return jnp.where(r > c, 1.0, 0.0).astype(BF16)


SB_SEQ = 2


def _sb_specs(t, nq):
    cb = COL_B // B_WIDTH
    sq = SB_SEQ
    q = pl.BlockSpec((sq, BLOCK, B_WIDTH), lambda b, i: (b, i, cb))
    k = pl.BlockSpec((sq, t, B_WIDTH), lambda b, i: (b, 0, cb + 1))
    v = pl.BlockSpec((sq, t, B_WIDTH), lambda b, i: (b, 0, cb + 2))
    blk = pl.BlockSpec((sq, BLOCK, B_WIDTH), lambda b, i: (b, i, 0))
    full = pl.BlockSpec((sq, t, B_WIDTH), lambda b, i: (b, 0, 0))
    carry = pl.BlockSpec((sq, 1, nq, BLOCK, SB_G), lambda b, i: (b, i, 0, 0, 0))
    return q, k, v, blk, full, carry


def _sb_heads(ref, rows):
    return jnp.concatenate([_heads(ref[b, rows, :], SB_G) for b in range(SB_SEQ)])


def _sb_unheads(a):
    return [_unheads(a[b * SB_G:(b + 1) * SB_G]) for b in range(SB_SEQ)]


def sb_fwd(proj, nseq):
    n = proj.shape[0]
    t = n // nseq
    nq = t // BLOCK
    g, sq = SB_G, SB_SEQ
    everything = pl.ds(0, BLOCK)

    def body(q_ref, k_ref, v_ref, o_ref, carry_ref):
        i = pl.program_id(1)
        tri = _sb_tri()
        qv = _sb_heads(q_ref, everything)

        def tile(j, c, fn):
            rows = pl.ds(pl.multiple_of(j * BLOCK, BLOCK), BLOCK)
            for b in range(sq):
                carry_ref[b, 0, j] = jnp.concatenate([c[b * g + h] for h in range(g)], axis=1)
            return fn(qv, _sb_heads(k_ref, rows), _sb_heads(v_ref, rows), c, tri)

        def step(it, st):
            o_acc, c = st
            o, tot = tile(i - 1 - it, c, _sb_tiles_off)
            return o_acc + o, c + tot

        o_acc, _ = lax.fori_loop(0, i, step, tile(i, jnp.zeros((sq * g, BLOCK, 1), F32), _sb_tiles_diag))
        for b, o in enumerate(_sb_unheads(o_acc)):
            o_ref[b] = o

    q, k, v, blk, _, carry = _sb_specs(t, nq)
    proj3 = proj.reshape(nseq, t, -1)
    o, carries = pl.pallas_call(
        body, name="sb_fwd", grid=(nseq // sq, nq),
        in_specs=[q, k, v],
        out_specs=[blk, carry],
        out_shape=[jax.ShapeDtypeStruct((nseq, t, B_WIDTH), F32),
                   jax.ShapeDtypeStruct((nseq, nq, nq, BLOCK, g), F32)],
        compiler_params=_cparams("parallel", "arbitrary"),
    )(proj3, proj3, proj3)
    return o.reshape(n, B_WIDTH), carries


def sb_bwd(proj, carries, do, nseq):
    n = proj.shape[0]
    t = n // nseq
    nq = t // BLOCK
    g, sq = SB_G, SB_SEQ
    everything = pl.ds(0, BLOCK)

    def body(q_ref, k_ref, v_ref, carry_ref, do_ref, dq_ref, dk_ref, dv_ref):
        i = pl.program_id(1)

        @pl.when(i == 0)
        def _():
            dk_ref[...] = jnp.zeros_like(dk_ref)
            dv_ref[...] = jnp.zeros_like(dv_ref)

        tri = _sb_tri()
        qv = _sb_heads(q_ref, everything)
        dov = _sb_heads(do_ref, everything)

        def tile(j, st, fn):
            dq_acc, dc = st
            rows = pl.ds(pl.multiple_of(j * BLOCK, BLOCK), BLOCK)
            cj = [carry_ref[b, 0, j] for b in range(sq)]
            dq, dk, dv, dcj = fn(qv, _sb_heads(k_ref, rows), _sb_heads(v_ref, rows),
                                 jnp.stack([cj[b][:, h:h + 1] for b in range(sq) for h in range(g)]), dov, dc, tri)
            for b, (dkb, dvb) in enumerate(zip(_sb_unheads(dk), _sb_unheads(dv))):
                dk_ref[b, rows, :] += dkb
                dv_ref[b, rows, :] += dvb
            return dq_acc + dq, dc + dcj

        st = lax.fori_loop(0, i, lambda j, st: tile(j, st, _sb_grads_off),
                           (jnp.zeros((sq * g, BLOCK, HEAD_DIM), F32), jnp.zeros((sq * g, BLOCK, 1), F32)))
        dq_acc, _ = tile(i, st, _sb_grads_diag)
        for b, dq in enumerate(_sb_unheads(dq_acc)):
            dq_ref[b] = dq

    q, k, v, blk, full, carry = _sb_specs(t, nq)
    proj3, do3 = proj.reshape(nseq, t, -1), do.reshape(nseq, t, -1)
    grads = pl.pallas_call(
        body, name="sb_bwd", grid=(nseq // sq, nq),
        in_specs=[q, k, v, carry, blk],
        out_specs=[blk, full, full],
        out_shape=[jax.ShapeDtypeStruct((nseq, t, B_WIDTH), F32)] * 3,
        compiler_params=_cparams("parallel", "arbitrary"),
    )(proj3, proj3, proj3, carries, do3)
    return [a.reshape(n, B_WIDTH) for a in grads]


def _unit_lower_inverse(a):
    n = a.shape[0]
    eye = jnp.where(lax.broadcasted_iota(jnp.int32, (n, n), 0) == lax.broadcasted_iota(jnp.int32, (n, n), 1), 1.0, 0.0)
    tmat = eye.astype(F32) - a
    p = a
    for _ in range(5):
        p = _mm32(p, p)
        tmat = tmat + _mm32(tmat, p)
    return tmat


@jax.custom_vjp
def _known_inverse(a, tmat):
    return tmat


def _known_inverse_fwd(a, tmat):
    return tmat, tmat


def _known_inverse_bwd(tmat, g):
    return -_mm32(_mm32(tmat, g, TN), tmat, NT), jnp.zeros_like(tmat)


_known_inverse.defvjp(_known_inverse_fwd, _known_inverse_bwd)


def _gdn_chunk(q, k, v, al_c, al_r, br_c, alog, dtb, nw, s, tmat_in):
    c = GDN_CHUNK
    ri = lax.broadcasted_iota(jnp.int32, (c, c), 0)
    ci = lax.broadcasted_iota(jnp.int32, (c, c), 1)
    incl, strict = ri >= ci, ri > ci
    rate = -jnp.exp(alog)
    g_c = rate * _softplus(al_c + dtb)
    g_r = rate * _softplus(al_r + dtb)
    beta = _sigmoid(br_c)
    gc_c = jnp.sum(jnp.where(incl, g_r, 0.0), axis=1, keepdims=True)
    gc_r = jnp.sum(jnp.where(ri <= ci, g_c, 0.0), axis=0, keepdims=True)
    gl = jnp.sum(g_r, axis=1, keepdims=True)
    decay = jnp.where(incl, jnp.exp(jnp.where(incl, gc_c - gc_r, 0.0)), 0.0)
    qn = q * lax.rsqrt(jnp.sum(q * q, axis=-1, keepdims=True) + RMS_EPS) * (HEAD_DIM ** -0.5)
    kn = k * lax.rsqrt(jnp.sum(k * k, axis=-1, keepdims=True) + RMS_EPS)
    kb = kn * beta
    a = jnp.where(strict, _mm(kb, kn, NT) * decay, 0.0)
    tmat = _unit_lower_inverse(a) if tmat_in is None else _known_inverse(a, tmat_in)
    u = _mm(tmat, v * beta)
    w = _mm(tmat, kb * jnp.exp(gc_c))
    qk = _mm(qn, kn, NT) * decay
    v_new = u - _mm(w, s)
    o = _mm(qn * jnp.exp(gc_c), s) + _mm(qk, v_new)
    s_new = s * jnp.exp(gl) + _mm(kn * jnp.exp(gl - gc_c), v_new, TN)
    o = o * lax.rsqrt(jnp.mean(o * o, axis=-1, keepdims=True) + RMS_EPS) * nw
    return o, s_new, tmat


_gdn_chunks_fwd = jax.vmap(functools.partial(_gdn_chunk, tmat_in=None), in_axes=(0, 0, 0, 0, 0, 0, 0, 0, None, 0))
_gdn_chunks_bwd = jax.vmap(_gdn_chunk, in_axes=(0, 0, 0, 0, 0, 0, 0, 0, None, 0, 0))

GDN_TB = 256
GDN_SEQ_FWD = 2
GDN_SEQ_BWD = 2


def _gdn_block(q3, k3, v3, ba, alog, dtb, nw, s, tm=None):
    nh = N_HEADS_A
    ns = q3.shape[0]
    bat = [ba[b].T for b in range(ns)]
    br_c = jnp.stack([ba[b][:, h:h + 1] for b in range(ns) for h in range(nh)])
    al_c = jnp.stack([ba[b][:, nh + h:nh + h + 1] for b in range(ns) for h in range(nh)])
    al_r = jnp.stack([bat[b][nh + h:nh + h + 1, :] for b in range(ns) for h in range(nh)])
    heads = lambda a: jnp.concatenate([_heads(a[b], nh) for b in range(ns)])
    args = (heads(q3), heads(k3), heads(v3), al_c, al_r, br_c, jnp.concatenate([alog] * ns), jnp.concatenate([dtb] * ns), nw, s)
    o, s_new, tmat = _gdn_chunks_fwd(*args) if tm is None else _gdn_chunks_bwd(*args, tm)
    o3 = jnp.stack([_unheads(o[b * nh:(b + 1) * nh]) for b in range(ns)])
    return (o3, s_new, tmat) if tm is None else (o3, s_new)


def _gdn_specs(nt, sq, rev):
    tpos = (lambda i: nt - 1 - i) if rev else (lambda i: i)
    ncb = GDN_TB // GDN_CHUNK
    qkv = [pl.BlockSpec((sq, GDN_TB, A_WIDTH), lambda b, i, j=j: (b, tpos(i), j)) for j in range(3)]
    ba = pl.BlockSpec((sq, GDN_TB, BLOCK), lambda b, i: (b, tpos(i), COL_BA // BLOCK))
    one = pl.BlockSpec((N_HEADS_A, 1, 1), lambda b, i: (0, 0, 0))
    vec = pl.BlockSpec((1, HEAD_DIM), lambda b, i: (0, 0))
    st = pl.BlockSpec((sq, ncb, N_HEADS_A, HEAD_DIM, HEAD_DIM), lambda b, i: (b, tpos(i), 0, 0, 0))
    oa = pl.BlockSpec((sq, GDN_TB, A_WIDTH), lambda b, i: (b, tpos(i), 0))
    return qkv, ba, one, vec, st, oa, tpos


def gdn_fwd(ya, proj, alog, dtb, nw, nseq):
    n = ya.shape[0]
    t = n // nseq
    nc, nt, ncb = t // GDN_CHUNK, t // GDN_TB, GDN_TB // GDN_CHUNK
    sq = GDN_SEQ_FWD
    nh = N_HEADS_A

    def body(q_ref, k_ref, v_ref, ba_ref, alog_ref, dtb_ref, nw_ref, o_ref, st_ref, tm_ref, s_s):
        @pl.when(pl.program_id(1) == 0)
        def _():
            s_s[...] = jnp.zeros_like(s_s)

        def step(c, s):
            rows = pl.ds(pl.multiple_of(c * GDN_CHUNK, GDN_CHUNK), GDN_CHUNK)
            o, s_new, tmat = _gdn_block(q_ref[:, rows, :], k_ref[:, rows, :], v_ref[:, rows, :], ba_ref[:, rows, :],
                                        alog_ref[...], dtb_ref[...], nw_ref[...], s)
            for b in range(sq):
                st_ref[b, c] = s[b * nh:(b + 1) * nh]
                tm_ref[b, c] = tmat[b * nh:(b + 1) * nh]
            o_ref[:, rows, :] = o
            return s_new

        s_s[...] = lax.fori_loop(0, ncb, step, s_s[...])

    qkv, ba, one, vec, st, oa, _ = _gdn_specs(nt, sq, False)
    ya3, proj3 = ya.reshape(nseq, t, -1), proj.reshape(nseq, t, -1)
    per_chunk = jax.ShapeDtypeStruct((nseq, nc, nh, HEAD_DIM, HEAD_DIM), F32)
    o, states, inverses = pl.pallas_call(
        body, name="gdn_fwd", grid=(nseq // sq, nt),
        in_specs=qkv + [ba, one, one, vec],
        out_specs=[oa, st, st],
        out_shape=[jax.ShapeDtypeStruct((nseq, t, A_WIDTH), F32), per_chunk, per_chunk],
        scratch_shapes=[pltpu.VMEM((sq * nh, HEAD_DIM, HEAD_DIM), F32)],
        compiler_params=_cparams("parallel", "arbitrary"),
    )(ya3, ya3, ya3, proj3, alog, dtb, nw)
    return o.reshape(n, A_WIDTH), states, inverses


def gdn_bwd(ya, proj, alog, dtb, nw, states, inverses, do, nseq):
    n = ya.shape[0]
    t = n // nseq
    nt, ncb = t // GDN_TB, GDN_TB // GDN_CHUNK
    nh = N_HEADS_A
    sq = GDN_SEQ_BWD

    def body(q_ref, k_ref, v_ref, ba_ref, alog_ref, dtb_ref, nw_ref, st_ref, tm_ref, do_ref,
             dya_ref, dba_ref, dalog_ref, ddtb_ref, dnw_ref, ds_s):
        @pl.when(pl.program_id(1) == 0)
        def _():
            ds_s[...] = jnp.zeros_like(ds_s)
            dalog_ref[...] = jnp.zeros_like(dalog_ref)
            ddtb_ref[...] = jnp.zeros_like(ddtb_ref)
            dnw_ref[...] = jnp.zeros_like(dnw_ref)

        def step(it, carry):
            ds, dalog, ddtb, dnw = carry
            c = ncb - 1 - it
            rows = pl.ds(pl.multiple_of(c * GDN_CHUNK, GDN_CHUNK), GDN_CHUNK)
            s_in = jnp.concatenate([st_ref[b, c] for b in range(sq)])
            tm_in = jnp.concatenate([tm_ref[b, c] for b in range(sq)])
            _, vjp = jax.vjp(functools.partial(_gdn_block, tm=tm_in), q_ref[:, rows, :], k_ref[:, rows, :], v_ref[:, rows, :],
                             ba_ref[:, rows, :], alog_ref[...], dtb_ref[...], nw_ref[...], s_in)
            dq, dk, dv, dba, da, dd, dn, ds = vjp((do_ref[:, rows, :], ds))
            dya_ref[:, rows, 0:A_WIDTH] = dq
            dya_ref[:, rows, A_WIDTH:2 * A_WIDTH] = dk
            dya_ref[:, rows, 2 * A_WIDTH:3 * A_WIDTH] = dv
            dba_ref[:, rows, :] = dba.astype(BF16)
            return ds, dalog + da, ddtb + dd, dnw + dn

        z11 = jnp.zeros((nh, 1, 1), F32)
        ds, dalog, ddtb, dnw = lax.fori_loop(0, ncb, step, (ds_s[...], z11, z11, jnp.zeros((1, HEAD_DIM), F32)))
        ds_s[...] = ds
        dalog_ref[0] += dalog
        ddtb_ref[0] += ddtb
        dnw_ref[0] += dnw

    qkv, ba, one, vec, st, oa, tpos = _gdn_specs(nt, sq, True)
    per_grp = pl.BlockSpec((1, nh, 1, 1), lambda b, i: (b, 0, 0, 0))
    sd = jax.ShapeDtypeStruct
    ya3, proj3, do3 = ya.reshape(nseq, t, -1), proj.reshape(nseq, t, -1), do.reshape(nseq, t, -1)
    dya, dba, dalog, ddtb, dnw = pl.pallas_call(
        body, name="gdn_bwd", grid=(nseq // sq, nt),
        in_specs=qkv + [ba, one, one, vec, st, st, oa],
        out_specs=[pl.BlockSpec((sq, GDN_TB, 3 * A_WIDTH), lambda b, i: (b, tpos(i), 0)),
                   pl.BlockSpec((sq, GDN_TB, BLOCK), lambda b, i: (b, tpos(i), 0)),
                   per_grp, per_grp, pl.BlockSpec((1, 1, HEAD_DIM), lambda b, i: (b, 0, 0))],
        out_shape=[sd((nseq, t, 3 * A_WIDTH), F32), sd((nseq, t, BLOCK), BF16), sd((nseq // sq, nh, 1, 1), F32),
                   sd((nseq // sq, nh, 1, 1), F32), sd((nseq // sq, 1, HEAD_DIM), F32)],
        scratch_shapes=[pltpu.VMEM((sq * nh, HEAD_DIM, HEAD_DIM), F32)],
        compiler_params=_cparams("parallel", "arbitrary"),
    )(ya3, ya3, ya3, proj3, alog, dtb, nw, states, inverses, do3)
    return dya.reshape(n, 3 * A_WIDTH), dba.reshape(n, BLOCK), dalog, ddtb, dnw


DIL_NB = tuple((SEQ // d) // BLOCK for _, d in DILATED_PAIRS)
DIL_D = tuple(d for _, d in DILATED_PAIRS)
DIL_STEPS = tuple(w // d for w, d in DILATED_PAIRS)
DIL_B = 8
PAIR = 2 * HEAD_DIM


def _rope_tables(t):
    half = ROPE_DIM // 2
    inv_freq = ROPE_THETA ** (-jnp.arange(half, dtype=F32) / half)
    ang = jnp.arange(t, dtype=F32)[:, None] * inv_freq[None, :]
    ones = jnp.ones((t, HEAD_DIM - ROPE_DIM), F32)
    cs = jnp.concatenate([jnp.cos(ang), jnp.cos(ang), ones], axis=1)
    sn = jnp.concatenate([jnp.sin(ang), jnp.sin(ang), 0.0 * ones], axis=1)
    i = jnp.arange(PAIR)[:, None]
    j = jnp.arange(PAIR)[None, :]
    same = (i // HEAD_DIM) == (j // HEAD_DIM)
    ih, jh = i % HEAD_DIM, j % HEAD_DIM
    pm = (jnp.where(same & (jh < half) & (ih == jh + half), -1.0, 0.0)
          + jnp.where(same & (jh >= half) & (jh < ROPE_DIM) & (ih == jh - half), 1.0, 0.0))
    mean = jnp.where(same, 1.0 / HEAD_DIM, 0.0)
    twice = lambda m: jnp.concatenate([m, m]).astype(BF16)
    return jnp.tile(cs, (1, 2)), jnp.tile(sn, (1, 2)), twice(mean), twice(pm)


def _split_dot(x, w2):
    hi = x.astype(BF16)
    lo = lax.stop_gradient(x - hi.astype(F32)).astype(BF16)
    return lax.dot_general(jnp.concatenate([hi, lo], axis=1), w2, NN, preferred_element_type=F32)


def _dil_prep(x, w, cs, sn, mean2, pm2):
    y = x * lax.rsqrt(_split_dot(x * x, mean2) + RMS_EPS) * w
    return y * cs + _split_dot(y, pm2) * sn


def _dil_tile(qn, kk, vv, bias):
    lane = lax.broadcasted_iota(jnp.int32, (1, PAIR), 1)
    outs, lses = [], []
    for h in range(2):
        s = _mm(jnp.where(lane // HEAD_DIM == h, qn, 0.0) * (HEAD_DIM ** -0.5), kk, NT) + bias
        m = lax.stop_gradient(jnp.max(s, axis=-1, keepdims=True))
        p = jnp.exp(s - m)
        denom = jnp.sum(p, axis=-1, keepdims=True)
        outs.append(_mm(p, vv) / denom)
        lses.append(m + jnp.log(denom))
    return jnp.where(lane < HEAD_DIM, outs[0], outs[1]), jnp.concatenate(lses, axis=1)


_dil_tiles = jax.vmap(_dil_tile)


def _spread(a):
    lane = lax.broadcasted_iota(jnp.int32, (a.shape[0], PAIR), 1)
    return jnp.where(lane < HEAD_DIM, a[:, 0:1], a[:, 1:2])


def _dil_mix(o1, o2, o3, l1, l2, l3):
    m = lax.stop_gradient(jnp.maximum(jnp.maximum(l1, l2), l3))
    e1, e2, e3 = jnp.exp(l1 - m), jnp.exp(l2 - m), jnp.exp(l3 - m)
    r = 1.0 / (e1 + e2 + e3)
    return _spread(e1 * r) * o1 + _spread(e2 * r) * o2 + _spread(e3 * r) * o3


def _dil_fill_biases(bias_s):
    steps, = set(DIL_STEPS)
    qi = lax.broadcasted_iota(jnp.int32, (BLOCK, 1), 0)
    kj = lax.broadcasted_iota(jnp.int32, (1, 2 * BLOCK), 1)
    rel = qi - kj + BLOCK
    inside = (rel >= 0) & (rel <= steps)
    bias_s[0] = jnp.where(inside, 0.0, NEG)
    bias_s[1] = jnp.where(inside & (kj >= BLOCK), 0.0, NEG)
    bias_s[2] = jnp.where((qi >= kj) & (qi - kj <= steps), 0.0, NEG)


def _dil_mask(it, g, bias_s):
    qrows = pl.ds(pl.multiple_of(it * BLOCK, BLOCK), BLOCK)
    if DIL_NB[g] == 1:
        return bias_s[2, :, 0:BLOCK], qrows, qrows
    which = jnp.where(it == 0, 2, jnp.where(it % DIL_NB[g] == 0, 1, 0))
    kstart = jnp.maximum(it - 1, 0) * BLOCK
    return bias_s[which], qrows, pl.ds(pl.multiple_of(kstart, BLOCK), 2 * BLOCK)


def _dil_gather(src, dst, d):
    t = src.shape[0]
    ln = t // d
    for r in range(d):
        dst[pl.ds(r * ln, ln), :] = src[pl.ds(r, ln, stride=d), :]


def _dil_scatter(src, dst, d):
    t = src.shape[0]
    ln = t // d
    for r in range(d):
        dst[pl.ds(r, ln, stride=d), :] = src[pl.ds(r * ln, ln), :]


def _dil_forward_parts(q_ref, k_ref, v_ref, qw, kw, cs_ref, sn_ref, mean2, pm2, qn_s, kn_s, dl_s, od_s, ld_s, on_s, ln_s, bias_s):
    t = qn_s.shape[0]
    _dil_fill_biases(bias_s)

    def prep(c, _):
        rows = pl.ds(pl.multiple_of(c * ROWS, ROWS), ROWS)
        qn_s[rows, :] = _dil_prep(q_ref[rows, :], qw, cs_ref[rows, :], sn_ref[rows, :], mean2, pm2)
        kn_s[rows, :] = _dil_prep(k_ref[rows, :], kw, cs_ref[rows, :], sn_ref[rows, :], mean2, pm2)
        return 0

    lax.fori_loop(0, t // ROWS, prep, 0)
    for g in (1, 2):
        _dil_gather(qn_s, dl_s.at[g - 1, 0], DIL_D[g])
        _dil_gather(kn_s, dl_s.at[g - 1, 1], DIL_D[g])
        _dil_gather(v_ref, dl_s.at[g - 1, 2], DIL_D[g])
    for g in range(3):
        qs = qn_s if g == 0 else dl_s.at[g - 1, 0]
        ks = kn_s if g == 0 else dl_s.at[g - 1, 1]
        vs = v_ref if g == 0 else dl_s.at[g - 1, 2]

        def tiles(i, _, g=g, qs=qs, ks=ks, vs=vs):
            where = [_dil_mask(i * DIL_B + b, g, bias_s) for b in range(DIL_B)]
            o, lse = _dil_tiles(jnp.stack([qs[qr, :] for _, qr, _ in where]), jnp.stack([ks[kr, :] for _, _, kr in where]),
                                jnp.stack([vs[kr, :] for _, _, kr in where]), jnp.stack([m for m, _, _ in where]))
            for b, (_, qr, _) in enumerate(where):
                od_s[g, qr, :] = o[b]
                ld_s[g, qr, :] = lse[b]
            return 0

        lax.fori_loop(0, t // BLOCK // DIL_B, tiles, 0)
    for g in (1, 2):
        _dil_scatter(od_s.at[g], on_s.at[g - 1], DIL_D[g])
        _dil_scatter(ld_s.at[g], ln_s.at[g - 1], DIL_D[g])


def _dil_scratch(t):
    return [pltpu.VMEM((t, PAIR), F32), pltpu.VMEM((t, PAIR), F32),
            pltpu.VMEM((2, 3, t, PAIR), F32),
            pltpu.VMEM((3, t, PAIR), F32), pltpu.VMEM((3, t, 2), F32),
            pltpu.VMEM((2, t, PAIR), F32), pltpu.VMEM((2, t, 2), F32),
            pltpu.VMEM((3, BLOCK, 2 * BLOCK), F32)]


def _dil_specs(t):
    cb = COL_C // BLOCK
    per = C_WIDTH // BLOCK
    qkv = [pl.BlockSpec((t, BLOCK), lambda b, p, j=j: (b, cb + j * per + p)) for j in range(3)]
    vec = pl.BlockSpec((1, PAIR), lambda b, p: (0, 0))
    tab = pl.BlockSpec((t, PAIR), lambda b, p: (0, 0))
    mat = pl.BlockSpec((2 * PAIR, PAIR), lambda b, p: (0, 0))
    pair = pl.BlockSpec((t, BLOCK), lambda b, p: (b, p))
    return qkv, vec, tab, mat, pair


def dil_fwd(proj, qw, kw, cs, sn, mean2, pm2, nseq):
    n = proj.shape[0]
    t = n // nseq

    def body(q_ref, k_ref, v_ref, qw_ref, kw_ref, cs_ref, sn_ref, mean_ref, pm_ref, o_ref,
             qn_s, kn_s, dl_s, od_s, ld_s, on_s, ln_s, bias_s):
        _dil_forward_parts(q_ref, k_ref, v_ref, qw_ref[...], kw_ref[...], cs_ref, sn_ref, mean_ref[...], pm_ref[...],
                           qn_s, kn_s, dl_s, od_s, ld_s, on_s, ln_s, bias_s)

        def mix(c, _):
            rows = pl.ds(pl.multiple_of(c * ROWS, ROWS), ROWS)
            o_ref[rows, :] = _dil_mix(od_s[0, rows, :], on_s[0, rows, :], on_s[1, rows, :],
                                      ld_s[0, rows, :], ln_s[0, rows, :], ln_s[1, rows, :])
            return 0

        lax.fori_loop(0, t // ROWS, mix, 0)

    qkv, vec, tab, mat, pair = _dil_specs(t)
    return pl.pallas_call(
        body, name="dil_fwd", grid=(nseq, C_WIDTH // BLOCK),
        in_specs=qkv + [vec, vec, tab, tab, mat, mat],
        out_specs=pair,
        out_shape=jax.ShapeDtypeStruct((n, C_WIDTH), F32),
        scratch_shapes=_dil_scratch(t),
        compiler_params=_cparams("parallel", "parallel"),
    )(proj, proj, proj, qw, kw, cs, sn, mean2, pm2)


def dil_bwd(proj, qw, kw, cs, sn, mean2, pm2, do, nseq):
    n = proj.shape[0]
    t = n // nseq

    def body(q_ref, k_ref, v_ref, qw_ref, kw_ref, cs_ref, sn_ref, mean_ref, pm_ref, do_ref,
             dq_ref, dk_ref, dv_ref, dqw_ref, dkw_ref,
             qn_s, kn_s, dl_s, od_s, ld_s, on_s, ln_s, bias_s, tq_s, tk_s, tv_s):
        qw, kw, mean2, pm2 = qw_ref[...], kw_ref[...], mean_ref[...], pm_ref[...]
        _dil_forward_parts(q_ref, k_ref, v_ref, qw, kw, cs_ref, sn_ref, mean2, pm2, qn_s, kn_s, dl_s, od_s, ld_s, on_s, ln_s, bias_s)

        def mix(c, _):
            rows = pl.ds(pl.multiple_of(c * ROWS, ROWS), ROWS)
            _, vjp = jax.vjp(_dil_mix, od_s[0, rows, :], on_s[0, rows, :], on_s[1, rows, :],
                             ld_s[0, rows, :], ln_s[0, rows, :], ln_s[1, rows, :])
            d1, d2, d3, e1, e2, e3 = vjp(do_ref[rows, :])
            od_s[0, rows, :] = d1
            on_s[0, rows, :] = d2
            on_s[1, rows, :] = d3
            ld_s[0, rows, :] = e1
            ln_s[0, rows, :] = e2
            ln_s[1, rows, :] = e3
            return 0

        lax.fori_loop(0, t // ROWS, mix, 0)
        for g in (1, 2):
            _dil_gather(on_s.at[g - 1], od_s.at[g], DIL_D[g])
            _dil_gather(ln_s.at[g - 1], ld_s.at[g], DIL_D[g])
        on_s[...] = jnp.zeros_like(on_s)
        dv_ref[...] = jnp.zeros_like(dv_ref)
        for g in range(3):
            qs = qn_s if g == 0 else dl_s.at[g - 1, 0]
            ks = kn_s if g == 0 else dl_s.at[g - 1, 1]
            vs = v_ref if g == 0 else dl_s.at[g - 1, 2]
            gq = on_s.at[0] if g == 0 else tq_s
            gk = on_s.at[1] if g == 0 else tk_s
            gv = dv_ref if g == 0 else tv_s
            if g > 0:
                tk_s[...] = jnp.zeros_like(tk_s)
                tv_s[...] = jnp.zeros_like(tv_s)

            def tiles(i, _, g=g, qs=qs, ks=ks, vs=vs, gq=gq, gk=gk, gv=gv):
                where = [_dil_mask(i * DIL_B + b, g, bias_s) for b in range(DIL_B)]
                biases = jnp.stack([m for m, _, _ in where])
                _, vjp = jax.vjp(lambda q_, k_, v_: _dil_tiles(q_, k_, v_, biases),
                                 jnp.stack([qs[qr, :] for _, qr, _ in where]), jnp.stack([ks[kr, :] for _, _, kr in where]),
                                 jnp.stack([vs[kr, :] for _, _, kr in where]))
                dq, dkk, dvv = vjp((jnp.stack([od_s[g, qr, :] for _, qr, _ in where]),
                                    jnp.stack([ld_s[g, qr, :] for _, qr, _ in where])))
                for b, (_, qr, kr) in enumerate(where):
                    gq[qr, :] = dq[b]
                    gk[kr, :] += dkk[b]
                    gv[kr, :] += dvv[b]
                return 0

            lax.fori_loop(0, t // BLOCK // DIL_B, tiles, 0)
            if g > 0:
                d = DIL_D[g]
                ln = t // d
                for r in range(d):
                    nat, dil = pl.ds(r, ln, stride=d), pl.ds(r * ln, ln)
                    on_s[0, nat, :] += tq_s[dil, :]
                    on_s[1, nat, :] += tk_s[dil, :]
                    dv_ref[nat, :] += tv_s[dil, :]

        def prep(c, acc):
            rows = pl.ds(pl.multiple_of(c * ROWS, ROWS), ROWS)
            f = lambda x, w: _dil_prep(x, w, cs_ref[rows, :], sn_ref[rows, :], mean2, pm2)
            _, vq = jax.vjp(f, q_ref[rows, :], qw)
            _, vk = jax.vjp(f, k_ref[rows, :], kw)
            dq, dqw = vq(on_s[0, rows, :])
            dk, dkw = vk(on_s[1, rows, :])
            dq_ref[rows, :] = dq
            dk_ref[rows, :] = dk
            return acc[0] + dqw, acc[1] + dkw

        dqw, dkw = lax.fori_loop(0, t // ROWS, prep, (jnp.zeros((1, PAIR), F32), jnp.zeros((1, PAIR), F32)))
        dqw_ref[0] = dqw
        dkw_ref[0] = dkw

    qkv, vec, tab, mat, pair = _dil_specs(t)
    per = C_WIDTH // BLOCK
    wout = pl.BlockSpec((1, 1, PAIR), lambda b, p: (b * per + p, 0, 0))
    return pl.pallas_call(
        body, name="dil_bwd", grid=(nseq, per),
        in_specs=qkv + [vec, vec, tab, tab, mat, mat, pair],
        out_specs=[pair, pair, pair, wout, wout],
        out_shape=[jax.ShapeDtypeStruct((n, C_WIDTH), F32)] * 3 + [jax.ShapeDtypeStruct((nseq * per, 1, PAIR), F32)] * 2,
        scratch_shapes=_dil_scratch(t) + [pltpu.VMEM((t, PAIR), F32)] * 3,
        compiler_params=_cparams("parallel", "parallel"),
    )(proj, proj, proj, qw, kw, cs, sn, mean2, pm2, do)


N_CHIPS = 4
SUM_ROWS = 432
MESH_IDS = pl.DeviceIdType.MESH
ANY = pl.BlockSpec(memory_space=pl.ANY)


def plane_exchange(src, all_to_all):
    blk_shape = src.shape[1:] if all_to_all else src.shape

    def body(src_ref, out_ref, send_sems, recv_sems, local_sem):
        x, y, c = lax.axis_index("x"), lax.axis_index("y"), lax.axis_index("c")
        me = 2 * x + y
        mine = pltpu.make_async_copy(src_ref.at[me] if all_to_all else src_ref, out_ref.at[me], local_sem)
        mine.start()
        sends = []
        for k in (1, 2, 3):
            px = 1 - x if k & 2 else x
            py = 1 - y if k & 1 else y
            peer = 2 * px + py
            cp = pltpu.make_async_remote_copy(
                src_ref=src_ref.at[peer] if all_to_all else src_ref, dst_ref=out_ref.at[me],
                send_sem=send_sems.at[k - 1], recv_sem=recv_sems.at[k - 1],
                device_id=(px, py, c), device_id_type=MESH_IDS)
            cp.start()
            sends.append((cp, peer, (px, py, c)))
        for k, (cp, peer, dev) in enumerate(sends):
            pltpu.make_async_remote_copy(
                src_ref=out_ref.at[me], dst_ref=out_ref.at[peer],
                send_sem=send_sems.at[k], recv_sem=recv_sems.at[k],
                device_id=dev, device_id_type=MESH_IDS).wait_recv()
        for cp, _, _ in sends:
            cp.wait_send()
        mine.wait()

    return pl.pallas_call(
        body, name="plane_all_to_all" if all_to_all else "plane_all_gather",
        in_specs=[ANY], out_specs=ANY,
        out_shape=jax.ShapeDtypeStruct((N_CHIPS,) + blk_shape, src.dtype),
        scratch_shapes=[pltpu.SemaphoreType.DMA((3,)), pltpu.SemaphoreType.DMA((3,)), pltpu.SemaphoreType.DMA],
    )(src)


def sibling_swap(src, other_half=False):
    shape = (src.shape[0], src.shape[1] // 2) + src.shape[2:] if other_half else src.shape

    def body(src_ref, out_ref, send_sem, recv_sem):
        x, y, c = lax.axis_index("x"), lax.axis_index("y"), lax.axis_index("c")
        part = src_ref.at[:, pl.ds((1 - c) * shape[1], shape[1])] if other_half else src_ref
        cp = pltpu.make_async_remote_copy(src_ref=part, dst_ref=out_ref, send_sem=send_sem, recv_sem=recv_sem,
                                          device_id=(x, y, 1 - c), device_id_type=MESH_IDS)
        cp.start()
        cp.wait()

    return pl.pallas_call(
        body, name="sibling_swap", in_specs=[ANY], out_specs=ANY,
        out_shape=jax.ShapeDtypeStruct(shape, src.dtype),
        scratch_shapes=[pltpu.SemaphoreType.DMA, pltpu.SemaphoreType.DMA],
    )(src)


def sum4(a):
    _, r, c = a.shape
    tr = SUM_ROWS

    def body(a_ref, o_ref):
        p = [a_ref[i].astype(F32) for i in range(N_CHIPS)]
        o_ref[...] = (p[0] + p[1]) + (p[2] + p[3])

    return pl.pallas_call(
        body, name="sum4", grid=(r // tr,),
        in_specs=[pl.BlockSpec((N_CHIPS, tr, c), lambda i: (0, i, 0))],
        out_specs=pl.BlockSpec((tr, c), lambda i: (i, 0)),
        out_shape=jax.ShapeDtypeStruct((r, c), F32),
        compiler_params=_cparams("parallel"),
    )(a)


def add_my_half(mine, got, c):
    nchip, r2, cols = mine.shape
    nt = r2 // 2 // SUM_ROWS

    def body(c_ref, a_ref, b_ref, o_ref):
        o_ref[...] = (a_ref[...] + b_ref[...]).astype(BF16)

    blk = pl.BlockSpec((1, SUM_ROWS, cols), lambda j, i, c_ref: (j, i, 0))
    return pl.pallas_call(
        body, name="add_my_half",
        grid_spec=pltpu.PrefetchScalarGridSpec(
            num_scalar_prefetch=1, grid=(nchip, nt),
            in_specs=[pl.BlockSpec((1, SUM_ROWS, cols), lambda j, i, c_ref: (j, c_ref[0] * nt + i, 0)), blk],
            out_specs=blk),
        out_shape=jax.ShapeDtypeStruct((nchip, r2 // 2, cols), BF16),
        compiler_params=_cparams("parallel", "parallel"),
    )(jnp.reshape(c, (1,)).astype(jnp.int32), mine, got)


PACK_COLS = 1152
PACK_ROWS = 2592
ROW_TILE = 16


def _pack(parts):
    blocks = []
    for p in parts:
        p2 = p.reshape(-1, p.shape[-1])
        blocks.append(jnp.pad(p2, ((0, -p2.shape[0] % ROW_TILE), (0, PACK_COLS - p2.shape[1]))))
    rows = sum(b.shape[0] for b in blocks)
    blocks.append(jnp.zeros((PACK_ROWS - rows, PACK_COLS), blocks[0].dtype))
    return jnp.concatenate(blocks)


def _unpack(buf, shapes):
    out, at = [], 0
    for s in shapes:
        rows = math.prod(s[:-1])
        out.append(buf[at:at + rows, :s[-1]].reshape(s))
        at += rows + (-rows % ROW_TILE)
    return out


def _pack_small(g):
    blk = jnp.zeros((ROW_TILE, PACK_COLS), F32)
    for i, k in enumerate(SMALL):
        blk = blk.at[2 * i:2 * i + 2, :g[k].shape[1]].set(g[k])
    return blk


def _unpack_small(blk, shapes):
    return [blk[2 * i:2 * i + 2, :s[1]] for i, s in enumerate(shapes)]


def _layer_fwd(x, p, nseq, tabs):
    proj, hdn = inproj_fwd(x, p["norm_w"][None], p["w_in"])
    ya = conv_fwd(proj, p["conv_w"], nseq)
    oa, states, inverses = gdn_fwd(ya, proj, p["a_log"].reshape(N_HEADS_A, 1, 1), p["dt_bias"].reshape(N_HEADS_A, 1, 1),
                         p["gdn_norm_w"][None], nseq)
    ob, carries = sb_fwd(proj, nseq)
    oc = dil_fwd(proj, jnp.tile(p["q_norm_w"], 2)[None], jnp.tile(p["k_norm_w"], 2)[None], *tabs, nseq)
    y, mixed = outproj_fwd(x, oa, ob, oc, proj, p["w_out"])
    return y, dict(x=x, hdn=hdn, proj=proj, ya=ya, states=states, inverses=inverses, carries=carries, oa=oa, ob=ob, oc=oc, mixed=mixed)


def _layer_bwd(dy, p, res, nseq, tabs):
    proj = res["proj"]
    g = {}
    g["w_out"] = mat_tn(res["mixed"], [dy])[0]
    doa, dob, doc, dza, dzb, dzc = outproj_bwd(dy, res["oa"], res["ob"], res["oc"], proj, p["w_out"])
    dqc, dkc, dvc, dqw, dkw = dil_bwd(proj, jnp.tile(p["q_norm_w"], 2)[None], jnp.tile(p["k_norm_w"], 2)[None], *tabs, doc,
                                      nseq)
    g["q_norm_w"], g["k_norm_w"] = dqw.reshape(-1, HEAD_DIM).sum(0), dkw.reshape(-1, HEAD_DIM).sum(0)
    dqb, dkb, dvb = sb_bwd(proj, res["carries"], dob, nseq)
    dya, dba, dalog, ddtb, dnw = gdn_bwd(res["ya"], proj, p["a_log"].reshape(N_HEADS_A, 1, 1),
                                         p["dt_bias"].reshape(N_HEADS_A, 1, 1), p["gdn_norm_w"][None], res["states"], res["inverses"], doa,
                                         nseq)
    g["a_log"], g["dt_bias"], g["gdn_norm_w"] = dalog.sum(0).reshape(-1), ddtb.sum(0).reshape(-1), dnw.sum((0, 1))
    dqkv, dcw = conv_bwd(proj, p["conv_w"], dya, nseq)
    g["conv_w"] = dcw.sum(0)
    slabs = [dqkv, dza, dqc, dkc, dvc, dzc, dqb, dkb, dvb, dzb, dba]
    hdn = res["hdn"]
    g["w_in"] = jnp.concatenate(mat_tn(hdn, slabs[:6]) + mat_tn(hdn, slabs[6:]), axis=1)
    dx, dnw_tiles = inproj_bwd(slabs, p["w_in"], res["x"], p["norm_w"][None], dy)
    g["norm_w"] = dnw_tiles.sum((0, 1))
    return dx, g


SMALL = ("norm_w", "a_log", "dt_bias", "gdn_norm_w", "q_norm_w", "k_norm_w")


def _local_step(x, target, full):
    nseq, t, d = x.shape
    tabs = _rope_tables(t)
    h = x.reshape(nseq * t, d)
    saved = []
    for l in range(DEPTH):
        p = {k: v[l] for k, v in full.items()}
        h, res = _layer_fwd(h, p, nseq, tabs)
        saved.append((p, res))
    dy, parts = loss_fwd_bwd(h, target.reshape(nseq * t, d))
    loss = parts[:, 0, 0].sum()
    grads = [None] * DEPTH
    for l in reversed(range(DEPTH)):
        p, res = saved[l]
        dy, grads[l] = _layer_bwd(dy, p, res, nseq, tabs)
    return loss, dy.reshape(nseq, t, d), {k: jnp.stack([g[k] for g in grads]) for k in grads[0]}


def _pad_cols(w):
    b0 = ORIG_A + ORIG_BA
    c0 = b0 + ORIG_B
    zeros = jnp.zeros(w.shape[:-1] + (BLOCK - ORIG_BA,), w.dtype)
    return jnp.concatenate([w[..., :ORIG_A], w[..., c0:], w[..., b0:c0], w[..., ORIG_A:b0], zeros], axis=-1)


def _unpad_cols(w):
    return jnp.concatenate([w[..., :COL_C], w[..., COL_BA:COL_BA + ORIG_BA], w[..., COL_B:COL_BA], w[..., COL_C:COL_B]],
                           axis=-1)


def kernel(x, norm_w, w_in, conv_w, a_log, dt_bias, gdn_norm_w, q_norm_w, k_norm_w, w_out, loss_target, m_norm_w, m_w_in, m_conv_w, m_a_log, m_dt_bias, m_gdn_norm_w, m_q_norm_w, m_k_norm_w, m_w_out, v_norm_w, v_w_in, v_conv_w, v_a_log, v_dt_bias, v_gdn_norm_w, v_q_norm_w, v_k_norm_w, v_w_out):
    weights = dict(norm_w=norm_w, w_in=w_in, conv_w=conv_w, a_log=a_log, dt_bias=dt_bias, gdn_norm_w=gdn_norm_w,
                   q_norm_w=q_norm_w, k_norm_w=k_norm_w, w_out=w_out)
    moms = dict(norm_w=m_norm_w, w_in=m_w_in, conv_w=m_conv_w, a_log=m_a_log, dt_bias=m_dt_bias,
                gdn_norm_w=m_gdn_norm_w, q_norm_w=m_q_norm_w, k_norm_w=m_k_norm_w, w_out=m_w_out)
    vars_ = dict(norm_w=v_norm_w, w_in=v_w_in, conv_w=v_conv_w, a_log=v_a_log, dt_bias=v_dt_bias,
                 gdn_norm_w=v_gdn_norm_w, q_norm_w=v_q_norm_w, k_norm_w=v_k_norm_w, w_out=v_w_out)
    names = list(weights)
    sharded = ("w_in", "w_out", "conv_w")
    shard_shapes = [weights[k].shape for k in sharded]

    c = lax.axis_index("c")
    half = PACK_ROWS // 2
    conv_bits = lax.bitcast_convert_type(conv_w, BF16).reshape(conv_w.shape[:2] + (2 * conv_w.shape[2],))
    shard = _pack([w_in.astype(BF16), w_out.astype(BF16), conv_bits])
    mine = plane_exchange(lax.dynamic_slice_in_dim(shard, c * half, half, axis=0), all_to_all=False)
    other = sibling_swap(mine)
    got = jnp.concatenate([jnp.where(c == 0, mine, other), jnp.where(c == 0, other, mine)], axis=1)
    per_chip = [_unpack(got[i], shard_shapes[:2] + [conv_bits.shape]) for i in range(N_CHIPS)]
    full = {k: weights[k] for k in SMALL}
    full["w_in"] = _pad_cols(jnp.concatenate([pc[0] for pc in per_chip], axis=2))
    full["w_out"] = jnp.concatenate([pc[1] for pc in per_chip], axis=1)
    full["conv_w"] = jnp.concatenate(
        [lax.bitcast_convert_type(pc[2].reshape(conv_w.shape + (2,)), F32) for pc in per_chip], axis=2)

    loss, grad_x, g = _local_step(x, loss_target, full)

    gw_in = _unpad_cols(g["w_in"])
    cols, rows = w_in.shape[2], w_out.shape[1]
    small = _pack_small(g)
    send = jnp.stack([_pack([gw_in[:, :, i * cols:(i + 1) * cols], g["w_out"][:, i * rows:(i + 1) * rows],
                             g["conv_w"][:, :, i * conv_w.shape[2]:(i + 1) * conv_w.shape[2]], small])
                      for i in range(N_CHIPS)])
    chip_sum = add_my_half(send, sibling_swap(send, other_half=True), c)
    mine = sum4(plane_exchange(chip_sum, all_to_all=True))
    other = sibling_swap(mine)
    total = jnp.concatenate([jnp.where(c == 0, mine, other), jnp.where(c == 0, other, mine)])
    reduced = _unpack(total, shard_shapes + [(ROW_TILE, PACK_COLS)])
    grads = dict(zip(sharded, reduced[:3]))
    grads.update(zip(SMALL, _unpack_small(reduced[3], [weights[k].shape for k in SMALL])))
    loss = lax.psum(loss, ("x", "y", "c"))

    def two_d(a):
        return a.reshape(-1, a.shape[-1])

    delta, new_m, new_v = {}, {}, {}
    for k in names:
        d_, m_, v_ = adamw(two_d(weights[k]), two_d(grads[k]), two_d(moms[k]), two_d(vars_[k]))
        delta[k], new_m[k], new_v[k] = (a.reshape(weights[k].shape) for a in (d_, m_, v_))
    return (loss, grad_x, *[grads[k] for k in names], *[delta[k] for k in names],
            *[new_m[k] for k in names], *[new_v[k] for k in names])
```

```python
import functools
import math

import jax
import jax.numpy as jnp
from jax import lax
from jax.experimental import pallas as pl
from jax.experimental.pallas import tpu as pltpu

F32 = jnp.float32
BF16 = jnp.bfloat16

D_MODEL = 1024
SEQ = 2048
DEPTH = 2
HEAD_DIM = 64
N_HEADS_A, N_HEADS_B, N_HEADS_C = 6, 4, 6
A_WIDTH, B_WIDTH, C_WIDTH = N_HEADS_A * HEAD_DIM, N_HEADS_B * HEAD_DIM, N_HEADS_C * HEAD_DIM
CONV_WIDTH = 4
GDN_CHUNK = 64
BLOCK = 128
ROPE_DIM = 16
ROPE_THETA = 500000.0
DILATED_PAIRS = ((128, 1), (512, 4), (2048, 16))
RMS_EPS = 1e-6
NEG = -1e30

NT = (((1,), (1,)), ((), ()))
NN = (((1,), (0,)), ((), ()))
TN = (((0,), (0,)), ((), ()))

VMEM_LIMIT = 48 * 1024 * 1024

ORIG_A = 4 * A_WIDTH
ORIG_BA = 2 * N_HEADS_A
ORIG_B = 4 * B_WIDTH
COL_AZ = 3 * A_WIDTH
COL_C = 4 * A_WIDTH
COL_B = COL_C + 4 * C_WIDTH
COL_BA = COL_B + 4 * B_WIDTH
P_COLS = COL_BA + BLOCK
TN_COLS = 384
INPROJ_PARTS = 4
TM_ROWS = 512
ROWS = 1024


def _mm(a, b, dims=NN):
    return lax.dot_general(a.astype(BF16), b.astype(BF16), dims, preferred_element_type=F32)


def _mm32(a, b, dims=NN):
    return lax.dot_general(a, b, dims, precision=lax.Precision.HIGH, preferred_element_type=F32)


def _cparams(*sem):
    return pltpu.CompilerParams(dimension_semantics=sem, vmem_limit_bytes=VMEM_LIMIT)


def _sigmoid(x):
    return 0.5 * (jnp.tanh(0.5 * x) + 1.0)


def _softplus(x):
    return jnp.maximum(x, 0.0) + jnp.log(1.0 + jnp.exp(-jnp.abs(x)))


def _rms(x, w):
    return x * lax.rsqrt(jnp.mean(x * x, axis=-1, keepdims=True) + RMS_EPS) * w


def _heads(a, n):
    return jnp.stack([a[:, h * HEAD_DIM:(h + 1) * HEAD_DIM] for h in range(n)])


def _unheads(a):
    return jnp.concatenate([a[h] for h in range(a.shape[0])], axis=1)


def inproj_fwd(x, nw, w):
    n, d = x.shape
    p = w.shape[1]

    def body(x_ref, nw_ref, w_ref, proj_ref, hdn_ref):
        step = TM_ROWS // INPROJ_PARTS
        for r in range(INPROJ_PARTS):
            rows = pl.ds(r * step, step)
            h = _rms(x_ref[rows, :], nw_ref[...]).astype(BF16)
            hdn_ref[rows, :] = h
            proj_ref[rows, :] = jnp.dot(h, w_ref[...], preferred_element_type=F32)

    return pl.pallas_call(
        body, name="inproj_fwd", grid=(n // TM_ROWS,),
        in_specs=[pl.BlockSpec((TM_ROWS, d), lambda i: (i, 0)), pl.BlockSpec((1, d), lambda i: (0, 0)),
                  pl.BlockSpec((d, p), lambda i: (0, 0))],
        out_specs=[pl.BlockSpec((TM_ROWS, p), lambda i: (i, 0)), pl.BlockSpec((TM_ROWS, d), lambda i: (i, 0))],
        out_shape=[jax.ShapeDtypeStruct((n, p), F32), jax.ShapeDtypeStruct((n, d), BF16)],
        compiler_params=_cparams("parallel"),
    )(x, nw, w)


def mat_tn(a, slabs):
    n, ka = a.shape
    ns = len(slabs)

    def body(*refs):
        a_ref, s_refs, o_refs = refs[0], refs[1:1 + ns], refs[1 + ns:]

        @pl.when(pl.program_id(0) == 0)
        def _():
            for o_ref in o_refs:
                o_ref[...] = jnp.zeros_like(o_ref)

        at = a_ref[...].T
        for s_ref, o_ref in zip(s_refs, o_refs):
            o_ref[...] += jnp.dot(at, s_ref[...].astype(BF16), preferred_element_type=F32)

    return pl.pallas_call(
        body, name="mat_tn", grid=(n // TM_ROWS,),
        in_specs=[pl.BlockSpec((TM_ROWS, ka), lambda k: (k, 0))]
                 + [pl.BlockSpec((TM_ROWS, s.shape[1]), lambda k: (k, 0)) for s in slabs],
        out_specs=[pl.BlockSpec((ka, s.shape[1]), lambda k: (0, 0)) for s in slabs],
        out_shape=[jax.ShapeDtypeStruct((ka, s.shape[1]), F32) for s in slabs],
        compiler_params=_cparams("arbitrary"),
    )(a, *slabs)


def inproj_bwd(slabs, w, x, nw, dy):
    n, d = x.shape
    p = w.shape[1]
    tm = 256
    ns = len(slabs)

    def body(*refs):
        s_refs = refs[:ns]
        w_ref, x_ref, nw_ref, dy_ref, dx_ref, dnw_ref = refs[ns:]
        dh = jnp.zeros((tm, d), F32)
        at = 0
        for s_ref in s_refs:
            wd = s_ref.shape[1]
            dh = dh + lax.dot_general(s_ref[...].astype(BF16), w_ref[:, at:at + wd], NT, preferred_element_type=F32)
            at += wd
        _, vjp = jax.vjp(_rms, x_ref[...], nw_ref[...])
        dx, dnw = vjp(dh)
        dx_ref[...] = dx + dy_ref[...]
        dnw_ref[0] = dnw

    return pl.pallas_call(
        body, name="inproj_bwd", grid=(n // tm,),
        in_specs=[pl.BlockSpec((tm, s.shape[1]), lambda i: (i, 0)) for s in slabs]
                 + [pl.BlockSpec((d, p), lambda i: (0, 0)), pl.BlockSpec((tm, d), lambda i: (i, 0)),
                    pl.BlockSpec((1, d), lambda i: (0, 0)), pl.BlockSpec((tm, d), lambda i: (i, 0))],
        out_specs=[pl.BlockSpec((tm, d), lambda i: (i, 0)), pl.BlockSpec((1, 1, d), lambda i: (i, 0, 0))],
        out_shape=[jax.ShapeDtypeStruct((n, d), F32), jax.ShapeDtypeStruct((n // tm, 1, d), F32)],
        compiler_params=_cparams("parallel"),
    )(*slabs, w, x, nw, dy)


CONV_PAD = 8
CONV_ROWS = 256


def _conv_pre(pad_s, cw, c):
    xs = [pad_s[pl.ds(c * CONV_ROWS + CONV_PAD - (CONV_WIDTH - 1) + k, CONV_ROWS), :] for k in range(CONV_WIDTH)]
    pre = xs[0] * cw[0:1, :]
    for k in range(1, CONV_WIDTH):
        pre = pre + xs[k] * cw[k:k + 1, :]
    return pre, xs


def conv_fwd(proj, cw, nseq):
    n = proj.shape[0]
    t = n // nseq
    ch = cw.shape[1]

    def body(x_ref, cw_ref, y_ref, pad_s):
        pad_s[pl.ds(0, CONV_PAD), :] = jnp.zeros((CONV_PAD, TN_COLS), F32)
        pad_s[pl.ds(CONV_PAD, t), :] = x_ref[...]
        cwv = cw_ref[...]
        for c in range(t // CONV_ROWS):
            pre, _ = _conv_pre(pad_s, cwv, c)
            y_ref[pl.ds(c * CONV_ROWS, CONV_ROWS), :] = pre * _sigmoid(pre)

    return pl.pallas_call(
        body, name="conv_fwd", grid=(nseq, ch // TN_COLS),
        in_specs=[pl.BlockSpec((t, TN_COLS), lambda b, j: (b, j)), pl.BlockSpec((CONV_WIDTH, TN_COLS), lambda b, j: (0, j))],
        out_specs=pl.BlockSpec((t, TN_COLS), lambda b, j: (b, j)),
        out_shape=jax.ShapeDtypeStruct((n, ch), F32),
        scratch_shapes=[pltpu.VMEM((t + CONV_PAD, TN_COLS), F32)],
        compiler_params=_cparams("parallel", "parallel"),
    )(proj, cw)


def conv_bwd(proj, cw, dy, nseq):
    n = proj.shape[0]
    t = n // nseq
    ch = cw.shape[1]

    def body(x_ref, cw_ref, dy_ref, dx_ref, dcw_ref, pad_s, dpad_s):
        pad_s[pl.ds(0, CONV_PAD), :] = jnp.zeros((CONV_PAD, TN_COLS), F32)
        pad_s[pl.ds(CONV_PAD, t), :] = x_ref[...]
        dpad_s[pl.ds(t, CONV_PAD), :] = jnp.zeros((CONV_PAD, TN_COLS), F32)
        cwv = cw_ref[...]
        acc = [jnp.zeros((1, TN_COLS), F32)] * CONV_WIDTH
        for c in range(t // CONV_ROWS):
            pre, xs = _conv_pre(pad_s, cwv, c)
            sg = _sigmoid(pre)
            dpre = dy_ref[pl.ds(c * CONV_ROWS, CONV_ROWS), :] * (sg * (1.0 + pre * (1.0 - sg)))
            dpad_s[pl.ds(c * CONV_ROWS, CONV_ROWS), :] = dpre
            acc = [acc[k] + jnp.sum(dpre * xs[k], axis=0, keepdims=True) for k in range(CONV_WIDTH)]
        for k in range(CONV_WIDTH):
            dcw_ref[0, pl.ds(k, 1), :] = acc[k]
        for c in range(t // CONV_ROWS):
            dx = dpad_s[pl.ds(c * CONV_ROWS + CONV_WIDTH - 1, CONV_ROWS), :] * cwv[0:1, :]
            for k in range(1, CONV_WIDTH):
                dx = dx + dpad_s[pl.ds(c * CONV_ROWS + CONV_WIDTH - 1 - k, CONV_ROWS), :] * cwv[k:k + 1, :]
            dx_ref[pl.ds(c * CONV_ROWS, CONV_ROWS), :] = dx

    blk = pl.BlockSpec((t, TN_COLS), lambda b, j: (b, j))
    return pl.pallas_call(
        body, name="conv_bwd", grid=(nseq, ch // TN_COLS),
        in_specs=[blk, pl.BlockSpec((CONV_WIDTH, TN_COLS), lambda b, j: (0, j)), blk],
        out_specs=[blk, pl.BlockSpec((1, CONV_WIDTH, TN_COLS), lambda b, j: (b, 0, j))],
        out_shape=[jax.ShapeDtypeStruct((n, ch), F32), jax.ShapeDtypeStruct((nseq, CONV_WIDTH, ch), F32)],
        scratch_shapes=[pltpu.VMEM((t + CONV_PAD, TN_COLS), F32)] * 2,
        compiler_params=_cparams("parallel", "parallel"),
    )(proj, cw, dy)


def _gate_specs(d):
    wide = pl.BlockSpec((TM_ROWS, d), lambda i: (i, 0))
    oa = pl.BlockSpec((TM_ROWS, A_WIDTH), lambda i: (i, 0))
    ob = pl.BlockSpec((TM_ROWS, B_WIDTH), lambda i: (i, 0))
    oc = pl.BlockSpec((TM_ROWS, C_WIDTH), lambda i: (i, 0))
    za = pl.BlockSpec((TM_ROWS, A_WIDTH), lambda i: (i, COL_AZ // A_WIDTH))
    zb = pl.BlockSpec((TM_ROWS, B_WIDTH), lambda i: (i, (COL_B + 3 * B_WIDTH) // B_WIDTH))
    zc = pl.BlockSpec((TM_ROWS, C_WIDTH), lambda i: (i, (COL_C + 3 * C_WIDTH) // C_WIDTH))
    return wide, oa, ob, oc, za, zb, zc


BRANCH_COLS = ((0, A_WIDTH), (A_WIDTH, A_WIDTH + B_WIDTH), (A_WIDTH + B_WIDTH, D_MODEL))


def outproj_fwd(x, oa, ob, oc, proj, w):
    n, d = x.shape

    def body(x_ref, oa_ref, ob_ref, oc_ref, za_ref, zb_ref, zc_ref, w_ref, y_ref, m_ref):
        for (lo, hi), o_ref, z_ref in zip(BRANCH_COLS, (oa_ref, ob_ref, oc_ref), (za_ref, zb_ref, zc_ref)):
            zv = z_ref[...]
            m_ref[:, lo:hi] = (o_ref[...] * (zv * _sigmoid(zv))).astype(BF16)
        y_ref[...] = x_ref[...] + jnp.dot(m_ref[...], w_ref[...], preferred_element_type=F32)

    wide, sa, sb, sc, za, zb, zc = _gate_specs(d)
    return pl.pallas_call(
        body, name="outproj_fwd", grid=(n // TM_ROWS,),
        in_specs=[wide, sa, sb, sc, za, zb, zc, pl.BlockSpec((d, d), lambda i: (0, 0))],
        out_specs=[wide, wide],
        out_shape=[jax.ShapeDtypeStruct((n, d), F32), jax.ShapeDtypeStruct((n, d), BF16)],
        compiler_params=_cparams("parallel"),
    )(x, oa, ob, oc, proj, proj, proj, w)


def outproj_bwd(dy, oa, ob, oc, proj, w):
    n, d = dy.shape

    def body(dy_ref, oa_ref, ob_ref, oc_ref, za_ref, zb_ref, zc_ref, w_ref, doa_ref, dob_ref, doc_ref, dza_ref, dzb_ref, dzc_ref):
        dm = lax.dot_general(dy_ref[...].astype(BF16), w_ref[...], NT, preferred_element_type=F32)
        for (lo, hi), o_ref, z_ref, do_ref, dz_ref in zip(BRANCH_COLS, (oa_ref, ob_ref, oc_ref), (za_ref, zb_ref, zc_ref),
                                                          (doa_ref, dob_ref, doc_ref), (dza_ref, dzb_ref, dzc_ref)):
            zv = z_ref[...]
            sg = _sigmoid(zv)
            dmv = dm[:, lo:hi]
            do_ref[...] = dmv * (zv * sg)
            dz_ref[...] = dmv * o_ref[...] * (sg * (1.0 + zv * (1.0 - sg)))

    wide, sa, sb, sc, za, zb, zc = _gate_specs(d)
    sd = jax.ShapeDtypeStruct
    outs = [sd((n, A_WIDTH), F32), sd((n, B_WIDTH), F32), sd((n, C_WIDTH), F32)]
    return pl.pallas_call(
        body, name="outproj_bwd", grid=(n // TM_ROWS,),
        in_specs=[wide, sa, sb, sc, za, zb, zc, pl.BlockSpec((d, d), lambda i: (0, 0))],
        out_specs=[sa, sb, sc, sa, sb, sc],
        out_shape=outs + outs,
        compiler_params=_cparams("parallel"),
    )(dy, oa, ob, oc, proj, proj, proj, w)


def loss_fwd_bwd(y, target):
    n, d = y.shape

    def body(y_ref, t_ref, dy_ref, part_ref):
        e = y_ref[...] - t_ref[...]
        dy_ref[...] = e * (1.0 / d)
        part_ref[...] = jnp.zeros_like(part_ref) + 0.5 * jnp.sum(e * e) * (1.0 / d)

    blk = pl.BlockSpec((TM_ROWS, d), lambda i: (i, 0))
    return pl.pallas_call(
        body, name="loss", grid=(n // TM_ROWS,),
        in_specs=[blk, blk],
        out_specs=[blk, pl.BlockSpec((1, 8, BLOCK), lambda i: (i, 0, 0))],
        out_shape=[jax.ShapeDtypeStruct((n, d), F32), jax.ShapeDtypeStruct((n // TM_ROWS, 8, BLOCK), F32)],
        compiler_params=_cparams("parallel"),
    )(y, target)


ADAM_LR, ADAM_B1, ADAM_B2, ADAM_EPS, ADAM_WD, ADAM_STEP = 0.001, 0.9, 0.999, 1e-08, 0.01, 10


def adamw(w, g, m, v):
    r, c = w.shape
    tr = r if r <= 256 else 256

    def body(w_ref, g_ref, m_ref, v_ref, d_ref, nm_ref, nv_ref):
        gv = g_ref[...]
        nm = ADAM_B1 * m_ref[...] + (1.0 - ADAM_B1) * gv
        nv = ADAM_B2 * v_ref[...] + (1.0 - ADAM_B2) * (gv * gv)
        m_hat = nm / (1.0 - ADAM_B1 ** ADAM_STEP)
        v_hat = nv / (1.0 - ADAM_B2 ** ADAM_STEP)
        d_ref[...] = -ADAM_LR * (m_hat / (jnp.sqrt(v_hat) + ADAM_EPS) + ADAM_WD * w_ref[...])
        nm_ref[...] = nm
        nv_ref[...] = nv

    blk = pl.BlockSpec((tr, c), lambda i: (i, 0))
    return pl.pallas_call(
        body, name="adamw", grid=(r // tr,),
        in_specs=[blk] * 4, out_specs=[blk] * 3,
        out_shape=[jax.ShapeDtypeStruct((r, c), F32)] * 3,
        compiler_params=_cparams("parallel"),
    )(w, g, m, v)


SB_G = N_HEADS_B


def _sb_weights(qs, k, carry, tri, diag):
    z = _mm(qs, k, NT)
    sp = jnp.log(1.0 + jnp.exp(-jnp.abs(z)))
    ls_pos = jnp.minimum(z, 0.0) - sp
    ls_neg = jnp.minimum(-z, 0.0) - sp
    earlier = (lax.broadcasted_iota(jnp.int32, z.shape, 1) < lax.broadcasted_iota(jnp.int32, z.shape, 0)) if diag else None
    log_keep = jnp.where(earlier, ls_neg, 0.0) if diag else ls_neg
    hi = log_keep.astype(BF16)
    lo = (log_keep - hi.astype(F32)).astype(BF16)
    within = lax.dot_general(jnp.concatenate([hi, lo], axis=1), tri, NN, preferred_element_type=F32)
    arg = ls_pos + within + carry
    wts = jnp.where(earlier, jnp.exp(jnp.where(earlier, arg, 0.0)), 0.0) if diag else jnp.exp(arg)
    return ls_pos, ls_neg, log_keep, wts, earlier


def _sb_tile(q, k, v, carry, tri, diag):
    _, _, log_keep, wts, _ = _sb_weights(q * (HEAD_DIM ** -0.5), k, carry, tri, diag)
    return _mm(wts, v), jnp.sum(log_keep, axis=1, keepdims=True)


def _sb_tile_grads(q, k, v, carry, do, dtot, tri, diag):
    qs = q * (HEAD_DIM ** -0.5)
    ls_pos, ls_neg, _, wts, earlier = _sb_weights(qs, k, carry, tri, diag)
    dv = _mm(wts, do, TN)
    darg = _mm(do, v, NT) * wts
    dkeep = _mm(darg, tri[:BLOCK], NT) + dtot
    if diag:
        dkeep = jnp.where(earlier, dkeep, 0.0)
    dz = darg * jnp.exp(ls_neg) - dkeep * jnp.exp(ls_pos)
    return _mm(dz, k) * (HEAD_DIM ** -0.5), _mm(dz, qs, TN), dv, jnp.sum(darg, axis=1, keepdims=True)


_sb_tiles_diag = jax.vmap(functools.partial(_sb_tile, diag=True), in_axes=(0, 0, 0, 0, None))
_sb_tiles_off = jax.vmap(functools.partial(_sb_tile, diag=False), in_axes=(0, 0, 0, 0, None))
_sb_grads_diag = jax.vmap(functools.partial(_sb_tile_grads, diag=True), in_axes=(0, 0, 0, 0, 0, 0, None))
_sb_grads_off = jax.vmap(functools.partial(_sb_tile_grads, diag=False), in_axes=(0, 0, 0, 0, 0, 0, None))


def _sb_tri():
    r = lax.broadcasted_iota(jnp.int32, (2 * BLOCK, BLOCK), 0) % BLOCK
    c = lax.broadcasted_iota(jnp.int32, (2 * BLOCK, BLOCK), 1)
    return jnp.where(r > c, 1.0, 0.0).astype(BF16)


SB_SEQ = 2


def _sb_specs(t, nq):
    cb = COL_B // B_WIDTH
    sq = SB_SEQ
    q = pl.BlockSpec((sq, BLOCK, B_WIDTH), lambda b, i: (b, i, cb))
    k = pl.BlockSpec((sq, t, B_WIDTH), lambda b, i: (b, 0, cb + 1))
    v = pl.BlockSpec((sq, t, B_WIDTH), lambda b, i: (b, 0, cb + 2))
    blk = pl.BlockSpec((sq, BLOCK, B_WIDTH), lambda b, i: (b, i, 0))
    full = pl.BlockSpec((sq, t, B_WIDTH), lambda b, i: (b, 0, 0))
    carry = pl.BlockSpec((sq, 1, nq, BLOCK, SB_G), lambda b, i: (b, i, 0, 0, 0))
    return q, k, v, blk, full, carry


def _sb_heads(ref, rows):
    return jnp.concatenate([_heads(ref[b, rows, :], SB_G) for b in range(SB_SEQ)])


def _sb_unheads(a):
    return [_unheads(a[b * SB_G:(b + 1) * SB_G]) for b in range(SB_SEQ)]


def sb_fwd(proj, nseq):
    n = proj.shape[0]
    t = n // nseq
    nq = t // BLOCK
    g, sq = SB_G, SB_SEQ
    everything = pl.ds(0, BLOCK)

    def body(q_ref, k_ref, v_ref, o_ref, carry_ref):
        i = pl.program_id(1)
        tri = _sb_tri()
        qv = _sb_heads(q_ref, everything)

        def tile(j, c, fn):
            rows = pl.ds(pl.multiple_of(j * BLOCK, BLOCK), BLOCK)
            for b in range(sq):
                carry_ref[b, 0, j] = jnp.concatenate([c[b * g + h] for h in range(g)], axis=1)
            return fn(qv, _sb_heads(k_ref, rows), _sb_heads(v_ref, rows), c, tri)

        def step(it, st):
            o_acc, c = st
            o, tot = tile(i - 1 - it, c, _sb_tiles_off)
            return o_acc + o, c + tot

        o_acc, _ = lax.fori_loop(0, i, step, tile(i, jnp.zeros((sq * g, BLOCK, 1), F32), _sb_tiles_diag))
        for b, o in enumerate(_sb_unheads(o_acc)):
            o_ref[b] = o

    q, k, v, blk, _, carry = _sb_specs(t, nq)
    proj3 = proj.reshape(nseq, t, -1)
    o, carries = pl.pallas_call(
        body, name="sb_fwd", grid=(nseq // sq, nq),
        in_specs=[q, k, v],
        out_specs=[blk, carry],
        out_shape=[jax.ShapeDtypeStruct((nseq, t, B_WIDTH), F32),
                   jax.ShapeDtypeStruct((nseq, nq, nq, BLOCK, g), F32)],
        compiler_params=_cparams("parallel", "arbitrary"),
    )(proj3, proj3, proj3)
    return o.reshape(n, B_WIDTH), carries


def sb_bwd(proj, carries, do, nseq):
    n = proj.shape[0]
    t = n // nseq
    nq = t // BLOCK
    g, sq = SB_G, SB_SEQ
    everything = pl.ds(0, BLOCK)

    def body(q_ref, k_ref, v_ref, carry_ref, do_ref, dq_ref, dk_ref, dv_ref):
        i = pl.program_id(1)

        @pl.when(i == 0)
        def _():
            dk_ref[...] = jnp.zeros_like(dk_ref)
            dv_ref[...] = jnp.zeros_like(dv_ref)

        tri = _sb_tri()
        qv = _sb_heads(q_ref, everything)
        dov = _sb_heads(do_ref, everything)

        def tile(j, st, fn):
            dq_acc, dc = st
            rows = pl.ds(pl.multiple_of(j * BLOCK, BLOCK), BLOCK)
            cj = [carry_ref[b, 0, j] for b in range(sq)]
            dq, dk, dv, dcj = fn(qv, _sb_heads(k_ref, rows), _sb_heads(v_ref, rows),
                                 jnp.stack([cj[b][:, h:h + 1] for b in range(sq) for h in range(g)]), dov, dc, tri)
            for b, (dkb, dvb) in enumerate(zip(_sb_unheads(dk), _sb_unheads(dv))):
                dk_ref[b, rows, :] += dkb
                dv_ref[b, rows, :] += dvb
            return dq_acc + dq, dc + dcj

        st = lax.fori_loop(0, i, lambda j, st: tile(j, st, _sb_grads_off),
                           (jnp.zeros((sq * g, BLOCK, HEAD_DIM), F32), jnp.zeros((sq * g, BLOCK, 1), F32)))
        dq_acc, _ = tile(i, st, _sb_grads_diag)
        for b, dq in enumerate(_sb_unheads(dq_acc)):
            dq_ref[b] = dq

    q, k, v, blk, full, carry = _sb_specs(t, nq)
    proj3, do3 = proj.reshape(nseq, t, -1), do.reshape(nseq, t, -1)
    grads = pl.pallas_call(
        body, name="sb_bwd", grid=(nseq // sq, nq),
        in_specs=[q, k, v, carry, blk],
        out_specs=[blk, full, full],
        out_shape=[jax.ShapeDtypeStruct((nseq, t, B_WIDTH), F32)] * 3,
        compiler_params=_cparams("parallel", "arbitrary"),
    )(proj3, proj3, proj3, carries, do3)
    return [a.reshape(n, B_WIDTH) for a in grads]


def _unit_lower_inverse(a):
    n = a.shape[0]
    eye = jnp.where(lax.broadcasted_iota(jnp.int32, (n, n), 0) == lax.broadcasted_iota(jnp.int32, (n, n), 1), 1.0, 0.0)
    tmat = eye.astype(F32) - a
    p = a
    for _ in range(5):
        p = _mm32(p, p)
        tmat = tmat + _mm32(tmat, p)
    return tmat


@jax.custom_vjp
def _known_inverse(a, tmat):
    return tmat


def _known_inverse_fwd(a, tmat):
    return tmat, tmat


def _known_inverse_bwd(tmat, g):
    return -_mm32(_mm32(tmat, g, TN), tmat, NT), jnp.zeros_like(tmat)


_known_inverse.defvjp(_known_inverse_fwd, _known_inverse_bwd)


def _gdn_chunk(q, k, v, al_c, al_r, br_c, alog, dtb, nw, s, tmat_in):
    c = GDN_CHUNK
    ri = lax.broadcasted_iota(jnp.int32, (c, c), 0)
    ci = lax.broadcasted_iota(jnp.int32, (c, c), 1)
    incl, strict = ri >= ci, ri > ci
    rate = -jnp.exp(alog)
    g_c = rate * _softplus(al_c + dtb)
    g_r = rate * _softplus(al_r + dtb)
    beta = _sigmoid(br_c)
    gc_c = jnp.sum(jnp.where(incl, g_r, 0.0), axis=1, keepdims=True)
    gc_r = jnp.sum(jnp.where(ri <= ci, g_c, 0.0), axis=0, keepdims=True)
    gl = jnp.sum(g_r, axis=1, keepdims=True)
    decay = jnp.where(incl, jnp.exp(jnp.where(incl, gc_c - gc_r, 0.0)), 0.0)
    qn = q * lax.rsqrt(jnp.sum(q * q, axis=-1, keepdims=True) + RMS_EPS) * (HEAD_DIM ** -0.5)
    kn = k * lax.rsqrt(jnp.sum(k * k, axis=-1, keepdims=True) + RMS_EPS)
    kb = kn * beta
    a = jnp.where(strict, _mm(kb, kn, NT) * decay, 0.0)
    tmat = _unit_lower_inverse(a) if tmat_in is None else _known_inverse(a, tmat_in)
    u = _mm(tmat, v * beta)
    w = _mm(tmat, kb * jnp.exp(gc_c))
    qk = _mm(qn, kn, NT) * decay
    v_new = u - _mm(w, s)
    o = _mm(qn * jnp.exp(gc_c), s) + _mm(qk, v_new)
    s_new = s * jnp.exp(gl) + _mm(kn * jnp.exp(gl - gc_c), v_new, TN)
    o = o * lax.rsqrt(jnp.mean(o * o, axis=-1, keepdims=True) + RMS_EPS) * nw
    return o, s_new, tmat


_gdn_chunks_fwd = jax.vmap(functools.partial(_gdn_chunk, tmat_in=None), in_axes=(0, 0, 0, 0, 0, 0, 0, 0, None, 0))
_gdn_chunks_bwd = jax.vmap(_gdn_chunk, in_axes=(0, 0, 0, 0, 0, 0, 0, 0, None, 0, 0))

GDN_TB = 256
GDN_SEQ_FWD = 2
GDN_SEQ_BWD = 2


def _gdn_block(q3, k3, v3, ba, alog, dtb, nw, s, tm=None):
    nh = N_HEADS_A
    ns = q3.shape[0]
    bat = [ba[b].T for b in range(ns)]
    br_c = jnp.stack([ba[b][:, h:h + 1] for b in range(ns) for h in range(nh)])
    al_c = jnp.stack([ba[b][:, nh + h:nh + h + 1] for b in range(ns) for h in range(nh)])
    al_r = jnp.stack([bat[b][nh + h:nh + h + 1, :] for b in range(ns) for h in range(nh)])
    heads = lambda a: jnp.concatenate([_heads(a[b], nh) for b in range(ns)])
    args = (heads(q3), heads(k3), heads(v3), al_c, al_r, br_c, jnp.concatenate([alog] * ns), jnp.concatenate([dtb] * ns), nw, s)
    o, s_new, tmat = _gdn_chunks_fwd(*args) if tm is None else _gdn_chunks_bwd(*args, tm)
    o3 = jnp.stack([_unheads(o[b * nh:(b + 1) * nh]) for b in range(ns)])
    return (o3, s_new, tmat) if tm is None else (o3, s_new)


def _gdn_specs(nt, sq, rev):
    tpos = (lambda i: nt - 1 - i) if rev else (lambda i: i)
    ncb = GDN_TB // GDN_CHUNK
    qkv = [pl.BlockSpec((sq, GDN_TB, A_WIDTH), lambda b, i, j=j: (b, tpos(i), j)) for j in range(3)]
    ba = pl.BlockSpec((sq, GDN_TB, BLOCK), lambda b, i: (b, tpos(i), COL_BA // BLOCK))
    one = pl.BlockSpec((N_HEADS_A, 1, 1), lambda b, i: (0, 0, 0))
    vec = pl.BlockSpec((1, HEAD_DIM), lambda b, i: (0, 0))
    st = pl.BlockSpec((sq, ncb, N_HEADS_A, HEAD_DIM, HEAD_DIM), lambda b, i: (b, tpos(i), 0, 0, 0))
    oa = pl.BlockSpec((sq, GDN_TB, A_WIDTH), lambda b, i: (b, tpos(i), 0))
    return qkv, ba, one, vec, st, oa, tpos


def gdn_fwd(ya, proj, alog, dtb, nw, nseq):
    n = ya.shape[0]
    t = n // nseq
    nc, nt, ncb = t // GDN_CHUNK, t // GDN_TB, GDN_TB // GDN_CHUNK
    sq = GDN_SEQ_FWD
    nh = N_HEADS_A

    def body(q_ref, k_ref, v_ref, ba_ref, alog_ref, dtb_ref, nw_ref, o_ref, st_ref, tm_ref, s_s):
        @pl.when(pl.program_id(1) == 0)
        def _():
            s_s[...] = jnp.zeros_like(s_s)

        def step(c, s):
            rows = pl.ds(pl.multiple_of(c * GDN_CHUNK, GDN_CHUNK), GDN_CHUNK)
            o, s_new, tmat = _gdn_block(q_ref[:, rows, :], k_ref[:, rows, :], v_ref[:, rows, :], ba_ref[:, rows, :],
                                        alog_ref[...], dtb_ref[...], nw_ref[...], s)
            for b in range(sq):
                st_ref[b, c] = s[b * nh:(b + 1) * nh]
                tm_ref[b, c] = tmat[b * nh:(b + 1) * nh]
            o_ref[:, rows, :] = o
            return s_new

        s_s[...] = lax.fori_loop(0, ncb, step, s_s[...])

    qkv, ba, one, vec, st, oa, _ = _gdn_specs(nt, sq, False)
    ya3, proj3 = ya.reshape(nseq, t, -1), proj.reshape(nseq, t, -1)
    per_chunk = jax.ShapeDtypeStruct((nseq, nc, nh, HEAD_DIM, HEAD_DIM), F32)
    o, states, inverses = pl.pallas_call(
        body, name="gdn_fwd", grid=(nseq // sq, nt),
        in_specs=qkv + [ba, one, one, vec],
        out_specs=[oa, st, st],
        out_shape=[jax.ShapeDtypeStruct((nseq, t, A_WIDTH), F32), per_chunk, per_chunk],
        scratch_shapes=[pltpu.VMEM((sq * nh, HEAD_DIM, HEAD_DIM), F32)],
        compiler_params=_cparams("parallel", "arbitrary"),
    )(ya3, ya3, ya3, proj3, alog, dtb, nw)
    return o.reshape(n, A_WIDTH), states, inverses


def gdn_bwd(ya, proj, alog, dtb, nw, states, inverses, do, nseq):
    n = ya.shape[0]
    t = n // nseq
    nt, ncb = t // GDN_TB, GDN_TB // GDN_CHUNK
    nh = N_HEADS_A
    sq = GDN_SEQ_BWD

    def body(q_ref, k_ref, v_ref, ba_ref, alog_ref, dtb_ref, nw_ref, st_ref, tm_ref, do_ref,
             dya_ref, dba_ref, dalog_ref, ddtb_ref, dnw_ref, ds_s):
        @pl.when(pl.program_id(1) == 0)
        def _():
            ds_s[...] = jnp.zeros_like(ds_s)
            dalog_ref[...] = jnp.zeros_like(dalog_ref)
            ddtb_ref[...] = jnp.zeros_like(ddtb_ref)
            dnw_ref[...] = jnp.zeros_like(dnw_ref)

        def step(it, carry):
            ds, dalog, ddtb, dnw = carry
            c = ncb - 1 - it
            rows = pl.ds(pl.multiple_of(c * GDN_CHUNK, GDN_CHUNK), GDN_CHUNK)
            s_in = jnp.concatenate([st_ref[b, c] for b in range(sq)])
            tm_in = jnp.concatenate([tm_ref[b, c] for b in range(sq)])
            _, vjp = jax.vjp(functools.partial(_gdn_block, tm=tm_in), q_ref[:, rows, :], k_ref[:, rows, :], v_ref[:, rows, :],
                             ba_ref[:, rows, :], alog_ref[...], dtb_ref[...], nw_ref[...], s_in)
            dq, dk, dv, dba, da, dd, dn, ds = vjp((do_ref[:, rows, :], ds))
            dya_ref[:, rows, 0:A_WIDTH] = dq
            dya_ref[:, rows, A_WIDTH:2 * A_WIDTH] = dk
            dya_ref[:, rows, 2 * A_WIDTH:3 * A_WIDTH] = dv
            dba_ref[:, rows, :] = dba
            return ds, dalog + da, ddtb + dd, dnw + dn

        z11 = jnp.zeros((nh, 1, 1), F32)
        ds, dalog, ddtb, dnw = lax.fori_loop(0, ncb, step, (ds_s[...], z11, z11, jnp.zeros((1, HEAD_DIM), F32)))
        ds_s[...] = ds
        dalog_ref[0] += dalog
        ddtb_ref[0] += ddtb
        dnw_ref[0] += dnw

    qkv, ba, one, vec, st, oa, tpos = _gdn_specs(nt, sq, True)
    per_grp = pl.BlockSpec((1, nh, 1, 1), lambda b, i: (b, 0, 0, 0))
    sd = jax.ShapeDtypeStruct
    ya3, proj3, do3 = ya.reshape(nseq, t, -1), proj.reshape(nseq, t, -1), do.reshape(nseq, t, -1)
    dya, dba, dalog, ddtb, dnw = pl.pallas_call(
        body, name="gdn_bwd", grid=(nseq // sq, nt),
        in_specs=qkv + [ba, one, one, vec, st, st, oa],
        out_specs=[pl.BlockSpec((sq, GDN_TB, 3 * A_WIDTH), lambda b, i: (b, tpos(i), 0)),
                   pl.BlockSpec((sq, GDN_TB, BLOCK), lambda b, i: (b, tpos(i), 0)),
                   per_grp, per_grp, pl.BlockSpec((1, 1, HEAD_DIM), lambda b, i: (b, 0, 0))],
        out_shape=[sd((nseq, t, 3 * A_WIDTH), F32), sd((nseq, t, BLOCK), F32), sd((nseq // sq, nh, 1, 1), F32),
                   sd((nseq // sq, nh, 1, 1), F32), sd((nseq // sq, 1, HEAD_DIM), F32)],
        scratch_shapes=[pltpu.VMEM((sq * nh, HEAD_DIM, HEAD_DIM), F32)],
        compiler_params=_cparams("parallel", "arbitrary"),
    )(ya3, ya3, ya3, proj3, alog, dtb, nw, states, inverses, do3)
    return dya.reshape(n, 3 * A_WIDTH), dba.reshape(n, BLOCK), dalog, ddtb, dnw


DIL_NB = tuple((SEQ // d) // BLOCK for _, d in DILATED_PAIRS)
DIL_D = tuple(d for _, d in DILATED_PAIRS)
DIL_STEPS = tuple(w // d for w, d in DILATED_PAIRS)
DIL_B = 8
PAIR = 2 * HEAD_DIM


def _rope_tables(t):
    half = ROPE_DIM // 2
    inv_freq = ROPE_THETA ** (-jnp.arange(half, dtype=F32) / half)
    ang = jnp.arange(t, dtype=F32)[:, None] * inv_freq[None, :]
    ones = jnp.ones((t, HEAD_DIM - ROPE_DIM), F32)
    cs = jnp.concatenate([jnp.cos(ang), jnp.cos(ang), ones], axis=1)
    sn = jnp.concatenate([jnp.sin(ang), jnp.sin(ang), 0.0 * ones], axis=1)
    i = jnp.arange(PAIR)[:, None]
    j = jnp.arange(PAIR)[None, :]
    same = (i // HEAD_DIM) == (j // HEAD_DIM)
    ih, jh = i % HEAD_DIM, j % HEAD_DIM
    pm = (jnp.where(same & (jh < half) & (ih == jh + half), -1.0, 0.0)
          + jnp.where(same & (jh >= half) & (jh < ROPE_DIM) & (ih == jh - half), 1.0, 0.0))
    mean = jnp.where(same, 1.0 / HEAD_DIM, 0.0)
    twice = lambda m: jnp.concatenate([m, m]).astype(BF16)
    return jnp.tile(cs, (1, 2)), jnp.tile(sn, (1, 2)), twice(mean), twice(pm)


def _split_dot(x, w2):
    hi = x.astype(BF16)
    lo = lax.stop_gradient(x - hi.astype(F32)).astype(BF16)
    return lax.dot_general(jnp.concatenate([hi, lo], axis=1), w2, NN, preferred_element_type=F32)


def _dil_prep(x, w, cs, sn, mean2, pm2):
    y = x * lax.rsqrt(_split_dot(x * x, mean2) + RMS_EPS) * w
    return y * cs + _split_dot(y, pm2) * sn


def _dil_tile(qn, kk, vv, bias):
    lane = lax.broadcasted_iota(jnp.int32, (1, PAIR), 1)
    outs, lses = [], []
    for h in range(2):
        s = _mm(jnp.where(lane // HEAD_DIM == h, qn, 0.0) * (HEAD_DIM ** -0.5), kk, NT) + bias
        m = lax.stop_gradient(jnp.max(s, axis=-1, keepdims=True))
        p = jnp.exp(s - m)
        denom = jnp.sum(p, axis=-1, keepdims=True)
        outs.append(_mm(p, vv) / denom)
        lses.append(m + jnp.log(denom))
    return jnp.where(lane < HEAD_DIM, outs[0], outs[1]), jnp.concatenate(lses, axis=1)


_dil_tiles = jax.vmap(_dil_tile)


def _spread(a):
    lane = lax.broadcasted_iota(jnp.int32, (a.shape[0], PAIR), 1)
    return jnp.where(lane < HEAD_DIM, a[:, 0:1], a[:, 1:2])


def _dil_mix(o1, o2, o3, l1, l2, l3):
    m = lax.stop_gradient(jnp.maximum(jnp.maximum(l1, l2), l3))
    e1, e2, e3 = jnp.exp(l1 - m), jnp.exp(l2 - m), jnp.exp(l3 - m)
    r = 1.0 / (e1 + e2 + e3)
    return _spread(e1 * r) * o1 + _spread(e2 * r) * o2 + _spread(e3 * r) * o3


def _dil_fill_biases(bias_s):
    steps, = set(DIL_STEPS)
    qi = lax.broadcasted_iota(jnp.int32, (BLOCK, 1), 0)
    kj = lax.broadcasted_iota(jnp.int32, (1, 2 * BLOCK), 1)
    rel = qi - kj + BLOCK
    inside = (rel >= 0) & (rel <= steps)
    bias_s[0] = jnp.where(inside, 0.0, NEG)
    bias_s[1] = jnp.where(inside & (kj >= BLOCK), 0.0, NEG)
    bias_s[2] = jnp.where((qi >= kj) & (qi - kj <= steps), 0.0, NEG)


def _dil_mask(it, g, bias_s):
    qrows = pl.ds(pl.multiple_of(it * BLOCK, BLOCK), BLOCK)
    if DIL_NB[g] == 1:
        return bias_s[2, :, 0:BLOCK], qrows, qrows
    which = jnp.where(it == 0, 2, jnp.where(it % DIL_NB[g] == 0, 1, 0))
    kstart = jnp.maximum(it - 1, 0) * BLOCK
    return bias_s[which], qrows, pl.ds(pl.multiple_of(kstart, BLOCK), 2 * BLOCK)


def _dil_gather(src, dst, d):
    t = src.shape[0]
    ln = t // d
    for r in range(d):
        dst[pl.ds(r * ln, ln), :] = src[pl.ds(r, ln, stride=d), :]


def _dil_scatter(src, dst, d):
    t = src.shape[0]
    ln = t // d
    for r in range(d):
        dst[pl.ds(r, ln, stride=d), :] = src[pl.ds(r * ln, ln), :]


def _dil_forward_parts(q_ref, k_ref, v_ref, qw, kw, cs_ref, sn_ref, mean2, pm2, qn_s, kn_s, dl_s, od_s, ld_s, on_s, ln_s, bias_s):
    t = qn_s.shape[0]
    _dil_fill_biases(bias_s)

    def prep(c, _):
        rows = pl.ds(pl.multiple_of(c * ROWS, ROWS), ROWS)
        qn_s[rows, :] = _dil_prep(q_ref[rows, :], qw, cs_ref[rows, :], sn_ref[rows, :], mean2, pm2)
        kn_s[rows, :] = _dil_prep(k_ref[rows, :], kw, cs_ref[rows, :], sn_ref[rows, :], mean2, pm2)
        return 0

    lax.fori_loop(0, t // ROWS, prep, 0)
    for g in (1, 2):
        _dil_gather(qn_s, dl_s.at[g - 1, 0], DIL_D[g])
        _dil_gather(kn_s, dl_s.at[g - 1, 1], DIL_D[g])
        _dil_gather(v_ref, dl_s.at[g - 1, 2], DIL_D[g])
    for g in range(3):
        qs = qn_s if g == 0 else dl_s.at[g - 1, 0]
        ks = kn_s if g == 0 else dl_s.at[g - 1, 1]
        vs = v_ref if g == 0 else dl_s.at[g - 1, 2]

        def tiles(i, _, g=g, qs=qs, ks=ks, vs=vs):
            where = [_dil_mask(i * DIL_B + b, g, bias_s) for b in range(DIL_B)]
            o, lse = _dil_tiles(jnp.stack([qs[qr, :] for _, qr, _ in where]), jnp.stack([ks[kr, :] for _, _, kr in where]),
                                jnp.stack([vs[kr, :] for _, _, kr in where]), jnp.stack([m for m, _, _ in where]))
            for b, (_, qr, _) in enumerate(where):
                od_s[g, qr, :] = o[b]
                ld_s[g, qr, :] = lse[b]
            return 0

        lax.fori_loop(0, t // BLOCK // DIL_B, tiles, 0)
    for g in (1, 2):
        _dil_scatter(od_s.at[g], on_s.at[g - 1], DIL_D[g])
        _dil_scatter(ld_s.at[g], ln_s.at[g - 1], DIL_D[g])


def _dil_scratch(t):
    return [pltpu.VMEM((t, PAIR), F32), pltpu.VMEM((t, PAIR), F32),
            pltpu.VMEM((2, 3, t, PAIR), F32),
            pltpu.VMEM((3, t, PAIR), F32), pltpu.VMEM((3, t, 2), F32),
            pltpu.VMEM((2, t, PAIR), F32), pltpu.VMEM((2, t, 2), F32),
            pltpu.VMEM((3, BLOCK, 2 * BLOCK), F32)]


def _dil_specs(t):
    cb = COL_C // BLOCK
    per = C_WIDTH // BLOCK
    qkv = [pl.BlockSpec((t, BLOCK), lambda b, p, j=j: (b, cb + j * per + p)) for j in range(3)]
    vec = pl.BlockSpec((1, PAIR), lambda b, p: (0, 0))
    tab = pl.BlockSpec((t, PAIR), lambda b, p: (0, 0))
    mat = pl.BlockSpec((2 * PAIR, PAIR), lambda b, p: (0, 0))
    pair = pl.BlockSpec((t, BLOCK), lambda b, p: (b, p))
    return qkv, vec, tab, mat, pair


def dil_fwd(proj, qw, kw, cs, sn, mean2, pm2, nseq):
    n = proj.shape[0]
    t = n // nseq

    def body(q_ref, k_ref, v_ref, qw_ref, kw_ref, cs_ref, sn_ref, mean_ref, pm_ref, o_ref,
             qn_s, kn_s, dl_s, od_s, ld_s, on_s, ln_s, bias_s):
        _dil_forward_parts(q_ref, k_ref, v_ref, qw_ref[...], kw_ref[...], cs_ref, sn_ref, mean_ref[...], pm_ref[...],
                           qn_s, kn_s, dl_s, od_s, ld_s, on_s, ln_s, bias_s)

        def mix(c, _):
            rows = pl.ds(pl.multiple_of(c * ROWS, ROWS), ROWS)
            o_ref[rows, :] = _dil_mix(od_s[0, rows, :], on_s[0, rows, :], on_s[1, rows, :],
                                      ld_s[0, rows, :], ln_s[0, rows, :], ln_s[1, rows, :])
            return 0

        lax.fori_loop(0, t // ROWS, mix, 0)

    qkv, vec, tab, mat, pair = _dil_specs(t)
    return pl.pallas_call(
        body, name="dil_fwd", grid=(nseq, C_WIDTH // BLOCK),
        in_specs=qkv + [vec, vec, tab, tab, mat, mat],
        out_specs=pair,
        out_shape=jax.ShapeDtypeStruct((n, C_WIDTH), F32),
        scratch_shapes=_dil_scratch(t),
        compiler_params=_cparams("parallel", "parallel"),
    )(proj, proj, proj, qw, kw, cs, sn, mean2, pm2)


def dil_bwd(proj, qw, kw, cs, sn, mean2, pm2, do, nseq):
    n = proj.shape[0]
    t = n // nseq

    def body(q_ref, k_ref, v_ref, qw_ref, kw_ref, cs_ref, sn_ref, mean_ref, pm_ref, do_ref,
             dq_ref, dk_ref, dv_ref, dqw_ref, dkw_ref,
             qn_s, kn_s, dl_s, od_s, ld_s, on_s, ln_s, bias_s, tq_s, tk_s, tv_s):
        qw, kw, mean2, pm2 = qw_ref[...], kw_ref[...], mean_ref[...], pm_ref[...]
        _dil_forward_parts(q_ref, k_ref, v_ref, qw, kw, cs_ref, sn_ref, mean2, pm2, qn_s, kn_s, dl_s, od_s, ld_s, on_s, ln_s, bias_s)

        def mix(c, _):
            rows = pl.ds(pl.multiple_of(c * ROWS, ROWS), ROWS)
            _, vjp = jax.vjp(_dil_mix, od_s[0, rows, :], on_s[0, rows, :], on_s[1, rows, :],
                             ld_s[0, rows, :], ln_s[0, rows, :], ln_s[1, rows, :])
            d1, d2, d3, e1, e2, e3 = vjp(do_ref[rows, :])
            od_s[0, rows, :] = d1
            on_s[0, rows, :] = d2
            on_s[1, rows, :] = d3
            ld_s[0, rows, :] = e1
            ln_s[0, rows, :] = e2
            ln_s[1, rows, :] = e3
            return 0

        lax.fori_loop(0, t // ROWS, mix, 0)
        for g in (1, 2):
            _dil_gather(on_s.at[g - 1], od_s.at[g], DIL_D[g])
            _dil_gather(ln_s.at[g - 1], ld_s.at[g], DIL_D[g])
        on_s[...] = jnp.zeros_like(on_s)
        dv_ref[...] = jnp.zeros_like(dv_ref)
        for g in range(3):
            qs = qn_s if g == 0 else dl_s.at[g - 1, 0]
            ks = kn_s if g == 0 else dl_s.at[g - 1, 1]
            vs = v_ref if g == 0 else dl_s.at[g - 1, 2]
            gq = on_s.at[0] if g == 0 else tq_s
            gk = on_s.at[1] if g == 0 else tk_s
            gv = dv_ref if g == 0 else tv_s
            if g > 0:
                tk_s[...] = jnp.zeros_like(tk_s)
                tv_s[...] = jnp.zeros_like(tv_s)

            def tiles(i, _, g=g, qs=qs, ks=ks, vs=vs, gq=gq, gk=gk, gv=gv):
                where = [_dil_mask(i * DIL_B + b, g, bias_s) for b in range(DIL_B)]
                biases = jnp.stack([m for m, _, _ in where])
                _, vjp = jax.vjp(lambda q_, k_, v_: _dil_tiles(q_, k_, v_, biases),
                                 jnp.stack([qs[qr, :] for _, qr, _ in where]), jnp.stack([ks[kr, :] for _, _, kr in where]),
                                 jnp.stack([vs[kr, :] for _, _, kr in where]))
                dq, dkk, dvv = vjp((jnp.stack([od_s[g, qr, :] for _, qr, _ in where]),
                                    jnp.stack([ld_s[g, qr, :] for _, qr, _ in where])))
                for b, (_, qr, kr) in enumerate(where):
                    gq[qr, :] = dq[b]
                    gk[kr, :] += dkk[b]
                    gv[kr, :] += dvv[b]
                return 0

            lax.fori_loop(0, t // BLOCK // DIL_B, tiles, 0)
            if g > 0:
                d = DIL_D[g]
                ln = t // d
                for r in range(d):
                    nat, dil = pl.ds(r, ln, stride=d), pl.ds(r * ln, ln)
                    on_s[0, nat, :] += tq_s[dil, :]
                    on_s[1, nat, :] += tk_s[dil, :]
                    dv_ref[nat, :] += tv_s[dil, :]

        def prep(c, acc):
            rows = pl.ds(pl.multiple_of(c * ROWS, ROWS), ROWS)
            f = lambda x, w: _dil_prep(x, w, cs_ref[rows, :], sn_ref[rows, :], mean2, pm2)
            _, vq = jax.vjp(f, q_ref[rows, :], qw)
            _, vk = jax.vjp(f, k_ref[rows, :], kw)
            dq, dqw = vq(on_s[0, rows, :])
            dk, dkw = vk(on_s[1, rows, :])
            dq_ref[rows, :] = dq
            dk_ref[rows, :] = dk
            return acc[0] + dqw, acc[1] + dkw

        dqw, dkw = lax.fori_loop(0, t // ROWS, prep, (jnp.zeros((1, PAIR), F32), jnp.zeros((1, PAIR), F32)))
        dqw_ref[0] = dqw
        dkw_ref[0] = dkw

    qkv, vec, tab, mat, pair = _dil_specs(t)
    per = C_WIDTH // BLOCK
    wout = pl.BlockSpec((1, 1, PAIR), lambda b, p: (b * per + p, 0, 0))
    return pl.pallas_call(
        body, name="dil_bwd", grid=(nseq, per),
        in_specs=qkv + [vec, vec, tab, tab, mat, mat, pair],
        out_specs=[pair, pair, pair, wout, wout],
        out_shape=[jax.ShapeDtypeStruct((n, C_WIDTH), F32)] * 3 + [jax.ShapeDtypeStruct((nseq * per, 1, PAIR), F32)] * 2,
        scratch_shapes=_dil_scratch(t) + [pltpu.VMEM((t, PAIR), F32)] * 3,
        compiler_params=_cparams("parallel", "parallel"),
    )(proj, proj, proj, qw, kw, cs, sn, mean2, pm2, do)


N_CHIPS = 4
SUM_ROWS = 432
MESH_IDS = pl.DeviceIdType.MESH
ANY = pl.BlockSpec(memory_space=pl.ANY)


def plane_exchange(src, all_to_all):
    blk_shape = src.shape[1:] if all_to_all else src.shape

    def body(src_ref, out_ref, send_sems, recv_sems, local_sem):
        x, y, c = lax.axis_index("x"), lax.axis_index("y"), lax.axis_index("c")
        me = 2 * x + y
        mine = pltpu.make_async_copy(src_ref.at[me] if all_to_all else src_ref, out_ref.at[me], local_sem)
        mine.start()
        sends = []
        for k in (1, 2, 3):
            px = 1 - x if k & 2 else x
            py = 1 - y if k & 1 else y
            peer = 2 * px + py
            cp = pltpu.make_async_remote_copy(
                src_ref=src_ref.at[peer] if all_to_all else src_ref, dst_ref=out_ref.at[me],
                send_sem=send_sems.at[k - 1], recv_sem=recv_sems.at[k - 1],
                device_id=(px, py, c), device_id_type=MESH_IDS)
            cp.start()
            sends.append((cp, peer, (px, py, c)))
        for k, (cp, peer, dev) in enumerate(sends):
            pltpu.make_async_remote_copy(
                src_ref=out_ref.at[me], dst_ref=out_ref.at[peer],
                send_sem=send_sems.at[k], recv_sem=recv_sems.at[k],
                device_id=dev, device_id_type=MESH_IDS).wait_recv()
        for cp, _, _ in sends:
            cp.wait_send()
        mine.wait()

    return pl.pallas_call(
        body, name="plane_all_to_all" if all_to_all else "plane_all_gather",
        in_specs=[ANY], out_specs=ANY,
        out_shape=jax.ShapeDtypeStruct((N_CHIPS,) + blk_shape, src.dtype),
        scratch_shapes=[pltpu.SemaphoreType.DMA((3,)), pltpu.SemaphoreType.DMA((3,)), pltpu.SemaphoreType.DMA],
    )(src)


def sibling_swap(src, other_half=False):
    shape = (src.shape[0], src.shape[1] // 2) + src.shape[2:] if other_half else src.shape

    def body(src_ref, out_ref, send_sem, recv_sem):
        x, y, c = lax.axis_index("x"), lax.axis_index("y"), lax.axis_index("c")
        part = src_ref.at[:, pl.ds((1 - c) * shape[1], shape[1])] if other_half else src_ref
        cp = pltpu.make_async_remote_copy(src_ref=part, dst_ref=out_ref, send_sem=send_sem, recv_sem=recv_sem,
                                          device_id=(x, y, 1 - c), device_id_type=MESH_IDS)
        cp.start()
        cp.wait()

    return pl.pallas_call(
        body, name="sibling_swap", in_specs=[ANY], out_specs=ANY,
        out_shape=jax.ShapeDtypeStruct(shape, src.dtype),
        scratch_shapes=[pltpu.SemaphoreType.DMA, pltpu.SemaphoreType.DMA],
    )(src)


def sum4(a):
    _, r, c = a.shape
    tr = SUM_ROWS

    def body(a_ref, o_ref):
        p = [a_ref[i].astype(F32) for i in range(N_CHIPS)]
        o_ref[...] = (p[0] + p[1]) + (p[2] + p[3])

    return pl.pallas_call(
        body, name="sum4", grid=(r // tr,),
        in_specs=[pl.BlockSpec((N_CHIPS, tr, c), lambda i: (0, i, 0))],
        out_specs=pl.BlockSpec((tr, c), lambda i: (i, 0)),
        out_shape=jax.ShapeDtypeStruct((r, c), F32),
        compiler_params=_cparams("parallel"),
    )(a)


def add_my_half(mine, got, c):
    nchip, r2, cols = mine.shape
    nt = r2 // 2 // SUM_ROWS

    def body(c_ref, a_ref, b_ref, o_ref):
        o_ref[...] = (a_ref[...] + b_ref[...]).astype(BF16)

    blk = pl.BlockSpec((1, SUM_ROWS, cols), lambda j, i, c_ref: (j, i, 0))
    return pl.pallas_call(
        body, name="add_my_half",
        grid_spec=pltpu.PrefetchScalarGridSpec(
            num_scalar_prefetch=1, grid=(nchip, nt),
            in_specs=[pl.BlockSpec((1, SUM_ROWS, cols), lambda j, i, c_ref: (j, c_ref[0] * nt + i, 0)), blk],
            out_specs=blk),
        out_shape=jax.ShapeDtypeStruct((nchip, r2 // 2, cols), BF16),
        compiler_params=_cparams("parallel", "parallel"),
    )(jnp.reshape(c, (1,)).astype(jnp.int32), mine, got)


PACK_COLS = 1152
PACK_ROWS = 2592
ROW_TILE = 16


def _pack(parts):
    blocks = []
    for p in parts:
        p2 = p.reshape(-1, p.shape[-1])
        blocks.append(jnp.pad(p2, ((0, -p2.shape[0] % ROW_TILE), (0, PACK_COLS - p2.shape[1]))))
    rows = sum(b.shape[0] for b in blocks)
    blocks.append(jnp.zeros((PACK_ROWS - rows, PACK_COLS), blocks[0].dtype))
    return jnp.concatenate(blocks)


def _unpack(buf, shapes):
    out, at = [], 0
    for s in shapes:
        rows = math.prod(s[:-1])
        out.append(buf[at:at + rows, :s[-1]].reshape(s))
        at += rows + (-rows % ROW_TILE)
    return out


def _pack_small(g):
    blk = jnp.zeros((ROW_TILE, PACK_COLS), F32)
    for i, k in enumerate(SMALL):
        blk = blk.at[2 * i:2 * i + 2, :g[k].shape[1]].set(g[k])
    return blk


def _unpack_small(blk, shapes):
    return [blk[2 * i:2 * i + 2, :s[1]] for i, s in enumerate(shapes)]


def _layer_fwd(x, p, nseq, tabs):
    proj, hdn = inproj_fwd(x, p["norm_w"][None], p["w_in"])
    ya = conv_fwd(proj, p["conv_w"], nseq)
    oa, states, inverses = gdn_fwd(ya, proj, p["a_log"].reshape(N_HEADS_A, 1, 1), p["dt_bias"].reshape(N_HEADS_A, 1, 1),
                         p["gdn_norm_w"][None], nseq)
    ob, carries = sb_fwd(proj, nseq)
    oc = dil_fwd(proj, jnp.tile(p["q_norm_w"], 2)[None], jnp.tile(p["k_norm_w"], 2)[None], *tabs, nseq)
    y, mixed = outproj_fwd(x, oa, ob, oc, proj, p["w_out"])
    return y, dict(x=x, hdn=hdn, proj=proj, ya=ya, states=states, inverses=inverses, carries=carries, oa=oa, ob=ob, oc=oc, mixed=mixed)


def _layer_bwd(dy, p, res, nseq, tabs):
    proj = res["proj"]
    g = {}
    g["w_out"] = mat_tn(res["mixed"], [dy])[0]
    doa, dob, doc, dza, dzb, dzc = outproj_bwd(dy, res["oa"], res["ob"], res["oc"], proj, p["w_out"])
    dqc, dkc, dvc, dqw, dkw = dil_bwd(proj, jnp.tile(p["q_norm_w"], 2)[None], jnp.tile(p["k_norm_w"], 2)[None], *tabs, doc,
                                      nseq)
    g["q_norm_w"], g["k_norm_w"] = dqw.reshape(-1, HEAD_DIM).sum(0), dkw.reshape(-1, HEAD_DIM).sum(0)
    dqb, dkb, dvb = sb_bwd(proj, res["carries"], dob, nseq)
    dya, dba, dalog, ddtb, dnw = gdn_bwd(res["ya"], proj, p["a_log"].reshape(N_HEADS_A, 1, 1),
                                         p["dt_bias"].reshape(N_HEADS_A, 1, 1), p["gdn_norm_w"][None], res["states"], res["inverses"], doa,
                                         nseq)
    g["a_log"], g["dt_bias"], g["gdn_norm_w"] = dalog.sum(0).reshape(-1), ddtb.sum(0).reshape(-1), dnw.sum((0, 1))
    dqkv, dcw = conv_bwd(proj, p["conv_w"], dya, nseq)
    g["conv_w"] = dcw.sum(0)
    slabs = [dqkv, dza, dqc, dkc, dvc, dzc, dqb, dkb, dvb, dzb, dba]
    hdn = res["hdn"]
    g["w_in"] = jnp.concatenate(mat_tn(hdn, slabs[:6]) + mat_tn(hdn, slabs[6:]), axis=1)
    dx, dnw_tiles = inproj_bwd(slabs, p["w_in"], res["x"], p["norm_w"][None], dy)
    g["norm_w"] = dnw_tiles.sum((0, 1))
    return dx, g


SMALL = ("norm_w", "a_log", "dt_bias", "gdn_norm_w", "q_norm_w", "k_norm_w")


def _local_step(x, target, full):
    nseq, t, d = x.shape
    tabs = _rope_tables(t)
    h = x.reshape(nseq * t, d)
    saved = []
    for l in range(DEPTH):
        p = {k: v[l] for k, v in full.items()}
        h, res = _layer_fwd(h, p, nseq, tabs)
        saved.append((p, res))
    dy, parts = loss_fwd_bwd(h, target.reshape(nseq * t, d))
    loss = parts[:, 0, 0].sum()
    grads = [None] * DEPTH
    for l in reversed(range(DEPTH)):
        p, res = saved[l]
        dy, grads[l] = _layer_bwd(dy, p, res, nseq, tabs)
    return loss, dy.reshape(nseq, t, d), {k: jnp.stack([g[k] for g in grads]) for k in grads[0]}


def _pad_cols(w):
    b0 = ORIG_A + ORIG_BA
    c0 = b0 + ORIG_B
    zeros = jnp.zeros(w.shape[:-1] + (BLOCK - ORIG_BA,), w.dtype)
    return jnp.concatenate([w[..., :ORIG_A], w[..., c0:], w[..., b0:c0], w[..., ORIG_A:b0], zeros], axis=-1)


def _unpad_cols(w):
    return jnp.concatenate([w[..., :COL_C], w[..., COL_BA:COL_BA + ORIG_BA], w[..., COL_B:COL_BA], w[..., COL_C:COL_B]],
                           axis=-1)


def kernel(x, norm_w, w_in, conv_w, a_log, dt_bias, gdn_norm_w, q_norm_w, k_norm_w, w_out, loss_target, m_norm_w, m_w_in, m_conv_w, m_a_log, m_dt_bias, m_gdn_norm_w, m_q_norm_w, m_k_norm_w, m_w_out, v_norm_w, v_w_in, v_conv_w, v_a_log, v_dt_bias, v_gdn_norm_w, v_q_norm_w, v_k_norm_w, v_w_out):
    weights = dict(norm_w=norm_w, w_in=w_in, conv_w=conv_w, a_log=a_log, dt_bias=dt_bias, gdn_norm_w=gdn_norm_w,
                   q_norm_w=q_norm_w, k_norm_w=k_norm_w, w_out=w_out)
    moms = dict(norm_w=m_norm_w, w_in=m_w_in, conv_w=m_conv_w, a_log=m_a_log, dt_bias=m_dt_bias,
                gdn_norm_w=m_gdn_norm_w, q_norm_w=m_q_norm_w, k_norm_w=m_k_norm_w, w_out=m_w_out)
    vars_ = dict(norm_w=v_norm_w, w_in=v_w_in, conv_w=v_conv_w, a_log=v_a_log, dt_bias=v_dt_bias,
                 gdn_norm_w=v_gdn_norm_w, q_norm_w=v_q_norm_w, k_norm_w=v_k_norm_w, w_out=v_w_out)
    names = list(weights)
    sharded = ("w_in", "w_out", "conv_w")
    shard_shapes = [weights[k].shape for k in sharded]

    c = lax.axis_index("c")
    half = PACK_ROWS // 2
    conv_bits = lax.bitcast_convert_type(conv_w, BF16).reshape(conv_w.shape[:2] + (2 * conv_w.shape[2],))
    shard = _pack([w_in.astype(BF16), w_out.astype(BF16), conv_bits])
    mine = plane_exchange(lax.dynamic_slice_in_dim(shard, c * half, half, axis=0), all_to_all=False)
    other = sibling_swap(mine)
    got = jnp.concatenate([jnp.where(c == 0, mine, other), jnp.where(c == 0, other, mine)], axis=1)
    per_chip = [_unpack(got[i], shard_shapes[:2] + [conv_bits.shape]) for i in range(N_CHIPS)]
    full = {k: weights[k] for k in SMALL}
    full["w_in"] = _pad_cols(jnp.concatenate([pc[0] for pc in per_chip], axis=2))
    full["w_out"] = jnp.concatenate([pc[1] for pc in per_chip], axis=1)
    full["conv_w"] = jnp.concatenate(
        [lax.bitcast_convert_type(pc[2].reshape(conv_w.shape + (2,)), F32) for pc in per_chip], axis=2)

    loss, grad_x, g = _local_step(x, loss_target, full)

    gw_in = _unpad_cols(g["w_in"])
    cols, rows = w_in.shape[2], w_out.shape[1]
    small = _pack_small(g)
    send = jnp.stack([_pack([gw_in[:, :, i * cols:(i + 1) * cols], g["w_out"][:, i * rows:(i + 1) * rows],
                             g["conv_w"][:, :, i * conv_w.shape[2]:(i + 1) * conv_w.shape[2]], small])
                      for i in range(N_CHIPS)])
    chip_sum = add_my_half(send, sibling_swap(send, other_half=True), c)
    mine = sum4(plane_exchange(chip_sum, all_to_all=True))
    other = sibling_swap(mine)
    total = jnp.concatenate([jnp.where(c == 0, mine, other), jnp.where(c == 0, other, mine)])
    reduced = _unpack(total, shard_shapes + [(ROW_TILE, PACK_COLS)])
    grads = dict(zip(sharded, reduced[:3]))
    grads.update(zip(SMALL, _unpack_small(reduced[3], [weights[k].shape for k in SMALL])))
    loss = lax.psum(loss, ("x", "y", "c"))

    def two_d(a):
        return a.reshape(-1, a.shape[-1])

    delta, new_m, new_v = {}, {}, {}
    for k in names:
        d_, m_, v_ = adamw(two_d(weights[k]), two_d(grads[k]), two_d(moms[k]), two_d(vars_[k]))
        delta[k], new_m[k], new_v[k] = (a.reshape(weights[k].shape) for a in (d_, m_, v_))
    return (loss, grad_x, *[grads[k] for k in names], *[delta[k] for k in names],
            *[new_m[k] for k in names], *[new_v[k] for k in names])
```

```python
import functools
import math

import jax
import jax.numpy as jnp
from jax import lax
from jax.experimental import pallas as pl
from jax.experimental.pallas import tpu as pltpu

F32 = jnp.float32
BF16 = jnp.bfloat16

D_MODEL = 1024
SEQ = 2048
DEPTH = 2
HEAD_DIM = 64
N_HEADS_A, N_HEADS_B, N_HEADS_C = 6, 4, 6
A_WIDTH, B_WIDTH, C_WIDTH = N_HEADS_A * HEAD_DIM, N_HEADS_B * HEAD_DIM, N_HEADS_C * HEAD_DIM
CONV_WIDTH = 4
GDN_CHUNK = 64
BLOCK = 128
ROPE_DIM = 16
ROPE_THETA = 500000.0
DILATED_PAIRS = ((128, 1), (512, 4), (2048, 16))
RMS_EPS = 1e-6
NEG = -1e30

NT = (((1,), (1,)), ((), ()))
NN = (((1,), (0,)), ((), ()))
TN = (((0,), (0,)), ((), ()))

VMEM_LIMIT = 48 * 1024 * 1024

ORIG_A = 4 * A_WIDTH
ORIG_BA = 2 * N_HEADS_A
ORIG_B = 4 * B_WIDTH
COL_AZ = 3 * A_WIDTH
COL_C = 4 * A_WIDTH
COL_B = COL_C + 4 * C_WIDTH
COL_BA = COL_B + 4 * B_WIDTH
P_COLS = COL_BA + BLOCK
TN_COLS = 384
INPROJ_PARTS = 4
TM_ROWS = 512
ROWS = 1024


def _mm(a, b, dims=NN):
    return lax.dot_general(a.astype(BF16), b.astype(BF16), dims, preferred_element_type=F32)


def _mm32(a, b, dims=NN):
    return lax.dot_general(a, b, dims, precision=lax.Precision.HIGH, preferred_element_type=F32)


def _cparams(*sem):
    return pltpu.CompilerParams(dimension_semantics=sem, vmem_limit_bytes=VMEM_LIMIT)


def _sigmoid(x):
    return 0.5 * (jnp.tanh(0.5 * x) + 1.0)


def _softplus(x):
    return jnp.maximum(x, 0.0) + jnp.log(1.0 + jnp.exp(-jnp.abs(x)))


def _rms(x, w):
    return x * lax.rsqrt(jnp.mean(x * x, axis=-1, keepdims=True) + RMS_EPS) * w


def _heads(a, n):
    return jnp.stack([a[:, h * HEAD_DIM:(h + 1) * HEAD_DIM] for h in range(n)])


def _unheads(a):
    return jnp.concatenate([a[h] for h in range(a.shape[0])], axis=1)


def inproj_fwd(x, nw, w):
    n, d = x.shape
    p = w.shape[1]

    def body(x_ref, nw_ref, w_ref, proj_ref, hdn_ref):
        step = TM_ROWS // INPROJ_PARTS
        for r in range(INPROJ_PARTS):
            rows = pl.ds(r * step, step)
            h = _rms(x_ref[rows, :], nw_ref[...]).astype(BF16)
            hdn_ref[rows, :] = h
            proj_ref[rows, :] = jnp.dot(h, w_ref[...], preferred_element_type=F32)

    return pl.pallas_call(
        body, name="inproj_fwd", grid=(n // TM_ROWS,),
        in_specs=[pl.BlockSpec((TM_ROWS, d), lambda i: (i, 0)), pl.BlockSpec((1, d), lambda i: (0, 0)),
                  pl.BlockSpec((d, p), lambda i: (0, 0))],
        out_specs=[pl.BlockSpec((TM_ROWS, p), lambda i: (i, 0)), pl.BlockSpec((TM_ROWS, d), lambda i: (i, 0))],
        out_shape=[jax.ShapeDtypeStruct((n, p), F32), jax.ShapeDtypeStruct((n, d), BF16)],
        compiler_params=_cparams("parallel"),
    )(x, nw, w)


def mat_tn(a, slabs):
    n, ka = a.shape
    ns = len(slabs)

    def body(*refs):
        a_ref, s_refs, o_refs = refs[0], refs[1:1 + ns], refs[1 + ns:]

        @pl.when(pl.program_id(0) == 0)
        def _():
            for o_ref in o_refs:
                o_ref[...] = jnp.zeros_like(o_ref)

        at = a_ref[...].T
        for s_ref, o_ref in zip(s_refs, o_refs):
            o_ref[...] += jnp.dot(at, s_ref[...].astype(BF16), preferred_element_type=F32)

    return pl.pallas_call(
        body, name="mat_tn", grid=(n // TM_ROWS,),
        in_specs=[pl.BlockSpec((TM_ROWS, ka), lambda k: (k, 0))]
                 + [pl.BlockSpec((TM_ROWS, s.shape[1]), lambda k: (k, 0)) for s in slabs],
        out_specs=[pl.BlockSpec((ka, s.shape[1]), lambda k: (0, 0)) for s in slabs],
        out_shape=[jax.ShapeDtypeStruct((ka, s.shape[1]), F32) for s in slabs],
        compiler_params=_cparams("arbitrary"),
    )(a, *slabs)


def inproj_bwd(slabs, w, x, nw, dy):
    n, d = x.shape
    p = w.shape[1]
    tm = 256
    ns = len(slabs)

    def body(*refs):
        s_refs = refs[:ns]
        w_ref, x_ref, nw_ref, dy_ref, dx_ref, dnw_ref = refs[ns:]
        dh = jnp.zeros((tm, d), F32)
        at = 0
        for s_ref in s_refs:
            wd = s_ref.shape[1]
            dh = dh + lax.dot_general(s_ref[...].astype(BF16), w_ref[:, at:at + wd], NT, preferred_element_type=F32)
            at += wd
        _, vjp = jax.vjp(_rms, x_ref[...], nw_ref[...])
        dx, dnw = vjp(dh)
        dx_ref[...] = dx + dy_ref[...]
        dnw_ref[0] = dnw

    return pl.pallas_call(
        body, name="inproj_bwd", grid=(n // tm,),
        in_specs=[pl.BlockSpec((tm, s.shape[1]), lambda i: (i, 0)) for s in slabs]
                 + [pl.BlockSpec((d, p), lambda i: (0, 0)), pl.BlockSpec((tm, d), lambda i: (i, 0)),
                    pl.BlockSpec((1, d), lambda i: (0, 0)), pl.BlockSpec((tm, d), lambda i: (i, 0))],
        out_specs=[pl.BlockSpec((tm, d), lambda i: (i, 0)), pl.BlockSpec((1, 1, d), lambda i: (i, 0, 0))],
        out_shape=[jax.ShapeDtypeStruct((n, d), F32), jax.ShapeDtypeStruct((n // tm, 1, d), F32)],
        compiler_params=_cparams("parallel"),
    )(*slabs, w, x, nw, dy)


CONV_PAD = 8
CONV_ROWS = 256


def _conv_pre(pad_s, cw, c):
    xs = [pad_s[pl.ds(c * CONV_ROWS + CONV_PAD - (CONV_WIDTH - 1) + k, CONV_ROWS), :] for k in range(CONV_WIDTH)]
    pre = xs[0] * cw[0:1, :]
    for k in range(1, CONV_WIDTH):
        pre = pre + xs[k] * cw[k:k + 1, :]
    return pre, xs


def conv_fwd(proj, cw, nseq):
    n = proj.shape[0]
    t = n // nseq
    ch = cw.shape[1]

    def body(x_ref, cw_ref, y_ref, pad_s):
        pad_s[pl.ds(0, CONV_PAD), :] = jnp.zeros((CONV_PAD, TN_COLS), F32)
        pad_s[pl.ds(CONV_PAD, t), :] = x_ref[...]
        cwv = cw_ref[...]
        for c in range(t // CONV_ROWS):
            pre, _ = _conv_pre(pad_s, cwv, c)
            y_ref[pl.ds(c * CONV_ROWS, CONV_ROWS), :] = pre * _sigmoid(pre)

    return pl.pallas_call(
        body, name="conv_fwd", grid=(nseq, ch // TN_COLS),
        in_specs=[pl.BlockSpec((t, TN_COLS), lambda b, j: (b, j)), pl.BlockSpec((CONV_WIDTH, TN_COLS), lambda b, j: (0, j))],
        out_specs=pl.BlockSpec((t, TN_COLS), lambda b, j: (b, j)),
        out_shape=jax.ShapeDtypeStruct((n, ch), F32),
        scratch_shapes=[pltpu.VMEM((t + CONV_PAD, TN_COLS), F32)],
        compiler_params=_cparams("parallel", "parallel"),
    )(proj, cw)


def conv_bwd(proj, cw, dy, nseq):
    n = proj.shape[0]
    t = n // nseq
    ch = cw.shape[1]

    def body(x_ref, cw_ref, dy_ref, dx_ref, dcw_ref, pad_s, dpad_s):
        pad_s[pl.ds(0, CONV_PAD), :] = jnp.zeros((CONV_PAD, TN_COLS), F32)
        pad_s[pl.ds(CONV_PAD, t), :] = x_ref[...]
        dpad_s[pl.ds(t, CONV_PAD), :] = jnp.zeros((CONV_PAD, TN_COLS), F32)
        cwv = cw_ref[...]
        acc = [jnp.zeros((1, TN_COLS), F32)] * CONV_WIDTH
        for c in range(t // CONV_ROWS):
            pre, xs = _conv_pre(pad_s, cwv, c)
            sg = _sigmoid(pre)
            dpre = dy_ref[pl.ds(c * CONV_ROWS, CONV_ROWS), :] * (sg * (1.0 + pre * (1.0 - sg)))
            dpad_s[pl.ds(c * CONV_ROWS, CONV_ROWS), :] = dpre
            acc = [acc[k] + jnp.sum(dpre * xs[k], axis=0, keepdims=True) for k in range(CONV_WIDTH)]
        for k in range(CONV_WIDTH):
            dcw_ref[0, pl.ds(k, 1), :] = acc[k]
        for c in range(t // CONV_ROWS):
            dx = dpad_s[pl.ds(c * CONV_ROWS + CONV_WIDTH - 1, CONV_ROWS), :] * cwv[0:1, :]
            for k in range(1, CONV_WIDTH):
                dx = dx + dpad_s[pl.ds(c * CONV_ROWS + CONV_WIDTH - 1 - k, CONV_ROWS), :] * cwv[k:k + 1, :]
            dx_ref[pl.ds(c * CONV_ROWS, CONV_ROWS), :] = dx

    blk = pl.BlockSpec((t, TN_COLS), lambda b, j: (b, j))
    return pl.pallas_call(
        body, name="conv_bwd", grid=(nseq, ch // TN_COLS),
        in_specs=[blk, pl.BlockSpec((CONV_WIDTH, TN_COLS), lambda b, j: (0, j)), blk],
        out_specs=[blk, pl.BlockSpec((1, CONV_WIDTH, TN_COLS), lambda b, j: (b, 0, j))],
        out_shape=[jax.ShapeDtypeStruct((n, ch), F32), jax.ShapeDtypeStruct((nseq, CONV_WIDTH, ch), F32)],
        scratch_shapes=[pltpu.VMEM((t + CONV_PAD, TN_COLS), F32)] * 2,
        compiler_params=_cparams("parallel", "parallel"),
    )(proj, cw, dy)


def _gate_specs(d):
    wide = pl.BlockSpec((TM_ROWS, d), lambda i: (i, 0))
    oa = pl.BlockSpec((TM_ROWS, A_WIDTH), lambda i: (i, 0))
    ob = pl.BlockSpec((TM_ROWS, B_WIDTH), lambda i: (i, 0))
    oc = pl.BlockSpec((TM_ROWS, C_WIDTH), lambda i: (i, 0))
    za = pl.BlockSpec((TM_ROWS, A_WIDTH), lambda i: (i, COL_AZ // A_WIDTH))
    zb = pl.BlockSpec((TM_ROWS, B_WIDTH), lambda i: (i, (COL_B + 3 * B_WIDTH) // B_WIDTH))
    zc = pl.BlockSpec((TM_ROWS, C_WIDTH), lambda i: (i, (COL_C + 3 * C_WIDTH) // C_WIDTH))
    return wide, oa, ob, oc, za, zb, zc


BRANCH_COLS = ((0, A_WIDTH), (A_WIDTH, A_WIDTH + B_WIDTH), (A_WIDTH + B_WIDTH, D_MODEL))


def outproj_fwd(x, oa, ob, oc, proj, w):
    n, d = x.shape

    def body(x_ref, oa_ref, ob_ref, oc_ref, za_ref, zb_ref, zc_ref, w_ref, y_ref, m_ref):
        for (lo, hi), o_ref, z_ref in zip(BRANCH_COLS, (oa_ref, ob_ref, oc_ref), (za_ref, zb_ref, zc_ref)):
            zv = z_ref[...]
            m_ref[:, lo:hi] = (o_ref[...] * (zv * _sigmoid(zv))).astype(BF16)
        y_ref[...] = x_ref[...] + jnp.dot(m_ref[...], w_ref[...], preferred_element_type=F32)

    wide, sa, sb, sc, za, zb, zc = _gate_specs(d)
    return pl.pallas_call(
        body, name="outproj_fwd", grid=(n // TM_ROWS,),
        in_specs=[wide, sa, sb, sc, za, zb, zc, pl.BlockSpec((d, d), lambda i: (0, 0))],
        out_specs=[wide, wide],
        out_shape=[jax.ShapeDtypeStruct((n, d), F32), jax.ShapeDtypeStruct((n, d), BF16)],
        compiler_params=_cparams("parallel"),
    )(x, oa, ob, oc, proj, proj, proj, w)


def outproj_loss(x, oa, ob, oc, proj, w, target):
    n, d = x.shape

    def body(x_ref, oa_ref, ob_ref, oc_ref, za_ref, zb_ref, zc_ref, w_ref, t_ref, dy_ref, m_ref, part_ref):
        for (lo, hi), o_ref, z_ref in zip(BRANCH_COLS, (oa_ref, ob_ref, oc_ref), (za_ref, zb_ref, zc_ref)):
            zv = z_ref[...]
            m_ref[:, lo:hi] = (o_ref[...] * (zv * _sigmoid(zv))).astype(BF16)
        e = x_ref[...] + jnp.dot(m_ref[...], w_ref[...], preferred_element_type=F32) - t_ref[...]
        dy_ref[...] = e * (1.0 / d)
        part_ref[...] = jnp.zeros_like(part_ref) + 0.5 * jnp.sum(e * e) * (1.0 / d)

    wide, sa, sb, sc, za, zb, zc = _gate_specs(d)
    return pl.pallas_call(
        body, name="outproj_loss", grid=(n // TM_ROWS,),
        in_specs=[wide, sa, sb, sc, za, zb, zc, pl.BlockSpec((d, d), lambda i: (0, 0)), wide],
        out_specs=[wide, wide, pl.BlockSpec((1, 8, BLOCK), lambda i: (i, 0, 0))],
        out_shape=[jax.ShapeDtypeStruct((n, d), F32), jax.ShapeDtypeStruct((n, d), BF16),
                   jax.ShapeDtypeStruct((n // TM_ROWS, 8, BLOCK), F32)],
        compiler_params=_cparams("parallel"),
    )(x, oa, ob, oc, proj, proj, proj, w, target)


def outproj_bwd(dy, oa, ob, oc, proj, w):
    n, d = dy.shape

    def body(dy_ref, oa_ref, ob_ref, oc_ref, za_ref, zb_ref, zc_ref, w_ref, doa_ref, dob_ref, doc_ref, dza_ref, dzb_ref, dzc_ref):
        dm = lax.dot_general(dy_ref[...].astype(BF16), w_ref[...], NT, preferred_element_type=F32)
        for (lo, hi), o_ref, z_ref, do_ref, dz_ref in zip(BRANCH_COLS, (oa_ref, ob_ref, oc_ref), (za_ref, zb_ref, zc_ref),
                                                          (doa_ref, dob_ref, doc_ref), (dza_ref, dzb_ref, dzc_ref)):
            zv = z_ref[...]
            sg = _sigmoid(zv)
            dmv = dm[:, lo:hi]
            do_ref[...] = dmv * (zv * sg)
            dz_ref[...] = dmv * o_ref[...] * (sg * (1.0 + zv * (1.0 - sg)))

    wide, sa, sb, sc, za, zb, zc = _gate_specs(d)
    sd = jax.ShapeDtypeStruct
    outs = [sd((n, A_WIDTH), F32), sd((n, B_WIDTH), F32), sd((n, C_WIDTH), F32)]
    return pl.pallas_call(
        body, name="outproj_bwd", grid=(n // TM_ROWS,),
        in_specs=[wide, sa, sb, sc, za, zb, zc, pl.BlockSpec((d, d), lambda i: (0, 0))],
        out_specs=[sa, sb, sc, sa, sb, sc],
        out_shape=outs + outs,
        compiler_params=_cparams("parallel"),
    )(dy, oa, ob, oc, proj, proj, proj, w)


ADAM_LR, ADAM_B1, ADAM_B2, ADAM_EPS, ADAM_WD, ADAM_STEP = 0.001, 0.9, 0.999, 1e-08, 0.01, 10


def adamw(w, g, m, v):
    r, c = w.shape
    tr = r if r <= 256 else 256

    def body(w_ref, g_ref, m_ref, v_ref, d_ref, nm_ref, nv_ref):
        gv = g_ref[...]
        nm = ADAM_B1 * m_ref[...] + (1.0 - ADAM_B1) * gv
        nv = ADAM_B2 * v_ref[...] + (1.0 - ADAM_B2) * (gv * gv)
        m_hat = nm / (1.0 - ADAM_B1 ** ADAM_STEP)
        v_hat = nv / (1.0 - ADAM_B2 ** ADAM_STEP)
        d_ref[...] = -ADAM_LR * (m_hat / (jnp.sqrt(v_hat) + ADAM_EPS) + ADAM_WD * w_ref[...])
        nm_ref[...] = nm
        nv_ref[...] = nv

    blk = pl.BlockSpec((tr, c), lambda i: (i, 0))
    return pl.pallas_call(
        body, name="adamw", grid=(r // tr,),
        in_specs=[blk] * 4, out_specs=[blk] * 3,
        out_shape=[jax.ShapeDtypeStruct((r, c), F32)] * 3,
        compiler_params=_cparams("parallel"),
    )(w, g, m, v)


SB_G = N_HEADS_B


def _sb_weights(qs, k, carry, tri, diag):
    z = _mm(qs, k, NT)
    sp = jnp.log(1.0 + jnp.exp(-jnp.abs(z)))
    ls_pos = jnp.minimum(z, 0.0) - sp
    ls_neg = jnp.minimum(-z, 0.0) - sp
    earlier = (lax.broadcasted_iota(jnp.int32, z.shape, 1) < lax.broadcasted_iota(jnp.int32, z.shape, 0)) if diag else None
    log_keep = jnp.where(earlier, ls_neg, 0.0) if diag else ls_neg
    hi = log_keep.astype(BF16)
    lo = (log_keep - hi.astype(F32)).astype(BF16)
    within = lax.dot_general(jnp.concatenate([hi, lo], axis=1), tri, NN, preferred_element_type=F32)
    arg = ls_pos + within + carry
    wts = jnp.where(earlier, jnp.exp(jnp.where(earlier, arg, 0.0)), 0.0) if diag else jnp.exp(arg)
    return ls_pos, ls_neg, log_keep, wts, earlier


def _sb_tile(q, k, v, carry, tri, diag):
    _, _, log_keep, wts, _ = _sb_weights(q * (HEAD_DIM ** -0.5), k, carry, tri, diag)
    return _mm(wts, v), jnp.sum(log_keep, axis=1, keepdims=True)


def _sb_tile_grads(q, k, v, carry, do, dtot, tri, diag):
    qs = q * (HEAD_DIM ** -0.5)
    ls_pos, ls_neg, _, wts, earlier = _sb_weights(qs, k, carry, tri, diag)
    dv = _mm(wts, do, TN)
    darg = _mm(do, v, NT) * wts
    dkeep = _mm(darg, tri[:BLOCK], NT) + dtot
    if diag:
        dkeep = jnp.where(earlier, dkeep, 0.0)
    dz = darg * jnp.exp(ls_neg) - dkeep * jnp.exp(ls_pos)
    return _mm(dz, k) * (HEAD_DIM ** -0.5), _mm(dz, qs, TN), dv, jnp.sum(darg, axis=1, keepdims=True)


_sb_tiles_diag = jax.vmap(functools.partial(_sb_tile, diag=True), in_axes=(0, 0, 0, 0, None))
_sb_tiles_off = jax.vmap(functools.partial(_sb_tile, diag=False), in_axes=(0, 0, 0, 0, None))
_sb_grads_diag = jax.vmap(functools.partial(_sb_tile_grads, diag=True), in_axes=(0, 0, 0, 0, 0, 0, None))
_sb_grads_off = jax.vmap(functools.partial(_sb_tile_grads, diag=False), in_axes=(0, 0, 0, 0, 0, 0, None))


def _sb_tri():
    r = lax.broadcasted_iota(jnp.int32, (2 * BLOCK, BLOCK), 0) % BLOCK
    c = lax.broadcasted_iota(jnp.int32, (2 * BLOCK, BLOCK), 1)
    return jnp.where(r > c, 1.0, 0.0).astype(BF16)


SB_SEQ = 2


def _sb_specs(t, nq):
    cb = COL_B // B_WIDTH
    sq = SB_SEQ
    q = pl.BlockSpec((sq, BLOCK, B_WIDTH), lambda b, i: (b, i, cb))
    k = pl.BlockSpec((sq, t, B_WIDTH), lambda b, i: (b, 0, cb + 1))
    v = pl.BlockSpec((sq, t, B_WIDTH), lambda b, i: (b, 0, cb + 2))
    blk = pl.BlockSpec((sq, BLOCK, B_WIDTH), lambda b, i: (b, i, 0))
    full = pl.BlockSpec((sq, t, B_WIDTH), lambda b, i: (b, 0, 0))
    carry = pl.BlockSpec((sq, 1, nq, BLOCK, SB_G), lambda b, i: (b, i, 0, 0, 0))
    return q, k, v, blk, full, carry


def _sb_heads(ref, rows):
    return jnp.concatenate([_heads(ref[b, rows, :], SB_G) for b in range(SB_SEQ)])


def _sb_unheads(a):
    return [_unheads(a[b * SB_G:(b + 1) * SB_G]) for b in range(SB_SEQ)]


def sb_fwd(proj, nseq):
    n = proj.shape[0]
    t = n // nseq
    nq = t // BLOCK
    g, sq = SB_G, SB_SEQ
    everything = pl.ds(0, BLOCK)

    def body(q_ref, k_ref, v_ref, o_ref, carry_ref):
        i = pl.program_id(1)
        tri = _sb_tri()
        qv = _sb_heads(q_ref, everything)

        def tile(j, c, fn):
            rows = pl.ds(pl.multiple_of(j * BLOCK, BLOCK), BLOCK)
            for b in range(sq):
                carry_ref[b, 0, j] = jnp.concatenate([c[b * g + h] for h in range(g)], axis=1)
            return fn(qv, _sb_heads(k_ref, rows), _sb_heads(v_ref, rows), c, tri)

        def step(it, st):
            o_acc, c = st
            o, tot = tile(i - 1 - it, c, _sb_tiles_off)
            return o_acc + o, c + tot

        o_acc, _ = lax.fori_loop(0, i, step, tile(i, jnp.zeros((sq * g, BLOCK, 1), F32), _sb_tiles_diag))
        for b, o in enumerate(_sb_unheads(o_acc)):
            o_ref[b] = o

    q, k, v, blk, _, carry = _sb_specs(t, nq)
    proj3 = proj.reshape(nseq, t, -1)
    o, carries = pl.pallas_call(
        body, name="sb_fwd", grid=(nseq // sq, nq),
        in_specs=[q, k, v],
        out_specs=[blk, carry],
        out_shape=[jax.ShapeDtypeStruct((nseq, t, B_WIDTH), F32),
                   jax.ShapeDtypeStruct((nseq, nq, nq, BLOCK, g), F32)],
        compiler_params=_cparams("parallel", "arbitrary"),
    )(proj3, proj3, proj3)
    return o.reshape(n, B_WIDTH), carries


def sb_bwd(proj, carries, do, nseq):
    n = proj.shape[0]
    t = n // nseq
    nq = t // BLOCK
    g, sq = SB_G, SB_SEQ
    everything = pl.ds(0, BLOCK)

    def body(q_ref, k_ref, v_ref, carry_ref, do_ref, dq_ref, dk_ref, dv_ref):
        i = pl.program_id(1)

        @pl.when(i == 0)
        def _():
            dk_ref[...] = jnp.zeros_like(dk_ref)
            dv_ref[...] = jnp.zeros_like(dv_ref)

        tri = _sb_tri()
        qv = _sb_heads(q_ref, everything)
        dov = _sb_heads(do_ref, everything)

        def tile(j, st, fn):
            dq_acc, dc = st
            rows = pl.ds(pl.multiple_of(j * BLOCK, BLOCK), BLOCK)
            cj = [carry_ref[b, 0, j] for b in range(sq)]
            dq, dk, dv, dcj = fn(qv, _sb_heads(k_ref, rows), _sb_heads(v_ref, rows),
                                 jnp.stack([cj[b][:, h:h + 1] for b in range(sq) for h in range(g)]), dov, dc, tri)
            for b, (dkb, dvb) in enumerate(zip(_sb_unheads(dk), _sb_unheads(dv))):
                dk_ref[b, rows, :] += dkb
                dv_ref[b, rows, :] += dvb
            return dq_acc + dq, dc + dcj

        st = lax.fori_loop(0, i, lambda j, st: tile(j, st, _sb_grads_off),
                           (jnp.zeros((sq * g, BLOCK, HEAD_DIM), F32), jnp.zeros((sq * g, BLOCK, 1), F32)))
        dq_acc, _ = tile(i, st, _sb_grads_diag)
        for b, dq in enumerate(_sb_unheads(dq_acc)):
            dq_ref[b] = dq

    q, k, v, blk, full, carry = _sb_specs(t, nq)
    proj3, do3 = proj.reshape(nseq, t, -1), do.reshape(nseq, t, -1)
    grads = pl.pallas_call(
        body, name="sb_bwd", grid=(nseq // sq, nq),
        in_specs=[q, k, v, carry, blk],
        out_specs=[blk, full, full],
        out_shape=[jax.ShapeDtypeStruct((nseq, t, B_WIDTH), F32)] * 3,
        compiler_params=_cparams("parallel", "arbitrary"),
    )(proj3, proj3, proj3, carries, do3)
    return [a.reshape(n, B_WIDTH) for a in grads]


def _unit_lower_inverse(a):
    n = a.shape[0]
    eye = jnp.where(lax.broadcasted_iota(jnp.int32, (n, n), 0) == lax.broadcasted_iota(jnp.int32, (n, n), 1), 1.0, 0.0)
    tmat = eye.astype(F32) - a
    p = a
    for _ in range(5):
        p = _mm32(p, p)
        tmat = tmat + _mm32(tmat, p)
    return tmat


@jax.custom_vjp
def _known_inverse(a, tmat):
    return tmat


def _known_inverse_fwd(a, tmat):
    return tmat, tmat


def _known_inverse_bwd(tmat, g):
    return -_mm32(_mm32(tmat, g, TN), tmat, NT), jnp.zeros_like(tmat)


_known_inverse.defvjp(_known_inverse_fwd, _known_inverse_bwd)


def _gdn_chunk(q, k, v, al_c, al_r, br_c, alog, dtb, nw, s, tmat_in):
    c = GDN_CHUNK
    ri = lax.broadcasted_iota(jnp.int32, (c, c), 0)
    ci = lax.broadcasted_iota(jnp.int32, (c, c), 1)
    incl, strict = ri >= ci, ri > ci
    rate = -jnp.exp(alog)
    g_c = rate * _softplus(al_c + dtb)
    g_r = rate * _softplus(al_r + dtb)
    beta = _sigmoid(br_c)
    gc_c = jnp.sum(jnp.where(incl, g_r, 0.0), axis=1, keepdims=True)
    gc_r = jnp.sum(jnp.where(ri <= ci, g_c, 0.0), axis=0, keepdims=True)
    gl = jnp.sum(g_r, axis=1, keepdims=True)
    decay = jnp.where(incl, jnp.exp(jnp.where(incl, gc_c - gc_r, 0.0)), 0.0)
    qn = q * lax.rsqrt(jnp.sum(q * q, axis=-1, keepdims=True) + RMS_EPS) * (HEAD_DIM ** -0.5)
    kn = k * lax.rsqrt(jnp.sum(k * k, axis=-1, keepdims=True) + RMS_EPS)
    kb = kn * beta
    a = jnp.where(strict, _mm(kb, kn, NT) * decay, 0.0)
    tmat = _unit_lower_inverse(a) if tmat_in is None else _known_inverse(a, tmat_in)
    u = _mm(tmat, v * beta)
    w = _mm(tmat, kb * jnp.exp(gc_c))
    qk = _mm(qn, kn, NT) * decay
    v_new = u - _mm(w, s)
    o = _mm(qn * jnp.exp(gc_c), s) + _mm(qk, v_new)
    s_new = s * jnp.exp(gl) + _mm(kn * jnp.exp(gl - gc_c), v_new, TN)
    o = o * lax.rsqrt(jnp.mean(o * o, axis=-1, keepdims=True) + RMS_EPS) * nw
    return o, s_new, tmat


_gdn_chunks_fwd = jax.vmap(functools.partial(_gdn_chunk, tmat_in=None), in_axes=(0, 0, 0, 0, 0, 0, 0, 0, None, 0))
_gdn_chunks_bwd = jax.vmap(_gdn_chunk, in_axes=(0, 0, 0, 0, 0, 0, 0, 0, None, 0, 0))

GDN_TB = 256
GDN_SEQ_FWD = 2
GDN_SEQ_BWD = 2


def _gdn_block(q3, k3, v3, ba, alog, dtb, nw, s, tm=None):
    nh = N_HEADS_A
    ns = q3.shape[0]
    bat = [ba[b].T for b in range(ns)]
    br_c = jnp.stack([ba[b][:, h:h + 1] for b in range(ns) for h in range(nh)])
    al_c = jnp.stack([ba[b][:, nh + h:nh + h + 1] for b in range(ns) for h in range(nh)])
    al_r = jnp.stack([bat[b][nh + h:nh + h + 1, :] for b in range(ns) for h in range(nh)])
    heads = lambda a: jnp.concatenate([_heads(a[b], nh) for b in range(ns)])
    args = (heads(q3), heads(k3), heads(v3), al_c, al_r, br_c, jnp.concatenate([alog] * ns), jnp.concatenate([dtb] * ns), nw, s)
    o, s_new, tmat = _gdn_chunks_fwd(*args) if tm is None else _gdn_chunks_bwd(*args, tm)
    o3 = jnp.stack([_unheads(o[b * nh:(b + 1) * nh]) for b in range(ns)])
    return (o3, s_new, tmat) if tm is None else (o3, s_new)


def _gdn_specs(nt, sq, rev):
    tpos = (lambda i: nt - 1 - i) if rev else (lambda i: i)
    ncb = GDN_TB // GDN_CHUNK
    qkv = [pl.BlockSpec((sq, GDN_TB, A_WIDTH), lambda b, i, j=j: (b, tpos(i), j)) for j in range(3)]
    ba = pl.BlockSpec((sq, GDN_TB, BLOCK), lambda b, i: (b, tpos(i), COL_BA // BLOCK))
    one = pl.BlockSpec((N_HEADS_A, 1, 1), lambda b, i: (0, 0, 0))
    vec = pl.BlockSpec((1, HEAD_DIM), lambda b, i: (0, 0))
    st = pl.BlockSpec((sq, ncb, N_HEADS_A, HEAD_DIM, HEAD_DIM), lambda b, i: (b, tpos(i), 0, 0, 0))
    oa = pl.BlockSpec((sq, GDN_TB, A_WIDTH), lambda b, i: (b, tpos(i), 0))
    return qkv, ba, one, vec, st, oa, tpos


def gdn_fwd(ya, proj, alog, dtb, nw, nseq):
    n = ya.shape[0]
    t = n // nseq
    nc, nt, ncb = t // GDN_CHUNK, t // GDN_TB, GDN_TB // GDN_CHUNK
    sq = GDN_SEQ_FWD
    nh = N_HEADS_A

    def body(q_ref, k_ref, v_ref, ba_ref, alog_ref, dtb_ref, nw_ref, o_ref, st_ref, tm_ref, s_s):
        @pl.when(pl.program_id(1) == 0)
        def _():
            s_s[...] = jnp.zeros_like(s_s)

        def step(c, s):
            rows = pl.ds(pl.multiple_of(c * GDN_CHUNK, GDN_CHUNK), GDN_CHUNK)
            o, s_new, tmat = _gdn_block(q_ref[:, rows, :], k_ref[:, rows, :], v_ref[:, rows, :], ba_ref[:, rows, :],
                                        alog_ref[...], dtb_ref[...], nw_ref[...], s)
            for b in range(sq):
                st_ref[b, c] = s[b * nh:(b + 1) * nh]
                tm_ref[b, c] = tmat[b * nh:(b + 1) * nh]
            o_ref[:, rows, :] = o
            return s_new

        s_s[...] = lax.fori_loop(0, ncb, step, s_s[...])

    qkv, ba, one, vec, st, oa, _ = _gdn_specs(nt, sq, False)
    ya3, proj3 = ya.reshape(nseq, t, -1), proj.reshape(nseq, t, -1)
    per_chunk = jax.ShapeDtypeStruct((nseq, nc, nh, HEAD_DIM, HEAD_DIM), F32)
    o, states, inverses = pl.pallas_call(
        body, name="gdn_fwd", grid=(nseq // sq, nt),
        in_specs=qkv + [ba, one, one, vec],
        out_specs=[oa, st, st],
        out_shape=[jax.ShapeDtypeStruct((nseq, t, A_WIDTH), F32), per_chunk, per_chunk],
        scratch_shapes=[pltpu.VMEM((sq * nh, HEAD_DIM, HEAD_DIM), F32)],
        compiler_params=_cparams("parallel", "arbitrary"),
    )(ya3, ya3, ya3, proj3, alog, dtb, nw)
    return o.reshape(n, A_WIDTH), states, inverses


def gdn_bwd(ya, proj, alog, dtb, nw, states, inverses, do, nseq):
    n = ya.shape[0]
    t = n // nseq
    nt, ncb = t // GDN_TB, GDN_TB // GDN_CHUNK
    nh = N_HEADS_A
    sq = GDN_SEQ_BWD

    def body(q_ref, k_ref, v_ref, ba_ref, alog_ref, dtb_ref, nw_ref, st_ref, tm_ref, do_ref,
             dya_ref, dba_ref, dalog_ref, ddtb_ref, dnw_ref, ds_s):
        @pl.when(pl.program_id(1) == 0)
        def _():
            ds_s[...] = jnp.zeros_like(ds_s)
            dalog_ref[...] = jnp.zeros_like(dalog_ref)
            ddtb_ref[...] = jnp.zeros_like(ddtb_ref)
            dnw_ref[...] = jnp.zeros_like(dnw_ref)

        def step(it, carry):
            ds, dalog, ddtb, dnw = carry
            c = ncb - 1 - it
            rows = pl.ds(pl.multiple_of(c * GDN_CHUNK, GDN_CHUNK), GDN_CHUNK)
            s_in = jnp.concatenate([st_ref[b, c] for b in range(sq)])
            tm_in = jnp.concatenate([tm_ref[b, c] for b in range(sq)])
            _, vjp = jax.vjp(functools.partial(_gdn_block, tm=tm_in), q_ref[:, rows, :], k_ref[:, rows, :], v_ref[:, rows, :],
                             ba_ref[:, rows, :], alog_ref[...], dtb_ref[...], nw_ref[...], s_in)
            dq, dk, dv, dba, da, dd, dn, ds = vjp((do_ref[:, rows, :], ds))
            dya_ref[:, rows, 0:A_WIDTH] = dq
            dya_ref[:, rows, A_WIDTH:2 * A_WIDTH] = dk
            dya_ref[:, rows, 2 * A_WIDTH:3 * A_WIDTH] = dv
            dba_ref[:, rows, :] = dba
            return ds, dalog + da, ddtb + dd, dnw + dn

        z11 = jnp.zeros((nh, 1, 1), F32)
        ds, dalog, ddtb, dnw = lax.fori_loop(0, ncb, step, (ds_s[...], z11, z11, jnp.zeros((1, HEAD_DIM), F32)))
        ds_s[...] = ds
        dalog_ref[0] += dalog
        ddtb_ref[0] += ddtb
        dnw_ref[0] += dnw

    qkv, ba, one, vec, st, oa, tpos = _gdn_specs(nt, sq, True)
    per_grp = pl.BlockSpec((1, nh, 1, 1), lambda b, i: (b, 0, 0, 0))
    sd = jax.ShapeDtypeStruct
    ya3, proj3, do3 = ya.reshape(nseq, t, -1), proj.reshape(nseq, t, -1), do.reshape(nseq, t, -1)
    dya, dba, dalog, ddtb, dnw = pl.pallas_call(
        body, name="gdn_bwd", grid=(nseq // sq, nt),
        in_specs=qkv + [ba, one, one, vec, st, st, oa],
        out_specs=[pl.BlockSpec((sq, GDN_TB, 3 * A_WIDTH), lambda b, i: (b, tpos(i), 0)),
                   pl.BlockSpec((sq, GDN_TB, BLOCK), lambda b, i: (b, tpos(i), 0)),
                   per_grp, per_grp, pl.BlockSpec((1, 1, HEAD_DIM), lambda b, i: (b, 0, 0))],
        out_shape=[sd((nseq, t, 3 * A_WIDTH), F32), sd((nseq, t, BLOCK), F32), sd((nseq // sq, nh, 1, 1), F32),
                   sd((nseq // sq, nh, 1, 1), F32), sd((nseq // sq, 1, HEAD_DIM), F32)],
        scratch_shapes=[pltpu.VMEM((sq * nh, HEAD_DIM, HEAD_DIM), F32)],
        compiler_params=_cparams("parallel", "arbitrary"),
    )(ya3, ya3, ya3, proj3, alog, dtb, nw, states, inverses, do3)
    return dya.reshape(n, 3 * A_WIDTH), dba.reshape(n, BLOCK), dalog, ddtb, dnw


DIL_NB = tuple((SEQ // d) // BLOCK for _, d in DILATED_PAIRS)
DIL_D = tuple(d for _, d in DILATED_PAIRS)
DIL_STEPS = tuple(w // d for w, d in DILATED_PAIRS)
DIL_B = 8
PAIR = 2 * HEAD_DIM


def _rope_tables(t):
    half = ROPE_DIM // 2
    inv_freq = ROPE_THETA ** (-jnp.arange(half, dtype=F32) / half)
    ang = jnp.arange(t, dtype=F32)[:, None] * inv_freq[None, :]
    ones = jnp.ones((t, HEAD_DIM - ROPE_DIM), F32)
    cs = jnp.concatenate([jnp.cos(ang), jnp.cos(ang), ones], axis=1)
    sn = jnp.concatenate([jnp.sin(ang), jnp.sin(ang), 0.0 * ones], axis=1)
    i = jnp.arange(PAIR)[:, None]
    j = jnp.arange(PAIR)[None, :]
    same = (i // HEAD_DIM) == (j // HEAD_DIM)
    ih, jh = i % HEAD_DIM, j % HEAD_DIM
    pm = (jnp.where(same & (jh < half) & (ih == jh + half), -1.0, 0.0)
          + jnp.where(same & (jh >= half) & (jh < ROPE_DIM) & (ih == jh - half), 1.0, 0.0))
    mean = jnp.where(same, 1.0 / HEAD_DIM, 0.0)
    twice = lambda m: jnp.concatenate([m, m]).astype(BF16)
    return jnp.tile(cs, (1, 2)), jnp.tile(sn, (1, 2)), twice(mean), twice(pm)


def _split_dot(x, w2):
    hi = x.astype(BF16)
    lo = lax.stop_gradient(x - hi.astype(F32)).astype(BF16)
    return lax.dot_general(jnp.concatenate([hi, lo], axis=1), w2, NN, preferred_element_type=F32)


def _dil_prep(x, w, cs, sn, mean2, pm2):
    y = x * lax.rsqrt(_split_dot(x * x, mean2) + RMS_EPS) * w
    return y * cs + _split_dot(y, pm2) * sn


def _dil_tile(qn, kk, vv, bias):
    lane = lax.broadcasted_iota(jnp.int32, (1, PAIR), 1)
    outs, lses = [], []
    for h in range(2):
        s = _mm(jnp.where(lane // HEAD_DIM == h, qn, 0.0) * (HEAD_DIM ** -0.5), kk, NT) + bias
        m = lax.stop_gradient(jnp.max(s, axis=-1, keepdims=True))
        p = jnp.exp(s - m)
        denom = jnp.sum(p, axis=-1, keepdims=True)
        outs.append(_mm(p, vv) / denom)
        lses.append(m + jnp.log(denom))
    return jnp.where(lane < HEAD_DIM, outs[0], outs[1]), jnp.concatenate(lses, axis=1)


_dil_tiles = jax.vmap(_dil_tile)


def _spread(a):
    lane = lax.broadcasted_iota(jnp.int32, (a.shape[0], PAIR), 1)
    return jnp.where(lane < HEAD_DIM, a[:, 0:1], a[:, 1:2])


def _dil_mix(o1, o2, o3, l1, l2, l3):
    m = lax.stop_gradient(jnp.maximum(jnp.maximum(l1, l2), l3))
    e1, e2, e3 = jnp.exp(l1 - m), jnp.exp(l2 - m), jnp.exp(l3 - m)
    r = 1.0 / (e1 + e2 + e3)
    return _spread(e1 * r) * o1 + _spread(e2 * r) * o2 + _spread(e3 * r) * o3


def _dil_fill_biases(bias_s):
    steps, = set(DIL_STEPS)
    qi = lax.broadcasted_iota(jnp.int32, (BLOCK, 1), 0)
    kj = lax.broadcasted_iota(jnp.int32, (1, 2 * BLOCK), 1)
    rel = qi - kj + BLOCK
    inside = (rel >= 0) & (rel <= steps)
    bias_s[0] = jnp.where(inside, 0.0, NEG)
    bias_s[1] = jnp.where(inside & (kj >= BLOCK), 0.0, NEG)
    bias_s[2] = jnp.where((qi >= kj) & (qi - kj <= steps), 0.0, NEG)


def _dil_mask(it, g, bias_s):
    qrows = pl.ds(pl.multiple_of(it * BLOCK, BLOCK), BLOCK)
    if DIL_NB[g] == 1:
        return bias_s[2, :, 0:BLOCK], qrows, qrows
    which = jnp.where(it == 0, 2, jnp.where(it % DIL_NB[g] == 0, 1, 0))
    kstart = jnp.maximum(it - 1, 0) * BLOCK
    return bias_s[which], qrows, pl.ds(pl.multiple_of(kstart, BLOCK), 2 * BLOCK)


def _dil_gather(src, dst, d):
    t = src.shape[0]
    ln = t // d
    for r in range(d):
        dst[pl.ds(r * ln, ln), :] = src[pl.ds(r, ln, stride=d), :]


def _dil_scatter(src, dst, d):
    t = src.shape[0]
    ln = t // d
    for r in range(d):
        dst[pl.ds(r, ln, stride=d), :] = src[pl.ds(r * ln, ln), :]


def _dil_forward_parts(q_ref, k_ref, v_ref, qw, kw, cs_ref, sn_ref, mean2, pm2, qn_s, kn_s, dl_s, od_s, ld_s, on_s, ln_s, bias_s):
    t = qn_s.shape[0]
    _dil_fill_biases(bias_s)

    def prep(c, _):
        rows = pl.ds(pl.multiple_of(c * ROWS, ROWS), ROWS)
        qn_s[rows, :] = _dil_prep(q_ref[rows, :], qw, cs_ref[rows, :], sn_ref[rows, :], mean2, pm2)
        kn_s[rows, :] = _dil_prep(k_ref[rows, :], kw, cs_ref[rows, :], sn_ref[rows, :], mean2, pm2)
        return 0

    lax.fori_loop(0, t // ROWS, prep, 0)
    for g in (1, 2):
        _dil_gather(qn_s, dl_s.at[g - 1, 0], DIL_D[g])
        _dil_gather(kn_s, dl_s.at[g - 1, 1], DIL_D[g])
        _dil_gather(v_ref, dl_s.at[g - 1, 2], DIL_D[g])
    for g in range(3):
        qs = qn_s if g == 0 else dl_s.at[g - 1, 0]
        ks = kn_s if g == 0 else dl_s.at[g - 1, 1]
        vs = v_ref if g == 0 else dl_s.at[g - 1, 2]

        def tiles(i, _, g=g, qs=qs, ks=ks, vs=vs):
            where = [_dil_mask(i * DIL_B + b, g, bias_s) for b in range(DIL_B)]
            o, lse = _dil_tiles(jnp.stack([qs[qr, :] for _, qr, _ in where]), jnp.stack([ks[kr, :] for _, _, kr in where]),
                                jnp.stack([vs[kr, :] for _, _, kr in where]), jnp.stack([m for m, _, _ in where]))
            for b, (_, qr, _) in enumerate(where):
                od_s[g, qr, :] = o[b]
                ld_s[g, qr, :] = lse[b]
            return 0

        lax.fori_loop(0, t // BLOCK // DIL_B, tiles, 0)
    for g in (1, 2):
        _dil_scatter(od_s.at[g], on_s.at[g - 1], DIL_D[g])
        _dil_scatter(ld_s.at[g], ln_s.at[g - 1], DIL_D[g])


def _dil_scratch(t):
    return [pltpu.VMEM((t, PAIR), F32), pltpu.VMEM((t, PAIR), F32),
            pltpu.VMEM((2, 3, t, PAIR), F32),
            pltpu.VMEM((3, t, PAIR), F32), pltpu.VMEM((3, t, 2), F32),
            pltpu.VMEM((2, t, PAIR), F32), pltpu.VMEM((2, t, 2), F32),
            pltpu.VMEM((3, BLOCK, 2 * BLOCK), F32)]


def _dil_specs(t):
    cb = COL_C // BLOCK
    per = C_WIDTH // BLOCK
    qkv = [pl.BlockSpec((t, BLOCK), lambda b, p, j=j: (b, cb + j * per + p)) for j in range(3)]
    vec = pl.BlockSpec((1, PAIR), lambda b, p: (0, 0))
    tab = pl.BlockSpec((t, PAIR), lambda b, p: (0, 0))
    mat = pl.BlockSpec((2 * PAIR, PAIR), lambda b, p: (0, 0))
    pair = pl.BlockSpec((t, BLOCK), lambda b, p: (b, p))
    return qkv, vec, tab, mat, pair


def dil_fwd(proj, qw, kw, cs, sn, mean2, pm2, nseq):
    n = proj.shape[0]
    t = n // nseq

    def body(q_ref, k_ref, v_ref, qw_ref, kw_ref, cs_ref, sn_ref, mean_ref, pm_ref, o_ref,
             qn_s, kn_s, dl_s, od_s, ld_s, on_s, ln_s, bias_s):
        _dil_forward_parts(q_ref, k_ref, v_ref, qw_ref[...], kw_ref[...], cs_ref, sn_ref, mean_ref[...], pm_ref[...],
                           qn_s, kn_s, dl_s, od_s, ld_s, on_s, ln_s, bias_s)

        def mix(c, _):
            rows = pl.ds(pl.multiple_of(c * ROWS, ROWS), ROWS)
            o_ref[rows, :] = _dil_mix(od_s[0, rows, :], on_s[0, rows, :], on_s[1, rows, :],
                                      ld_s[0, rows, :], ln_s[0, rows, :], ln_s[1, rows, :])
            return 0

        lax.fori_loop(0, t // ROWS, mix, 0)

    qkv, vec, tab, mat, pair = _dil_specs(t)
    return pl.pallas_call(
        body, name="dil_fwd", grid=(nseq, C_WIDTH // BLOCK),
        in_specs=qkv + [vec, vec, tab, tab, mat, mat],
        out_specs=pair,
        out_shape=jax.ShapeDtypeStruct((n, C_WIDTH), F32),
        scratch_shapes=_dil_scratch(t),
        compiler_params=_cparams("parallel", "parallel"),
    )(proj, proj, proj, qw, kw, cs, sn, mean2, pm2)


def dil_bwd(proj, qw, kw, cs, sn, mean2, pm2, do, nseq):
    n = proj.shape[0]
    t = n // nseq

    def body(q_ref, k_ref, v_ref, qw_ref, kw_ref, cs_ref, sn_ref, mean_ref, pm_ref, do_ref,
             dq_ref, dk_ref, dv_ref, dqw_ref, dkw_ref,
             qn_s, kn_s, dl_s, od_s, ld_s, on_s, ln_s, bias_s, tq_s, tk_s, tv_s):
        qw, kw, mean2, pm2 = qw_ref[...], kw_ref[...], mean_ref[...], pm_ref[...]
        _dil_forward_parts(q_ref, k_ref, v_ref, qw, kw, cs_ref, sn_ref, mean2, pm2, qn_s, kn_s, dl_s, od_s, ld_s, on_s, ln_s, bias_s)

        def mix(c, _):
            rows = pl.ds(pl.multiple_of(c * ROWS, ROWS), ROWS)
            _, vjp = jax.vjp(_dil_mix, od_s[0, rows, :], on_s[0, rows, :], on_s[1, rows, :],
                             ld_s[0, rows, :], ln_s[0, rows, :], ln_s[1, rows, :])
            d1, d2, d3, e1, e2, e3 = vjp(do_ref[rows, :])
            od_s[0, rows, :] = d1
            on_s[0, rows, :] = d2
            on_s[1, rows, :] = d3
            ld_s[0, rows, :] = e1
            ln_s[0, rows, :] = e2
            ln_s[1, rows, :] = e3
            return 0

        lax.fori_loop(0, t // ROWS, mix, 0)
        for g in (1, 2):
            _dil_gather(on_s.at[g - 1], od_s.at[g], DIL_D[g])
            _dil_gather(ln_s.at[g - 1], ld_s.at[g], DIL_D[g])
        on_s[...] = jnp.zeros_like(on_s)
        dv_ref[...] = jnp.zeros_like(dv_ref)
        for g in range(3):
            qs = qn_s if g == 0 else dl_s.at[g - 1, 0]
            ks = kn_s if g == 0 else dl_s.at[g - 1, 1]
            vs = v_ref if g == 0 else dl_s.at[g - 1, 2]
            gq = on_s.at[0] if g == 0 else tq_s
            gk = on_s.at[1] if g == 0 else tk_s
            gv = dv_ref if g == 0 else tv_s
            if g > 0:
                tk_s[...] = jnp.zeros_like(tk_s)
                tv_s[...] = jnp.zeros_like(tv_s)

            def tiles(i, _, g=g, qs=qs, ks=ks, vs=vs, gq=gq, gk=gk, gv=gv):
                where = [_dil_mask(i * DIL_B + b, g, bias_s) for b in range(DIL_B)]
                biases = jnp.stack([m for m, _, _ in where])
                _, vjp = jax.vjp(lambda q_, k_, v_: _dil_tiles(q_, k_, v_, biases),
                                 jnp.stack([qs[qr, :] for _, qr, _ in where]), jnp.stack([ks[kr, :] for _, _, kr in where]),
                                 jnp.stack([vs[kr, :] for _, _, kr in where]))
                dq, dkk, dvv = vjp((jnp.stack([od_s[g, qr, :] for _, qr, _ in where]),
                                    jnp.stack([ld_s[g, qr, :] for _, qr, _ in where])))
                for b, (_, qr, kr) in enumerate(where):
                    gq[qr, :] = dq[b]
                    gk[kr, :] += dkk[b]
                    gv[kr, :] += dvv[b]
                return 0

            lax.fori_loop(0, t // BLOCK // DIL_B, tiles, 0)
            if g > 0:
                d = DIL_D[g]
                ln = t // d
                for r in range(d):
                    nat, dil = pl.ds(r, ln, stride=d), pl.ds(r * ln, ln)
                    on_s[0, nat, :] += tq_s[dil, :]
                    on_s[1, nat, :] += tk_s[dil, :]
                    dv_ref[nat, :] += tv_s[dil, :]

        def prep(c, acc):
            rows = pl.ds(pl.multiple_of(c * ROWS, ROWS), ROWS)
            f = lambda x, w: _dil_prep(x, w, cs_ref[rows, :], sn_ref[rows, :], mean2, pm2)
            _, vq = jax.vjp(f, q_ref[rows, :], qw)
            _, vk = jax.vjp(f, k_ref[rows, :], kw)
            dq, dqw = vq(on_s[0, rows, :])
            dk, dkw = vk(on_s[1, rows, :])
            dq_ref[rows, :] = dq
            dk_ref[rows, :] = dk
            return acc[0] + dqw, acc[1] + dkw

        dqw, dkw = lax.fori_loop(0, t // ROWS, prep, (jnp.zeros((1, PAIR), F32), jnp.zeros((1, PAIR), F32)))
        dqw_ref[0] = dqw
        dkw_ref[0] = dkw

    qkv, vec, tab, mat, pair = _dil_specs(t)
    per = C_WIDTH // BLOCK
    wout = pl.BlockSpec((1, 1, PAIR), lambda b, p: (b * per + p, 0, 0))
    return pl.pallas_call(
        body, name="dil_bwd", grid=(nseq, per),
        in_specs=qkv + [vec, vec, tab, tab, mat, mat, pair],
        out_specs=[pair, pair, pair, wout, wout],
        out_shape=[jax.ShapeDtypeStruct((n, C_WIDTH), F32)] * 3 + [jax.ShapeDtypeStruct((nseq * per, 1, PAIR), F32)] * 2,
        scratch_shapes=_dil_scratch(t) + [pltpu.VMEM((t, PAIR), F32)] * 3,
        compiler_params=_cparams("parallel", "parallel"),
    )(proj, proj, proj, qw, kw, cs, sn, mean2, pm2, do)


N_CHIPS = 4
SUM_ROWS = 432
MESH_IDS = pl.DeviceIdType.MESH
ANY = pl.BlockSpec(memory_space=pl.ANY)


def plane_exchange(src, all_to_all):
    blk_shape = src.shape[1:] if all_to_all else src.shape

    def body(src_ref, out_ref, send_sems, recv_sems, local_sem):
        x, y, c = lax.axis_index("x"), lax.axis_index("y"), lax.axis_index("c")
        me = 2 * x + y
        mine = pltpu.make_async_copy(src_ref.at[me] if all_to_all else src_ref, out_ref.at[me], local_sem)
        mine.start()
        sends = []
        for k in (1, 2, 3):
            px = 1 - x if k & 2 else x
            py = 1 - y if k & 1 else y
            peer = 2 * px + py
            cp = pltpu.make_async_remote_copy(
                src_ref=src_ref.at[peer] if all_to_all else src_ref, dst_ref=out_ref.at[me],
                send_sem=send_sems.at[k - 1], recv_sem=recv_sems.at[k - 1],
                device_id=(px, py, c), device_id_type=MESH_IDS)
            cp.start()
            sends.append((cp, peer, (px, py, c)))
        for k, (cp, peer, dev) in enumerate(sends):
            pltpu.make_async_remote_copy(
                src_ref=out_ref.at[me], dst_ref=out_ref.at[peer],
                send_sem=send_sems.at[k], recv_sem=recv_sems.at[k],
                device_id=dev, device_id_type=MESH_IDS).wait_recv()
        for cp, _, _ in sends:
            cp.wait_send()
        mine.wait()

    return pl.pallas_call(
        body, name="plane_all_to_all" if all_to_all else "plane_all_gather",
        in_specs=[ANY], out_specs=ANY,
        out_shape=jax.ShapeDtypeStruct((N_CHIPS,) + blk_shape, src.dtype),
        scratch_shapes=[pltpu.SemaphoreType.DMA((3,)), pltpu.SemaphoreType.DMA((3,)), pltpu.SemaphoreType.DMA],
    )(src)


def sibling_swap(src, other_half=False):
    shape = (src.shape[0], src.shape[1] // 2) + src.shape[2:] if other_half else src.shape

    def body(src_ref, out_ref, send_sem, recv_sem):
        x, y, c = lax.axis_index("x"), lax.axis_index("y"), lax.axis_index("c")
        part = src_ref.at[:, pl.ds((1 - c) * shape[1], shape[1])] if other_half else src_ref
        cp = pltpu.make_async_remote_copy(src_ref=part, dst_ref=out_ref, send_sem=send_sem, recv_sem=recv_sem,
                                          device_id=(x, y, 1 - c), device_id_type=MESH_IDS)
        cp.start()
        cp.wait()

    return pl.pallas_call(
        body, name="sibling_swap", in_specs=[ANY], out_specs=ANY,
        out_shape=jax.ShapeDtypeStruct(shape, src.dtype),
        scratch_shapes=[pltpu.SemaphoreType.DMA, pltpu.SemaphoreType.DMA],
    )(src)


def sum4(a):
    _, r, c = a.shape
    tr = SUM_ROWS

    def body(a_ref, o_ref):
        p = [a_ref[i].astype(F32) for i in range(N_CHIPS)]
        o_ref[...] = (p[0] + p[1]) + (p[2] + p[3])

    return pl.pallas_call(
        body, name="sum4", grid=(r // tr,),
        in_specs=[pl.BlockSpec((N_CHIPS, tr, c), lambda i: (0, i, 0))],
        out_specs=pl.BlockSpec((tr, c), lambda i: (i, 0)),
        out_shape=jax.ShapeDtypeStruct((r, c), F32),
        compiler_params=_cparams("parallel"),
    )(a)


def add_my_half(mine, got, c):
    nchip, r2, cols = mine.shape
    nt = r2 // 2 // SUM_ROWS

    def body(c_ref, a_ref, b_ref, o_ref):
        o_ref[...] = (a_ref[...] + b_ref[...]).astype(BF16)

    blk = pl.BlockSpec((1, SUM_ROWS, cols), lambda j, i, c_ref: (j, i, 0))
    return pl.pallas_call(
        body, name="add_my_half",
        grid_spec=pltpu.PrefetchScalarGridSpec(
            num_scalar_prefetch=1, grid=(nchip, nt),
            in_specs=[pl.BlockSpec((1, SUM_ROWS, cols), lambda j, i, c_ref: (j, c_ref[0] * nt + i, 0)), blk],
            out_specs=blk),
        out_shape=jax.ShapeDtypeStruct((nchip, r2 // 2, cols), BF16),
        compiler_params=_cparams("parallel", "parallel"),
    )(jnp.reshape(c, (1,)).astype(jnp.int32), mine, got)


PACK_COLS = 1152
PACK_ROWS = 2592
ROW_TILE = 16


def _pack(parts):
    blocks = []
    for p in parts:
        p2 = p.reshape(-1, p.shape[-1])
        blocks.append(jnp.pad(p2, ((0, -p2.shape[0] % ROW_TILE), (0, PACK_COLS - p2.shape[1]))))
    rows = sum(b.shape[0] for b in blocks)
    blocks.append(jnp.zeros((PACK_ROWS - rows, PACK_COLS), blocks[0].dtype))
    return jnp.concatenate(blocks)


def _unpack(buf, shapes):
    out, at = [], 0
    for s in shapes:
        rows = math.prod(s[:-1])
        out.append(buf[at:at + rows, :s[-1]].reshape(s))
        at += rows + (-rows % ROW_TILE)
    return out


def _pack_small(g):
    blk = jnp.zeros((ROW_TILE, PACK_COLS), F32)
    for i, k in enumerate(SMALL):
        blk = blk.at[2 * i:2 * i + 2, :g[k].shape[1]].set(g[k])
    return blk


def _unpack_small(blk, shapes):
    return [blk[2 * i:2 * i + 2, :s[1]] for i, s in enumerate(shapes)]


def _layer_fwd(x, p, nseq, tabs, target=None):
    proj, hdn = inproj_fwd(x, p["norm_w"][None], p["w_in"])
    ya = conv_fwd(proj, p["conv_w"], nseq)
    oa, states, inverses = gdn_fwd(ya, proj, p["a_log"].reshape(N_HEADS_A, 1, 1), p["dt_bias"].reshape(N_HEADS_A, 1, 1),
                         p["gdn_norm_w"][None], nseq)
    ob, carries = sb_fwd(proj, nseq)
    oc = dil_fwd(proj, jnp.tile(p["q_norm_w"], 2)[None], jnp.tile(p["k_norm_w"], 2)[None], *tabs, nseq)
    if target is None:
        y, mixed = outproj_fwd(x, oa, ob, oc, proj, p["w_out"])
    else:
        dy, mixed, parts = outproj_loss(x, oa, ob, oc, proj, p["w_out"], target)
        y = (dy, parts)
    return y, dict(x=x, hdn=hdn, proj=proj, ya=ya, states=states, inverses=inverses, carries=carries, oa=oa, ob=ob, oc=oc, mixed=mixed)


def _layer_bwd(dy, p, res, nseq, tabs):
    proj = res["proj"]
    g = {}
    g["w_out"] = mat_tn(res["mixed"], [dy])[0]
    doa, dob, doc, dza, dzb, dzc = outproj_bwd(dy, res["oa"], res["ob"], res["oc"], proj, p["w_out"])
    dqc, dkc, dvc, dqw, dkw = dil_bwd(proj, jnp.tile(p["q_norm_w"], 2)[None], jnp.tile(p["k_norm_w"], 2)[None], *tabs, doc,
                                      nseq)
    g["q_norm_w"], g["k_norm_w"] = dqw.reshape(-1, HEAD_DIM).sum(0), dkw.reshape(-1, HEAD_DIM).sum(0)
    dqb, dkb, dvb = sb_bwd(proj, res["carries"], dob, nseq)
    dya, dba, dalog, ddtb, dnw = gdn_bwd(res["ya"], proj, p["a_log"].reshape(N_HEADS_A, 1, 1),
                                         p["dt_bias"].reshape(N_HEADS_A, 1, 1), p["gdn_norm_w"][None], res["states"], res["inverses"], doa,
                                         nseq)
    g["a_log"], g["dt_bias"], g["gdn_norm_w"] = dalog.sum(0).reshape(-1), ddtb.sum(0).reshape(-1), dnw.sum((0, 1))
    dqkv, dcw = conv_bwd(proj, p["conv_w"], dya, nseq)
    g["conv_w"] = dcw.sum(0)
    slabs = [dqkv, dza, dqc, dkc, dvc, dzc, dqb, dkb, dvb, dzb, dba]
    hdn = res["hdn"]
    g["w_in"] = jnp.concatenate(mat_tn(hdn, slabs[:6]) + mat_tn(hdn, slabs[6:]), axis=1)
    dx, dnw_tiles = inproj_bwd(slabs, p["w_in"], res["x"], p["norm_w"][None], dy)
    g["norm_w"] = dnw_tiles.sum((0, 1))
    return dx, g


SMALL = ("norm_w", "a_log", "dt_bias", "gdn_norm_w", "q_norm_w", "k_norm_w")


def _local_step(x, target, full):
    nseq, t, d = x.shape
    tabs = _rope_tables(t)
    h = x.reshape(nseq * t, d)
    saved = []
    for l in range(DEPTH):
        p = {k: v[l] for k, v in full.items()}
        h, res = _layer_fwd(h, p, nseq, tabs, target.reshape(nseq * t, d) if l == DEPTH - 1 else None)
        saved.append((p, res))
    dy, parts = h
    loss = parts[:, 0, 0].sum()
    grads = [None] * DEPTH
    for l in reversed(range(DEPTH)):
        p, res = saved[l]
        dy, grads[l] = _layer_bwd(dy, p, res, nseq, tabs)
    return loss, dy.reshape(nseq, t, d), {k: jnp.stack([g[k] for g in grads]) for k in grads[0]}


def _pad_cols(w):
    b0 = ORIG_A + ORIG_BA
    c0 = b0 + ORIG_B
    zeros = jnp.zeros(w.shape[:-1] + (BLOCK - ORIG_BA,), w.dtype)
    return jnp.concatenate([w[..., :ORIG_A], w[..., c0:], w[..., b0:c0], w[..., ORIG_A:b0], zeros], axis=-1)


def _unpad_cols(w):
    return jnp.concatenate([w[..., :COL_C], w[..., COL_BA:COL_BA + ORIG_BA], w[..., COL_B:COL_BA], w[..., COL_C:COL_B]],
                           axis=-1)


def kernel(x, norm_w, w_in, conv_w, a_log, dt_bias, gdn_norm_w, q_norm_w, k_norm_w, w_out, loss_target, m_norm_w, m_w_in, m_conv_w, m_a_log, m_dt_bias, m_gdn_norm_w, m_q_norm_w, m_k_norm_w, m_w_out, v_norm_w, v_w_in, v_conv_w, v_a_log, v_dt_bias, v_gdn_norm_w, v_q_norm_w, v_k_norm_w, v_w_out):
    weights = dict(norm_w=norm_w, w_in=w_in, conv_w=conv_w, a_log=a_log, dt_bias=dt_bias, gdn_norm_w=gdn_norm_w,
                   q_norm_w=q_norm_w, k_norm_w=k_norm_w, w_out=w_out)
    moms = dict(norm_w=m_norm_w, w_in=m_w_in, conv_w=m_conv_w, a_log=m_a_log, dt_bias=m_dt_bias,
                gdn_norm_w=m_gdn_norm_w, q_norm_w=m_q_norm_w, k_norm_w=m_k_norm_w, w_out=m_w_out)
    vars_ = dict(norm_w=v_norm_w, w_in=v_w_in, conv_w=v_conv_w, a_log=v_a_log, dt_bias=v_dt_bias,
                 gdn_norm_w=v_gdn_norm_w, q_norm_w=v_q_norm_w, k_norm_w=v_k_norm_w, w_out=v_w_out)
    names = list(weights)
    sharded = ("w_in", "w_out", "conv_w")
    shard_shapes = [weights[k].shape for k in sharded]

    c = lax.axis_index("c")
    half = PACK_ROWS // 2
    conv_bits = lax.bitcast_convert_type(conv_w, BF16).reshape(conv_w.shape[:2] + (2 * conv_w.shape[2],))
    shard = _pack([w_in.astype(BF16), w_out.astype(BF16), conv_bits])
    mine = plane_exchange(lax.dynamic_slice_in_dim(shard, c * half, half, axis=0), all_to_all=False)
    other = sibling_swap(mine)
    got = jnp.concatenate([jnp.where(c == 0, mine, other), jnp.where(c == 0, other, mine)], axis=1)
    per_chip = [_unpack(got[i], shard_shapes[:2] + [conv_bits.shape]) for i in range(N_CHIPS)]
    full = {k: weights[k] for k in SMALL}
    full["w_in"] = _pad_cols(jnp.concatenate([pc[0] for pc in per_chip], axis=2))
    full["w_out"] = jnp.concatenate([pc[1] for pc in per_chip], axis=1)
    full["conv_w"] = jnp.concatenate(
        [lax.bitcast_convert_type(pc[2].reshape(conv_w.shape + (2,)), F32) for pc in per_chip], axis=2)

    loss, grad_x, g = _local_step(x, loss_target, full)

    gw_in = _unpad_cols(g["w_in"])
    cols, rows = w_in.shape[2], w_out.shape[1]
    small = _pack_small(g)
    send = jnp.stack([_pack([gw_in[:, :, i * cols:(i + 1) * cols], g["w_out"][:, i * rows:(i + 1) * rows],
                             g["conv_w"][:, :, i * conv_w.shape[2]:(i + 1) * conv_w.shape[2]], small])
                      for i in range(N_CHIPS)])
    chip_sum = add_my_half(send, sibling_swap(send, other_half=True), c)
    mine = sum4(plane_exchange(chip_sum, all_to_all=True))
    other = sibling_swap(mine)
    total = jnp.concatenate([jnp.where(c == 0, mine, other), jnp.where(c == 0, other, mine)])
    reduced = _unpack(total, shard_shapes + [(ROW_TILE, PACK_COLS)])
    grads = dict(zip(sharded, reduced[:3]))
    grads.update(zip(SMALL, _unpack_small(reduced[3], [weights[k].shape for k in SMALL])))
    loss = lax.psum(loss, ("x", "y", "c"))

    def two_d(a):
        return a.reshape(-1, a.shape[-1])

    delta, new_m, new_v = {}, {}, {}
    for k in names:
        d_, m_, v_ = adamw(two_d(weights[k]), two_d(grads[k]), two_d(moms[k]), two_d(vars_[k]))
        delta[k], new_m[k], new_v[k] = (a.reshape(weights[k].shape) for a in (d_, m_, v_))
    return (loss, grad_x, *[grads[k] for k in names], *[delta[k] for k in names],
            *[new_m[k] for k in names], *[new_v[k] for k in names])
```

```python
import functools
import math

import jax
import jax.numpy as jnp
from jax import lax
from jax.experimental import pallas as pl
from jax.experimental.pallas import tpu as pltpu

F32 = jnp.float32
BF16 = jnp.bfloat16

D_MODEL = 1024
SEQ = 2048
DEPTH = 2
HEAD_DIM = 64
N_HEADS_A, N_HEADS_B, N_HEADS_C = 6, 4, 6
A_WIDTH, B_WIDTH, C_WIDTH = N_HEADS_A * HEAD_DIM, N_HEADS_B * HEAD_DIM, N_HEADS_C * HEAD_DIM
CONV_WIDTH = 4
GDN_CHUNK = 64
BLOCK = 128
ROPE_DIM = 16
ROPE_THETA = 500000.0
DILATED_PAIRS = ((128, 1), (512, 4), (2048, 16))
RMS_EPS = 1e-6
NEG = -1e30

NT = (((1,), (1,)), ((), ()))
NN = (((1,), (0,)), ((), ()))
TN = (((0,), (0,)), ((), ()))

VMEM_LIMIT = 48 * 1024 * 1024

ORIG_A = 4 * A_WIDTH
ORIG_BA = 2 * N_HEADS_A
ORIG_B = 4 * B_WIDTH
COL_AZ = 3 * A_WIDTH
COL_C = 4 * A_WIDTH
COL_B = COL_C + 4 * C_WIDTH
COL_BA = COL_B + 4 * B_WIDTH
P_COLS = COL_BA + BLOCK
TN_COLS = 384
INPROJ_PARTS = 4
TM_ROWS = 512
ROWS = 1024


def _mm(a, b, dims=NN):
    return lax.dot_general(a.astype(BF16), b.astype(BF16), dims, preferred_element_type=F32)


def _mm32(a, b, dims=NN):
    return lax.dot_general(a, b, dims, precision=lax.Precision.HIGH, preferred_element_type=F32)


def _cparams(*sem):
    return pltpu.CompilerParams(dimension_semantics=sem, vmem_limit_bytes=VMEM_LIMIT)


def _sigmoid(x):
    return 0.5 * (jnp.tanh(0.5 * x) + 1.0)


def _softplus(x):
    return jnp.maximum(x, 0.0) + jnp.log(1.0 + jnp.exp(-jnp.abs(x)))


def _rms(x, w):
    return x * lax.rsqrt(jnp.mean(x * x, axis=-1, keepdims=True) + RMS_EPS) * w


def _heads(a, n):
    return jnp.stack([a[:, h * HEAD_DIM:(h + 1) * HEAD_DIM] for h in range(n)])


def _unheads(a):
    return jnp.concatenate([a[h] for h in range(a.shape[0])], axis=1)


def inproj_fwd(x, nw, w):
    n, d = x.shape
    p = w.shape[1]

    def body(x_ref, nw_ref, w_ref, proj_ref, hdn_ref):
        step = TM_ROWS // INPROJ_PARTS
        for r in range(INPROJ_PARTS):
            rows = pl.ds(r * step, step)
            h = _rms(x_ref[rows, :], nw_ref[...]).astype(BF16)
            hdn_ref[rows, :] = h
            proj_ref[rows, :] = jnp.dot(h, w_ref[...], preferred_element_type=F32)

    return pl.pallas_call(
        body, name="inproj_fwd", grid=(n // TM_ROWS,),
        in_specs=[pl.BlockSpec((TM_ROWS, d), lambda i: (i, 0)), pl.BlockSpec((1, d), lambda i: (0, 0)),
                  pl.BlockSpec((d, p), lambda i: (0, 0))],
        out_specs=[pl.BlockSpec((TM_ROWS, p), lambda i: (i, 0)), pl.BlockSpec((TM_ROWS, d), lambda i: (i, 0))],
        out_shape=[jax.ShapeDtypeStruct((n, p), F32), jax.ShapeDtypeStruct((n, d), BF16)],
        compiler_params=_cparams("parallel"),
    )(x, nw, w)


def mat_tn(a, slabs):
    n, ka = a.shape
    ns = len(slabs)

    def body(*refs):
        a_ref, s_refs, o_refs = refs[0], refs[1:1 + ns], refs[1 + ns:]

        @pl.when(pl.program_id(0) == 0)
        def _():
            for o_ref in o_refs:
                o_ref[...] = jnp.zeros_like(o_ref)

        at = a_ref[...].T
        for s_ref, o_ref in zip(s_refs, o_refs):
            o_ref[...] += jnp.dot(at, s_ref[...].astype(BF16), preferred_element_type=F32)

    return pl.pallas_call(
        body, name="mat_tn", grid=(n // TM_ROWS,),
        in_specs=[pl.BlockSpec((TM_ROWS, ka), lambda k: (k, 0))]
                 + [pl.BlockSpec((TM_ROWS, s.shape[1]), lambda k: (k, 0)) for s in slabs],
        out_specs=[pl.BlockSpec((ka, s.shape[1]), lambda k: (0, 0)) for s in slabs],
        out_shape=[jax.ShapeDtypeStruct((ka, s.shape[1]), F32) for s in slabs],
        compiler_params=_cparams("arbitrary"),
    )(a, *slabs)


def inproj_bwd(slabs, w, x, nw, dy):
    n, d = x.shape
    p = w.shape[1]
    tm = 256
    ns = len(slabs)

    def body(*refs):
        s_refs = refs[:ns]
        w_ref, x_ref, nw_ref, dy_ref, dx_ref, dnw_ref = refs[ns:]
        dh = jnp.zeros((tm, d), F32)
        at = 0
        for s_ref in s_refs:
            wd = s_ref.shape[1]
            dh = dh + lax.dot_general(s_ref[...].astype(BF16), w_ref[:, at:at + wd], NT, preferred_element_type=F32)
            at += wd
        _, vjp = jax.vjp(_rms, x_ref[...], nw_ref[...])
        dx, dnw = vjp(dh)
        dx_ref[...] = dx + dy_ref[...]
        dnw_ref[0] = dnw

    return pl.pallas_call(
        body, name="inproj_bwd", grid=(n // tm,),
        in_specs=[pl.BlockSpec((tm, s.shape[1]), lambda i: (i, 0)) for s in slabs]
                 + [pl.BlockSpec((d, p), lambda i: (0, 0)), pl.BlockSpec((tm, d), lambda i: (i, 0)),
                    pl.BlockSpec((1, d), lambda i: (0, 0)), pl.BlockSpec((tm, d), lambda i: (i, 0))],
        out_specs=[pl.BlockSpec((tm, d), lambda i: (i, 0)), pl.BlockSpec((1, 1, d), lambda i: (i, 0, 0))],
        out_shape=[jax.ShapeDtypeStruct((n, d), F32), jax.ShapeDtypeStruct((n // tm, 1, d), F32)],
        compiler_params=_cparams("parallel"),
    )(*slabs, w, x, nw, dy)


CONV_PAD = 8
CONV_ROWS = 256


def _conv_pre(pad_s, cw, c):
    xs = [pad_s[pl.ds(c * CONV_ROWS + CONV_PAD - (CONV_WIDTH - 1) + k, CONV_ROWS), :] for k in range(CONV_WIDTH)]
    pre = xs[0] * cw[0:1, :]
    for k in range(1, CONV_WIDTH):
        pre = pre + xs[k] * cw[k:k + 1, :]
    return pre, xs


def conv_fwd(proj, cw, nseq):
    n = proj.shape[0]
    t = n // nseq
    ch = cw.shape[1]

    def body(x_ref, cw_ref, y_ref, pad_s):
        pad_s[pl.ds(0, CONV_PAD), :] = jnp.zeros((CONV_PAD, TN_COLS), F32)
        pad_s[pl.ds(CONV_PAD, t), :] = x_ref[...]
        cwv = cw_ref[...]
        for c in range(t // CONV_ROWS):
            pre, _ = _conv_pre(pad_s, cwv, c)
            y_ref[pl.ds(c * CONV_ROWS, CONV_ROWS), :] = pre * _sigmoid(pre)

    return pl.pallas_call(
        body, name="conv_fwd", grid=(nseq, ch // TN_COLS),
        in_specs=[pl.BlockSpec((t, TN_COLS), lambda b, j: (b, j)), pl.BlockSpec((CONV_WIDTH, TN_COLS), lambda b, j: (0, j))],
        out_specs=pl.BlockSpec((t, TN_COLS), lambda b, j: (b, j)),
        out_shape=jax.ShapeDtypeStruct((n, ch), F32),
        scratch_shapes=[pltpu.VMEM((t + CONV_PAD, TN_COLS), F32)],
        compiler_params=_cparams("parallel", "parallel"),
    )(proj, cw)


def conv_bwd(proj, cw, dy, nseq):
    n = proj.shape[0]
    t = n // nseq
    ch = cw.shape[1]

    def body(x_ref, cw_ref, dy_ref, dx_ref, dcw_ref, pad_s, dpad_s):
        pad_s[pl.ds(0, CONV_PAD), :] = jnp.zeros((CONV_PAD, TN_COLS), F32)
        pad_s[pl.ds(CONV_PAD, t), :] = x_ref[...]
        dpad_s[pl.ds(t, CONV_PAD), :] = jnp.zeros((CONV_PAD, TN_COLS), F32)
        cwv = cw_ref[...]
        acc = [jnp.zeros((1, TN_COLS), F32)] * CONV_WIDTH
        for c in range(t // CONV_ROWS):
            pre, xs = _conv_pre(pad_s, cwv, c)
            sg = _sigmoid(pre)
            dpre = dy_ref[pl.ds(c * CONV_ROWS, CONV_ROWS), :] * (sg * (1.0 + pre * (1.0 - sg)))
            dpad_s[pl.ds(c * CONV_ROWS, CONV_ROWS), :] = dpre
            acc = [acc[k] + jnp.sum(dpre * xs[k], axis=0, keepdims=True) for k in range(CONV_WIDTH)]
        for k in range(CONV_WIDTH):
            dcw_ref[0, pl.ds(k, 1), :] = acc[k]
        for c in range(t // CONV_ROWS):
            dx = dpad_s[pl.ds(c * CONV_ROWS + CONV_WIDTH - 1, CONV_ROWS), :] * cwv[0:1, :]
            for k in range(1, CONV_WIDTH):
                dx = dx + dpad_s[pl.ds(c * CONV_ROWS + CONV_WIDTH - 1 - k, CONV_ROWS), :] * cwv[k:k + 1, :]
            dx_ref[pl.ds(c * CONV_ROWS, CONV_ROWS), :] = dx

    blk = pl.BlockSpec((t, TN_COLS), lambda b, j: (b, j))
    return pl.pallas_call(
        body, name="conv_bwd", grid=(nseq, ch // TN_COLS),
        in_specs=[blk, pl.BlockSpec((CONV_WIDTH, TN_COLS), lambda b, j: (0, j)), blk],
        out_specs=[blk, pl.BlockSpec((1, CONV_WIDTH, TN_COLS), lambda b, j: (b, 0, j))],
        out_shape=[jax.ShapeDtypeStruct((n, ch), F32), jax.ShapeDtypeStruct((nseq, CONV_WIDTH, ch), F32)],
        scratch_shapes=[pltpu.VMEM((t + CONV_PAD, TN_COLS), F32)] * 2,
        compiler_params=_cparams("parallel", "parallel"),
    )(proj, cw, dy)


def _gate_specs(d):
    wide = pl.BlockSpec((TM_ROWS, d), lambda i: (i, 0))
    oa = pl.BlockSpec((TM_ROWS, A_WIDTH), lambda i: (i, 0))
    ob = pl.BlockSpec((TM_ROWS, B_WIDTH), lambda i: (i, 0))
    oc = pl.BlockSpec((TM_ROWS, C_WIDTH), lambda i: (i, 0))
    za = pl.BlockSpec((TM_ROWS, A_WIDTH), lambda i: (i, COL_AZ // A_WIDTH))
    zb = pl.BlockSpec((TM_ROWS, B_WIDTH), lambda i: (i, (COL_B + 3 * B_WIDTH) // B_WIDTH))
    zc = pl.BlockSpec((TM_ROWS, C_WIDTH), lambda i: (i, (COL_C + 3 * C_WIDTH) // C_WIDTH))
    return wide, oa, ob, oc, za, zb, zc


BRANCH_COLS = ((0, A_WIDTH), (A_WIDTH, A_WIDTH + B_WIDTH), (A_WIDTH + B_WIDTH, D_MODEL))


def outproj_fwd(x, oa, ob, oc, proj, w):
    n, d = x.shape

    def body(x_ref, oa_ref, ob_ref, oc_ref, za_ref, zb_ref, zc_ref, w_ref, y_ref, m_ref):
        for (lo, hi), o_ref, z_ref in zip(BRANCH_COLS, (oa_ref, ob_ref, oc_ref), (za_ref, zb_ref, zc_ref)):
            zv = z_ref[...]
            m_ref[:, lo:hi] = (o_ref[...] * (zv * _sigmoid(zv))).astype(BF16)
        y_ref[...] = x_ref[...] + jnp.dot(m_ref[...], w_ref[...], preferred_element_type=F32)

    wide, sa, sb, sc, za, zb, zc = _gate_specs(d)
    return pl.pallas_call(
        body, name="outproj_fwd", grid=(n // TM_ROWS,),
        in_specs=[wide, sa, sb, sc, za, zb, zc, pl.BlockSpec((d, d), lambda i: (0, 0))],
        out_specs=[wide, wide],
        out_shape=[jax.ShapeDtypeStruct((n, d), F32), jax.ShapeDtypeStruct((n, d), BF16)],
        compiler_params=_cparams("parallel"),
    )(x, oa, ob, oc, proj, proj, proj, w)


def outproj_loss(x, oa, ob, oc, proj, w, target):
    n, d = x.shape

    def body(x_ref, oa_ref, ob_ref, oc_ref, za_ref, zb_ref, zc_ref, w_ref, t_ref, dy_ref, m_ref, part_ref):
        for (lo, hi), o_ref, z_ref in zip(BRANCH_COLS, (oa_ref, ob_ref, oc_ref), (za_ref, zb_ref, zc_ref)):
            zv = z_ref[...]
            m_ref[:, lo:hi] = (o_ref[...] * (zv * _sigmoid(zv))).astype(BF16)
        e = x_ref[...] + jnp.dot(m_ref[...], w_ref[...], preferred_element_type=F32) - t_ref[...]
        dy_ref[...] = e * (1.0 / d)
        part_ref[...] = jnp.zeros_like(part_ref) + 0.5 * jnp.sum(e * e) * (1.0 / d)

    wide, sa, sb, sc, za, zb, zc = _gate_specs(d)
    return pl.pallas_call(
        body, name="outproj_loss", grid=(n // TM_ROWS,),
        in_specs=[wide, sa, sb, sc, za, zb, zc, pl.BlockSpec((d, d), lambda i: (0, 0)), wide],
        out_specs=[wide, wide, pl.BlockSpec((1, 8, BLOCK), lambda i: (i, 0, 0))],
        out_shape=[jax.ShapeDtypeStruct((n, d), F32), jax.ShapeDtypeStruct((n, d), BF16),
                   jax.ShapeDtypeStruct((n // TM_ROWS, 8, BLOCK), F32)],
        compiler_params=_cparams("parallel"),
    )(x, oa, ob, oc, proj, proj, proj, w, target)


def outproj_bwd(dy, oa, ob, oc, proj, w, mixed):
    n, d = dy.shape

    def body(dy_ref, oa_ref, ob_ref, oc_ref, za_ref, zb_ref, zc_ref, w_ref, m_ref,
             doa_ref, dob_ref, doc_ref, dza_ref, dzb_ref, dzc_ref, dw_ref):
        @pl.when(pl.program_id(0) == 0)
        def _():
            dw_ref[...] = jnp.zeros_like(dw_ref)

        dyb = dy_ref[...].astype(BF16)
        dw_ref[...] += jnp.dot(m_ref[...].T, dyb, preferred_element_type=F32)
        dm = lax.dot_general(dyb, w_ref[...], NT, preferred_element_type=F32)
        for (lo, hi), o_ref, z_ref, do_ref, dz_ref in zip(BRANCH_COLS, (oa_ref, ob_ref, oc_ref), (za_ref, zb_ref, zc_ref),
                                                          (doa_ref, dob_ref, doc_ref), (dza_ref, dzb_ref, dzc_ref)):
            zv = z_ref[...]
            sg = _sigmoid(zv)
            dmv = dm[:, lo:hi]
            do_ref[...] = dmv * (zv * sg)
            dz_ref[...] = dmv * o_ref[...] * (sg * (1.0 + zv * (1.0 - sg)))

    wide, sa, sb, sc, za, zb, zc = _gate_specs(d)
    sd = jax.ShapeDtypeStruct
    outs = [sd((n, A_WIDTH), F32), sd((n, B_WIDTH), F32), sd((n, C_WIDTH), F32)]
    return pl.pallas_call(
        body, name="outproj_bwd", grid=(n // TM_ROWS,),
        in_specs=[wide, sa, sb, sc, za, zb, zc, pl.BlockSpec((d, d), lambda i: (0, 0)), wide],
        out_specs=[sa, sb, sc, sa, sb, sc, pl.BlockSpec((d, d), lambda i: (0, 0))],
        out_shape=outs + outs + [sd((d, d), F32)],
        compiler_params=_cparams("arbitrary"),
    )(dy, oa, ob, oc, proj, proj, proj, w, mixed)


ADAM_LR, ADAM_B1, ADAM_B2, ADAM_EPS, ADAM_WD, ADAM_STEP = 0.001, 0.9, 0.999, 1e-08, 0.01, 10


def adamw(w, g, m, v):
    r, c = w.shape
    tr = r if r <= 256 else 256

    def body(w_ref, g_ref, m_ref, v_ref, d_ref, nm_ref, nv_ref):
        gv = g_ref[...]
        nm = ADAM_B1 * m_ref[...] + (1.0 - ADAM_B1) * gv
        nv = ADAM_B2 * v_ref[...] + (1.0 - ADAM_B2) * (gv * gv)
        m_hat = nm / (1.0 - ADAM_B1 ** ADAM_STEP)
        v_hat = nv / (1.0 - ADAM_B2 ** ADAM_STEP)
        d_ref[...] = -ADAM_LR * (m_hat / (jnp.sqrt(v_hat) + ADAM_EPS) + ADAM_WD * w_ref[...])
        nm_ref[...] = nm
        nv_ref[...] = nv

    blk = pl.BlockSpec((tr, c), lambda i: (i, 0))
    return pl.pallas_call(
        body, name="adamw", grid=(r // tr,),
        in_specs=[blk] * 4, out_specs=[blk] * 3,
        out_shape=[jax.ShapeDtypeStruct((r, c), F32)] * 3,
        compiler_params=_cparams("parallel"),
    )(w, g, m, v)


SB_G = N_HEADS_B


def _sb_weights(qs, k, carry, tri, diag):
    z = _mm(qs, k, NT)
    sp = jnp.log(1.0 + jnp.exp(-jnp.abs(z)))
    ls_pos = jnp.minimum(z, 0.0) - sp
    ls_neg = jnp.minimum(-z, 0.0) - sp
    earlier = (lax.broadcasted_iota(jnp.int32, z.shape, 1) < lax.broadcasted_iota(jnp.int32, z.shape, 0)) if diag else None
    log_keep = jnp.where(earlier, ls_neg, 0.0) if diag else ls_neg
    hi = log_keep.astype(BF16)
    lo = (log_keep - hi.astype(F32)).astype(BF16)
    within = lax.dot_general(jnp.concatenate([hi, lo], axis=1), tri, NN, preferred_element_type=F32)
    arg = ls_pos + within + carry
    wts = jnp.where(earlier, jnp.exp(jnp.where(earlier, arg, 0.0)), 0.0) if diag else jnp.exp(arg)
    return ls_pos, ls_neg, log_keep, wts, earlier


def _sb_tile(q, k, v, carry, tri, diag):
    _, _, log_keep, wts, _ = _sb_weights(q * (HEAD_DIM ** -0.5), k, carry, tri, diag)
    return _mm(wts, v), jnp.sum(log_keep, axis=1, keepdims=True)


def _sb_tile_grads(q, k, v, carry, do, dtot, tri, diag):
    qs = q * (HEAD_DIM ** -0.5)
    ls_pos, ls_neg, _, wts, earlier = _sb_weights(qs, k, carry, tri, diag)
    dv = _mm(wts, do, TN)
    darg = _mm(do, v, NT) * wts
    dkeep = _mm(darg, tri[:BLOCK], NT) + dtot
    if diag:
        dkeep = jnp.where(earlier, dkeep, 0.0)
    dz = darg * jnp.exp(ls_neg) - dkeep * jnp.exp(ls_pos)
    return _mm(dz, k) * (HEAD_DIM ** -0.5), _mm(dz, qs, TN), dv, jnp.sum(darg, axis=1, keepdims=True)


_sb_tiles_diag = jax.vmap(functools.partial(_sb_tile, diag=True), in_axes=(0, 0, 0, 0, None))
_sb_tiles_off = jax.vmap(functools.partial(_sb_tile, diag=False), in_axes=(0, 0, 0, 0, None))
_sb_grads_diag = jax.vmap(functools.partial(_sb_tile_grads, diag=True), in_axes=(0, 0, 0, 0, 0, 0, None))
_sb_grads_off = jax.vmap(functools.partial(_sb_tile_grads, diag=False), in_axes=(0, 0, 0, 0, 0, 0, None))


def _sb_tri():
    r = lax.broadcasted_iota(jnp.int32, (2 * BLOCK, BLOCK), 0) % BLOCK
    c = lax.broadcasted_iota(jnp.int32, (2 * BLOCK, BLOCK), 1)
    return jnp.where(r > c, 1.0, 0.0).astype(BF16)


SB_SEQ = 2


def _sb_specs(t, nq):
    cb = COL_B // B_WIDTH
    sq = SB_SEQ
    q = pl.BlockSpec((sq, BLOCK, B_WIDTH), lambda b, i: (b, i, cb))
    k = pl.BlockSpec((sq, t, B_WIDTH), lambda b, i: (b, 0, cb + 1))
    v = pl.BlockSpec((sq, t, B_WIDTH), lambda b, i: (b, 0, cb + 2))
    blk = pl.BlockSpec((sq, BLOCK, B_WIDTH), lambda b, i: (b, i, 0))
    full = pl.BlockSpec((sq, t, B_WIDTH), lambda b, i: (b, 0, 0))
    carry = pl.BlockSpec((sq, 1, nq, BLOCK, SB_G), lambda b, i: (b, i, 0, 0, 0))
    return q, k, v, blk, full, carry


def _sb_heads(ref, rows):
    return jnp.concatenate([_heads(ref[b, rows, :], SB_G) for b in range(SB_SEQ)])


def _sb_unheads(a):
    return [_unheads(a[b * SB_G:(b + 1) * SB_G]) for b in range(SB_SEQ)]


def sb_fwd(proj, nseq):
    n = proj.shape[0]
    t = n // nseq
    nq = t // BLOCK
    g, sq = SB_G, SB_SEQ
    everything = pl.ds(0, BLOCK)

    def body(q_ref, k_ref, v_ref, o_ref, carry_ref):
        i = pl.program_id(1)
        tri = _sb_tri()
        qv = _sb_heads(q_ref, everything)

        def tile(j, c, fn):
            rows = pl.ds(pl.multiple_of(j * BLOCK, BLOCK), BLOCK)
            for b in range(sq):
                carry_ref[b, 0, j] = jnp.concatenate([c[b * g + h] for h in range(g)], axis=1)
            return fn(qv, _sb_heads(k_ref, rows), _sb_heads(v_ref, rows), c, tri)

        def step(it, st):
            o_acc, c = st
            o, tot = tile(i - 1 - it, c, _sb_tiles_off)
            return o_acc + o, c + tot

        o_acc, _ = lax.fori_loop(0, i, step, tile(i, jnp.zeros((sq * g, BLOCK, 1), F32), _sb_tiles_diag))
        for b, o in enumerate(_sb_unheads(o_acc)):
            o_ref[b] = o

    q, k, v, blk, _, carry = _sb_specs(t, nq)
    proj3 = proj.reshape(nseq, t, -1)
    o, carries = pl.pallas_call(
        body, name="sb_fwd", grid=(nseq // sq, nq),
        in_specs=[q, k, v],
        out_specs=[blk, carry],
        out_shape=[jax.ShapeDtypeStruct((nseq, t, B_WIDTH), F32),
                   jax.ShapeDtypeStruct((nseq, nq, nq, BLOCK, g), F32)],
        compiler_params=_cparams("parallel", "arbitrary"),
    )(proj3, proj3, proj3)
    return o.reshape(n, B_WIDTH), carries


def sb_bwd(proj, carries, do, nseq):
    n = proj.shape[0]
    t = n // nseq
    nq = t // BLOCK
    g, sq = SB_G, SB_SEQ
    everything = pl.ds(0, BLOCK)

    def body(q_ref, k_ref, v_ref, carry_ref, do_ref, dq_ref, dk_ref, dv_ref):
        i = pl.program_id(1)

        @pl.when(i == 0)
        def _():
            dk_ref[...] = jnp.zeros_like(dk_ref)
            dv_ref[...] = jnp.zeros_like(dv_ref)

        tri = _sb_tri()
        qv = _sb_heads(q_ref, everything)
        dov = _sb_heads(do_ref, everything)

        def tile(j, st, fn):
            dq_acc, dc = st
            rows = pl.ds(pl.multiple_of(j * BLOCK, BLOCK), BLOCK)
            cj = [carry_ref[b, 0, j] for b in range(sq)]
            dq, dk, dv, dcj = fn(qv, _sb_heads(k_ref, rows), _sb_heads(v_ref, rows),
                                 jnp.stack([cj[b][:, h:h + 1] for b in range(sq) for h in range(g)]), dov, dc, tri)
            for b, (dkb, dvb) in enumerate(zip(_sb_unheads(dk), _sb_unheads(dv))):
                dk_ref[b, rows, :] += dkb
                dv_ref[b, rows, :] += dvb
            return dq_acc + dq, dc + dcj

        st = lax.fori_loop(0, i, lambda j, st: tile(j, st, _sb_grads_off),
                           (jnp.zeros((sq * g, BLOCK, HEAD_DIM), F32), jnp.zeros((sq * g, BLOCK, 1), F32)))
        dq_acc, _ = tile(i, st, _sb_grads_diag)
        for b, dq in enumerate(_sb_unheads(dq_acc)):
            dq_ref[b] = dq

    q, k, v, blk, full, carry = _sb_specs(t, nq)
    proj3, do3 = proj.reshape(nseq, t, -1), do.reshape(nseq, t, -1)
    grads = pl.pallas_call(
        body, name="sb_bwd", grid=(nseq // sq, nq),
        in_specs=[q, k, v, carry, blk],
        out_specs=[blk, full, full],
        out_shape=[jax.ShapeDtypeStruct((nseq, t, B_WIDTH), F32)] * 3,
        compiler_params=_cparams("parallel", "arbitrary"),
    )(proj3, proj3, proj3, carries, do3)
    return [a.reshape(n, B_WIDTH) for a in grads]


def _unit_lower_inverse(a):
    n = a.shape[0]
    eye = jnp.where(lax.broadcasted_iota(jnp.int32, (n, n), 0) == lax.broadcasted_iota(jnp.int32, (n, n), 1), 1.0, 0.0)
    tmat = eye.astype(F32) - a
    p = a
    for _ in range(5):
        p = _mm32(p, p)
        tmat = tmat + _mm32(tmat, p)
    return tmat


@jax.custom_vjp
def _known_inverse(a, tmat):
    return tmat


def _known_inverse_fwd(a, tmat):
    return tmat, tmat


def _known_inverse_bwd(tmat, g):
    return -_mm32(_mm32(tmat, g, TN), tmat, NT), jnp.zeros_like(tmat)


_known_inverse.defvjp(_known_inverse_fwd, _known_inverse_bwd)


def _gdn_chunk(q, k, v, al_c, al_r, br_c, alog, dtb, nw, s, tmat_in):
    c = GDN_CHUNK
    ri = lax.broadcasted_iota(jnp.int32, (c, c), 0)
    ci = lax.broadcasted_iota(jnp.int32, (c, c), 1)
    incl, strict = ri >= ci, ri > ci
    rate = -jnp.exp(alog)
    g_c = rate * _softplus(al_c + dtb)
    g_r = rate * _softplus(al_r + dtb)
    beta = _sigmoid(br_c)
    gc_c = jnp.sum(jnp.where(incl, g_r, 0.0), axis=1, keepdims=True)
    gc_r = jnp.sum(jnp.where(ri <= ci, g_c, 0.0), axis=0, keepdims=True)
    gl = jnp.sum(g_r, axis=1, keepdims=True)
    decay = jnp.where(incl, jnp.exp(jnp.where(incl, gc_c - gc_r, 0.0)), 0.0)
    qn = q * lax.rsqrt(jnp.sum(q * q, axis=-1, keepdims=True) + RMS_EPS) * (HEAD_DIM ** -0.5)
    kn = k * lax.rsqrt(jnp.sum(k * k, axis=-1, keepdims=True) + RMS_EPS)
    kb = kn * beta
    a = jnp.where(strict, _mm(kb, kn, NT) * decay, 0.0)
    tmat = _unit_lower_inverse(a) if tmat_in is None else _known_inverse(a, tmat_in)
    u = _mm(tmat, v * beta)
    w = _mm(tmat, kb * jnp.exp(gc_c))
    qk = _mm(qn, kn, NT) * decay
    v_new = u - _mm(w, s)
    o = _mm(qn * jnp.exp(gc_c), s) + _mm(qk, v_new)
    s_new = s * jnp.exp(gl) + _mm(kn * jnp.exp(gl - gc_c), v_new, TN)
    o = o * lax.rsqrt(jnp.mean(o * o, axis=-1, keepdims=True) + RMS_EPS) * nw
    return o, s_new, tmat


_gdn_chunks_fwd = jax.vmap(functools.partial(_gdn_chunk, tmat_in=None), in_axes=(0, 0, 0, 0, 0, 0, 0, 0, None, 0))
_gdn_chunks_bwd = jax.vmap(_gdn_chunk, in_axes=(0, 0, 0, 0, 0, 0, 0, 0, None, 0, 0))

GDN_TB = 256
GDN_SEQ_FWD = 2
GDN_SEQ_BWD = 2


def _gdn_block(q3, k3, v3, ba, alog, dtb, nw, s, tm=None):
    nh = N_HEADS_A
    ns = q3.shape[0]
    bat = [ba[b].T for b in range(ns)]
    br_c = jnp.stack([ba[b][:, h:h + 1] for b in range(ns) for h in range(nh)])
    al_c = jnp.stack([ba[b][:, nh + h:nh + h + 1] for b in range(ns) for h in range(nh)])
    al_r = jnp.stack([bat[b][nh + h:nh + h + 1, :] for b in range(ns) for h in range(nh)])
    heads = lambda a: jnp.concatenate([_heads(a[b], nh) for b in range(ns)])
    args = (heads(q3), heads(k3), heads(v3), al_c, al_r, br_c, jnp.concatenate([alog] * ns), jnp.concatenate([dtb] * ns), nw, s)
    o, s_new, tmat = _gdn_chunks_fwd(*args) if tm is None else _gdn_chunks_bwd(*args, tm)
    o3 = jnp.stack([_unheads(o[b * nh:(b + 1) * nh]) for b in range(ns)])
    return (o3, s_new, tmat) if tm is None else (o3, s_new)


def _gdn_specs(nt, sq, rev):
    tpos = (lambda i: nt - 1 - i) if rev else (lambda i: i)
    ncb = GDN_TB // GDN_CHUNK
    qkv = [pl.BlockSpec((sq, GDN_TB, A_WIDTH), lambda b, i, j=j: (b, tpos(i), j)) for j in range(3)]
    ba = pl.BlockSpec((sq, GDN_TB, BLOCK), lambda b, i: (b, tpos(i), COL_BA // BLOCK))
    one = pl.BlockSpec((N_HEADS_A, 1, 1), lambda b, i: (0, 0, 0))
    vec = pl.BlockSpec((1, HEAD_DIM), lambda b, i: (0, 0))
    st = pl.BlockSpec((sq, ncb, N_HEADS_A, HEAD_DIM, HEAD_DIM), lambda b, i: (b, tpos(i), 0, 0, 0))
    oa = pl.BlockSpec((sq, GDN_TB, A_WIDTH), lambda b, i: (b, tpos(i), 0))
    return qkv, ba, one, vec, st, oa, tpos


def gdn_fwd(ya, proj, alog, dtb, nw, nseq):
    n = ya.shape[0]
    t = n // nseq
    nc, nt, ncb = t // GDN_CHUNK, t // GDN_TB, GDN_TB // GDN_CHUNK
    sq = GDN_SEQ_FWD
    nh = N_HEADS_A

    def body(q_ref, k_ref, v_ref, ba_ref, alog_ref, dtb_ref, nw_ref, o_ref, st_ref, tm_ref, s_s):
        @pl.when(pl.program_id(1) == 0)
        def _():
            s_s[...] = jnp.zeros_like(s_s)

        def step(c, s):
            rows = pl.ds(pl.multiple_of(c * GDN_CHUNK, GDN_CHUNK), GDN_CHUNK)
            o, s_new, tmat = _gdn_block(q_ref[:, rows, :], k_ref[:, rows, :], v_ref[:, rows, :], ba_ref[:, rows, :],
                                        alog_ref[...], dtb_ref[...], nw_ref[...], s)
            for b in range(sq):
                st_ref[b, c] = s[b * nh:(b + 1) * nh]
                tm_ref[b, c] = tmat[b * nh:(b + 1) * nh]
            o_ref[:, rows, :] = o
            return s_new

        s_s[...] = lax.fori_loop(0, ncb, step, s_s[...])

    qkv, ba, one, vec, st, oa, _ = _gdn_specs(nt, sq, False)
    ya3, proj3 = ya.reshape(nseq, t, -1), proj.reshape(nseq, t, -1)
    per_chunk = jax.ShapeDtypeStruct((nseq, nc, nh, HEAD_DIM, HEAD_DIM), F32)
    o, states, inverses = pl.pallas_call(
        body, name="gdn_fwd", grid=(nseq // sq, nt),
        in_specs=qkv + [ba, one, one, vec],
        out_specs=[oa, st, st],
        out_shape=[jax.ShapeDtypeStruct((nseq, t, A_WIDTH), F32), per_chunk, per_chunk],
        scratch_shapes=[pltpu.VMEM((sq * nh, HEAD_DIM, HEAD_DIM), F32)],
        compiler_params=_cparams("parallel", "arbitrary"),
    )(ya3, ya3, ya3, proj3, alog, dtb, nw)
    return o.reshape(n, A_WIDTH), states, inverses


def gdn_bwd(ya, proj, alog, dtb, nw, states, inverses, do, nseq):
    n = ya.shape[0]
    t = n // nseq
    nt, ncb = t // GDN_TB, GDN_TB // GDN_CHUNK
    nh = N_HEADS_A
    sq = GDN_SEQ_BWD

    def body(q_ref, k_ref, v_ref, ba_ref, alog_ref, dtb_ref, nw_ref, st_ref, tm_ref, do_ref,
             dya_ref, dba_ref, dalog_ref, ddtb_ref, dnw_ref, ds_s):
        @pl.when(pl.program_id(1) == 0)
        def _():
            ds_s[...] = jnp.zeros_like(ds_s)
            dalog_ref[...] = jnp.zeros_like(dalog_ref)
            ddtb_ref[...] = jnp.zeros_like(ddtb_ref)
            dnw_ref[...] = jnp.zeros_like(dnw_ref)

        def step(it, carry):
            ds, dalog, ddtb, dnw = carry
            c = ncb - 1 - it
            rows = pl.ds(pl.multiple_of(c * GDN_CHUNK, GDN_CHUNK), GDN_CHUNK)
            s_in = jnp.concatenate([st_ref[b, c] for b in range(sq)])
            tm_in = jnp.concatenate([tm_ref[b, c] for b in range(sq)])
            _, vjp = jax.vjp(functools.partial(_gdn_block, tm=tm_in), q_ref[:, rows, :], k_ref[:, rows, :], v_ref[:, rows, :],
                             ba_ref[:, rows, :], alog_ref[...], dtb_ref[...], nw_ref[...], s_in)
            dq, dk, dv, dba, da, dd, dn, ds = vjp((do_ref[:, rows, :], ds))
            dya_ref[:, rows, 0:A_WIDTH] = dq
            dya_ref[:, rows, A_WIDTH:2 * A_WIDTH] = dk
            dya_ref[:, rows, 2 * A_WIDTH:3 * A_WIDTH] = dv
            dba_ref[:, rows, :] = dba
            return ds, dalog + da, ddtb + dd, dnw + dn

        z11 = jnp.zeros((nh, 1, 1), F32)
        ds, dalog, ddtb, dnw = lax.fori_loop(0, ncb, step, (ds_s[...], z11, z11, jnp.zeros((1, HEAD_DIM), F32)))
        ds_s[...] = ds
        dalog_ref[0] += dalog
        ddtb_ref[0] += ddtb
        dnw_ref[0] += dnw

    qkv, ba, one, vec, st, oa, tpos = _gdn_specs(nt, sq, True)
    per_grp = pl.BlockSpec((1, nh, 1, 1), lambda b, i: (b, 0, 0, 0))
    sd = jax.ShapeDtypeStruct
    ya3, proj3, do3 = ya.reshape(nseq, t, -1), proj.reshape(nseq, t, -1), do.reshape(nseq, t, -1)
    dya, dba, dalog, ddtb, dnw = pl.pallas_call(
        body, name="gdn_bwd", grid=(nseq // sq, nt),
        in_specs=qkv + [ba, one, one, vec, st, st, oa],
        out_specs=[pl.BlockSpec((sq, GDN_TB, 3 * A_WIDTH), lambda b, i: (b, tpos(i), 0)),
                   pl.BlockSpec((sq, GDN_TB, BLOCK), lambda b, i: (b, tpos(i), 0)),
                   per_grp, per_grp, pl.BlockSpec((1, 1, HEAD_DIM), lambda b, i: (b, 0, 0))],
        out_shape=[sd((nseq, t, 3 * A_WIDTH), F32), sd((nseq, t, BLOCK), F32), sd((nseq // sq, nh, 1, 1), F32),
                   sd((nseq // sq, nh, 1, 1), F32), sd((nseq // sq, 1, HEAD_DIM), F32)],
        scratch_shapes=[pltpu.VMEM((sq * nh, HEAD_DIM, HEAD_DIM), F32)],
        compiler_params=_cparams("parallel", "arbitrary"),
    )(ya3, ya3, ya3, proj3, alog, dtb, nw, states, inverses, do3)
    return dya.reshape(n, 3 * A_WIDTH), dba.reshape(n, BLOCK), dalog, ddtb, dnw


DIL_NB = tuple((SEQ // d) // BLOCK for _, d in DILATED_PAIRS)
DIL_D = tuple(d for _, d in DILATED_PAIRS)
DIL_STEPS = tuple(w // d for w, d in DILATED_PAIRS)
DIL_B = 8
PAIR = 2 * HEAD_DIM


def _rope_tables(t):
    half = ROPE_DIM // 2
    inv_freq = ROPE_THETA ** (-jnp.arange(half, dtype=F32) / half)
    ang = jnp.arange(t, dtype=F32)[:, None] * inv_freq[None, :]
    ones = jnp.ones((t, HEAD_DIM - ROPE_DIM), F32)
    cs = jnp.concatenate([jnp.cos(ang), jnp.cos(ang), ones], axis=1)
    sn = jnp.concatenate([jnp.sin(ang), jnp.sin(ang), 0.0 * ones], axis=1)
    i = jnp.arange(PAIR)[:, None]
    j = jnp.arange(PAIR)[None, :]
    same = (i // HEAD_DIM) == (j // HEAD_DIM)
    ih, jh = i % HEAD_DIM, j % HEAD_DIM
    pm = (jnp.where(same & (jh < half) & (ih == jh + half), -1.0, 0.0)
          + jnp.where(same & (jh >= half) & (jh < ROPE_DIM) & (ih == jh - half), 1.0, 0.0))
    mean = jnp.where(same, 1.0 / HEAD_DIM, 0.0)
    twice = lambda m: jnp.concatenate([m, m]).astype(BF16)
    return jnp.tile(cs, (1, 2)), jnp.tile(sn, (1, 2)), twice(mean), twice(pm)


def _split_dot(x, w2):
    hi = x.astype(BF16)
    lo = lax.stop_gradient(x - hi.astype(F32)).astype(BF16)
    return lax.dot_general(jnp.concatenate([hi, lo], axis=1), w2, NN, preferred_element_type=F32)


def _dil_prep(x, w, cs, sn, mean2, pm2):
    y = x * lax.rsqrt(_split_dot(x * x, mean2) + RMS_EPS) * w
    return y * cs + _split_dot(y, pm2) * sn


def _dil_tile(qn, kk, vv, bias):
    lane = lax.broadcasted_iota(jnp.int32, (1, PAIR), 1)
    outs, lses = [], []
    for h in range(2):
        s = _mm(jnp.where(lane // HEAD_DIM == h, qn, 0.0) * (HEAD_DIM ** -0.5), kk, NT) + bias
        m = lax.stop_gradient(jnp.max(s, axis=-1, keepdims=True))
        p = jnp.exp(s - m)
        denom = jnp.sum(p, axis=-1, keepdims=True)
        outs.append(_mm(p, vv) / denom)
        lses.append(m + jnp.log(denom))
    return jnp.where(lane < HEAD_DIM, outs[0], outs[1]), jnp.concatenate(lses, axis=1)


_dil_tiles = jax.vmap(_dil_tile)


def _spread(a):
    lane = lax.broadcasted_iota(jnp.int32, (a.shape[0], PAIR), 1)
    return jnp.where(lane < HEAD_DIM, a[:, 0:1], a[:, 1:2])


def _dil_mix(o1, o2, o3, l1, l2, l3):
    m = lax.stop_gradient(jnp.maximum(jnp.maximum(l1, l2), l3))
    e1, e2, e3 = jnp.exp(l1 - m), jnp.exp(l2 - m), jnp.exp(l3 - m)
    r = 1.0 / (e1 + e2 + e3)
    return _spread(e1 * r) * o1 + _spread(e2 * r) * o2 + _spread(e3 * r) * o3


def _dil_fill_biases(bias_s):
    steps, = set(DIL_STEPS)
    qi = lax.broadcasted_iota(jnp.int32, (BLOCK, 1), 0)
    kj = lax.broadcasted_iota(jnp.int32, (1, 2 * BLOCK), 1)
    rel = qi - kj + BLOCK
    inside = (rel >= 0) & (rel <= steps)
    bias_s[0] = jnp.where(inside, 0.0, NEG)
    bias_s[1] = jnp.where(inside & (kj >= BLOCK), 0.0, NEG)
    bias_s[2] = jnp.where((qi >= kj) & (qi - kj <= steps), 0.0, NEG)


def _dil_mask(it, g, bias_s):
    qrows = pl.ds(pl.multiple_of(it * BLOCK, BLOCK), BLOCK)
    if DIL_NB[g] == 1:
        return bias_s[2, :, 0:BLOCK], qrows, qrows
    which = jnp.where(it == 0, 2, jnp.where(it % DIL_NB[g] == 0, 1, 0))
    kstart = jnp.maximum(it - 1, 0) * BLOCK
    return bias_s[which], qrows, pl.ds(pl.multiple_of(kstart, BLOCK), 2 * BLOCK)


def _dil_gather(src, dst, d):
    t = src.shape[0]
    ln = t // d
    for r in range(d):
        dst[pl.ds(r * ln, ln), :] = src[pl.ds(r, ln, stride=d), :]


def _dil_scatter(src, dst, d):
    t = src.shape[0]
    ln = t // d
    for r in range(d):
        dst[pl.ds(r, ln, stride=d), :] = src[pl.ds(r * ln, ln), :]


def _dil_forward_parts(q_ref, k_ref, v_ref, qw, kw, cs_ref, sn_ref, mean2, pm2, qn_s, kn_s, dl_s, od_s, ld_s, on_s, ln_s, bias_s):
    t = qn_s.shape[0]
    _dil_fill_biases(bias_s)

    def prep(c, _):
        rows = pl.ds(pl.multiple_of(c * ROWS, ROWS), ROWS)
        qn_s[rows, :] = _dil_prep(q_ref[rows, :], qw, cs_ref[rows, :], sn_ref[rows, :], mean2, pm2)
        kn_s[rows, :] = _dil_prep(k_ref[rows, :], kw, cs_ref[rows, :], sn_ref[rows, :], mean2, pm2)
        return 0

    lax.fori_loop(0, t // ROWS, prep, 0)
    for g in (1, 2):
        _dil_gather(qn_s, dl_s.at[g - 1, 0], DIL_D[g])
        _dil_gather(kn_s, dl_s.at[g - 1, 1], DIL_D[g])
        _dil_gather(v_ref, dl_s.at[g - 1, 2], DIL_D[g])
    for g in range(3):
        qs = qn_s if g == 0 else dl_s.at[g - 1, 0]
        ks = kn_s if g == 0 else dl_s.at[g - 1, 1]
        vs = v_ref if g == 0 else dl_s.at[g - 1, 2]

        def tiles(i, _, g=g, qs=qs, ks=ks, vs=vs):
            where = [_dil_mask(i * DIL_B + b, g, bias_s) for b in range(DIL_B)]
            o, lse = _dil_tiles(jnp.stack([qs[qr, :] for _, qr, _ in where]), jnp.stack([ks[kr, :] for _, _, kr in where]),
                                jnp.stack([vs[kr, :] for _, _, kr in where]), jnp.stack([m for m, _, _ in where]))
            for b, (_, qr, _) in enumerate(where):
                od_s[g, qr, :] = o[b]
                ld_s[g, qr, :] = lse[b]
            return 0

        lax.fori_loop(0, t // BLOCK // DIL_B, tiles, 0)
    for g in (1, 2):
        _dil_scatter(od_s.at[g], on_s.at[g - 1], DIL_D[g])
        _dil_scatter(ld_s.at[g], ln_s.at[g - 1], DIL_D[g])


def _dil_scratch(t):
    return [pltpu.VMEM((t, PAIR), F32), pltpu.VMEM((t, PAIR), F32),
            pltpu.VMEM((2, 3, t, PAIR), F32),
            pltpu.VMEM((3, t, PAIR), F32), pltpu.VMEM((3, t, 2), F32),
            pltpu.VMEM((2, t, PAIR), F32), pltpu.VMEM((2, t, 2), F32),
            pltpu.VMEM((3, BLOCK, 2 * BLOCK), F32)]


def _dil_specs(t):
    cb = COL_C // BLOCK
    per = C_WIDTH // BLOCK
    qkv = [pl.BlockSpec((t, BLOCK), lambda b, p, j=j: (b, cb + j * per + p)) for j in range(3)]
    vec = pl.BlockSpec((1, PAIR), lambda b, p: (0, 0))
    tab = pl.BlockSpec((t, PAIR), lambda b, p: (0, 0))
    mat = pl.BlockSpec((2 * PAIR, PAIR), lambda b, p: (0, 0))
    pair = pl.BlockSpec((t, BLOCK), lambda b, p: (b, p))
    return qkv, vec, tab, mat, pair


def dil_fwd(proj, qw, kw, cs, sn, mean2, pm2, nseq):
    n = proj.shape[0]
    t = n // nseq

    def body(q_ref, k_ref, v_ref, qw_ref, kw_ref, cs_ref, sn_ref, mean_ref, pm_ref, o_ref,
             qn_s, kn_s, dl_s, od_s, ld_s, on_s, ln_s, bias_s):
        _dil_forward_parts(q_ref, k_ref, v_ref, qw_ref[...], kw_ref[...], cs_ref, sn_ref, mean_ref[...], pm_ref[...],
                           qn_s, kn_s, dl_s, od_s, ld_s, on_s, ln_s, bias_s)

        def mix(c, _):
            rows = pl.ds(pl.multiple_of(c * ROWS, ROWS), ROWS)
            o_ref[rows, :] = _dil_mix(od_s[0, rows, :], on_s[0, rows, :], on_s[1, rows, :],
                                      ld_s[0, rows, :], ln_s[0, rows, :], ln_s[1, rows, :])
            return 0

        lax.fori_loop(0, t // ROWS, mix, 0)

    qkv, vec, tab, mat, pair = _dil_specs(t)
    return pl.pallas_call(
        body, name="dil_fwd", grid=(nseq, C_WIDTH // BLOCK),
        in_specs=qkv + [vec, vec, tab, tab, mat, mat],
        out_specs=pair,
        out_shape=jax.ShapeDtypeStruct((n, C_WIDTH), F32),
        scratch_shapes=_dil_scratch(t),
        compiler_params=_cparams("parallel", "parallel"),
    )(proj, proj, proj, qw, kw, cs, sn, mean2, pm2)


def dil_bwd(proj, qw, kw, cs, sn, mean2, pm2, do, nseq):
    n = proj.shape[0]
    t = n // nseq

    def body(q_ref, k_ref, v_ref, qw_ref, kw_ref, cs_ref, sn_ref, mean_ref, pm_ref, do_ref,
             dq_ref, dk_ref, dv_ref, dqw_ref, dkw_ref,
             qn_s, kn_s, dl_s, od_s, ld_s, on_s, ln_s, bias_s, tq_s, tk_s, tv_s):
        qw, kw, mean2, pm2 = qw_ref[...], kw_ref[...], mean_ref[...], pm_ref[...]
        _dil_forward_parts(q_ref, k_ref, v_ref, qw, kw, cs_ref, sn_ref, mean2, pm2, qn_s, kn_s, dl_s, od_s, ld_s, on_s, ln_s, bias_s)

        def mix(c, _):
            rows = pl.ds(pl.multiple_of(c * ROWS, ROWS), ROWS)
            _, vjp = jax.vjp(_dil_mix, od_s[0, rows, :], on_s[0, rows, :], on_s[1, rows, :],
                             ld_s[0, rows, :], ln_s[0, rows, :], ln_s[1, rows, :])
            d1, d2, d3, e1, e2, e3 = vjp(do_ref[rows, :])
            od_s[0, rows, :] = d1
            on_s[0, rows, :] = d2
            on_s[1, rows, :] = d3
            ld_s[0, rows, :] = e1
            ln_s[0, rows, :] = e2
            ln_s[1, rows, :] = e3
            return 0

        lax.fori_loop(0, t // ROWS, mix, 0)
        for g in (1, 2):
            _dil_gather(on_s.at[g - 1], od_s.at[g], DIL_D[g])
            _dil_gather(ln_s.at[g - 1], ld_s.at[g], DIL_D[g])
        on_s[...] = jnp.zeros_like(on_s)
        dv_ref[...] = jnp.zeros_like(dv_ref)
        for g in range(3):
            qs = qn_s if g == 0 else dl_s.at[g - 1, 0]
            ks = kn_s if g == 0 else dl_s.at[g - 1, 1]
            vs = v_ref if g == 0 else dl_s.at[g - 1, 2]
            gq = on_s.at[0] if g == 0 else tq_s
            gk = on_s.at[1] if g == 0 else tk_s
            gv = dv_ref if g == 0 else tv_s
            if g > 0:
                tk_s[...] = jnp.zeros_like(tk_s)
                tv_s[...] = jnp.zeros_like(tv_s)

            def tiles(i, _, g=g, qs=qs, ks=ks, vs=vs, gq=gq, gk=gk, gv=gv):
                where = [_dil_mask(i * DIL_B + b, g, bias_s) for b in range(DIL_B)]
                biases = jnp.stack([m for m, _, _ in where])
                _, vjp = jax.vjp(lambda q_, k_, v_: _dil_tiles(q_, k_, v_, biases),
                                 jnp.stack([qs[qr, :] for _, qr, _ in where]), jnp.stack([ks[kr, :] for _, _, kr in where]),
                                 jnp.stack([vs[kr, :] for _, _, kr in where]))
                dq, dkk, dvv = vjp((jnp.stack([od_s[g, qr, :] for _, qr, _ in where]),
                                    jnp.stack([ld_s[g, qr, :] for _, qr, _ in where])))
                for b, (_, qr, kr) in enumerate(where):
                    gq[qr, :] = dq[b]
                    gk[kr, :] += dkk[b]
                    gv[kr, :] += dvv[b]
                return 0

            lax.fori_loop(0, t // BLOCK // DIL_B, tiles, 0)
            if g > 0:
                d = DIL_D[g]
                ln = t // d
                for r in range(d):
                    nat, dil = pl.ds(r, ln, stride=d), pl.ds(r * ln, ln)
                    on_s[0, nat, :] += tq_s[dil, :]
                    on_s[1, nat, :] += tk_s[dil, :]
                    dv_ref[nat, :] += tv_s[dil, :]

        def prep(c, acc):
            rows = pl.ds(pl.multiple_of(c * ROWS, ROWS), ROWS)
            f = lambda x, w: _dil_prep(x, w, cs_ref[rows, :], sn_ref[rows, :], mean2, pm2)
            _, vq = jax.vjp(f, q_ref[rows, :], qw)
            _, vk = jax.vjp(f, k_ref[rows, :], kw)
            dq, dqw = vq(on_s[0, rows, :])
            dk, dkw = vk(on_s[1, rows, :])
            dq_ref[rows, :] = dq
            dk_ref[rows, :] = dk
            return acc[0] + dqw, acc[1] + dkw

        dqw, dkw = lax.fori_loop(0, t // ROWS, prep, (jnp.zeros((1, PAIR), F32), jnp.zeros((1, PAIR), F32)))
        dqw_ref[0] = dqw
        dkw_ref[0] = dkw

    qkv, vec, tab, mat, pair = _dil_specs(t)
    per = C_WIDTH // BLOCK
    wout = pl.BlockSpec((1, 1, PAIR), lambda b, p: (b * per + p, 0, 0))
    return pl.pallas_call(
        body, name="dil_bwd", grid=(nseq, per),
        in_specs=qkv + [vec, vec, tab, tab, mat, mat, pair],
        out_specs=[pair, pair, pair, wout, wout],
        out_shape=[jax.ShapeDtypeStruct((n, C_WIDTH), F32)] * 3 + [jax.ShapeDtypeStruct((nseq * per, 1, PAIR), F32)] * 2,
        scratch_shapes=_dil_scratch(t) + [pltpu.VMEM((t, PAIR), F32)] * 3,
        compiler_params=_cparams("parallel", "parallel"),
    )(proj, proj, proj, qw, kw, cs, sn, mean2, pm2, do)


N_CHIPS = 4
SUM_ROWS = 432
MESH_IDS = pl.DeviceIdType.MESH
ANY = pl.BlockSpec(memory_space=pl.ANY)


def plane_exchange(src, all_to_all):
    blk_shape = src.shape[1:] if all_to_all else src.shape

    def body(src_ref, out_ref, send_sems, recv_sems, local_sem):
        x, y, c = lax.axis_index("x"), lax.axis_index("y"), lax.axis_index("c")
        me = 2 * x + y
        mine = pltpu.make_async_copy(src_ref.at[me] if all_to_all else src_ref, out_ref.at[me], local_sem)
        mine.start()
        sends = []
        for k in (1, 2, 3):
            px = 1 - x if k & 2 else x
            py = 1 - y if k & 1 else y
            peer = 2 * px + py
            cp = pltpu.make_async_remote_copy(
                src_ref=src_ref.at[peer] if all_to_all else src_ref, dst_ref=out_ref.at[me],
                send_sem=send_sems.at[k - 1], recv_sem=recv_sems.at[k - 1],
                device_id=(px, py, c), device_id_type=MESH_IDS)
            cp.start()
            sends.append((cp, peer, (px, py, c)))
        for k, (cp, peer, dev) in enumerate(sends):
            pltpu.make_async_remote_copy(
                src_ref=out_ref.at[me], dst_ref=out_ref.at[peer],
                send_sem=send_sems.at[k], recv_sem=recv_sems.at[k],
                device_id=dev, device_id_type=MESH_IDS).wait_recv()
        for cp, _, _ in sends:
            cp.wait_send()
        mine.wait()

    return pl.pallas_call(
        body, name="plane_all_to_all" if all_to_all else "plane_all_gather",
        in_specs=[ANY], out_specs=ANY,
        out_shape=jax.ShapeDtypeStruct((N_CHIPS,) + blk_shape, src.dtype),
        scratch_shapes=[pltpu.SemaphoreType.DMA((3,)), pltpu.SemaphoreType.DMA((3,)), pltpu.SemaphoreType.DMA],
    )(src)


def sibling_swap(src, other_half=False):
    shape = (src.shape[0], src.shape[1] // 2) + src.shape[2:] if other_half else src.shape

    def body(src_ref, out_ref, send_sem, recv_sem):
        x, y, c = lax.axis_index("x"), lax.axis_index("y"), lax.axis_index("c")
        part = src_ref.at[:, pl.ds((1 - c) * shape[1], shape[1])] if other_half else src_ref
        cp = pltpu.make_async_remote_copy(src_ref=part, dst_ref=out_ref, send_sem=send_sem, recv_sem=recv_sem,
                                          device_id=(x, y, 1 - c), device_id_type=MESH_IDS)
        cp.start()
        cp.wait()

    return pl.pallas_call(
        body, name="sibling_swap", in_specs=[ANY], out_specs=ANY,
        out_shape=jax.ShapeDtypeStruct(shape, src.dtype),
        scratch_shapes=[pltpu.SemaphoreType.DMA, pltpu.SemaphoreType.DMA],
    )(src)


def sum4(a):
    _, r, c = a.shape
    tr = SUM_ROWS

    def body(a_ref, o_ref):
        p = [a_ref[i].astype(F32) for i in range(N_CHIPS)]
        o_ref[...] = (p[0] + p[1]) + (p[2] + p[3])

    return pl.pallas_call(
        body, name="sum4", grid=(r // tr,),
        in_specs=[pl.BlockSpec((N_CHIPS, tr, c), lambda i: (0, i, 0))],
        out_specs=pl.BlockSpec((tr, c), lambda i: (i, 0)),
        out_shape=jax.ShapeDtypeStruct((r, c), F32),
        compiler_params=_cparams("parallel"),
    )(a)


def add_my_half(mine, got, c):
    nchip, r2, cols = mine.shape
    nt = r2 // 2 // SUM_ROWS

    def body(c_ref, a_ref, b_ref, o_ref):
        o_ref[...] = (a_ref[...] + b_ref[...]).astype(BF16)

    blk = pl.BlockSpec((1, SUM_ROWS, cols), lambda j, i, c_ref: (j, i, 0))
    return pl.pallas_call(
        body, name="add_my_half",
        grid_spec=pltpu.PrefetchScalarGridSpec(
            num_scalar_prefetch=1, grid=(nchip, nt),
            in_specs=[pl.BlockSpec((1, SUM_ROWS, cols), lambda j, i, c_ref: (j, c_ref[0] * nt + i, 0)), blk],
            out_specs=blk),
        out_shape=jax.ShapeDtypeStruct((nchip, r2 // 2, cols), BF16),
        compiler_params=_cparams("parallel", "parallel"),
    )(jnp.reshape(c, (1,)).astype(jnp.int32), mine, got)


PACK_COLS = 1152
PACK_ROWS = 2592
ROW_TILE = 16


def _pack(parts):
    blocks = []
    for p in parts:
        p2 = p.reshape(-1, p.shape[-1])
        blocks.append(jnp.pad(p2, ((0, -p2.shape[0] % ROW_TILE), (0, PACK_COLS - p2.shape[1]))))
    rows = sum(b.shape[0] for b in blocks)
    blocks.append(jnp.zeros((PACK_ROWS - rows, PACK_COLS), blocks[0].dtype))
    return jnp.concatenate(blocks)


def _unpack(buf, shapes):
    out, at = [], 0
    for s in shapes:
        rows = math.prod(s[:-1])
        out.append(buf[at:at + rows, :s[-1]].reshape(s))
        at += rows + (-rows % ROW_TILE)
    return out


def _pack_small(g):
    blk = jnp.zeros((ROW_TILE, PACK_COLS), F32)
    for i, k in enumerate(SMALL):
        blk = blk.at[2 * i:2 * i + 2, :g[k].shape[1]].set(g[k])
    return blk


def _unpack_small(blk, shapes):
    return [blk[2 * i:2 * i + 2, :s[1]] for i, s in enumerate(shapes)]


def _layer_fwd(x, p, nseq, tabs, target=None):
    proj, hdn = inproj_fwd(x, p["norm_w"][None], p["w_in"])
    ya = conv_fwd(proj, p["conv_w"], nseq)
    oa, states, inverses = gdn_fwd(ya, proj, p["a_log"].reshape(N_HEADS_A, 1, 1), p["dt_bias"].reshape(N_HEADS_A, 1, 1),
                         p["gdn_norm_w"][None], nseq)
    ob, carries = sb_fwd(proj, nseq)
    oc = dil_fwd(proj, jnp.tile(p["q_norm_w"], 2)[None], jnp.tile(p["k_norm_w"], 2)[None], *tabs, nseq)
    if target is None:
        y, mixed = outproj_fwd(x, oa, ob, oc, proj, p["w_out"])
    else:
        dy, mixed, parts = outproj_loss(x, oa, ob, oc, proj, p["w_out"], target)
        y = (dy, parts)
    return y, dict(x=x, hdn=hdn, proj=proj, ya=ya, states=states, inverses=inverses, carries=carries, oa=oa, ob=ob, oc=oc, mixed=mixed)


def _layer_bwd(dy, p, res, nseq, tabs):
    proj = res["proj"]
    g = {}
    doa, dob, doc, dza, dzb, dzc, g["w_out"] = outproj_bwd(dy, res["oa"], res["ob"], res["oc"], proj, p["w_out"], res["mixed"])
    dqc, dkc, dvc, dqw, dkw = dil_bwd(proj, jnp.tile(p["q_norm_w"], 2)[None], jnp.tile(p["k_norm_w"], 2)[None], *tabs, doc,
                                      nseq)
    g["q_norm_w"], g["k_norm_w"] = dqw.reshape(-1, HEAD_DIM).sum(0), dkw.reshape(-1, HEAD_DIM).sum(0)
    dqb, dkb, dvb = sb_bwd(proj, res["carries"], dob, nseq)
    dya, dba, dalog, ddtb, dnw = gdn_bwd(res["ya"], proj, p["a_log"].reshape(N_HEADS_A, 1, 1),
                                         p["dt_bias"].reshape(N_HEADS_A, 1, 1), p["gdn_norm_w"][None], res["states"], res["inverses"], doa,
                                         nseq)
    g["a_log"], g["dt_bias"], g["gdn_norm_w"] = dalog.sum(0).reshape(-1), ddtb.sum(0).reshape(-1), dnw.sum((0, 1))
    dqkv, dcw = conv_bwd(proj, p["conv_w"], dya, nseq)
    g["conv_w"] = dcw.sum(0)
    slabs = [dqkv, dza, dqc, dkc, dvc, dzc, dqb, dkb, dvb, dzb, dba]
    hdn = res["hdn"]
    g["w_in"] = jnp.concatenate(mat_tn(hdn, slabs[:6]) + mat_tn(hdn, slabs[6:]), axis=1)
    dx, dnw_tiles = inproj_bwd(slabs, p["w_in"], res["x"], p["norm_w"][None], dy)
    g["norm_w"] = dnw_tiles.sum((0, 1))
    return dx, g


SMALL = ("norm_w", "a_log", "dt_bias", "gdn_norm_w", "q_norm_w", "k_norm_w")


def _local_step(x, target, full):
    nseq, t, d = x.shape
    tabs = _rope_tables(t)
    h = x.reshape(nseq * t, d)
    saved = []
    for l in range(DEPTH):
        p = {k: v[l] for k, v in full.items()}
        h, res = _layer_fwd(h, p, nseq, tabs, target.reshape(nseq * t, d) if l == DEPTH - 1 else None)
        saved.append((p, res))
    dy, parts = h
    loss = parts[:, 0, 0].sum()
    grads = [None] * DEPTH
    for l in reversed(range(DEPTH)):
        p, res = saved[l]
        dy, grads[l] = _layer_bwd(dy, p, res, nseq, tabs)
    return loss, dy.reshape(nseq, t, d), {k: jnp.stack([g[k] for g in grads]) for k in grads[0]}


def _pad_cols(w):
    b0 = ORIG_A + ORIG_BA
    c0 = b0 + ORIG_B
    zeros = jnp.zeros(w.shape[:-1] + (BLOCK - ORIG_BA,), w.dtype)
    return jnp.concatenate([w[..., :ORIG_A], w[..., c0:], w[..., b0:c0], w[..., ORIG_A:b0], zeros], axis=-1)


def _unpad_cols(w):
    return jnp.concatenate([w[..., :COL_C], w[..., COL_BA:COL_BA + ORIG_BA], w[..., COL_B:COL_BA], w[..., COL_C:COL_B]],
                           axis=-1)


def kernel(x, norm_w, w_in, conv_w, a_log, dt_bias, gdn_norm_w, q_norm_w, k_norm_w, w_out, loss_target, m_norm_w, m_w_in, m_conv_w, m_a_log, m_dt_bias, m_gdn_norm_w, m_q_norm_w, m_k_norm_w, m_w_out, v_norm_w, v_w_in, v_conv_w, v_a_log, v_dt_bias, v_gdn_norm_w, v_q_norm_w, v_k_norm_w, v_w_out):
    weights = dict(norm_w=norm_w, w_in=w_in, conv_w=conv_w, a_log=a_log, dt_bias=dt_bias, gdn_norm_w=gdn_norm_w,
                   q_norm_w=q_norm_w, k_norm_w=k_norm_w, w_out=w_out)
    moms = dict(norm_w=m_norm_w, w_in=m_w_in, conv_w=m_conv_w, a_log=m_a_log, dt_bias=m_dt_bias,
                gdn_norm_w=m_gdn_norm_w, q_norm_w=m_q_norm_w, k_norm_w=m_k_norm_w, w_out=m_w_out)
    vars_ = dict(norm_w=v_norm_w, w_in=v_w_in, conv_w=v_conv_w, a_log=v_a_log, dt_bias=v_dt_bias,
                 gdn_norm_w=v_gdn_norm_w, q_norm_w=v_q_norm_w, k_norm_w=v_k_norm_w, w_out=v_w_out)
    names = list(weights)
    sharded = ("w_in", "w_out", "conv_w")
    shard_shapes = [weights[k].shape for k in sharded]

    c = lax.axis_index("c")
    half = PACK_ROWS // 2
    conv_bits = lax.bitcast_convert_type(conv_w, BF16).reshape(conv_w.shape[:2] + (2 * conv_w.shape[2],))
    shard = _pack([w_in.astype(BF16), w_out.astype(BF16), conv_bits])
    mine = plane_exchange(lax.dynamic_slice_in_dim(shard, c * half, half, axis=0), all_to_all=False)
    other = sibling_swap(mine)
    got = jnp.concatenate([jnp.where(c == 0, mine, other), jnp.where(c == 0, other, mine)], axis=1)
    per_chip = [_unpack(got[i], shard_shapes[:2] + [conv_bits.shape]) for i in range(N_CHIPS)]
    full = {k: weights[k] for k in SMALL}
    full["w_in"] = _pad_cols(jnp.concatenate([pc[0] for pc in per_chip], axis=2))
    full["w_out"] = jnp.concatenate([pc[1] for pc in per_chip], axis=1)
    full["conv_w"] = jnp.concatenate(
        [lax.bitcast_convert_type(pc[2].reshape(conv_w.shape + (2,)), F32) for pc in per_chip], axis=2)

    loss, grad_x, g = _local_step(x, loss_target, full)

    gw_in = _unpad_cols(g["w_in"])
    cols, rows = w_in.shape[2], w_out.shape[1]
    small = _pack_small(g)
    send = jnp.stack([_pack([gw_in[:, :, i * cols:(i + 1) * cols], g["w_out"][:, i * rows:(i + 1) * rows],
                             g["conv_w"][:, :, i * conv_w.shape[2]:(i + 1) * conv_w.shape[2]], small])
                      for i in range(N_CHIPS)])
    chip_sum = add_my_half(send, sibling_swap(send, other_half=True), c)
    mine = sum4(plane_exchange(chip_sum, all_to_all=True))
    other = sibling_swap(mine)
    total = jnp.concatenate([jnp.where(c == 0, mine, other), jnp.where(c == 0, other, mine)])
    reduced = _unpack(total, shard_shapes + [(ROW_TILE, PACK_COLS)])
    grads = dict(zip(sharded, reduced[:3]))
    grads.update(zip(SMALL, _unpack_small(reduced[3], [weights[k].shape for k in SMALL])))
    loss = lax.psum(loss, ("x", "y", "c"))

    def two_d(a):
        return a.reshape(-1, a.shape[-1])

    delta, new_m, new_v = {}, {}, {}
    for k in names:
        d_, m_, v_ = adamw(two_d(weights[k]), two_d(grads[k]), two_d(moms[k]), two_d(vars_[k]))
        delta[k], new_m[k], new_v[k] = (a.reshape(weights[k].shape) for a in (d_, m_, v_))
    return (loss, grad_x, *[grads[k] for k in names], *[delta[k] for k in names],
            *[new_m[k] for k in names], *[new_v[k] for k in names])
```
